```python
import jax, jax.numpy as jnp
from jax import lax
import numpy as np

D_MODEL = 1024
BATCH = 8
SEQ = 2048
DEPTH = 2

GRID_W = 64
N_MIXERS = 2
HEAD_DIM = 64
NA_HEADS = 16
NA_WIN_H = 8
NA_WIN_W = 16
NA_QCOL_BLOCK = 16
NA_KCOL_SPAN = NA_QCOL_BLOCK + NA_WIN_W
SW_Q_HEADS = 16
SW_KV_HEADS = 4
SW_GROUP = SW_Q_HEADS // SW_KV_HEADS
SW_WINDOW = 128
SW_BLOCK = 128
T5_BUCKETS = 32
T5_MAX_DIST = 128
N_EXPERTS = 16
EXPERT_FF = 2048
EC_CAPACITY = 2
RMS_EPS = 1e-6
NEG = -1e30

kernel_name = "hybrid_natten_swa_ec_moe_encoder"


def rms_norm(x, g):
    x32 = x.astype(jnp.float32)
    y = x32 * lax.rsqrt(jnp.mean(x32 * x32, axis=-1, keepdims=True) + RMS_EPS)
    return (y * g.astype(jnp.float32)).astype(x.dtype)


def neighbourhood_attention(h, w_qkv, w_o, rpb):
    B, S, _ = h.shape
    rows = S // GRID_W
    kh = min(NA_WIN_H, rows)
    ncb = GRID_W // NA_QCOL_BLOCK
    q, k, v = jnp.split(h @ w_qkv, 3, axis=-1)
    grid = lambda t: t.reshape(B, rows, GRID_W, NA_HEADS, HEAD_DIM)
    q, k, v = grid(q) * (HEAD_DIM ** -0.5), grid(k), grid(v)
    qcol = np.arange(GRID_W).reshape(ncb, NA_QCOL_BLOCK)
    cstart = np.clip(qcol - NA_WIN_W // 2, 0, GRID_W - NA_WIN_W)
    kstart = np.clip(np.arange(ncb) * NA_QCOL_BLOCK - NA_WIN_W // 2, 0, GRID_W - NA_KCOL_SPAN)
    kcol = kstart[:, None] + np.arange(NA_KCOL_SPAN)
    col_ok = (kcol[:, None, :] >= cstart[:, :, None]) & (kcol[:, None, :] < cstart[:, :, None] + NA_WIN_W)
    dc_idx = np.clip(kcol[:, None, :] - qcol[:, :, None] + NA_WIN_W - 1, 0, 2 * NA_WIN_W - 2)
    mask = col_ok[:, :, None, :]

    def row_step(r):
        rs = jnp.clip(r - kh // 2, 0, rows - kh)
        q_b = lax.dynamic_index_in_dim(q, r, axis=1, keepdims=False).reshape(
            B, ncb, NA_QCOL_BLOCK, NA_HEADS, HEAD_DIM)
        k_b = lax.dynamic_slice_in_dim(k, rs, kh, axis=1)[:, :, kcol]
        v_b = lax.dynamic_slice_in_dim(v, rs, kh, axis=1)[:, :, kcol]
        s = jnp.einsum('bnqhd,binchd->bhnqic', q_b, k_b).astype(jnp.float32)
        dr_idx = rs + jnp.arange(kh) - r + NA_WIN_H - 1
        bias = rpb[:, dr_idx][:, :, dc_idx].astype(jnp.float32)
        bias = jnp.transpose(bias, (0, 2, 3, 1, 4))
        s = jnp.where(mask, s + bias[None], NEG)
        p = jax.nn.softmax(s.reshape(s.shape[:4] + (-1,)), axis=-1).reshape(s.shape)
        o = jnp.einsum('bhnqic,binchd->bnqhd', p.astype(v.dtype), v_b)
        return o.reshape(B, GRID_W, NA_HEADS * HEAD_DIM)

    out = lax.map(row_step, jnp.arange(rows))
    return jnp.moveaxis(out, 0, 1).reshape(B, S, NA_HEADS * HEAD_DIM) @ w_o


def t5_buckets(rel):
    half = T5_BUCKETS // 2
    max_exact = half // 2
    n = np.abs(rel)
    large = max_exact + (np.log(np.maximum(n, 1) / max_exact)
                         / np.log(T5_MAX_DIST / max_exact) * (half - max_exact)).astype(np.int32)
    large = np.minimum(large, half - 1)
    return (rel > 0).astype(np.int32) * half + np.where(n < max_exact, n, large)


def sliding_window_gqa(h, w_qkv, w_o, sinks, t5_table):
    B, S, _ = h.shape
    nb = S // SW_BLOCK
    qkv = h @ w_qkv
    nq = SW_Q_HEADS * HEAD_DIM
    nkv = SW_KV_HEADS * HEAD_DIM
    q = qkv[..., :nq].reshape(B, nb, SW_BLOCK, SW_KV_HEADS, SW_GROUP, HEAD_DIM) * (HEAD_DIM ** -0.5)
    k = qkv[..., nq:nq + nkv].reshape(B, S, SW_KV_HEADS, HEAD_DIM)
    v = qkv[..., nq + nkv:].reshape(B, S, SW_KV_HEADS, HEAD_DIM)

    def band(t):
        tp = jnp.pad(t, ((0, 0), (SW_BLOCK, SW_BLOCK), (0, 0), (0, 0))).reshape(
            B, nb + 2, SW_BLOCK, SW_KV_HEADS, HEAD_DIM)
        return jnp.concatenate([tp[:, :-2], tp[:, 1:-1], tp[:, 2:]], axis=2)

    kb, vb = band(k), band(v)
    rel = (np.arange(3 * SW_BLOCK)[None, :] - SW_BLOCK) - np.arange(SW_BLOCK)[:, None]
    kpos = np.arange(nb)[:, None] * SW_BLOCK - SW_BLOCK + np.arange(3 * SW_BLOCK)[None, :]
    mask = (np.abs(rel) <= SW_WINDOW)[None] & ((kpos >= 0) & (kpos < S))[:, None, :]
    bias = jnp.transpose(t5_table[t5_buckets(rel)].astype(jnp.float32), (2, 0, 1))
    bias = bias.reshape(SW_KV_HEADS, SW_GROUP, 1, SW_BLOCK, 3 * SW_BLOCK)
    s = jnp.einsum('bnqkgd,bnjkd->bkgnqj', q, kb).astype(jnp.float32)
    s = jnp.where(mask, s + bias, NEG)
    sink = sinks.astype(jnp.float32).reshape(1, SW_KV_HEADS, SW_GROUP, 1, 1, 1)
    m = jnp.maximum(jnp.max(s, axis=-1, keepdims=True), sink)
    p = jnp.exp(s - m)
    p = p / (jnp.sum(p, axis=-1, keepdims=True) + jnp.exp(sink - m))
    o = jnp.einsum('bkgnqj,bnjkd->bnqkgd', p.astype(v.dtype), vb)
    return o.reshape(B, S, nq) @ w_o


def expert_choice_ffn(h, w_router, w_gate, w_up, w_down):
    B, S, D = h.shape
    cap = EC_CAPACITY * S // N_EXPERTS
    aff = jax.nn.softmax((h @ w_router).astype(jnp.float32), axis=-1)
    g, idx = lax.top_k(jnp.swapaxes(aff, 1, 2), cap)
    xg = jax.vmap(lambda hb, ib: hb[ib])(h, idx)
    a = jnp.einsum('becd,edf->becf', xg, w_gate)
    u = jnp.einsum('becd,edf->becf', xg, w_up)
    y = jnp.einsum('becf,efd->becd', jax.nn.silu(a) * u, w_down) * g[..., None].astype(h.dtype)
    return jax.vmap(lambda yb, ib: jnp.zeros((S, D), h.dtype).at[ib.reshape(-1)].add(
        yb.reshape(-1, D)))(y, idx)


def setup_inputs(seed: int = 0) -> dict:
    key = jax.random.key(seed)
    ks = jax.random.split(key, 17)
    D = D_MODEL
    n_a = (DEPTH + N_MIXERS - 1) // N_MIXERS
    n_b = DEPTH // N_MIXERS
    nrm = lambda k, shape, s: jax.random.normal(k, shape, jnp.float32) * s
    sw_width = (SW_Q_HEADS + 2 * SW_KV_HEADS) * HEAD_DIM
    return {
        "x": nrm(ks[0], (BATCH, SEQ, D), 1.0),
        "c": nrm(ks[1], (BATCH, D), 1.0),
        "ada_w": nrm(ks[2], (DEPTH, D, 6 * D), 0.5 * D ** -0.5),
        "ada_b": nrm(ks[3], (DEPTH, 6 * D), 0.01),
        "norm_g": 1.0 + nrm(ks[4], (DEPTH, 2, D), 0.01),
        "na_w_qkv": nrm(ks[5], (n_a, D, 3 * NA_HEADS * HEAD_DIM), D ** -0.5),
        "na_w_o": nrm(ks[6], (n_a, NA_HEADS * HEAD_DIM, D), (NA_HEADS * HEAD_DIM) ** -0.5),
        "na_rpb": nrm(ks[7], (n_a, NA_HEADS, 2 * NA_WIN_H - 1, 2 * NA_WIN_W - 1), 0.1),
        "sw_w_qkv": nrm(ks[8], (n_b, D, sw_width), D ** -0.5),
        "sw_w_o": nrm(ks[9], (n_b, SW_Q_HEADS * HEAD_DIM, D), (SW_Q_HEADS * HEAD_DIM) ** -0.5),
        "sw_sinks": nrm(ks[10], (n_b, SW_Q_HEADS), 0.5),
        "t5_bias": nrm(ks[11], (T5_BUCKETS, SW_Q_HEADS), 0.1),
        "moe_w_router": nrm(ks[12], (DEPTH, D, N_EXPERTS), D ** -0.5),
        "moe_w_gate": nrm(ks[13], (DEPTH, N_EXPERTS, D, EXPERT_FF), D ** -0.5),
        "moe_w_up": nrm(ks[14], (DEPTH, N_EXPERTS, D, EXPERT_FF), D ** -0.5),
        "moe_w_down": nrm(ks[15], (DEPTH, N_EXPERTS, EXPERT_FF, D), EXPERT_FF ** -0.5),
        "final_g": 1.0 + nrm(ks[16], (D,), 0.01),
    }


def reference(x, c, ada_w, ada_b, norm_g, na_w_qkv, na_w_o, na_rpb, sw_w_qkv, sw_w_o,
              sw_sinks, t5_bias, moe_w_router, moe_w_gate, moe_w_up, moe_w_down, final_g):
    c_act = jax.nn.silu(c)
    for l in range(DEPTH):
        mod = c_act @ ada_w[l] + ada_b[l]
        sh1, sc1, g1, sh2, sc2, g2 = jnp.split(mod, 6, axis=-1)
        h = rms_norm(x, norm_g[l, 0]) * (1.0 + sc1[:, None]) + sh1[:, None]
        j = l // N_MIXERS
        if l % N_MIXERS == 0:
            y = neighbourhood_attention(h, na_w_qkv[j], na_w_o[j], na_rpb[j])
        else:
            y = sliding_window_gqa(h, sw_w_qkv[j], sw_w_o[j], sw_sinks[j], t5_bias)
        x = x + g1[:, None] * y
        h = rms_norm(x, norm_g[l, 1]) * (1.0 + sc2[:, None]) + sh2[:, None]
        x = x + g2[:, None] * expert_choice_ffn(h, moe_w_router[l], moe_w_gate[l],
                                                moe_w_up[l], moe_w_down[l])
    return rms_norm(x, final_g)
```

```python
import functools

import numpy as np
import jax
import jax.numpy as jnp
from jax import lax
from jax.experimental import pallas as pl
from jax.experimental.pallas import tpu as pltpu

D_MODEL = 1024
BATCH = 8
SEQ = 2048
DEPTH = 2
GRID_W = 64
ROWS = SEQ // GRID_W
N_MIXERS = 2
HEAD_DIM = 64
NA_HEADS = 16
NA_WIN_H = 8
NA_WIN_W = 16
SW_Q_HEADS = 16
SW_KV_HEADS = 4
SW_GROUP = SW_Q_HEADS // SW_KV_HEADS
SW_WINDOW = 128
SW_BLOCK = 128
SW_NB = SEQ // SW_BLOCK
SW_SPAN = 3 * SW_BLOCK
T5_BUCKETS = 32
T5_MAX_DIST = 128
N_EXPERTS = 16
EXPERT_FF = 2048
EC_CAPACITY = 2
CAP = EC_CAPACITY * SEQ // N_EXPERTS
RMS_EPS = 1e-6
NEG = -1e30

LANES = 128
MIB = 1024 * 1024
F32 = jnp.float32
BF16 = jnp.bfloat16

ROW_TILE = 512
FF_TILE = 512
EXPERT_ROWS = 1024
COMBINE_COLS = 256
PREFIX_CHUNK = 256


def _params(semantics, vmem_mib):
    return pltpu.CompilerParams(dimension_semantics=semantics, vmem_limit_bytes=vmem_mib * MIB)


def _norm_mod(x, g, sc, sh):
    y = x * lax.rsqrt(jnp.mean(x * x, axis=-1, keepdims=True) + RMS_EPS)
    return (y * g) * (1.0 + sc) + sh


def _softmax0(z):
    z = z - jnp.max(z, axis=0, keepdims=True)
    p = jnp.exp(z)
    return p / jnp.sum(p, axis=0, keepdims=True)


def _half_masks(rows):
    lane = lax.broadcasted_iota(jnp.int32, (rows, LANES), 1)
    lo = jnp.where(lane < HEAD_DIM, 1.0, 0.0).astype(BF16)
    hi = jnp.where(lane < HEAD_DIM, 0.0, 1.0).astype(BF16)
    return lo, hi


def _ada_kernel(c_ref, w_ref, b_ref, o_ref):
    c = c_ref[...]
    act = (c * jax.nn.sigmoid(c)).astype(BF16)
    o_ref[0] = jnp.dot(act, w_ref[0].astype(BF16), preferred_element_type=F32) + b_ref[0]


def _ada(c, ada_w, ada_b):
    out = pl.pallas_call(
        _ada_kernel,
        grid=(DEPTH, 6),
        in_specs=[
            pl.BlockSpec((BATCH, D_MODEL), lambda l, k: (0, 0)),
            pl.BlockSpec((1, D_MODEL, D_MODEL), lambda l, k: (l, 0, k)),
            pl.BlockSpec((1, 1, D_MODEL), lambda l, k: (l * 6 + k, 0, 0)),
        ],
        out_specs=pl.BlockSpec((1, BATCH, D_MODEL), lambda l, k: (l * 6 + k, 0, 0)),
        out_shape=jax.ShapeDtypeStruct((DEPTH * 6, BATCH, D_MODEL), F32),
        compiler_params=_params(("arbitrary", "arbitrary"), 32),
        name="ada_mod",
    )(c, ada_w, ada_b.reshape(DEPTH * 6, 1, D_MODEL))
    return out.reshape(DEPTH * 6 * BATCH, 1, D_MODEL)


def _mod_spec(layer, chunk):
    tiles_per_seq = SEQ // ROW_TILE
    return pl.BlockSpec((1, 1, D_MODEL),
                        lambda i: ((layer * 6 + chunk) * BATCH + i // tiles_per_seq, 0, 0))


def _qkv_kernel(x_ref, g_ref, sc_ref, sh_ref, w_ref, o_ref):
    h = _norm_mod(x_ref[...], g_ref[0], sc_ref[0], sh_ref[0])
    o_ref[...] = jnp.dot(h.astype(BF16), w_ref[...], preferred_element_type=F32).astype(BF16)


def _qkv(x, norm_g3, mod, w, layer):
    n = w.shape[1]
    return pl.pallas_call(
        _qkv_kernel,
        grid=(BATCH * SEQ // ROW_TILE,),
        in_specs=[
            pl.BlockSpec((ROW_TILE, D_MODEL), lambda i: (i, 0)),
            pl.BlockSpec((1, 1, D_MODEL), lambda i: (layer * 2, 0, 0)),
            _mod_spec(layer, 1),
            _mod_spec(layer, 0),
            pl.BlockSpec((D_MODEL, n), lambda i: (0, 0)),
        ],
        out_specs=pl.BlockSpec((ROW_TILE, n), lambda i: (i, 0)),
        out_shape=jax.ShapeDtypeStruct((BATCH * SEQ, n), BF16),
        compiler_params=_params(("arbitrary",), 48),
        name="norm_qkv",
    )(x, norm_g3, mod, mod, w)


def _na_bias_table(rpb):
    d = np.arange(NA_WIN_H)[:, None]
    i = np.arange(NA_WIN_H)[None, :]
    dr = i - d + NA_WIN_H - 1
    col = np.arange(GRID_W)[:, None]
    kc = np.arange(GRID_W)[None, :]
    cstart = np.clip(col - NA_WIN_W // 2, 0, GRID_W - NA_WIN_W)
    ok = (kc >= cstart) & (kc < cstart + NA_WIN_W)
    dc = np.clip(kc - col + NA_WIN_W - 1, 0, 2 * NA_WIN_W - 2)
    t = rpb[:, dr][:, :, :, dc].astype(F32)
    t = jnp.where(ok[None, None, None], t, NEG)
    t = jnp.transpose(t, (1, 0, 3, 2, 4))
    return t.reshape(NA_WIN_H, NA_HEADS // 2, 2 * GRID_W, NA_WIN_H * GRID_W)


def _na_kernel(q_ref, k_ref, v_ref, bias_ref, o_ref):
    lo, hi = _half_masks(GRID_W)
    lo_f = lax.broadcasted_iota(jnp.int32, (GRID_W, LANES), 1) < HEAD_DIM
    span = NA_WIN_H * GRID_W

    def row_step(r, carry):
        rs = jnp.clip(r - NA_WIN_H // 2, 0, ROWS - NA_WIN_H)
        q = q_ref[0, pl.ds(pl.multiple_of(r * GRID_W, GRID_W), GRID_W), :]
        lhs = jnp.concatenate([q * lo, q * hi], axis=0)
        k0 = pl.multiple_of(rs * GRID_W, GRID_W)
        kw = k_ref[0, pl.ds(k0, span), :]
        vw = v_ref[0, pl.ds(k0, span), :]
        s = lax.dot_general(lhs, kw, (((1,), (1,)), ((), ())), preferred_element_type=F32)
        s = s + bias_ref[r - rs, 0]
        m = jnp.max(s, axis=-1, keepdims=True)
        p = jnp.exp(s - m)
        denom = jnp.sum(p, axis=-1, keepdims=True)
        o = jnp.dot(p.astype(BF16), vw, preferred_element_type=F32) / denom
        out = jnp.where(lo_f, o[:GRID_W], o[GRID_W:])
        o_ref[0, pl.ds(pl.multiple_of(r * GRID_W, GRID_W), GRID_W), :] = out.astype(BF16)
        return carry

    lax.fori_loop(0, ROWS, row_step, 0)


def _na_attention(qkv, bias):
    pairs = NA_HEADS // 2
    return pl.pallas_call(
        _na_kernel,
        grid=(pairs, BATCH),
        in_specs=[
            pl.BlockSpec((1, SEQ, LANES), lambda p, b: (b, 0, p)),
            pl.BlockSpec((1, SEQ, LANES), lambda p, b: (b, 0, pairs + p)),
            pl.BlockSpec((1, SEQ, LANES), lambda p, b: (b, 0, 2 * pairs + p)),
            pl.BlockSpec((NA_WIN_H, 1, 2 * GRID_W, NA_WIN_H * GRID_W), lambda p, b: (0, p, 0, 0)),
        ],
        out_specs=pl.BlockSpec((1, SEQ, LANES), lambda p, b: (b, 0, p)),
        out_shape=jax.ShapeDtypeStruct((BATCH, SEQ, NA_HEADS * HEAD_DIM), BF16),
        compiler_params=_params(("arbitrary", "arbitrary"), 32),
        name="na_attention",
    )(qkv, qkv, qkv, bias)


def _t5_buckets(rel):
    half = T5_BUCKETS // 2
    max_exact = half // 2
    n = np.abs(rel)
    large = max_exact + (np.log(np.maximum(n, 1) / max_exact)
                         / np.log(T5_MAX_DIST / max_exact) * (half - max_exact)).astype(np.int32)
    large = np.minimum(large, half - 1)
    return (rel > 0).astype(np.int32) * half + np.where(n < max_exact, n, large)


def _sw_window_start(n):
    return jnp.clip(n - 1, 0, SW_NB - 3) if not isinstance(n, int) else min(max(n - 1, 0), SW_NB - 3)


def _sw_bias_table(t5_table):
    tables = []
    for n in (0, 1, SW_NB - 1):
        start = _sw_window_start(n) * SW_BLOCK
        kpos = start + np.arange(SW_SPAN)[None, :]
        qpos = n * SW_BLOCK + np.arange(SW_BLOCK)[:, None]
        rel = kpos - qpos
        ok = np.abs(rel) <= SW_WINDOW
        b = jnp.transpose(t5_table[_t5_buckets(rel)].astype(F32), (2, 0, 1))
        tables.append(jnp.where(ok[None], b, NEG))
    return jnp.stack(tables)


def _sw_kernel(sink_ref, q_ref, k_ref, v_ref, bias_ref, o_ref):
    j = pl.program_id(1)
    n = pl.program_id(2)
    lo, hi = _half_masks(SW_BLOCK)
    lo_f = lax.broadcasted_iota(jnp.int32, (SW_BLOCK, LANES), 1) < HEAD_DIM
    start = pl.multiple_of(_sw_window_start(n) * SW_BLOCK, SW_BLOCK)
    q = q_ref[0]
    t0, t1 = q[:, :LANES], q[:, LANES:]
    lhs = jnp.concatenate([t0 * lo, t0 * hi, t1 * lo, t1 * hi], axis=0)
    kw = k_ref[0, pl.ds(start, SW_SPAN), :]
    vw = v_ref[0, pl.ds(start, SW_SPAN), :]
    s = lax.dot_general(lhs, kw, (((1,), (1,)), ((), ())), preferred_element_type=F32)
    s = s + bias_ref[0].reshape(SW_GROUP * SW_BLOCK, SW_SPAN)
    sink = jnp.concatenate(
        [jnp.full((SW_BLOCK, 1), sink_ref[j * SW_GROUP + g], F32) for g in range(SW_GROUP)], axis=0)
    m = jnp.maximum(jnp.max(s, axis=-1, keepdims=True), sink)
    p = jnp.exp(s - m)
    denom = jnp.sum(p, axis=-1, keepdims=True) + jnp.exp(sink - m)
    o = jnp.dot(p.astype(BF16), vw, preferred_element_type=F32) / denom
    b = SW_BLOCK
    out = jnp.concatenate([jnp.where(lo_f, o[0:b], o[b:2 * b]),
                           jnp.where(lo_f, o[2 * b:3 * b], o[3 * b:4 * b])], axis=1)
    o_ref[0] = out.astype(BF16)


def _sw_attention(qkv, bias, sinks):
    qw = SW_GROUP * HEAD_DIM
    k_off = SW_Q_HEADS * HEAD_DIM // LANES
    v_off = k_off + SW_KV_HEADS

    def bias_idx(b, j, n):
        return (jnp.where(n == 0, 0, jnp.where(n == SW_NB - 1, 2, 1)), j, 0, 0)

    return pl.pallas_call(
        _sw_kernel,
        grid=(BATCH, SW_KV_HEADS, SW_NB),
        in_specs=[
            pl.BlockSpec(memory_space=pltpu.SMEM),
            pl.BlockSpec((1, SW_BLOCK, qw), lambda b, j, n: (b, n, j)),
            pl.BlockSpec((1, SEQ, LANES), lambda b, j, n: (b, 0, k_off + j)),
            pl.BlockSpec((1, SEQ, LANES), lambda b, j, n: (b, 0, v_off + j)),
            pl.BlockSpec((1, SW_GROUP, SW_BLOCK, SW_SPAN), bias_idx),
        ],
        out_specs=pl.BlockSpec((1, SW_BLOCK, qw), lambda b, j, n: (b, n, j)),
        out_shape=jax.ShapeDtypeStruct((BATCH, SEQ, SW_Q_HEADS * HEAD_DIM), BF16),
        compiler_params=_params(("arbitrary", "arbitrary", "arbitrary"), 32),
        name="sw_attention",
    )(sinks, qkv, qkv, qkv, bias)


def _post_attn_kernel(o_ref, wo_ref, x_ref, gate_ref, g_ref, sc_ref, sh_ref, wr_ref,
                      xo_ref, h_ref, aff_ref):
    y = jnp.dot(o_ref[...], wo_ref[...], preferred_element_type=F32)
    xn = x_ref[...] + gate_ref[0] * y
    xo_ref[...] = xn
    hb = _norm_mod(xn, g_ref[0], sc_ref[0], sh_ref[0]).astype(BF16)
    h_ref[...] = hb
    logits = lax.dot_general(wr_ref[...], hb, (((1,), (1,)), ((), ())), preferred_element_type=F32)
    aff_ref[...] = _softmax0(logits)


def _post_attn(o, w_o, x, norm_g3, mod, w_router_t, layer):
    rows = BATCH * SEQ
    return pl.pallas_call(
        _post_attn_kernel,
        grid=(rows // ROW_TILE,),
        in_specs=[
            pl.BlockSpec((ROW_TILE, D_MODEL), lambda i: (i, 0)),
            pl.BlockSpec((D_MODEL, D_MODEL), lambda i: (0, 0)),
            pl.BlockSpec((ROW_TILE, D_MODEL), lambda i: (i, 0)),
            _mod_spec(layer, 2),
            pl.BlockSpec((1, 1, D_MODEL), lambda i: (layer * 2 + 1, 0, 0)),
            _mod_spec(layer, 4),
            _mod_spec(layer, 3),
            pl.BlockSpec((N_EXPERTS, D_MODEL), lambda i: (0, 0)),
        ],
        out_specs=[
            pl.BlockSpec((ROW_TILE, D_MODEL), lambda i: (i, 0)),
            pl.BlockSpec((ROW_TILE, D_MODEL), lambda i: (i, 0)),
            pl.BlockSpec((N_EXPERTS, ROW_TILE), lambda i: (0, i)),
        ],
        out_shape=[
            jax.ShapeDtypeStruct((rows, D_MODEL), F32),
            jax.ShapeDtypeStruct((rows, D_MODEL), BF16),
            jax.ShapeDtypeStruct((N_EXPERTS, rows), F32),
        ],
        compiler_params=_params(("arbitrary",), 48),
        name="post_attn",
    )(o, w_o, x, mod, norm_g3, mod, mod, w_router_t)


def _prefix_count(x):
    nchunk = SEQ // PREFIX_CHUNK
    r = lax.broadcasted_iota(jnp.int32, (PREFIX_CHUNK, PREFIX_CHUNK), 0)
    c = lax.broadcasted_iota(jnp.int32, (PREFIX_CHUNK, PREFIX_CHUNK), 1)
    upper = jnp.where(r < c, 1.0, 0.0).astype(BF16)
    chunks = [x[:, k * PREFIX_CHUNK:(k + 1) * PREFIX_CHUNK] for k in range(nchunk)]
    local = jnp.dot(jnp.concatenate(chunks, axis=0).astype(BF16), upper, preferred_element_type=F32)
    out = []
    offset = jnp.zeros((N_EXPERTS, 1), F32)
    for k in range(nchunk):
        out.append(local[k * N_EXPERTS:(k + 1) * N_EXPERTS] + offset)
        offset = offset + jnp.sum(chunks[k], axis=1, keepdims=True)
    return jnp.concatenate(out, axis=1)


def _route_kernel(aff_ref, slot_ref):
    bits = pltpu.bitcast(aff_ref[...], jnp.int32)

    def count(mask):
        return jnp.sum(jnp.where(mask, 1.0, 0.0), axis=1, keepdims=True)

    def search(_, bounds):
        lo, hi = bounds
        mid = lo + ((hi - lo) >> 1)
        ge = count(bits >= mid) >= CAP
        return jnp.where(ge, mid, lo), jnp.where(ge, hi, mid)

    lo0 = jnp.zeros((N_EXPERTS, 1), jnp.int32)
    hi0 = jnp.full((N_EXPERTS, 1), 0x7F800000, jnp.int32)
    tau, _ = lax.fori_loop(0, 31, search, (lo0, hi0))
    gt = bits > tau
    eq = jnp.where(bits == tau, 1.0, 0.0)
    need = CAP - count(gt)
    sel = jnp.where(gt, 1.0, jnp.where(_prefix_count(eq) < need, eq, 0.0))
    pos = _prefix_count(sel).astype(jnp.int32)
    slot_ref[...] = jnp.where(sel > 0.5, pos, -1)


def _route(aff_t):
    return pl.pallas_call(
        _route_kernel,
        grid=(BATCH,),
        in_specs=[pl.BlockSpec((N_EXPERTS, SEQ), lambda b: (0, b))],
        out_specs=pl.BlockSpec((N_EXPERTS, SEQ), lambda b: (0, b)),
        out_shape=jax.ShapeDtypeStruct((N_EXPERTS, BATCH * SEQ), jnp.int32),
        compiler_params=_params(("arbitrary",), 32),
        name="route",
    )(aff_t)


def _gather_kernel(slot_ref, h_ref, xg_ref):
    e = pl.program_id(1)
    row = slot_ref[pl.ds(e, 1), :]
    c = lax.broadcasted_iota(jnp.int32, (CAP, SEQ), 0)
    onehot = jnp.where(row == c, 1.0, 0.0).astype(BF16)
    xg_ref[0, 0] = jnp.dot(onehot, h_ref[0], preferred_element_type=F32).astype(BF16)


def _gather(slot, h):
    return pl.pallas_call(
        _gather_kernel,
        grid=(BATCH, N_EXPERTS),
        in_specs=[
            pl.BlockSpec((N_EXPERTS, SEQ), lambda b, e: (0, b)),
            pl.BlockSpec((1, SEQ, D_MODEL), lambda b, e: (b, 0, 0)),
        ],
        out_specs=pl.BlockSpec((1, 1, CAP, D_MODEL), lambda b, e: (b, e, 0, 0)),
        out_shape=jax.ShapeDtypeStruct((BATCH, N_EXPERTS, CAP, D_MODEL), BF16),
        compiler_params=_params(("arbitrary", "arbitrary"), 32),
        name="moe_gather",
    )(slot, h)


def _expert_kernel(x_ref, wr_ref, wg_ref, wu_ref, wd_ref, y_ref, acc_ref, gate_ref):
    e = pl.program_id(0)
    f = pl.program_id(2)
    x = x_ref[...].reshape(EXPERT_ROWS, D_MODEL)

    @pl.when(f == 0)
    def _():
        logits = jnp.dot(x, wr_ref[...], preferred_element_type=F32)
        z = logits - jnp.max(logits, axis=-1, keepdims=True)
        p = jnp.exp(z)
        aff = p / jnp.sum(p, axis=-1, keepdims=True)
        lane = lax.broadcasted_iota(jnp.int32, aff.shape, 1)
        gate_ref[...] = jnp.sum(jnp.where(lane == e, aff, 0.0), axis=-1, keepdims=True)

    a = jnp.dot(x, wg_ref[0, 0].astype(BF16), preferred_element_type=F32)
    u = jnp.dot(x, wu_ref[0, 0].astype(BF16), preferred_element_type=F32)
    act = ((a * jax.nn.sigmoid(a)) * u).astype(BF16)
    part = jnp.dot(act, wd_ref[0, 0].astype(BF16), preferred_element_type=F32)

    @pl.when(f == 0)
    def _():
        acc_ref[...] = part

    @pl.when(f > 0)
    def _():
        acc_ref[...] += part

    @pl.when(f == EXPERT_FF // FF_TILE - 1)
    def _():
        y_ref[0] = (acc_ref[...] * gate_ref[...]).astype(BF16).reshape(y_ref.shape[1:])


def _experts(xg, w_router, w_gate, w_up, w_down, layer):
    seq_per_step = EXPERT_ROWS // CAP
    return pl.pallas_call(
        _expert_kernel,
        grid=(N_EXPERTS, BATCH // seq_per_step, EXPERT_FF // FF_TILE),
        in_specs=[
            pl.BlockSpec((seq_per_step, 1, CAP, D_MODEL), lambda e, m, f: (m, e, 0, 0)),
            pl.BlockSpec((D_MODEL, N_EXPERTS), lambda e, m, f: (0, 0)),
            pl.BlockSpec((1, 1, D_MODEL, FF_TILE), lambda e, m, f: (layer, e, 0, f)),
            pl.BlockSpec((1, 1, D_MODEL, FF_TILE), lambda e, m, f: (layer, e, 0, f)),
            pl.BlockSpec((1, 1, FF_TILE, D_MODEL), lambda e, m, f: (layer, e, f, 0)),
        ],
        out_specs=pl.BlockSpec((1, seq_per_step, CAP, D_MODEL), lambda e, m, f: (e, m, 0, 0)),
        out_shape=jax.ShapeDtypeStruct((N_EXPERTS, BATCH, CAP, D_MODEL), BF16),
        scratch_shapes=[pltpu.VMEM((EXPERT_ROWS, D_MODEL), F32), pltpu.VMEM((EXPERT_ROWS, 1), F32)],
        compiler_params=_params(("arbitrary", "arbitrary", "arbitrary"), 48),
        name="moe_experts",
    )(xg, w_router, w_gate, w_up, w_down)


def _combine_kernel(slot_ref, y_ref, x_ref, gate_ref, o_ref, onehot_ref):
    @pl.when(pl.program_id(1) == 0)
    def _():
        lane = lax.broadcasted_iota(jnp.int32, (SEQ, CAP), 1)
        slots = slot_ref[0]
        for e in range(N_EXPERTS):
            hit = slots[:, e:e + 1] == lane
            onehot_ref[:, e * CAP:(e + 1) * CAP] = jnp.where(hit, 1.0, 0.0).astype(BF16)

    y = y_ref[...].reshape(N_EXPERTS * CAP, COMBINE_COLS)
    moe = jnp.dot(onehot_ref[...], y, preferred_element_type=F32)
    o_ref[0] = x_ref[0] + gate_ref[0] * moe


def _combine(slot_t, y, x, mod, layer):
    return pl.pallas_call(
        _combine_kernel,
        grid=(BATCH, D_MODEL // COMBINE_COLS),
        in_specs=[
            pl.BlockSpec((1, SEQ, N_EXPERTS), lambda b, n: (b, 0, 0)),
            pl.BlockSpec((N_EXPERTS, 1, CAP, COMBINE_COLS), lambda b, n: (0, b, 0, n)),
            pl.BlockSpec((1, SEQ, COMBINE_COLS), lambda b, n: (b, 0, n)),
            pl.BlockSpec((1, 1, COMBINE_COLS), lambda b, n: ((layer * 6 + 5) * BATCH + b, 0, n)),
        ],
        out_specs=pl.BlockSpec((1, SEQ, COMBINE_COLS), lambda b, n: (b, 0, n)),
        out_shape=jax.ShapeDtypeStruct((BATCH, SEQ, D_MODEL), F32),
        scratch_shapes=[pltpu.VMEM((SEQ, N_EXPERTS * CAP), BF16)],
        compiler_params=_params(("arbitrary", "arbitrary"), 48),
        name="moe_combine",
    )(slot_t, y, x, mod)


def _final_norm_kernel(x_ref, g_ref, o_ref):
    x = x_ref[...]
    y = x * lax.rsqrt(jnp.mean(x * x, axis=-1, keepdims=True) + RMS_EPS)
    o_ref[...] = y * g_ref[...]


def _final_norm(x, g):
    rows = BATCH * SEQ
    return pl.pallas_call(
        _final_norm_kernel,
        grid=(rows // ROW_TILE,),
        in_specs=[pl.BlockSpec((ROW_TILE, D_MODEL), lambda i: (i, 0)),
                  pl.BlockSpec((1, D_MODEL), lambda i: (0, 0))],
        out_specs=pl.BlockSpec((ROW_TILE, D_MODEL), lambda i: (i, 0)),
        out_shape=jax.ShapeDtypeStruct((rows, D_MODEL), F32),
        compiler_params=_params(("arbitrary",), 32),
        name="final_norm",
    )(x, g.reshape(1, D_MODEL))


def _na_qkv_weight(w):
    nq = NA_HEADS * HEAD_DIM
    scale = jnp.concatenate([jnp.full((nq,), HEAD_DIM ** -0.5, F32), jnp.ones((2 * nq,), F32)])
    return (w * scale).astype(BF16)


def _sw_qkv_weight(w):
    nq = SW_Q_HEADS * HEAD_DIM
    nkv = SW_KV_HEADS * HEAD_DIM
    dup = lambda t: jnp.concatenate([t.reshape(D_MODEL, SW_KV_HEADS, 1, HEAD_DIM)] * 2, axis=2).reshape(
        D_MODEL, 2 * nkv)
    wq = w[:, :nq] * HEAD_DIM ** -0.5
    return jnp.concatenate([wq, dup(w[:, nq:nq + nkv]), dup(w[:, nq + nkv:])], axis=1).astype(BF16)


def kernel(x, c, ada_w, ada_b, norm_g, na_w_qkv, na_w_o, na_rpb, sw_w_qkv, sw_w_o, sw_sinks, t5_bias,
           moe_w_router, moe_w_gate, moe_w_up, moe_w_down, final_g):
    mod = _ada(c, ada_w, ada_b)
    norm_g3 = norm_g.reshape(DEPTH * 2, 1, D_MODEL)
    x = x.reshape(BATCH * SEQ, D_MODEL)
    for layer in range(DEPTH):
        j = layer // N_MIXERS
        if layer % N_MIXERS == 0:
            qkv = _qkv(x, norm_g3, mod, _na_qkv_weight(na_w_qkv[j]), layer)
            o = _na_attention(qkv.reshape(BATCH, SEQ, -1), _na_bias_table(na_rpb[j]))
            w_o = na_w_o[j]
        else:
            qkv = _qkv(x, norm_g3, mod, _sw_qkv_weight(sw_w_qkv[j]), layer)
            o = _sw_attention(qkv.reshape(BATCH, SEQ, -1), _sw_bias_table(t5_bias), sw_sinks[j])
            w_o = sw_w_o[j]
        w_router = moe_w_router[layer].astype(BF16)
        x, h, aff_t = _post_attn(o.reshape(BATCH * SEQ, -1), w_o.astype(BF16), x, norm_g3, mod,
                                 w_router.T, layer)
        slot = _route(aff_t)
        xg = _gather(slot, h.reshape(BATCH, SEQ, D_MODEL))
        y = _experts(xg, w_router, moe_w_gate, moe_w_up, moe_w_down, layer)
        slot_t = slot.reshape(N_EXPERTS, BATCH, SEQ).transpose(1, 2, 0)
        x = _combine(slot_t, y, x.reshape(BATCH, SEQ, D_MODEL), mod, layer).reshape(BATCH * SEQ, D_MODEL)
    return _final_norm(x, final_g).reshape(BATCH, SEQ, D_MODEL)
```

```python
import functools

import numpy as np
import jax
import jax.numpy as jnp
from jax import lax
from jax.experimental import pallas as pl
from jax.experimental.pallas import tpu as pltpu

D_MODEL = 1024
BATCH = 8
SEQ = 2048
DEPTH = 2
GRID_W = 64
ROWS = SEQ // GRID_W
N_MIXERS = 2
HEAD_DIM = 64
NA_HEADS = 16
NA_WIN_H = 8
NA_WIN_W = 16
SW_Q_HEADS = 16
SW_KV_HEADS = 4
SW_GROUP = SW_Q_HEADS // SW_KV_HEADS
SW_WINDOW = 128
SW_BLOCK = 128
SW_NB = SEQ // SW_BLOCK
SW_SPAN = 3 * SW_BLOCK
T5_BUCKETS = 32
T5_MAX_DIST = 128
N_EXPERTS = 16
EXPERT_FF = 2048
EC_CAPACITY = 2
CAP = EC_CAPACITY * SEQ // N_EXPERTS
RMS_EPS = 1e-6
NEG = -1e30

LANES = 128
MIB = 1024 * 1024
F32 = jnp.float32
BF16 = jnp.bfloat16

ROW_TILE = 512
FF_TILE = 512
COMBINE_COLS = 256
PREFIX_CHUNK = 256


def _params(semantics, vmem_mib, flags=None):
    return pltpu.CompilerParams(dimension_semantics=semantics, vmem_limit_bytes=vmem_mib * MIB, flags=flags)


def _norm_mod(x, g, sc, sh):
    y = x * lax.rsqrt(jnp.mean(x * x, axis=-1, keepdims=True) + RMS_EPS)
    return (y * g) * (1.0 + sc) + sh


def _softmax0(z):
    z = z - jnp.max(z, axis=0, keepdims=True)
    p = jnp.exp(z)
    return p / jnp.sum(p, axis=0, keepdims=True)


def _half_masks(rows):
    lane = lax.broadcasted_iota(jnp.int32, (rows, LANES), 1)
    lo = jnp.where(lane < HEAD_DIM, 1.0, 0.0).astype(BF16)
    hi = jnp.where(lane < HEAD_DIM, 0.0, 1.0).astype(BF16)
    return lo, hi


def _ada_kernel(c_ref, w_ref, b_ref, o_ref):
    c = c_ref[...]
    act = (c * jax.nn.sigmoid(c)).astype(BF16)
    o_ref[0] = jnp.dot(act, w_ref[0].astype(BF16), preferred_element_type=F32) + b_ref[0]


def _ada(c, ada_w, ada_b):
    out = pl.pallas_call(
        _ada_kernel,
        grid=(DEPTH, 6),
        in_specs=[
            pl.BlockSpec((BATCH, D_MODEL), lambda l, k: (0, 0)),
            pl.BlockSpec((1, D_MODEL, D_MODEL), lambda l, k: (l, 0, k)),
            pl.BlockSpec((1, 1, D_MODEL), lambda l, k: (l * 6 + k, 0, 0)),
        ],
        out_specs=pl.BlockSpec((1, BATCH, D_MODEL), lambda l, k: (l * 6 + k, 0, 0)),
        out_shape=jax.ShapeDtypeStruct((DEPTH * 6, BATCH, D_MODEL), F32),
        compiler_params=_params(("arbitrary", "arbitrary"), 32),
        name="ada_mod",
    )(c, ada_w, ada_b.reshape(DEPTH * 6, 1, D_MODEL))
    return out.reshape(DEPTH * 6 * BATCH, 1, D_MODEL)


def _mod_spec(layer, chunk):
    tiles_per_seq = SEQ // ROW_TILE
    return pl.BlockSpec((1, 1, D_MODEL),
                        lambda i: ((layer * 6 + chunk) * BATCH + i // tiles_per_seq, 0, 0))


def _qkv_kernel(x_ref, g_ref, sc_ref, sh_ref, w_ref, o_ref):
    h = _norm_mod(x_ref[...], g_ref[0], sc_ref[0], sh_ref[0])
    o_ref[...] = jnp.dot(h.astype(BF16), w_ref[...], preferred_element_type=F32).astype(BF16)


def _qkv(x, norm_g3, mod, w, layer):
    n = w.shape[1]
    return pl.pallas_call(
        _qkv_kernel,
        grid=(BATCH * SEQ // ROW_TILE,),
        in_specs=[
            pl.BlockSpec((ROW_TILE, D_MODEL), lambda i: (i, 0)),
            pl.BlockSpec((1, 1, D_MODEL), lambda i: (layer * 2, 0, 0)),
            _mod_spec(layer, 1),
            _mod_spec(layer, 0),
            pl.BlockSpec((D_MODEL, n), lambda i: (0, 0)),
        ],
        out_specs=pl.BlockSpec((ROW_TILE, n), lambda i: (i, 0)),
        out_shape=jax.ShapeDtypeStruct((BATCH * SEQ, n), BF16),
        compiler_params=_params(("arbitrary",), 48),
        name="norm_qkv",
    )(x, norm_g3, mod, mod, w)


NA_BIAS_ROWS = 2 * NA_WIN_H - 1
NA_ROW_UNROLL = 4


def _na_rpb_rows(rpb):
    w = NA_WIN_W - 1
    pad = jnp.zeros(rpb.shape[:2] + (LANES - 2 * w - 1,), F32)
    return jnp.concatenate([rpb[..., w:].astype(F32), pad, rpb[..., :w].astype(F32)], axis=-1)


def _na_build_bias(w_ref, bias_ref):
    lane = lax.broadcasted_iota(jnp.int32, (GRID_W, LANES), 1)
    col = lax.broadcasted_iota(jnp.int32, (GRID_W, LANES), 0)
    kc = jnp.bitwise_and(lane, GRID_W - 1)
    cstart = jnp.clip(col - NA_WIN_W // 2, 0, GRID_W - NA_WIN_W)
    in_window = jnp.where(kc >= cstart, jnp.where(kc < cstart + NA_WIN_W, 1.0, 0.0), 0.0) > 0.5
    low_half = lane < GRID_W

    def toeplitz(head, a, shift):
        row = jnp.broadcast_to(w_ref[head, a:a + 1, :], (GRID_W, LANES))
        return pltpu.roll(row, shift, 1, stride=1, stride_axis=0)

    for head in range(2):
        tiles = [jnp.where(in_window, jnp.where(low_half, toeplitz(head, a, 0), toeplitz(head, a + 1, GRID_W)), NEG)
                 for a in range(NA_BIAS_ROWS - 1)]
        for d in range(NA_WIN_H):
            for m in range(NA_WIN_H // 2):
                bias_ref[d, head * GRID_W:(head + 1) * GRID_W, m * LANES:(m + 1) * LANES] = (
                    tiles[NA_WIN_H - 1 - d + 2 * m])


def _na_kernel(w_ref, q_ref, k_ref, v_ref, o_ref, bias_ref, p_ref, denom_ref):
    @pl.when(pl.program_id(1) == 0)
    def _():
        _na_build_bias(w_ref, bias_ref)

    lo, hi = _half_masks(GRID_W)
    lo_f = lax.broadcasted_iota(jnp.int32, (GRID_W, LANES), 1) < HEAD_DIM
    span = NA_WIN_H * GRID_W

    def pair_rows(r):
        return pl.ds(pl.multiple_of(r * 2 * GRID_W, 2 * GRID_W), 2 * GRID_W)

    def softmax_step(r, carry):
        rs = jnp.clip(r - NA_WIN_H // 2, 0, ROWS - NA_WIN_H)
        q = q_ref[0, pl.ds(pl.multiple_of(r * GRID_W, GRID_W), GRID_W), :]
        lhs = jnp.concatenate([q * lo, q * hi], axis=0)
        kw = k_ref[0, pl.ds(pl.multiple_of(rs * GRID_W, GRID_W), span), :]
        s = lax.dot_general(lhs, kw, (((1,), (1,)), ((), ())), preferred_element_type=F32)
        s = s + bias_ref[r - rs]
        m = jnp.broadcast_to(jnp.max(s, axis=-1, keepdims=True), (2 * GRID_W, LANES))
        p = jnp.exp(s - jnp.concatenate([m] * (span // LANES), axis=1))
        p_ref[pair_rows(r), :] = p.astype(BF16)
        denom_ref[pair_rows(r), :] = jnp.broadcast_to(jnp.sum(p, axis=-1, keepdims=True), (2 * GRID_W, LANES))
        return carry

    def value_step(r, carry):
        rs = jnp.clip(r - NA_WIN_H // 2, 0, ROWS - NA_WIN_H)
        vw = v_ref[0, pl.ds(pl.multiple_of(rs * GRID_W, GRID_W), span), :]
        o = jnp.dot(p_ref[pair_rows(r), :], vw, preferred_element_type=F32) / denom_ref[pair_rows(r), :]
        out = jnp.where(lo_f, o[:GRID_W], o[GRID_W:])
        o_ref[0, pl.ds(pl.multiple_of(r * GRID_W, GRID_W), GRID_W), :] = out.astype(BF16)
        return carry

    lax.fori_loop(0, ROWS, softmax_step, 0, unroll=NA_ROW_UNROLL)
    lax.fori_loop(0, ROWS, value_step, 0, unroll=NA_ROW_UNROLL)


def _na_attention(qkv, rpb_rows):
    pairs = NA_HEADS // 2
    return pl.pallas_call(
        _na_kernel,
        grid=(pairs, BATCH),
        in_specs=[
            pl.BlockSpec((2, NA_BIAS_ROWS, LANES), lambda p, b: (p, 0, 0)),
            pl.BlockSpec((1, SEQ, LANES), lambda p, b: (b, 0, p)),
            pl.BlockSpec((1, SEQ, LANES), lambda p, b: (b, 0, pairs + p)),
            pl.BlockSpec((1, SEQ, LANES), lambda p, b: (b, 0, 2 * pairs + p)),
        ],
        out_specs=pl.BlockSpec((1, SEQ, LANES), lambda p, b: (b, 0, p)),
        out_shape=jax.ShapeDtypeStruct((BATCH, SEQ, NA_HEADS * HEAD_DIM), BF16),
        scratch_shapes=[pltpu.VMEM((NA_WIN_H, 2 * GRID_W, NA_WIN_H * GRID_W), F32),
                        pltpu.VMEM((2 * SEQ, NA_WIN_H * GRID_W), BF16),
                        pltpu.VMEM((2 * SEQ, LANES), F32)],
        compiler_params=_params(("arbitrary", "arbitrary"), 32),
        name="na_attention",
    )(rpb_rows, qkv, qkv, qkv)


def _t5_buckets(rel):
    half = T5_BUCKETS // 2
    max_exact = half // 2
    n = np.abs(rel)
    large = max_exact + (np.log(np.maximum(n, 1) / max_exact)
                         / np.log(T5_MAX_DIST / max_exact) * (half - max_exact)).astype(np.int32)
    large = np.minimum(large, half - 1)
    return (rel > 0).astype(np.int32) * half + np.where(n < max_exact, n, large)


SW_REL_PERIOD = 512
SW_BLOCK_UNROLL = 2


def _sw_rel_table(t5_table):
    k = np.arange(SW_REL_PERIOD)
    rel = np.where(k < SW_REL_PERIOD // 2, k, k - SW_REL_PERIOD)
    ok = np.abs(rel) <= SW_WINDOW
    vals = jnp.where(ok[:, None], t5_table[_t5_buckets(rel)].astype(F32), NEG)
    return vals.T.reshape(SW_KV_HEADS, SW_GROUP, SW_REL_PERIOD)


def _sw_build_bias(rel_ref, bias_ref):
    for g in range(SW_GROUP):
        row = jnp.broadcast_to(rel_ref[0, g:g + 1, :], (SW_BLOCK, SW_REL_PERIOD))
        t = pltpu.roll(row, 0, 1, stride=1, stride_axis=0)
        rows = slice(g * SW_BLOCK, (g + 1) * SW_BLOCK)
        bias_ref[0, rows, :] = t[:, :SW_SPAN]
        bias_ref[1, rows, :] = jnp.concatenate([t[:, SW_SPAN:], t[:, :2 * SW_BLOCK]], axis=1)
        bias_ref[2, rows, :] = jnp.concatenate([t[:, 2 * SW_BLOCK:], t[:, :SW_BLOCK]], axis=1)


def _sw_kernel(sink_ref, rel_ref, q_ref, k_ref, v_ref, o_ref, bias_ref, p_ref, denom_ref):
    j = pl.program_id(0)

    @pl.when(pl.program_id(1) == 0)
    def _():
        _sw_build_bias(rel_ref, bias_ref)

    masks = _half_masks(SW_BLOCK)
    lo_f = lax.broadcasted_iota(jnp.int32, (SW_BLOCK, LANES), 1) < HEAD_DIM
    sinks = [jnp.full((SW_BLOCK, LANES), sink_ref[j * SW_GROUP + g], F32) for g in range(SW_GROUP)]

    def chain_rows(n, g):
        return pl.ds(pl.multiple_of((n * SW_GROUP + g) * SW_BLOCK, SW_BLOCK), SW_BLOCK)

    def softmax_step(n, carry):
        first = jnp.clip(n - 1, 0, SW_NB - 3)
        rows = pl.ds(pl.multiple_of(n * SW_BLOCK, SW_BLOCK), SW_BLOCK)
        kw = k_ref[0, pl.ds(pl.multiple_of(first * SW_BLOCK, SW_BLOCK), SW_SPAN), :]
        for g in range(SW_GROUP):
            q = q_ref[0, rows, (g // 2) * LANES:(g // 2 + 1) * LANES] * masks[g % 2]
            s = lax.dot_general(q, kw, (((1,), (1,)), ((), ())), preferred_element_type=F32)
            s = s + bias_ref[n - first, g * SW_BLOCK:(g + 1) * SW_BLOCK, :]
            m = jnp.maximum(jnp.broadcast_to(jnp.max(s, axis=-1, keepdims=True), (SW_BLOCK, LANES)), sinks[g])
            p = jnp.exp(s - jnp.concatenate([m] * (SW_SPAN // LANES), axis=1))
            p_ref[chain_rows(n, g), :] = p.astype(BF16)
            denom_ref[chain_rows(n, g), :] = (
                jnp.broadcast_to(jnp.sum(p, axis=-1, keepdims=True), (SW_BLOCK, LANES)) + jnp.exp(sinks[g] - m))
        return carry

    def value_step(n, carry):
        first = jnp.clip(n - 1, 0, SW_NB - 3)
        vw = v_ref[0, pl.ds(pl.multiple_of(first * SW_BLOCK, SW_BLOCK), SW_SPAN), :]
        heads = [jnp.dot(p_ref[chain_rows(n, g), :], vw, preferred_element_type=F32) / denom_ref[chain_rows(n, g), :]
                 for g in range(SW_GROUP)]
        out = jnp.concatenate([jnp.where(lo_f, heads[0], heads[1]), jnp.where(lo_f, heads[2], heads[3])], axis=1)
        o_ref[0, pl.ds(pl.multiple_of(n * SW_BLOCK, SW_BLOCK), SW_BLOCK), :] = out.astype(BF16)
        return carry

    lax.fori_loop(0, SW_NB, softmax_step, 0, unroll=SW_BLOCK_UNROLL)
    lax.fori_loop(0, SW_NB, value_step, 0, unroll=SW_BLOCK_UNROLL)


def _sw_attention(qkv, rel_table, sinks):
    qw = SW_GROUP * HEAD_DIM
    k_off = SW_Q_HEADS * HEAD_DIM // LANES
    v_off = k_off + SW_KV_HEADS
    return pl.pallas_call(
        _sw_kernel,
        grid=(SW_KV_HEADS, BATCH),
        in_specs=[
            pl.BlockSpec(memory_space=pltpu.SMEM),
            pl.BlockSpec((1, SW_GROUP, SW_REL_PERIOD), lambda j, b: (j, 0, 0)),
            pl.BlockSpec((1, SEQ, qw), lambda j, b: (b, 0, j)),
            pl.BlockSpec((1, SEQ, LANES), lambda j, b: (b, 0, k_off + j)),
            pl.BlockSpec((1, SEQ, LANES), lambda j, b: (b, 0, v_off + j)),
        ],
        out_specs=pl.BlockSpec((1, SEQ, qw), lambda j, b: (b, 0, j)),
        out_shape=jax.ShapeDtypeStruct((BATCH, SEQ, SW_Q_HEADS * HEAD_DIM), BF16),
        scratch_shapes=[pltpu.VMEM((3, SW_GROUP * SW_BLOCK, SW_SPAN), F32),
                        pltpu.VMEM((SW_GROUP * SEQ, SW_SPAN), BF16),
                        pltpu.VMEM((SW_GROUP * SEQ, LANES), F32)],
        compiler_params=_params(("arbitrary", "arbitrary"), 32),
        name="sw_attention",
    )(sinks, rel_table, qkv, qkv, qkv)


def _post_attn_kernel(o_ref, wo_ref, x_ref, gate_ref, g_ref, sc_ref, sh_ref, wr_ref,
                      xo_ref, h_ref, aff_ref):
    y = jnp.dot(o_ref[...], wo_ref[...], preferred_element_type=F32)
    xn = x_ref[...] + gate_ref[0] * y
    xo_ref[...] = xn
    hb = _norm_mod(xn, g_ref[0], sc_ref[0], sh_ref[0]).astype(BF16)
    h_ref[...] = hb
    logits = lax.dot_general(wr_ref[...], hb, (((1,), (1,)), ((), ())), preferred_element_type=F32)
    aff_ref[...] = _softmax0(logits)


def _post_attn(o, w_o, x, norm_g3, mod, w_router_t, layer):
    rows = BATCH * SEQ
    return pl.pallas_call(
        _post_attn_kernel,
        grid=(rows // ROW_TILE,),
        in_specs=[
            pl.BlockSpec((ROW_TILE, D_MODEL), lambda i: (i, 0)),
            pl.BlockSpec((D_MODEL, D_MODEL), lambda i: (0, 0)),
            pl.BlockSpec((ROW_TILE, D_MODEL), lambda i: (i, 0)),
            _mod_spec(layer, 2),
            pl.BlockSpec((1, 1, D_MODEL), lambda i: (layer * 2 + 1, 0, 0)),
            _mod_spec(layer, 4),
            _mod_spec(layer, 3),
            pl.BlockSpec((N_EXPERTS, D_MODEL), lambda i: (0, 0)),
        ],
        out_specs=[
            pl.BlockSpec((ROW_TILE, D_MODEL), lambda i: (i, 0)),
            pl.BlockSpec((ROW_TILE, D_MODEL), lambda i: (i, 0)),
            pl.BlockSpec((N_EXPERTS, ROW_TILE), lambda i: (0, i)),
        ],
        out_shape=[
            jax.ShapeDtypeStruct((rows, D_MODEL), F32),
            jax.ShapeDtypeStruct((rows, D_MODEL), BF16),
            jax.ShapeDtypeStruct((N_EXPERTS, rows), F32),
        ],
        compiler_params=_params(("arbitrary",), 48),
        name="post_attn",
    )(o, w_o, x, mod, norm_g3, mod, mod, w_router_t)


def _prefix_count(x):
    nchunk = SEQ // PREFIX_CHUNK
    r = lax.broadcasted_iota(jnp.int32, (PREFIX_CHUNK, PREFIX_CHUNK), 0)
    c = lax.broadcasted_iota(jnp.int32, (PREFIX_CHUNK, PREFIX_CHUNK), 1)
    upper = jnp.where(r < c, 1.0, 0.0).astype(BF16)
    chunks = [x[:, k * PREFIX_CHUNK:(k + 1) * PREFIX_CHUNK] for k in range(nchunk)]
    local = jnp.dot(jnp.concatenate(chunks, axis=0).astype(BF16), upper, preferred_element_type=F32)
    out = []
    offset = jnp.zeros((N_EXPERTS, 1), F32)
    for k in range(nchunk):
        out.append(local[k * N_EXPERTS:(k + 1) * N_EXPERTS] + offset)
        offset = offset + jnp.sum(chunks[k], axis=1, keepdims=True)
    return jnp.concatenate(out, axis=1)


def _route_kernel(aff_ref, slot_ref):
    bits = pltpu.bitcast(aff_ref[...], jnp.int32)

    def count(mask):
        return jnp.sum(jnp.where(mask, 1.0, 0.0), axis=1, keepdims=True)

    def search(_, bounds):
        lo, hi = bounds
        mid = lo + ((hi - lo) >> 1)
        ge = count(bits >= mid) >= CAP
        return jnp.where(ge, mid, lo), jnp.where(ge, hi, mid)

    lo0 = jnp.zeros((N_EXPERTS, 1), jnp.int32)
    hi0 = jnp.full((N_EXPERTS, 1), 0x7F800000, jnp.int32)
    tau, _ = lax.fori_loop(0, 31, search, (lo0, hi0))
    gt = bits > tau
    eq = jnp.where(bits == tau, 1.0, 0.0)
    need = CAP - count(gt)
    sel = jnp.where(gt, 1.0, jnp.where(_prefix_count(eq) < need, eq, 0.0))
    pos = _prefix_count(sel).astype(jnp.int32)
    slot_ref[...] = jnp.where(sel > 0.5, pos, -1)


def _route(aff_t):
    return pl.pallas_call(
        _route_kernel,
        grid=(BATCH,),
        in_specs=[pl.BlockSpec((N_EXPERTS, SEQ), lambda b: (0, b))],
        out_specs=pl.BlockSpec((N_EXPERTS, SEQ), lambda b: (0, b)),
        out_shape=jax.ShapeDtypeStruct((N_EXPERTS, BATCH * SEQ), jnp.int32),
        compiler_params=_params(("arbitrary",), 32),
        name="route",
    )(aff_t)


def _gather_kernel(slot_ref, h_ref, xg_ref):
    e = pl.program_id(1)
    row = slot_ref[pl.ds(e, 1), :]
    c = lax.broadcasted_iota(jnp.int32, (CAP, SEQ), 0)
    onehot = jnp.where(row == c, 1.0, 0.0).astype(BF16)
    xg_ref[0, 0] = jnp.dot(onehot, h_ref[0], preferred_element_type=F32).astype(BF16)


def _gather(slot, h):
    return pl.pallas_call(
        _gather_kernel,
        grid=(BATCH, N_EXPERTS),
        in_specs=[
            pl.BlockSpec((N_EXPERTS, SEQ), lambda b, e: (0, b)),
            pl.BlockSpec((1, SEQ, D_MODEL), lambda b, e: (b, 0, 0)),
        ],
        out_specs=pl.BlockSpec((1, 1, CAP, D_MODEL), lambda b, e: (b, e, 0, 0)),
        out_shape=jax.ShapeDtypeStruct((BATCH, N_EXPERTS, CAP, D_MODEL), BF16),
        compiler_params=_params(("arbitrary", "arbitrary"), 32),
        name="moe_gather",
    )(slot, h)


def _expert_kernel(x_ref, wr_ref, wg_ref, wu_ref, wd_ref, y_ref, acc_ref, gate_ref):
    e = pl.program_id(0)
    f = pl.program_id(1)

    @pl.when(f == 0)
    def _():
        acc_ref[...] = jnp.zeros_like(acc_ref)
        for b in range(BATCH):
            logits = jnp.dot(x_ref[b, 0], wr_ref[...], preferred_element_type=F32)
            z = logits - jnp.max(logits, axis=-1, keepdims=True)
            p = jnp.exp(z)
            aff = p / jnp.sum(p, axis=-1, keepdims=True)
            lane = lax.broadcasted_iota(jnp.int32, aff.shape, 1)
            gate_ref[b] = jnp.sum(jnp.where(lane == e, aff, 0.0), axis=-1, keepdims=True)

    wg = wg_ref[0, 0].astype(BF16)
    wu = wu_ref[0, 0].astype(BF16)
    wd = wd_ref[0, 0].astype(BF16)
    for b in range(BATCH):
        x = x_ref[b, 0]
        a = jnp.dot(x, wg, preferred_element_type=F32)
        u = jnp.dot(x, wu, preferred_element_type=F32)
        act = ((a * jax.nn.sigmoid(a)) * u).astype(BF16)
        acc_ref[b] += jnp.dot(act, wd, preferred_element_type=F32)

    @pl.when(f == EXPERT_FF // FF_TILE - 1)
    def _():
        for b in range(BATCH):
            y_ref[0, b] = (acc_ref[b] * gate_ref[b]).astype(BF16)


def _experts(xg, w_router, w_gate, w_up, w_down, layer):
    return pl.pallas_call(
        _expert_kernel,
        grid=(N_EXPERTS, EXPERT_FF // FF_TILE),
        in_specs=[
            pl.BlockSpec((BATCH, 1, CAP, D_MODEL), lambda e, f: (0, e, 0, 0)),
            pl.BlockSpec((D_MODEL, N_EXPERTS), lambda e, f: (0, 0)),
            pl.BlockSpec((1, 1, D_MODEL, FF_TILE), lambda e, f: (layer, e, 0, f)),
            pl.BlockSpec((1, 1, D_MODEL, FF_TILE), lambda e, f: (layer, e, 0, f)),
            pl.BlockSpec((1, 1, FF_TILE, D_MODEL), lambda e, f: (layer, e, f, 0)),
        ],
        out_specs=pl.BlockSpec((1, BATCH, CAP, D_MODEL), lambda e, f: (e, 0, 0, 0)),
        out_shape=jax.ShapeDtypeStruct((N_EXPERTS, BATCH, CAP, D_MODEL), BF16),
        scratch_shapes=[pltpu.VMEM((BATCH, CAP, D_MODEL), F32), pltpu.VMEM((BATCH, CAP, 1), F32)],
        compiler_params=_params(("arbitrary", "arbitrary"), 56),
        name="moe_experts",
    )(xg, w_router, w_gate, w_up, w_down)


def _combine_kernel(slot_ref, y_ref, x_ref, gate_ref, o_ref, onehot_ref):
    @pl.when(pl.program_id(1) == 0)
    def _():
        lane = lax.broadcasted_iota(jnp.int32, (SEQ, CAP), 1)
        slots = slot_ref[0]
        for e in range(N_EXPERTS):
            hit = slots[:, e:e + 1] == lane
            onehot_ref[:, e * CAP:(e + 1) * CAP] = jnp.where(hit, 1.0, 0.0).astype(BF16)

    y = y_ref[...].reshape(N_EXPERTS * CAP, COMBINE_COLS)
    moe = jnp.dot(onehot_ref[...], y, preferred_element_type=F32)
    o_ref[0] = x_ref[0] + gate_ref[0] * moe


def _combine(slot_t, y, x, mod, layer):
    return pl.pallas_call(
        _combine_kernel,
        grid=(BATCH, D_MODEL // COMBINE_COLS),
        in_specs=[
            pl.BlockSpec((1, SEQ, N_EXPERTS), lambda b, n: (b, 0, 0)),
            pl.BlockSpec((N_EXPERTS, 1, CAP, COMBINE_COLS), lambda b, n: (0, b, 0, n)),
            pl.BlockSpec((1, SEQ, COMBINE_COLS), lambda b, n: (b, 0, n)),
            pl.BlockSpec((1, 1, COMBINE_COLS), lambda b, n: ((layer * 6 + 5) * BATCH + b, 0, n)),
        ],
        out_specs=pl.BlockSpec((1, SEQ, COMBINE_COLS), lambda b, n: (b, 0, n)),
        out_shape=jax.ShapeDtypeStruct((BATCH, SEQ, D_MODEL), F32),
        scratch_shapes=[pltpu.VMEM((SEQ, N_EXPERTS * CAP), BF16)],
        compiler_params=_params(("arbitrary", "arbitrary"), 48),
        name="moe_combine",
    )(slot_t, y, x, mod)


def _final_norm_kernel(x_ref, g_ref, o_ref):
    x = x_ref[...]
    y = x * lax.rsqrt(jnp.mean(x * x, axis=-1, keepdims=True) + RMS_EPS)
    o_ref[...] = y * g_ref[...]


def _final_norm(x, g):
    rows = BATCH * SEQ
    return pl.pallas_call(
        _final_norm_kernel,
        grid=(rows // ROW_TILE,),
        in_specs=[pl.BlockSpec((ROW_TILE, D_MODEL), lambda i: (i, 0)),
                  pl.BlockSpec((1, D_MODEL), lambda i: (0, 0))],
        out_specs=pl.BlockSpec((ROW_TILE, D_MODEL), lambda i: (i, 0)),
        out_shape=jax.ShapeDtypeStruct((rows, D_MODEL), F32),
        compiler_params=_params(("arbitrary",), 32),
        name="final_norm",
    )(x, g.reshape(1, D_MODEL))


def _na_qkv_weight(w):
    nq = NA_HEADS * HEAD_DIM
    scale = jnp.concatenate([jnp.full((nq,), HEAD_DIM ** -0.5, F32), jnp.ones((2 * nq,), F32)])
    return (w * scale).astype(BF16)


def _sw_qkv_weight(w):
    nq = SW_Q_HEADS * HEAD_DIM
    nkv = SW_KV_HEADS * HEAD_DIM
    dup = lambda t: jnp.concatenate([t.reshape(D_MODEL, SW_KV_HEADS, 1, HEAD_DIM)] * 2, axis=2).reshape(
        D_MODEL, 2 * nkv)
    wq = w[:, :nq] * HEAD_DIM ** -0.5
    return jnp.concatenate([wq, dup(w[:, nq:nq + nkv]), dup(w[:, nq + nkv:])], axis=1).astype(BF16)


def kernel(x, c, ada_w, ada_b, norm_g, na_w_qkv, na_w_o, na_rpb, sw_w_qkv, sw_w_o, sw_sinks, t5_bias,
           moe_w_router, moe_w_gate, moe_w_up, moe_w_down, final_g):
    mod = _ada(c, ada_w, ada_b)
    norm_g3 = norm_g.reshape(DEPTH * 2, 1, D_MODEL)
    x = x.reshape(BATCH * SEQ, D_MODEL)
    for layer in range(DEPTH):
        j = layer // N_MIXERS
        if layer % N_MIXERS == 0:
            qkv = _qkv(x, norm_g3, mod, _na_qkv_weight(na_w_qkv[j]), layer)
            o = _na_attention(qkv.reshape(BATCH, SEQ, -1), _na_rpb_rows(na_rpb[j]))
            w_o = na_w_o[j]
        else:
            qkv = _qkv(x, norm_g3, mod, _sw_qkv_weight(sw_w_qkv[j]), layer)
            o = _sw_attention(qkv.reshape(BATCH, SEQ, -1), _sw_rel_table(t5_bias), sw_sinks[j])
            w_o = sw_w_o[j]
        w_router = moe_w_router[layer].astype(BF16)
        x, h, aff_t = _post_attn(o.reshape(BATCH * SEQ, -1), w_o.astype(BF16), x, norm_g3, mod,
                                 w_router.T, layer)
        slot = _route(aff_t)
        xg = _gather(slot, h.reshape(BATCH, SEQ, D_MODEL))
        y = _experts(xg, w_router, moe_w_gate, moe_w_up, moe_w_down, layer)
        slot_t = slot.reshape(N_EXPERTS, BATCH, SEQ).transpose(1, 2, 0)
        x = _combine(slot_t, y, x.reshape(BATCH, SEQ, D_MODEL), mod, layer).reshape(BATCH * SEQ, D_MODEL)
    return _final_norm(x, final_g).reshape(BATCH, SEQ, D_MODEL)
```

```python
import functools

import numpy as np
import jax
import jax.numpy as jnp
from jax import lax
from jax.experimental import pallas as pl
from jax.experimental.pallas import tpu as pltpu

D_MODEL = 1024
BATCH = 8
SEQ = 2048
DEPTH = 2
GRID_W = 64
ROWS = SEQ // GRID_W
N_MIXERS = 2
HEAD_DIM = 64
NA_HEADS = 16
NA_WIN_H = 8
NA_WIN_W = 16
SW_Q_HEADS = 16
SW_KV_HEADS = 4
SW_GROUP = SW_Q_HEADS // SW_KV_HEADS
SW_WINDOW = 128
SW_BLOCK = 128
SW_NB = SEQ // SW_BLOCK
SW_SPAN = 3 * SW_BLOCK
T5_BUCKETS = 32
T5_MAX_DIST = 128
N_EXPERTS = 16
EXPERT_FF = 2048
EC_CAPACITY = 2
CAP = EC_CAPACITY * SEQ // N_EXPERTS
RMS_EPS = 1e-6
NEG = -1e30

LANES = 128
MIB = 1024 * 1024
F32 = jnp.float32
BF16 = jnp.bfloat16

ROW_TILE = 512
FF_TILE = 512
COMBINE_ROWS = 512
PREFIX_CHUNK = 256


def _params(semantics, vmem_mib, flags=None):
    return pltpu.CompilerParams(dimension_semantics=semantics, vmem_limit_bytes=vmem_mib * MIB, flags=flags)


def _norm_mod(x, g, sc, sh):
    y = x * lax.rsqrt(jnp.mean(x * x, axis=-1, keepdims=True) + RMS_EPS)
    return (y * g) * (1.0 + sc) + sh


def _softmax0(z):
    z = z - jnp.max(z, axis=0, keepdims=True)
    p = jnp.exp(z)
    return p / jnp.sum(p, axis=0, keepdims=True)


def _half_masks(rows):
    lane = lax.broadcasted_iota(jnp.int32, (rows, LANES), 1)
    lo = jnp.where(lane < HEAD_DIM, 1.0, 0.0).astype(BF16)
    hi = jnp.where(lane < HEAD_DIM, 0.0, 1.0).astype(BF16)
    return lo, hi


def _ada_kernel(c_ref, w_ref, b_ref, o_ref):
    c = c_ref[...]
    act = (c * jax.nn.sigmoid(c)).astype(BF16)
    o_ref[0] = jnp.dot(act, w_ref[0].astype(BF16), preferred_element_type=F32) + b_ref[0]


def _ada(c, ada_w, ada_b):
    out = pl.pallas_call(
        _ada_kernel,
        grid=(DEPTH, 6),
        in_specs=[
            pl.BlockSpec((BATCH, D_MODEL), lambda l, k: (0, 0)),
            pl.BlockSpec((1, D_MODEL, D_MODEL), lambda l, k: (l, 0, k)),
            pl.BlockSpec((1, 1, D_MODEL), lambda l, k: (l * 6 + k, 0, 0)),
        ],
        out_specs=pl.BlockSpec((1, BATCH, D_MODEL), lambda l, k: (l * 6 + k, 0, 0)),
        out_shape=jax.ShapeDtypeStruct((DEPTH * 6, BATCH, D_MODEL), F32),
        compiler_params=_params(("arbitrary", "arbitrary"), 32),
        name="ada_mod",
    )(c, ada_w, ada_b.reshape(DEPTH * 6, 1, D_MODEL))
    return out.reshape(DEPTH * 6 * BATCH, 1, D_MODEL)


def _mod_spec(layer, chunk):
    tiles_per_seq = SEQ // ROW_TILE
    return pl.BlockSpec((1, 1, D_MODEL),
                        lambda i: ((layer * 6 + chunk) * BATCH + i // tiles_per_seq, 0, 0))


def _qkv_kernel(x_ref, g_ref, sc_ref, sh_ref, w_ref, o_ref):
    h = _norm_mod(x_ref[...], g_ref[0], sc_ref[0], sh_ref[0])
    o_ref[...] = jnp.dot(h.astype(BF16), w_ref[...], preferred_element_type=F32).astype(BF16)


def _qkv(x, norm_g3, mod, w, layer):
    n = w.shape[1]
    return pl.pallas_call(
        _qkv_kernel,
        grid=(BATCH * SEQ // ROW_TILE,),
        in_specs=[
            pl.BlockSpec((ROW_TILE, D_MODEL), lambda i: (i, 0)),
            pl.BlockSpec((1, 1, D_MODEL), lambda i: (layer * 2, 0, 0)),
            _mod_spec(layer, 1),
            _mod_spec(layer, 0),
            pl.BlockSpec((D_MODEL, n), lambda i: (0, 0)),
        ],
        out_specs=pl.BlockSpec((ROW_TILE, n), lambda i: (i, 0)),
        out_shape=jax.ShapeDtypeStruct((BATCH * SEQ, n), BF16),
        compiler_params=_params(("arbitrary",), 48),
        name="norm_qkv",
    )(x, norm_g3, mod, mod, w)


NA_BIAS_ROWS = 2 * NA_WIN_H - 1
NA_ROW_UNROLL = 4


def _na_rpb_rows(rpb):
    w = NA_WIN_W - 1
    pad = jnp.zeros(rpb.shape[:2] + (LANES - 2 * w - 1,), F32)
    return jnp.concatenate([rpb[..., w:].astype(F32), pad, rpb[..., :w].astype(F32)], axis=-1)


def _na_build_bias(w_ref, bias_ref):
    lane = lax.broadcasted_iota(jnp.int32, (GRID_W, LANES), 1)
    col = lax.broadcasted_iota(jnp.int32, (GRID_W, LANES), 0)
    kc = jnp.bitwise_and(lane, GRID_W - 1)
    cstart = jnp.clip(col - NA_WIN_W // 2, 0, GRID_W - NA_WIN_W)
    in_window = jnp.where(kc >= cstart, jnp.where(kc < cstart + NA_WIN_W, 1.0, 0.0), 0.0) > 0.5
    low_half = lane < GRID_W

    def toeplitz(head, a, shift):
        row = jnp.broadcast_to(w_ref[head, a:a + 1, :], (GRID_W, LANES))
        return pltpu.roll(row, shift, 1, stride=1, stride_axis=0)

    for head in range(2):
        tiles = [jnp.where(in_window, jnp.where(low_half, toeplitz(head, a, 0), toeplitz(head, a + 1, GRID_W)), NEG)
                 for a in range(NA_BIAS_ROWS - 1)]
        for d in range(NA_WIN_H):
            for m in range(NA_WIN_H // 2):
                bias_ref[d, head * GRID_W:(head + 1) * GRID_W, m * LANES:(m + 1) * LANES] = (
                    tiles[NA_WIN_H - 1 - d + 2 * m])


def _na_kernel(w_ref, q_ref, k_ref, v_ref, o_ref, bias_ref, p_ref, denom_ref, s0_ref, s1_ref):
    @pl.when(pl.program_id(1) == 0)
    def _():
        _na_build_bias(w_ref, bias_ref)

    lo, hi = _half_masks(GRID_W)
    lo_f = lax.broadcasted_iota(jnp.int32, (GRID_W, LANES), 1) < HEAD_DIM
    span = NA_WIN_H * GRID_W

    def pair_rows(r):
        return pl.ds(pl.multiple_of(r * 2 * GRID_W, 2 * GRID_W), 2 * GRID_W)

    def group_scores(i, s_ref):
        for u in range(NA_ROW_UNROLL):
            r = jnp.minimum(i * NA_ROW_UNROLL + u, ROWS - 1)
            rs = jnp.clip(r - NA_WIN_H // 2, 0, ROWS - NA_WIN_H)
            q = q_ref[0, pl.ds(pl.multiple_of(r * GRID_W, GRID_W), GRID_W), :]
            lhs = jnp.concatenate([q * lo, q * hi], axis=0)
            kw = k_ref[0, pl.ds(pl.multiple_of(rs * GRID_W, GRID_W), span), :]
            s_ref[u] = lax.dot_general(lhs, kw, (((1,), (1,)), ((), ())), preferred_element_type=F32)

    def group_softmax(i, s_ref):
        for u in range(NA_ROW_UNROLL):
            r = i * NA_ROW_UNROLL + u
            rs = jnp.clip(r - NA_WIN_H // 2, 0, ROWS - NA_WIN_H)
            s = s_ref[u] + bias_ref[r - rs]
            m = jnp.broadcast_to(jnp.max(s, axis=-1, keepdims=True), (2 * GRID_W, LANES))
            p = jnp.exp(s - jnp.concatenate([m] * (span // LANES), axis=1))
            p_ref[pair_rows(r), :] = p.astype(BF16)
            denom_ref[pair_rows(r), :] = jnp.broadcast_to(jnp.sum(p, axis=-1, keepdims=True), (2 * GRID_W, LANES))

    def softmax_step(i, carry):
        group_scores(2 * i + 1, s1_ref)
        group_softmax(2 * i, s0_ref)
        group_scores(2 * i + 2, s0_ref)
        group_softmax(2 * i + 1, s1_ref)
        return carry

    def value_step(r, carry):
        rs = jnp.clip(r - NA_WIN_H // 2, 0, ROWS - NA_WIN_H)
        vw = v_ref[0, pl.ds(pl.multiple_of(rs * GRID_W, GRID_W), span), :]
        o = jnp.dot(p_ref[pair_rows(r), :], vw, preferred_element_type=F32) / denom_ref[pair_rows(r), :]
        out = jnp.where(lo_f, o[:GRID_W], o[GRID_W:])
        o_ref[0, pl.ds(pl.multiple_of(r * GRID_W, GRID_W), GRID_W), :] = out.astype(BF16)
        return carry

    group_scores(0, s0_ref)
    lax.fori_loop(0, ROWS // (2 * NA_ROW_UNROLL), softmax_step, 0)
    lax.fori_loop(0, ROWS, value_step, 0, unroll=NA_ROW_UNROLL)


def _na_attention(qkv, rpb_rows):
    pairs = NA_HEADS // 2
    return pl.pallas_call(
        _na_kernel,
        grid=(pairs, BATCH),
        in_specs=[
            pl.BlockSpec((2, NA_BIAS_ROWS, LANES), lambda p, b: (p, 0, 0)),
            pl.BlockSpec((1, SEQ, LANES), lambda p, b: (b, 0, p)),
            pl.BlockSpec((1, SEQ, LANES), lambda p, b: (b, 0, pairs + p)),
            pl.BlockSpec((1, SEQ, LANES), lambda p, b: (b, 0, 2 * pairs + p)),
        ],
        out_specs=pl.BlockSpec((1, SEQ, LANES), lambda p, b: (b, 0, p)),
        out_shape=jax.ShapeDtypeStruct((BATCH, SEQ, NA_HEADS * HEAD_DIM), BF16),
        scratch_shapes=[pltpu.VMEM((NA_WIN_H, 2 * GRID_W, NA_WIN_H * GRID_W), F32),
                        pltpu.VMEM((2 * SEQ, NA_WIN_H * GRID_W), BF16),
                        pltpu.VMEM((2 * SEQ, LANES), F32),
                        pltpu.VMEM((NA_ROW_UNROLL, 2 * GRID_W, NA_WIN_H * GRID_W), F32),
                        pltpu.VMEM((NA_ROW_UNROLL, 2 * GRID_W, NA_WIN_H * GRID_W), F32)],
        compiler_params=_params(("arbitrary", "arbitrary"), 32),
        name="na_attention",
    )(rpb_rows, qkv, qkv, qkv)


def _t5_buckets(rel):
    half = T5_BUCKETS // 2
    max_exact = half // 2
    n = np.abs(rel)
    large = max_exact + (np.log(np.maximum(n, 1) / max_exact)
                         / np.log(T5_MAX_DIST / max_exact) * (half - max_exact)).astype(np.int32)
    large = np.minimum(large, half - 1)
    return (rel > 0).astype(np.int32) * half + np.where(n < max_exact, n, large)


SW_REL_PERIOD = 512
SW_BLOCK_UNROLL = 2


def _sw_rel_table(t5_table):
    k = np.arange(SW_REL_PERIOD)
    rel = np.where(k < SW_REL_PERIOD // 2, k, k - SW_REL_PERIOD)
    ok = np.abs(rel) <= SW_WINDOW
    vals = jnp.where(ok[:, None], t5_table[_t5_buckets(rel)].astype(F32), NEG)
    return vals.T.reshape(SW_KV_HEADS, SW_GROUP, SW_REL_PERIOD)


def _sw_build_bias(rel_ref, bias_ref):
    for g in range(SW_GROUP):
        row = jnp.broadcast_to(rel_ref[0, g:g + 1, :], (SW_BLOCK, SW_REL_PERIOD))
        t = pltpu.roll(row, 0, 1, stride=1, stride_axis=0)
        rows = slice(g * SW_BLOCK, (g + 1) * SW_BLOCK)
        bias_ref[0, rows, :] = t[:, :SW_SPAN]
        bias_ref[1, rows, :] = jnp.concatenate([t[:, SW_SPAN:], t[:, :2 * SW_BLOCK]], axis=1)
        bias_ref[2, rows, :] = jnp.concatenate([t[:, 2 * SW_BLOCK:], t[:, :SW_BLOCK]], axis=1)


def _sw_kernel(sink_ref, rel_ref, q_ref, k_ref, v_ref, o_ref, bias_ref, p_ref, denom_ref):
    j = pl.program_id(0)

    @pl.when(pl.program_id(1) == 0)
    def _():
        _sw_build_bias(rel_ref, bias_ref)

    masks = _half_masks(SW_BLOCK)
    lo_f = lax.broadcasted_iota(jnp.int32, (SW_BLOCK, LANES), 1) < HEAD_DIM
    sinks = [jnp.full((SW_BLOCK, LANES), sink_ref[j * SW_GROUP + g], F32) for g in range(SW_GROUP)]

    def chain_rows(n, g):
        return pl.ds(pl.multiple_of((n * SW_GROUP + g) * SW_BLOCK, SW_BLOCK), SW_BLOCK)

    def softmax_step(n, carry):
        first = jnp.clip(n - 1, 0, SW_NB - 3)
        rows = pl.ds(pl.multiple_of(n * SW_BLOCK, SW_BLOCK), SW_BLOCK)
        kw = k_ref[0, pl.ds(pl.multiple_of(first * SW_BLOCK, SW_BLOCK), SW_SPAN), :]
        for g in range(SW_GROUP):
            q = q_ref[0, rows, (g // 2) * LANES:(g // 2 + 1) * LANES] * masks[g % 2]
            s = lax.dot_general(q, kw, (((1,), (1,)), ((), ())), preferred_element_type=F32)
            s = s + bias_ref[n - first, g * SW_BLOCK:(g + 1) * SW_BLOCK, :]
            m = jnp.maximum(jnp.broadcast_to(jnp.max(s, axis=-1, keepdims=True), (SW_BLOCK, LANES)), sinks[g])
            p = jnp.exp(s - jnp.concatenate([m] * (SW_SPAN // LANES), axis=1))
            p_ref[chain_rows(n, g), :] = p.astype(BF16)
            denom_ref[chain_rows(n, g), :] = (
                jnp.broadcast_to(jnp.sum(p, axis=-1, keepdims=True), (SW_BLOCK, LANES)) + jnp.exp(sinks[g] - m))
        return carry

    def value_step(n, carry):
        first = jnp.clip(n - 1, 0, SW_NB - 3)
        vw = v_ref[0, pl.ds(pl.multiple_of(first * SW_BLOCK, SW_BLOCK), SW_SPAN), :]
        heads = [jnp.dot(p_ref[chain_rows(n, g), :], vw, preferred_element_type=F32) / denom_ref[chain_rows(n, g), :]
                 for g in range(SW_GROUP)]
        out = jnp.concatenate([jnp.where(lo_f, heads[0], heads[1]), jnp.where(lo_f, heads[2], heads[3])], axis=1)
        o_ref[0, pl.ds(pl.multiple_of(n * SW_BLOCK, SW_BLOCK), SW_BLOCK), :] = out.astype(BF16)
        return carry

    lax.fori_loop(0, SW_NB, softmax_step, 0, unroll=SW_BLOCK_UNROLL)
    lax.fori_loop(0, SW_NB, value_step, 0, unroll=SW_BLOCK_UNROLL)


def _sw_attention(qkv, rel_table, sinks):
    qw = SW_GROUP * HEAD_DIM
    k_off = SW_Q_HEADS * HEAD_DIM // LANES
    v_off = k_off + SW_KV_HEADS
    return pl.pallas_call(
        _sw_kernel,
        grid=(SW_KV_HEADS, BATCH),
        in_specs=[
            pl.BlockSpec(memory_space=pltpu.SMEM),
            pl.BlockSpec((1, SW_GROUP, SW_REL_PERIOD), lambda j, b: (j, 0, 0)),
            pl.BlockSpec((1, SEQ, qw), lambda j, b: (b, 0, j)),
            pl.BlockSpec((1, SEQ, LANES), lambda j, b: (b, 0, k_off + j)),
            pl.BlockSpec((1, SEQ, LANES), lambda j, b: (b, 0, v_off + j)),
        ],
        out_specs=pl.BlockSpec((1, SEQ, qw), lambda j, b: (b, 0, j)),
        out_shape=jax.ShapeDtypeStruct((BATCH, SEQ, SW_Q_HEADS * HEAD_DIM), BF16),
        scratch_shapes=[pltpu.VMEM((3, SW_GROUP * SW_BLOCK, SW_SPAN), F32),
                        pltpu.VMEM((SW_GROUP * SEQ, SW_SPAN), BF16),
                        pltpu.VMEM((SW_GROUP * SEQ, LANES), F32)],
        compiler_params=_params(("arbitrary", "arbitrary"), 32),
        name="sw_attention",
    )(sinks, rel_table, qkv, qkv, qkv)


def _post_attn_kernel(o_ref, wo_ref, x_ref, gate_ref, g_ref, sc_ref, sh_ref, wr_ref,
                      xo_ref, h_ref, aff_ref):
    y = jnp.dot(o_ref[...], wo_ref[...], preferred_element_type=F32)
    xn = x_ref[...] + gate_ref[0] * y
    xo_ref[...] = xn
    hb = _norm_mod(xn, g_ref[0], sc_ref[0], sh_ref[0]).astype(BF16)
    h_ref[...] = hb
    logits = lax.dot_general(wr_ref[...], hb, (((1,), (1,)), ((), ())), preferred_element_type=F32)
    aff_ref[...] = _softmax0(logits)


def _post_attn(o, w_o, x, norm_g3, mod, w_router_t, layer):
    rows = BATCH * SEQ
    return pl.pallas_call(
        _post_attn_kernel,
        grid=(rows // ROW_TILE,),
        in_specs=[
            pl.BlockSpec((ROW_TILE, D_MODEL), lambda i: (i, 0)),
            pl.BlockSpec((D_MODEL, D_MODEL), lambda i: (0, 0)),
            pl.BlockSpec((ROW_TILE, D_MODEL), lambda i: (i, 0)),
            _mod_spec(layer, 2),
            pl.BlockSpec((1, 1, D_MODEL), lambda i: (layer * 2 + 1, 0, 0)),
            _mod_spec(layer, 4),
            _mod_spec(layer, 3),
            pl.BlockSpec((N_EXPERTS, D_MODEL), lambda i: (0, 0)),
        ],
        out_specs=[
            pl.BlockSpec((ROW_TILE, D_MODEL), lambda i: (i, 0)),
            pl.BlockSpec((ROW_TILE, D_MODEL), lambda i: (i, 0)),
            pl.BlockSpec((N_EXPERTS, ROW_TILE), lambda i: (0, i)),
        ],
        out_shape=[
            jax.ShapeDtypeStruct((rows, D_MODEL), F32),
            jax.ShapeDtypeStruct((rows, D_MODEL), BF16),
            jax.ShapeDtypeStruct((N_EXPERTS, rows), F32),
        ],
        compiler_params=_params(("arbitrary",), 48),
        name="post_attn",
    )(o, w_o, x, mod, norm_g3, mod, mod, w_router_t)


def _prefix_count(x):
    nchunk = SEQ // PREFIX_CHUNK
    r = lax.broadcasted_iota(jnp.int32, (PREFIX_CHUNK, PREFIX_CHUNK), 0)
    c = lax.broadcasted_iota(jnp.int32, (PREFIX_CHUNK, PREFIX_CHUNK), 1)
    upper = jnp.where(r < c, 1.0, 0.0).astype(BF16)
    chunks = [x[:, k * PREFIX_CHUNK:(k + 1) * PREFIX_CHUNK] for k in range(nchunk)]
    local = jnp.dot(jnp.concatenate(chunks, axis=0).astype(BF16), upper, preferred_element_type=F32)
    out = []
    offset = jnp.zeros((N_EXPERTS, 1), F32)
    for k in range(nchunk):
        out.append(local[k * N_EXPERTS:(k + 1) * N_EXPERTS] + offset)
        offset = offset + jnp.sum(chunks[k], axis=1, keepdims=True)
    return jnp.concatenate(out, axis=1)


def _route_kernel(aff_ref, slot_ref):
    bits = pltpu.bitcast(aff_ref[...], jnp.int32)

    def count(mask):
        return jnp.sum(jnp.where(mask, 1.0, 0.0), axis=1, keepdims=True)

    def search(_, bounds):
        lo, hi = bounds
        mid = lo + ((hi - lo) >> 1)
        ge = count(bits >= mid) >= CAP
        return jnp.where(ge, mid, lo), jnp.where(ge, hi, mid)

    lo0 = jnp.zeros((N_EXPERTS, 1), jnp.int32)
    hi0 = jnp.full((N_EXPERTS, 1), 0x7F800000, jnp.int32)
    tau, _ = lax.fori_loop(0, 31, search, (lo0, hi0))
    gt = bits > tau
    eq = jnp.where(bits == tau, 1.0, 0.0)
    need = CAP - count(gt)
    sel = jnp.where(gt, 1.0, jnp.where(_prefix_count(eq) < need, eq, 0.0))
    pos = _prefix_count(sel).astype(jnp.int32)
    slot_ref[...] = jnp.where(sel > 0.5, pos, -1)


def _route(aff_t):
    return pl.pallas_call(
        _route_kernel,
        grid=(BATCH,),
        in_specs=[pl.BlockSpec((N_EXPERTS, SEQ), lambda b: (0, b))],
        out_specs=pl.BlockSpec((N_EXPERTS, SEQ), lambda b: (0, b)),
        out_shape=jax.ShapeDtypeStruct((N_EXPERTS, BATCH * SEQ), jnp.int32),
        compiler_params=_params(("arbitrary",), 32),
        name="route",
    )(aff_t)


def _gather_kernel(slot_ref, h_ref, xg_ref):
    e = pl.program_id(1)
    row = slot_ref[pl.ds(e, 1), :]
    c = lax.broadcasted_iota(jnp.int32, (CAP, SEQ), 0)
    onehot = jnp.where(row == c, 1.0, 0.0).astype(BF16)
    xg_ref[0, 0] = jnp.dot(onehot, h_ref[0], preferred_element_type=F32).astype(BF16)


def _gather(slot, h):
    return pl.pallas_call(
        _gather_kernel,
        grid=(BATCH, N_EXPERTS),
        in_specs=[
            pl.BlockSpec((N_EXPERTS, SEQ), lambda b, e: (0, b)),
            pl.BlockSpec((1, SEQ, D_MODEL), lambda b, e: (b, 0, 0)),
        ],
        out_specs=pl.BlockSpec((1, 1, CAP, D_MODEL), lambda b, e: (b, e, 0, 0)),
        out_shape=jax.ShapeDtypeStruct((BATCH, N_EXPERTS, CAP, D_MODEL), BF16),
        compiler_params=_params(("arbitrary", "arbitrary"), 32),
        name="moe_gather",
    )(slot, h)


def _expert_kernel(x_ref, wr_ref, wg_ref, wu_ref, wd_ref, y_ref, acc_ref, gate_ref):
    e = pl.program_id(0)
    f = pl.program_id(1)

    @pl.when(f == 0)
    def _():
        acc_ref[...] = jnp.zeros_like(acc_ref)
        for b in range(BATCH):
            logits = jnp.dot(x_ref[b, 0], wr_ref[...], preferred_element_type=F32)
            z = logits - jnp.max(logits, axis=-1, keepdims=True)
            p = jnp.exp(z)
            aff = p / jnp.sum(p, axis=-1, keepdims=True)
            lane = lax.broadcasted_iota(jnp.int32, aff.shape, 1)
            gate_ref[b] = jnp.sum(jnp.where(lane == e, aff, 0.0), axis=-1, keepdims=True)

    wg = wg_ref[0, 0].astype(BF16)
    wu = wu_ref[0, 0].astype(BF16)
    wd = wd_ref[0, 0].astype(BF16)
    for b in range(BATCH):
        x = x_ref[b, 0]
        a = jnp.dot(x, wg, preferred_element_type=F32)
        u = jnp.dot(x, wu, preferred_element_type=F32)
        act = ((a * jax.nn.sigmoid(a)) * u).astype(BF16)
        acc_ref[b] += jnp.dot(act, wd, preferred_element_type=F32)

    @pl.when(f == EXPERT_FF // FF_TILE - 1)
    def _():
        for b in range(BATCH):
            y_ref[0, b] = (acc_ref[b] * gate_ref[b]).astype(BF16)


def _experts(xg, w_router, w_gate, w_up, w_down, layer):
    return pl.pallas_call(
        _expert_kernel,
        grid=(N_EXPERTS, EXPERT_FF // FF_TILE),
        in_specs=[
            pl.BlockSpec((BATCH, 1, CAP, D_MODEL), lambda e, f: (0, e, 0, 0)),
            pl.BlockSpec((D_MODEL, N_EXPERTS), lambda e, f: (0, 0)),
            pl.BlockSpec((1, 1, D_MODEL, FF_TILE), lambda e, f: (layer, e, 0, f)),
            pl.BlockSpec((1, 1, D_MODEL, FF_TILE), lambda e, f: (layer, e, 0, f)),
            pl.BlockSpec((1, 1, FF_TILE, D_MODEL), lambda e, f: (layer, e, f, 0)),
        ],
        out_specs=pl.BlockSpec((1, BATCH, CAP, D_MODEL), lambda e, f: (e, 0, 0, 0)),
        out_shape=jax.ShapeDtypeStruct((N_EXPERTS, BATCH, CAP, D_MODEL), BF16),
        scratch_shapes=[pltpu.VMEM((BATCH, CAP, D_MODEL), F32), pltpu.VMEM((BATCH, CAP, 1), F32)],
        compiler_params=_params(("arbitrary", "arbitrary"), 56),
        name="moe_experts",
    )(xg, w_router, w_gate, w_up, w_down)


def _combine_kernel(slot_ref, y_ref, x_ref, gate_ref, g_ref, o_ref, *, final):
    lane = lax.broadcasted_iota(jnp.int32, (COMBINE_ROWS, CAP), 1)
    slots = slot_ref[0]
    onehot = jnp.concatenate(
        [jnp.where(slots[:, e:e + 1] == lane, 1.0, 0.0).astype(BF16) for e in range(N_EXPERTS)], axis=1)
    moe = jnp.dot(onehot, y_ref[...].reshape(N_EXPERTS * CAP, D_MODEL), preferred_element_type=F32)
    x = x_ref[0] + gate_ref[0] * moe
    if final:
        x = (x * lax.rsqrt(jnp.mean(x * x, axis=-1, keepdims=True) + RMS_EPS)) * g_ref[...]
    o_ref[0] = x


def _combine(slot_t, y, x, mod, final_g, layer, final):
    return pl.pallas_call(
        functools.partial(_combine_kernel, final=final),
        grid=(BATCH, SEQ // COMBINE_ROWS),
        in_specs=[
            pl.BlockSpec((1, COMBINE_ROWS, N_EXPERTS), lambda b, t: (b, t, 0)),
            pl.BlockSpec((N_EXPERTS, 1, CAP, D_MODEL), lambda b, t: (0, b, 0, 0)),
            pl.BlockSpec((1, COMBINE_ROWS, D_MODEL), lambda b, t: (b, t, 0)),
            pl.BlockSpec((1, 1, D_MODEL), lambda b, t: ((layer * 6 + 5) * BATCH + b, 0, 0)),
            pl.BlockSpec((1, D_MODEL), lambda b, t: (0, 0)),
        ],
        out_specs=pl.BlockSpec((1, COMBINE_ROWS, D_MODEL), lambda b, t: (b, t, 0)),
        out_shape=jax.ShapeDtypeStruct((BATCH, SEQ, D_MODEL), F32),
        compiler_params=_params(("arbitrary", "arbitrary"), 48),
        name="moe_combine",
    )(slot_t, y, x, mod, final_g.reshape(1, D_MODEL))


def _na_qkv_weight(w):
    nq = NA_HEADS * HEAD_DIM
    scale = jnp.concatenate([jnp.full((nq,), HEAD_DIM ** -0.5, F32), jnp.ones((2 * nq,), F32)])
    return (w * scale).astype(BF16)


def _sw_qkv_weight(w):
    nq = SW_Q_HEADS * HEAD_DIM
    nkv = SW_KV_HEADS * HEAD_DIM
    dup = lambda t: jnp.concatenate([t.reshape(D_MODEL, SW_KV_HEADS, 1, HEAD_DIM)] * 2, axis=2).reshape(
        D_MODEL, 2 * nkv)
    wq = w[:, :nq] * HEAD_DIM ** -0.5
    return jnp.concatenate([wq, dup(w[:, nq:nq + nkv]), dup(w[:, nq + nkv:])], axis=1).astype(BF16)


def kernel(x, c, ada_w, ada_b, norm_g, na_w_qkv, na_w_o, na_rpb, sw_w_qkv, sw_w_o, sw_sinks, t5_bias,
           moe_w_router, moe_w_gate, moe_w_up, moe_w_down, final_g):
    mod = _ada(c, ada_w, ada_b)
    norm_g3 = norm_g.reshape(DEPTH * 2, 1, D_MODEL)
    x = x.reshape(BATCH * SEQ, D_MODEL)
    for layer in range(DEPTH):
        j = layer // N_MIXERS
        if layer % N_MIXERS == 0:
            qkv = _qkv(x, norm_g3, mod, _na_qkv_weight(na_w_qkv[j]), layer)
            o = _na_attention(qkv.reshape(BATCH, SEQ, -1), _na_rpb_rows(na_rpb[j]))
            w_o = na_w_o[j]
        else:
            qkv = _qkv(x, norm_g3, mod, _sw_qkv_weight(sw_w_qkv[j]), layer)
            o = _sw_attention(qkv.reshape(BATCH, SEQ, -1), _sw_rel_table(t5_bias), sw_sinks[j])
            w_o = sw_w_o[j]
        w_router = moe_w_router[layer].astype(BF16)
        x, h, aff_t = _post_attn(o.reshape(BATCH * SEQ, -1), w_o.astype(BF16), x, norm_g3, mod,
                                 w_router.T, layer)
        slot = _route(aff_t)
        xg = _gather(slot, h.reshape(BATCH, SEQ, D_MODEL))
        y = _experts(xg, w_router, moe_w_gate, moe_w_up, moe_w_down, layer)
        slot_t = slot.reshape(N_EXPERTS, BATCH, SEQ).transpose(1, 2, 0)
        x = _combine(slot_t, y, x.reshape(BATCH, SEQ, D_MODEL), mod, final_g, layer,
                     final=layer == DEPTH - 1).reshape(BATCH * SEQ, D_MODEL)
    return x.reshape(BATCH, SEQ, D_MODEL)
```

```python
import functools

import numpy as np
import jax
import jax.numpy as jnp
from jax import lax
from jax.experimental import pallas as pl
from jax.experimental.pallas import tpu as pltpu

D_MODEL = 1024
BATCH = 8
SEQ = 2048
DEPTH = 2
GRID_W = 64
ROWS = SEQ // GRID_W
N_MIXERS = 2
HEAD_DIM = 64
NA_HEADS = 16
NA_WIN_H = 8
NA_WIN_W = 16
SW_Q_HEADS = 16
SW_KV_HEADS = 4
SW_GROUP = SW_Q_HEADS // SW_KV_HEADS
SW_WINDOW = 128
SW_BLOCK = 128
SW_NB = SEQ // SW_BLOCK
SW_SPAN = 3 * SW_BLOCK
T5_BUCKETS = 32
T5_MAX_DIST = 128
N_EXPERTS = 16
EXPERT_FF = 2048
EC_CAPACITY = 2
CAP = EC_CAPACITY * SEQ // N_EXPERTS
RMS_EPS = 1e-6
NEG = -1e30

LANES = 128
TOKEN_TILE_ROWS = D_MODEL // LANES
MIB = 1024 * 1024
F32 = jnp.float32
BF16 = jnp.bfloat16

ROW_TILE = 512
FF_TILE = 512
COMBINE_ROWS = 512
PREFIX_CHUNK = 256


def _params(semantics, vmem_mib, flags=None):
    return pltpu.CompilerParams(dimension_semantics=semantics, vmem_limit_bytes=vmem_mib * MIB, flags=flags)


def _norm_mod(x, g, sc, sh):
    y = x * lax.rsqrt(jnp.mean(x * x, axis=-1, keepdims=True) + RMS_EPS)
    return (y * g) * (1.0 + sc) + sh


def _softmax0(z):
    z = z - jnp.max(z, axis=0, keepdims=True)
    p = jnp.exp(z)
    return p / jnp.sum(p, axis=0, keepdims=True)


def _half_masks(rows):
    lane = lax.broadcasted_iota(jnp.int32, (rows, LANES), 1)
    lo = jnp.where(lane < HEAD_DIM, 1.0, 0.0).astype(BF16)
    hi = jnp.where(lane < HEAD_DIM, 0.0, 1.0).astype(BF16)
    return lo, hi


def _ada_kernel(c_ref, w_ref, b_ref, o_ref):
    c = c_ref[...]
    act = (c * jax.nn.sigmoid(c)).astype(BF16)
    o_ref[0] = jnp.dot(act, w_ref[0].astype(BF16), preferred_element_type=F32) + b_ref[0]


def _ada(c, ada_w, ada_b):
    out = pl.pallas_call(
        _ada_kernel,
        grid=(DEPTH, 6),
        in_specs=[
            pl.BlockSpec((BATCH, D_MODEL), lambda l, k: (0, 0)),
            pl.BlockSpec((1, D_MODEL, D_MODEL), lambda l, k: (l, 0, k)),
            pl.BlockSpec((1, 1, D_MODEL), lambda l, k: (l * 6 + k, 0, 0)),
        ],
        out_specs=pl.BlockSpec((1, BATCH, D_MODEL), lambda l, k: (l * 6 + k, 0, 0)),
        out_shape=jax.ShapeDtypeStruct((DEPTH * 6, BATCH, D_MODEL), F32),
        compiler_params=_params(("arbitrary", "arbitrary"), 32),
        name="ada_mod",
    )(c, ada_w, ada_b.reshape(DEPTH * 6, 1, D_MODEL))
    return out.reshape(DEPTH * 6 * BATCH, 1, D_MODEL)


def _mod_spec(layer, chunk):
    tiles_per_seq = SEQ // ROW_TILE
    return pl.BlockSpec((1, 1, D_MODEL),
                        lambda i: ((layer * 6 + chunk) * BATCH + i // tiles_per_seq, 0, 0))


def _qkv_kernel(x_ref, g_ref, sc_ref, sh_ref, w_ref, o_ref):
    h = _norm_mod(x_ref[...], g_ref[0], sc_ref[0], sh_ref[0])
    o_ref[...] = jnp.dot(h.astype(BF16), w_ref[...], preferred_element_type=F32).astype(BF16)


def _qkv(x, norm_g3, mod, w, layer):
    n = w.shape[1]
    return pl.pallas_call(
        _qkv_kernel,
        grid=(BATCH * SEQ // ROW_TILE,),
        in_specs=[
            pl.BlockSpec((ROW_TILE, D_MODEL), lambda i: (i, 0)),
            pl.BlockSpec((1, 1, D_MODEL), lambda i: (layer * 2, 0, 0)),
            _mod_spec(layer, 1),
            _mod_spec(layer, 0),
            pl.BlockSpec((D_MODEL, n), lambda i: (0, 0)),
        ],
        out_specs=pl.BlockSpec((ROW_TILE, n), lambda i: (i, 0)),
        out_shape=jax.ShapeDtypeStruct((BATCH * SEQ, n), BF16),
        compiler_params=_params(("arbitrary",), 48),
        name="norm_qkv",
    )(x, norm_g3, mod, mod, w)


NA_BIAS_ROWS = 2 * NA_WIN_H - 1
NA_ROW_UNROLL = 4


def _na_rpb_rows(rpb):
    w = NA_WIN_W - 1
    pad = jnp.zeros(rpb.shape[:2] + (LANES - 2 * w - 1,), F32)
    return jnp.concatenate([rpb[..., w:].astype(F32), pad, rpb[..., :w].astype(F32)], axis=-1)


def _na_build_bias(w_ref, bias_ref):
    lane = lax.broadcasted_iota(jnp.int32, (GRID_W, LANES), 1)
    col = lax.broadcasted_iota(jnp.int32, (GRID_W, LANES), 0)
    kc = jnp.bitwise_and(lane, GRID_W - 1)
    cstart = jnp.clip(col - NA_WIN_W // 2, 0, GRID_W - NA_WIN_W)
    in_window = jnp.where(kc >= cstart, jnp.where(kc < cstart + NA_WIN_W, 1.0, 0.0), 0.0) > 0.5
    low_half = lane < GRID_W

    def toeplitz(head, a, shift):
        row = jnp.broadcast_to(w_ref[head, a:a + 1, :], (GRID_W, LANES))
        return pltpu.roll(row, shift, 1, stride=1, stride_axis=0)

    for head in range(2):
        tiles = [jnp.where(in_window, jnp.where(low_half, toeplitz(head, a, 0), toeplitz(head, a + 1, GRID_W)), NEG)
                 for a in range(NA_BIAS_ROWS - 1)]
        for d in range(NA_WIN_H):
            for m in range(NA_WIN_H // 2):
                bias_ref[d, head * GRID_W:(head + 1) * GRID_W, m * LANES:(m + 1) * LANES] = (
                    tiles[NA_WIN_H - 1 - d + 2 * m])


def _na_kernel(w_ref, q_ref, k_ref, v_ref, o_ref, bias_ref, p_ref, denom_ref, s0_ref, s1_ref):
    @pl.when(pl.program_id(1) == 0)
    def _():
        _na_build_bias(w_ref, bias_ref)

    lo, hi = _half_masks(GRID_W)
    lo_f = lax.broadcasted_iota(jnp.int32, (GRID_W, LANES), 1) < HEAD_DIM
    span = NA_WIN_H * GRID_W

    def pair_rows(r):
        return pl.ds(pl.multiple_of(r * 2 * GRID_W, 2 * GRID_W), 2 * GRID_W)

    def group_scores(i, s_ref):
        for u in range(NA_ROW_UNROLL):
            r = jnp.minimum(i * NA_ROW_UNROLL + u, ROWS - 1)
            rs = jnp.clip(r - NA_WIN_H // 2, 0, ROWS - NA_WIN_H)
            q = q_ref[0, pl.ds(pl.multiple_of(r * GRID_W, GRID_W), GRID_W), :]
            lhs = jnp.concatenate([q * lo, q * hi], axis=0)
            kw = k_ref[0, pl.ds(pl.multiple_of(rs * GRID_W, GRID_W), span), :]
            s_ref[u] = lax.dot_general(lhs, kw, (((1,), (1,)), ((), ())), preferred_element_type=F32)

    def group_softmax(i, s_ref):
        for u in range(NA_ROW_UNROLL):
            r = i * NA_ROW_UNROLL + u
            rs = jnp.clip(r - NA_WIN_H // 2, 0, ROWS - NA_WIN_H)
            s = s_ref[u] + bias_ref[r - rs]
            m = jnp.broadcast_to(jnp.max(s, axis=-1, keepdims=True), (2 * GRID_W, LANES))
            p = jnp.exp(s - jnp.concatenate([m] * (span // LANES), axis=1))
            p_ref[pair_rows(r), :] = p.astype(BF16)
            denom_ref[pair_rows(r), :] = jnp.broadcast_to(jnp.sum(p, axis=-1, keepdims=True), (2 * GRID_W, LANES))

    def softmax_step(i, carry):
        group_scores(2 * i + 1, s1_ref)
        group_softmax(2 * i, s0_ref)
        group_scores(2 * i + 2, s0_ref)
        group_softmax(2 * i + 1, s1_ref)
        return carry

    def value_step(r, carry):
        rs = jnp.clip(r - NA_WIN_H // 2, 0, ROWS - NA_WIN_H)
        vw = v_ref[0, pl.ds(pl.multiple_of(rs * GRID_W, GRID_W), span), :]
        o = jnp.dot(p_ref[pair_rows(r), :], vw, preferred_element_type=F32) / denom_ref[pair_rows(r), :]
        out = jnp.where(lo_f, o[:GRID_W], o[GRID_W:])
        o_ref[0, pl.ds(pl.multiple_of(r * GRID_W, GRID_W), GRID_W), :] = out.astype(BF16)
        return carry

    group_scores(0, s0_ref)
    lax.fori_loop(0, ROWS // (2 * NA_ROW_UNROLL), softmax_step, 0)
    lax.fori_loop(0, ROWS, value_step, 0, unroll=NA_ROW_UNROLL)


def _na_attention(qkv, rpb_rows):
    pairs = NA_HEADS // 2
    return pl.pallas_call(
        _na_kernel,
        grid=(pairs, BATCH),
        in_specs=[
            pl.BlockSpec((2, NA_BIAS_ROWS, LANES), lambda p, b: (p, 0, 0)),
            pl.BlockSpec((1, SEQ, LANES), lambda p, b: (b, 0, p)),
            pl.BlockSpec((1, SEQ, LANES), lambda p, b: (b, 0, pairs + p)),
            pl.BlockSpec((1, SEQ, LANES), lambda p, b: (b, 0, 2 * pairs + p)),
        ],
        out_specs=pl.BlockSpec((1, SEQ, LANES), lambda p, b: (b, 0, p)),
        out_shape=jax.ShapeDtypeStruct((BATCH, SEQ, NA_HEADS * HEAD_DIM), BF16),
        scratch_shapes=[pltpu.VMEM((NA_WIN_H, 2 * GRID_W, NA_WIN_H * GRID_W), F32),
                        pltpu.VMEM((2 * SEQ, NA_WIN_H * GRID_W), BF16),
                        pltpu.VMEM((2 * SEQ, LANES), F32),
                        pltpu.VMEM((NA_ROW_UNROLL, 2 * GRID_W, NA_WIN_H * GRID_W), F32),
                        pltpu.VMEM((NA_ROW_UNROLL, 2 * GRID_W, NA_WIN_H * GRID_W), F32)],
        compiler_params=_params(("arbitrary", "arbitrary"), 32),
        name="na_attention",
    )(rpb_rows, qkv, qkv, qkv)


def _t5_buckets(rel):
    half = T5_BUCKETS // 2
    max_exact = half // 2
    n = np.abs(rel)
    large = max_exact + (np.log(np.maximum(n, 1) / max_exact)
                         / np.log(T5_MAX_DIST / max_exact) * (half - max_exact)).astype(np.int32)
    large = np.minimum(large, half - 1)
    return (rel > 0).astype(np.int32) * half + np.where(n < max_exact, n, large)


SW_REL_PERIOD = 512
SW_BLOCK_UNROLL = 2


def _sw_rel_table(t5_table):
    k = np.arange(SW_REL_PERIOD)
    rel = np.where(k < SW_REL_PERIOD // 2, k, k - SW_REL_PERIOD)
    ok = np.abs(rel) <= SW_WINDOW
    vals = jnp.where(ok[:, None], t5_table[_t5_buckets(rel)].astype(F32), NEG)
    return vals.T.reshape(SW_KV_HEADS, SW_GROUP, SW_REL_PERIOD)


def _sw_build_bias(rel_ref, bias_ref):
    for g in range(SW_GROUP):
        row = jnp.broadcast_to(rel_ref[0, g:g + 1, :], (SW_BLOCK, SW_REL_PERIOD))
        t = pltpu.roll(row, 0, 1, stride=1, stride_axis=0)
        rows = slice(g * SW_BLOCK, (g + 1) * SW_BLOCK)
        bias_ref[0, rows, :] = t[:, :SW_SPAN]
        bias_ref[1, rows, :] = jnp.concatenate([t[:, SW_SPAN:], t[:, :2 * SW_BLOCK]], axis=1)
        bias_ref[2, rows, :] = jnp.concatenate([t[:, 2 * SW_BLOCK:], t[:, :SW_BLOCK]], axis=1)


def _sw_kernel(sink_ref, rel_ref, q_ref, k_ref, v_ref, o_ref, bias_ref, p_ref, denom_ref):
    j = pl.program_id(0)

    @pl.when(pl.program_id(1) == 0)
    def _():
        _sw_build_bias(rel_ref, bias_ref)

    masks = _half_masks(SW_BLOCK)
    lo_f = lax.broadcasted_iota(jnp.int32, (SW_BLOCK, LANES), 1) < HEAD_DIM
    sinks = [jnp.full((SW_BLOCK, LANES), sink_ref[j * SW_GROUP + g], F32) for g in range(SW_GROUP)]

    def chain_rows(n, g):
        return pl.ds(pl.multiple_of((n * SW_GROUP + g) * SW_BLOCK, SW_BLOCK), SW_BLOCK)

    def softmax_step(n, carry):
        first = jnp.clip(n - 1, 0, SW_NB - 3)
        rows = pl.ds(pl.multiple_of(n * SW_BLOCK, SW_BLOCK), SW_BLOCK)
        kw = k_ref[0, pl.ds(pl.multiple_of(first * SW_BLOCK, SW_BLOCK), SW_SPAN), :]
        for g in range(SW_GROUP):
            q = q_ref[0, rows, (g // 2) * LANES:(g // 2 + 1) * LANES] * masks[g % 2]
            s = lax.dot_general(q, kw, (((1,), (1,)), ((), ())), preferred_element_type=F32)
            s = s + bias_ref[n - first, g * SW_BLOCK:(g + 1) * SW_BLOCK, :]
            m = jnp.maximum(jnp.broadcast_to(jnp.max(s, axis=-1, keepdims=True), (SW_BLOCK, LANES)), sinks[g])
            p = jnp.exp(s - jnp.concatenate([m] * (SW_SPAN // LANES), axis=1))
            p_ref[chain_rows(n, g), :] = p.astype(BF16)
            denom_ref[chain_rows(n, g), :] = (
                jnp.broadcast_to(jnp.sum(p, axis=-1, keepdims=True), (SW_BLOCK, LANES)) + jnp.exp(sinks[g] - m))
        return carry

    def value_step(n, carry):
        first = jnp.clip(n - 1, 0, SW_NB - 3)
        vw = v_ref[0, pl.ds(pl.multiple_of(first * SW_BLOCK, SW_BLOCK), SW_SPAN), :]
        heads = [jnp.dot(p_ref[chain_rows(n, g), :], vw, preferred_element_type=F32) / denom_ref[chain_rows(n, g), :]
                 for g in range(SW_GROUP)]
        out = jnp.concatenate([jnp.where(lo_f, heads[0], heads[1]), jnp.where(lo_f, heads[2], heads[3])], axis=1)
        o_ref[0, pl.ds(pl.multiple_of(n * SW_BLOCK, SW_BLOCK), SW_BLOCK), :] = out.astype(BF16)
        return carry

    lax.fori_loop(0, SW_NB, softmax_step, 0, unroll=SW_BLOCK_UNROLL)
    lax.fori_loop(0, SW_NB, value_step, 0, unroll=SW_BLOCK_UNROLL)


def _sw_attention(qkv, rel_table, sinks):
    qw = SW_GROUP * HEAD_DIM
    k_off = SW_Q_HEADS * HEAD_DIM // LANES
    v_off = k_off + SW_KV_HEADS
    return pl.pallas_call(
        _sw_kernel,
        grid=(SW_KV_HEADS, BATCH),
        in_specs=[
            pl.BlockSpec(memory_space=pltpu.SMEM),
            pl.BlockSpec((1, SW_GROUP, SW_REL_PERIOD), lambda j, b: (j, 0, 0)),
            pl.BlockSpec((1, SEQ, qw), lambda j, b: (b, 0, j)),
            pl.BlockSpec((1, SEQ, LANES), lambda j, b: (b, 0, k_off + j)),
            pl.BlockSpec((1, SEQ, LANES), lambda j, b: (b, 0, v_off + j)),
        ],
        out_specs=pl.BlockSpec((1, SEQ, qw), lambda j, b: (b, 0, j)),
        out_shape=jax.ShapeDtypeStruct((BATCH, SEQ, SW_Q_HEADS * HEAD_DIM), BF16),
        scratch_shapes=[pltpu.VMEM((3, SW_GROUP * SW_BLOCK, SW_SPAN), F32),
                        pltpu.VMEM((SW_GROUP * SEQ, SW_SPAN), BF16),
                        pltpu.VMEM((SW_GROUP * SEQ, LANES), F32)],
        compiler_params=_params(("arbitrary", "arbitrary"), 32),
        name="sw_attention",
    )(sinks, rel_table, qkv, qkv, qkv)


def _post_attn_kernel(o_ref, wo_ref, x_ref, gate_ref, g_ref, sc_ref, sh_ref, wr_ref,
                      xo_ref, h_ref, aff_ref):
    y = jnp.dot(o_ref[...], wo_ref[...], preferred_element_type=F32)
    xn = x_ref[...] + gate_ref[0] * y
    xo_ref[...] = xn
    h = _norm_mod(xn, g_ref[0], sc_ref[0], sh_ref[0])
    for j in range(TOKEN_TILE_ROWS):
        h_ref[pl.ds(j, ROW_TILE, stride=TOKEN_TILE_ROWS), :] = h[:, j * LANES:(j + 1) * LANES]
    hb = h.astype(BF16)
    logits = lax.dot_general(wr_ref[...], hb, (((1,), (1,)), ((), ())), preferred_element_type=F32)
    aff_ref[...] = _softmax0(logits)


def _post_attn(o, w_o, x, norm_g3, mod, w_router_t, layer):
    rows = BATCH * SEQ
    return pl.pallas_call(
        _post_attn_kernel,
        grid=(rows // ROW_TILE,),
        in_specs=[
            pl.BlockSpec((ROW_TILE, D_MODEL), lambda i: (i, 0)),
            pl.BlockSpec((D_MODEL, D_MODEL), lambda i: (0, 0)),
            pl.BlockSpec((ROW_TILE, D_MODEL), lambda i: (i, 0)),
            _mod_spec(layer, 2),
            pl.BlockSpec((1, 1, D_MODEL), lambda i: (layer * 2 + 1, 0, 0)),
            _mod_spec(layer, 4),
            _mod_spec(layer, 3),
            pl.BlockSpec((N_EXPERTS, D_MODEL), lambda i: (0, 0)),
        ],
        out_specs=[
            pl.BlockSpec((ROW_TILE, D_MODEL), lambda i: (i, 0)),
            pl.BlockSpec((ROW_TILE * TOKEN_TILE_ROWS, LANES), lambda i: (i, 0)),
            pl.BlockSpec((N_EXPERTS, ROW_TILE), lambda i: (0, i)),
        ],
        out_shape=[
            jax.ShapeDtypeStruct((rows, D_MODEL), F32),
            jax.ShapeDtypeStruct((rows * TOKEN_TILE_ROWS, LANES), F32),
            jax.ShapeDtypeStruct((N_EXPERTS, rows), F32),
        ],
        compiler_params=_params(("arbitrary",), 48),
        name="post_attn",
    )(o, w_o, x, mod, norm_g3, mod, mod, w_router_t)


def _prefix_count(x):
    nchunk = SEQ // PREFIX_CHUNK
    r = lax.broadcasted_iota(jnp.int32, (PREFIX_CHUNK, PREFIX_CHUNK), 0)
    c = lax.broadcasted_iota(jnp.int32, (PREFIX_CHUNK, PREFIX_CHUNK), 1)
    upper = jnp.where(r < c, 1.0, 0.0).astype(BF16)
    chunks = [x[:, k * PREFIX_CHUNK:(k + 1) * PREFIX_CHUNK] for k in range(nchunk)]
    local = jnp.dot(jnp.concatenate(chunks, axis=0).astype(BF16), upper, preferred_element_type=F32)
    out = []
    offset = jnp.zeros((N_EXPERTS, 1), F32)
    for k in range(nchunk):
        out.append(local[k * N_EXPERTS:(k + 1) * N_EXPERTS] + offset)
        offset = offset + jnp.sum(chunks[k], axis=1, keepdims=True)
    return jnp.concatenate(out, axis=1)


def _route_kernel(aff_ref, slot_ref, tok_ref):
    bits = pltpu.bitcast(aff_ref[...], jnp.int32)

    def count(mask):
        return jnp.sum(jnp.where(mask, 1.0, 0.0), axis=1, keepdims=True)

    def search(_, bounds):
        lo, hi = bounds
        mid = lo + ((hi - lo) >> 1)
        ge = count(bits >= mid) >= CAP
        return jnp.where(ge, mid, lo), jnp.where(ge, hi, mid)

    lo0 = jnp.zeros((N_EXPERTS, 1), jnp.int32)
    hi0 = jnp.full((N_EXPERTS, 1), 0x7F800000, jnp.int32)
    tau, _ = lax.fori_loop(0, 31, search, (lo0, hi0))
    gt = bits > tau
    eq = jnp.where(bits == tau, 1.0, 0.0)
    need = CAP - count(gt)
    sel = jnp.where(gt, 1.0, jnp.where(_prefix_count(eq) < need, eq, 0.0))
    pos = _prefix_count(sel).astype(jnp.int32)
    slot = jnp.where(sel > 0.5, pos, -1)
    slot_ref[...] = slot
    cap_id = lax.broadcasted_iota(jnp.int32, (CAP, SEQ), 0)
    token = lax.broadcasted_iota(jnp.int32, (CAP, SEQ), 1).astype(F32)
    cols = [jnp.sum(jnp.where(slot[e:e + 1, :] == cap_id, token, 0.0), axis=1, keepdims=True)
            for e in range(N_EXPERTS)]
    tok_ref[0] = jnp.concatenate(cols, axis=1).astype(jnp.int32)


def _route(aff_t):
    return pl.pallas_call(
        _route_kernel,
        grid=(BATCH,),
        in_specs=[pl.BlockSpec((N_EXPERTS, SEQ), lambda b: (0, b))],
        out_specs=[pl.BlockSpec((N_EXPERTS, SEQ), lambda b: (0, b)),
                   pl.BlockSpec((1, CAP, N_EXPERTS), lambda b: (b, 0, 0))],
        out_shape=[jax.ShapeDtypeStruct((N_EXPERTS, BATCH * SEQ), jnp.int32),
                   jax.ShapeDtypeStruct((BATCH, CAP, N_EXPERTS), jnp.int32)],
        compiler_params=_params(("arbitrary",), 32),
        name="route",
    )(aff_t)


GATHER_ROWS = BATCH * CAP
FF_STEPS = EXPERT_FF // FF_TILE
ROWS_PER_CHUNK = GATHER_ROWS // (FF_STEPS * BATCH)
PROLOGUE_UNROLL = 8


def _expert_kernel(tok_ref, h_ref, wr_ref, wg_ref, wu_ref, wd_ref, y_ref, x_ref, sem, acc_ref, gate_ref):
    e = pl.program_id(0)
    f = pl.program_id(1)
    slot = lax.rem(e, 2)
    next_slot = 1 - slot
    next_e = jnp.minimum(e + 1, N_EXPERTS - 1)

    tile = TOKEN_TILE_ROWS

    def row_copy(expert, row, dst_slot):
        src = pl.multiple_of(tok_ref[expert * GATHER_ROWS + row] * tile, tile)
        dst = pl.multiple_of(row * tile, tile)
        return pltpu.make_async_copy(h_ref.at[pl.ds(src, tile), :], x_ref.at[dst_slot, pl.ds(dst, tile), :],
                                     sem.at[dst_slot])

    def wait_rows(dst_slot):
        pltpu.make_async_copy(h_ref.at[pl.ds(0, GATHER_ROWS * tile), :], x_ref.at[dst_slot],
                              sem.at[dst_slot]).wait()

    def chunk_rows(b):
        cols = [x_ref[slot, pl.ds(b * CAP * tile + j, CAP, stride=tile), :] for j in range(tile)]
        return jnp.concatenate(cols, axis=1).astype(BF16)

    @pl.when(jnp.logical_and(e == 0, f == 0))
    def _():
        def issue(i, carry):
            for u in range(PROLOGUE_UNROLL):
                row_copy(0, i * PROLOGUE_UNROLL + u, 0).start()
            return carry
        lax.fori_loop(0, GATHER_ROWS // PROLOGUE_UNROLL, issue, 0)

    @pl.when(f == 0)
    def _():
        wait_rows(slot)
        acc_ref[...] = jnp.zeros_like(acc_ref)
        for b in range(BATCH):
            x = chunk_rows(b)
            logits = jnp.dot(x, wr_ref[...], preferred_element_type=F32)
            z = logits - jnp.max(logits, axis=-1, keepdims=True)
            p = jnp.exp(z)
            aff = p / jnp.sum(p, axis=-1, keepdims=True)
            lane = lax.broadcasted_iota(jnp.int32, aff.shape, 1)
            gate_ref[b] = jnp.sum(jnp.where(lane == e, aff, 0.0), axis=-1, keepdims=True)

    wg = wg_ref[0, 0].astype(BF16)
    wu = wu_ref[0, 0].astype(BF16)
    wd = wd_ref[0, 0].astype(BF16)
    for b in range(BATCH):
        first = (f * BATCH + b) * ROWS_PER_CHUNK
        for j in range(ROWS_PER_CHUNK):
            row_copy(next_e, first + j, next_slot).start()
        x = chunk_rows(b)
        a = jnp.dot(x, wg, preferred_element_type=F32)
        u = jnp.dot(x, wu, preferred_element_type=F32)
        act = ((a * jax.nn.sigmoid(a)) * u).astype(BF16)
        acc_ref[b] += jnp.dot(act, wd, preferred_element_type=F32)

    @pl.when(f == FF_STEPS - 1)
    def _():
        for b in range(BATCH):
            y_ref[0, b] = (acc_ref[b] * gate_ref[b]).astype(BF16)

    @pl.when(jnp.logical_and(e == N_EXPERTS - 1, f == FF_STEPS - 1))
    def _():
        wait_rows(next_slot)


def _experts(tok, h, w_router, w_gate, w_up, w_down, layer):
    grid_spec = pltpu.PrefetchScalarGridSpec(
        num_scalar_prefetch=1,
        grid=(N_EXPERTS, FF_STEPS),
        in_specs=[
            pl.BlockSpec(memory_space=pl.ANY),
            pl.BlockSpec((D_MODEL, N_EXPERTS), lambda e, f, tok: (0, 0)),
            pl.BlockSpec((1, 1, D_MODEL, FF_TILE), lambda e, f, tok: (layer, e, 0, f)),
            pl.BlockSpec((1, 1, D_MODEL, FF_TILE), lambda e, f, tok: (layer, e, 0, f)),
            pl.BlockSpec((1, 1, FF_TILE, D_MODEL), lambda e, f, tok: (layer, e, f, 0)),
        ],
        out_specs=pl.BlockSpec((1, BATCH, CAP, D_MODEL), lambda e, f, tok: (e, 0, 0, 0)),
        scratch_shapes=[pltpu.VMEM((2, GATHER_ROWS * TOKEN_TILE_ROWS, LANES), F32), pltpu.SemaphoreType.DMA((2,)),
                        pltpu.VMEM((BATCH, CAP, D_MODEL), F32), pltpu.VMEM((BATCH, CAP, 1), F32)],
    )
    return pl.pallas_call(
        _expert_kernel,
        grid_spec=grid_spec,
        out_shape=jax.ShapeDtypeStruct((N_EXPERTS, BATCH, CAP, D_MODEL), BF16),
        compiler_params=_params(("arbitrary", "arbitrary"), 60),
        name="moe_experts",
    )(tok, h, w_router, w_gate, w_up, w_down)


def _combine_kernel(slot_ref, y_ref, x_ref, gate_ref, g_ref, o_ref, *, final):
    lane = lax.broadcasted_iota(jnp.int32, (COMBINE_ROWS, CAP), 1)
    slots = slot_ref[0]
    onehot = jnp.concatenate(
        [jnp.where(slots[:, e:e + 1] == lane, 1.0, 0.0).astype(BF16) for e in range(N_EXPERTS)], axis=1)
    moe = jnp.dot(onehot, y_ref[...].reshape(N_EXPERTS * CAP, D_MODEL), preferred_element_type=F32)
    x = x_ref[0] + gate_ref[0] * moe
    if final:
        x = (x * lax.rsqrt(jnp.mean(x * x, axis=-1, keepdims=True) + RMS_EPS)) * g_ref[...]
    o_ref[0] = x


def _combine(slot_t, y, x, mod, final_g, layer, final):
    return pl.pallas_call(
        functools.partial(_combine_kernel, final=final),
        grid=(BATCH, SEQ // COMBINE_ROWS),
        in_specs=[
            pl.BlockSpec((1, COMBINE_ROWS, N_EXPERTS), lambda b, t: (b, t, 0)),
            pl.BlockSpec((N_EXPERTS, 1, CAP, D_MODEL), lambda b, t: (0, b, 0, 0)),
            pl.BlockSpec((1, COMBINE_ROWS, D_MODEL), lambda b, t: (b, t, 0)),
            pl.BlockSpec((1, 1, D_MODEL), lambda b, t: ((layer * 6 + 5) * BATCH + b, 0, 0)),
            pl.BlockSpec((1, D_MODEL), lambda b, t: (0, 0)),
        ],
        out_specs=pl.BlockSpec((1, COMBINE_ROWS, D_MODEL), lambda b, t: (b, t, 0)),
        out_shape=jax.ShapeDtypeStruct((BATCH, SEQ, D_MODEL), F32),
        compiler_params=_params(("arbitrary", "arbitrary"), 48),
        name="moe_combine",
    )(slot_t, y, x, mod, final_g.reshape(1, D_MODEL))


def _na_qkv_weight(w):
    nq = NA_HEADS * HEAD_DIM
    scale = jnp.concatenate([jnp.full((nq,), HEAD_DIM ** -0.5, F32), jnp.ones((2 * nq,), F32)])
    return (w * scale).astype(BF16)


def _sw_qkv_weight(w):
    nq = SW_Q_HEADS * HEAD_DIM
    nkv = SW_KV_HEADS * HEAD_DIM
    dup = lambda t: jnp.concatenate([t.reshape(D_MODEL, SW_KV_HEADS, 1, HEAD_DIM)] * 2, axis=2).reshape(
        D_MODEL, 2 * nkv)
    wq = w[:, :nq] * HEAD_DIM ** -0.5
    return jnp.concatenate([wq, dup(w[:, nq:nq + nkv]), dup(w[:, nq + nkv:])], axis=1).astype(BF16)


def kernel(x, c, ada_w, ada_b, norm_g, na_w_qkv, na_w_o, na_rpb, sw_w_qkv, sw_w_o, sw_sinks, t5_bias,
           moe_w_router, moe_w_gate, moe_w_up, moe_w_down, final_g):
    mod = _ada(c, ada_w, ada_b)
    norm_g3 = norm_g.reshape(DEPTH * 2, 1, D_MODEL)
    x = x.reshape(BATCH * SEQ, D_MODEL)
    for layer in range(DEPTH):
        j = layer // N_MIXERS
        if layer % N_MIXERS == 0:
            qkv = _qkv(x, norm_g3, mod, _na_qkv_weight(na_w_qkv[j]), layer)
            o = _na_attention(qkv.reshape(BATCH, SEQ, -1), _na_rpb_rows(na_rpb[j]))
            w_o = na_w_o[j]
        else:
            qkv = _qkv(x, norm_g3, mod, _sw_qkv_weight(sw_w_qkv[j]), layer)
            o = _sw_attention(qkv.reshape(BATCH, SEQ, -1), _sw_rel_table(t5_bias), sw_sinks[j])
            w_o = sw_w_o[j]
        w_router = moe_w_router[layer].astype(BF16)
        x, h, aff_t = _post_attn(o.reshape(BATCH * SEQ, -1), w_o.astype(BF16), x, norm_g3, mod,
                                 w_router.T, layer)
        slot, tok = _route(aff_t)
        rows = tok.transpose(2, 0, 1) + (jnp.arange(BATCH, dtype=jnp.int32) * SEQ)[None, :, None]
        y = _experts(rows.reshape(-1), h, w_router, moe_w_gate, moe_w_up, moe_w_down, layer)
        slot_t = slot.reshape(N_EXPERTS, BATCH, SEQ).transpose(1, 2, 0)
        x = _combine(slot_t, y, x.reshape(BATCH, SEQ, D_MODEL), mod, final_g, layer,
                     final=layer == DEPTH - 1).reshape(BATCH * SEQ, D_MODEL)
    return x.reshape(BATCH, SEQ, D_MODEL)
```

```python
import functools

import numpy as np
import jax
import jax.numpy as jnp
from jax import lax
from jax.experimental import pallas as pl
from jax.experimental.pallas import tpu as pltpu

D_MODEL = 1024
BATCH = 8
SEQ = 2048
DEPTH = 2
GRID_W = 64
ROWS = SEQ // GRID_W
N_MIXERS = 2
HEAD_DIM = 64
NA_HEADS = 16
NA_WIN_H = 8
NA_WIN_W = 16
SW_Q_HEADS = 16
SW_KV_HEADS = 4
SW_GROUP = SW_Q_HEADS // SW_KV_HEADS
SW_WINDOW = 128
SW_BLOCK = 128
SW_NB = SEQ // SW_BLOCK
SW_SPAN = 3 * SW_BLOCK
T5_BUCKETS = 32
T5_MAX_DIST = 128
N_EXPERTS = 16
EXPERT_FF = 2048
EC_CAPACITY = 2
CAP = EC_CAPACITY * SEQ // N_EXPERTS
RMS_EPS = 1e-6
NEG = -1e30

LANES = 128
TOKEN_TILE_ROWS = D_MODEL // LANES
MIB = 1024 * 1024
F32 = jnp.float32
BF16 = jnp.bfloat16

ROW_TILE = 512
FF_TILE = 512
COMBINE_ROWS = 512
PREFIX_CHUNK = 256


def _params(semantics, vmem_mib, flags=None):
    return pltpu.CompilerParams(dimension_semantics=semantics, vmem_limit_bytes=vmem_mib * MIB, flags=flags)


def _norm_mod(x, g, sc, sh):
    y = x * lax.rsqrt(jnp.mean(x * x, axis=-1, keepdims=True) + RMS_EPS)
    return (y * g) * (1.0 + sc) + sh


def _softmax0(z):
    z = z - jnp.max(z, axis=0, keepdims=True)
    p = jnp.exp(z)
    return p / jnp.sum(p, axis=0, keepdims=True)


def _half_masks(rows):
    lane = lax.broadcasted_iota(jnp.int32, (rows, LANES), 1)
    lo = jnp.where(lane < HEAD_DIM, 1.0, 0.0).astype(BF16)
    hi = jnp.where(lane < HEAD_DIM, 0.0, 1.0).astype(BF16)
    return lo, hi


def _ada_kernel(c_ref, w_ref, b_ref, o_ref):
    c = c_ref[...]
    act = (c * jax.nn.sigmoid(c)).astype(BF16)
    o_ref[0] = jnp.dot(act, w_ref[0].astype(BF16), preferred_element_type=F32) + b_ref[0]


def _ada(c, ada_w, ada_b):
    out = pl.pallas_call(
        _ada_kernel,
        grid=(DEPTH, 6),
        in_specs=[
            pl.BlockSpec((BATCH, D_MODEL), lambda l, k: (0, 0)),
            pl.BlockSpec((1, D_MODEL, D_MODEL), lambda l, k: (l, 0, k)),
            pl.BlockSpec((1, 1, D_MODEL), lambda l, k: (l * 6 + k, 0, 0)),
        ],
        out_specs=pl.BlockSpec((1, BATCH, D_MODEL), lambda l, k: (l * 6 + k, 0, 0)),
        out_shape=jax.ShapeDtypeStruct((DEPTH * 6, BATCH, D_MODEL), F32),
        compiler_params=_params(("arbitrary", "arbitrary"), 32),
        name="ada_mod",
    )(c, ada_w, ada_b.reshape(DEPTH * 6, 1, D_MODEL))
    return out.reshape(DEPTH * 6 * BATCH, 1, D_MODEL)


def _mod_spec(layer, chunk):
    tiles_per_seq = SEQ // ROW_TILE
    return pl.BlockSpec((1, 1, D_MODEL),
                        lambda i: ((layer * 6 + chunk) * BATCH + i // tiles_per_seq, 0, 0))


def _qkv_kernel(x_ref, g_ref, sc_ref, sh_ref, w_ref, o_ref):
    h = _norm_mod(x_ref[...], g_ref[0], sc_ref[0], sh_ref[0])
    o_ref[...] = jnp.dot(h.astype(BF16), w_ref[...], preferred_element_type=F32).astype(BF16)


def _qkv(x, norm_g3, mod, w, layer):
    n = w.shape[1]
    return pl.pallas_call(
        _qkv_kernel,
        grid=(BATCH * SEQ // ROW_TILE,),
        in_specs=[
            pl.BlockSpec((ROW_TILE, D_MODEL), lambda i: (i, 0)),
            pl.BlockSpec((1, 1, D_MODEL), lambda i: (layer * 2, 0, 0)),
            _mod_spec(layer, 1),
            _mod_spec(layer, 0),
            pl.BlockSpec((D_MODEL, n), lambda i: (0, 0)),
        ],
        out_specs=pl.BlockSpec((ROW_TILE, n), lambda i: (i, 0)),
        out_shape=jax.ShapeDtypeStruct((BATCH * SEQ, n), BF16),
        compiler_params=_params(("arbitrary",), 48),
        name="norm_qkv",
    )(x, norm_g3, mod, mod, w)


NA_BIAS_ROWS = 2 * NA_WIN_H - 1
NA_ROW_UNROLL = 4


def _na_rpb_rows(rpb):
    w = NA_WIN_W - 1
    pad = jnp.zeros(rpb.shape[:2] + (LANES - 2 * w - 1,), F32)
    return jnp.concatenate([rpb[..., w:].astype(F32), pad, rpb[..., :w].astype(F32)], axis=-1)


def _na_build_bias(w_ref, bias_ref):
    lane = lax.broadcasted_iota(jnp.int32, (GRID_W, LANES), 1)
    col = lax.broadcasted_iota(jnp.int32, (GRID_W, LANES), 0)
    kc = jnp.bitwise_and(lane, GRID_W - 1)
    cstart = jnp.clip(col - NA_WIN_W // 2, 0, GRID_W - NA_WIN_W)
    in_window = jnp.where(kc >= cstart, jnp.where(kc < cstart + NA_WIN_W, 1.0, 0.0), 0.0) > 0.5
    low_half = lane < GRID_W

    def toeplitz(head, a, shift):
        row = jnp.broadcast_to(w_ref[head, a:a + 1, :], (GRID_W, LANES))
        return pltpu.roll(row, shift, 1, stride=1, stride_axis=0)

    for head in range(2):
        tiles = [jnp.where(in_window, jnp.where(low_half, toeplitz(head, a, 0), toeplitz(head, a + 1, GRID_W)), NEG)
                 for a in range(NA_BIAS_ROWS - 1)]
        for d in range(NA_WIN_H):
            for m in range(NA_WIN_H // 2):
                bias_ref[d, head * GRID_W:(head + 1) * GRID_W, m * LANES:(m + 1) * LANES] = (
                    tiles[NA_WIN_H - 1 - d + 2 * m])


def _na_kernel(w_ref, q_ref, k_ref, v_ref, o_ref, bias_ref, p_ref, denom_ref, s0_ref, s1_ref):
    @pl.when(pl.program_id(1) == 0)
    def _():
        _na_build_bias(w_ref, bias_ref)

    lo, hi = _half_masks(GRID_W)
    lo_f = lax.broadcasted_iota(jnp.int32, (GRID_W, LANES), 1) < HEAD_DIM
    span = NA_WIN_H * GRID_W

    def pair_rows(r):
        return pl.ds(pl.multiple_of(r * 2 * GRID_W, 2 * GRID_W), 2 * GRID_W)

    def group_scores(i, s_ref):
        for u in range(NA_ROW_UNROLL):
            r = jnp.minimum(i * NA_ROW_UNROLL + u, ROWS - 1)
            rs = jnp.clip(r - NA_WIN_H // 2, 0, ROWS - NA_WIN_H)
            q = q_ref[0, pl.ds(pl.multiple_of(r * GRID_W, GRID_W), GRID_W), :]
            lhs = jnp.concatenate([q * lo, q * hi], axis=0)
            kw = k_ref[0, pl.ds(pl.multiple_of(rs * GRID_W, GRID_W), span), :]
            s_ref[u] = lax.dot_general(lhs, kw, (((1,), (1,)), ((), ())), preferred_element_type=F32)

    def group_softmax(i, s_ref):
        for u in range(NA_ROW_UNROLL):
            r = i * NA_ROW_UNROLL + u
            rs = jnp.clip(r - NA_WIN_H // 2, 0, ROWS - NA_WIN_H)
            s = s_ref[u] + bias_ref[r - rs]
            m = jnp.broadcast_to(jnp.max(s, axis=-1, keepdims=True), (2 * GRID_W, LANES))
            p = jnp.exp(s - jnp.concatenate([m] * (span // LANES), axis=1))
            p_ref[pair_rows(r), :] = p.astype(BF16)
            denom_ref[pair_rows(r), :] = jnp.broadcast_to(jnp.sum(p, axis=-1, keepdims=True), (2 * GRID_W, LANES))

    def softmax_step(i, carry):
        group_scores(2 * i + 1, s1_ref)
        group_softmax(2 * i, s0_ref)
        group_scores(2 * i + 2, s0_ref)
        group_softmax(2 * i + 1, s1_ref)
        return carry

    def value_step(r, carry):
        rs = jnp.clip(r - NA_WIN_H // 2, 0, ROWS - NA_WIN_H)
        vw = v_ref[0, pl.ds(pl.multiple_of(rs * GRID_W, GRID_W), span), :]
        o = jnp.dot(p_ref[pair_rows(r), :], vw, preferred_element_type=F32) / denom_ref[pair_rows(r), :]
        out = jnp.where(lo_f, o[:GRID_W], o[GRID_W:])
        o_ref[0, pl.ds(pl.multiple_of(r * GRID_W, GRID_W), GRID_W), :] = out.astype(BF16)
        return carry

    group_scores(0, s0_ref)
    lax.fori_loop(0, ROWS // (2 * NA_ROW_UNROLL), softmax_step, 0)
    lax.fori_loop(0, ROWS, value_step, 0, unroll=NA_ROW_UNROLL)


def _na_attention(qkv, rpb_rows):
    pairs = NA_HEADS // 2
    return pl.pallas_call(
        _na_kernel,
        grid=(pairs, BATCH),
        in_specs=[
            pl.BlockSpec((2, NA_BIAS_ROWS, LANES), lambda p, b: (p, 0, 0)),
            pl.BlockSpec((1, SEQ, LANES), lambda p, b: (b, 0, p)),
            pl.BlockSpec((1, SEQ, LANES), lambda p, b: (b, 0, pairs + p)),
            pl.BlockSpec((1, SEQ, LANES), lambda p, b: (b, 0, 2 * pairs + p)),
        ],
        out_specs=pl.BlockSpec((1, SEQ, LANES), lambda p, b: (b, 0, p)),
        out_shape=jax.ShapeDtypeStruct((BATCH, SEQ, NA_HEADS * HEAD_DIM), BF16),
        scratch_shapes=[pltpu.VMEM((NA_WIN_H, 2 * GRID_W, NA_WIN_H * GRID_W), F32),
                        pltpu.VMEM((2 * SEQ, NA_WIN_H * GRID_W), BF16),
                        pltpu.VMEM((2 * SEQ, LANES), F32),
                        pltpu.VMEM((NA_ROW_UNROLL, 2 * GRID_W, NA_WIN_H * GRID_W), F32),
                        pltpu.VMEM((NA_ROW_UNROLL, 2 * GRID_W, NA_WIN_H * GRID_W), F32)],
        compiler_params=_params(("arbitrary", "arbitrary"), 32),
        name="na_attention",
    )(rpb_rows, qkv, qkv, qkv)


def _t5_buckets(rel):
    half = T5_BUCKETS // 2
    max_exact = half // 2
    n = np.abs(rel)
    large = max_exact + (np.log(np.maximum(n, 1) / max_exact)
                         / np.log(T5_MAX_DIST / max_exact) * (half - max_exact)).astype(np.int32)
    large = np.minimum(large, half - 1)
    return (rel > 0).astype(np.int32) * half + np.where(n < max_exact, n, large)


SW_REL_PERIOD = 512
SW_BLOCK_UNROLL = 2


def _sw_rel_table(t5_table):
    k = np.arange(SW_REL_PERIOD)
    rel = np.where(k < SW_REL_PERIOD // 2, k, k - SW_REL_PERIOD)
    ok = np.abs(rel) <= SW_WINDOW
    vals = jnp.where(ok[:, None], t5_table[_t5_buckets(rel)].astype(F32), NEG)
    return vals.T.reshape(SW_KV_HEADS, SW_GROUP, SW_REL_PERIOD)


def _sw_build_bias(rel_ref, bias_ref):
    for g in range(SW_GROUP):
        row = jnp.broadcast_to(rel_ref[0, g:g + 1, :], (SW_BLOCK, SW_REL_PERIOD))
        t = pltpu.roll(row, 0, 1, stride=1, stride_axis=0)
        rows = slice(g * SW_BLOCK, (g + 1) * SW_BLOCK)
        bias_ref[0, rows, :] = t[:, :SW_SPAN]
        bias_ref[1, rows, :] = jnp.concatenate([t[:, SW_SPAN:], t[:, :2 * SW_BLOCK]], axis=1)
        bias_ref[2, rows, :] = jnp.concatenate([t[:, 2 * SW_BLOCK:], t[:, :SW_BLOCK]], axis=1)


def _sw_kernel(sink_ref, rel_ref, q_ref, k_ref, v_ref, o_ref, bias_ref, p_ref, denom_ref):
    j = pl.program_id(0)

    @pl.when(pl.program_id(1) == 0)
    def _():
        _sw_build_bias(rel_ref, bias_ref)

    masks = _half_masks(SW_BLOCK)
    lo_f = lax.broadcasted_iota(jnp.int32, (SW_BLOCK, LANES), 1) < HEAD_DIM
    sinks = [jnp.full((SW_BLOCK, LANES), sink_ref[j * SW_GROUP + g], F32) for g in range(SW_GROUP)]

    def chain_rows(n, g):
        return pl.ds(pl.multiple_of((n * SW_GROUP + g) * SW_BLOCK, SW_BLOCK), SW_BLOCK)

    def softmax_step(n, carry):
        first = jnp.clip(n - 1, 0, SW_NB - 3)
        rows = pl.ds(pl.multiple_of(n * SW_BLOCK, SW_BLOCK), SW_BLOCK)
        kw = k_ref[0, pl.ds(pl.multiple_of(first * SW_BLOCK, SW_BLOCK), SW_SPAN), :]
        for g in range(SW_GROUP):
            q = q_ref[0, rows, (g // 2) * LANES:(g // 2 + 1) * LANES] * masks[g % 2]
            s = lax.dot_general(q, kw, (((1,), (1,)), ((), ())), preferred_element_type=F32)
            s = s + bias_ref[n - first, g * SW_BLOCK:(g + 1) * SW_BLOCK, :]
            m = jnp.maximum(jnp.broadcast_to(jnp.max(s, axis=-1, keepdims=True), (SW_BLOCK, LANES)), sinks[g])
            p = jnp.exp(s - jnp.concatenate([m] * (SW_SPAN // LANES), axis=1))
            p_ref[chain_rows(n, g), :] = p.astype(BF16)
            denom_ref[chain_rows(n, g), :] = (
                jnp.broadcast_to(jnp.sum(p, axis=-1, keepdims=True), (SW_BLOCK, LANES)) + jnp.exp(sinks[g] - m))
        return carry

    def value_step(n, carry):
        first = jnp.clip(n - 1, 0, SW_NB - 3)
        vw = v_ref[0, pl.ds(pl.multiple_of(first * SW_BLOCK, SW_BLOCK), SW_SPAN), :]
        heads = [jnp.dot(p_ref[chain_rows(n, g), :], vw, preferred_element_type=F32) / denom_ref[chain_rows(n, g), :]
                 for g in range(SW_GROUP)]
        out = jnp.concatenate([jnp.where(lo_f, heads[0], heads[1]), jnp.where(lo_f, heads[2], heads[3])], axis=1)
        o_ref[0, pl.ds(pl.multiple_of(n * SW_BLOCK, SW_BLOCK), SW_BLOCK), :] = out.astype(BF16)
        return carry

    lax.fori_loop(0, SW_NB, softmax_step, 0, unroll=SW_BLOCK_UNROLL)
    lax.fori_loop(0, SW_NB, value_step, 0, unroll=SW_BLOCK_UNROLL)


def _sw_attention(qkv, rel_table, sinks):
    qw = SW_GROUP * HEAD_DIM
    k_off = SW_Q_HEADS * HEAD_DIM // LANES
    v_off = k_off + SW_KV_HEADS
    return pl.pallas_call(
        _sw_kernel,
        grid=(SW_KV_HEADS, BATCH),
        in_specs=[
            pl.BlockSpec(memory_space=pltpu.SMEM),
            pl.BlockSpec((1, SW_GROUP, SW_REL_PERIOD), lambda j, b: (j, 0, 0)),
            pl.BlockSpec((1, SEQ, qw), lambda j, b: (b, 0, j)),
            pl.BlockSpec((1, SEQ, LANES), lambda j, b: (b, 0, k_off + j)),
            pl.BlockSpec((1, SEQ, LANES), lambda j, b: (b, 0, v_off + j)),
        ],
        out_specs=pl.BlockSpec((1, SEQ, qw), lambda j, b: (b, 0, j)),
        out_shape=jax.ShapeDtypeStruct((BATCH, SEQ, SW_Q_HEADS * HEAD_DIM), BF16),
        scratch_shapes=[pltpu.VMEM((3, SW_GROUP * SW_BLOCK, SW_SPAN), F32),
                        pltpu.VMEM((SW_GROUP * SEQ, SW_SPAN), BF16),
                        pltpu.VMEM((SW_GROUP * SEQ, LANES), F32)],
        compiler_params=_params(("arbitrary", "arbitrary"), 32),
        name="sw_attention",
    )(sinks, rel_table, qkv, qkv, qkv)


def _post_attn_kernel(o_ref, wo_ref, x_ref, gate_ref, g_ref, sc_ref, sh_ref, wr_ref,
                      xo_ref, h_ref, aff_ref):
    y = jnp.dot(o_ref[...], wo_ref[...], preferred_element_type=F32)
    xn = x_ref[...] + gate_ref[0] * y
    xo_ref[...] = xn
    h = _norm_mod(xn, g_ref[0], sc_ref[0], sh_ref[0])
    for j in range(TOKEN_TILE_ROWS):
        h_ref[pl.ds(j, ROW_TILE, stride=TOKEN_TILE_ROWS), :] = h[:, j * LANES:(j + 1) * LANES]
    hb = h.astype(BF16)
    logits = lax.dot_general(wr_ref[...], hb, (((1,), (1,)), ((), ())), preferred_element_type=F32)
    aff_ref[...] = _softmax0(logits)


def _post_attn(o, w_o, x, norm_g3, mod, w_router_t, layer):
    rows = BATCH * SEQ
    return pl.pallas_call(
        _post_attn_kernel,
        grid=(rows // ROW_TILE,),
        in_specs=[
            pl.BlockSpec((ROW_TILE, D_MODEL), lambda i: (i, 0)),
            pl.BlockSpec((D_MODEL, D_MODEL), lambda i: (0, 0)),
            pl.BlockSpec((ROW_TILE, D_MODEL), lambda i: (i, 0)),
            _mod_spec(layer, 2),
            pl.BlockSpec((1, 1, D_MODEL), lambda i: (layer * 2 + 1, 0, 0)),
            _mod_spec(layer, 4),
            _mod_spec(layer, 3),
            pl.BlockSpec((N_EXPERTS, D_MODEL), lambda i: (0, 0)),
        ],
        out_specs=[
            pl.BlockSpec((ROW_TILE, D_MODEL), lambda i: (i, 0)),
            pl.BlockSpec((ROW_TILE * TOKEN_TILE_ROWS, LANES), lambda i: (i, 0)),
            pl.BlockSpec((N_EXPERTS, ROW_TILE), lambda i: (0, i)),
        ],
        out_shape=[
            jax.ShapeDtypeStruct((rows, D_MODEL), F32),
            jax.ShapeDtypeStruct((rows * TOKEN_TILE_ROWS, LANES), F32),
            jax.ShapeDtypeStruct((N_EXPERTS, rows), F32),
        ],
        compiler_params=_params(("arbitrary",), 48),
        name="post_attn",
    )(o, w_o, x, mod, norm_g3, mod, mod, w_router_t)


def _prefix_count(x):
    nchunk = SEQ // PREFIX_CHUNK
    r = lax.broadcasted_iota(jnp.int32, (PREFIX_CHUNK, PREFIX_CHUNK), 0)
    c = lax.broadcasted_iota(jnp.int32, (PREFIX_CHUNK, PREFIX_CHUNK), 1)
    upper = jnp.where(r < c, 1.0, 0.0).astype(BF16)
    chunks = [x[:, k * PREFIX_CHUNK:(k + 1) * PREFIX_CHUNK] for k in range(nchunk)]
    local = jnp.dot(jnp.concatenate(chunks, axis=0).astype(BF16), upper, preferred_element_type=F32)
    out = []
    offset = jnp.zeros((N_EXPERTS, 1), F32)
    for k in range(nchunk):
        out.append(local[k * N_EXPERTS:(k + 1) * N_EXPERTS] + offset)
        offset = offset + jnp.sum(chunks[k], axis=1, keepdims=True)
    return jnp.concatenate(out, axis=1)


def _route_kernel(aff_ref, slot_ref, tok_ref, gate_ref):
    aff = aff_ref[...]
    bits = pltpu.bitcast(aff, jnp.int32)

    def count(mask):
        return jnp.sum(jnp.where(mask, 1.0, 0.0), axis=1, keepdims=True)

    def search(_, bounds):
        lo, hi = bounds
        mid = lo + ((hi - lo) >> 1)
        ge = count(bits >= mid) >= CAP
        return jnp.where(ge, mid, lo), jnp.where(ge, hi, mid)

    lo0 = jnp.zeros((N_EXPERTS, 1), jnp.int32)
    hi0 = jnp.full((N_EXPERTS, 1), 0x7F800000, jnp.int32)
    tau, _ = lax.fori_loop(0, 31, search, (lo0, hi0))
    gt = bits > tau
    eq = jnp.where(bits == tau, 1.0, 0.0)
    need = CAP - count(gt)
    sel = jnp.where(gt, 1.0, jnp.where(_prefix_count(eq) < need, eq, 0.0))
    pos = _prefix_count(sel).astype(jnp.int32)
    slot = jnp.where(sel > 0.5, pos, -1)
    slot_ref[...] = slot
    cap_id = lax.broadcasted_iota(jnp.int32, (CAP, SEQ), 0)
    token = lax.broadcasted_iota(jnp.int32, (CAP, SEQ), 1).astype(F32)
    tok_cols, gate_cols = [], []
    for e in range(N_EXPERTS):
        hit = slot[e:e + 1, :] == cap_id
        tok_cols.append(jnp.sum(jnp.where(hit, token, 0.0), axis=1, keepdims=True))
        gate_cols.append(jnp.sum(jnp.where(hit, aff[e:e + 1, :], 0.0), axis=1, keepdims=True))
    tok_ref[0] = jnp.concatenate(tok_cols, axis=1).astype(jnp.int32)
    gate_ref[0] = jnp.concatenate(gate_cols, axis=1)


def _route(aff_t):
    return pl.pallas_call(
        _route_kernel,
        grid=(BATCH,),
        in_specs=[pl.BlockSpec((N_EXPERTS, SEQ), lambda b: (0, b))],
        out_specs=[pl.BlockSpec((N_EXPERTS, SEQ), lambda b: (0, b)),
                   pl.BlockSpec((1, CAP, N_EXPERTS), lambda b: (b, 0, 0)),
                   pl.BlockSpec((1, CAP, N_EXPERTS), lambda b: (b, 0, 0))],
        out_shape=[jax.ShapeDtypeStruct((N_EXPERTS, BATCH * SEQ), jnp.int32),
                   jax.ShapeDtypeStruct((BATCH, CAP, N_EXPERTS), jnp.int32),
                   jax.ShapeDtypeStruct((BATCH, CAP, N_EXPERTS), F32)],
        compiler_params=_params(("arbitrary",), 32),
        name="route",
    )(aff_t)


GATHER_ROWS = BATCH * CAP
FF_STEPS = EXPERT_FF // FF_TILE
ROWS_PER_CHUNK = GATHER_ROWS // (FF_STEPS * BATCH)
PROLOGUE_UNROLL = 8


def _expert_kernel(tok_ref, h_ref, gates_ref, wg_ref, wu_ref, wd_ref, y_ref, x_ref, sem, xb_ref, acc_ref):
    e = pl.program_id(0)
    f = pl.program_id(1)
    slot = lax.rem(e, 2)
    next_slot = 1 - slot
    next_e = jnp.minimum(e + 1, N_EXPERTS - 1)

    tile = TOKEN_TILE_ROWS

    def row_copy(expert, row, dst_slot):
        src = pl.multiple_of(tok_ref[expert * GATHER_ROWS + row] * tile, tile)
        dst = pl.multiple_of(row * tile, tile)
        return pltpu.make_async_copy(h_ref.at[pl.ds(src, tile), :], x_ref.at[dst_slot, pl.ds(dst, tile), :],
                                     sem.at[dst_slot])

    def wait_rows(dst_slot):
        pltpu.make_async_copy(h_ref.at[pl.ds(0, GATHER_ROWS * tile), :], x_ref.at[dst_slot],
                              sem.at[dst_slot]).wait()

    @pl.when(jnp.logical_and(e == 0, f == 0))
    def _():
        def issue(i, carry):
            for u in range(PROLOGUE_UNROLL):
                row_copy(0, i * PROLOGUE_UNROLL + u, 0).start()
            return carry
        lax.fori_loop(0, GATHER_ROWS // PROLOGUE_UNROLL, issue, 0)

    @pl.when(f == 0)
    def _():
        wait_rows(slot)
        acc_ref[...] = jnp.zeros_like(acc_ref)
        for b in range(BATCH):
            cols = [x_ref[slot, pl.ds(b * CAP * tile + j, CAP, stride=tile), :] for j in range(tile)]
            xb_ref[b] = jnp.concatenate(cols, axis=1).astype(BF16)

    wg = wg_ref[0, 0].astype(BF16)
    wu = wu_ref[0, 0].astype(BF16)
    wd = wd_ref[0, 0].astype(BF16)
    for b in range(BATCH):
        first = (f * BATCH + b) * ROWS_PER_CHUNK
        for j in range(ROWS_PER_CHUNK):
            row_copy(next_e, first + j, next_slot).start()
        x = xb_ref[b]
        a = jnp.dot(x, wg, preferred_element_type=F32)
        u = jnp.dot(x, wu, preferred_element_type=F32)
        act = ((a * jax.nn.sigmoid(a)) * u).astype(BF16)
        acc_ref[b] += jnp.dot(act, wd, preferred_element_type=F32)

    @pl.when(f == FF_STEPS - 1)
    def _():
        gates = gates_ref[0]
        for b in range(BATCH):
            y_ref[0, b] = (acc_ref[b] * gates[:, b:b + 1]).astype(BF16)

    @pl.when(jnp.logical_and(e == N_EXPERTS - 1, f == FF_STEPS - 1))
    def _():
        wait_rows(next_slot)


def _experts(tok, h, gates, w_gate, w_up, w_down, layer):
    grid_spec = pltpu.PrefetchScalarGridSpec(
        num_scalar_prefetch=1,
        grid=(N_EXPERTS, FF_STEPS),
        in_specs=[
            pl.BlockSpec(memory_space=pl.ANY),
            pl.BlockSpec((1, CAP, BATCH), lambda e, f, tok: (e, 0, 0)),
            pl.BlockSpec((1, 1, D_MODEL, FF_TILE), lambda e, f, tok: (layer, e, 0, f)),
            pl.BlockSpec((1, 1, D_MODEL, FF_TILE), lambda e, f, tok: (layer, e, 0, f)),
            pl.BlockSpec((1, 1, FF_TILE, D_MODEL), lambda e, f, tok: (layer, e, f, 0)),
        ],
        out_specs=pl.BlockSpec((1, BATCH, CAP, D_MODEL), lambda e, f, tok: (e, 0, 0, 0)),
        scratch_shapes=[pltpu.VMEM((2, GATHER_ROWS * TOKEN_TILE_ROWS, LANES), F32), pltpu.SemaphoreType.DMA((2,)),
                        pltpu.VMEM((BATCH, CAP, D_MODEL), BF16), pltpu.VMEM((BATCH, CAP, D_MODEL), F32)],
    )
    return pl.pallas_call(
        _expert_kernel,
        grid_spec=grid_spec,
        out_shape=jax.ShapeDtypeStruct((N_EXPERTS, BATCH, CAP, D_MODEL), BF16),
        compiler_params=_params(("arbitrary", "arbitrary"), 60),
        name="moe_experts",
    )(tok, h, gates, w_gate, w_up, w_down)


def _combine_kernel(slot_ref, y_ref, x_ref, gate_ref, g_ref, o_ref, *, final):
    lane = lax.broadcasted_iota(jnp.int32, (COMBINE_ROWS, CAP), 1)
    slots = slot_ref[0]
    onehot = jnp.concatenate(
        [jnp.where(slots[:, e:e + 1] == lane, 1.0, 0.0).astype(BF16) for e in range(N_EXPERTS)], axis=1)
    moe = jnp.dot(onehot, y_ref[...].reshape(N_EXPERTS * CAP, D_MODEL), preferred_element_type=F32)
    x = x_ref[0] + gate_ref[0] * moe
    if final:
        x = (x * lax.rsqrt(jnp.mean(x * x, axis=-1, keepdims=True) + RMS_EPS)) * g_ref[...]
    o_ref[0] = x


def _combine(slot_t, y, x, mod, final_g, layer, final):
    return pl.pallas_call(
        functools.partial(_combine_kernel, final=final),
        grid=(BATCH, SEQ // COMBINE_ROWS),
        in_specs=[
            pl.BlockSpec((1, COMBINE_ROWS, N_EXPERTS), lambda b, t: (b, t, 0)),
            pl.BlockSpec((N_EXPERTS, 1, CAP, D_MODEL), lambda b, t: (0, b, 0, 0)),
            pl.BlockSpec((1, COMBINE_ROWS, D_MODEL), lambda b, t: (b, t, 0)),
            pl.BlockSpec((1, 1, D_MODEL), lambda b, t: ((layer * 6 + 5) * BATCH + b, 0, 0)),
            pl.BlockSpec((1, D_MODEL), lambda b, t: (0, 0)),
        ],
        out_specs=pl.BlockSpec((1, COMBINE_ROWS, D_MODEL), lambda b, t: (b, t, 0)),
        out_shape=jax.ShapeDtypeStruct((BATCH, SEQ, D_MODEL), F32),
        compiler_params=_params(("arbitrary", "arbitrary"), 48),
        name="moe_combine",
    )(slot_t, y, x, mod, final_g.reshape(1, D_MODEL))


def _na_qkv_weight(w):
    nq = NA_HEADS * HEAD_DIM
    scale = jnp.concatenate([jnp.full((nq,), HEAD_DIM ** -0.5, F32), jnp.ones((2 * nq,), F32)])
    return (w * scale).astype(BF16)


def _sw_qkv_weight(w):
    nq = SW_Q_HEADS * HEAD_DIM
    nkv = SW_KV_HEADS * HEAD_DIM
    dup = lambda t: jnp.concatenate([t.reshape(D_MODEL, SW_KV_HEADS, 1, HEAD_DIM)] * 2, axis=2).reshape(
        D_MODEL, 2 * nkv)
    wq = w[:, :nq] * HEAD_DIM ** -0.5
    return jnp.concatenate([wq, dup(w[:, nq:nq + nkv]), dup(w[:, nq + nkv:])], axis=1).astype(BF16)


def kernel(x, c, ada_w, ada_b, norm_g, na_w_qkv, na_w_o, na_rpb, sw_w_qkv, sw_w_o, sw_sinks, t5_bias,
           moe_w_router, moe_w_gate, moe_w_up, moe_w_down, final_g):
    mod = _ada(c, ada_w, ada_b)
    norm_g3 = norm_g.reshape(DEPTH * 2, 1, D_MODEL)
    x = x.reshape(BATCH * SEQ, D_MODEL)
    for layer in range(DEPTH):
        j = layer // N_MIXERS
        if layer % N_MIXERS == 0:
            qkv = _qkv(x, norm_g3, mod, _na_qkv_weight(na_w_qkv[j]), layer)
            o = _na_attention(qkv.reshape(BATCH, SEQ, -1), _na_rpb_rows(na_rpb[j]))
            w_o = na_w_o[j]
        else:
            qkv = _qkv(x, norm_g3, mod, _sw_qkv_weight(sw_w_qkv[j]), layer)
            o = _sw_attention(qkv.reshape(BATCH, SEQ, -1), _sw_rel_table(t5_bias), sw_sinks[j])
            w_o = sw_w_o[j]
        w_router = moe_w_router[layer].astype(BF16)
        x, h, aff_t = _post_attn(o.reshape(BATCH * SEQ, -1), w_o.astype(BF16), x, norm_g3, mod,
                                 w_router.T, layer)
        slot, tok, gate = _route(aff_t)
        rows = tok.transpose(2, 0, 1) + (jnp.arange(BATCH, dtype=jnp.int32) * SEQ)[None, :, None]
        y = _experts(rows.reshape(-1), h, gate.transpose(2, 1, 0), moe_w_gate, moe_w_up, moe_w_down, layer)
        slot_t = slot.reshape(N_EXPERTS, BATCH, SEQ).transpose(1, 2, 0)
        x = _combine(slot_t, y, x.reshape(BATCH, SEQ, D_MODEL), mod, final_g, layer,
                     final=layer == DEPTH - 1).reshape(BATCH * SEQ, D_MODEL)
    return x.reshape(BATCH, SEQ, D_MODEL)
```

```python
import functools

import numpy as np
import jax
import jax.numpy as jnp
from jax import lax
from jax.experimental import pallas as pl
from jax.experimental.pallas import tpu as pltpu

D_MODEL = 1024
BATCH = 8
SEQ = 2048
DEPTH = 2
GRID_W = 64
ROWS = SEQ // GRID_W
N_MIXERS = 2
HEAD_DIM = 64
NA_HEADS = 16
NA_WIN_H = 8
NA_WIN_W = 16
SW_Q_HEADS = 16
SW_KV_HEADS = 4
SW_GROUP = SW_Q_HEADS // SW_KV_HEADS
SW_WINDOW = 128
SW_BLOCK = 128
SW_NB = SEQ // SW_BLOCK
SW_SPAN = 3 * SW_BLOCK
T5_BUCKETS = 32
T5_MAX_DIST = 128
N_EXPERTS = 16
EXPERT_FF = 2048
EC_CAPACITY = 2
CAP = EC_CAPACITY * SEQ // N_EXPERTS
RMS_EPS = 1e-6
NEG = -1e30

LANES = 128
TOKEN_TILE_ROWS = D_MODEL // LANES
MIB = 1024 * 1024
F32 = jnp.float32
BF16 = jnp.bfloat16

ROW_TILE = 512
FF_TILE = 512
COMBINE_ROWS = 512
PREFIX_CHUNK = 256


def _params(semantics, vmem_mib, flags=None):
    return pltpu.CompilerParams(dimension_semantics=semantics, vmem_limit_bytes=vmem_mib * MIB, flags=flags)


def _norm_mod(x, g, sc, sh):
    y = x * lax.rsqrt(jnp.mean(x * x, axis=-1, keepdims=True) + RMS_EPS)
    return (y * g) * (1.0 + sc) + sh


def _softmax0(z):
    z = z - jnp.max(z, axis=0, keepdims=True)
    p = jnp.exp(z)
    return p / jnp.sum(p, axis=0, keepdims=True)


def _half_masks(rows):
    lane = lax.broadcasted_iota(jnp.int32, (rows, LANES), 1)
    lo = jnp.where(lane < HEAD_DIM, 1.0, 0.0).astype(BF16)
    hi = jnp.where(lane < HEAD_DIM, 0.0, 1.0).astype(BF16)
    return lo, hi


def _ada_kernel(c_ref, w_ref, b_ref, o_ref):
    c = c_ref[...]
    act = (c * jax.nn.sigmoid(c)).astype(BF16)
    o_ref[0] = jnp.dot(act, w_ref[0].astype(BF16), preferred_element_type=F32) + b_ref[0]


def _ada(c, ada_w, ada_b):
    out = pl.pallas_call(
        _ada_kernel,
        grid=(DEPTH, 6),
        in_specs=[
            pl.BlockSpec((BATCH, D_MODEL), lambda l, k: (0, 0)),
            pl.BlockSpec((1, D_MODEL, D_MODEL), lambda l, k: (l, 0, k)),
            pl.BlockSpec((1, 1, D_MODEL), lambda l, k: (l * 6 + k, 0, 0)),
        ],
        out_specs=pl.BlockSpec((1, BATCH, D_MODEL), lambda l, k: (l * 6 + k, 0, 0)),
        out_shape=jax.ShapeDtypeStruct((DEPTH * 6, BATCH, D_MODEL), F32),
        compiler_params=_params(("arbitrary", "arbitrary"), 32),
        name="ada_mod",
    )(c, ada_w, ada_b.reshape(DEPTH * 6, 1, D_MODEL))
    return out.reshape(DEPTH * 6 * BATCH, 1, D_MODEL)


def _mod_spec(layer, chunk):
    tiles_per_seq = SEQ // ROW_TILE
    return pl.BlockSpec((1, 1, D_MODEL),
                        lambda i: ((layer * 6 + chunk) * BATCH + i // tiles_per_seq, 0, 0))


def _qkv_kernel(x_ref, g_ref, sc_ref, sh_ref, w_ref, o_ref):
    h = _norm_mod(x_ref[...], g_ref[0], sc_ref[0], sh_ref[0])
    o_ref[...] = jnp.dot(h.astype(BF16), w_ref[...], preferred_element_type=F32).astype(BF16)


def _qkv(x, norm_g3, mod, w, layer):
    n = w.shape[1]
    return pl.pallas_call(
        _qkv_kernel,
        grid=(BATCH * SEQ // ROW_TILE,),
        in_specs=[
            pl.BlockSpec((ROW_TILE, D_MODEL), lambda i: (i, 0)),
            pl.BlockSpec((1, 1, D_MODEL), lambda i: (layer * 2, 0, 0)),
            _mod_spec(layer, 1),
            _mod_spec(layer, 0),
            pl.BlockSpec((D_MODEL, n), lambda i: (0, 0)),
        ],
        out_specs=pl.BlockSpec((ROW_TILE, n), lambda i: (i, 0)),
        out_shape=jax.ShapeDtypeStruct((BATCH * SEQ, n), BF16),
        compiler_params=_params(("arbitrary",), 48),
        name="norm_qkv",
    )(x, norm_g3, mod, mod, w)


NA_BIAS_ROWS = 2 * NA_WIN_H - 1
NA_ROW_UNROLL = 4


def _na_rpb_rows(rpb):
    w = NA_WIN_W - 1
    pad = jnp.zeros(rpb.shape[:2] + (LANES - 2 * w - 1,), F32)
    return jnp.concatenate([rpb[..., w:].astype(F32), pad, rpb[..., :w].astype(F32)], axis=-1)


def _na_build_bias(w_ref, bias_ref):
    lane = lax.broadcasted_iota(jnp.int32, (GRID_W, LANES), 1)
    col = lax.broadcasted_iota(jnp.int32, (GRID_W, LANES), 0)
    kc = jnp.bitwise_and(lane, GRID_W - 1)
    cstart = jnp.clip(col - NA_WIN_W // 2, 0, GRID_W - NA_WIN_W)
    in_window = jnp.where(kc >= cstart, jnp.where(kc < cstart + NA_WIN_W, 1.0, 0.0), 0.0) > 0.5
    low_half = lane < GRID_W

    def toeplitz(head, a, shift):
        row = jnp.broadcast_to(w_ref[head, a:a + 1, :], (GRID_W, LANES))
        return pltpu.roll(row, shift, 1, stride=1, stride_axis=0)

    for head in range(2):
        tiles = [jnp.where(in_window, jnp.where(low_half, toeplitz(head, a, 0), toeplitz(head, a + 1, GRID_W)), NEG)
                 for a in range(NA_BIAS_ROWS - 1)]
        for d in range(NA_WIN_H):
            for m in range(NA_WIN_H // 2):
                bias_ref[d, head * GRID_W:(head + 1) * GRID_W, m * LANES:(m + 1) * LANES] = (
                    tiles[NA_WIN_H - 1 - d + 2 * m])


def _na_kernel(w_ref, q_ref, k_ref, v_ref, o_ref, bias_ref, p_ref, denom_ref, s0_ref, s1_ref):
    @pl.when(pl.program_id(1) == 0)
    def _():
        _na_build_bias(w_ref, bias_ref)

    lo, hi = _half_masks(GRID_W)
    lo_f = lax.broadcasted_iota(jnp.int32, (GRID_W, LANES), 1) < HEAD_DIM
    span = NA_WIN_H * GRID_W

    def pair_rows(r):
        return pl.ds(pl.multiple_of(r * 2 * GRID_W, 2 * GRID_W), 2 * GRID_W)

    def group_scores(i, s_ref):
        for u in range(NA_ROW_UNROLL):
            r = jnp.minimum(i * NA_ROW_UNROLL + u, ROWS - 1)
            rs = jnp.clip(r - NA_WIN_H // 2, 0, ROWS - NA_WIN_H)
            q = q_ref[0, pl.ds(pl.multiple_of(r * GRID_W, GRID_W), GRID_W), :]
            lhs = jnp.concatenate([q * lo, q * hi], axis=0)
            kw = k_ref[0, pl.ds(pl.multiple_of(rs * GRID_W, GRID_W), span), :]
            s_ref[u] = lax.dot_general(lhs, kw, (((1,), (1,)), ((), ())), preferred_element_type=F32)

    def group_softmax(i, s_ref):
        for u in range(NA_ROW_UNROLL):
            r = i * NA_ROW_UNROLL + u
            rs = jnp.clip(r - NA_WIN_H // 2, 0, ROWS - NA_WIN_H)
            s = s_ref[u] + bias_ref[r - rs]
            m = jnp.broadcast_to(jnp.max(s, axis=-1, keepdims=True), (2 * GRID_W, LANES))
            p = jnp.exp(s - jnp.concatenate([m] * (span // LANES), axis=1))
            p_ref[pair_rows(r), :] = p.astype(BF16)
            denom_ref[pair_rows(r), :] = jnp.broadcast_to(jnp.sum(p, axis=-1, keepdims=True), (2 * GRID_W, LANES))

    def softmax_step(i, carry):
        group_scores(2 * i + 1, s1_ref)
        group_softmax(2 * i, s0_ref)
        group_scores(2 * i + 2, s0_ref)
        group_softmax(2 * i + 1, s1_ref)
        return carry

    def value_step(r, carry):
        rs = jnp.clip(r - NA_WIN_H // 2, 0, ROWS - NA_WIN_H)
        vw = v_ref[0, pl.ds(pl.multiple_of(rs * GRID_W, GRID_W), span), :]
        o = jnp.dot(p_ref[pair_rows(r), :], vw, preferred_element_type=F32) / denom_ref[pair_rows(r), :]
        out = jnp.where(lo_f, o[:GRID_W], o[GRID_W:])
        o_ref[0, pl.ds(pl.multiple_of(r * GRID_W, GRID_W), GRID_W), :] = out.astype(BF16)
        return carry

    group_scores(0, s0_ref)
    lax.fori_loop(0, ROWS // (2 * NA_ROW_UNROLL), softmax_step, 0)
    lax.fori_loop(0, ROWS, value_step, 0, unroll=NA_ROW_UNROLL)


def _na_attention(qkv, rpb_rows):
    pairs = NA_HEADS // 2
    return pl.pallas_call(
        _na_kernel,
        grid=(pairs, BATCH),
        in_specs=[
            pl.BlockSpec((2, NA_BIAS_ROWS, LANES), lambda p, b: (p, 0, 0)),
            pl.BlockSpec((1, SEQ, LANES), lambda p, b: (b, 0, p)),
            pl.BlockSpec((1, SEQ, LANES), lambda p, b: (b, 0, pairs + p)),
            pl.BlockSpec((1, SEQ, LANES), lambda p, b: (b, 0, 2 * pairs + p)),
        ],
        out_specs=pl.BlockSpec((1, SEQ, LANES), lambda p, b: (b, 0, p)),
        out_shape=jax.ShapeDtypeStruct((BATCH, SEQ, NA_HEADS * HEAD_DIM), BF16),
        scratch_shapes=[pltpu.VMEM((NA_WIN_H, 2 * GRID_W, NA_WIN_H * GRID_W), F32),
                        pltpu.VMEM((2 * SEQ, NA_WIN_H * GRID_W), BF16),
                        pltpu.VMEM((2 * SEQ, LANES), F32),
                        pltpu.VMEM((NA_ROW_UNROLL, 2 * GRID_W, NA_WIN_H * GRID_W), F32),
                        pltpu.VMEM((NA_ROW_UNROLL, 2 * GRID_W, NA_WIN_H * GRID_W), F32)],
        compiler_params=_params(("arbitrary", "arbitrary"), 32),
        name="na_attention",
    )(rpb_rows, qkv, qkv, qkv)


def _t5_buckets(rel):
    half = T5_BUCKETS // 2
    max_exact = half // 2
    n = np.abs(rel)
    large = max_exact + (np.log(np.maximum(n, 1) / max_exact)
                         / np.log(T5_MAX_DIST / max_exact) * (half - max_exact)).astype(np.int32)
    large = np.minimum(large, half - 1)
    return (rel > 0).astype(np.int32) * half + np.where(n < max_exact, n, large)


SW_REL_PERIOD = 512
SW_BLOCK_UNROLL = 2


def _sw_rel_table(t5_table):
    k = np.arange(SW_REL_PERIOD)
    rel = np.where(k < SW_REL_PERIOD // 2, k, k - SW_REL_PERIOD)
    ok = np.abs(rel) <= SW_WINDOW
    vals = jnp.where(ok[:, None], t5_table[_t5_buckets(rel)].astype(F32), NEG)
    return vals.T.reshape(SW_KV_HEADS, SW_GROUP, SW_REL_PERIOD)


def _sw_build_bias(rel_ref, bias_ref):
    for g in range(SW_GROUP):
        row = jnp.broadcast_to(rel_ref[0, g:g + 1, :], (SW_BLOCK, SW_REL_PERIOD))
        t = pltpu.roll(row, 0, 1, stride=1, stride_axis=0)
        rows = slice(g * SW_BLOCK, (g + 1) * SW_BLOCK)
        bias_ref[0, rows, :] = t[:, :SW_SPAN]
        bias_ref[1, rows, :] = jnp.concatenate([t[:, SW_SPAN:], t[:, :2 * SW_BLOCK]], axis=1)
        bias_ref[2, rows, :] = jnp.concatenate([t[:, 2 * SW_BLOCK:], t[:, :SW_BLOCK]], axis=1)


def _sw_kernel(sink_ref, rel_ref, q_ref, k_ref, v_ref, o_ref, bias_ref, p_ref, denom_ref):
    j = pl.program_id(0)

    @pl.when(pl.program_id(1) == 0)
    def _():
        _sw_build_bias(rel_ref, bias_ref)

    masks = _half_masks(SW_BLOCK)
    lo_f = lax.broadcasted_iota(jnp.int32, (SW_BLOCK, LANES), 1) < HEAD_DIM
    sinks = [jnp.full((SW_BLOCK, LANES), sink_ref[j * SW_GROUP + g], F32) for g in range(SW_GROUP)]

    def chain_rows(n, g):
        return pl.ds(pl.multiple_of((n * SW_GROUP + g) * SW_BLOCK, SW_BLOCK), SW_BLOCK)

    def softmax_step(n, carry):
        first = jnp.clip(n - 1, 0, SW_NB - 3)
        rows = pl.ds(pl.multiple_of(n * SW_BLOCK, SW_BLOCK), SW_BLOCK)
        kw = k_ref[0, pl.ds(pl.multiple_of(first * SW_BLOCK, SW_BLOCK), SW_SPAN), :]
        for g in range(SW_GROUP):
            q = q_ref[0, rows, (g // 2) * LANES:(g // 2 + 1) * LANES] * masks[g % 2]
            s = lax.dot_general(q, kw, (((1,), (1,)), ((), ())), preferred_element_type=F32)
            s = s + bias_ref[n - first, g * SW_BLOCK:(g + 1) * SW_BLOCK, :]
            m = jnp.maximum(jnp.broadcast_to(jnp.max(s, axis=-1, keepdims=True), (SW_BLOCK, LANES)), sinks[g])
            p = jnp.exp(s - jnp.concatenate([m] * (SW_SPAN // LANES), axis=1))
            p_ref[chain_rows(n, g), :] = p.astype(BF16)
            denom_ref[chain_rows(n, g), :] = (
                jnp.broadcast_to(jnp.sum(p, axis=-1, keepdims=True), (SW_BLOCK, LANES)) + jnp.exp(sinks[g] - m))
        return carry

    def value_step(n, carry):
        first = jnp.clip(n - 1, 0, SW_NB - 3)
        vw = v_ref[0, pl.ds(pl.multiple_of(first * SW_BLOCK, SW_BLOCK), SW_SPAN), :]
        heads = [jnp.dot(p_ref[chain_rows(n, g), :], vw, preferred_element_type=F32) / denom_ref[chain_rows(n, g), :]
                 for g in range(SW_GROUP)]
        out = jnp.concatenate([jnp.where(lo_f, heads[0], heads[1]), jnp.where(lo_f, heads[2], heads[3])], axis=1)
        o_ref[0, pl.ds(pl.multiple_of(n * SW_BLOCK, SW_BLOCK), SW_BLOCK), :] = out.astype(BF16)
        return carry

    lax.fori_loop(0, SW_NB, softmax_step, 0, unroll=SW_BLOCK_UNROLL)
    lax.fori_loop(0, SW_NB, value_step, 0, unroll=SW_BLOCK_UNROLL)


def _sw_attention(qkv, rel_table, sinks):
    qw = SW_GROUP * HEAD_DIM
    k_off = SW_Q_HEADS * HEAD_DIM // LANES
    v_off = k_off + SW_KV_HEADS
    return pl.pallas_call(
        _sw_kernel,
        grid=(SW_KV_HEADS, BATCH),
        in_specs=[
            pl.BlockSpec(memory_space=pltpu.SMEM),
            pl.BlockSpec((1, SW_GROUP, SW_REL_PERIOD), lambda j, b: (j, 0, 0)),
            pl.BlockSpec((1, SEQ, qw), lambda j, b: (b, 0, j)),
            pl.BlockSpec((1, SEQ, LANES), lambda j, b: (b, 0, k_off + j)),
            pl.BlockSpec((1, SEQ, LANES), lambda j, b: (b, 0, v_off + j)),
        ],
        out_specs=pl.BlockSpec((1, SEQ, qw), lambda j, b: (b, 0, j)),
        out_shape=jax.ShapeDtypeStruct((BATCH, SEQ, SW_Q_HEADS * HEAD_DIM), BF16),
        scratch_shapes=[pltpu.VMEM((3, SW_GROUP * SW_BLOCK, SW_SPAN), F32),
                        pltpu.VMEM((SW_GROUP * SEQ, SW_SPAN), BF16),
                        pltpu.VMEM((SW_GROUP * SEQ, LANES), F32)],
        compiler_params=_params(("arbitrary", "arbitrary"), 32),
        name="sw_attention",
    )(sinks, rel_table, qkv, qkv, qkv)


def _post_attn_kernel(o_ref, wo_ref, x_ref, gate_ref, g_ref, sc_ref, sh_ref, wr_ref,
                      xo_ref, h_ref, aff_ref):
    y = jnp.dot(o_ref[...], wo_ref[...], preferred_element_type=F32)
    xn = x_ref[...] + gate_ref[0] * y
    xo_ref[...] = xn
    h = _norm_mod(xn, g_ref[0], sc_ref[0], sh_ref[0])
    for j in range(TOKEN_TILE_ROWS):
        h_ref[pl.ds(j, ROW_TILE, stride=TOKEN_TILE_ROWS), :] = h[:, j * LANES:(j + 1) * LANES]
    hb = h.astype(BF16)
    logits = lax.dot_general(wr_ref[...], hb, (((1,), (1,)), ((), ())), preferred_element_type=F32)
    aff_ref[...] = _softmax0(logits)


def _post_attn(o, w_o, x, norm_g3, mod, w_router_t, layer):
    rows = BATCH * SEQ
    return pl.pallas_call(
        _post_attn_kernel,
        grid=(rows // ROW_TILE,),
        in_specs=[
            pl.BlockSpec((ROW_TILE, D_MODEL), lambda i: (i, 0)),
            pl.BlockSpec((D_MODEL, D_MODEL), lambda i: (0, 0)),
            pl.BlockSpec((ROW_TILE, D_MODEL), lambda i: (i, 0)),
            _mod_spec(layer, 2),
            pl.BlockSpec((1, 1, D_MODEL), lambda i: (layer * 2 + 1, 0, 0)),
            _mod_spec(layer, 4),
            _mod_spec(layer, 3),
            pl.BlockSpec((N_EXPERTS, D_MODEL), lambda i: (0, 0)),
        ],
        out_specs=[
            pl.BlockSpec((ROW_TILE, D_MODEL), lambda i: (i, 0)),
            pl.BlockSpec((ROW_TILE * TOKEN_TILE_ROWS, LANES), lambda i: (i, 0)),
            pl.BlockSpec((N_EXPERTS, ROW_TILE), lambda i: (0, i)),
        ],
        out_shape=[
            jax.ShapeDtypeStruct((rows, D_MODEL), F32),
            jax.ShapeDtypeStruct((rows * TOKEN_TILE_ROWS, LANES), F32),
            jax.ShapeDtypeStruct((N_EXPERTS, rows), F32),
        ],
        compiler_params=_params(("arbitrary",), 48),
        name="post_attn",
    )(o, w_o, x, mod, norm_g3, mod, mod, w_router_t)


def _prefix_count(x):
    nchunk = SEQ // PREFIX_CHUNK
    r = lax.broadcasted_iota(jnp.int32, (PREFIX_CHUNK, PREFIX_CHUNK), 0)
    c = lax.broadcasted_iota(jnp.int32, (PREFIX_CHUNK, PREFIX_CHUNK), 1)
    upper = jnp.where(r < c, 1.0, 0.0).astype(BF16)
    chunks = [x[:, k * PREFIX_CHUNK:(k + 1) * PREFIX_CHUNK] for k in range(nchunk)]
    local = jnp.dot(jnp.concatenate(chunks, axis=0).astype(BF16), upper, preferred_element_type=F32)
    out = []
    offset = jnp.zeros((N_EXPERTS, 1), F32)
    for k in range(nchunk):
        out.append(local[k * N_EXPERTS:(k + 1) * N_EXPERTS] + offset)
        offset = offset + jnp.sum(chunks[k], axis=1, keepdims=True)
    return jnp.concatenate(out, axis=1)


def _route_kernel(aff_ref, slot_ref, tok_ref, gate_ref):
    aff = aff_ref[...]
    bits = pltpu.bitcast(aff, jnp.int32)

    def count(mask):
        return jnp.sum(jnp.where(mask, 1.0, 0.0), axis=1, keepdims=True)

    def search(_, bounds):
        lo, hi = bounds
        mid = lo + ((hi - lo) >> 1)
        ge = count(bits >= mid) >= CAP
        return jnp.where(ge, mid, lo), jnp.where(ge, hi, mid)

    lo0 = jnp.zeros((N_EXPERTS, 1), jnp.int32)
    hi0 = jnp.full((N_EXPERTS, 1), 0x7F800000, jnp.int32)
    tau, _ = lax.fori_loop(0, 31, search, (lo0, hi0))
    gt = bits > tau
    eq = jnp.where(bits == tau, 1.0, 0.0)
    need = CAP - count(gt)
    sel = jnp.where(gt, 1.0, jnp.where(_prefix_count(eq) < need, eq, 0.0))
    pos = _prefix_count(sel).astype(jnp.int32)
    chosen = sel > 0.5
    slot_ref[...] = jnp.where(chosen, pos, -1)
    lane = lax.broadcasted_iota(jnp.int32, (N_EXPERTS, SEQ), 1)
    packed = jnp.where(chosen, ROUTE_VALID | (lane << ROUTE_BITS) | (lane - pos), 0)
    gate = aff
    for k in range(ROUTE_BITS):
        step = 1 << k
        from_right = pltpu.roll(packed, SEQ - step, 1)
        gate_right = pltpu.roll(gate, SEQ - step, 1)
        arrives = (from_right & step) != 0
        stays = (packed & step) == 0
        packed = jnp.where(arrives, from_right, jnp.where(stays, packed, 0))
        gate = jnp.where(arrives, gate_right, gate)
    tok_ref[0] = (packed[:, :CAP] >> ROUTE_BITS) & (SEQ - 1)
    gate_ref[0] = gate[:, :CAP]


ROUTE_BITS = SEQ.bit_length() - 1
ROUTE_VALID = 1 << (2 * ROUTE_BITS)


def _route(aff_t):
    return pl.pallas_call(
        _route_kernel,
        grid=(BATCH,),
        in_specs=[pl.BlockSpec((N_EXPERTS, SEQ), lambda b: (0, b))],
        out_specs=[pl.BlockSpec((N_EXPERTS, SEQ), lambda b: (0, b)),
                   pl.BlockSpec((1, N_EXPERTS, CAP), lambda b: (b, 0, 0)),
                   pl.BlockSpec((1, N_EXPERTS, CAP), lambda b: (b, 0, 0))],
        out_shape=[jax.ShapeDtypeStruct((N_EXPERTS, BATCH * SEQ), jnp.int32),
                   jax.ShapeDtypeStruct((BATCH, N_EXPERTS, CAP), jnp.int32),
                   jax.ShapeDtypeStruct((BATCH, N_EXPERTS, CAP), F32)],
        compiler_params=_params(("arbitrary",), 32),
        name="route",
    )(aff_t)


GATHER_ROWS = BATCH * CAP
FF_STEPS = EXPERT_FF // FF_TILE
ROWS_PER_CHUNK = GATHER_ROWS // (FF_STEPS * BATCH)
PROLOGUE_UNROLL = 8


def _expert_kernel(tok_ref, h_ref, gates_ref, wg_ref, wu_ref, wd_ref, y_ref, x_ref, sem, xb_ref, acc_ref):
    e = pl.program_id(0)
    f = pl.program_id(1)
    slot = lax.rem(e, 2)
    next_slot = 1 - slot
    next_e = jnp.minimum(e + 1, N_EXPERTS - 1)

    tile = TOKEN_TILE_ROWS

    def row_copy(expert, row, dst_slot):
        src = pl.multiple_of(tok_ref[expert * GATHER_ROWS + row] * tile, tile)
        dst = pl.multiple_of(row * tile, tile)
        return pltpu.make_async_copy(h_ref.at[pl.ds(src, tile), :], x_ref.at[dst_slot, pl.ds(dst, tile), :],
                                     sem.at[dst_slot])

    def wait_rows(dst_slot):
        pltpu.make_async_copy(h_ref.at[pl.ds(0, GATHER_ROWS * tile), :], x_ref.at[dst_slot],
                              sem.at[dst_slot]).wait()

    @pl.when(jnp.logical_and(e == 0, f == 0))
    def _():
        def issue(i, carry):
            for u in range(PROLOGUE_UNROLL):
                row_copy(0, i * PROLOGUE_UNROLL + u, 0).start()
            return carry
        lax.fori_loop(0, GATHER_ROWS // PROLOGUE_UNROLL, issue, 0)

    @pl.when(f == 0)
    def _():
        wait_rows(slot)
        acc_ref[...] = jnp.zeros_like(acc_ref)
        for b in range(BATCH):
            cols = [x_ref[slot, pl.ds(b * CAP * tile + j, CAP, stride=tile), :] for j in range(tile)]
            xb_ref[b] = jnp.concatenate(cols, axis=1).astype(BF16)

    wg = wg_ref[0, 0].astype(BF16)
    wu = wu_ref[0, 0].astype(BF16)
    wd = wd_ref[0, 0].astype(BF16)
    for b in range(BATCH):
        first = (f * BATCH + b) * ROWS_PER_CHUNK
        for j in range(ROWS_PER_CHUNK):
            row_copy(next_e, first + j, next_slot).start()
        x = xb_ref[b]
        a = jnp.dot(x, wg, preferred_element_type=F32)
        u = jnp.dot(x, wu, preferred_element_type=F32)
        act = ((a * jax.nn.sigmoid(a)) * u).astype(BF16)
        acc_ref[b] += jnp.dot(act, wd, preferred_element_type=F32)

    @pl.when(f == FF_STEPS - 1)
    def _():
        gates = gates_ref[0]
        for b in range(BATCH):
            y_ref[0, b] = (acc_ref[b] * gates[:, b:b + 1]).astype(BF16)

    @pl.when(jnp.logical_and(e == N_EXPERTS - 1, f == FF_STEPS - 1))
    def _():
        wait_rows(next_slot)


def _experts(tok, h, gates, w_gate, w_up, w_down, layer):
    grid_spec = pltpu.PrefetchScalarGridSpec(
        num_scalar_prefetch=1,
        grid=(N_EXPERTS, FF_STEPS),
        in_specs=[
            pl.BlockSpec(memory_space=pl.ANY),
            pl.BlockSpec((1, CAP, BATCH), lambda e, f, tok: (e, 0, 0)),
            pl.BlockSpec((1, 1, D_MODEL, FF_TILE), lambda e, f, tok: (layer, e, 0, f)),
            pl.BlockSpec((1, 1, D_MODEL, FF_TILE), lambda e, f, tok: (layer, e, 0, f)),
            pl.BlockSpec((1, 1, FF_TILE, D_MODEL), lambda e, f, tok: (layer, e, f, 0)),
        ],
        out_specs=pl.BlockSpec((1, BATCH, CAP, D_MODEL), lambda e, f, tok: (e, 0, 0, 0)),
        scratch_shapes=[pltpu.VMEM((2, GATHER_ROWS * TOKEN_TILE_ROWS, LANES), F32), pltpu.SemaphoreType.DMA((2,)),
                        pltpu.VMEM((BATCH, CAP, D_MODEL), BF16), pltpu.VMEM((BATCH, CAP, D_MODEL), F32)],
    )
    return pl.pallas_call(
        _expert_kernel,
        grid_spec=grid_spec,
        out_shape=jax.ShapeDtypeStruct((N_EXPERTS, BATCH, CAP, D_MODEL), BF16),
        compiler_params=_params(("arbitrary", "arbitrary"), 60),
        name="moe_experts",
    )(tok, h, gates, w_gate, w_up, w_down)


def _combine_kernel(slot_ref, y_ref, x_ref, gate_ref, g_ref, o_ref, *, final):
    lane = lax.broadcasted_iota(jnp.int32, (COMBINE_ROWS, CAP), 1)
    slots = slot_ref[0]
    onehot = jnp.concatenate(
        [jnp.where(slots[:, e:e + 1] == lane, 1.0, 0.0).astype(BF16) for e in range(N_EXPERTS)], axis=1)
    moe = jnp.dot(onehot, y_ref[...].reshape(N_EXPERTS * CAP, D_MODEL), preferred_element_type=F32)
    x = x_ref[0] + gate_ref[0] * moe
    if final:
        x = (x * lax.rsqrt(jnp.mean(x * x, axis=-1, keepdims=True) + RMS_EPS)) * g_ref[...]
    o_ref[0] = x


def _combine(slot_t, y, x, mod, final_g, layer, final):
    return pl.pallas_call(
        functools.partial(_combine_kernel, final=final),
        grid=(BATCH, SEQ // COMBINE_ROWS),
        in_specs=[
            pl.BlockSpec((1, COMBINE_ROWS, N_EXPERTS), lambda b, t: (b, t, 0)),
            pl.BlockSpec((N_EXPERTS, 1, CAP, D_MODEL), lambda b, t: (0, b, 0, 0)),
            pl.BlockSpec((1, COMBINE_ROWS, D_MODEL), lambda b, t: (b, t, 0)),
            pl.BlockSpec((1, 1, D_MODEL), lambda b, t: ((layer * 6 + 5) * BATCH + b, 0, 0)),
            pl.BlockSpec((1, D_MODEL), lambda b, t: (0, 0)),
        ],
        out_specs=pl.BlockSpec((1, COMBINE_ROWS, D_MODEL), lambda b, t: (b, t, 0)),
        out_shape=jax.ShapeDtypeStruct((BATCH, SEQ, D_MODEL), F32),
        compiler_params=_params(("arbitrary", "arbitrary"), 48),
        name="moe_combine",
    )(slot_t, y, x, mod, final_g.reshape(1, D_MODEL))


def _na_qkv_weight(w):
    nq = NA_HEADS * HEAD_DIM
    scale = jnp.concatenate([jnp.full((nq,), HEAD_DIM ** -0.5, F32), jnp.ones((2 * nq,), F32)])
    return (w * scale).astype(BF16)


def _sw_qkv_weight(w):
    nq = SW_Q_HEADS * HEAD_DIM
    nkv = SW_KV_HEADS * HEAD_DIM
    dup = lambda t: jnp.concatenate([t.reshape(D_MODEL, SW_KV_HEADS, 1, HEAD_DIM)] * 2, axis=2).reshape(
        D_MODEL, 2 * nkv)
    wq = w[:, :nq] * HEAD_DIM ** -0.5
    return jnp.concatenate([wq, dup(w[:, nq:nq + nkv]), dup(w[:, nq + nkv:])], axis=1).astype(BF16)


def kernel(x, c, ada_w, ada_b, norm_g, na_w_qkv, na_w_o, na_rpb, sw_w_qkv, sw_w_o, sw_sinks, t5_bias,
           moe_w_router, moe_w_gate, moe_w_up, moe_w_down, final_g):
    mod = _ada(c, ada_w, ada_b)
    norm_g3 = norm_g.reshape(DEPTH * 2, 1, D_MODEL)
    x = x.reshape(BATCH * SEQ, D_MODEL)
    for layer in range(DEPTH):
        j = layer // N_MIXERS
        if layer % N_MIXERS == 0:
            qkv = _qkv(x, norm_g3, mod, _na_qkv_weight(na_w_qkv[j]), layer)
            o = _na_attention(qkv.reshape(BATCH, SEQ, -1), _na_rpb_rows(na_rpb[j]))
            w_o = na_w_o[j]
        else:
            qkv = _qkv(x, norm_g3, mod, _sw_qkv_weight(sw_w_qkv[j]), layer)
            o = _sw_attention(qkv.reshape(BATCH, SEQ, -1), _sw_rel_table(t5_bias), sw_sinks[j])
            w_o = sw_w_o[j]
        w_router = moe_w_router[layer].astype(BF16)
        x, h, aff_t = _post_attn(o.reshape(BATCH * SEQ, -1), w_o.astype(BF16), x, norm_g3, mod,
                                 w_router.T, layer)
        slot, tok, gate = _route(aff_t)
        rows = tok.transpose(1, 0, 2) + (jnp.arange(BATCH, dtype=jnp.int32) * SEQ)[None, :, None]
        y = _experts(rows.reshape(-1), h, gate.transpose(1, 2, 0), moe_w_gate, moe_w_up, moe_w_down, layer)
        slot_t = slot.reshape(N_EXPERTS, BATCH, SEQ).transpose(1, 2, 0)
        x = _combine(slot_t, y, x.reshape(BATCH, SEQ, D_MODEL), mod, final_g, layer,
                     final=layer == DEPTH - 1).reshape(BATCH * SEQ, D_MODEL)
    return x.reshape(BATCH, SEQ, D_MODEL)
```

```python
import functools

import numpy as np
import jax
import jax.numpy as jnp
from jax import lax
from jax.experimental import pallas as pl
from jax.experimental.pallas import tpu as pltpu

D_MODEL = 1024
BATCH = 8
SEQ = 2048
DEPTH = 2
GRID_W = 64
ROWS = SEQ // GRID_W
N_MIXERS = 2
HEAD_DIM = 64
NA_HEADS = 16
NA_WIN_H = 8
NA_WIN_W = 16
SW_Q_HEADS = 16
SW_KV_HEADS = 4
SW_GROUP = SW_Q_HEADS // SW_KV_HEADS
SW_WINDOW = 128
SW_BLOCK = 128
SW_NB = SEQ // SW_BLOCK
SW_SPAN = 3 * SW_BLOCK
T5_BUCKETS = 32
T5_MAX_DIST = 128
N_EXPERTS = 16
EXPERT_FF = 2048
EC_CAPACITY = 2
CAP = EC_CAPACITY * SEQ // N_EXPERTS
RMS_EPS = 1e-6
NEG = -1e30

LANES = 128
TOKEN_TILE_ROWS = D_MODEL // LANES
MIB = 1024 * 1024
F32 = jnp.float32
BF16 = jnp.bfloat16

ROW_TILE = 512
FF_TILE = 512
COMBINE_ROWS = 512
PREFIX_CHUNK = 256


def _params(semantics, vmem_mib, flags=None):
    return pltpu.CompilerParams(dimension_semantics=semantics, vmem_limit_bytes=vmem_mib * MIB, flags=flags)


def _norm_mod(x, g, sc, sh):
    y = x * lax.rsqrt(jnp.mean(x * x, axis=-1, keepdims=True) + RMS_EPS)
    return (y * g) * (1.0 + sc) + sh


def _softmax0(z):
    z = z - jnp.max(z, axis=0, keepdims=True)
    p = jnp.exp(z)
    return p / jnp.sum(p, axis=0, keepdims=True)


def _half_masks(rows):
    lane = lax.broadcasted_iota(jnp.int32, (rows, LANES), 1)
    lo = jnp.where(lane < HEAD_DIM, 1.0, 0.0).astype(BF16)
    hi = jnp.where(lane < HEAD_DIM, 0.0, 1.0).astype(BF16)
    return lo, hi


def _ada_kernel(c_ref, w_ref, b_ref, o_ref):
    c = c_ref[...]
    act = (c * jax.nn.sigmoid(c)).astype(BF16)
    o_ref[0] = jnp.dot(act, w_ref[0].astype(BF16), preferred_element_type=F32) + b_ref[0]


def _ada(c, ada_w, ada_b):
    out = pl.pallas_call(
        _ada_kernel,
        grid=(DEPTH, 6),
        in_specs=[
            pl.BlockSpec((BATCH, D_MODEL), lambda l, k: (0, 0)),
            pl.BlockSpec((1, D_MODEL, D_MODEL), lambda l, k: (l, 0, k)),
            pl.BlockSpec((1, 1, D_MODEL), lambda l, k: (l * 6 + k, 0, 0)),
        ],
        out_specs=pl.BlockSpec((1, BATCH, D_MODEL), lambda l, k: (l * 6 + k, 0, 0)),
        out_shape=jax.ShapeDtypeStruct((DEPTH * 6, BATCH, D_MODEL), F32),
        compiler_params=_params(("arbitrary", "arbitrary"), 32),
        name="ada_mod",
    )(c, ada_w, ada_b.reshape(DEPTH * 6, 1, D_MODEL))
    return out.reshape(DEPTH * 6 * BATCH, 1, D_MODEL)


def _mod_spec(layer, chunk):
    tiles_per_seq = SEQ // ROW_TILE
    return pl.BlockSpec((1, 1, D_MODEL),
                        lambda i: ((layer * 6 + chunk) * BATCH + i // tiles_per_seq, 0, 0))


def _qkv_kernel(x_ref, g_ref, sc_ref, sh_ref, w_ref, o_ref):
    h = _norm_mod(x_ref[...], g_ref[0], sc_ref[0], sh_ref[0])
    o_ref[...] = jnp.dot(h.astype(BF16), w_ref[...], preferred_element_type=F32).astype(BF16)


def _qkv(x, norm_g3, mod, w, layer):
    n = w.shape[1]
    return pl.pallas_call(
        _qkv_kernel,
        grid=(BATCH * SEQ // ROW_TILE,),
        in_specs=[
            pl.BlockSpec((ROW_TILE, D_MODEL), lambda i: (i, 0)),
            pl.BlockSpec((1, 1, D_MODEL), lambda i: (layer * 2, 0, 0)),
            _mod_spec(layer, 1),
            _mod_spec(layer, 0),
            pl.BlockSpec((D_MODEL, n), lambda i: (0, 0)),
        ],
        out_specs=pl.BlockSpec((ROW_TILE, n), lambda i: (i, 0)),
        out_shape=jax.ShapeDtypeStruct((BATCH * SEQ, n), BF16),
        compiler_params=_params(("arbitrary",), 48),
        name="norm_qkv",
    )(x, norm_g3, mod, mod, w)


NA_BIAS_ROWS = 2 * NA_WIN_H - 1
NA_ROW_UNROLL = 4


def _na_rpb_rows(rpb):
    w = NA_WIN_W - 1
    pad = jnp.zeros(rpb.shape[:2] + (LANES - 2 * w - 1,), F32)
    return jnp.concatenate([rpb[..., w:].astype(F32), pad, rpb[..., :w].astype(F32)], axis=-1)


def _na_build_bias(w_ref, bias_ref):
    lane = lax.broadcasted_iota(jnp.int32, (GRID_W, LANES), 1)
    col = lax.broadcasted_iota(jnp.int32, (GRID_W, LANES), 0)
    kc = jnp.bitwise_and(lane, GRID_W - 1)
    cstart = jnp.clip(col - NA_WIN_W // 2, 0, GRID_W - NA_WIN_W)
    in_window = jnp.where(kc >= cstart, jnp.where(kc < cstart + NA_WIN_W, 1.0, 0.0), 0.0) > 0.5
    low_half = lane < GRID_W

    def toeplitz(head, a, shift):
        row = jnp.broadcast_to(w_ref[head, a:a + 1, :], (GRID_W, LANES))
        return pltpu.roll(row, shift, 1, stride=1, stride_axis=0)

    for head in range(2):
        tiles = [jnp.where(in_window, jnp.where(low_half, toeplitz(head, a, 0), toeplitz(head, a + 1, GRID_W)), NEG)
                 for a in range(NA_BIAS_ROWS - 1)]
        for d in range(NA_WIN_H):
            for m in range(NA_WIN_H // 2):
                bias_ref[d, head * GRID_W:(head + 1) * GRID_W, m * LANES:(m + 1) * LANES] = (
                    tiles[NA_WIN_H - 1 - d + 2 * m])


def _na_kernel(w_ref, q_ref, k_ref, v_ref, o_ref, bias_ref, p_ref, s0_ref, s1_ref):
    @pl.when(pl.program_id(1) == 0)
    def _():
        _na_build_bias(w_ref, bias_ref)

    lo, hi = _half_masks(GRID_W)
    lo_f = lax.broadcasted_iota(jnp.int32, (GRID_W, LANES), 1) < HEAD_DIM
    span = NA_WIN_H * GRID_W
    ones = jnp.ones((span, LANES), BF16)

    def pair_rows(r):
        return pl.ds(pl.multiple_of(r * 2 * GRID_W, 2 * GRID_W), 2 * GRID_W)

    def group_scores(i, s_ref):
        for u in range(NA_ROW_UNROLL):
            r = jnp.minimum(i * NA_ROW_UNROLL + u, ROWS - 1)
            rs = jnp.clip(r - NA_WIN_H // 2, 0, ROWS - NA_WIN_H)
            q = q_ref[0, pl.ds(pl.multiple_of(r * GRID_W, GRID_W), GRID_W), :]
            lhs = jnp.concatenate([q * lo, q * hi], axis=0)
            kw = k_ref[0, pl.ds(pl.multiple_of(rs * GRID_W, GRID_W), span), :]
            s_ref[u] = lax.dot_general(lhs, kw, (((1,), (1,)), ((), ())), preferred_element_type=F32)

    def group_softmax(i, s_ref):
        for u in range(NA_ROW_UNROLL):
            r = i * NA_ROW_UNROLL + u
            rs = jnp.clip(r - NA_WIN_H // 2, 0, ROWS - NA_WIN_H)
            s = s_ref[u] + bias_ref[r - rs]
            m = jnp.broadcast_to(jnp.max(s, axis=-1, keepdims=True), (2 * GRID_W, LANES))
            p = jnp.exp(s - jnp.concatenate([m] * (span // LANES), axis=1))
            p_ref[pair_rows(r), :] = p.astype(BF16)

    def softmax_step(i, carry):
        group_scores(2 * i + 1, s1_ref)
        group_softmax(2 * i, s0_ref)
        group_scores(2 * i + 2, s0_ref)
        group_softmax(2 * i + 1, s1_ref)
        return carry

    def value_step(r, carry):
        rs = jnp.clip(r - NA_WIN_H // 2, 0, ROWS - NA_WIN_H)
        vw = v_ref[0, pl.ds(pl.multiple_of(rs * GRID_W, GRID_W), span), :]
        o = jnp.dot(p_ref[pair_rows(r), :], jnp.concatenate([vw, ones], axis=1), preferred_element_type=F32)
        o = o[:, :LANES] / o[:, LANES:]
        out = jnp.where(lo_f, o[:GRID_W], o[GRID_W:])
        o_ref[0, pl.ds(pl.multiple_of(r * GRID_W, GRID_W), GRID_W), :] = out.astype(BF16)
        return carry

    group_scores(0, s0_ref)
    lax.fori_loop(0, ROWS // (2 * NA_ROW_UNROLL), softmax_step, 0)
    lax.fori_loop(0, ROWS, value_step, 0, unroll=NA_ROW_UNROLL)


def _na_attention(qkv, rpb_rows):
    pairs = NA_HEADS // 2
    return pl.pallas_call(
        _na_kernel,
        grid=(pairs, BATCH),
        in_specs=[
            pl.BlockSpec((2, NA_BIAS_ROWS, LANES), lambda p, b: (p, 0, 0)),
            pl.BlockSpec((1, SEQ, LANES), lambda p, b: (b, 0, p)),
            pl.BlockSpec((1, SEQ, LANES), lambda p, b: (b, 0, pairs + p)),
            pl.BlockSpec((1, SEQ, LANES), lambda p, b: (b, 0, 2 * pairs + p)),
        ],
        out_specs=pl.BlockSpec((1, SEQ, LANES), lambda p, b: (b, 0, p)),
        out_shape=jax.ShapeDtypeStruct((BATCH, SEQ, NA_HEADS * HEAD_DIM), BF16),
        scratch_shapes=[pltpu.VMEM((NA_WIN_H, 2 * GRID_W, NA_WIN_H * GRID_W), F32),
                        pltpu.VMEM((2 * SEQ, NA_WIN_H * GRID_W), BF16),
                        pltpu.VMEM((NA_ROW_UNROLL, 2 * GRID_W, NA_WIN_H * GRID_W), F32),
                        pltpu.VMEM((NA_ROW_UNROLL, 2 * GRID_W, NA_WIN_H * GRID_W), F32)],
        compiler_params=_params(("arbitrary", "arbitrary"), 32),
        name="na_attention",
    )(rpb_rows, qkv, qkv, qkv)


def _t5_buckets(rel):
    half = T5_BUCKETS // 2
    max_exact = half // 2
    n = np.abs(rel)
    large = max_exact + (np.log(np.maximum(n, 1) / max_exact)
                         / np.log(T5_MAX_DIST / max_exact) * (half - max_exact)).astype(np.int32)
    large = np.minimum(large, half - 1)
    return (rel > 0).astype(np.int32) * half + np.where(n < max_exact, n, large)


SW_REL_PERIOD = 512
SW_BLOCK_UNROLL = 2


def _sw_rel_table(t5_table):
    k = np.arange(SW_REL_PERIOD)
    rel = np.where(k < SW_REL_PERIOD // 2, k, k - SW_REL_PERIOD)
    ok = np.abs(rel) <= SW_WINDOW
    vals = jnp.where(ok[:, None], t5_table[_t5_buckets(rel)].astype(F32), NEG)
    return vals.T.reshape(SW_KV_HEADS, SW_GROUP, SW_REL_PERIOD)


def _sw_build_bias(rel_ref, bias_ref):
    for g in range(SW_GROUP):
        row = jnp.broadcast_to(rel_ref[0, g:g + 1, :], (SW_BLOCK, SW_REL_PERIOD))
        t = pltpu.roll(row, 0, 1, stride=1, stride_axis=0)
        rows = slice(g * SW_BLOCK, (g + 1) * SW_BLOCK)
        bias_ref[0, rows, :] = t[:, :SW_SPAN]
        bias_ref[1, rows, :] = jnp.concatenate([t[:, SW_SPAN:], t[:, :2 * SW_BLOCK]], axis=1)
        bias_ref[2, rows, :] = jnp.concatenate([t[:, 2 * SW_BLOCK:], t[:, :SW_BLOCK]], axis=1)


def _sw_kernel(sink_ref, rel_ref, q_ref, k_ref, v_ref, o_ref, bias_ref, p_ref, sink_term_ref):
    j = pl.program_id(0)
    ones = jnp.ones((SW_SPAN, LANES), BF16)

    @pl.when(pl.program_id(1) == 0)
    def _():
        _sw_build_bias(rel_ref, bias_ref)

    masks = _half_masks(SW_BLOCK)
    lo_f = lax.broadcasted_iota(jnp.int32, (SW_BLOCK, LANES), 1) < HEAD_DIM
    sinks = [jnp.full((SW_BLOCK, LANES), sink_ref[j * SW_GROUP + g], F32) for g in range(SW_GROUP)]

    def chain_rows(n, g):
        return pl.ds(pl.multiple_of((n * SW_GROUP + g) * SW_BLOCK, SW_BLOCK), SW_BLOCK)

    def softmax_step(n, carry):
        first = jnp.clip(n - 1, 0, SW_NB - 3)
        rows = pl.ds(pl.multiple_of(n * SW_BLOCK, SW_BLOCK), SW_BLOCK)
        kw = k_ref[0, pl.ds(pl.multiple_of(first * SW_BLOCK, SW_BLOCK), SW_SPAN), :]
        for g in range(SW_GROUP):
            q = q_ref[0, rows, (g // 2) * LANES:(g // 2 + 1) * LANES] * masks[g % 2]
            s = lax.dot_general(q, kw, (((1,), (1,)), ((), ())), preferred_element_type=F32)
            s = s + bias_ref[n - first, g * SW_BLOCK:(g + 1) * SW_BLOCK, :]
            m = jnp.maximum(jnp.broadcast_to(jnp.max(s, axis=-1, keepdims=True), (SW_BLOCK, LANES)), sinks[g])
            p = jnp.exp(s - jnp.concatenate([m] * (SW_SPAN // LANES), axis=1))
            p_ref[chain_rows(n, g), :] = p.astype(BF16)
            sink_term_ref[chain_rows(n, g), :] = jnp.exp(sinks[g] - m)
        return carry

    def value_step(n, carry):
        first = jnp.clip(n - 1, 0, SW_NB - 3)
        vw = v_ref[0, pl.ds(pl.multiple_of(first * SW_BLOCK, SW_BLOCK), SW_SPAN), :]
        v_ones = jnp.concatenate([vw, ones], axis=1)
        heads = []
        for g in range(SW_GROUP):
            o = jnp.dot(p_ref[chain_rows(n, g), :], v_ones, preferred_element_type=F32)
            heads.append(o[:, :LANES] / (o[:, LANES:] + sink_term_ref[chain_rows(n, g), :]))
        out = jnp.concatenate([jnp.where(lo_f, heads[0], heads[1]), jnp.where(lo_f, heads[2], heads[3])], axis=1)
        o_ref[0, pl.ds(pl.multiple_of(n * SW_BLOCK, SW_BLOCK), SW_BLOCK), :] = out.astype(BF16)
        return carry

    lax.fori_loop(0, SW_NB, softmax_step, 0, unroll=SW_BLOCK_UNROLL)
    lax.fori_loop(0, SW_NB, value_step, 0, unroll=SW_BLOCK_UNROLL)


def _sw_attention(qkv, rel_table, sinks):
    qw = SW_GROUP * HEAD_DIM
    k_off = SW_Q_HEADS * HEAD_DIM // LANES
    v_off = k_off + SW_KV_HEADS
    return pl.pallas_call(
        _sw_kernel,
        grid=(SW_KV_HEADS, BATCH),
        in_specs=[
            pl.BlockSpec(memory_space=pltpu.SMEM),
            pl.BlockSpec((1, SW_GROUP, SW_REL_PERIOD), lambda j, b: (j, 0, 0)),
            pl.BlockSpec((1, SEQ, qw), lambda j, b: (b, 0, j)),
            pl.BlockSpec((1, SEQ, LANES), lambda j, b: (b, 0, k_off + j)),
            pl.BlockSpec((1, SEQ, LANES), lambda j, b: (b, 0, v_off + j)),
        ],
        out_specs=pl.BlockSpec((1, SEQ, qw), lambda j, b: (b, 0, j)),
        out_shape=jax.ShapeDtypeStruct((BATCH, SEQ, SW_Q_HEADS * HEAD_DIM), BF16),
        scratch_shapes=[pltpu.VMEM((3, SW_GROUP * SW_BLOCK, SW_SPAN), F32),
                        pltpu.VMEM((SW_GROUP * SEQ, SW_SPAN), BF16),
                        pltpu.VMEM((SW_GROUP * SEQ, LANES), F32)],
        compiler_params=_params(("arbitrary", "arbitrary"), 32),
        name="sw_attention",
    )(sinks, rel_table, qkv, qkv, qkv)


def _post_attn_kernel(o_ref, wo_ref, x_ref, gate_ref, g_ref, sc_ref, sh_ref, wr_ref,
                      xo_ref, h_ref, aff_ref):
    y = jnp.dot(o_ref[...], wo_ref[...], preferred_element_type=F32)
    xn = x_ref[...] + gate_ref[0] * y
    xo_ref[...] = xn
    h = _norm_mod(xn, g_ref[0], sc_ref[0], sh_ref[0])
    for j in range(TOKEN_TILE_ROWS):
        h_ref[pl.ds(j, ROW_TILE, stride=TOKEN_TILE_ROWS), :] = h[:, j * LANES:(j + 1) * LANES]
    hb = h.astype(BF16)
    logits = lax.dot_general(wr_ref[...], hb, (((1,), (1,)), ((), ())), preferred_element_type=F32)
    aff_ref[...] = _softmax0(logits)


def _post_attn(o, w_o, x, norm_g3, mod, w_router_t, layer):
    rows = BATCH * SEQ
    return pl.pallas_call(
        _post_attn_kernel,
        grid=(rows // ROW_TILE,),
        in_specs=[
            pl.BlockSpec((ROW_TILE, D_MODEL), lambda i: (i, 0)),
            pl.BlockSpec((D_MODEL, D_MODEL), lambda i: (0, 0)),
            pl.BlockSpec((ROW_TILE, D_MODEL), lambda i: (i, 0)),
            _mod_spec(layer, 2),
            pl.BlockSpec((1, 1, D_MODEL), lambda i: (layer * 2 + 1, 0, 0)),
            _mod_spec(layer, 4),
            _mod_spec(layer, 3),
            pl.BlockSpec((N_EXPERTS, D_MODEL), lambda i: (0, 0)),
        ],
        out_specs=[
            pl.BlockSpec((ROW_TILE, D_MODEL), lambda i: (i, 0)),
            pl.BlockSpec((ROW_TILE * TOKEN_TILE_ROWS, LANES), lambda i: (i, 0)),
            pl.BlockSpec((N_EXPERTS, ROW_TILE), lambda i: (0, i)),
        ],
        out_shape=[
            jax.ShapeDtypeStruct((rows, D_MODEL), F32),
            jax.ShapeDtypeStruct((rows * TOKEN_TILE_ROWS, LANES), F32),
            jax.ShapeDtypeStruct((N_EXPERTS, rows), F32),
        ],
        compiler_params=_params(("arbitrary",), 48),
        name="post_attn",
    )(o, w_o, x, mod, norm_g3, mod, mod, w_router_t)


def _prefix_count(x):
    nchunk = SEQ // PREFIX_CHUNK
    r = lax.broadcasted_iota(jnp.int32, (PREFIX_CHUNK, PREFIX_CHUNK), 0)
    c = lax.broadcasted_iota(jnp.int32, (PREFIX_CHUNK, PREFIX_CHUNK), 1)
    upper = jnp.where(r < c, 1.0, 0.0).astype(BF16)
    chunks = [x[:, k * PREFIX_CHUNK:(k + 1) * PREFIX_CHUNK] for k in range(nchunk)]
    local = jnp.dot(jnp.concatenate(chunks, axis=0).astype(BF16), upper, preferred_element_type=F32)
    out = []
    offset = jnp.zeros((N_EXPERTS, 1), F32)
    for k in range(nchunk):
        out.append(local[k * N_EXPERTS:(k + 1) * N_EXPERTS] + offset)
        offset = offset + jnp.sum(chunks[k], axis=1, keepdims=True)
    return jnp.concatenate(out, axis=1)


def _route_kernel(aff_ref, slot_ref, tok_ref, gate_ref):
    aff = aff_ref[...]
    bits = pltpu.bitcast(aff, jnp.int32)

    def count(mask):
        return jnp.sum(jnp.where(mask, 1.0, 0.0), axis=1, keepdims=True)

    def search(_, bounds):
        lo, hi = bounds
        mid = lo + ((hi - lo) >> 1)
        ge = count(bits >= mid) >= CAP
        return jnp.where(ge, mid, lo), jnp.where(ge, hi, mid)

    lo0 = jnp.zeros((N_EXPERTS, 1), jnp.int32)
    hi0 = jnp.full((N_EXPERTS, 1), 0x7F800000, jnp.int32)
    tau, _ = lax.fori_loop(0, 31, search, (lo0, hi0))
    gt = bits > tau
    eq = jnp.where(bits == tau, 1.0, 0.0)
    need = CAP - count(gt)
    sel = jnp.where(gt, 1.0, jnp.where(_prefix_count(eq) < need, eq, 0.0))
    pos = _prefix_count(sel).astype(jnp.int32)
    chosen = sel > 0.5
    slot_ref[...] = jnp.where(chosen, pos, -1)
    lane = lax.broadcasted_iota(jnp.int32, (N_EXPERTS, SEQ), 1)
    packed = jnp.where(chosen, ROUTE_VALID | (lane << ROUTE_BITS) | (lane - pos), 0)
    gate = aff
    for k in range(ROUTE_BITS):
        step = 1 << k
        from_right = pltpu.roll(packed, SEQ - step, 1)
        gate_right = pltpu.roll(gate, SEQ - step, 1)
        arrives = (from_right & step) != 0
        stays = (packed & step) == 0
        packed = jnp.where(arrives, from_right, jnp.where(stays, packed, 0))
        gate = jnp.where(arrives, gate_right, gate)
    tok_ref[0] = (packed[:, :CAP] >> ROUTE_BITS) & (SEQ - 1)
    gate_ref[0] = gate[:, :CAP]


ROUTE_BITS = SEQ.bit_length() - 1
ROUTE_VALID = 1 << (2 * ROUTE_BITS)


def _route(aff_t):
    return pl.pallas_call(
        _route_kernel,
        grid=(BATCH,),
        in_specs=[pl.BlockSpec((N_EXPERTS, SEQ), lambda b: (0, b))],
        out_specs=[pl.BlockSpec((N_EXPERTS, SEQ), lambda b: (0, b)),
                   pl.BlockSpec((1, N_EXPERTS, CAP), lambda b: (b, 0, 0)),
                   pl.BlockSpec((1, N_EXPERTS, CAP), lambda b: (b, 0, 0))],
        out_shape=[jax.ShapeDtypeStruct((N_EXPERTS, BATCH * SEQ), jnp.int32),
                   jax.ShapeDtypeStruct((BATCH, N_EXPERTS, CAP), jnp.int32),
                   jax.ShapeDtypeStruct((BATCH, N_EXPERTS, CAP), F32)],
        compiler_params=_params(("arbitrary",), 32),
        name="route",
    )(aff_t)


GATHER_ROWS = BATCH * CAP
FF_STEPS = EXPERT_FF // FF_TILE
ROWS_PER_CHUNK = GATHER_ROWS // (FF_STEPS * BATCH)
PROLOGUE_UNROLL = 8


def _expert_kernel(tok_ref, h_ref, gates_ref, wg_ref, wu_ref, wd_ref, y_ref, x_ref, sem, xb_ref, acc_ref):
    e = pl.program_id(0)
    f = pl.program_id(1)
    slot = lax.rem(e, 2)
    next_slot = 1 - slot
    next_e = jnp.minimum(e + 1, N_EXPERTS - 1)

    tile = TOKEN_TILE_ROWS

    def row_copy(expert, row, dst_slot):
        src = pl.multiple_of(tok_ref[expert * GATHER_ROWS + row] * tile, tile)
        dst = pl.multiple_of(row * tile, tile)
        return pltpu.make_async_copy(h_ref.at[pl.ds(src, tile), :], x_ref.at[dst_slot, pl.ds(dst, tile), :],
                                     sem.at[dst_slot])

    def wait_rows(dst_slot):
        pltpu.make_async_copy(h_ref.at[pl.ds(0, GATHER_ROWS * tile), :], x_ref.at[dst_slot],
                              sem.at[dst_slot]).wait()

    @pl.when(jnp.logical_and(e == 0, f == 0))
    def _():
        def issue(i, carry):
            for u in range(PROLOGUE_UNROLL):
                row_copy(0, i * PROLOGUE_UNROLL + u, 0).start()
            return carry
        lax.fori_loop(0, GATHER_ROWS // PROLOGUE_UNROLL, issue, 0)

    @pl.when(f == 0)
    def _():
        wait_rows(slot)
        acc_ref[...] = jnp.zeros_like(acc_ref)
        for b in range(BATCH):
            cols = [x_ref[slot, pl.ds(b * CAP * tile + j, CAP, stride=tile), :] for j in range(tile)]
            xb_ref[b] = jnp.concatenate(cols, axis=1).astype(BF16)

    wg = wg_ref[0, 0].astype(BF16)
    wu = wu_ref[0, 0].astype(BF16)
    wd = wd_ref[0, 0].astype(BF16)
    for b in range(BATCH):
        first = (f * BATCH + b) * ROWS_PER_CHUNK
        for j in range(ROWS_PER_CHUNK):
            row_copy(next_e, first + j, next_slot).start()
        x = xb_ref[b]
        a = jnp.dot(x, wg, preferred_element_type=F32)
        u = jnp.dot(x, wu, preferred_element_type=F32)
        act = ((a * jax.nn.sigmoid(a)) * u).astype(BF16)
        acc_ref[b] += jnp.dot(act, wd, preferred_element_type=F32)

    @pl.when(f == FF_STEPS - 1)
    def _():
        gates = gates_ref[0]
        for b in range(BATCH):
            y_ref[0, b] = (acc_ref[b] * gates[:, b:b + 1]).astype(BF16)

    @pl.when(jnp.logical_and(e == N_EXPERTS - 1, f == FF_STEPS - 1))
    def _():
        wait_rows(next_slot)


def _experts(tok, h, gates, w_gate, w_up, w_down, layer):
    grid_spec = pltpu.PrefetchScalarGridSpec(
        num_scalar_prefetch=1,
        grid=(N_EXPERTS, FF_STEPS),
        in_specs=[
            pl.BlockSpec(memory_space=pl.ANY),
            pl.BlockSpec((1, CAP, BATCH), lambda e, f, tok: (e, 0, 0)),
            pl.BlockSpec((1, 1, D_MODEL, FF_TILE), lambda e, f, tok: (layer, e, 0, f)),
            pl.BlockSpec((1, 1, D_MODEL, FF_TILE), lambda e, f, tok: (layer, e, 0, f)),
            pl.BlockSpec((1, 1, FF_TILE, D_MODEL), lambda e, f, tok: (layer, e, f, 0)),
        ],
        out_specs=pl.BlockSpec((1, BATCH, CAP, D_MODEL), lambda e, f, tok: (e, 0, 0, 0)),
        scratch_shapes=[pltpu.VMEM((2, GATHER_ROWS * TOKEN_TILE_ROWS, LANES), F32), pltpu.SemaphoreType.DMA((2,)),
                        pltpu.VMEM((BATCH, CAP, D_MODEL), BF16), pltpu.VMEM((BATCH, CAP, D_MODEL), F32)],
    )
    return pl.pallas_call(
        _expert_kernel,
        grid_spec=grid_spec,
        out_shape=jax.ShapeDtypeStruct((N_EXPERTS, BATCH, CAP, D_MODEL), BF16),
        compiler_params=_params(("arbitrary", "arbitrary"), 60),
        name="moe_experts",
    )(tok, h, gates, w_gate, w_up, w_down)


def _combine_kernel(slot_ref, y_ref, x_ref, gate_ref, g_ref, o_ref, *, final):
    lane = lax.broadcasted_iota(jnp.int32, (COMBINE_ROWS, CAP), 1)
    slots = slot_ref[0]
    onehot = jnp.concatenate(
        [jnp.where(slots[:, e:e + 1] == lane, 1.0, 0.0).astype(BF16) for e in range(N_EXPERTS)], axis=1)
    moe = jnp.dot(onehot, y_ref[...].reshape(N_EXPERTS * CAP, D_MODEL), preferred_element_type=F32)
    x = x_ref[0] + gate_ref[0] * moe
    if final:
        x = (x * lax.rsqrt(jnp.mean(x * x, axis=-1, keepdims=True) + RMS_EPS)) * g_ref[...]
    o_ref[0] = x


def _combine(slot_t, y, x, mod, final_g, layer, final):
    return pl.pallas_call(
        functools.partial(_combine_kernel, final=final),
        grid=(BATCH, SEQ // COMBINE_ROWS),
        in_specs=[
            pl.BlockSpec((1, COMBINE_ROWS, N_EXPERTS), lambda b, t: (b, t, 0)),
            pl.BlockSpec((N_EXPERTS, 1, CAP, D_MODEL), lambda b, t: (0, b, 0, 0)),
            pl.BlockSpec((1, COMBINE_ROWS, D_MODEL), lambda b, t: (b, t, 0)),
            pl.BlockSpec((1, 1, D_MODEL), lambda b, t: ((layer * 6 + 5) * BATCH + b, 0, 0)),
            pl.BlockSpec((1, D_MODEL), lambda b, t: (0, 0)),
        ],
        out_specs=pl.BlockSpec((1, COMBINE_ROWS, D_MODEL), lambda b, t: (b, t, 0)),
        out_shape=jax.ShapeDtypeStruct((BATCH, SEQ, D_MODEL), F32),
        compiler_params=_params(("arbitrary", "arbitrary"), 48),
        name="moe_combine",
    )(slot_t, y, x, mod, final_g.reshape(1, D_MODEL))


def _na_qkv_weight(w):
    nq = NA_HEADS * HEAD_DIM
    scale = jnp.concatenate([jnp.full((nq,), HEAD_DIM ** -0.5, F32), jnp.ones((2 * nq,), F32)])
    return (w * scale).astype(BF16)


def _sw_qkv_weight(w):
    nq = SW_Q_HEADS * HEAD_DIM
    nkv = SW_KV_HEADS * HEAD_DIM
    dup = lambda t: jnp.concatenate([t.reshape(D_MODEL, SW_KV_HEADS, 1, HEAD_DIM)] * 2, axis=2).reshape(
        D_MODEL, 2 * nkv)
    wq = w[:, :nq] * HEAD_DIM ** -0.5
    return jnp.concatenate([wq, dup(w[:, nq:nq + nkv]), dup(w[:, nq + nkv:])], axis=1).astype(BF16)


def kernel(x, c, ada_w, ada_b, norm_g, na_w_qkv, na_w_o, na_rpb, sw_w_qkv, sw_w_o, sw_sinks, t5_bias,
           moe_w_router, moe_w_gate, moe_w_up, moe_w_down, final_g):
    mod = _ada(c, ada_w, ada_b)
    norm_g3 = norm_g.reshape(DEPTH * 2, 1, D_MODEL)
    x = x.reshape(BATCH * SEQ, D_MODEL)
    for layer in range(DEPTH):
        j = layer // N_MIXERS
        if layer % N_MIXERS == 0:
            qkv = _qkv(x, norm_g3, mod, _na_qkv_weight(na_w_qkv[j]), layer)
            o = _na_attention(qkv.reshape(BATCH, SEQ, -1), _na_rpb_rows(na_rpb[j]))
            w_o = na_w_o[j]
        else:
            qkv = _qkv(x, norm_g3, mod, _sw_qkv_weight(sw_w_qkv[j]), layer)
            o = _sw_attention(qkv.reshape(BATCH, SEQ, -1), _sw_rel_table(t5_bias), sw_sinks[j])
            w_o = sw_w_o[j]
        w_router = moe_w_router[layer].astype(BF16)
        x, h, aff_t = _post_attn(o.reshape(BATCH * SEQ, -1), w_o.astype(BF16), x, norm_g3, mod,
                                 w_router.T, layer)
        slot, tok, gate = _route(aff_t)
        rows = tok.transpose(1, 0, 2) + (jnp.arange(BATCH, dtype=jnp.int32) * SEQ)[None, :, None]
        y = _experts(rows.reshape(-1), h, gate.transpose(1, 2, 0), moe_w_gate, moe_w_up, moe_w_down, layer)
        slot_t = slot.reshape(N_EXPERTS, BATCH, SEQ).transpose(1, 2, 0)
        x = _combine(slot_t, y, x.reshape(BATCH, SEQ, D_MODEL), mod, final_g, layer,
                     final=layer == DEPTH - 1).reshape(BATCH * SEQ, D_MODEL)
    return x.reshape(BATCH, SEQ, D_MODEL)
```

```python
import functools

import numpy as np
import jax
import jax.numpy as jnp
from jax import lax
from jax.experimental import pallas as pl
from jax.experimental.pallas import tpu as pltpu

D_MODEL = 1024
BATCH = 8
SEQ = 2048
DEPTH = 2
GRID_W = 64
ROWS = SEQ // GRID_W
N_MIXERS = 2
HEAD_DIM = 64
NA_HEADS = 16
NA_WIN_H = 8
NA_WIN_W = 16
SW_Q_HEADS = 16
SW_KV_HEADS = 4
SW_GROUP = SW_Q_HEADS // SW_KV_HEADS
SW_WINDOW = 128
SW_BLOCK = 128
SW_NB = SEQ // SW_BLOCK
SW_SPAN = 3 * SW_BLOCK
T5_BUCKETS = 32
T5_MAX_DIST = 128
N_EXPERTS = 16
EXPERT_FF = 2048
EC_CAPACITY = 2
CAP = EC_CAPACITY * SEQ // N_EXPERTS
RMS_EPS = 1e-6
NEG = -1e30
LOG2E = 1.4426950408889634
Q_SCALE = HEAD_DIM ** -0.5 * LOG2E

LANES = 128
TOKEN_TILE_ROWS = D_MODEL // LANES
MIB = 1024 * 1024
F32 = jnp.float32
BF16 = jnp.bfloat16

ROW_TILE = 512
FF_TILE = 512
COMBINE_ROWS = 512
PREFIX_CHUNK = 256


def _params(semantics, vmem_mib, flags=None):
    return pltpu.CompilerParams(dimension_semantics=semantics, vmem_limit_bytes=vmem_mib * MIB, flags=flags)


def _norm_mod(x, g, sc, sh):
    y = x * lax.rsqrt(jnp.mean(x * x, axis=-1, keepdims=True) + RMS_EPS)
    return (y * g) * (1.0 + sc) + sh


def _softmax0(z):
    z = z - jnp.max(z, axis=0, keepdims=True)
    p = jnp.exp(z)
    return p / jnp.sum(p, axis=0, keepdims=True)


def _half_masks(rows):
    lane = lax.broadcasted_iota(jnp.int32, (rows, LANES), 1)
    lo = jnp.where(lane < HEAD_DIM, 1.0, 0.0).astype(BF16)
    hi = jnp.where(lane < HEAD_DIM, 0.0, 1.0).astype(BF16)
    return lo, hi


def _ada_kernel(c_ref, w_ref, b_ref, o_ref):
    c = c_ref[...]
    act = (c * jax.nn.sigmoid(c)).astype(BF16)
    o_ref[0] = jnp.dot(act, w_ref[0].astype(BF16), preferred_element_type=F32) + b_ref[0]


def _ada(c, ada_w, ada_b):
    out = pl.pallas_call(
        _ada_kernel,
        grid=(DEPTH, 6),
        in_specs=[
            pl.BlockSpec((BATCH, D_MODEL), lambda l, k: (0, 0)),
            pl.BlockSpec((1, D_MODEL, D_MODEL), lambda l, k: (l, 0, k)),
            pl.BlockSpec((1, 1, D_MODEL), lambda l, k: (l * 6 + k, 0, 0)),
        ],
        out_specs=pl.BlockSpec((1, BATCH, D_MODEL), lambda l, k: (l * 6 + k, 0, 0)),
        out_shape=jax.ShapeDtypeStruct((DEPTH * 6, BATCH, D_MODEL), F32),
        compiler_params=_params(("arbitrary", "arbitrary"), 32),
        name="ada_mod",
    )(c, ada_w, ada_b.reshape(DEPTH * 6, 1, D_MODEL))
    return out.reshape(DEPTH * 6 * BATCH, 1, D_MODEL)


def _mod_spec(layer, chunk):
    tiles_per_seq = SEQ // ROW_TILE
    return pl.BlockSpec((1, 1, D_MODEL),
                        lambda i: ((layer * 6 + chunk) * BATCH + i // tiles_per_seq, 0, 0))


def _qkv_kernel(x_ref, g_ref, sc_ref, sh_ref, w_ref, o_ref):
    h = _norm_mod(x_ref[...], g_ref[0], sc_ref[0], sh_ref[0])
    o_ref[...] = jnp.dot(h.astype(BF16), w_ref[...], preferred_element_type=F32).astype(BF16)


def _qkv(x, norm_g3, mod, w, layer):
    n = w.shape[1]
    return pl.pallas_call(
        _qkv_kernel,
        grid=(BATCH * SEQ // ROW_TILE,),
        in_specs=[
            pl.BlockSpec((ROW_TILE, D_MODEL), lambda i: (i, 0)),
            pl.BlockSpec((1, 1, D_MODEL), lambda i: (layer * 2, 0, 0)),
            _mod_spec(layer, 1),
            _mod_spec(layer, 0),
            pl.BlockSpec((D_MODEL, n), lambda i: (0, 0)),
        ],
        out_specs=pl.BlockSpec((ROW_TILE, n), lambda i: (i, 0)),
        out_shape=jax.ShapeDtypeStruct((BATCH * SEQ, n), BF16),
        compiler_params=_params(("arbitrary",), 48),
        name="norm_qkv",
    )(x, norm_g3, mod, mod, w)


NA_BIAS_ROWS = 2 * NA_WIN_H - 1
NA_ROW_UNROLL = 4


def _na_rpb_rows(rpb):
    w = NA_WIN_W - 1
    rpb = rpb.astype(F32) * LOG2E
    pad = jnp.zeros(rpb.shape[:2] + (LANES - 2 * w - 1,), F32)
    return jnp.concatenate([rpb[..., w:], pad, rpb[..., :w]], axis=-1)


def _na_build_bias(w_ref, bias_ref):
    lane = lax.broadcasted_iota(jnp.int32, (GRID_W, LANES), 1)
    col = lax.broadcasted_iota(jnp.int32, (GRID_W, LANES), 0)
    kc = jnp.bitwise_and(lane, GRID_W - 1)
    cstart = jnp.clip(col - NA_WIN_W // 2, 0, GRID_W - NA_WIN_W)
    in_window = jnp.where(kc >= cstart, jnp.where(kc < cstart + NA_WIN_W, 1.0, 0.0), 0.0) > 0.5
    low_half = lane < GRID_W

    def toeplitz(head, a, shift):
        row = jnp.broadcast_to(w_ref[head, a:a + 1, :], (GRID_W, LANES))
        return pltpu.roll(row, shift, 1, stride=1, stride_axis=0)

    for head in range(2):
        tiles = [jnp.where(in_window, jnp.where(low_half, toeplitz(head, a, 0), toeplitz(head, a + 1, GRID_W)), NEG)
                 for a in range(NA_BIAS_ROWS - 1)]
        for d in range(NA_WIN_H):
            for m in range(NA_WIN_H // 2):
                bias_ref[d, head * GRID_W:(head + 1) * GRID_W, m * LANES:(m + 1) * LANES] = (
                    tiles[NA_WIN_H - 1 - d + 2 * m])


def _na_kernel(w_ref, q_ref, k_ref, v_ref, o_ref, bias_ref, p_ref, s0_ref, s1_ref):
    @pl.when(pl.program_id(1) == 0)
    def _():
        _na_build_bias(w_ref, bias_ref)

    lo, hi = _half_masks(GRID_W)
    lo_f = lax.broadcasted_iota(jnp.int32, (GRID_W, LANES), 1) < HEAD_DIM
    span = NA_WIN_H * GRID_W
    ones = jnp.ones((span, LANES), BF16)

    def pair_rows(r):
        return pl.ds(pl.multiple_of(r * 2 * GRID_W, 2 * GRID_W), 2 * GRID_W)

    def group_scores(i, s_ref):
        for u in range(NA_ROW_UNROLL):
            r = jnp.minimum(i * NA_ROW_UNROLL + u, ROWS - 1)
            rs = jnp.clip(r - NA_WIN_H // 2, 0, ROWS - NA_WIN_H)
            q = q_ref[0, pl.ds(pl.multiple_of(r * GRID_W, GRID_W), GRID_W), :]
            lhs = jnp.concatenate([q * lo, q * hi], axis=0)
            kw = k_ref[0, pl.ds(pl.multiple_of(rs * GRID_W, GRID_W), span), :]
            s_ref[u] = lax.dot_general(lhs, kw, (((1,), (1,)), ((), ())), preferred_element_type=F32)

    def group_softmax(i, s_ref):
        for u in range(NA_ROW_UNROLL):
            r = i * NA_ROW_UNROLL + u
            rs = jnp.clip(r - NA_WIN_H // 2, 0, ROWS - NA_WIN_H)
            s = s_ref[u] + bias_ref[r - rs]
            m = jnp.broadcast_to(jnp.max(s, axis=-1, keepdims=True), (2 * GRID_W, LANES))
            p = jnp.exp2(s - jnp.concatenate([m] * (span // LANES), axis=1))
            p_ref[pair_rows(r), :] = p.astype(BF16)

    def softmax_step(i, carry):
        group_scores(2 * i + 1, s1_ref)
        group_softmax(2 * i, s0_ref)
        group_scores(2 * i + 2, s0_ref)
        group_softmax(2 * i + 1, s1_ref)
        return carry

    def value_step(r, carry):
        rs = jnp.clip(r - NA_WIN_H // 2, 0, ROWS - NA_WIN_H)
        vw = v_ref[0, pl.ds(pl.multiple_of(rs * GRID_W, GRID_W), span), :]
        o = jnp.dot(p_ref[pair_rows(r), :], jnp.concatenate([vw, ones], axis=1), preferred_element_type=F32)
        o = o[:, :LANES] / o[:, LANES:]
        out = jnp.where(lo_f, o[:GRID_W], o[GRID_W:])
        o_ref[0, pl.ds(pl.multiple_of(r * GRID_W, GRID_W), GRID_W), :] = out.astype(BF16)
        return carry

    group_scores(0, s0_ref)
    lax.fori_loop(0, ROWS // (2 * NA_ROW_UNROLL), softmax_step, 0)
    lax.fori_loop(0, ROWS, value_step, 0, unroll=NA_ROW_UNROLL)


def _na_attention(qkv, rpb_rows):
    pairs = NA_HEADS // 2
    return pl.pallas_call(
        _na_kernel,
        grid=(pairs, BATCH),
        in_specs=[
            pl.BlockSpec((2, NA_BIAS_ROWS, LANES), lambda p, b: (p, 0, 0)),
            pl.BlockSpec((1, SEQ, LANES), lambda p, b: (b, 0, p)),
            pl.BlockSpec((1, SEQ, LANES), lambda p, b: (b, 0, pairs + p)),
            pl.BlockSpec((1, SEQ, LANES), lambda p, b: (b, 0, 2 * pairs + p)),
        ],
        out_specs=pl.BlockSpec((1, SEQ, LANES), lambda p, b: (b, 0, p)),
        out_shape=jax.ShapeDtypeStruct((BATCH, SEQ, NA_HEADS * HEAD_DIM), BF16),
        scratch_shapes=[pltpu.VMEM((NA_WIN_H, 2 * GRID_W, NA_WIN_H * GRID_W), F32),
                        pltpu.VMEM((2 * SEQ, NA_WIN_H * GRID_W), BF16),
                        pltpu.VMEM((NA_ROW_UNROLL, 2 * GRID_W, NA_WIN_H * GRID_W), F32),
                        pltpu.VMEM((NA_ROW_UNROLL, 2 * GRID_W, NA_WIN_H * GRID_W), F32)],
        compiler_params=_params(("arbitrary", "arbitrary"), 32),
        name="na_attention",
    )(rpb_rows, qkv, qkv, qkv)


def _t5_buckets(rel):
    half = T5_BUCKETS // 2
    max_exact = half // 2
    n = np.abs(rel)
    large = max_exact + (np.log(np.maximum(n, 1) / max_exact)
                         / np.log(T5_MAX_DIST / max_exact) * (half - max_exact)).astype(np.int32)
    large = np.minimum(large, half - 1)
    return (rel > 0).astype(np.int32) * half + np.where(n < max_exact, n, large)


SW_REL_PERIOD = 512
SW_BLOCK_UNROLL = 2


def _sw_rel_table(t5_table):
    k = np.arange(SW_REL_PERIOD)
    rel = np.where(k < SW_REL_PERIOD // 2, k, k - SW_REL_PERIOD)
    ok = np.abs(rel) <= SW_WINDOW
    vals = jnp.where(ok[:, None], t5_table[_t5_buckets(rel)].astype(F32) * LOG2E, NEG)
    return vals.T.reshape(SW_KV_HEADS, SW_GROUP, SW_REL_PERIOD)


def _sw_build_bias(rel_ref, bias_ref):
    for g in range(SW_GROUP):
        row = jnp.broadcast_to(rel_ref[0, g:g + 1, :], (SW_BLOCK, SW_REL_PERIOD))
        t = pltpu.roll(row, 0, 1, stride=1, stride_axis=0)
        rows = slice(g * SW_BLOCK, (g + 1) * SW_BLOCK)
        bias_ref[0, rows, :] = t[:, :SW_SPAN]
        bias_ref[1, rows, :] = jnp.concatenate([t[:, SW_SPAN:], t[:, :2 * SW_BLOCK]], axis=1)
        bias_ref[2, rows, :] = jnp.concatenate([t[:, 2 * SW_BLOCK:], t[:, :SW_BLOCK]], axis=1)


def _sw_kernel(sink_ref, rel_ref, q_ref, k_ref, v_ref, o_ref, bias_ref, p_ref, sink_term_ref):
    j = pl.program_id(0)
    ones = jnp.ones((SW_SPAN, LANES), BF16)

    @pl.when(pl.program_id(1) == 0)
    def _():
        _sw_build_bias(rel_ref, bias_ref)

    masks = _half_masks(SW_BLOCK)
    lo_f = lax.broadcasted_iota(jnp.int32, (SW_BLOCK, LANES), 1) < HEAD_DIM
    sinks = [jnp.full((SW_BLOCK, LANES), sink_ref[j * SW_GROUP + g] * LOG2E, F32) for g in range(SW_GROUP)]

    def chain_rows(n, g):
        return pl.ds(pl.multiple_of((n * SW_GROUP + g) * SW_BLOCK, SW_BLOCK), SW_BLOCK)

    def softmax_step(n, carry):
        first = jnp.clip(n - 1, 0, SW_NB - 3)
        rows = pl.ds(pl.multiple_of(n * SW_BLOCK, SW_BLOCK), SW_BLOCK)
        kw = k_ref[0, pl.ds(pl.multiple_of(first * SW_BLOCK, SW_BLOCK), SW_SPAN), :]
        for g in range(SW_GROUP):
            q = q_ref[0, rows, (g // 2) * LANES:(g // 2 + 1) * LANES] * masks[g % 2]
            s = lax.dot_general(q, kw, (((1,), (1,)), ((), ())), preferred_element_type=F32)
            s = s + bias_ref[n - first, g * SW_BLOCK:(g + 1) * SW_BLOCK, :]
            m = jnp.maximum(jnp.broadcast_to(jnp.max(s, axis=-1, keepdims=True), (SW_BLOCK, LANES)), sinks[g])
            p = jnp.exp2(s - jnp.concatenate([m] * (SW_SPAN // LANES), axis=1))
            p_ref[chain_rows(n, g), :] = p.astype(BF16)
            sink_term_ref[chain_rows(n, g), :] = jnp.exp2(sinks[g] - m)
        return carry

    def value_step(n, carry):
        first = jnp.clip(n - 1, 0, SW_NB - 3)
        vw = v_ref[0, pl.ds(pl.multiple_of(first * SW_BLOCK, SW_BLOCK), SW_SPAN), :]
        v_ones = jnp.concatenate([vw, ones], axis=1)
        heads = []
        for g in range(SW_GROUP):
            o = jnp.dot(p_ref[chain_rows(n, g), :], v_ones, preferred_element_type=F32)
            heads.append(o[:, :LANES] / (o[:, LANES:] + sink_term_ref[chain_rows(n, g), :]))
        out = jnp.concatenate([jnp.where(lo_f, heads[0], heads[1]), jnp.where(lo_f, heads[2], heads[3])], axis=1)
        o_ref[0, pl.ds(pl.multiple_of(n * SW_BLOCK, SW_BLOCK), SW_BLOCK), :] = out.astype(BF16)
        return carry

    lax.fori_loop(0, SW_NB, softmax_step, 0, unroll=SW_BLOCK_UNROLL)
    lax.fori_loop(0, SW_NB, value_step, 0, unroll=SW_BLOCK_UNROLL)


def _sw_attention(qkv, rel_table, sinks):
    qw = SW_GROUP * HEAD_DIM
    k_off = SW_Q_HEADS * HEAD_DIM // LANES
    v_off = k_off + SW_KV_HEADS
    return pl.pallas_call(
        _sw_kernel,
        grid=(SW_KV_HEADS, BATCH),
        in_specs=[
            pl.BlockSpec(memory_space=pltpu.SMEM),
            pl.BlockSpec((1, SW_GROUP, SW_REL_PERIOD), lambda j, b: (j, 0, 0)),
            pl.BlockSpec((1, SEQ, qw), lambda j, b: (b, 0, j)),
            pl.BlockSpec((1, SEQ, LANES), lambda j, b: (b, 0, k_off + j)),
            pl.BlockSpec((1, SEQ, LANES), lambda j, b: (b, 0, v_off + j)),
        ],
        out_specs=pl.BlockSpec((1, SEQ, qw), lambda j, b: (b, 0, j)),
        out_shape=jax.ShapeDtypeStruct((BATCH, SEQ, SW_Q_HEADS * HEAD_DIM), BF16),
        scratch_shapes=[pltpu.VMEM((3, SW_GROUP * SW_BLOCK, SW_SPAN), F32),
                        pltpu.VMEM((SW_GROUP * SEQ, SW_SPAN), BF16),
                        pltpu.VMEM((SW_GROUP * SEQ, LANES), F32)],
        compiler_params=_params(("arbitrary", "arbitrary"), 32),
        name="sw_attention",
    )(sinks, rel_table, qkv, qkv, qkv)


def _post_attn_kernel(o_ref, wo_ref, x_ref, gate_ref, g_ref, sc_ref, sh_ref, wr_ref,
                      xo_ref, h_ref, aff_ref):
    y = jnp.dot(o_ref[...], wo_ref[...], preferred_element_type=F32)
    xn = x_ref[...] + gate_ref[0] * y
    xo_ref[...] = xn
    h = _norm_mod(xn, g_ref[0], sc_ref[0], sh_ref[0])
    for j in range(TOKEN_TILE_ROWS):
        h_ref[pl.ds(j, ROW_TILE, stride=TOKEN_TILE_ROWS), :] = h[:, j * LANES:(j + 1) * LANES]
    hb = h.astype(BF16)
    logits = lax.dot_general(wr_ref[...], hb, (((1,), (1,)), ((), ())), preferred_element_type=F32)
    aff_ref[...] = _softmax0(logits)


def _post_attn(o, w_o, x, norm_g3, mod, w_router_t, layer):
    rows = BATCH * SEQ
    return pl.pallas_call(
        _post_attn_kernel,
        grid=(rows // ROW_TILE,),
        in_specs=[
            pl.BlockSpec((ROW_TILE, D_MODEL), lambda i: (i, 0)),
            pl.BlockSpec((D_MODEL, D_MODEL), lambda i: (0, 0)),
            pl.BlockSpec((ROW_TILE, D_MODEL), lambda i: (i, 0)),
            _mod_spec(layer, 2),
            pl.BlockSpec((1, 1, D_MODEL), lambda i: (layer * 2 + 1, 0, 0)),
            _mod_spec(layer, 4),
            _mod_spec(layer, 3),
            pl.BlockSpec((N_EXPERTS, D_MODEL), lambda i: (0, 0)),
        ],
        out_specs=[
            pl.BlockSpec((ROW_TILE, D_MODEL), lambda i: (i, 0)),
            pl.BlockSpec((ROW_TILE * TOKEN_TILE_ROWS, LANES), lambda i: (i, 0)),
            pl.BlockSpec((N_EXPERTS, ROW_TILE), lambda i: (0, i)),
        ],
        out_shape=[
            jax.ShapeDtypeStruct((rows, D_MODEL), F32),
            jax.ShapeDtypeStruct((rows * TOKEN_TILE_ROWS, LANES), F32),
            jax.ShapeDtypeStruct((N_EXPERTS, rows), F32),
        ],
        compiler_params=_params(("arbitrary",), 48),
        name="post_attn",
    )(o, w_o, x, mod, norm_g3, mod, mod, w_router_t)


def _prefix_count(x):
    nchunk = SEQ // PREFIX_CHUNK
    r = lax.broadcasted_iota(jnp.int32, (PREFIX_CHUNK, PREFIX_CHUNK), 0)
    c = lax.broadcasted_iota(jnp.int32, (PREFIX_CHUNK, PREFIX_CHUNK), 1)
    upper = jnp.where(r < c, 1.0, 0.0).astype(BF16)
    chunks = [x[:, k * PREFIX_CHUNK:(k + 1) * PREFIX_CHUNK] for k in range(nchunk)]
    local = jnp.dot(jnp.concatenate(chunks, axis=0).astype(BF16), upper, preferred_element_type=F32)
    out = []
    offset = jnp.zeros((N_EXPERTS, 1), F32)
    for k in range(nchunk):
        out.append(local[k * N_EXPERTS:(k + 1) * N_EXPERTS] + offset)
        offset = offset + jnp.sum(chunks[k], axis=1, keepdims=True)
    return jnp.concatenate(out, axis=1)


def _route_kernel(aff_ref, slot_ref, tok_ref, gate_ref):
    aff = aff_ref[...]
    bits = pltpu.bitcast(aff, jnp.int32)

    def count(mask):
        return jnp.sum(jnp.where(mask, 1.0, 0.0), axis=1, keepdims=True)

    def search(_, bounds):
        lo, hi = bounds
        mid = lo + ((hi - lo) >> 1)
        ge = count(bits >= mid) >= CAP
        return jnp.where(ge, mid, lo), jnp.where(ge, hi, mid)

    lo0 = jnp.zeros((N_EXPERTS, 1), jnp.int32)
    hi0 = jnp.full((N_EXPERTS, 1), 0x7F800000, jnp.int32)
    tau, _ = lax.fori_loop(0, 31, search, (lo0, hi0))
    gt = bits > tau
    eq = jnp.where(bits == tau, 1.0, 0.0)
    need = CAP - count(gt)
    sel = jnp.where(gt, 1.0, jnp.where(_prefix_count(eq) < need, eq, 0.0))
    pos = _prefix_count(sel).astype(jnp.int32)
    chosen = sel > 0.5
    slot_ref[...] = jnp.where(chosen, pos, -1)
    lane = lax.broadcasted_iota(jnp.int32, (N_EXPERTS, SEQ), 1)
    packed = jnp.where(chosen, ROUTE_VALID | (lane << ROUTE_BITS) | (lane - pos), 0)
    gate = aff
    for k in range(ROUTE_BITS):
        step = 1 << k
        from_right = pltpu.roll(packed, SEQ - step, 1)
        gate_right = pltpu.roll(gate, SEQ - step, 1)
        arrives = (from_right & step) != 0
        stays = (packed & step) == 0
        packed = jnp.where(arrives, from_right, jnp.where(stays, packed, 0))
        gate = jnp.where(arrives, gate_right, gate)
    tok_ref[0] = (packed[:, :CAP] >> ROUTE_BITS) & (SEQ - 1)
    gate_ref[0] = gate[:, :CAP]


ROUTE_BITS = SEQ.bit_length() - 1
ROUTE_VALID = 1 << (2 * ROUTE_BITS)


def _route(aff_t):
    return pl.pallas_call(
        _route_kernel,
        grid=(BATCH,),
        in_specs=[pl.BlockSpec((N_EXPERTS, SEQ), lambda b: (0, b))],
        out_specs=[pl.BlockSpec((N_EXPERTS, SEQ), lambda b: (0, b)),
                   pl.BlockSpec((1, N_EXPERTS, CAP), lambda b: (b, 0, 0)),
                   pl.BlockSpec((1, N_EXPERTS, CAP), lambda b: (b, 0, 0))],
        out_shape=[jax.ShapeDtypeStruct((N_EXPERTS, BATCH * SEQ), jnp.int32),
                   jax.ShapeDtypeStruct((BATCH, N_EXPERTS, CAP), jnp.int32),
                   jax.ShapeDtypeStruct((BATCH, N_EXPERTS, CAP), F32)],
        compiler_params=_params(("arbitrary",), 32),
        name="route",
    )(aff_t)


GATHER_ROWS = BATCH * CAP
FF_STEPS = EXPERT_FF // FF_TILE
ROWS_PER_CHUNK = GATHER_ROWS // (FF_STEPS * BATCH)
PROLOGUE_UNROLL = 8


def _expert_kernel(tok_ref, h_ref, gates_ref, wg_ref, wu_ref, wd_ref, y_ref, x_ref, sem, xb_ref, acc_ref):
    e = pl.program_id(0)
    f = pl.program_id(1)
    slot = lax.rem(e, 2)
    next_slot = 1 - slot
    next_e = jnp.minimum(e + 1, N_EXPERTS - 1)

    tile = TOKEN_TILE_ROWS

    def row_copy(expert, row, dst_slot):
        src = pl.multiple_of(tok_ref[expert * GATHER_ROWS + row] * tile, tile)
        dst = pl.multiple_of(row * tile, tile)
        return pltpu.make_async_copy(h_ref.at[pl.ds(src, tile), :], x_ref.at[dst_slot, pl.ds(dst, tile), :],
                                     sem.at[dst_slot])

    def wait_rows(dst_slot):
        pltpu.make_async_copy(h_ref.at[pl.ds(0, GATHER_ROWS * tile), :], x_ref.at[dst_slot],
                              sem.at[dst_slot]).wait()

    @pl.when(jnp.logical_and(e == 0, f == 0))
    def _():
        def issue(i, carry):
            for u in range(PROLOGUE_UNROLL):
                row_copy(0, i * PROLOGUE_UNROLL + u, 0).start()
            return carry
        lax.fori_loop(0, GATHER_ROWS // PROLOGUE_UNROLL, issue, 0)

    @pl.when(f == 0)
    def _():
        wait_rows(slot)
        acc_ref[...] = jnp.zeros_like(acc_ref)
        for b in range(BATCH):
            cols = [x_ref[slot, pl.ds(b * CAP * tile + j, CAP, stride=tile), :] for j in range(tile)]
            xb_ref[b] = jnp.concatenate(cols, axis=1).astype(BF16)

    wg = wg_ref[0, 0].astype(BF16)
    wu = wu_ref[0, 0].astype(BF16)
    wd = wd_ref[0, 0].astype(BF16)
    for b in range(BATCH):
        first = (f * BATCH + b) * ROWS_PER_CHUNK
        for j in range(ROWS_PER_CHUNK):
            row_copy(next_e, first + j, next_slot).start()
        x = xb_ref[b]
        a = jnp.dot(x, wg, preferred_element_type=F32)
        u = jnp.dot(x, wu, preferred_element_type=F32)
        act = ((a * jax.nn.sigmoid(a)) * u).astype(BF16)
        acc_ref[b] += jnp.dot(act, wd, preferred_element_type=F32)

    @pl.when(f == FF_STEPS - 1)
    def _():
        gates = gates_ref[0]
        for b in range(BATCH):
            y_ref[0, b] = (acc_ref[b] * gates[:, b:b + 1]).astype(BF16)

    @pl.when(jnp.logical_and(e == N_EXPERTS - 1, f == FF_STEPS - 1))
    def _():
        wait_rows(next_slot)


def _experts(tok, h, gates, w_gate, w_up, w_down, layer):
    grid_spec = pltpu.PrefetchScalarGridSpec(
        num_scalar_prefetch=1,
        grid=(N_EXPERTS, FF_STEPS),
        in_specs=[
            pl.BlockSpec(memory_space=pl.ANY),
            pl.BlockSpec((1, CAP, BATCH), lambda e, f, tok: (e, 0, 0)),
            pl.BlockSpec((1, 1, D_MODEL, FF_TILE), lambda e, f, tok: (layer, e, 0, f)),
            pl.BlockSpec((1, 1, D_MODEL, FF_TILE), lambda e, f, tok: (layer, e, 0, f)),
            pl.BlockSpec((1, 1, FF_TILE, D_MODEL), lambda e, f, tok: (layer, e, f, 0)),
        ],
        out_specs=pl.BlockSpec((1, BATCH, CAP, D_MODEL), lambda e, f, tok: (e, 0, 0, 0)),
        scratch_shapes=[pltpu.VMEM((2, GATHER_ROWS * TOKEN_TILE_ROWS, LANES), F32), pltpu.SemaphoreType.DMA((2,)),
                        pltpu.VMEM((BATCH, CAP, D_MODEL), BF16), pltpu.VMEM((BATCH, CAP, D_MODEL), F32)],
    )
    return pl.pallas_call(
        _expert_kernel,
        grid_spec=grid_spec,
        out_shape=jax.ShapeDtypeStruct((N_EXPERTS, BATCH, CAP, D_MODEL), BF16),
        compiler_params=_params(("arbitrary", "arbitrary"), 60),
        name="moe_experts",
    )(tok, h, gates, w_gate, w_up, w_down)


def _combine_kernel(slot_ref, y_ref, x_ref, gate_ref, g_ref, o_ref, *, final):
    lane = lax.broadcasted_iota(jnp.int32, (COMBINE_ROWS, CAP), 1)
    slots = slot_ref[0]
    onehot = jnp.concatenate(
        [jnp.where(slots[:, e:e + 1] == lane, 1.0, 0.0).astype(BF16) for e in range(N_EXPERTS)], axis=1)
    moe = jnp.dot(onehot, y_ref[...].reshape(N_EXPERTS * CAP, D_MODEL), preferred_element_type=F32)
    x = x_ref[0] + gate_ref[0] * moe
    if final:
        x = (x * lax.rsqrt(jnp.mean(x * x, axis=-1, keepdims=True) + RMS_EPS)) * g_ref[...]
    o_ref[0] = x


def _combine(slot_t, y, x, mod, final_g, layer, final):
    return pl.pallas_call(
        functools.partial(_combine_kernel, final=final),
        grid=(BATCH, SEQ // COMBINE_ROWS),
        in_specs=[
            pl.BlockSpec((1, COMBINE_ROWS, N_EXPERTS), lambda b, t: (b, t, 0)),
            pl.BlockSpec((N_EXPERTS, 1, CAP, D_MODEL), lambda b, t: (0, b, 0, 0)),
            pl.BlockSpec((1, COMBINE_ROWS, D_MODEL), lambda b, t: (b, t, 0)),
            pl.BlockSpec((1, 1, D_MODEL), lambda b, t: ((layer * 6 + 5) * BATCH + b, 0, 0)),
            pl.BlockSpec((1, D_MODEL), lambda b, t: (0, 0)),
        ],
        out_specs=pl.BlockSpec((1, COMBINE_ROWS, D_MODEL), lambda b, t: (b, t, 0)),
        out_shape=jax.ShapeDtypeStruct((BATCH, SEQ, D_MODEL), F32),
        compiler_params=_params(("arbitrary", "arbitrary"), 48),
        name="moe_combine",
    )(slot_t, y, x, mod, final_g.reshape(1, D_MODEL))


def _na_qkv_weight(w):
    nq = NA_HEADS * HEAD_DIM
    scale = jnp.concatenate([jnp.full((nq,), Q_SCALE, F32), jnp.ones((2 * nq,), F32)])
    return (w * scale).astype(BF16)


def _sw_qkv_weight(w):
    nq = SW_Q_HEADS * HEAD_DIM
    nkv = SW_KV_HEADS * HEAD_DIM
    dup = lambda t: jnp.concatenate([t.reshape(D_MODEL, SW_KV_HEADS, 1, HEAD_DIM)] * 2, axis=2).reshape(
        D_MODEL, 2 * nkv)
    wq = w[:, :nq] * Q_SCALE
    return jnp.concatenate([wq, dup(w[:, nq:nq + nkv]), dup(w[:, nq + nkv:])], axis=1).astype(BF16)


def kernel(x, c, ada_w, ada_b, norm_g, na_w_qkv, na_w_o, na_rpb, sw_w_qkv, sw_w_o, sw_sinks, t5_bias,
           moe_w_router, moe_w_gate, moe_w_up, moe_w_down, final_g):
    mod = _ada(c, ada_w, ada_b)
    norm_g3 = norm_g.reshape(DEPTH * 2, 1, D_MODEL)
    x = x.reshape(BATCH * SEQ, D_MODEL)
    for layer in range(DEPTH):
        j = layer // N_MIXERS
        if layer % N_MIXERS == 0:
            qkv = _qkv(x, norm_g3, mod, _na_qkv_weight(na_w_qkv[j]), layer)
            o = _na_attention(qkv.reshape(BATCH, SEQ, -1), _na_rpb_rows(na_rpb[j]))
            w_o = na_w_o[j]
        else:
            qkv = _qkv(x, norm_g3, mod, _sw_qkv_weight(sw_w_qkv[j]), layer)
            o = _sw_attention(qkv.reshape(BATCH, SEQ, -1), _sw_rel_table(t5_bias), sw_sinks[j])
            w_o = sw_w_o[j]
        w_router = moe_w_router[layer].astype(BF16)
        x, h, aff_t = _post_attn(o.reshape(BATCH * SEQ, -1), w_o.astype(BF16), x, norm_g3, mod,
                                 w_router.T, layer)
        slot, tok, gate = _route(aff_t)
        rows = tok.transpose(1, 0, 2) + (jnp.arange(BATCH, dtype=jnp.int32) * SEQ)[None, :, None]
        y = _experts(rows.reshape(-1), h, gate.transpose(1, 2, 0), moe_w_gate, moe_w_up, moe_w_down, layer)
        slot_t = slot.reshape(N_EXPERTS, BATCH, SEQ).transpose(1, 2, 0)
        x = _combine(slot_t, y, x.reshape(BATCH, SEQ, D_MODEL), mod, final_g, layer,
                     final=layer == DEPTH - 1).reshape(BATCH * SEQ, D_MODEL)
    return x.reshape(BATCH, SEQ, D_MODEL)
```

```python
import functools

import numpy as np
import jax
import jax.numpy as jnp
from jax import lax
from jax.experimental import pallas as pl
from jax.experimental.pallas import tpu as pltpu

D_MODEL = 1024
BATCH = 8
SEQ = 2048
DEPTH = 2
GRID_W = 64
ROWS = SEQ // GRID_W
N_MIXERS = 2
HEAD_DIM = 64
NA_HEADS = 16
NA_WIN_H = 8
NA_WIN_W = 16
SW_Q_HEADS = 16
SW_KV_HEADS = 4
SW_GROUP = SW_Q_HEADS // SW_KV_HEADS
SW_WINDOW = 128
SW_BLOCK = 128
SW_NB = SEQ // SW_BLOCK
SW_SPAN = 3 * SW_BLOCK
T5_BUCKETS = 32
T5_MAX_DIST = 128
N_EXPERTS = 16
EXPERT_FF = 2048
EC_CAPACITY = 2
CAP = EC_CAPACITY * SEQ // N_EXPERTS
RMS_EPS = 1e-6
NEG = -1e30
LOG2E = 1.4426950408889634
Q_SCALE = HEAD_DIM ** -0.5 * LOG2E

LANES = 128
TOKEN_TILE_ROWS = D_MODEL // LANES
MIB = 1024 * 1024
F32 = jnp.float32
BF16 = jnp.bfloat16

ROW_TILE = 512
FF_TILE = 512
COMBINE_ROWS = 512
PREFIX_CHUNK = 256


def _params(semantics, vmem_mib, flags=None):
    return pltpu.CompilerParams(dimension_semantics=semantics, vmem_limit_bytes=vmem_mib * MIB, flags=flags)


def _norm_mod(x, g, sc, sh):
    y = x * lax.rsqrt(jnp.mean(x * x, axis=-1, keepdims=True) + RMS_EPS)
    return (y * g) * (1.0 + sc) + sh


def _softmax0(z):
    z = z - jnp.max(z, axis=0, keepdims=True)
    p = jnp.exp(z)
    return p / jnp.sum(p, axis=0, keepdims=True)


def _half_masks(rows):
    lane = lax.broadcasted_iota(jnp.int32, (rows, LANES), 1)
    lo = jnp.where(lane < HEAD_DIM, 1.0, 0.0).astype(BF16)
    hi = jnp.where(lane < HEAD_DIM, 0.0, 1.0).astype(BF16)
    return lo, hi


def _ada_kernel(c_ref, w_ref, b_ref, o_ref):
    c = c_ref[...]
    act = (c * jax.nn.sigmoid(c)).astype(BF16)
    o_ref[0] = jnp.dot(act, w_ref[0].astype(BF16), preferred_element_type=F32) + b_ref[0]


def _ada(c, ada_w, ada_b):
    out = pl.pallas_call(
        _ada_kernel,
        grid=(DEPTH, 6),
        in_specs=[
            pl.BlockSpec((BATCH, D_MODEL), lambda l, k: (0, 0)),
            pl.BlockSpec((1, D_MODEL, D_MODEL), lambda l, k: (l, 0, k)),
            pl.BlockSpec((1, 1, D_MODEL), lambda l, k: (l * 6 + k, 0, 0)),
        ],
        out_specs=pl.BlockSpec((1, BATCH, D_MODEL), lambda l, k: (l * 6 + k, 0, 0)),
        out_shape=jax.ShapeDtypeStruct((DEPTH * 6, BATCH, D_MODEL), F32),
        compiler_params=_params(("arbitrary", "arbitrary"), 32),
        name="ada_mod",
    )(c, ada_w, ada_b.reshape(DEPTH * 6, 1, D_MODEL))
    return out.reshape(DEPTH * 6 * BATCH, 1, D_MODEL)


def _mod_spec(layer, chunk):
    tiles_per_seq = SEQ // ROW_TILE
    return pl.BlockSpec((1, 1, D_MODEL),
                        lambda i: ((layer * 6 + chunk) * BATCH + i // tiles_per_seq, 0, 0))


def _qkv_kernel(x_ref, g_ref, sc_ref, sh_ref, w_ref, o_ref):
    h = _norm_mod(x_ref[...], g_ref[0], sc_ref[0], sh_ref[0])
    o_ref[...] = jnp.dot(h.astype(BF16), w_ref[...], preferred_element_type=F32).astype(BF16)


def _qkv(x, norm_g3, mod, w, layer):
    n = w.shape[1]
    return pl.pallas_call(
        _qkv_kernel,
        grid=(BATCH * SEQ // ROW_TILE,),
        in_specs=[
            pl.BlockSpec((ROW_TILE, D_MODEL), lambda i: (i, 0)),
            pl.BlockSpec((1, 1, D_MODEL), lambda i: (layer * 2, 0, 0)),
            _mod_spec(layer, 1),
            _mod_spec(layer, 0),
            pl.BlockSpec((D_MODEL, n), lambda i: (0, 0)),
        ],
        out_specs=pl.BlockSpec((ROW_TILE, n), lambda i: (i, 0)),
        out_shape=jax.ShapeDtypeStruct((BATCH * SEQ, n), BF16),
        compiler_params=_params(("arbitrary",), 48),
        name="norm_qkv",
    )(x, norm_g3, mod, mod, w)


NA_BIAS_ROWS = 2 * NA_WIN_H - 1
NA_ROW_UNROLL = 4


def _na_rpb_rows(rpb):
    w = NA_WIN_W - 1
    rpb = rpb.astype(F32) * LOG2E
    pad = jnp.zeros(rpb.shape[:2] + (LANES - 2 * w - 1,), F32)
    return jnp.concatenate([rpb[..., w:], pad, rpb[..., :w]], axis=-1)


def _na_build_bias(w_ref, bias_ref):
    lane = lax.broadcasted_iota(jnp.int32, (GRID_W, LANES), 1)
    col = lax.broadcasted_iota(jnp.int32, (GRID_W, LANES), 0)
    kc = jnp.bitwise_and(lane, GRID_W - 1)
    cstart = jnp.clip(col - NA_WIN_W // 2, 0, GRID_W - NA_WIN_W)
    in_window = jnp.where(kc >= cstart, jnp.where(kc < cstart + NA_WIN_W, 1.0, 0.0), 0.0) > 0.5
    low_half = lane < GRID_W

    def toeplitz(head, a, shift):
        row = jnp.broadcast_to(w_ref[head, a:a + 1, :], (GRID_W, LANES))
        return pltpu.roll(row, shift, 1, stride=1, stride_axis=0)

    for head in range(2):
        tiles = [jnp.where(in_window, jnp.where(low_half, toeplitz(head, a, 0), toeplitz(head, a + 1, GRID_W)), NEG)
                 for a in range(NA_BIAS_ROWS - 1)]
        for d in range(NA_WIN_H):
            for m in range(NA_WIN_H // 2):
                bias_ref[d, head * GRID_W:(head + 1) * GRID_W, m * LANES:(m + 1) * LANES] = (
                    tiles[NA_WIN_H - 1 - d + 2 * m])


def _na_kernel(w_ref, q_ref, k_ref, v_ref, o_ref, bias_ref, s0_ref, s1_ref, p0_ref, p1_ref):
    @pl.when(pl.program_id(1) == 0)
    def _():
        _na_build_bias(w_ref, bias_ref)

    lo, hi = _half_masks(GRID_W)
    lo_f = lax.broadcasted_iota(jnp.int32, (GRID_W, LANES), 1) < HEAD_DIM
    span = NA_WIN_H * GRID_W
    ones = jnp.ones((span, LANES), BF16)
    groups = ROWS // NA_ROW_UNROLL

    def group_rows(i):
        for u in range(NA_ROW_UNROLL):
            r = jnp.clip(i, 0, groups - 1) * NA_ROW_UNROLL + u
            yield u, r, jnp.clip(r - NA_WIN_H // 2, 0, ROWS - NA_WIN_H)

    def scores(i, s_ref):
        for u, r, rs in group_rows(i):
            q = q_ref[0, pl.ds(pl.multiple_of(r * GRID_W, GRID_W), GRID_W), :]
            lhs = jnp.concatenate([q * lo, q * hi], axis=0)
            kw = k_ref[0, pl.ds(pl.multiple_of(rs * GRID_W, GRID_W), span), :]
            s_ref[u] = lax.dot_general(lhs, kw, (((1,), (1,)), ((), ())), preferred_element_type=F32)

    def softmax(i, s_ref, p_ref):
        for u, r, rs in group_rows(i):
            s = s_ref[u] + bias_ref[r - rs]
            m = jnp.broadcast_to(jnp.max(s, axis=-1, keepdims=True), (2 * GRID_W, LANES))
            p_ref[u] = jnp.exp2(s - jnp.concatenate([m] * (span // LANES), axis=1)).astype(BF16)

    def values(i, p_ref):
        for u, r, rs in group_rows(i):
            vw = v_ref[0, pl.ds(pl.multiple_of(rs * GRID_W, GRID_W), span), :]
            o = jnp.dot(p_ref[u], jnp.concatenate([vw, ones], axis=1), preferred_element_type=F32)
            o = o[:, :LANES] / o[:, LANES:]
            out = jnp.where(lo_f, o[:GRID_W], o[GRID_W:])
            o_ref[0, pl.ds(pl.multiple_of(r * GRID_W, GRID_W), GRID_W), :] = out.astype(BF16)

    def step(i, carry):
        values(2 * i - 1, p1_ref)
        scores(2 * i + 1, s1_ref)
        softmax(2 * i, s0_ref, p0_ref)
        values(2 * i, p0_ref)
        scores(2 * i + 2, s0_ref)
        softmax(2 * i + 1, s1_ref, p1_ref)
        return carry

    p1_ref[...] = jnp.ones_like(p1_ref)
    scores(0, s0_ref)
    lax.fori_loop(0, groups // 2, step, 0)
    values(groups - 1, p1_ref)


def _na_attention(qkv, rpb_rows):
    pairs = NA_HEADS // 2
    return pl.pallas_call(
        _na_kernel,
        grid=(pairs, BATCH),
        in_specs=[
            pl.BlockSpec((2, NA_BIAS_ROWS, LANES), lambda p, b: (p, 0, 0)),
            pl.BlockSpec((1, SEQ, LANES), lambda p, b: (b, 0, p)),
            pl.BlockSpec((1, SEQ, LANES), lambda p, b: (b, 0, pairs + p)),
            pl.BlockSpec((1, SEQ, LANES), lambda p, b: (b, 0, 2 * pairs + p)),
        ],
        out_specs=pl.BlockSpec((1, SEQ, LANES), lambda p, b: (b, 0, p)),
        out_shape=jax.ShapeDtypeStruct((BATCH, SEQ, NA_HEADS * HEAD_DIM), BF16),
        scratch_shapes=[pltpu.VMEM((NA_WIN_H, 2 * GRID_W, NA_WIN_H * GRID_W), F32),
                        pltpu.VMEM((NA_ROW_UNROLL, 2 * GRID_W, NA_WIN_H * GRID_W), F32),
                        pltpu.VMEM((NA_ROW_UNROLL, 2 * GRID_W, NA_WIN_H * GRID_W), F32),
                        pltpu.VMEM((NA_ROW_UNROLL, 2 * GRID_W, NA_WIN_H * GRID_W), BF16),
                        pltpu.VMEM((NA_ROW_UNROLL, 2 * GRID_W, NA_WIN_H * GRID_W), BF16)],
        compiler_params=_params(("arbitrary", "arbitrary"), 32),
        name="na_attention",
    )(rpb_rows, qkv, qkv, qkv)


def _t5_buckets(rel):
    half = T5_BUCKETS // 2
    max_exact = half // 2
    n = np.abs(rel)
    large = max_exact + (np.log(np.maximum(n, 1) / max_exact)
                         / np.log(T5_MAX_DIST / max_exact) * (half - max_exact)).astype(np.int32)
    large = np.minimum(large, half - 1)
    return (rel > 0).astype(np.int32) * half + np.where(n < max_exact, n, large)


SW_REL_PERIOD = 512
SW_BLOCK_UNROLL = 2


def _sw_rel_table(t5_table):
    k = np.arange(SW_REL_PERIOD)
    rel = np.where(k < SW_REL_PERIOD // 2, k, k - SW_REL_PERIOD)
    ok = np.abs(rel) <= SW_WINDOW
    vals = jnp.where(ok[:, None], t5_table[_t5_buckets(rel)].astype(F32) * LOG2E, NEG)
    return vals.T.reshape(SW_KV_HEADS, SW_GROUP, SW_REL_PERIOD)


def _sw_build_bias(rel_ref, bias_ref):
    for g in range(SW_GROUP):
        row = jnp.broadcast_to(rel_ref[0, g:g + 1, :], (SW_BLOCK, SW_REL_PERIOD))
        t = pltpu.roll(row, 0, 1, stride=1, stride_axis=0)
        rows = slice(g * SW_BLOCK, (g + 1) * SW_BLOCK)
        bias_ref[0, rows, :] = t[:, :SW_SPAN]
        bias_ref[1, rows, :] = jnp.concatenate([t[:, SW_SPAN:], t[:, :2 * SW_BLOCK]], axis=1)
        bias_ref[2, rows, :] = jnp.concatenate([t[:, 2 * SW_BLOCK:], t[:, :SW_BLOCK]], axis=1)


def _sw_kernel(sink_ref, rel_ref, q_ref, k_ref, v_ref, o_ref, bias_ref, s0_ref, s1_ref, p0_ref, p1_ref,
               t0_ref, t1_ref):
    j = pl.program_id(0)
    ones = jnp.ones((SW_SPAN, LANES), BF16)

    @pl.when(pl.program_id(1) == 0)
    def _():
        _sw_build_bias(rel_ref, bias_ref)

    masks = _half_masks(SW_BLOCK)
    lo_f = lax.broadcasted_iota(jnp.int32, (SW_BLOCK, LANES), 1) < HEAD_DIM
    sinks = [jnp.full((SW_BLOCK, LANES), sink_ref[j * SW_GROUP + g] * LOG2E, F32) for g in range(SW_GROUP)]
    groups = SW_NB // SW_BLOCK_UNROLL

    def group_blocks(i):
        for u in range(SW_BLOCK_UNROLL):
            n = jnp.clip(i, 0, groups - 1) * SW_BLOCK_UNROLL + u
            yield u, n, jnp.clip(n - 1, 0, SW_NB - 3)

    def block_rows(n):
        return pl.ds(pl.multiple_of(n * SW_BLOCK, SW_BLOCK), SW_BLOCK)

    def scores(i, s_ref):
        for u, n, first in group_blocks(i):
            kw = k_ref[0, pl.ds(pl.multiple_of(first * SW_BLOCK, SW_BLOCK), SW_SPAN), :]
            for g in range(SW_GROUP):
                q = q_ref[0, block_rows(n), (g // 2) * LANES:(g // 2 + 1) * LANES] * masks[g % 2]
                s_ref[u * SW_GROUP + g] = lax.dot_general(q, kw, (((1,), (1,)), ((), ())),
                                                          preferred_element_type=F32)

    def softmax(i, s_ref, p_ref, t_ref):
        for u, n, first in group_blocks(i):
            for g in range(SW_GROUP):
                c = u * SW_GROUP + g
                s = s_ref[c] + bias_ref[n - first, g * SW_BLOCK:(g + 1) * SW_BLOCK, :]
                m = jnp.maximum(jnp.broadcast_to(jnp.max(s, axis=-1, keepdims=True), (SW_BLOCK, LANES)), sinks[g])
                p_ref[c] = jnp.exp2(s - jnp.concatenate([m] * (SW_SPAN // LANES), axis=1)).astype(BF16)
                t_ref[c] = jnp.exp2(sinks[g] - m)

    def values(i, p_ref, t_ref):
        for u, n, first in group_blocks(i):
            vw = v_ref[0, pl.ds(pl.multiple_of(first * SW_BLOCK, SW_BLOCK), SW_SPAN), :]
            v_ones = jnp.concatenate([vw, ones], axis=1)
            heads = []
            for g in range(SW_GROUP):
                c = u * SW_GROUP + g
                o = jnp.dot(p_ref[c], v_ones, preferred_element_type=F32)
                heads.append(o[:, :LANES] / (o[:, LANES:] + t_ref[c]))
            out = jnp.concatenate([jnp.where(lo_f, heads[0], heads[1]), jnp.where(lo_f, heads[2], heads[3])],
                                  axis=1)
            o_ref[0, block_rows(n), :] = out.astype(BF16)

    def step(i, carry):
        values(2 * i - 1, p1_ref, t1_ref)
        scores(2 * i + 1, s1_ref)
        softmax(2 * i, s0_ref, p0_ref, t0_ref)
        values(2 * i, p0_ref, t0_ref)
        scores(2 * i + 2, s0_ref)
        softmax(2 * i + 1, s1_ref, p1_ref, t1_ref)
        return carry

    p1_ref[...] = jnp.ones_like(p1_ref)
    t1_ref[...] = jnp.ones_like(t1_ref)
    scores(0, s0_ref)
    lax.fori_loop(0, groups // 2, step, 0)
    values(groups - 1, p1_ref, t1_ref)


def _sw_attention(qkv, rel_table, sinks):
    qw = SW_GROUP * HEAD_DIM
    k_off = SW_Q_HEADS * HEAD_DIM // LANES
    v_off = k_off + SW_KV_HEADS
    chains = SW_BLOCK_UNROLL * SW_GROUP
    return pl.pallas_call(
        _sw_kernel,
        grid=(SW_KV_HEADS, BATCH),
        in_specs=[
            pl.BlockSpec(memory_space=pltpu.SMEM),
            pl.BlockSpec((1, SW_GROUP, SW_REL_PERIOD), lambda j, b: (j, 0, 0)),
            pl.BlockSpec((1, SEQ, qw), lambda j, b: (b, 0, j)),
            pl.BlockSpec((1, SEQ, LANES), lambda j, b: (b, 0, k_off + j)),
            pl.BlockSpec((1, SEQ, LANES), lambda j, b: (b, 0, v_off + j)),
        ],
        out_specs=pl.BlockSpec((1, SEQ, qw), lambda j, b: (b, 0, j)),
        out_shape=jax.ShapeDtypeStruct((BATCH, SEQ, SW_Q_HEADS * HEAD_DIM), BF16),
        scratch_shapes=[pltpu.VMEM((3, SW_GROUP * SW_BLOCK, SW_SPAN), F32)]
        + [pltpu.VMEM((chains, SW_BLOCK, SW_SPAN), F32)] * 2
        + [pltpu.VMEM((chains, SW_BLOCK, SW_SPAN), BF16)] * 2
        + [pltpu.VMEM((chains, SW_BLOCK, LANES), F32)] * 2,
        compiler_params=_params(("arbitrary", "arbitrary"), 32),
        name="sw_attention",
    )(sinks, rel_table, qkv, qkv, qkv)


def _post_attn_kernel(o_ref, wo_ref, x_ref, gate_ref, g_ref, sc_ref, sh_ref, wr_ref,
                      xo_ref, h_ref, aff_ref):
    y = jnp.dot(o_ref[...], wo_ref[...], preferred_element_type=F32)
    xn = x_ref[...] + gate_ref[0] * y
    xo_ref[...] = xn
    h = _norm_mod(xn, g_ref[0], sc_ref[0], sh_ref[0])
    for j in range(TOKEN_TILE_ROWS):
        h_ref[pl.ds(j, ROW_TILE, stride=TOKEN_TILE_ROWS), :] = h[:, j * LANES:(j + 1) * LANES]
    hb = h.astype(BF16)
    logits = lax.dot_general(wr_ref[...], hb, (((1,), (1,)), ((), ())), preferred_element_type=F32)
    aff_ref[...] = _softmax0(logits)


def _post_attn(o, w_o, x, norm_g3, mod, w_router_t, layer):
    rows = BATCH * SEQ
    return pl.pallas_call(
        _post_attn_kernel,
        grid=(rows // ROW_TILE,),
        in_specs=[
            pl.BlockSpec((ROW_TILE, D_MODEL), lambda i: (i, 0)),
            pl.BlockSpec((D_MODEL, D_MODEL), lambda i: (0, 0)),
            pl.BlockSpec((ROW_TILE, D_MODEL), lambda i: (i, 0)),
            _mod_spec(layer, 2),
            pl.BlockSpec((1, 1, D_MODEL), lambda i: (layer * 2 + 1, 0, 0)),
            _mod_spec(layer, 4),
            _mod_spec(layer, 3),
            pl.BlockSpec((N_EXPERTS, D_MODEL), lambda i: (0, 0)),
        ],
        out_specs=[
            pl.BlockSpec((ROW_TILE, D_MODEL), lambda i: (i, 0)),
            pl.BlockSpec((ROW_TILE * TOKEN_TILE_ROWS, LANES), lambda i: (i, 0)),
            pl.BlockSpec((N_EXPERTS, ROW_TILE), lambda i: (0, i)),
        ],
        out_shape=[
            jax.ShapeDtypeStruct((rows, D_MODEL), F32),
            jax.ShapeDtypeStruct((rows * TOKEN_TILE_ROWS, LANES), F32),
            jax.ShapeDtypeStruct((N_EXPERTS, rows), F32),
        ],
        compiler_params=_params(("arbitrary",), 48),
        name="post_attn",
    )(o, w_o, x, mod, norm_g3, mod, mod, w_router_t)


def _prefix_count(x):
    nchunk = SEQ // PREFIX_CHUNK
    r = lax.broadcasted_iota(jnp.int32, (PREFIX_CHUNK, PREFIX_CHUNK), 0)
    c = lax.broadcasted_iota(jnp.int32, (PREFIX_CHUNK, PREFIX_CHUNK), 1)
    upper = jnp.where(r < c, 1.0, 0.0).astype(BF16)
    chunks = [x[:, k * PREFIX_CHUNK:(k + 1) * PREFIX_CHUNK] for k in range(nchunk)]
    local = jnp.dot(jnp.concatenate(chunks, axis=0).astype(BF16), upper, preferred_element_type=F32)
    out = []
    offset = jnp.zeros((N_EXPERTS, 1), F32)
    for k in range(nchunk):
        out.append(local[k * N_EXPERTS:(k + 1) * N_EXPERTS] + offset)
        offset = offset + jnp.sum(chunks[k], axis=1, keepdims=True)
    return jnp.concatenate(out, axis=1)


def _route_kernel(aff_ref, slot_ref, tok_ref, gate_ref):
    aff = aff_ref[...]
    bits = pltpu.bitcast(aff, jnp.int32)

    def count(mask):
        return jnp.sum(jnp.where(mask, 1.0, 0.0), axis=1, keepdims=True)

    def search(_, bounds):
        lo, hi = bounds
        mid = lo + ((hi - lo) >> 1)
        ge = count(bits >= mid) >= CAP
        return jnp.where(ge, mid, lo), jnp.where(ge, hi, mid)

    lo0 = jnp.zeros((N_EXPERTS, 1), jnp.int32)
    hi0 = jnp.full((N_EXPERTS, 1), 0x7F800000, jnp.int32)
    tau, _ = lax.fori_loop(0, 31, search, (lo0, hi0))
    gt = bits > tau
    eq = jnp.where(bits == tau, 1.0, 0.0)
    need = CAP - count(gt)
    sel = jnp.where(gt, 1.0, jnp.where(_prefix_count(eq) < need, eq, 0.0))
    pos = _prefix_count(sel).astype(jnp.int32)
    chosen = sel > 0.5
    slot_ref[...] = jnp.where(chosen, pos, -1)
    lane = lax.broadcasted_iota(jnp.int32, (N_EXPERTS, SEQ), 1)
    packed = jnp.where(chosen, ROUTE_VALID | (lane << ROUTE_BITS) | (lane - pos), 0)
    gate = aff
    for k in range(ROUTE_BITS):
        step = 1 << k
        from_right = pltpu.roll(packed, SEQ - step, 1)
        gate_right = pltpu.roll(gate, SEQ - step, 1)
        arrives = (from_right & step) != 0
        stays = (packed & step) == 0
        packed = jnp.where(arrives, from_right, jnp.where(stays, packed, 0))
        gate = jnp.where(arrives, gate_right, gate)
    tok_ref[0] = (packed[:, :CAP] >> ROUTE_BITS) & (SEQ - 1)
    gate_ref[0] = gate[:, :CAP]


ROUTE_BITS = SEQ.bit_length() - 1
ROUTE_VALID = 1 << (2 * ROUTE_BITS)


def _route(aff_t):
    return pl.pallas_call(
        _route_kernel,
        grid=(BATCH,),
        in_specs=[pl.BlockSpec((N_EXPERTS, SEQ), lambda b: (0, b))],
        out_specs=[pl.BlockSpec((N_EXPERTS, SEQ), lambda b: (0, b)),
                   pl.BlockSpec((1, N_EXPERTS, CAP), lambda b: (b, 0, 0)),
                   pl.BlockSpec((1, N_EXPERTS, CAP), lambda b: (b, 0, 0))],
        out_shape=[jax.ShapeDtypeStruct((N_EXPERTS, BATCH * SEQ), jnp.int32),
                   jax.ShapeDtypeStruct((BATCH, N_EXPERTS, CAP), jnp.int32),
                   jax.ShapeDtypeStruct((BATCH, N_EXPERTS, CAP), F32)],
        compiler_params=_params(("arbitrary",), 32),
        name="route",
    )(aff_t)


GATHER_ROWS = BATCH * CAP
FF_STEPS = EXPERT_FF // FF_TILE
ROWS_PER_CHUNK = GATHER_ROWS // (FF_STEPS * BATCH)
PROLOGUE_UNROLL = 8


def _expert_kernel(tok_ref, h_ref, gates_ref, wg_ref, wu_ref, wd_ref, y_ref, x_ref, sem, xb_ref, acc_ref):
    e = pl.program_id(0)
    f = pl.program_id(1)
    slot = lax.rem(e, 2)
    next_slot = 1 - slot
    next_e = jnp.minimum(e + 1, N_EXPERTS - 1)

    tile = TOKEN_TILE_ROWS

    def row_copy(expert, row, dst_slot):
        src = pl.multiple_of(tok_ref[expert * GATHER_ROWS + row] * tile, tile)
        dst = pl.multiple_of(row * tile, tile)
        return pltpu.make_async_copy(h_ref.at[pl.ds(src, tile), :], x_ref.at[dst_slot, pl.ds(dst, tile), :],
                                     sem.at[dst_slot])

    def wait_rows(dst_slot):
        pltpu.make_async_copy(h_ref.at[pl.ds(0, GATHER_ROWS * tile), :], x_ref.at[dst_slot],
                              sem.at[dst_slot]).wait()

    @pl.when(jnp.logical_and(e == 0, f == 0))
    def _():
        def issue(i, carry):
            for u in range(PROLOGUE_UNROLL):
                row_copy(0, i * PROLOGUE_UNROLL + u, 0).start()
            return carry
        lax.fori_loop(0, GATHER_ROWS // PROLOGUE_UNROLL, issue, 0)

    @pl.when(f == 0)
    def _():
        wait_rows(slot)
        acc_ref[...] = jnp.zeros_like(acc_ref)
        for b in range(BATCH):
            cols = [x_ref[slot, pl.ds(b * CAP * tile + j, CAP, stride=tile), :] for j in range(tile)]
            xb_ref[b] = jnp.concatenate(cols, axis=1).astype(BF16)

    wg = wg_ref[0, 0].astype(BF16)
    wu = wu_ref[0, 0].astype(BF16)
    wd = wd_ref[0, 0].astype(BF16)
    for b in range(BATCH):
        first = (f * BATCH + b) * ROWS_PER_CHUNK
        for j in range(ROWS_PER_CHUNK):
            row_copy(next_e, first + j, next_slot).start()
        x = xb_ref[b]
        a = jnp.dot(x, wg, preferred_element_type=F32)
        u = jnp.dot(x, wu, preferred_element_type=F32)
        act = ((a * jax.nn.sigmoid(a)) * u).astype(BF16)
        acc_ref[b] += jnp.dot(act, wd, preferred_element_type=F32)

    @pl.when(f == FF_STEPS - 1)
    def _():
        gates = gates_ref[0]
        for b in range(BATCH):
            y_ref[0, b] = (acc_ref[b] * gates[:, b:b + 1]).astype(BF16)

    @pl.when(jnp.logical_and(e == N_EXPERTS - 1, f == FF_STEPS - 1))
    def _():
        wait_rows(next_slot)


def _experts(tok, h, gates, w_gate, w_up, w_down, layer):
    grid_spec = pltpu.PrefetchScalarGridSpec(
        num_scalar_prefetch=1,
        grid=(N_EXPERTS, FF_STEPS),
        in_specs=[
            pl.BlockSpec(memory_space=pl.ANY),
            pl.BlockSpec((1, CAP, BATCH), lambda e, f, tok: (e, 0, 0)),
            pl.BlockSpec((1, 1, D_MODEL, FF_TILE), lambda e, f, tok: (layer, e, 0, f)),
            pl.BlockSpec((1, 1, D_MODEL, FF_TILE), lambda e, f, tok: (layer, e, 0, f)),
            pl.BlockSpec((1, 1, FF_TILE, D_MODEL), lambda e, f, tok: (layer, e, f, 0)),
        ],
        out_specs=pl.BlockSpec((1, BATCH, CAP, D_MODEL), lambda e, f, tok: (e, 0, 0, 0)),
        scratch_shapes=[pltpu.VMEM((2, GATHER_ROWS * TOKEN_TILE_ROWS, LANES), F32), pltpu.SemaphoreType.DMA((2,)),
                        pltpu.VMEM((BATCH, CAP, D_MODEL), BF16), pltpu.VMEM((BATCH, CAP, D_MODEL), F32)],
    )
    return pl.pallas_call(
        _expert_kernel,
        grid_spec=grid_spec,
        out_shape=jax.ShapeDtypeStruct((N_EXPERTS, BATCH, CAP, D_MODEL), BF16),
        compiler_params=_params(("arbitrary", "arbitrary"), 60),
        name="moe_experts",
    )(tok, h, gates, w_gate, w_up, w_down)


def _combine_kernel(slot_ref, y_ref, x_ref, gate_ref, g_ref, o_ref, *, final):
    lane = lax.broadcasted_iota(jnp.int32, (COMBINE_ROWS, CAP), 1)
    slots = slot_ref[0]
    onehot = jnp.concatenate(
        [jnp.where(slots[:, e:e + 1] == lane, 1.0, 0.0).astype(BF16) for e in range(N_EXPERTS)], axis=1)
    moe = jnp.dot(onehot, y_ref[...].reshape(N_EXPERTS * CAP, D_MODEL), preferred_element_type=F32)
    x = x_ref[0] + gate_ref[0] * moe
    if final:
        x = (x * lax.rsqrt(jnp.mean(x * x, axis=-1, keepdims=True) + RMS_EPS)) * g_ref[...]
    o_ref[0] = x


def _combine(slot_t, y, x, mod, final_g, layer, final):
    return pl.pallas_call(
        functools.partial(_combine_kernel, final=final),
        grid=(BATCH, SEQ // COMBINE_ROWS),
        in_specs=[
            pl.BlockSpec((1, COMBINE_ROWS, N_EXPERTS), lambda b, t: (b, t, 0)),
            pl.BlockSpec((N_EXPERTS, 1, CAP, D_MODEL), lambda b, t: (0, b, 0, 0)),
            pl.BlockSpec((1, COMBINE_ROWS, D_MODEL), lambda b, t: (b, t, 0)),
            pl.BlockSpec((1, 1, D_MODEL), lambda b, t: ((layer * 6 + 5) * BATCH + b, 0, 0)),
            pl.BlockSpec((1, D_MODEL), lambda b, t: (0, 0)),
        ],
        out_specs=pl.BlockSpec((1, COMBINE_ROWS, D_MODEL), lambda b, t: (b, t, 0)),
        out_shape=jax.ShapeDtypeStruct((BATCH, SEQ, D_MODEL), F32),
        compiler_params=_params(("arbitrary", "arbitrary"), 48),
        name="moe_combine",
    )(slot_t, y, x, mod, final_g.reshape(1, D_MODEL))


def _na_qkv_weight(w):
    nq = NA_HEADS * HEAD_DIM
    scale = jnp.concatenate([jnp.full((nq,), Q_SCALE, F32), jnp.ones((2 * nq,), F32)])
    return (w * scale).astype(BF16)


def _sw_qkv_weight(w):
    nq = SW_Q_HEADS * HEAD_DIM
    nkv = SW_KV_HEADS * HEAD_DIM
    dup = lambda t: jnp.concatenate([t.reshape(D_MODEL, SW_KV_HEADS, 1, HEAD_DIM)] * 2, axis=2).reshape(
        D_MODEL, 2 * nkv)
    wq = w[:, :nq] * Q_SCALE
    return jnp.concatenate([wq, dup(w[:, nq:nq + nkv]), dup(w[:, nq + nkv:])], axis=1).astype(BF16)


def kernel(x, c, ada_w, ada_b, norm_g, na_w_qkv, na_w_o, na_rpb, sw_w_qkv, sw_w_o, sw_sinks, t5_bias,
           moe_w_router, moe_w_gate, moe_w_up, moe_w_down, final_g):
    mod = _ada(c, ada_w, ada_b)
    norm_g3 = norm_g.reshape(DEPTH * 2, 1, D_MODEL)
    x = x.reshape(BATCH * SEQ, D_MODEL)
    for layer in range(DEPTH):
        j = layer // N_MIXERS
        if layer % N_MIXERS == 0:
            qkv = _qkv(x, norm_g3, mod, _na_qkv_weight(na_w_qkv[j]), layer)
            o = _na_attention(qkv.reshape(BATCH, SEQ, -1), _na_rpb_rows(na_rpb[j]))
            w_o = na_w_o[j]
        else:
            qkv = _qkv(x, norm_g3, mod, _sw_qkv_weight(sw_w_qkv[j]), layer)
            o = _sw_attention(qkv.reshape(BATCH, SEQ, -1), _sw_rel_table(t5_bias), sw_sinks[j])
            w_o = sw_w_o[j]
        w_router = moe_w_router[layer].astype(BF16)
        x, h, aff_t = _post_attn(o.reshape(BATCH * SEQ, -1), w_o.astype(BF16), x, norm_g3, mod,
                                 w_router.T, layer)
        slot, tok, gate = _route(aff_t)
        rows = tok.transpose(1, 0, 2) + (jnp.arange(BATCH, dtype=jnp.int32) * SEQ)[None, :, None]
        y = _experts(rows.reshape(-1), h, gate.transpose(1, 2, 0), moe_w_gate, moe_w_up, moe_w_down, layer)
        slot_t = slot.reshape(N_EXPERTS, BATCH, SEQ).transpose(1, 2, 0)
        x = _combine(slot_t, y, x.reshape(BATCH, SEQ, D_MODEL), mod, final_g, layer,
                     final=layer == DEPTH - 1).reshape(BATCH * SEQ, D_MODEL)
    return x.reshape(BATCH, SEQ, D_MODEL)
```

```python
import functools

import numpy as np
import jax
import jax.numpy as jnp
from jax import lax
from jax.experimental import pallas as pl
from jax.experimental.pallas import tpu as pltpu

D_MODEL = 1024
BATCH = 8
SEQ = 2048
DEPTH = 2
GRID_W = 64
ROWS = SEQ // GRID_W
N_MIXERS = 2
HEAD_DIM = 64
NA_HEADS = 16
NA_WIN_H = 8
NA_WIN_W = 16
SW_Q_HEADS = 16
SW_KV_HEADS = 4
SW_GROUP = SW_Q_HEADS // SW_KV_HEADS
SW_WINDOW = 128
SW_BLOCK = 128
SW_NB = SEQ // SW_BLOCK
SW_SPAN = 3 * SW_BLOCK
T5_BUCKETS = 32
T5_MAX_DIST = 128
N_EXPERTS = 16
EXPERT_FF = 2048
EC_CAPACITY = 2
CAP = EC_CAPACITY * SEQ // N_EXPERTS
RMS_EPS = 1e-6
NEG = -1e30
LOG2E = 1.4426950408889634
Q_SCALE = HEAD_DIM ** -0.5 * LOG2E

LANES = 128
TOKEN_TILE_ROWS = D_MODEL // LANES
MIB = 1024 * 1024
F32 = jnp.float32
BF16 = jnp.bfloat16

ROW_TILE = 512
FF_TILE = 512
COMBINE_ROWS = 512
PREFIX_CHUNK = 256


def _params(semantics, vmem_mib, flags=None):
    return pltpu.CompilerParams(dimension_semantics=semantics, vmem_limit_bytes=vmem_mib * MIB, flags=flags)


def _norm_mod(x, g, sc, sh):
    y = x * lax.rsqrt(jnp.mean(x * x, axis=-1, keepdims=True) + RMS_EPS)
    return (y * g) * (1.0 + sc) + sh


def _softmax0(z):
    z = z - jnp.max(z, axis=0, keepdims=True)
    p = jnp.exp(z)
    return p / jnp.sum(p, axis=0, keepdims=True)


def _half_masks(rows):
    lane = lax.broadcasted_iota(jnp.int32, (rows, LANES), 1)
    lo = jnp.where(lane < HEAD_DIM, 1.0, 0.0).astype(BF16)
    hi = jnp.where(lane < HEAD_DIM, 0.0, 1.0).astype(BF16)
    return lo, hi


def _ada_kernel(c_ref, w_ref, b_ref, o_ref):
    c = c_ref[...]
    act = (c * jax.nn.sigmoid(c)).astype(BF16)
    o_ref[0] = jnp.dot(act, w_ref[0].astype(BF16), preferred_element_type=F32) + b_ref[0]


def _ada(c, ada_w, ada_b):
    out = pl.pallas_call(
        _ada_kernel,
        grid=(DEPTH, 6),
        in_specs=[
            pl.BlockSpec((BATCH, D_MODEL), lambda l, k: (0, 0)),
            pl.BlockSpec((1, D_MODEL, D_MODEL), lambda l, k: (l, 0, k)),
            pl.BlockSpec((1, 1, D_MODEL), lambda l, k: (l * 6 + k, 0, 0)),
        ],
        out_specs=pl.BlockSpec((1, BATCH, D_MODEL), lambda l, k: (l * 6 + k, 0, 0)),
        out_shape=jax.ShapeDtypeStruct((DEPTH * 6, BATCH, D_MODEL), F32),
        compiler_params=_params(("arbitrary", "arbitrary"), 32),
        name="ada_mod",
    )(c, ada_w, ada_b.reshape(DEPTH * 6, 1, D_MODEL))
    return out.reshape(DEPTH * 6 * BATCH, 1, D_MODEL)


def _mod_spec(layer, chunk):
    tiles_per_seq = SEQ // ROW_TILE
    return pl.BlockSpec((1, 1, D_MODEL),
                        lambda i: ((layer * 6 + chunk) * BATCH + i // tiles_per_seq, 0, 0))


def _qkv_kernel(x_ref, g_ref, sc_ref, sh_ref, w_ref, o_ref):
    h = _norm_mod(x_ref[...], g_ref[0], sc_ref[0], sh_ref[0])
    o_ref[...] = jnp.dot(h.astype(BF16), w_ref[...], preferred_element_type=F32).astype(BF16)


def _qkv(x, norm_g3, mod, w, layer):
    n = w.shape[1]
    return pl.pallas_call(
        _qkv_kernel,
        grid=(BATCH * SEQ // ROW_TILE,),
        in_specs=[
            pl.BlockSpec((ROW_TILE, D_MODEL), lambda i: (i, 0)),
            pl.BlockSpec((1, 1, D_MODEL), lambda i: (layer * 2, 0, 0)),
            _mod_spec(layer, 1),
            _mod_spec(layer, 0),
            pl.BlockSpec((D_MODEL, n), lambda i: (0, 0)),
        ],
        out_specs=pl.BlockSpec((ROW_TILE, n), lambda i: (i, 0)),
        out_shape=jax.ShapeDtypeStruct((BATCH * SEQ, n), BF16),
        compiler_params=_params(("arbitrary",), 48),
        name="norm_qkv",
    )(x, norm_g3, mod, mod, w)


NA_BIAS_ROWS = 2 * NA_WIN_H - 1
NA_ROW_UNROLL = 4


def _na_rpb_rows(rpb):
    w = NA_WIN_W - 1
    rpb = rpb.astype(F32) * LOG2E
    pad = jnp.zeros(rpb.shape[:2] + (LANES - 2 * w - 1,), F32)
    return jnp.concatenate([rpb[..., w:], pad, rpb[..., :w]], axis=-1)


def _na_build_bias(w_ref, bias_ref):
    lane = lax.broadcasted_iota(jnp.int32, (GRID_W, LANES), 1)
    col = lax.broadcasted_iota(jnp.int32, (GRID_W, LANES), 0)
    kc = jnp.bitwise_and(lane, GRID_W - 1)
    cstart = jnp.clip(col - NA_WIN_W // 2, 0, GRID_W - NA_WIN_W)
    in_window = jnp.where(kc >= cstart, jnp.where(kc < cstart + NA_WIN_W, 1.0, 0.0), 0.0) > 0.5
    low_half = lane < GRID_W

    def toeplitz(head, a, shift):
        row = jnp.broadcast_to(w_ref[head, a:a + 1, :], (GRID_W, LANES))
        return pltpu.roll(row, shift, 1, stride=1, stride_axis=0)

    for head in range(2):
        tiles = [jnp.where(in_window, jnp.where(low_half, toeplitz(head, a, 0), toeplitz(head, a + 1, GRID_W)), NEG)
                 for a in range(NA_BIAS_ROWS - 1)]
        for d in range(NA_WIN_H):
            for m in range(NA_WIN_H // 2):
                bias_ref[d, head * GRID_W:(head + 1) * GRID_W, m * LANES:(m + 1) * LANES] = (
                    tiles[NA_WIN_H - 1 - d + 2 * m])


def _na_kernel(w_ref, q_ref, k_ref, v_ref, o_ref, bias_ref, s0_ref, s1_ref, p0_ref, p1_ref):
    @pl.when(pl.program_id(1) == 0)
    def _():
        _na_build_bias(w_ref, bias_ref)

    lo, hi = _half_masks(GRID_W)
    lo_f = lax.broadcasted_iota(jnp.int32, (GRID_W, LANES), 1) < HEAD_DIM
    span = NA_WIN_H * GRID_W
    ones = jnp.ones((span, LANES), BF16)
    groups = ROWS // NA_ROW_UNROLL

    def group_rows(i):
        for u in range(NA_ROW_UNROLL):
            r = jnp.clip(i, 0, groups - 1) * NA_ROW_UNROLL + u
            yield u, r, jnp.clip(r - NA_WIN_H // 2, 0, ROWS - NA_WIN_H)

    def scores(i, s_ref):
        for u, r, rs in group_rows(i):
            q = q_ref[0, pl.ds(pl.multiple_of(r * GRID_W, GRID_W), GRID_W), :]
            lhs = jnp.concatenate([q * lo, q * hi], axis=0)
            kw = k_ref[0, pl.ds(pl.multiple_of(rs * GRID_W, GRID_W), span), :]
            s_ref[u] = lax.dot_general(lhs, kw, (((1,), (1,)), ((), ())), preferred_element_type=F32)

    def softmax(i, s_ref, p_ref):
        for u, r, rs in group_rows(i):
            s = s_ref[u] + bias_ref[r - rs]
            m = jnp.broadcast_to(jnp.max(s, axis=-1, keepdims=True), (2 * GRID_W, LANES))
            p_ref[u] = jnp.exp2(s - jnp.concatenate([m] * (span // LANES), axis=1)).astype(BF16)

    def values(i, p_ref):
        for u, r, rs in group_rows(i):
            vw = v_ref[0, pl.ds(pl.multiple_of(rs * GRID_W, GRID_W), span), :]
            o = jnp.dot(p_ref[u], jnp.concatenate([vw, ones], axis=1), preferred_element_type=F32)
            o = o[:, :LANES] / o[:, LANES:]
            out = jnp.where(lo_f, o[:GRID_W], o[GRID_W:])
            o_ref[0, pl.ds(pl.multiple_of(r * GRID_W, GRID_W), GRID_W), :] = out.astype(BF16)

    def step(i, carry):
        values(2 * i - 1, p1_ref)
        scores(2 * i + 1, s1_ref)
        softmax(2 * i, s0_ref, p0_ref)
        values(2 * i, p0_ref)
        scores(2 * i + 2, s0_ref)
        softmax(2 * i + 1, s1_ref, p1_ref)
        return carry

    p1_ref[...] = jnp.ones_like(p1_ref)
    scores(0, s0_ref)
    lax.fori_loop(0, groups // 2, step, 0)
    values(groups - 1, p1_ref)


def _na_attention(qkv, rpb_rows):
    pairs = NA_HEADS // 2
    return pl.pallas_call(
        _na_kernel,
        grid=(pairs, BATCH),
        in_specs=[
            pl.BlockSpec((2, NA_BIAS_ROWS, LANES), lambda p, b: (p, 0, 0)),
            pl.BlockSpec((1, SEQ, LANES), lambda p, b: (b, 0, p)),
            pl.BlockSpec((1, SEQ, LANES), lambda p, b: (b, 0, pairs + p)),
            pl.BlockSpec((1, SEQ, LANES), lambda p, b: (b, 0, 2 * pairs + p)),
        ],
        out_specs=pl.BlockSpec((1, SEQ, LANES), lambda p, b: (b, 0, p)),
        out_shape=jax.ShapeDtypeStruct((BATCH, SEQ, NA_HEADS * HEAD_DIM), BF16),
        scratch_shapes=[pltpu.VMEM((NA_WIN_H, 2 * GRID_W, NA_WIN_H * GRID_W), F32),
                        pltpu.VMEM((NA_ROW_UNROLL, 2 * GRID_W, NA_WIN_H * GRID_W), F32),
                        pltpu.VMEM((NA_ROW_UNROLL, 2 * GRID_W, NA_WIN_H * GRID_W), F32),
                        pltpu.VMEM((NA_ROW_UNROLL, 2 * GRID_W, NA_WIN_H * GRID_W), BF16),
                        pltpu.VMEM((NA_ROW_UNROLL, 2 * GRID_W, NA_WIN_H * GRID_W), BF16)],
        compiler_params=_params(("arbitrary", "arbitrary"), 32),
        name="na_attention",
    )(rpb_rows, qkv, qkv, qkv)


def _t5_buckets(rel):
    half = T5_BUCKETS // 2
    max_exact = half // 2
    n = np.abs(rel)
    large = max_exact + (np.log(np.maximum(n, 1) / max_exact)
                         / np.log(T5_MAX_DIST / max_exact) * (half - max_exact)).astype(np.int32)
    large = np.minimum(large, half - 1)
    return (rel > 0).astype(np.int32) * half + np.where(n < max_exact, n, large)


SW_REL_PERIOD = 512
SW_BLOCK_UNROLL = 2


def _sw_rel_table(t5_table):
    k = np.arange(SW_REL_PERIOD)
    rel = np.where(k < SW_REL_PERIOD // 2, k, k - SW_REL_PERIOD)
    ok = np.abs(rel) <= SW_WINDOW
    vals = jnp.where(ok[:, None], t5_table[_t5_buckets(rel)].astype(F32) * LOG2E, NEG)
    return vals.T.reshape(SW_KV_HEADS, SW_GROUP, SW_REL_PERIOD)


def _sw_build_bias(rel_ref, bias_ref):
    for g in range(SW_GROUP):
        row = jnp.broadcast_to(rel_ref[0, g:g + 1, :], (SW_BLOCK, SW_REL_PERIOD))
        t = pltpu.roll(row, 0, 1, stride=1, stride_axis=0)
        rows = slice(g * SW_BLOCK, (g + 1) * SW_BLOCK)
        bias_ref[0, rows, :] = t[:, :SW_SPAN]
        bias_ref[1, rows, :] = jnp.concatenate([t[:, SW_SPAN:], t[:, :2 * SW_BLOCK]], axis=1)
        bias_ref[2, rows, :] = jnp.concatenate([t[:, 2 * SW_BLOCK:], t[:, :SW_BLOCK]], axis=1)


def _sw_kernel(sink_ref, rel_ref, q_ref, k_ref, v_ref, o_ref, bias_ref, s0_ref, s1_ref, p0_ref, p1_ref,
               t0_ref, t1_ref):
    j = pl.program_id(0)
    ones = jnp.ones((SW_SPAN, LANES), BF16)

    @pl.when(pl.program_id(1) == 0)
    def _():
        _sw_build_bias(rel_ref, bias_ref)

    masks = _half_masks(SW_BLOCK)
    lo_f = lax.broadcasted_iota(jnp.int32, (SW_BLOCK, LANES), 1) < HEAD_DIM
    sinks = [jnp.full((SW_BLOCK, LANES), sink_ref[j * SW_GROUP + g] * LOG2E, F32) for g in range(SW_GROUP)]
    groups = SW_NB // SW_BLOCK_UNROLL

    def group_blocks(i):
        for u in range(SW_BLOCK_UNROLL):
            n = jnp.clip(i, 0, groups - 1) * SW_BLOCK_UNROLL + u
            yield u, n, jnp.clip(n - 1, 0, SW_NB - 3)

    def block_rows(n):
        return pl.ds(pl.multiple_of(n * SW_BLOCK, SW_BLOCK), SW_BLOCK)

    def scores(i, s_ref):
        for u, n, first in group_blocks(i):
            kw = k_ref[0, pl.ds(pl.multiple_of(first * SW_BLOCK, SW_BLOCK), SW_SPAN), :]
            for g in range(SW_GROUP):
                q = q_ref[0, block_rows(n), (g // 2) * LANES:(g // 2 + 1) * LANES] * masks[g % 2]
                s_ref[u * SW_GROUP + g] = lax.dot_general(q, kw, (((1,), (1,)), ((), ())),
                                                          preferred_element_type=F32)

    def softmax(i, s_ref, p_ref, t_ref):
        for u, n, first in group_blocks(i):
            for g in range(SW_GROUP):
                c = u * SW_GROUP + g
                s = s_ref[c] + bias_ref[n - first, g * SW_BLOCK:(g + 1) * SW_BLOCK, :]
                m = jnp.maximum(jnp.broadcast_to(jnp.max(s, axis=-1, keepdims=True), (SW_BLOCK, LANES)), sinks[g])
                p_ref[c] = jnp.exp2(s - jnp.concatenate([m] * (SW_SPAN // LANES), axis=1)).astype(BF16)
                t_ref[c] = jnp.exp2(sinks[g] - m)

    def values(i, p_ref, t_ref):
        for u, n, first in group_blocks(i):
            vw = v_ref[0, pl.ds(pl.multiple_of(first * SW_BLOCK, SW_BLOCK), SW_SPAN), :]
            v_ones = jnp.concatenate([vw, ones], axis=1)
            heads = []
            for g in range(SW_GROUP):
                c = u * SW_GROUP + g
                o = jnp.dot(p_ref[c], v_ones, preferred_element_type=F32)
                heads.append(o[:, :LANES] / (o[:, LANES:] + t_ref[c]))
            out = jnp.concatenate([jnp.where(lo_f, heads[0], heads[1]), jnp.where(lo_f, heads[2], heads[3])],
                                  axis=1)
            o_ref[0, block_rows(n), :] = out.astype(BF16)

    def step(i, carry):
        values(2 * i - 1, p1_ref, t1_ref)
        scores(2 * i + 1, s1_ref)
        softmax(2 * i, s0_ref, p0_ref, t0_ref)
        values(2 * i, p0_ref, t0_ref)
        scores(2 * i + 2, s0_ref)
        softmax(2 * i + 1, s1_ref, p1_ref, t1_ref)
        return carry

    p1_ref[...] = jnp.ones_like(p1_ref)
    t1_ref[...] = jnp.ones_like(t1_ref)
    scores(0, s0_ref)
    lax.fori_loop(0, groups // 2, step, 0)
    values(groups - 1, p1_ref, t1_ref)


def _sw_attention(qkv, rel_table, sinks):
    qw = SW_GROUP * HEAD_DIM
    k_off = SW_Q_HEADS * HEAD_DIM // LANES
    v_off = k_off + SW_KV_HEADS
    chains = SW_BLOCK_UNROLL * SW_GROUP
    return pl.pallas_call(
        _sw_kernel,
        grid=(SW_KV_HEADS, BATCH),
        in_specs=[
            pl.BlockSpec(memory_space=pltpu.SMEM),
            pl.BlockSpec((1, SW_GROUP, SW_REL_PERIOD), lambda j, b: (j, 0, 0)),
            pl.BlockSpec((1, SEQ, qw), lambda j, b: (b, 0, j)),
            pl.BlockSpec((1, SEQ, LANES), lambda j, b: (b, 0, k_off + j)),
            pl.BlockSpec((1, SEQ, LANES), lambda j, b: (b, 0, v_off + j)),
        ],
        out_specs=pl.BlockSpec((1, SEQ, qw), lambda j, b: (b, 0, j)),
        out_shape=jax.ShapeDtypeStruct((BATCH, SEQ, SW_Q_HEADS * HEAD_DIM), BF16),
        scratch_shapes=[pltpu.VMEM((3, SW_GROUP * SW_BLOCK, SW_SPAN), F32)]
        + [pltpu.VMEM((chains, SW_BLOCK, SW_SPAN), F32)] * 2
        + [pltpu.VMEM((chains, SW_BLOCK, SW_SPAN), BF16)] * 2
        + [pltpu.VMEM((chains, SW_BLOCK, LANES), F32)] * 2,
        compiler_params=_params(("arbitrary", "arbitrary"), 32),
        name="sw_attention",
    )(sinks, rel_table, qkv, qkv, qkv)


def _post_attn_kernel(o_ref, wo_ref, x_ref, gate_ref, g_ref, sc_ref, sh_ref, wr_ref,
                      xo_ref, h_ref, aff_ref):
    y = jnp.dot(o_ref[...], wo_ref[...], preferred_element_type=F32)
    xn = x_ref[...] + gate_ref[0] * y
    xo_ref[...] = xn
    h = _norm_mod(xn, g_ref[0], sc_ref[0], sh_ref[0])
    for j in range(TOKEN_TILE_ROWS):
        h_ref[pl.ds(j, ROW_TILE, stride=TOKEN_TILE_ROWS), :] = h[:, j * LANES:(j + 1) * LANES]
    hb = h.astype(BF16)
    logits = lax.dot_general(wr_ref[...], hb, (((1,), (1,)), ((), ())), preferred_element_type=F32)
    aff_ref[...] = _softmax0(logits)


def _post_attn(o, w_o, x, norm_g3, mod, w_router_t, layer):
    rows = BATCH * SEQ
    return pl.pallas_call(
        _post_attn_kernel,
        grid=(rows // ROW_TILE,),
        in_specs=[
            pl.BlockSpec((ROW_TILE, D_MODEL), lambda i: (i, 0)),
            pl.BlockSpec((D_MODEL, D_MODEL), lambda i: (0, 0)),
            pl.BlockSpec((ROW_TILE, D_MODEL), lambda i: (i, 0)),
            _mod_spec(layer, 2),
            pl.BlockSpec((1, 1, D_MODEL), lambda i: (layer * 2 + 1, 0, 0)),
            _mod_spec(layer, 4),
            _mod_spec(layer, 3),
            pl.BlockSpec((N_EXPERTS, D_MODEL), lambda i: (0, 0)),
        ],
        out_specs=[
            pl.BlockSpec((ROW_TILE, D_MODEL), lambda i: (i, 0)),
            pl.BlockSpec((ROW_TILE * TOKEN_TILE_ROWS, LANES), lambda i: (i, 0)),
            pl.BlockSpec((N_EXPERTS, ROW_TILE), lambda i: (0, i)),
        ],
        out_shape=[
            jax.ShapeDtypeStruct((rows, D_MODEL), F32),
            jax.ShapeDtypeStruct((rows * TOKEN_TILE_ROWS, LANES), F32),
            jax.ShapeDtypeStruct((N_EXPERTS, rows), F32),
        ],
        compiler_params=_params(("arbitrary",), 48),
        name="post_attn",
    )(o, w_o, x, mod, norm_g3, mod, mod, w_router_t)


ROUTE_SEQS = BATCH
ROUTE_ROWS = ROUTE_SEQS * N_EXPERTS
ROUTE_BITS = SEQ.bit_length() - 1
ROUTE_VALID = 1 << (2 * ROUTE_BITS)


def _prefix_count(x):
    rows = x.shape[0]
    nchunk = SEQ // PREFIX_CHUNK
    r = lax.broadcasted_iota(jnp.int32, (PREFIX_CHUNK, PREFIX_CHUNK), 0)
    c = lax.broadcasted_iota(jnp.int32, (PREFIX_CHUNK, PREFIX_CHUNK), 1)
    upper = jnp.where(r < c, 1.0, 0.0).astype(BF16)
    chunks = [x[:, k * PREFIX_CHUNK:(k + 1) * PREFIX_CHUNK] for k in range(nchunk)]
    local = jnp.dot(jnp.concatenate(chunks, axis=0).astype(BF16), upper, preferred_element_type=F32)
    out = []
    offset = jnp.zeros((rows, 1), F32)
    for k in range(nchunk):
        out.append(local[k * rows:(k + 1) * rows] + offset)
        offset = offset + jnp.sum(chunks[k], axis=1, keepdims=True)
    return jnp.concatenate(out, axis=1)


def _route_kernel(aff_ref, slot_ref, tok_ref, gate_ref):
    aff = jnp.concatenate([aff_ref[:, b * SEQ:(b + 1) * SEQ] for b in range(ROUTE_SEQS)], axis=0)
    bits = pltpu.bitcast(aff, jnp.int32)

    def count(mask):
        return jnp.sum(jnp.where(mask, 1.0, 0.0), axis=1, keepdims=True)

    def search(_, bounds):
        lo, hi = bounds
        mid = lo + ((hi - lo) >> 1)
        ge = count(bits >= mid) >= CAP
        return jnp.where(ge, mid, lo), jnp.where(ge, hi, mid)

    lo0 = jnp.zeros((ROUTE_ROWS, 1), jnp.int32)
    hi0 = jnp.full((ROUTE_ROWS, 1), 0x7F800000, jnp.int32)
    tau, _ = lax.fori_loop(0, 31, search, (lo0, hi0))
    gt = bits > tau
    eq = jnp.where(bits == tau, 1.0, 0.0)
    need = CAP - count(gt)
    sel = jnp.where(gt, 1.0, jnp.where(_prefix_count(eq) < need, eq, 0.0))
    pos = _prefix_count(sel).astype(jnp.int32)
    chosen = sel > 0.5
    slot_ref[...] = jnp.where(chosen, pos, -1)
    lane = lax.broadcasted_iota(jnp.int32, (ROUTE_ROWS, SEQ), 1)
    packed = jnp.where(chosen, ROUTE_VALID | (lane << ROUTE_BITS) | (lane - pos), 0)
    gate = aff
    for k in range(ROUTE_BITS):
        step = 1 << k
        from_right = pltpu.roll(packed, SEQ - step, 1)
        gate_right = pltpu.roll(gate, SEQ - step, 1)
        arrives = (from_right & step) != 0
        stays = (packed & step) == 0
        packed = jnp.where(arrives, from_right, jnp.where(stays, packed, 0))
        gate = jnp.where(arrives, gate_right, gate)
    tok_ref[...] = (packed[:, :CAP] >> ROUTE_BITS) & (SEQ - 1)
    gate_ref[...] = gate[:, :CAP]


def _route(aff_t):
    return pl.pallas_call(
        _route_kernel,
        grid=(BATCH // ROUTE_SEQS,),
        in_specs=[pl.BlockSpec((N_EXPERTS, ROUTE_SEQS * SEQ), lambda i: (0, i))],
        out_specs=[pl.BlockSpec((ROUTE_ROWS, SEQ), lambda i: (i, 0)),
                   pl.BlockSpec((ROUTE_ROWS, CAP), lambda i: (i, 0)),
                   pl.BlockSpec((ROUTE_ROWS, CAP), lambda i: (i, 0))],
        out_shape=[jax.ShapeDtypeStruct((BATCH * N_EXPERTS, SEQ), jnp.int32),
                   jax.ShapeDtypeStruct((BATCH * N_EXPERTS, CAP), jnp.int32),
                   jax.ShapeDtypeStruct((BATCH * N_EXPERTS, CAP), F32)],
        compiler_params=_params(("arbitrary",), 48),
        name="route",
    )(aff_t)


GATHER_ROWS = BATCH * CAP
FF_STEPS = EXPERT_FF // FF_TILE
ROWS_PER_CHUNK = GATHER_ROWS // (FF_STEPS * BATCH)
PROLOGUE_UNROLL = 8


def _expert_kernel(tok_ref, h_ref, gates_ref, wg_ref, wu_ref, wd_ref, y_ref, x_ref, sem, xb_ref, acc_ref):
    e = pl.program_id(0)
    f = pl.program_id(1)
    slot = lax.rem(e, 2)
    next_slot = 1 - slot
    next_e = jnp.minimum(e + 1, N_EXPERTS - 1)

    tile = TOKEN_TILE_ROWS

    def row_copy(expert, row, dst_slot):
        src = pl.multiple_of(tok_ref[expert * GATHER_ROWS + row] * tile, tile)
        dst = pl.multiple_of(row * tile, tile)
        return pltpu.make_async_copy(h_ref.at[pl.ds(src, tile), :], x_ref.at[dst_slot, pl.ds(dst, tile), :],
                                     sem.at[dst_slot])

    def wait_rows(dst_slot):
        pltpu.make_async_copy(h_ref.at[pl.ds(0, GATHER_ROWS * tile), :], x_ref.at[dst_slot],
                              sem.at[dst_slot]).wait()

    @pl.when(jnp.logical_and(e == 0, f == 0))
    def _():
        def issue(i, carry):
            for u in range(PROLOGUE_UNROLL):
                row_copy(0, i * PROLOGUE_UNROLL + u, 0).start()
            return carry
        lax.fori_loop(0, GATHER_ROWS // PROLOGUE_UNROLL, issue, 0)

    @pl.when(f == 0)
    def _():
        wait_rows(slot)
        acc_ref[...] = jnp.zeros_like(acc_ref)
        for b in range(BATCH):
            cols = [x_ref[slot, pl.ds(b * CAP * tile + j, CAP, stride=tile), :] for j in range(tile)]
            xb_ref[b] = jnp.concatenate(cols, axis=1).astype(BF16)

    wg = wg_ref[0, 0].astype(BF16)
    wu = wu_ref[0, 0].astype(BF16)
    wd = wd_ref[0, 0].astype(BF16)
    for b in range(BATCH):
        first = (f * BATCH + b) * ROWS_PER_CHUNK
        for j in range(ROWS_PER_CHUNK):
            row_copy(next_e, first + j, next_slot).start()
        x = xb_ref[b]
        a = jnp.dot(x, wg, preferred_element_type=F32)
        u = jnp.dot(x, wu, preferred_element_type=F32)
        act = ((a * jax.nn.sigmoid(a)) * u).astype(BF16)
        acc_ref[b] += jnp.dot(act, wd, preferred_element_type=F32)

    @pl.when(f == FF_STEPS - 1)
    def _():
        gates = gates_ref[0]
        for b in range(BATCH):
            y_ref[0, b] = (acc_ref[b] * gates[:, b:b + 1]).astype(BF16)

    @pl.when(jnp.logical_and(e == N_EXPERTS - 1, f == FF_STEPS - 1))
    def _():
        wait_rows(next_slot)


def _experts(tok, h, gates, w_gate, w_up, w_down, layer):
    grid_spec = pltpu.PrefetchScalarGridSpec(
        num_scalar_prefetch=1,
        grid=(N_EXPERTS, FF_STEPS),
        in_specs=[
            pl.BlockSpec(memory_space=pl.ANY),
            pl.BlockSpec((1, CAP, BATCH), lambda e, f, tok: (e, 0, 0)),
            pl.BlockSpec((1, 1, D_MODEL, FF_TILE), lambda e, f, tok: (layer, e, 0, f)),
            pl.BlockSpec((1, 1, D_MODEL, FF_TILE), lambda e, f, tok: (layer, e, 0, f)),
            pl.BlockSpec((1, 1, FF_TILE, D_MODEL), lambda e, f, tok: (layer, e, f, 0)),
        ],
        out_specs=pl.BlockSpec((1, BATCH, CAP, D_MODEL), lambda e, f, tok: (e, 0, 0, 0)),
        scratch_shapes=[pltpu.VMEM((2, GATHER_ROWS * TOKEN_TILE_ROWS, LANES), F32), pltpu.SemaphoreType.DMA((2,)),
                        pltpu.VMEM((BATCH, CAP, D_MODEL), BF16), pltpu.VMEM((BATCH, CAP, D_MODEL), F32)],
    )
    return pl.pallas_call(
        _expert_kernel,
        grid_spec=grid_spec,
        out_shape=jax.ShapeDtypeStruct((N_EXPERTS, BATCH, CAP, D_MODEL), BF16),
        compiler_params=_params(("arbitrary", "arbitrary"), 60),
        name="moe_experts",
    )(tok, h, gates, w_gate, w_up, w_down)


def _combine_kernel(slot_ref, y_ref, x_ref, gate_ref, g_ref, o_ref, *, final):
    lane = lax.broadcasted_iota(jnp.int32, (COMBINE_ROWS, CAP), 1)
    slots = slot_ref[0]
    onehot = jnp.concatenate(
        [jnp.where(slots[:, e:e + 1] == lane, 1.0, 0.0).astype(BF16) for e in range(N_EXPERTS)], axis=1)
    moe = jnp.dot(onehot, y_ref[...].reshape(N_EXPERTS * CAP, D_MODEL), preferred_element_type=F32)
    x = x_ref[0] + gate_ref[0] * moe
    if final:
        x = (x * lax.rsqrt(jnp.mean(x * x, axis=-1, keepdims=True) + RMS_EPS)) * g_ref[...]
    o_ref[0] = x


def _combine(slot_t, y, x, mod, final_g, layer, final):
    return pl.pallas_call(
        functools.partial(_combine_kernel, final=final),
        grid=(BATCH, SEQ // COMBINE_ROWS),
        in_specs=[
            pl.BlockSpec((1, COMBINE_ROWS, N_EXPERTS), lambda b, t: (b, t, 0)),
            pl.BlockSpec((N_EXPERTS, 1, CAP, D_MODEL), lambda b, t: (0, b, 0, 0)),
            pl.BlockSpec((1, COMBINE_ROWS, D_MODEL), lambda b, t: (b, t, 0)),
            pl.BlockSpec((1, 1, D_MODEL), lambda b, t: ((layer * 6 + 5) * BATCH + b, 0, 0)),
            pl.BlockSpec((1, D_MODEL), lambda b, t: (0, 0)),
        ],
        out_specs=pl.BlockSpec((1, COMBINE_ROWS, D_MODEL), lambda b, t: (b, t, 0)),
        out_shape=jax.ShapeDtypeStruct((BATCH, SEQ, D_MODEL), F32),
        compiler_params=_params(("arbitrary", "arbitrary"), 48),
        name="moe_combine",
    )(slot_t, y, x, mod, final_g.reshape(1, D_MODEL))


def _na_qkv_weight(w):
    nq = NA_HEADS * HEAD_DIM
    scale = jnp.concatenate([jnp.full((nq,), Q_SCALE, F32), jnp.ones((2 * nq,), F32)])
    return (w * scale).astype(BF16)


def _sw_qkv_weight(w):
    nq = SW_Q_HEADS * HEAD_DIM
    nkv = SW_KV_HEADS * HEAD_DIM
    dup = lambda t: jnp.concatenate([t.reshape(D_MODEL, SW_KV_HEADS, 1, HEAD_DIM)] * 2, axis=2).reshape(
        D_MODEL, 2 * nkv)
    wq = w[:, :nq] * Q_SCALE
    return jnp.concatenate([wq, dup(w[:, nq:nq + nkv]), dup(w[:, nq + nkv:])], axis=1).astype(BF16)


def kernel(x, c, ada_w, ada_b, norm_g, na_w_qkv, na_w_o, na_rpb, sw_w_qkv, sw_w_o, sw_sinks, t5_bias,
           moe_w_router, moe_w_gate, moe_w_up, moe_w_down, final_g):
    mod = _ada(c, ada_w, ada_b)
    norm_g3 = norm_g.reshape(DEPTH * 2, 1, D_MODEL)
    x = x.reshape(BATCH * SEQ, D_MODEL)
    for layer in range(DEPTH):
        j = layer // N_MIXERS
        if layer % N_MIXERS == 0:
            qkv = _qkv(x, norm_g3, mod, _na_qkv_weight(na_w_qkv[j]), layer)
            o = _na_attention(qkv.reshape(BATCH, SEQ, -1), _na_rpb_rows(na_rpb[j]))
            w_o = na_w_o[j]
        else:
            qkv = _qkv(x, norm_g3, mod, _sw_qkv_weight(sw_w_qkv[j]), layer)
            o = _sw_attention(qkv.reshape(BATCH, SEQ, -1), _sw_rel_table(t5_bias), sw_sinks[j])
            w_o = sw_w_o[j]
        w_router = moe_w_router[layer].astype(BF16)
        x, h, aff_t = _post_attn(o.reshape(BATCH * SEQ, -1), w_o.astype(BF16), x, norm_g3, mod,
                                 w_router.T, layer)
        slot, tok, gate = _route(aff_t)
        tok = tok.reshape(BATCH, N_EXPERTS, CAP)
        rows = tok.transpose(1, 0, 2) + (jnp.arange(BATCH, dtype=jnp.int32) * SEQ)[None, :, None]
        gates = gate.reshape(BATCH, N_EXPERTS, CAP).transpose(1, 2, 0)
        y = _experts(rows.reshape(-1), h, gates, moe_w_gate, moe_w_up, moe_w_down, layer)
        slot_t = slot.reshape(BATCH, N_EXPERTS, SEQ).transpose(0, 2, 1)
        x = _combine(slot_t, y, x.reshape(BATCH, SEQ, D_MODEL), mod, final_g, layer,
                     final=layer == DEPTH - 1).reshape(BATCH * SEQ, D_MODEL)
    return x.reshape(BATCH, SEQ, D_MODEL)
```

```python
import functools

import numpy as np
import jax
import jax.numpy as jnp
from jax import lax
from jax.experimental import pallas as pl
from jax.experimental.pallas import tpu as pltpu

D_MODEL = 1024
BATCH = 8
SEQ = 2048
DEPTH = 2
GRID_W = 64
ROWS = SEQ // GRID_W
N_MIXERS = 2
HEAD_DIM = 64
NA_HEADS = 16
NA_WIN_H = 8
NA_WIN_W = 16
SW_Q_HEADS = 16
SW_KV_HEADS = 4
SW_GROUP = SW_Q_HEADS // SW_KV_HEADS
SW_WINDOW = 128
SW_BLOCK = 128
SW_NB = SEQ // SW_BLOCK
SW_SPAN = 3 * SW_BLOCK
T5_BUCKETS = 32
T5_MAX_DIST = 128
N_EXPERTS = 16
EXPERT_FF = 2048
EC_CAPACITY = 2
CAP = EC_CAPACITY * SEQ // N_EXPERTS
RMS_EPS = 1e-6
NEG = -1e30
LOG2E = 1.4426950408889634
Q_SCALE = HEAD_DIM ** -0.5 * LOG2E

LANES = 128
TOKEN_TILE_ROWS = D_MODEL // LANES
MIB = 1024 * 1024
F32 = jnp.float32
BF16 = jnp.bfloat16

ROW_TILE = 512
FF_TILE = 512
COMBINE_ROWS = 512
PREFIX_CHUNK = 256


def _params(semantics, vmem_mib, flags=None):
    return pltpu.CompilerParams(dimension_semantics=semantics, vmem_limit_bytes=vmem_mib * MIB, flags=flags)


def _norm_mod(x, g, sc, sh):
    y = x * lax.rsqrt(jnp.mean(x * x, axis=-1, keepdims=True) + RMS_EPS)
    return (y * g) * (1.0 + sc) + sh


def _softmax0(z):
    z = z - jnp.max(z, axis=0, keepdims=True)
    p = jnp.exp(z)
    return p / jnp.sum(p, axis=0, keepdims=True)


def _half_masks(rows):
    lane = lax.broadcasted_iota(jnp.int32, (rows, LANES), 1)
    lo = jnp.where(lane < HEAD_DIM, 1.0, 0.0).astype(BF16)
    hi = jnp.where(lane < HEAD_DIM, 0.0, 1.0).astype(BF16)
    return lo, hi


def _ada_kernel(c_ref, w_ref, b_ref, o_ref):
    c = c_ref[...]
    act = (c * jax.nn.sigmoid(c)).astype(BF16)
    o_ref[0] = jnp.dot(act, w_ref[0].astype(BF16), preferred_element_type=F32) + b_ref[0]


def _ada(c, ada_w, ada_b):
    out = pl.pallas_call(
        _ada_kernel,
        grid=(DEPTH, 6),
        in_specs=[
            pl.BlockSpec((BATCH, D_MODEL), lambda l, k: (0, 0)),
            pl.BlockSpec((1, D_MODEL, D_MODEL), lambda l, k: (l, 0, k)),
            pl.BlockSpec((1, 1, D_MODEL), lambda l, k: (l * 6 + k, 0, 0)),
        ],
        out_specs=pl.BlockSpec((1, BATCH, D_MODEL), lambda l, k: (l * 6 + k, 0, 0)),
        out_shape=jax.ShapeDtypeStruct((DEPTH * 6, BATCH, D_MODEL), F32),
        compiler_params=_params(("arbitrary", "arbitrary"), 32),
        name="ada_mod",
    )(c, ada_w, ada_b.reshape(DEPTH * 6, 1, D_MODEL))
    return out.reshape(DEPTH * 6 * BATCH, 1, D_MODEL)


def _mod_spec(layer, chunk):
    tiles_per_seq = SEQ // ROW_TILE
    return pl.BlockSpec((1, 1, D_MODEL),
                        lambda i: ((layer * 6 + chunk) * BATCH + i // tiles_per_seq, 0, 0))


def _qkv_kernel(x_ref, g_ref, sc_ref, sh_ref, w_ref, o_ref):
    h = _norm_mod(x_ref[...], g_ref[0], sc_ref[0], sh_ref[0])
    o_ref[...] = jnp.dot(h.astype(BF16), w_ref[...], preferred_element_type=F32).astype(BF16)


def _qkv(x, norm_g3, mod, w, layer):
    n = w.shape[1]
    return pl.pallas_call(
        _qkv_kernel,
        grid=(BATCH * SEQ // ROW_TILE,),
        in_specs=[
            pl.BlockSpec((ROW_TILE, D_MODEL), lambda i: (i, 0)),
            pl.BlockSpec((1, 1, D_MODEL), lambda i: (layer * 2, 0, 0)),
            _mod_spec(layer, 1),
            _mod_spec(layer, 0),
            pl.BlockSpec((D_MODEL, n), lambda i: (0, 0)),
        ],
        out_specs=pl.BlockSpec((ROW_TILE, n), lambda i: (i, 0)),
        out_shape=jax.ShapeDtypeStruct((BATCH * SEQ, n), BF16),
        compiler_params=_params(("arbitrary",), 48),
        name="norm_qkv",
    )(x, norm_g3, mod, mod, w)


NA_BIAS_ROWS = 2 * NA_WIN_H - 1
NA_ROW_UNROLL = 4


def _na_rpb_rows(rpb):
    w = NA_WIN_W - 1
    rpb = rpb.astype(F32) * LOG2E
    pad = jnp.zeros(rpb.shape[:2] + (LANES - 2 * w - 1,), F32)
    return jnp.concatenate([rpb[..., w:], pad, rpb[..., :w]], axis=-1)


def _na_build_bias(w_ref, bias_ref):
    lane = lax.broadcasted_iota(jnp.int32, (GRID_W, LANES), 1)
    col = lax.broadcasted_iota(jnp.int32, (GRID_W, LANES), 0)
    kc = jnp.bitwise_and(lane, GRID_W - 1)
    cstart = jnp.clip(col - NA_WIN_W // 2, 0, GRID_W - NA_WIN_W)
    in_window = jnp.where(kc >= cstart, jnp.where(kc < cstart + NA_WIN_W, 1.0, 0.0), 0.0) > 0.5
    low_half = lane < GRID_W

    def toeplitz(head, a, shift):
        row = jnp.broadcast_to(w_ref[head, a:a + 1, :], (GRID_W, LANES))
        return pltpu.roll(row, shift, 1, stride=1, stride_axis=0)

    for head in range(2):
        tiles = [jnp.where(in_window, jnp.where(low_half, toeplitz(head, a, 0), toeplitz(head, a + 1, GRID_W)), NEG)
                 for a in range(NA_BIAS_ROWS - 1)]
        for d in range(NA_WIN_H):
            for m in range(NA_WIN_H // 2):
                bias_ref[d, head * GRID_W:(head + 1) * GRID_W, m * LANES:(m + 1) * LANES] = (
                    tiles[NA_WIN_H - 1 - d + 2 * m])


def _na_kernel(w_ref, q_ref, k_ref, v_ref, o_ref, bias_ref, s0_ref, s1_ref, p0_ref, p1_ref):
    @pl.when(pl.program_id(1) == 0)
    def _():
        _na_build_bias(w_ref, bias_ref)

    lo, hi = _half_masks(GRID_W)
    lo_f = lax.broadcasted_iota(jnp.int32, (GRID_W, LANES), 1) < HEAD_DIM
    span = NA_WIN_H * GRID_W
    ones = jnp.ones((span, LANES), BF16)
    groups = ROWS // NA_ROW_UNROLL

    def group_rows(i):
        for u in range(NA_ROW_UNROLL):
            r = jnp.clip(i, 0, groups - 1) * NA_ROW_UNROLL + u
            yield u, r, jnp.clip(r - NA_WIN_H // 2, 0, ROWS - NA_WIN_H)

    def scores(i, s_ref):
        for u, r, rs in group_rows(i):
            q = q_ref[0, pl.ds(pl.multiple_of(r * GRID_W, GRID_W), GRID_W), :]
            lhs = jnp.concatenate([q * lo, q * hi], axis=0)
            kw = k_ref[0, pl.ds(pl.multiple_of(rs * GRID_W, GRID_W), span), :]
            s_ref[u] = lax.dot_general(lhs, kw, (((1,), (1,)), ((), ())), preferred_element_type=F32)

    def softmax(i, s_ref, p_ref):
        for u, r, rs in group_rows(i):
            s = s_ref[u] + bias_ref[r - rs]
            m = jnp.broadcast_to(jnp.max(s, axis=-1, keepdims=True), (2 * GRID_W, LANES))
            p_ref[u] = jnp.exp2(s - jnp.concatenate([m] * (span // LANES), axis=1)).astype(BF16)

    def values(i, p_ref):
        for u, r, rs in group_rows(i):
            vw = v_ref[0, pl.ds(pl.multiple_of(rs * GRID_W, GRID_W), span), :]
            o = jnp.dot(p_ref[u], jnp.concatenate([vw, ones], axis=1), preferred_element_type=F32)
            o = o[:, :LANES] / o[:, LANES:]
            out = jnp.where(lo_f, o[:GRID_W], o[GRID_W:])
            o_ref[0, pl.ds(pl.multiple_of(r * GRID_W, GRID_W), GRID_W), :] = out.astype(BF16)

    def step(i, carry):
        values(2 * i - 1, p1_ref)
        scores(2 * i + 1, s1_ref)
        softmax(2 * i, s0_ref, p0_ref)
        values(2 * i, p0_ref)
        scores(2 * i + 2, s0_ref)
        softmax(2 * i + 1, s1_ref, p1_ref)
        return carry

    p1_ref[...] = jnp.ones_like(p1_ref)
    scores(0, s0_ref)
    lax.fori_loop(0, groups // 2, step, 0)
    values(groups - 1, p1_ref)


def _na_attention(qkv, rpb_rows):
    pairs = NA_HEADS // 2
    return pl.pallas_call(
        _na_kernel,
        grid=(pairs, BATCH),
        in_specs=[
            pl.BlockSpec((2, NA_BIAS_ROWS, LANES), lambda p, b: (p, 0, 0)),
            pl.BlockSpec((1, SEQ, LANES), lambda p, b: (b, 0, p)),
            pl.BlockSpec((1, SEQ, LANES), lambda p, b: (b, 0, pairs + p)),
            pl.BlockSpec((1, SEQ, LANES), lambda p, b: (b, 0, 2 * pairs + p)),
        ],
        out_specs=pl.BlockSpec((1, SEQ, LANES), lambda p, b: (b, 0, p)),
        out_shape=jax.ShapeDtypeStruct((BATCH, SEQ, NA_HEADS * HEAD_DIM), BF16),
        scratch_shapes=[pltpu.VMEM((NA_WIN_H, 2 * GRID_W, NA_WIN_H * GRID_W), F32),
                        pltpu.VMEM((NA_ROW_UNROLL, 2 * GRID_W, NA_WIN_H * GRID_W), F32),
                        pltpu.VMEM((NA_ROW_UNROLL, 2 * GRID_W, NA_WIN_H * GRID_W), F32),
                        pltpu.VMEM((NA_ROW_UNROLL, 2 * GRID_W, NA_WIN_H * GRID_W), BF16),
                        pltpu.VMEM((NA_ROW_UNROLL, 2 * GRID_W, NA_WIN_H * GRID_W), BF16)],
        compiler_params=_params(("arbitrary", "arbitrary"), 32),
        name="na_attention",
    )(rpb_rows, qkv, qkv, qkv)


def _t5_buckets(rel):
    half = T5_BUCKETS // 2
    max_exact = half // 2
    n = np.abs(rel)
    large = max_exact + (np.log(np.maximum(n, 1) / max_exact)
                         / np.log(T5_MAX_DIST / max_exact) * (half - max_exact)).astype(np.int32)
    large = np.minimum(large, half - 1)
    return (rel > 0).astype(np.int32) * half + np.where(n < max_exact, n, large)


SW_REL_PERIOD = 512
SW_BLOCK_UNROLL = 2


def _sw_rel_table(t5_table):
    k = np.arange(SW_REL_PERIOD)
    rel = np.where(k < SW_REL_PERIOD // 2, k, k - SW_REL_PERIOD)
    ok = np.abs(rel) <= SW_WINDOW
    vals = jnp.where(ok[:, None], t5_table[_t5_buckets(rel)].astype(F32) * LOG2E, NEG)
    return vals.T.reshape(SW_KV_HEADS, SW_GROUP, SW_REL_PERIOD)


def _sw_build_bias(rel_ref, bias_ref):
    for g in range(SW_GROUP):
        row = jnp.broadcast_to(rel_ref[0, g:g + 1, :], (SW_BLOCK, SW_REL_PERIOD))
        t = pltpu.roll(row, 0, 1, stride=1, stride_axis=0)
        rows = slice(g * SW_BLOCK, (g + 1) * SW_BLOCK)
        bias_ref[0, rows, :] = t[:, :SW_SPAN]
        bias_ref[1, rows, :] = jnp.concatenate([t[:, SW_SPAN:], t[:, :2 * SW_BLOCK]], axis=1)
        bias_ref[2, rows, :] = jnp.concatenate([t[:, 2 * SW_BLOCK:], t[:, :SW_BLOCK]], axis=1)


def _sw_kernel(sink_ref, rel_ref, q_ref, k_ref, v_ref, o_ref, bias_ref, s0_ref, s1_ref, p0_ref, p1_ref,
               t0_ref, t1_ref):
    j = pl.program_id(0)
    ones = jnp.ones((SW_SPAN, LANES), BF16)

    @pl.when(pl.program_id(1) == 0)
    def _():
        _sw_build_bias(rel_ref, bias_ref)

    masks = _half_masks(SW_BLOCK)
    lo_f = lax.broadcasted_iota(jnp.int32, (SW_BLOCK, LANES), 1) < HEAD_DIM
    sinks = [jnp.full((SW_BLOCK, LANES), sink_ref[j * SW_GROUP + g] * LOG2E, F32) for g in range(SW_GROUP)]
    groups = SW_NB // SW_BLOCK_UNROLL

    def group_blocks(i):
        for u in range(SW_BLOCK_UNROLL):
            n = jnp.clip(i, 0, groups - 1) * SW_BLOCK_UNROLL + u
            yield u, n, jnp.clip(n - 1, 0, SW_NB - 3)

    def block_rows(n):
        return pl.ds(pl.multiple_of(n * SW_BLOCK, SW_BLOCK), SW_BLOCK)

    def scores(i, s_ref):
        for u, n, first in group_blocks(i):
            kw = k_ref[0, pl.ds(pl.multiple_of(first * SW_BLOCK, SW_BLOCK), SW_SPAN), :]
            for g in range(SW_GROUP):
                q = q_ref[0, block_rows(n), (g // 2) * LANES:(g // 2 + 1) * LANES] * masks[g % 2]
                s_ref[u * SW_GROUP + g] = lax.dot_general(q, kw, (((1,), (1,)), ((), ())),
                                                          preferred_element_type=F32)

    def softmax(i, s_ref, p_ref, t_ref):
        for u, n, first in group_blocks(i):
            for g in range(SW_GROUP):
                c = u * SW_GROUP + g
                s = s_ref[c] + bias_ref[n - first, g * SW_BLOCK:(g + 1) * SW_BLOCK, :]
                m = jnp.maximum(jnp.broadcast_to(jnp.max(s, axis=-1, keepdims=True), (SW_BLOCK, LANES)), sinks[g])
                p_ref[c] = jnp.exp2(s - jnp.concatenate([m] * (SW_SPAN // LANES), axis=1)).astype(BF16)
                t_ref[c] = jnp.exp2(sinks[g] - m)

    def values(i, p_ref, t_ref):
        for u, n, first in group_blocks(i):
            vw = v_ref[0, pl.ds(pl.multiple_of(first * SW_BLOCK, SW_BLOCK), SW_SPAN), :]
            v_ones = jnp.concatenate([vw, ones], axis=1)
            heads = []
            for g in range(SW_GROUP):
                c = u * SW_GROUP + g
                o = jnp.dot(p_ref[c], v_ones, preferred_element_type=F32)
                heads.append(o[:, :LANES] / (o[:, LANES:] + t_ref[c]))
            out = jnp.concatenate([jnp.where(lo_f, heads[0], heads[1]), jnp.where(lo_f, heads[2], heads[3])],
                                  axis=1)
            o_ref[0, block_rows(n), :] = out.astype(BF16)

    def step(i, carry):
        values(2 * i - 1, p1_ref, t1_ref)
        scores(2 * i + 1, s1_ref)
        softmax(2 * i, s0_ref, p0_ref, t0_ref)
        values(2 * i, p0_ref, t0_ref)
        scores(2 * i + 2, s0_ref)
        softmax(2 * i + 1, s1_ref, p1_ref, t1_ref)
        return carry

    p1_ref[...] = jnp.ones_like(p1_ref)
    t1_ref[...] = jnp.ones_like(t1_ref)
    scores(0, s0_ref)
    lax.fori_loop(0, groups // 2, step, 0)
    values(groups - 1, p1_ref, t1_ref)


def _sw_attention(qkv, rel_table, sinks):
    qw = SW_GROUP * HEAD_DIM
    k_off = SW_Q_HEADS * HEAD_DIM // LANES
    v_off = k_off + SW_KV_HEADS
    chains = SW_BLOCK_UNROLL * SW_GROUP
    return pl.pallas_call(
        _sw_kernel,
        grid=(SW_KV_HEADS, BATCH),
        in_specs=[
            pl.BlockSpec(memory_space=pltpu.SMEM),
            pl.BlockSpec((1, SW_GROUP, SW_REL_PERIOD), lambda j, b: (j, 0, 0)),
            pl.BlockSpec((1, SEQ, qw), lambda j, b: (b, 0, j)),
            pl.BlockSpec((1, SEQ, LANES), lambda j, b: (b, 0, k_off + j)),
            pl.BlockSpec((1, SEQ, LANES), lambda j, b: (b, 0, v_off + j)),
        ],
        out_specs=pl.BlockSpec((1, SEQ, qw), lambda j, b: (b, 0, j)),
        out_shape=jax.ShapeDtypeStruct((BATCH, SEQ, SW_Q_HEADS * HEAD_DIM), BF16),
        scratch_shapes=[pltpu.VMEM((3, SW_GROUP * SW_BLOCK, SW_SPAN), F32)]
        + [pltpu.VMEM((chains, SW_BLOCK, SW_SPAN), F32)] * 2
        + [pltpu.VMEM((chains, SW_BLOCK, SW_SPAN), BF16)] * 2
        + [pltpu.VMEM((chains, SW_BLOCK, LANES), F32)] * 2,
        compiler_params=_params(("arbitrary", "arbitrary"), 32),
        name="sw_attention",
    )(sinks, rel_table, qkv, qkv, qkv)


def _post_attn_kernel(o_ref, wo_ref, x_ref, gate_ref, g_ref, sc_ref, sh_ref, wr_ref,
                      xo_ref, h_ref, aff_ref):
    y = jnp.dot(o_ref[...], wo_ref[...], preferred_element_type=F32)
    xn = x_ref[...] + gate_ref[0] * y
    xo_ref[...] = xn
    h = _norm_mod(xn, g_ref[0], sc_ref[0], sh_ref[0])
    for j in range(TOKEN_TILE_ROWS):
        h_ref[pl.ds(j, ROW_TILE, stride=TOKEN_TILE_ROWS), :] = h[:, j * LANES:(j + 1) * LANES]
    hb = h.astype(BF16)
    logits = lax.dot_general(wr_ref[...], hb, (((1,), (1,)), ((), ())), preferred_element_type=F32)
    aff_ref[...] = _softmax0(logits)


def _post_attn(o, w_o, x, norm_g3, mod, w_router_t, layer):
    rows = BATCH * SEQ
    return pl.pallas_call(
        _post_attn_kernel,
        grid=(rows // ROW_TILE,),
        in_specs=[
            pl.BlockSpec((ROW_TILE, D_MODEL), lambda i: (i, 0)),
            pl.BlockSpec((D_MODEL, D_MODEL), lambda i: (0, 0)),
            pl.BlockSpec((ROW_TILE, D_MODEL), lambda i: (i, 0)),
            _mod_spec(layer, 2),
            pl.BlockSpec((1, 1, D_MODEL), lambda i: (layer * 2 + 1, 0, 0)),
            _mod_spec(layer, 4),
            _mod_spec(layer, 3),
            pl.BlockSpec((N_EXPERTS, D_MODEL), lambda i: (0, 0)),
        ],
        out_specs=[
            pl.BlockSpec((ROW_TILE, D_MODEL), lambda i: (i, 0)),
            pl.BlockSpec((ROW_TILE * TOKEN_TILE_ROWS, LANES), lambda i: (i, 0)),
            pl.BlockSpec((N_EXPERTS, ROW_TILE), lambda i: (0, i)),
        ],
        out_shape=[
            jax.ShapeDtypeStruct((rows, D_MODEL), F32),
            jax.ShapeDtypeStruct((rows * TOKEN_TILE_ROWS, LANES), F32),
            jax.ShapeDtypeStruct((N_EXPERTS, rows), F32),
        ],
        compiler_params=_params(("arbitrary",), 48),
        name="post_attn",
    )(o, w_o, x, mod, norm_g3, mod, mod, w_router_t)


ROUTE_SEQS = BATCH
ROUTE_ROWS = ROUTE_SEQS * N_EXPERTS
ROUTE_BITS = SEQ.bit_length() - 1
EXPERT_BITS = N_EXPERTS.bit_length() - 1
PAIRS = N_EXPERTS * CAP
ROUTE_VALID = 1 << (2 * ROUTE_BITS)


def _prefix_count(x):
    rows = x.shape[0]
    nchunk = SEQ // PREFIX_CHUNK
    r = lax.broadcasted_iota(jnp.int32, (PREFIX_CHUNK, PREFIX_CHUNK), 0)
    c = lax.broadcasted_iota(jnp.int32, (PREFIX_CHUNK, PREFIX_CHUNK), 1)
    upper = jnp.where(r < c, 1.0, 0.0).astype(BF16)
    chunks = [x[:, k * PREFIX_CHUNK:(k + 1) * PREFIX_CHUNK] for k in range(nchunk)]
    local = jnp.dot(jnp.concatenate(chunks, axis=0).astype(BF16), upper, preferred_element_type=F32)
    out = []
    offset = jnp.zeros((rows, 1), F32)
    for k in range(nchunk):
        out.append(local[k * rows:(k + 1) * rows] + offset)
        offset = offset + jnp.sum(chunks[k], axis=1, keepdims=True)
    return jnp.concatenate(out, axis=1)


def _route_kernel(aff_ref, tok_ref, gate_ref, dest_ref, begin_ref, count_ref):
    aff = jnp.concatenate([aff_ref[:, b * SEQ:(b + 1) * SEQ] for b in range(ROUTE_SEQS)], axis=0)
    bits = pltpu.bitcast(aff, jnp.int32)

    def count(mask):
        return jnp.sum(jnp.where(mask, 1.0, 0.0), axis=1, keepdims=True)

    def search(_, bounds):
        lo, hi = bounds
        mid = lo + ((hi - lo) >> 1)
        ge = count(bits >= mid) >= CAP
        return jnp.where(ge, mid, lo), jnp.where(ge, hi, mid)

    lo0 = jnp.zeros((ROUTE_ROWS, 1), jnp.int32)
    hi0 = jnp.full((ROUTE_ROWS, 1), 0x7F800000, jnp.int32)
    tau, _ = lax.fori_loop(0, 31, search, (lo0, hi0))
    gt = bits > tau
    eq = jnp.where(bits == tau, 1.0, 0.0)
    need = CAP - count(gt)
    sel = jnp.where(gt, 1.0, jnp.where(_prefix_count(eq) < need, eq, 0.0))
    pos = _prefix_count(sel).astype(jnp.int32)
    chosen = sel > 0.5
    r_id = lax.broadcasted_iota(jnp.int32, (ROUTE_ROWS, ROUTE_ROWS), 0)
    c_id = lax.broadcasted_iota(jnp.int32, (ROUTE_ROWS, ROUTE_ROWS), 1)
    same_seq = (r_id >> EXPERT_BITS) == (c_id >> EXPERT_BITS)
    sel16 = sel.astype(BF16)
    earlier_experts = jnp.dot(jnp.where(same_seq, jnp.where(c_id < r_id, 1.0, 0.0), 0.0).astype(BF16), sel16,
                              preferred_element_type=F32)
    pairs = jnp.dot(jnp.where(same_seq, 1.0, 0.0).astype(BF16), sel16, preferred_element_type=F32)
    pairs_before = _prefix_count(pairs)
    dest = (pairs_before + earlier_experts).astype(jnp.int32)
    first_rows = [slice(b * N_EXPERTS, b * N_EXPERTS + 1) for b in range(ROUTE_SEQS)]
    begin_ref[...] = jnp.concatenate([pairs_before[r] for r in first_rows], axis=0).astype(jnp.int32)
    count_ref[...] = jnp.concatenate([pairs[r] for r in first_rows], axis=0).astype(jnp.int32)
    lane = lax.broadcasted_iota(jnp.int32, (ROUTE_ROWS, SEQ), 1)
    packed = jnp.where(chosen, ROUTE_VALID | (lane << ROUTE_BITS) | (lane - pos), 0)
    gate = aff
    for k in range(ROUTE_BITS):
        step = 1 << k
        from_right = pltpu.roll(packed, SEQ - step, 1)
        gate_right = pltpu.roll(gate, SEQ - step, 1)
        dest_right = pltpu.roll(dest, SEQ - step, 1)
        arrives = (from_right & step) != 0
        stays = (packed & step) == 0
        packed = jnp.where(arrives, from_right, jnp.where(stays, packed, 0))
        gate = jnp.where(arrives, gate_right, gate)
        dest = jnp.where(arrives, dest_right, dest)
    tok_ref[...] = (packed[:, :CAP] >> ROUTE_BITS) & (SEQ - 1)
    gate_ref[...] = gate[:, :CAP]
    dest_ref[...] = dest[:, :CAP]


def _route(aff_t):
    slots = pl.BlockSpec((ROUTE_ROWS, CAP), lambda i: (i, 0))
    tokens = pl.BlockSpec((ROUTE_SEQS, SEQ), lambda i: (i, 0))
    return pl.pallas_call(
        _route_kernel,
        grid=(BATCH // ROUTE_SEQS,),
        in_specs=[pl.BlockSpec((N_EXPERTS, ROUTE_SEQS * SEQ), lambda i: (0, i))],
        out_specs=[slots, slots, slots, tokens, tokens],
        out_shape=[jax.ShapeDtypeStruct((BATCH * N_EXPERTS, CAP), jnp.int32),
                   jax.ShapeDtypeStruct((BATCH * N_EXPERTS, CAP), F32),
                   jax.ShapeDtypeStruct((BATCH * N_EXPERTS, CAP), jnp.int32),
                   jax.ShapeDtypeStruct((BATCH, SEQ), jnp.int32),
                   jax.ShapeDtypeStruct((BATCH, SEQ), jnp.int32)],
        compiler_params=_params(("arbitrary",), 48),
        name="route",
    )(aff_t)


GATHER_ROWS = BATCH * CAP
FF_STEPS = EXPERT_FF // FF_TILE
ROWS_PER_CHUNK = GATHER_ROWS // (FF_STEPS * BATCH)
ISSUE_UNROLL = 8
TOK_BITS = (BATCH * SEQ).bit_length() - 1
ZCHUNK = 256


def _expert_kernel(route_ref, h_ref, gates_ref, wg_ref, wu_ref, wd_ref, z_ref,
                   x_ref, sem, xb_ref, acc_ref, y_ref, out_sem):
    e = pl.program_id(0)
    f = pl.program_id(1)
    slot = lax.rem(e, 2)
    next_slot = 1 - slot
    next_e = jnp.minimum(e + 1, N_EXPERTS - 1)
    prev_e = jnp.maximum(e - 1, 0)
    tile = TOKEN_TILE_ROWS

    def row_copy(expert, row, dst_slot):
        src = pl.multiple_of((route_ref[expert * GATHER_ROWS + row] & ((1 << TOK_BITS) - 1)) * tile, tile)
        dst = pl.multiple_of(row * tile, tile)
        return pltpu.make_async_copy(h_ref.at[pl.ds(src, tile), :], x_ref.at[dst_slot, pl.ds(dst, tile), :],
                                     sem.at[dst_slot])

    def wait_rows(dst_slot):
        pltpu.make_async_copy(h_ref.at[pl.ds(0, GATHER_ROWS * tile), :], x_ref.at[dst_slot],
                              sem.at[dst_slot]).wait()

    def out_copy(expert, row):
        dst = pl.multiple_of((route_ref[expert * GATHER_ROWS + row] >> TOK_BITS) * tile, tile)
        src = pl.multiple_of(row * tile, tile)
        return pltpu.make_async_copy(y_ref.at[pl.ds(src, tile), :], z_ref.at[pl.ds(dst, tile), :], out_sem.at[0])

    def wait_out():
        pltpu.make_async_copy(y_ref, z_ref.at[pl.ds(0, GATHER_ROWS * tile), :], out_sem.at[0]).wait()

    @pl.when(jnp.logical_and(e == 0, f == 0))
    def _():
        def issue(i, carry):
            for u in range(ISSUE_UNROLL):
                row_copy(0, i * ISSUE_UNROLL + u, 0).start()
            return carry
        lax.fori_loop(0, GATHER_ROWS // ISSUE_UNROLL, issue, 0)
        y_ref[...] = jnp.zeros_like(y_ref)

    @pl.when(f == 0)
    def _():
        wait_rows(slot)
        acc_ref[...] = jnp.zeros_like(acc_ref)
        for b in range(BATCH):
            cols = [x_ref[slot, pl.ds(b * CAP * tile + j, CAP, stride=tile), :] for j in range(tile)]
            xb_ref[b] = jnp.concatenate(cols, axis=1).astype(BF16)

    wg = wg_ref[0, 0].astype(BF16)
    wu = wu_ref[0, 0].astype(BF16)
    wd = wd_ref[0, 0].astype(BF16)
    for b in range(BATCH):
        first = (f * BATCH + b) * ROWS_PER_CHUNK
        for j in range(ROWS_PER_CHUNK):
            row_copy(next_e, first + j, next_slot).start()
            out_copy(prev_e, first + j).start()
        x = xb_ref[b]
        a = jnp.dot(x, wg, preferred_element_type=F32)
        u = jnp.dot(x, wu, preferred_element_type=F32)
        act = ((a * jax.nn.sigmoid(a)) * u).astype(BF16)
        acc_ref[b] += jnp.dot(act, wd, preferred_element_type=F32)

    @pl.when(f == FF_STEPS - 1)
    def _():
        wait_out()
        gates = gates_ref[0]
        for b in range(BATCH):
            y = acc_ref[b] * gates[:, b:b + 1]
            for j in range(tile):
                y_ref[pl.ds(b * CAP * tile + j, CAP, stride=tile), :] = y[:, j * LANES:(j + 1) * LANES]

    @pl.when(jnp.logical_and(e == N_EXPERTS - 1, f == FF_STEPS - 1))
    def _():
        wait_rows(next_slot)

        def issue(i, carry):
            for u in range(ISSUE_UNROLL):
                out_copy(N_EXPERTS - 1, i * ISSUE_UNROLL + u).start()
            return carry
        lax.fori_loop(0, GATHER_ROWS // ISSUE_UNROLL, issue, 0)
        wait_out()


def _experts(route, h, gates, w_gate, w_up, w_down, layer):
    grid_spec = pltpu.PrefetchScalarGridSpec(
        num_scalar_prefetch=1,
        grid=(N_EXPERTS, FF_STEPS),
        in_specs=[
            pl.BlockSpec(memory_space=pl.ANY),
            pl.BlockSpec((1, CAP, BATCH), lambda e, f, route: (e, 0, 0)),
            pl.BlockSpec((1, 1, D_MODEL, FF_TILE), lambda e, f, route: (layer, e, 0, f)),
            pl.BlockSpec((1, 1, D_MODEL, FF_TILE), lambda e, f, route: (layer, e, 0, f)),
            pl.BlockSpec((1, 1, FF_TILE, D_MODEL), lambda e, f, route: (layer, e, f, 0)),
        ],
        out_specs=pl.BlockSpec(memory_space=pl.ANY),
        scratch_shapes=[pltpu.VMEM((2, GATHER_ROWS * TOKEN_TILE_ROWS, LANES), F32), pltpu.SemaphoreType.DMA((2,)),
                        pltpu.VMEM((BATCH, CAP, D_MODEL), BF16), pltpu.VMEM((BATCH, CAP, D_MODEL), F32),
                        pltpu.VMEM((GATHER_ROWS * TOKEN_TILE_ROWS, LANES), F32), pltpu.SemaphoreType.DMA((1,))],
    )
    return pl.pallas_call(
        _expert_kernel,
        grid_spec=grid_spec,
        out_shape=jax.ShapeDtypeStruct((BATCH * PAIRS * TOKEN_TILE_ROWS, LANES), F32),
        compiler_params=_params(("arbitrary", "arbitrary"), 60),
        name="moe_experts",
    )(route, h, gates, w_gate, w_up, w_down)


def _combine_kernel(range_ref, span_ref, z_ref, x_ref, gate_ref, g_ref, o_ref, zbuf, zsem, acc_ref, *, final):
    b = pl.program_id(0)
    word = (b * (SEQ // COMBINE_ROWS) + pl.program_id(1)) * 2
    first = range_ref[word]
    n = range_ref[word + 1] - first
    tile = TOKEN_TILE_ROWS

    def chunk_copy(c, slot):
        src = pl.multiple_of((b * (PAIRS // ZCHUNK) + c) * (ZCHUNK * tile), ZCHUNK * tile)
        return pltpu.make_async_copy(z_ref.at[pl.ds(src, ZCHUNK * tile), :], zbuf.at[slot], zsem.at[slot])

    @pl.when(n > 0)
    def _():
        chunk_copy(first, 0).start()

    acc_ref[...] = jnp.zeros_like(acc_ref)
    span = span_ref[0]
    lo, hi = span[:, 0:1], span[:, 1:2]

    def chunk_step(k, carry):
        slot = lax.rem(k, 2)
        chunk_copy(first + k, slot).wait()

        @pl.when(k + 1 < n)
        def _():
            chunk_copy(first + k + 1, 1 - slot).start()

        rows = jnp.concatenate([zbuf[slot, pl.ds(j, ZCHUNK, stride=tile), :] for j in range(tile)],
                               axis=1).astype(BF16)
        pair = (first + k) * ZCHUNK + lax.broadcasted_iota(jnp.int32, (COMBINE_ROWS, ZCHUNK), 1)
        onehot = jnp.where(pair >= lo, jnp.where(pair < hi, 1.0, 0.0), 0.0).astype(BF16)
        acc_ref[...] += jnp.dot(onehot, rows, preferred_element_type=F32)
        return carry

    lax.fori_loop(0, n, chunk_step, 0)
    x = x_ref[0] + gate_ref[0] * acc_ref[...]
    if final:
        x = (x * lax.rsqrt(jnp.mean(x * x, axis=-1, keepdims=True) + RMS_EPS)) * g_ref[...]
    o_ref[0] = x


def _combine(ranges, span, z, x, mod, final_g, layer, final):
    grid_spec = pltpu.PrefetchScalarGridSpec(
        num_scalar_prefetch=1,
        grid=(BATCH, SEQ // COMBINE_ROWS),
        in_specs=[
            pl.BlockSpec((1, COMBINE_ROWS, 2), lambda b, t, r: (b, t, 0)),
            pl.BlockSpec(memory_space=pl.ANY),
            pl.BlockSpec((1, COMBINE_ROWS, D_MODEL), lambda b, t, r: (b, t, 0)),
            pl.BlockSpec((1, 1, D_MODEL), lambda b, t, r: ((layer * 6 + 5) * BATCH + b, 0, 0)),
            pl.BlockSpec((1, D_MODEL), lambda b, t, r: (0, 0)),
        ],
        out_specs=pl.BlockSpec((1, COMBINE_ROWS, D_MODEL), lambda b, t, r: (b, t, 0)),
        scratch_shapes=[pltpu.VMEM((2, ZCHUNK * TOKEN_TILE_ROWS, LANES), F32), pltpu.SemaphoreType.DMA((2,)),
                        pltpu.VMEM((COMBINE_ROWS, D_MODEL), F32)],
    )
    return pl.pallas_call(
        functools.partial(_combine_kernel, final=final),
        grid_spec=grid_spec,
        out_shape=jax.ShapeDtypeStruct((BATCH, SEQ, D_MODEL), F32),
        compiler_params=_params(("arbitrary", "arbitrary"), 32),
        name="moe_combine",
    )(ranges, span, z, x, mod, final_g.reshape(1, D_MODEL))


def _na_qkv_weight(w):
    nq = NA_HEADS * HEAD_DIM
    scale = jnp.concatenate([jnp.full((nq,), Q_SCALE, F32), jnp.ones((2 * nq,), F32)])
    return (w * scale).astype(BF16)


def _sw_qkv_weight(w):
    nq = SW_Q_HEADS * HEAD_DIM
    nkv = SW_KV_HEADS * HEAD_DIM
    dup = lambda t: jnp.concatenate([t.reshape(D_MODEL, SW_KV_HEADS, 1, HEAD_DIM)] * 2, axis=2).reshape(
        D_MODEL, 2 * nkv)
    wq = w[:, :nq] * Q_SCALE
    return jnp.concatenate([wq, dup(w[:, nq:nq + nkv]), dup(w[:, nq + nkv:])], axis=1).astype(BF16)


def kernel(x, c, ada_w, ada_b, norm_g, na_w_qkv, na_w_o, na_rpb, sw_w_qkv, sw_w_o, sw_sinks, t5_bias,
           moe_w_router, moe_w_gate, moe_w_up, moe_w_down, final_g):
    mod = _ada(c, ada_w, ada_b)
    norm_g3 = norm_g.reshape(DEPTH * 2, 1, D_MODEL)
    x = x.reshape(BATCH * SEQ, D_MODEL)
    for layer in range(DEPTH):
        j = layer // N_MIXERS
        if layer % N_MIXERS == 0:
            qkv = _qkv(x, norm_g3, mod, _na_qkv_weight(na_w_qkv[j]), layer)
            o = _na_attention(qkv.reshape(BATCH, SEQ, -1), _na_rpb_rows(na_rpb[j]))
            w_o = na_w_o[j]
        else:
            qkv = _qkv(x, norm_g3, mod, _sw_qkv_weight(sw_w_qkv[j]), layer)
            o = _sw_attention(qkv.reshape(BATCH, SEQ, -1), _sw_rel_table(t5_bias), sw_sinks[j])
            w_o = sw_w_o[j]
        w_router = moe_w_router[layer].astype(BF16)
        x, h, aff_t = _post_attn(o.reshape(BATCH * SEQ, -1), w_o.astype(BF16), x, norm_g3, mod,
                                 w_router.T, layer)
        tok, gate, dest, begin, count = _route(aff_t)
        seq = jnp.arange(BATCH, dtype=jnp.int32)[:, None, None]
        words = (tok.reshape(BATCH, N_EXPERTS, CAP) + seq * SEQ
                 + ((dest.reshape(BATCH, N_EXPERTS, CAP) + seq * PAIRS) << TOK_BITS))
        gates = gate.reshape(BATCH, N_EXPERTS, CAP).transpose(1, 2, 0)
        z = _experts(words.transpose(1, 0, 2).reshape(-1), h, gates, moe_w_gate, moe_w_up, moe_w_down, layer)
        end = begin + count
        ranges = jnp.stack([begin[:, ::COMBINE_ROWS] // ZCHUNK,
                            (end[:, COMBINE_ROWS - 1::COMBINE_ROWS] + ZCHUNK - 1) // ZCHUNK], axis=-1)
        x = _combine(ranges.reshape(-1), jnp.stack([begin, end], axis=-1), z, x.reshape(BATCH, SEQ, D_MODEL), mod,
                     final_g, layer, final=layer == DEPTH - 1).reshape(BATCH * SEQ, D_MODEL)
    return x.reshape(BATCH, SEQ, D_MODEL)
```

```python
import functools

import numpy as np
import jax
import jax.numpy as jnp
from jax import lax
from jax.experimental import pallas as pl
from jax.experimental.pallas import tpu as pltpu

D_MODEL = 1024
BATCH = 8
SEQ = 2048
DEPTH = 2
GRID_W = 64
ROWS = SEQ // GRID_W
N_MIXERS = 2
HEAD_DIM = 64
NA_HEADS = 16
NA_WIN_H = 8
NA_WIN_W = 16
SW_Q_HEADS = 16
SW_KV_HEADS = 4
SW_GROUP = SW_Q_HEADS // SW_KV_HEADS
SW_WINDOW = 128
SW_BLOCK = 128
SW_NB = SEQ // SW_BLOCK
SW_SPAN = 3 * SW_BLOCK
T5_BUCKETS = 32
T5_MAX_DIST = 128
N_EXPERTS = 16
EXPERT_FF = 2048
EC_CAPACITY = 2
CAP = EC_CAPACITY * SEQ // N_EXPERTS
RMS_EPS = 1e-6
NEG = -1e30
LOG2E = 1.4426950408889634
Q_SCALE = HEAD_DIM ** -0.5 * LOG2E

LANES = 128
TOKEN_TILE_ROWS = D_MODEL // LANES
MIB = 1024 * 1024
F32 = jnp.float32
BF16 = jnp.bfloat16

ROW_TILE = 512
FF_TILE = 512
COMBINE_ROWS = 512
PREFIX_CHUNK = 256


def _params(semantics, vmem_mib, flags=None):
    return pltpu.CompilerParams(dimension_semantics=semantics, vmem_limit_bytes=vmem_mib * MIB, flags=flags)


def _norm_mod(x, g, sc, sh):
    y = x * lax.rsqrt(jnp.mean(x * x, axis=-1, keepdims=True) + RMS_EPS)
    return (y * g) * (1.0 + sc) + sh


def _softmax0(z):
    z = z - jnp.max(z, axis=0, keepdims=True)
    p = jnp.exp(z)
    return p / jnp.sum(p, axis=0, keepdims=True)


def _half_masks(rows):
    lane = lax.broadcasted_iota(jnp.int32, (rows, LANES), 1)
    lo = jnp.where(lane < HEAD_DIM, 1.0, 0.0).astype(BF16)
    hi = jnp.where(lane < HEAD_DIM, 0.0, 1.0).astype(BF16)
    return lo, hi


def _ada_kernel(c_ref, w_ref, b_ref, o_ref):
    c = c_ref[...]
    act = (c * jax.nn.sigmoid(c)).astype(BF16)
    o_ref[0] = jnp.dot(act, w_ref[0].astype(BF16), preferred_element_type=F32) + b_ref[0]


def _ada(c, ada_w, ada_b):
    out = pl.pallas_call(
        _ada_kernel,
        grid=(DEPTH, 6),
        in_specs=[
            pl.BlockSpec((BATCH, D_MODEL), lambda l, k: (0, 0)),
            pl.BlockSpec((1, D_MODEL, D_MODEL), lambda l, k: (l, 0, k)),
            pl.BlockSpec((1, 1, D_MODEL), lambda l, k: (l * 6 + k, 0, 0)),
        ],
        out_specs=pl.BlockSpec((1, BATCH, D_MODEL), lambda l, k: (l * 6 + k, 0, 0)),
        out_shape=jax.ShapeDtypeStruct((DEPTH * 6, BATCH, D_MODEL), F32),
        compiler_params=_params(("arbitrary", "arbitrary"), 32),
        name="ada_mod",
    )(c, ada_w, ada_b.reshape(DEPTH * 6, 1, D_MODEL))
    return out.reshape(DEPTH * 6 * BATCH, 1, D_MODEL)


def _mod_spec(layer, chunk):
    tiles_per_seq = SEQ // ROW_TILE
    return pl.BlockSpec((1, 1, D_MODEL),
                        lambda i: ((layer * 6 + chunk) * BATCH + i // tiles_per_seq, 0, 0))


def _qkv_kernel(x_ref, g_ref, sc_ref, sh_ref, w_ref, o_ref, *, head_copies):
    h = _norm_mod(x_ref[...], g_ref[0], sc_ref[0], sh_ref[0])
    y = jnp.dot(h.astype(BF16), w_ref[...], preferred_element_type=F32)
    if head_copies:
        low_half = lax.broadcasted_iota(jnp.int32, (ROW_TILE, LANES), 1) < HEAD_DIM
        keep = y.shape[1] - head_copies * LANES
        tiles = [y[:, :keep]]
        for t in range(head_copies):
            pair = y[:, keep + t * LANES:keep + (t + 1) * LANES]
            swapped = pltpu.roll(pair, HEAD_DIM, 1)
            tiles += [jnp.where(low_half, pair, swapped), jnp.where(low_half, swapped, pair)]
        y = jnp.concatenate(tiles, axis=1)
    o_ref[...] = y.astype(BF16)


def _qkv(x, norm_g3, mod, w, layer, head_copies=0):
    n = w.shape[1] + head_copies * LANES
    return pl.pallas_call(
        functools.partial(_qkv_kernel, head_copies=head_copies),
        grid=(BATCH * SEQ // ROW_TILE,),
        in_specs=[
            pl.BlockSpec((ROW_TILE, D_MODEL), lambda i: (i, 0)),
            pl.BlockSpec((1, 1, D_MODEL), lambda i: (layer * 2, 0, 0)),
            _mod_spec(layer, 1),
            _mod_spec(layer, 0),
            pl.BlockSpec(w.shape, lambda i: (0, 0)),
        ],
        out_specs=pl.BlockSpec((ROW_TILE, n), lambda i: (i, 0)),
        out_shape=jax.ShapeDtypeStruct((BATCH * SEQ, n), BF16),
        compiler_params=_params(("arbitrary",), 48),
        name="norm_qkv",
    )(x, norm_g3, mod, mod, w)


NA_BIAS_ROWS = 2 * NA_WIN_H - 1
NA_ROW_UNROLL = 4


def _na_rpb_rows(rpb):
    w = NA_WIN_W - 1
    rpb = rpb.astype(F32) * LOG2E
    pad = jnp.zeros(rpb.shape[:2] + (LANES - 2 * w - 1,), F32)
    return jnp.concatenate([rpb[..., w:], pad, rpb[..., :w]], axis=-1)


def _na_build_bias(w_ref, bias_ref):
    lane = lax.broadcasted_iota(jnp.int32, (GRID_W, LANES), 1)
    col = lax.broadcasted_iota(jnp.int32, (GRID_W, LANES), 0)
    kc = jnp.bitwise_and(lane, GRID_W - 1)
    cstart = jnp.clip(col - NA_WIN_W // 2, 0, GRID_W - NA_WIN_W)
    in_window = jnp.where(kc >= cstart, jnp.where(kc < cstart + NA_WIN_W, 1.0, 0.0), 0.0) > 0.5
    low_half = lane < GRID_W

    def toeplitz(head, a, shift):
        row = jnp.broadcast_to(w_ref[head, a:a + 1, :], (GRID_W, LANES))
        return pltpu.roll(row, shift, 1, stride=1, stride_axis=0)

    for head in range(2):
        tiles = [jnp.where(in_window, jnp.where(low_half, toeplitz(head, a, 0), toeplitz(head, a + 1, GRID_W)), NEG)
                 for a in range(NA_BIAS_ROWS - 1)]
        for d in range(NA_WIN_H):
            for m in range(NA_WIN_H // 2):
                bias_ref[d, head * GRID_W:(head + 1) * GRID_W, m * LANES:(m + 1) * LANES] = (
                    tiles[NA_WIN_H - 1 - d + 2 * m])


def _na_kernel(w_ref, q_ref, k_ref, v_ref, o_ref, bias_ref, s0_ref, s1_ref, p0_ref, p1_ref):
    @pl.when(pl.program_id(1) == 0)
    def _():
        _na_build_bias(w_ref, bias_ref)

    lo, hi = _half_masks(GRID_W)
    lo_f = lax.broadcasted_iota(jnp.int32, (GRID_W, LANES), 1) < HEAD_DIM
    span = NA_WIN_H * GRID_W
    ones = jnp.ones((span, LANES), BF16)
    groups = ROWS // NA_ROW_UNROLL

    def group_rows(i):
        for u in range(NA_ROW_UNROLL):
            r = jnp.clip(i, 0, groups - 1) * NA_ROW_UNROLL + u
            yield u, r, jnp.clip(r - NA_WIN_H // 2, 0, ROWS - NA_WIN_H)

    def scores(i, s_ref):
        for u, r, rs in group_rows(i):
            q = q_ref[0, pl.ds(pl.multiple_of(r * GRID_W, GRID_W), GRID_W), :]
            lhs = jnp.concatenate([q * lo, q * hi], axis=0)
            kw = k_ref[0, pl.ds(pl.multiple_of(rs * GRID_W, GRID_W), span), :]
            s_ref[u] = lax.dot_general(lhs, kw, (((1,), (1,)), ((), ())), preferred_element_type=F32)

    def softmax(i, s_ref, p_ref):
        for u, r, rs in group_rows(i):
            s = s_ref[u] + bias_ref[r - rs]
            m = jnp.broadcast_to(jnp.max(s, axis=-1, keepdims=True), (2 * GRID_W, LANES))
            p_ref[u] = jnp.exp2(s - jnp.concatenate([m] * (span // LANES), axis=1)).astype(BF16)

    def values(i, p_ref):
        for u, r, rs in group_rows(i):
            vw = v_ref[0, pl.ds(pl.multiple_of(rs * GRID_W, GRID_W), span), :]
            o = jnp.dot(p_ref[u], jnp.concatenate([vw, ones], axis=1), preferred_element_type=F32)
            o = o[:, :LANES] / o[:, LANES:]
            out = jnp.where(lo_f, o[:GRID_W], o[GRID_W:])
            o_ref[0, pl.ds(pl.multiple_of(r * GRID_W, GRID_W), GRID_W), :] = out.astype(BF16)

    def step(i, carry):
        values(2 * i - 1, p1_ref)
        scores(2 * i + 1, s1_ref)
        softmax(2 * i, s0_ref, p0_ref)
        values(2 * i, p0_ref)
        scores(2 * i + 2, s0_ref)
        softmax(2 * i + 1, s1_ref, p1_ref)
        return carry

    p1_ref[...] = jnp.ones_like(p1_ref)
    scores(0, s0_ref)
    lax.fori_loop(0, groups // 2, step, 0)
    values(groups - 1, p1_ref)


def _na_attention(qkv, rpb_rows):
    pairs = NA_HEADS // 2
    return pl.pallas_call(
        _na_kernel,
        grid=(pairs, BATCH),
        in_specs=[
            pl.BlockSpec((2, NA_BIAS_ROWS, LANES), lambda p, b: (p, 0, 0)),
            pl.BlockSpec((1, SEQ, LANES), lambda p, b: (b, 0, p)),
            pl.BlockSpec((1, SEQ, LANES), lambda p, b: (b, 0, pairs + p)),
            pl.BlockSpec((1, SEQ, LANES), lambda p, b: (b, 0, 2 * pairs + p)),
        ],
        out_specs=pl.BlockSpec((1, SEQ, LANES), lambda p, b: (b, 0, p)),
        out_shape=jax.ShapeDtypeStruct((BATCH, SEQ, NA_HEADS * HEAD_DIM), BF16),
        scratch_shapes=[pltpu.VMEM((NA_WIN_H, 2 * GRID_W, NA_WIN_H * GRID_W), F32),
                        pltpu.VMEM((NA_ROW_UNROLL, 2 * GRID_W, NA_WIN_H * GRID_W), F32),
                        pltpu.VMEM((NA_ROW_UNROLL, 2 * GRID_W, NA_WIN_H * GRID_W), F32),
                        pltpu.VMEM((NA_ROW_UNROLL, 2 * GRID_W, NA_WIN_H * GRID_W), BF16),
                        pltpu.VMEM((NA_ROW_UNROLL, 2 * GRID_W, NA_WIN_H * GRID_W), BF16)],
        compiler_params=_params(("arbitrary", "arbitrary"), 32),
        name="na_attention",
    )(rpb_rows, qkv, qkv, qkv)


def _t5_buckets(rel):
    half = T5_BUCKETS // 2
    max_exact = half // 2
    n = np.abs(rel)
    large = max_exact + (np.log(np.maximum(n, 1) / max_exact)
                         / np.log(T5_MAX_DIST / max_exact) * (half - max_exact)).astype(np.int32)
    large = np.minimum(large, half - 1)
    return (rel > 0).astype(np.int32) * half + np.where(n < max_exact, n, large)


SW_REL_PERIOD = 512
SW_BLOCK_UNROLL = 2


def _sw_rel_table(t5_table):
    k = np.arange(SW_REL_PERIOD)
    rel = np.where(k < SW_REL_PERIOD // 2, k, k - SW_REL_PERIOD)
    ok = np.abs(rel) <= SW_WINDOW
    vals = jnp.where(ok[:, None], t5_table[_t5_buckets(rel)].astype(F32) * LOG2E, NEG)
    return vals.T.reshape(SW_KV_HEADS, SW_GROUP, SW_REL_PERIOD)


def _sw_build_bias(rel_ref, bias_ref):
    for g in range(SW_GROUP):
        row = jnp.broadcast_to(rel_ref[0, g:g + 1, :], (SW_BLOCK, SW_REL_PERIOD))
        t = pltpu.roll(row, 0, 1, stride=1, stride_axis=0)
        rows = slice(g * SW_BLOCK, (g + 1) * SW_BLOCK)
        bias_ref[0, rows, :] = t[:, :SW_SPAN]
        bias_ref[1, rows, :] = jnp.concatenate([t[:, SW_SPAN:], t[:, :2 * SW_BLOCK]], axis=1)
        bias_ref[2, rows, :] = jnp.concatenate([t[:, 2 * SW_BLOCK:], t[:, :SW_BLOCK]], axis=1)


def _sw_kernel(sink_ref, rel_ref, q_ref, k_ref, v_ref, o_ref, bias_ref, s0_ref, s1_ref, p0_ref, p1_ref,
               t0_ref, t1_ref):
    j = pl.program_id(0)
    ones = jnp.ones((SW_SPAN, LANES), BF16)

    @pl.when(pl.program_id(1) == 0)
    def _():
        _sw_build_bias(rel_ref, bias_ref)

    masks = _half_masks(SW_BLOCK)
    lo_f = lax.broadcasted_iota(jnp.int32, (SW_BLOCK, LANES), 1) < HEAD_DIM
    sinks = [jnp.full((SW_BLOCK, LANES), sink_ref[j * SW_GROUP + g] * LOG2E, F32) for g in range(SW_GROUP)]
    groups = SW_NB // SW_BLOCK_UNROLL

    def group_blocks(i):
        for u in range(SW_BLOCK_UNROLL):
            n = jnp.clip(i, 0, groups - 1) * SW_BLOCK_UNROLL + u
            yield u, n, jnp.clip(n - 1, 0, SW_NB - 3)

    def block_rows(n):
        return pl.ds(pl.multiple_of(n * SW_BLOCK, SW_BLOCK), SW_BLOCK)

    def scores(i, s_ref):
        for u, n, first in group_blocks(i):
            kw = k_ref[0, pl.ds(pl.multiple_of(first * SW_BLOCK, SW_BLOCK), SW_SPAN), :]
            for g in range(SW_GROUP):
                q = q_ref[0, block_rows(n), (g // 2) * LANES:(g // 2 + 1) * LANES] * masks[g % 2]
                s_ref[u * SW_GROUP + g] = lax.dot_general(q, kw, (((1,), (1,)), ((), ())),
                                                          preferred_element_type=F32)

    def softmax(i, s_ref, p_ref, t_ref):
        for u, n, first in group_blocks(i):
            for g in range(SW_GROUP):
                c = u * SW_GROUP + g
                s = s_ref[c] + bias_ref[n - first, g * SW_BLOCK:(g + 1) * SW_BLOCK, :]
                m = jnp.maximum(jnp.broadcast_to(jnp.max(s, axis=-1, keepdims=True), (SW_BLOCK, LANES)), sinks[g])
                p_ref[c] = jnp.exp2(s - jnp.concatenate([m] * (SW_SPAN // LANES), axis=1)).astype(BF16)
                t_ref[c] = jnp.exp2(sinks[g] - m)

    def values(i, p_ref, t_ref):
        for u, n, first in group_blocks(i):
            vw = v_ref[0, pl.ds(pl.multiple_of(first * SW_BLOCK, SW_BLOCK), SW_SPAN), :]
            v_ones = jnp.concatenate([vw, ones], axis=1)
            heads = []
            for g in range(SW_GROUP):
                c = u * SW_GROUP + g
                o = jnp.dot(p_ref[c], v_ones, preferred_element_type=F32)
                heads.append(o[:, :LANES] / (o[:, LANES:] + t_ref[c]))
            out = jnp.concatenate([jnp.where(lo_f, heads[0], heads[1]), jnp.where(lo_f, heads[2], heads[3])],
                                  axis=1)
            o_ref[0, block_rows(n), :] = out.astype(BF16)

    def step(i, carry):
        values(2 * i - 1, p1_ref, t1_ref)
        scores(2 * i + 1, s1_ref)
        softmax(2 * i, s0_ref, p0_ref, t0_ref)
        values(2 * i, p0_ref, t0_ref)
        scores(2 * i + 2, s0_ref)
        softmax(2 * i + 1, s1_ref, p1_ref, t1_ref)
        return carry

    p1_ref[...] = jnp.ones_like(p1_ref)
    t1_ref[...] = jnp.ones_like(t1_ref)
    scores(0, s0_ref)
    lax.fori_loop(0, groups // 2, step, 0)
    values(groups - 1, p1_ref, t1_ref)


def _sw_attention(qkv, rel_table, sinks):
    qw = SW_GROUP * HEAD_DIM
    k_off = SW_Q_HEADS * HEAD_DIM // LANES
    v_off = k_off + SW_KV_HEADS
    chains = SW_BLOCK_UNROLL * SW_GROUP
    return pl.pallas_call(
        _sw_kernel,
        grid=(SW_KV_HEADS, BATCH),
        in_specs=[
            pl.BlockSpec(memory_space=pltpu.SMEM),
            pl.BlockSpec((1, SW_GROUP, SW_REL_PERIOD), lambda j, b: (j, 0, 0)),
            pl.BlockSpec((1, SEQ, qw), lambda j, b: (b, 0, j)),
            pl.BlockSpec((1, SEQ, LANES), lambda j, b: (b, 0, k_off + j)),
            pl.BlockSpec((1, SEQ, LANES), lambda j, b: (b, 0, v_off + j)),
        ],
        out_specs=pl.BlockSpec((1, SEQ, qw), lambda j, b: (b, 0, j)),
        out_shape=jax.ShapeDtypeStruct((BATCH, SEQ, SW_Q_HEADS * HEAD_DIM), BF16),
        scratch_shapes=[pltpu.VMEM((3, SW_GROUP * SW_BLOCK, SW_SPAN), F32)]
        + [pltpu.VMEM((chains, SW_BLOCK, SW_SPAN), F32)] * 2
        + [pltpu.VMEM((chains, SW_BLOCK, SW_SPAN), BF16)] * 2
        + [pltpu.VMEM((chains, SW_BLOCK, LANES), F32)] * 2,
        compiler_params=_params(("arbitrary", "arbitrary"), 32),
        name="sw_attention",
    )(sinks, rel_table, qkv, qkv, qkv)


def _post_attn_kernel(o_ref, wo_ref, x_ref, gate_ref, g_ref, sc_ref, sh_ref, wr_ref,
                      xo_ref, h_ref, aff_ref):
    y = jnp.dot(o_ref[...], wo_ref[...], preferred_element_type=F32)
    xn = x_ref[...] + gate_ref[0] * y
    xo_ref[...] = xn
    h = _norm_mod(xn, g_ref[0], sc_ref[0], sh_ref[0])
    for j in range(TOKEN_TILE_ROWS):
        h_ref[pl.ds(j, ROW_TILE, stride=TOKEN_TILE_ROWS), :] = h[:, j * LANES:(j + 1) * LANES]
    hb = h.astype(BF16)
    logits = lax.dot_general(wr_ref[...], hb, (((1,), (1,)), ((), ())), preferred_element_type=F32)
    aff_ref[...] = _softmax0(logits)


def _post_attn(o, w_o, x, norm_g3, mod, w_router_t, layer):
    rows = BATCH * SEQ
    return pl.pallas_call(
        _post_attn_kernel,
        grid=(rows // ROW_TILE,),
        in_specs=[
            pl.BlockSpec((ROW_TILE, D_MODEL), lambda i: (i, 0)),
            pl.BlockSpec((D_MODEL, D_MODEL), lambda i: (0, 0)),
            pl.BlockSpec((ROW_TILE, D_MODEL), lambda i: (i, 0)),
            _mod_spec(layer, 2),
            pl.BlockSpec((1, 1, D_MODEL), lambda i: (layer * 2 + 1, 0, 0)),
            _mod_spec(layer, 4),
            _mod_spec(layer, 3),
            pl.BlockSpec((N_EXPERTS, D_MODEL), lambda i: (0, 0)),
        ],
        out_specs=[
            pl.BlockSpec((ROW_TILE, D_MODEL), lambda i: (i, 0)),
            pl.BlockSpec((ROW_TILE * TOKEN_TILE_ROWS, LANES), lambda i: (i, 0)),
            pl.BlockSpec((N_EXPERTS, ROW_TILE), lambda i: (0, i)),
        ],
        out_shape=[
            jax.ShapeDtypeStruct((rows, D_MODEL), F32),
            jax.ShapeDtypeStruct((rows * TOKEN_TILE_ROWS, LANES), F32),
            jax.ShapeDtypeStruct((N_EXPERTS, rows), F32),
        ],
        compiler_params=_params(("arbitrary",), 48),
        name="post_attn",
    )(o, w_o, x, mod, norm_g3, mod, mod, w_router_t)


ROUTE_SEQS = BATCH
ROUTE_ROWS = ROUTE_SEQS * N_EXPERTS
ROUTE_BITS = SEQ.bit_length() - 1
ROUTE_VALID = 1 << (2 * ROUTE_BITS)


def _prefix_count(x):
    rows = x.shape[0]
    nchunk = SEQ // PREFIX_CHUNK
    r = lax.broadcasted_iota(jnp.int32, (PREFIX_CHUNK, PREFIX_CHUNK), 0)
    c = lax.broadcasted_iota(jnp.int32, (PREFIX_CHUNK, PREFIX_CHUNK), 1)
    upper = jnp.where(r < c, 1.0, 0.0).astype(BF16)
    chunks = [x[:, k * PREFIX_CHUNK:(k + 1) * PREFIX_CHUNK] for k in range(nchunk)]
    local = jnp.dot(jnp.concatenate(chunks, axis=0).astype(BF16), upper, preferred_element_type=F32)
    out = []
    offset = jnp.zeros((rows, 1), F32)
    for k in range(nchunk):
        out.append(local[k * rows:(k + 1) * rows] + offset)
        offset = offset + jnp.sum(chunks[k], axis=1, keepdims=True)
    return jnp.concatenate(out, axis=1)


def _route_kernel(aff_ref, slot_ref, tok_ref, gate_ref):
    aff = jnp.concatenate([aff_ref[:, b * SEQ:(b + 1) * SEQ] for b in range(ROUTE_SEQS)], axis=0)
    bits = pltpu.bitcast(aff, jnp.int32)

    def count(mask):
        return jnp.sum(jnp.where(mask, 1.0, 0.0), axis=1, keepdims=True)

    def search(_, bounds):
        lo, hi = bounds
        mid = lo + ((hi - lo) >> 1)
        ge = count(bits >= mid) >= CAP
        return jnp.where(ge, mid, lo), jnp.where(ge, hi, mid)

    lo0 = jnp.zeros((ROUTE_ROWS, 1), jnp.int32)
    hi0 = jnp.full((ROUTE_ROWS, 1), 0x7F800000, jnp.int32)
    tau, _ = lax.fori_loop(0, 31, search, (lo0, hi0))
    gt = bits > tau
    eq = jnp.where(bits == tau, 1.0, 0.0)
    need = CAP - count(gt)
    sel = jnp.where(gt, 1.0, jnp.where(_prefix_count(eq) < need, eq, 0.0))
    pos = _prefix_count(sel).astype(jnp.int32)
    chosen = sel > 0.5
    slot_ref[...] = jnp.where(chosen, pos, -1)
    lane = lax.broadcasted_iota(jnp.int32, (ROUTE_ROWS, SEQ), 1)
    packed = jnp.where(chosen, ROUTE_VALID | (lane << ROUTE_BITS) | (lane - pos), 0)
    gate = aff
    for k in range(ROUTE_BITS):
        step = 1 << k
        from_right = pltpu.roll(packed, SEQ - step, 1)
        gate_right = pltpu.roll(gate, SEQ - step, 1)
        arrives = (from_right & step) != 0
        stays = (packed & step) == 0
        packed = jnp.where(arrives, from_right, jnp.where(stays, packed, 0))
        gate = jnp.where(arrives, gate_right, gate)
    tok_ref[...] = (packed[:, :CAP] >> ROUTE_BITS) & (SEQ - 1)
    gate_ref[...] = gate[:, :CAP]


def _route(aff_t):
    return pl.pallas_call(
        _route_kernel,
        grid=(BATCH // ROUTE_SEQS,),
        in_specs=[pl.BlockSpec((N_EXPERTS, ROUTE_SEQS * SEQ), lambda i: (0, i))],
        out_specs=[pl.BlockSpec((ROUTE_ROWS, SEQ), lambda i: (i, 0)),
                   pl.BlockSpec((ROUTE_ROWS, CAP), lambda i: (i, 0)),
                   pl.BlockSpec((ROUTE_ROWS, CAP), lambda i: (i, 0))],
        out_shape=[jax.ShapeDtypeStruct((BATCH * N_EXPERTS, SEQ), jnp.int32),
                   jax.ShapeDtypeStruct((BATCH * N_EXPERTS, CAP), jnp.int32),
                   jax.ShapeDtypeStruct((BATCH * N_EXPERTS, CAP), F32)],
        compiler_params=_params(("arbitrary",), 48),
        name="route",
    )(aff_t)


GATHER_ROWS = BATCH * CAP
FF_STEPS = EXPERT_FF // FF_TILE
ROWS_PER_CHUNK = GATHER_ROWS // (FF_STEPS * BATCH)
PROLOGUE_UNROLL = 8


def _expert_kernel(tok_ref, h_ref, gates_ref, wg_ref, wu_ref, wd_ref, y_ref, x_ref, sem, xb_ref, acc_ref):
    e = pl.program_id(0)
    f = pl.program_id(1)
    slot = lax.rem(e, 2)
    next_slot = 1 - slot
    next_e = jnp.minimum(e + 1, N_EXPERTS - 1)

    tile = TOKEN_TILE_ROWS

    def row_copy(expert, row, dst_slot):
        src = pl.multiple_of(tok_ref[expert * GATHER_ROWS + row] * tile, tile)
        dst = pl.multiple_of(row * tile, tile)
        return pltpu.make_async_copy(h_ref.at[pl.ds(src, tile), :], x_ref.at[dst_slot, pl.ds(dst, tile), :],
                                     sem.at[dst_slot])

    def wait_rows(dst_slot):
        pltpu.make_async_copy(h_ref.at[pl.ds(0, GATHER_ROWS * tile), :], x_ref.at[dst_slot],
                              sem.at[dst_slot]).wait()

    @pl.when(jnp.logical_and(e == 0, f == 0))
    def _():
        def issue(i, carry):
            for u in range(PROLOGUE_UNROLL):
                row_copy(0, i * PROLOGUE_UNROLL + u, 0).start()
            return carry
        lax.fori_loop(0, GATHER_ROWS // PROLOGUE_UNROLL, issue, 0)
        acc_ref[...] = jnp.zeros_like(acc_ref)

    @pl.when(f == 0)
    def _():
        wait_rows(slot)
        for b in range(BATCH):
            cols = [x_ref[slot, pl.ds(b * CAP * tile + j, CAP, stride=tile), :] for j in range(tile)]
            xb_ref[b] = jnp.concatenate(cols, axis=1).astype(BF16)

    wg = wg_ref[0, 0].astype(BF16)
    wu = wu_ref[0, 0].astype(BF16)
    wd = wd_ref[0, 0].astype(BF16)
    for b in range(BATCH):
        first = (f * BATCH + b) * ROWS_PER_CHUNK
        for j in range(ROWS_PER_CHUNK):
            row_copy(next_e, first + j, next_slot).start()
        x = xb_ref[b]
        a = jnp.dot(x, wg, preferred_element_type=F32)
        u = jnp.dot(x, wu, preferred_element_type=F32)
        act = ((a * jax.nn.sigmoid(a)) * u).astype(BF16)
        carried = jnp.where(f == 0, 0.0, acc_ref[b])
        acc_ref[b] = carried + jnp.dot(act, wd, preferred_element_type=F32)

    @pl.when(f == FF_STEPS - 1)
    def _():
        gates = gates_ref[0]
        for b in range(BATCH):
            y_ref[0, b] = (acc_ref[b] * gates[:, b:b + 1]).astype(BF16)

    @pl.when(jnp.logical_and(e == N_EXPERTS - 1, f == FF_STEPS - 1))
    def _():
        wait_rows(next_slot)


def _experts(tok, h, gates, w_gate, w_up, w_down, layer):
    grid_spec = pltpu.PrefetchScalarGridSpec(
        num_scalar_prefetch=1,
        grid=(N_EXPERTS, FF_STEPS),
        in_specs=[
            pl.BlockSpec(memory_space=pl.ANY),
            pl.BlockSpec((1, CAP, BATCH), lambda e, f, tok: (e, 0, 0)),
            pl.BlockSpec((1, 1, D_MODEL, FF_TILE), lambda e, f, tok: (layer, e, 0, f)),
            pl.BlockSpec((1, 1, D_MODEL, FF_TILE), lambda e, f, tok: (layer, e, 0, f)),
            pl.BlockSpec((1, 1, FF_TILE, D_MODEL), lambda e, f, tok: (layer, e, f, 0)),
        ],
        out_specs=pl.BlockSpec((1, BATCH, CAP, D_MODEL), lambda e, f, tok: (e, 0, 0, 0)),
        scratch_shapes=[pltpu.VMEM((2, GATHER_ROWS * TOKEN_TILE_ROWS, LANES), F32), pltpu.SemaphoreType.DMA((2,)),
                        pltpu.VMEM((BATCH, CAP, D_MODEL), BF16), pltpu.VMEM((BATCH, CAP, D_MODEL), F32)],
    )
    return pl.pallas_call(
        _expert_kernel,
        grid_spec=grid_spec,
        out_shape=jax.ShapeDtypeStruct((N_EXPERTS, BATCH, CAP, D_MODEL), BF16),
        compiler_params=_params(("arbitrary", "arbitrary"), 60),
        name="moe_experts",
    )(tok, h, gates, w_gate, w_up, w_down)


def _combine_kernel(slot_ref, y_ref, x_ref, gate_ref, g_ref, o_ref, *, final):
    lane = lax.broadcasted_iota(jnp.int32, (COMBINE_ROWS, CAP), 1)
    slots = slot_ref[0]
    onehot = jnp.concatenate(
        [jnp.where(slots[:, e:e + 1] == lane, 1.0, 0.0).astype(BF16) for e in range(N_EXPERTS)], axis=1)
    moe = jnp.dot(onehot, y_ref[...].reshape(N_EXPERTS * CAP, D_MODEL), preferred_element_type=F32)
    x = x_ref[0] + gate_ref[0] * moe
    if final:
        x = (x * lax.rsqrt(jnp.mean(x * x, axis=-1, keepdims=True) + RMS_EPS)) * g_ref[...]
    o_ref[0] = x


def _combine(slot_t, y, x, mod, final_g, layer, final):
    return pl.pallas_call(
        functools.partial(_combine_kernel, final=final),
        grid=(BATCH, SEQ // COMBINE_ROWS),
        in_specs=[
            pl.BlockSpec((1, COMBINE_ROWS, N_EXPERTS), lambda b, t: (b, t, 0)),
            pl.BlockSpec((N_EXPERTS, 1, CAP, D_MODEL), lambda b, t: (0, b, 0, 0)),
            pl.BlockSpec((1, COMBINE_ROWS, D_MODEL), lambda b, t: (b, t, 0)),
            pl.BlockSpec((1, 1, D_MODEL), lambda b, t: ((layer * 6 + 5) * BATCH + b, 0, 0)),
            pl.BlockSpec((1, D_MODEL), lambda b, t: (0, 0)),
        ],
        out_specs=pl.BlockSpec((1, COMBINE_ROWS, D_MODEL), lambda b, t: (b, t, 0)),
        out_shape=jax.ShapeDtypeStruct((BATCH, SEQ, D_MODEL), F32),
        compiler_params=_params(("arbitrary", "arbitrary"), 48),
        name="moe_combine",
    )(slot_t, y, x, mod, final_g.reshape(1, D_MODEL))


def _scaled_q_weight(w, q_columns):
    scale = jnp.concatenate([jnp.full((q_columns,), Q_SCALE, F32), jnp.ones((w.shape[1] - q_columns,), F32)])
    return (w * scale).astype(BF16)


def kernel(x, c, ada_w, ada_b, norm_g, na_w_qkv, na_w_o, na_rpb, sw_w_qkv, sw_w_o, sw_sinks, t5_bias,
           moe_w_router, moe_w_gate, moe_w_up, moe_w_down, final_g):
    mod = _ada(c, ada_w, ada_b)
    norm_g3 = norm_g.reshape(DEPTH * 2, 1, D_MODEL)
    x = x.reshape(BATCH * SEQ, D_MODEL)
    for layer in range(DEPTH):
        j = layer // N_MIXERS
        if layer % N_MIXERS == 0:
            qkv = _qkv(x, norm_g3, mod, _scaled_q_weight(na_w_qkv[j], NA_HEADS * HEAD_DIM), layer)
            o = _na_attention(qkv.reshape(BATCH, SEQ, -1), _na_rpb_rows(na_rpb[j]))
            w_o = na_w_o[j]
        else:
            qkv = _qkv(x, norm_g3, mod, _scaled_q_weight(sw_w_qkv[j], SW_Q_HEADS * HEAD_DIM), layer,
                       head_copies=2 * SW_KV_HEADS * HEAD_DIM // LANES)
            o = _sw_attention(qkv.reshape(BATCH, SEQ, -1), _sw_rel_table(t5_bias), sw_sinks[j])
            w_o = sw_w_o[j]
        w_router = moe_w_router[layer].astype(BF16)
        x, h, aff_t = _post_attn(o.reshape(BATCH * SEQ, -1), w_o.astype(BF16), x, norm_g3, mod,
                                 w_router.T, layer)
        slot, tok, gate = _route(aff_t)
        tok = tok.reshape(BATCH, N_EXPERTS, CAP)
        rows = tok.transpose(1, 0, 2) + (jnp.arange(BATCH, dtype=jnp.int32) * SEQ)[None, :, None]
        gates = gate.reshape(BATCH, N_EXPERTS, CAP).transpose(1, 2, 0)
        y = _experts(rows.reshape(-1), h, gates, moe_w_gate, moe_w_up, moe_w_down, layer)
        slot_t = slot.reshape(BATCH, N_EXPERTS, SEQ).transpose(0, 2, 1)
        x = _combine(slot_t, y, x.reshape(BATCH, SEQ, D_MODEL), mod, final_g, layer,
                     final=layer == DEPTH - 1).reshape(BATCH * SEQ, D_MODEL)
    return x.reshape(BATCH, SEQ, D_MODEL)
```

```python
import functools

import numpy as np
import jax
import jax.numpy as jnp
from jax import lax
from jax.experimental import pallas as pl
from jax.experimental.pallas import tpu as pltpu

D_MODEL = 1024
BATCH = 8
SEQ = 2048
DEPTH = 2
GRID_W = 64
ROWS = SEQ // GRID_W
N_MIXERS = 2
HEAD_DIM = 64
NA_HEADS = 16
NA_WIN_H = 8
NA_WIN_W = 16
SW_Q_HEADS = 16
SW_KV_HEADS = 4
SW_GROUP = SW_Q_HEADS // SW_KV_HEADS
SW_WINDOW = 128
SW_BLOCK = 128
SW_NB = SEQ // SW_BLOCK
SW_SPAN = 3 * SW_BLOCK
T5_BUCKETS = 32
T5_MAX_DIST = 128
N_EXPERTS = 16
EXPERT_FF = 2048
EC_CAPACITY = 2
CAP = EC_CAPACITY * SEQ // N_EXPERTS
RMS_EPS = 1e-6
NEG = -1e30
LOG2E = 1.4426950408889634
Q_SCALE = HEAD_DIM ** -0.5 * LOG2E

LANES = 128
TOKEN_TILE_ROWS = D_MODEL // LANES
MIB = 1024 * 1024
F32 = jnp.float32
BF16 = jnp.bfloat16

ROW_TILE = 512
FF_TILE = 512
COMBINE_ROWS = 512
PREFIX_CHUNK = 256


def _params(semantics, vmem_mib, flags=None):
    return pltpu.CompilerParams(dimension_semantics=semantics, vmem_limit_bytes=vmem_mib * MIB, flags=flags)


def _norm_mod(x, g, sc, sh):
    y = x * lax.rsqrt(jnp.mean(x * x, axis=-1, keepdims=True) + RMS_EPS)
    return (y * g) * (1.0 + sc) + sh


def _softmax0(z):
    z = z - jnp.max(z, axis=0, keepdims=True)
    p = jnp.exp(z)
    return p / jnp.sum(p, axis=0, keepdims=True)


def _half_masks(rows):
    lane = lax.broadcasted_iota(jnp.int32, (rows, LANES), 1)
    lo = jnp.where(lane < HEAD_DIM, 1.0, 0.0).astype(BF16)
    hi = jnp.where(lane < HEAD_DIM, 0.0, 1.0).astype(BF16)
    return lo, hi


def _ada_kernel(c_ref, w_ref, b_ref, o_ref):
    c = c_ref[...]
    act = (c * jax.nn.sigmoid(c)).astype(BF16)
    o_ref[0] = jnp.dot(act, w_ref[0].astype(BF16), preferred_element_type=F32) + b_ref[0]


def _ada(c, ada_w, ada_b):
    out = pl.pallas_call(
        _ada_kernel,
        grid=(DEPTH, 6),
        in_specs=[
            pl.BlockSpec((BATCH, D_MODEL), lambda l, k: (0, 0)),
            pl.BlockSpec((1, D_MODEL, D_MODEL), lambda l, k: (l, 0, k)),
            pl.BlockSpec((1, 1, D_MODEL), lambda l, k: (l * 6 + k, 0, 0)),
        ],
        out_specs=pl.BlockSpec((1, BATCH, D_MODEL), lambda l, k: (l * 6 + k, 0, 0)),
        out_shape=jax.ShapeDtypeStruct((DEPTH * 6, BATCH, D_MODEL), F32),
        compiler_params=_params(("arbitrary", "arbitrary"), 32),
        name="ada_mod",
    )(c, ada_w, ada_b.reshape(DEPTH * 6, 1, D_MODEL))
    return out.reshape(DEPTH * 6 * BATCH, 1, D_MODEL)


def _mod_spec(layer, chunk):
    tiles_per_seq = SEQ // ROW_TILE
    return pl.BlockSpec((1, 1, D_MODEL),
                        lambda i: ((layer * 6 + chunk) * BATCH + i // tiles_per_seq, 0, 0))


def _qkv_kernel(x_ref, g_ref, sc_ref, sh_ref, w_ref, o_ref, *, head_copies):
    h = _norm_mod(x_ref[...], g_ref[0], sc_ref[0], sh_ref[0])
    y = jnp.dot(h.astype(BF16), w_ref[...], preferred_element_type=F32)
    if head_copies:
        low_half = lax.broadcasted_iota(jnp.int32, (ROW_TILE, LANES), 1) < HEAD_DIM
        keep = y.shape[1] - head_copies * LANES
        tiles = [y[:, :keep]]
        for t in range(head_copies):
            pair = y[:, keep + t * LANES:keep + (t + 1) * LANES]
            swapped = pltpu.roll(pair, HEAD_DIM, 1)
            tiles += [jnp.where(low_half, pair, swapped), jnp.where(low_half, swapped, pair)]
        y = jnp.concatenate(tiles, axis=1)
    o_ref[...] = y.astype(BF16)


def _qkv(x, norm_g3, mod, w, layer, head_copies=0):
    n = w.shape[1] + head_copies * LANES
    return pl.pallas_call(
        functools.partial(_qkv_kernel, head_copies=head_copies),
        grid=(BATCH * SEQ // ROW_TILE,),
        in_specs=[
            pl.BlockSpec((ROW_TILE, D_MODEL), lambda i: (i, 0)),
            pl.BlockSpec((1, 1, D_MODEL), lambda i: (layer * 2, 0, 0)),
            _mod_spec(layer, 1),
            _mod_spec(layer, 0),
            pl.BlockSpec(w.shape, lambda i: (0, 0)),
        ],
        out_specs=pl.BlockSpec((ROW_TILE, n), lambda i: (i, 0)),
        out_shape=jax.ShapeDtypeStruct((BATCH * SEQ, n), BF16),
        compiler_params=_params(("arbitrary",), 48),
        name="norm_qkv",
    )(x, norm_g3, mod, mod, w)


NA_BIAS_ROWS = 2 * NA_WIN_H - 1
NA_ROW_UNROLL = 4


def _na_rpb_rows(rpb):
    w = NA_WIN_W - 1
    rpb = rpb.astype(F32) * LOG2E
    pad = jnp.zeros(rpb.shape[:2] + (LANES - 2 * w - 1,), F32)
    return jnp.concatenate([rpb[..., w:], pad, rpb[..., :w]], axis=-1)


def _na_build_bias(w_ref, bias_ref):
    lane = lax.broadcasted_iota(jnp.int32, (GRID_W, LANES), 1)
    col = lax.broadcasted_iota(jnp.int32, (GRID_W, LANES), 0)
    kc = jnp.bitwise_and(lane, GRID_W - 1)
    cstart = jnp.clip(col - NA_WIN_W // 2, 0, GRID_W - NA_WIN_W)
    in_window = jnp.where(kc >= cstart, jnp.where(kc < cstart + NA_WIN_W, 1.0, 0.0), 0.0) > 0.5
    low_half = lane < GRID_W

    def toeplitz(head, a, shift):
        row = jnp.broadcast_to(w_ref[head, a:a + 1, :], (GRID_W, LANES))
        return pltpu.roll(row, shift, 1, stride=1, stride_axis=0)

    for head in range(2):
        tiles = [jnp.where(in_window, jnp.where(low_half, toeplitz(head, a, 0), toeplitz(head, a + 1, GRID_W)), NEG)
                 for a in range(NA_BIAS_ROWS - 1)]
        for d in range(NA_WIN_H):
            for m in range(NA_WIN_H // 2):
                bias_ref[d, head * GRID_W:(head + 1) * GRID_W, m * LANES:(m + 1) * LANES] = (
                    tiles[NA_WIN_H - 1 - d + 2 * m])


def _na_kernel(w_ref, q_ref, k_ref, v_ref, o_ref, bias_ref, s0_ref, s1_ref, p0_ref, p1_ref):
    @pl.when(pl.program_id(1) == 0)
    def _():
        _na_build_bias(w_ref, bias_ref)

    lo, hi = _half_masks(GRID_W)
    lo_f = lax.broadcasted_iota(jnp.int32, (GRID_W, LANES), 1) < HEAD_DIM
    span = NA_WIN_H * GRID_W
    ones = jnp.ones((span, LANES), BF16)
    groups = ROWS // NA_ROW_UNROLL

    def group_rows(i):
        for u in range(NA_ROW_UNROLL):
            r = jnp.clip(i, 0, groups - 1) * NA_ROW_UNROLL + u
            yield u, r, jnp.clip(r - NA_WIN_H // 2, 0, ROWS - NA_WIN_H)

    def scores(i, s_ref):
        for u, r, rs in group_rows(i):
            q = q_ref[0, pl.ds(pl.multiple_of(r * GRID_W, GRID_W), GRID_W), :]
            lhs = jnp.concatenate([q * lo, q * hi], axis=0)
            kw = k_ref[0, pl.ds(pl.multiple_of(rs * GRID_W, GRID_W), span), :]
            s_ref[u] = lax.dot_general(lhs, kw, (((1,), (1,)), ((), ())), preferred_element_type=F32)

    def softmax(i, s_ref, p_ref):
        for u, r, rs in group_rows(i):
            s = s_ref[u] + bias_ref[r - rs]
            m = jnp.broadcast_to(jnp.max(s, axis=-1, keepdims=True), (2 * GRID_W, LANES))
            p_ref[u] = jnp.exp2(s - jnp.concatenate([m] * (span // LANES), axis=1)).astype(BF16)

    def values(i, p_ref):
        for u, r, rs in group_rows(i):
            vw = v_ref[0, pl.ds(pl.multiple_of(rs * GRID_W, GRID_W), span), :]
            o = jnp.dot(p_ref[u], jnp.concatenate([vw, ones], axis=1), preferred_element_type=F32)
            o = o[:, :LANES] / o[:, LANES:]
            out = jnp.where(lo_f, o[:GRID_W], o[GRID_W:])
            o_ref[0, pl.ds(pl.multiple_of(r * GRID_W, GRID_W), GRID_W), :] = out.astype(BF16)

    def step(i, carry):
        values(2 * i - 1, p1_ref)
        scores(2 * i + 1, s1_ref)
        softmax(2 * i, s0_ref, p0_ref)
        values(2 * i, p0_ref)
        scores(2 * i + 2, s0_ref)
        softmax(2 * i + 1, s1_ref, p1_ref)
        return carry

    p1_ref[...] = jnp.ones_like(p1_ref)
    scores(0, s0_ref)
    lax.fori_loop(0, groups // 2, step, 0)
    values(groups - 1, p1_ref)


def _na_attention(qkv, rpb_rows):
    pairs = NA_HEADS // 2
    return pl.pallas_call(
        _na_kernel,
        grid=(pairs, BATCH),
        in_specs=[
            pl.BlockSpec((2, NA_BIAS_ROWS, LANES), lambda p, b: (p, 0, 0)),
            pl.BlockSpec((1, SEQ, LANES), lambda p, b: (b, 0, p)),
            pl.BlockSpec((1, SEQ, LANES), lambda p, b: (b, 0, pairs + p)),
            pl.BlockSpec((1, SEQ, LANES), lambda p, b: (b, 0, 2 * pairs + p)),
        ],
        out_specs=pl.BlockSpec((1, SEQ, LANES), lambda p, b: (b, 0, p)),
        out_shape=jax.ShapeDtypeStruct((BATCH, SEQ, NA_HEADS * HEAD_DIM), BF16),
        scratch_shapes=[pltpu.VMEM((NA_WIN_H, 2 * GRID_W, NA_WIN_H * GRID_W), F32),
                        pltpu.VMEM((NA_ROW_UNROLL, 2 * GRID_W, NA_WIN_H * GRID_W), F32),
                        pltpu.VMEM((NA_ROW_UNROLL, 2 * GRID_W, NA_WIN_H * GRID_W), F32),
                        pltpu.VMEM((NA_ROW_UNROLL, 2 * GRID_W, NA_WIN_H * GRID_W), BF16),
                        pltpu.VMEM((NA_ROW_UNROLL, 2 * GRID_W, NA_WIN_H * GRID_W), BF16)],
        compiler_params=_params(("arbitrary", "arbitrary"), 32),
        name="na_attention",
    )(rpb_rows, qkv, qkv, qkv)


def _t5_buckets(rel):
    half = T5_BUCKETS // 2
    max_exact = half // 2
    n = np.abs(rel)
    large = max_exact + (np.log(np.maximum(n, 1) / max_exact)
                         / np.log(T5_MAX_DIST / max_exact) * (half - max_exact)).astype(np.int32)
    large = np.minimum(large, half - 1)
    return (rel > 0).astype(np.int32) * half + np.where(n < max_exact, n, large)


SW_REL_PERIOD = 512
SW_BLOCK_UNROLL = 2


def _sw_rel_table(t5_table):
    k = np.arange(SW_REL_PERIOD)
    rel = np.where(k < SW_REL_PERIOD // 2, k, k - SW_REL_PERIOD)
    ok = np.abs(rel) <= SW_WINDOW
    vals = jnp.where(ok[:, None], t5_table[_t5_buckets(rel)].astype(F32) * LOG2E, NEG)
    return vals.T.reshape(SW_KV_HEADS, SW_GROUP, SW_REL_PERIOD)


def _sw_build_bias(rel_ref, bias_ref):
    for g in range(SW_GROUP):
        row = jnp.broadcast_to(rel_ref[0, g:g + 1, :], (SW_BLOCK, SW_REL_PERIOD))
        t = pltpu.roll(row, 0, 1, stride=1, stride_axis=0)
        rows = slice(g * SW_BLOCK, (g + 1) * SW_BLOCK)
        bias_ref[0, rows, :] = t[:, :SW_SPAN]
        bias_ref[1, rows, :] = jnp.concatenate([t[:, SW_SPAN:], t[:, :2 * SW_BLOCK]], axis=1)
        bias_ref[2, rows, :] = jnp.concatenate([t[:, 2 * SW_BLOCK:], t[:, :SW_BLOCK]], axis=1)


def _sw_kernel(sink_ref, rel_ref, q_ref, k_ref, v_ref, o_ref, bias_ref, s0_ref, s1_ref, p0_ref, p1_ref,
               t0_ref, t1_ref):
    j = pl.program_id(0)
    ones = jnp.ones((SW_SPAN, LANES), BF16)

    @pl.when(pl.program_id(1) == 0)
    def _():
        _sw_build_bias(rel_ref, bias_ref)

    masks = _half_masks(SW_BLOCK)
    lo_f = lax.broadcasted_iota(jnp.int32, (SW_BLOCK, LANES), 1) < HEAD_DIM
    sinks = [jnp.full((SW_BLOCK, LANES), sink_ref[j * SW_GROUP + g] * LOG2E, F32) for g in range(SW_GROUP)]
    groups = SW_NB // SW_BLOCK_UNROLL

    def group_blocks(i):
        for u in range(SW_BLOCK_UNROLL):
            n = jnp.clip(i, 0, groups - 1) * SW_BLOCK_UNROLL + u
            yield u, n, jnp.clip(n - 1, 0, SW_NB - 3)

    def block_rows(n):
        return pl.ds(pl.multiple_of(n * SW_BLOCK, SW_BLOCK), SW_BLOCK)

    def scores(i, s_ref):
        for u, n, first in group_blocks(i):
            kw = k_ref[0, pl.ds(pl.multiple_of(first * SW_BLOCK, SW_BLOCK), SW_SPAN), :]
            for g in range(SW_GROUP):
                q = q_ref[0, block_rows(n), (g // 2) * LANES:(g // 2 + 1) * LANES] * masks[g % 2]
                s_ref[u * SW_GROUP + g] = lax.dot_general(q, kw, (((1,), (1,)), ((), ())),
                                                          preferred_element_type=F32)

    def softmax(i, s_ref, p_ref, t_ref):
        for u, n, first in group_blocks(i):
            for g in range(SW_GROUP):
                c = u * SW_GROUP + g
                s = s_ref[c] + bias_ref[n - first, g * SW_BLOCK:(g + 1) * SW_BLOCK, :]
                m = jnp.maximum(jnp.broadcast_to(jnp.max(s, axis=-1, keepdims=True), (SW_BLOCK, LANES)), sinks[g])
                p_ref[c] = jnp.exp2(s - jnp.concatenate([m] * (SW_SPAN // LANES), axis=1)).astype(BF16)
                t_ref[c] = jnp.exp2(sinks[g] - m)

    def values(i, p_ref, t_ref):
        for u, n, first in group_blocks(i):
            vw = v_ref[0, pl.ds(pl.multiple_of(first * SW_BLOCK, SW_BLOCK), SW_SPAN), :]
            v_ones = jnp.concatenate([vw, ones], axis=1)
            heads = []
            for g in range(SW_GROUP):
                c = u * SW_GROUP + g
                o = jnp.dot(p_ref[c], v_ones, preferred_element_type=F32)
                heads.append(o[:, :LANES] / (o[:, LANES:] + t_ref[c]))
            out = jnp.concatenate([jnp.where(lo_f, heads[0], heads[1]), jnp.where(lo_f, heads[2], heads[3])],
                                  axis=1)
            o_ref[0, block_rows(n), :] = out.astype(BF16)

    def step(i, carry):
        values(2 * i - 1, p1_ref, t1_ref)
        scores(2 * i + 1, s1_ref)
        softmax(2 * i, s0_ref, p0_ref, t0_ref)
        values(2 * i, p0_ref, t0_ref)
        scores(2 * i + 2, s0_ref)
        softmax(2 * i + 1, s1_ref, p1_ref, t1_ref)
        return carry

    p1_ref[...] = jnp.ones_like(p1_ref)
    t1_ref[...] = jnp.ones_like(t1_ref)
    scores(0, s0_ref)
    lax.fori_loop(0, groups // 2, step, 0)
    values(groups - 1, p1_ref, t1_ref)


def _sw_attention(qkv, rel_table, sinks):
    qw = SW_GROUP * HEAD_DIM
    k_off = SW_Q_HEADS * HEAD_DIM // LANES
    v_off = k_off + SW_KV_HEADS
    chains = SW_BLOCK_UNROLL * SW_GROUP
    return pl.pallas_call(
        _sw_kernel,
        grid=(SW_KV_HEADS, BATCH),
        in_specs=[
            pl.BlockSpec(memory_space=pltpu.SMEM),
            pl.BlockSpec((1, SW_GROUP, SW_REL_PERIOD), lambda j, b: (j, 0, 0)),
            pl.BlockSpec((1, SEQ, qw), lambda j, b: (b, 0, j)),
            pl.BlockSpec((1, SEQ, LANES), lambda j, b: (b, 0, k_off + j)),
            pl.BlockSpec((1, SEQ, LANES), lambda j, b: (b, 0, v_off + j)),
        ],
        out_specs=pl.BlockSpec((1, SEQ, qw), lambda j, b: (b, 0, j)),
        out_shape=jax.ShapeDtypeStruct((BATCH, SEQ, SW_Q_HEADS * HEAD_DIM), BF16),
        scratch_shapes=[pltpu.VMEM((3, SW_GROUP * SW_BLOCK, SW_SPAN), F32)]
        + [pltpu.VMEM((chains, SW_BLOCK, SW_SPAN), F32)] * 2
        + [pltpu.VMEM((chains, SW_BLOCK, SW_SPAN), BF16)] * 2
        + [pltpu.VMEM((chains, SW_BLOCK, LANES), F32)] * 2,
        compiler_params=_params(("arbitrary", "arbitrary"), 32),
        name="sw_attention",
    )(sinks, rel_table, qkv, qkv, qkv)


def _post_attn_kernel(o_ref, wo_ref, x_ref, gate_ref, g_ref, sc_ref, sh_ref, wr_ref,
                      xo_ref, h_ref, aff_ref):
    y = jnp.dot(o_ref[...], wo_ref[...], preferred_element_type=F32)
    xn = x_ref[...] + gate_ref[0] * y
    xo_ref[...] = xn
    h = _norm_mod(xn, g_ref[0], sc_ref[0], sh_ref[0])
    for j in range(TOKEN_TILE_ROWS):
        h_ref[pl.ds(j, ROW_TILE, stride=TOKEN_TILE_ROWS), :] = h[:, j * LANES:(j + 1) * LANES]
    hb = h.astype(BF16)
    logits = lax.dot_general(wr_ref[...], hb, (((1,), (1,)), ((), ())), preferred_element_type=F32)
    aff_ref[...] = _softmax0(logits)


def _post_attn(o, w_o, x, norm_g3, mod, w_router_t, layer):
    rows = BATCH * SEQ
    return pl.pallas_call(
        _post_attn_kernel,
        grid=(rows // ROW_TILE,),
        in_specs=[
            pl.BlockSpec((ROW_TILE, D_MODEL), lambda i: (i, 0)),
            pl.BlockSpec((D_MODEL, D_MODEL), lambda i: (0, 0)),
            pl.BlockSpec((ROW_TILE, D_MODEL), lambda i: (i, 0)),
            _mod_spec(layer, 2),
            pl.BlockSpec((1, 1, D_MODEL), lambda i: (layer * 2 + 1, 0, 0)),
            _mod_spec(layer, 4),
            _mod_spec(layer, 3),
            pl.BlockSpec((N_EXPERTS, D_MODEL), lambda i: (0, 0)),
        ],
        out_specs=[
            pl.BlockSpec((ROW_TILE, D_MODEL), lambda i: (i, 0)),
            pl.BlockSpec((ROW_TILE * TOKEN_TILE_ROWS, LANES), lambda i: (i, 0)),
            pl.BlockSpec((N_EXPERTS, ROW_TILE), lambda i: (0, i)),
        ],
        out_shape=[
            jax.ShapeDtypeStruct((rows, D_MODEL), F32),
            jax.ShapeDtypeStruct((rows * TOKEN_TILE_ROWS, LANES), F32),
            jax.ShapeDtypeStruct((N_EXPERTS, rows), F32),
        ],
        compiler_params=_params(("arbitrary",), 48),
        name="post_attn",
    )(o, w_o, x, mod, norm_g3, mod, mod, w_router_t)


ROUTE_SEQS = BATCH
ROUTE_ROWS = ROUTE_SEQS * N_EXPERTS
ROUTE_BITS = SEQ.bit_length() - 1
ROUTE_VALID = 1 << (2 * ROUTE_BITS)


def _prefix_count(x):
    rows = x.shape[0]
    nchunk = SEQ // PREFIX_CHUNK
    r = lax.broadcasted_iota(jnp.int32, (PREFIX_CHUNK, PREFIX_CHUNK), 0)
    c = lax.broadcasted_iota(jnp.int32, (PREFIX_CHUNK, PREFIX_CHUNK), 1)
    upper = jnp.where(r < c, 1.0, 0.0).astype(BF16)
    chunks = [x[:, k * PREFIX_CHUNK:(k + 1) * PREFIX_CHUNK] for k in range(nchunk)]
    local = jnp.dot(jnp.concatenate(chunks, axis=0).astype(BF16), upper, preferred_element_type=F32)
    out = []
    offset = jnp.zeros((rows, 1), F32)
    for k in range(nchunk):
        out.append(local[k * rows:(k + 1) * rows] + offset)
        offset = offset + jnp.sum(chunks[k], axis=1, keepdims=True)
    return jnp.concatenate(out, axis=1)


def _route_kernel(aff_ref, slot_ref, tok_ref, gate_ref, first_ref):
    aff = jnp.concatenate([aff_ref[:, b * SEQ:(b + 1) * SEQ] for b in range(ROUTE_SEQS)], axis=0)
    bits = pltpu.bitcast(aff, jnp.int32)

    def count(mask):
        return jnp.sum(jnp.where(mask, 1.0, 0.0), axis=1, keepdims=True)

    def search(_, bounds):
        lo, hi = bounds
        mid = lo + ((hi - lo) >> 1)
        ge = count(bits >= mid) >= CAP
        return jnp.where(ge, mid, lo), jnp.where(ge, hi, mid)

    lo0 = jnp.zeros((ROUTE_ROWS, 1), jnp.int32)
    hi0 = jnp.full((ROUTE_ROWS, 1), 0x7F800000, jnp.int32)
    tau, _ = lax.fori_loop(0, 31, search, (lo0, hi0))
    gt = bits > tau
    eq = jnp.where(bits == tau, 1.0, 0.0)
    need = CAP - count(gt)
    sel = jnp.where(gt, 1.0, jnp.where(_prefix_count(eq) < need, eq, 0.0))
    pos = _prefix_count(sel).astype(jnp.int32)
    chosen = sel > 0.5
    slot_ref[...] = jnp.where(chosen, pos, -1)
    tile_lane = lax.broadcasted_iota(jnp.int32, (ROUTE_ROWS, LANES), 1)
    first = jnp.zeros((ROUTE_ROWS, LANES), jnp.int32)
    for k in range(SEQ // COMBINE_ROWS):
        first = jnp.where(tile_lane == k, pos[:, k * COMBINE_ROWS:k * COMBINE_ROWS + 1], first)
    first_ref[...] = first
    lane = lax.broadcasted_iota(jnp.int32, (ROUTE_ROWS, SEQ), 1)
    packed = jnp.where(chosen, ROUTE_VALID | (lane << ROUTE_BITS) | (lane - pos), 0)
    gate = aff
    for k in range(ROUTE_BITS):
        step = 1 << k
        from_right = pltpu.roll(packed, SEQ - step, 1)
        gate_right = pltpu.roll(gate, SEQ - step, 1)
        arrives = (from_right & step) != 0
        stays = (packed & step) == 0
        packed = jnp.where(arrives, from_right, jnp.where(stays, packed, 0))
        gate = jnp.where(arrives, gate_right, gate)
    tok_ref[...] = (packed[:, :CAP] >> ROUTE_BITS) & (SEQ - 1)
    gate_ref[...] = gate[:, :CAP]


def _route(aff_t):
    return pl.pallas_call(
        _route_kernel,
        grid=(BATCH // ROUTE_SEQS,),
        in_specs=[pl.BlockSpec((N_EXPERTS, ROUTE_SEQS * SEQ), lambda i: (0, i))],
        out_specs=[pl.BlockSpec((ROUTE_ROWS, SEQ), lambda i: (i, 0)),
                   pl.BlockSpec((ROUTE_ROWS, CAP), lambda i: (i, 0)),
                   pl.BlockSpec((ROUTE_ROWS, CAP), lambda i: (i, 0)),
                   pl.BlockSpec((ROUTE_ROWS, LANES), lambda i: (i, 0))],
        out_shape=[jax.ShapeDtypeStruct((BATCH * N_EXPERTS, SEQ), jnp.int32),
                   jax.ShapeDtypeStruct((BATCH * N_EXPERTS, CAP), jnp.int32),
                   jax.ShapeDtypeStruct((BATCH * N_EXPERTS, CAP), F32),
                   jax.ShapeDtypeStruct((BATCH * N_EXPERTS, LANES), jnp.int32)],
        compiler_params=_params(("arbitrary",), 48),
        name="route",
    )(aff_t)


GATHER_ROWS = BATCH * CAP
FF_STEPS = EXPERT_FF // FF_TILE
ROWS_PER_CHUNK = GATHER_ROWS // (FF_STEPS * BATCH)
PROLOGUE_UNROLL = 8


def _expert_kernel(tok_ref, h_ref, gates_ref, wg_ref, wu_ref, wd_ref, y_ref, x_ref, sem, xb_ref, acc_ref):
    e = pl.program_id(0)
    f = pl.program_id(1)
    slot = lax.rem(e, 2)
    next_slot = 1 - slot
    next_e = jnp.minimum(e + 1, N_EXPERTS - 1)

    tile = TOKEN_TILE_ROWS

    def row_copy(expert, row, dst_slot):
        src = pl.multiple_of(tok_ref[expert * GATHER_ROWS + row] * tile, tile)
        dst = pl.multiple_of(row * tile, tile)
        return pltpu.make_async_copy(h_ref.at[pl.ds(src, tile), :], x_ref.at[dst_slot, pl.ds(dst, tile), :],
                                     sem.at[dst_slot])

    def wait_rows(dst_slot):
        pltpu.make_async_copy(h_ref.at[pl.ds(0, GATHER_ROWS * tile), :], x_ref.at[dst_slot],
                              sem.at[dst_slot]).wait()

    @pl.when(jnp.logical_and(e == 0, f == 0))
    def _():
        def issue(i, carry):
            for u in range(PROLOGUE_UNROLL):
                row_copy(0, i * PROLOGUE_UNROLL + u, 0).start()
            return carry
        lax.fori_loop(0, GATHER_ROWS // PROLOGUE_UNROLL, issue, 0)
        acc_ref[...] = jnp.zeros_like(acc_ref)

    @pl.when(f == 0)
    def _():
        wait_rows(slot)
        for b in range(BATCH):
            cols = [x_ref[slot, pl.ds(b * CAP * tile + j, CAP, stride=tile), :] for j in range(tile)]
            xb_ref[b] = jnp.concatenate(cols, axis=1).astype(BF16)

    wg = wg_ref[0, 0].astype(BF16)
    wu = wu_ref[0, 0].astype(BF16)
    wd = wd_ref[0, 0].astype(BF16)
    for b in range(BATCH):
        first = (f * BATCH + b) * ROWS_PER_CHUNK
        for j in range(ROWS_PER_CHUNK):
            row_copy(next_e, first + j, next_slot).start()
        x = xb_ref[b]
        a = jnp.dot(x, wg, preferred_element_type=F32)
        u = jnp.dot(x, wu, preferred_element_type=F32)
        act = ((a * jax.nn.sigmoid(a)) * u).astype(BF16)
        carried = jnp.where(f == 0, 0.0, acc_ref[b])
        acc_ref[b] = carried + jnp.dot(act, wd, preferred_element_type=F32)

    @pl.when(f == FF_STEPS - 1)
    def _():
        gates = gates_ref[0]
        for b in range(BATCH):
            y_ref[0, b] = (acc_ref[b] * gates[:, b:b + 1]).astype(BF16)

    @pl.when(jnp.logical_and(e == N_EXPERTS - 1, f == FF_STEPS - 1))
    def _():
        wait_rows(next_slot)


def _experts(tok, h, gates, w_gate, w_up, w_down, layer):
    grid_spec = pltpu.PrefetchScalarGridSpec(
        num_scalar_prefetch=1,
        grid=(N_EXPERTS, FF_STEPS),
        in_specs=[
            pl.BlockSpec(memory_space=pl.ANY),
            pl.BlockSpec((1, CAP, BATCH), lambda e, f, tok: (e, 0, 0)),
            pl.BlockSpec((1, 1, D_MODEL, FF_TILE), lambda e, f, tok: (layer, e, 0, f)),
            pl.BlockSpec((1, 1, D_MODEL, FF_TILE), lambda e, f, tok: (layer, e, 0, f)),
            pl.BlockSpec((1, 1, FF_TILE, D_MODEL), lambda e, f, tok: (layer, e, f, 0)),
        ],
        out_specs=pl.BlockSpec((1, BATCH, CAP, D_MODEL), lambda e, f, tok: (e, 0, 0, 0)),
        scratch_shapes=[pltpu.VMEM((2, GATHER_ROWS * TOKEN_TILE_ROWS, LANES), F32), pltpu.SemaphoreType.DMA((2,)),
                        pltpu.VMEM((BATCH, CAP, D_MODEL), BF16), pltpu.VMEM((BATCH, CAP, D_MODEL), F32)],
    )
    return pl.pallas_call(
        _expert_kernel,
        grid_spec=grid_spec,
        out_shape=jax.ShapeDtypeStruct((N_EXPERTS, BATCH, CAP, D_MODEL), BF16),
        compiler_params=_params(("arbitrary", "arbitrary"), 60),
        name="moe_experts",
    )(tok, h, gates, w_gate, w_up, w_down)


COMBINE_WINDOW = 128
SLOT_ALIGN = 16


def _combine_kernel(start_ref, short_ref, slot_ref, y_ref, x_ref, gate_ref, g_ref, o_ref, *, final):
    tile = pl.program_id(0) * (SEQ // COMBINE_ROWS) + pl.program_id(1)
    slots = slot_ref[0]

    def finish(moe):
        x = x_ref[0] + gate_ref[0] * moe
        if final:
            x = (x * lax.rsqrt(jnp.mean(x * x, axis=-1, keepdims=True) + RMS_EPS)) * g_ref[...]
        o_ref[0] = x

    @pl.when(short_ref[tile] == 1)
    def _():
        lane = lax.broadcasted_iota(jnp.int32, (COMBINE_ROWS, COMBINE_WINDOW), 1)
        hits, rows = [], []
        for e in range(N_EXPERTS):
            start = pl.multiple_of(start_ref[tile * N_EXPERTS + e], SLOT_ALIGN)
            hits.append(jnp.where(slots[:, e:e + 1] - start == lane, 1.0, 0.0).astype(BF16))
            rows.append(y_ref[e, 0, pl.ds(start, COMBINE_WINDOW), :])
        finish(jnp.dot(jnp.concatenate(hits, axis=1), jnp.concatenate(rows, axis=0), preferred_element_type=F32))

    @pl.when(short_ref[tile] != 1)
    def _():
        lane = lax.broadcasted_iota(jnp.int32, (COMBINE_ROWS, CAP), 1)
        onehot = jnp.concatenate(
            [jnp.where(slots[:, e:e + 1] == lane, 1.0, 0.0).astype(BF16) for e in range(N_EXPERTS)], axis=1)
        finish(jnp.dot(onehot, y_ref[...].reshape(N_EXPERTS * CAP, D_MODEL), preferred_element_type=F32))


def _combine_windows(first):
    last = jnp.concatenate([first[..., 1:], jnp.full(first.shape[:2] + (1,), CAP, jnp.int32)], axis=-1)
    start = jnp.minimum(first // SLOT_ALIGN * SLOT_ALIGN, CAP - COMBINE_WINDOW)
    short = jnp.all(last <= start + COMBINE_WINDOW, axis=1)
    return start.transpose(0, 2, 1).reshape(-1), short.astype(jnp.int32).reshape(-1)


def _combine(starts, short, slot_t, y, x, mod, final_g, layer, final):
    grid_spec = pltpu.PrefetchScalarGridSpec(
        num_scalar_prefetch=2,
        grid=(BATCH, SEQ // COMBINE_ROWS),
        in_specs=[
            pl.BlockSpec((1, COMBINE_ROWS, N_EXPERTS), lambda b, t, s, f: (b, t, 0)),
            pl.BlockSpec((N_EXPERTS, 1, CAP, D_MODEL), lambda b, t, s, f: (0, b, 0, 0)),
            pl.BlockSpec((1, COMBINE_ROWS, D_MODEL), lambda b, t, s, f: (b, t, 0)),
            pl.BlockSpec((1, 1, D_MODEL), lambda b, t, s, f: ((layer * 6 + 5) * BATCH + b, 0, 0)),
            pl.BlockSpec((1, D_MODEL), lambda b, t, s, f: (0, 0)),
        ],
        out_specs=pl.BlockSpec((1, COMBINE_ROWS, D_MODEL), lambda b, t, s, f: (b, t, 0)),
    )
    return pl.pallas_call(
        functools.partial(_combine_kernel, final=final),
        grid_spec=grid_spec,
        out_shape=jax.ShapeDtypeStruct((BATCH, SEQ, D_MODEL), F32),
        compiler_params=_params(("arbitrary", "arbitrary"), 48),
        name="moe_combine",
    )(starts, short, slot_t, y, x, mod, final_g.reshape(1, D_MODEL))


def _scaled_q_weight(w, q_columns):
    scale = jnp.concatenate([jnp.full((q_columns,), Q_SCALE, F32), jnp.ones((w.shape[1] - q_columns,), F32)])
    return (w * scale).astype(BF16)


def kernel(x, c, ada_w, ada_b, norm_g, na_w_qkv, na_w_o, na_rpb, sw_w_qkv, sw_w_o, sw_sinks, t5_bias,
           moe_w_router, moe_w_gate, moe_w_up, moe_w_down, final_g):
    mod = _ada(c, ada_w, ada_b)
    norm_g3 = norm_g.reshape(DEPTH * 2, 1, D_MODEL)
    x = x.reshape(BATCH * SEQ, D_MODEL)
    for layer in range(DEPTH):
        j = layer // N_MIXERS
        if layer % N_MIXERS == 0:
            qkv = _qkv(x, norm_g3, mod, _scaled_q_weight(na_w_qkv[j], NA_HEADS * HEAD_DIM), layer)
            o = _na_attention(qkv.reshape(BATCH, SEQ, -1), _na_rpb_rows(na_rpb[j]))
            w_o = na_w_o[j]
        else:
            qkv = _qkv(x, norm_g3, mod, _scaled_q_weight(sw_w_qkv[j], SW_Q_HEADS * HEAD_DIM), layer,
                       head_copies=2 * SW_KV_HEADS * HEAD_DIM // LANES)
            o = _sw_attention(qkv.reshape(BATCH, SEQ, -1), _sw_rel_table(t5_bias), sw_sinks[j])
            w_o = sw_w_o[j]
        w_router = moe_w_router[layer].astype(BF16)
        x, h, aff_t = _post_attn(o.reshape(BATCH * SEQ, -1), w_o.astype(BF16), x, norm_g3, mod,
                                 w_router.T, layer)
        slot, tok, gate, first = _route(aff_t)
        tok = tok.reshape(BATCH, N_EXPERTS, CAP)
        rows = tok.transpose(1, 0, 2) + (jnp.arange(BATCH, dtype=jnp.int32) * SEQ)[None, :, None]
        gates = gate.reshape(BATCH, N_EXPERTS, CAP).transpose(1, 2, 0)
        y = _experts(rows.reshape(-1), h, gates, moe_w_gate, moe_w_up, moe_w_down, layer)
        slot_t = slot.reshape(BATCH, N_EXPERTS, SEQ).transpose(0, 2, 1)
        starts, short = _combine_windows(first.reshape(BATCH, N_EXPERTS, LANES)[..., :SEQ // COMBINE_ROWS])
        x = _combine(starts, short, slot_t, y, x.reshape(BATCH, SEQ, D_MODEL), mod, final_g, layer,
                     final=layer == DEPTH - 1).reshape(BATCH * SEQ, D_MODEL)
    return x.reshape(BATCH, SEQ, D_MODEL)
```

```python
import functools

import numpy as np
import jax
import jax.numpy as jnp
from jax import lax
from jax.experimental import pallas as pl
from jax.experimental.pallas import tpu as pltpu

D_MODEL = 1024
BATCH = 8
SEQ = 2048
DEPTH = 2
GRID_W = 64
ROWS = SEQ // GRID_W
N_MIXERS = 2
HEAD_DIM = 64
NA_HEADS = 16
NA_WIN_H = 8
NA_WIN_W = 16
SW_Q_HEADS = 16
SW_KV_HEADS = 4
SW_GROUP = SW_Q_HEADS // SW_KV_HEADS
SW_WINDOW = 128
SW_BLOCK = 128
SW_NB = SEQ // SW_BLOCK
SW_SPAN = 3 * SW_BLOCK
T5_BUCKETS = 32
T5_MAX_DIST = 128
N_EXPERTS = 16
EXPERT_FF = 2048
EC_CAPACITY = 2
CAP = EC_CAPACITY * SEQ // N_EXPERTS
RMS_EPS = 1e-6
NEG = -1e30
LOG2E = 1.4426950408889634
Q_SCALE = HEAD_DIM ** -0.5 * LOG2E

LANES = 128
TOKEN_TILE_ROWS = D_MODEL // LANES
MIB = 1024 * 1024
F32 = jnp.float32
BF16 = jnp.bfloat16

ROW_TILE = 512
FF_TILE = 512
COMBINE_ROWS = 256
PREFIX_CHUNK = 256


def _params(semantics, vmem_mib, flags=None):
    return pltpu.CompilerParams(dimension_semantics=semantics, vmem_limit_bytes=vmem_mib * MIB, flags=flags)


def _norm_mod(x, g, sc, sh):
    y = x * lax.rsqrt(jnp.mean(x * x, axis=-1, keepdims=True) + RMS_EPS)
    return (y * g) * (1.0 + sc) + sh


def _softmax0(z):
    z = z - jnp.max(z, axis=0, keepdims=True)
    p = jnp.exp(z)
    return p / jnp.sum(p, axis=0, keepdims=True)


def _half_masks(rows):
    lane = lax.broadcasted_iota(jnp.int32, (rows, LANES), 1)
    lo = jnp.where(lane < HEAD_DIM, 1.0, 0.0).astype(BF16)
    hi = jnp.where(lane < HEAD_DIM, 0.0, 1.0).astype(BF16)
    return lo, hi


def _ada_kernel(c_ref, w_ref, b_ref, o_ref):
    c = c_ref[...]
    act = (c * jax.nn.sigmoid(c)).astype(BF16)
    o_ref[0] = jnp.dot(act, w_ref[0].astype(BF16), preferred_element_type=F32) + b_ref[0]


def _ada(c, ada_w, ada_b):
    out = pl.pallas_call(
        _ada_kernel,
        grid=(DEPTH, 6),
        in_specs=[
            pl.BlockSpec((BATCH, D_MODEL), lambda l, k: (0, 0)),
            pl.BlockSpec((1, D_MODEL, D_MODEL), lambda l, k: (l, 0, k)),
            pl.BlockSpec((1, 1, D_MODEL), lambda l, k: (l * 6 + k, 0, 0)),
        ],
        out_specs=pl.BlockSpec((1, BATCH, D_MODEL), lambda l, k: (l * 6 + k, 0, 0)),
        out_shape=jax.ShapeDtypeStruct((DEPTH * 6, BATCH, D_MODEL), F32),
        compiler_params=_params(("arbitrary", "arbitrary"), 32),
        name="ada_mod",
    )(c, ada_w, ada_b.reshape(DEPTH * 6, 1, D_MODEL))
    return out.reshape(DEPTH * 6 * BATCH, 1, D_MODEL)


def _mod_spec(layer, chunk):
    tiles_per_seq = SEQ // ROW_TILE
    return pl.BlockSpec((1, 1, D_MODEL),
                        lambda i: ((layer * 6 + chunk) * BATCH + i // tiles_per_seq, 0, 0))


def _qkv_kernel(x_ref, g_ref, sc_ref, sh_ref, w_ref, o_ref, *, head_copies):
    h = _norm_mod(x_ref[...], g_ref[0], sc_ref[0], sh_ref[0])
    y = jnp.dot(h.astype(BF16), w_ref[...], preferred_element_type=F32)
    if head_copies:
        low_half = lax.broadcasted_iota(jnp.int32, (ROW_TILE, LANES), 1) < HEAD_DIM
        keep = y.shape[1] - head_copies * LANES
        tiles = [y[:, :keep]]
        for t in range(head_copies):
            pair = y[:, keep + t * LANES:keep + (t + 1) * LANES]
            swapped = pltpu.roll(pair, HEAD_DIM, 1)
            tiles += [jnp.where(low_half, pair, swapped), jnp.where(low_half, swapped, pair)]
        y = jnp.concatenate(tiles, axis=1)
    o_ref[...] = y.astype(BF16)


def _qkv(x, norm_g3, mod, w, layer, head_copies=0):
    n = w.shape[1] + head_copies * LANES
    return pl.pallas_call(
        functools.partial(_qkv_kernel, head_copies=head_copies),
        grid=(BATCH * SEQ // ROW_TILE,),
        in_specs=[
            pl.BlockSpec((ROW_TILE, D_MODEL), lambda i: (i, 0)),
            pl.BlockSpec((1, 1, D_MODEL), lambda i: (layer * 2, 0, 0)),
            _mod_spec(layer, 1),
            _mod_spec(layer, 0),
            pl.BlockSpec(w.shape, lambda i: (0, 0)),
        ],
        out_specs=pl.BlockSpec((ROW_TILE, n), lambda i: (i, 0)),
        out_shape=jax.ShapeDtypeStruct((BATCH * SEQ, n), BF16),
        compiler_params=_params(("arbitrary",), 48),
        name="norm_qkv",
    )(x, norm_g3, mod, mod, w)


NA_BIAS_ROWS = 2 * NA_WIN_H - 1
NA_ROW_UNROLL = 4


def _na_rpb_rows(rpb):
    w = NA_WIN_W - 1
    rpb = rpb.astype(F32) * LOG2E
    pad = jnp.zeros(rpb.shape[:2] + (LANES - 2 * w - 1,), F32)
    return jnp.concatenate([rpb[..., w:], pad, rpb[..., :w]], axis=-1)


def _na_build_bias(w_ref, bias_ref):
    lane = lax.broadcasted_iota(jnp.int32, (GRID_W, LANES), 1)
    col = lax.broadcasted_iota(jnp.int32, (GRID_W, LANES), 0)
    kc = jnp.bitwise_and(lane, GRID_W - 1)
    cstart = jnp.clip(col - NA_WIN_W // 2, 0, GRID_W - NA_WIN_W)
    in_window = jnp.where(kc >= cstart, jnp.where(kc < cstart + NA_WIN_W, 1.0, 0.0), 0.0) > 0.5
    low_half = lane < GRID_W

    def toeplitz(head, a, shift):
        row = jnp.broadcast_to(w_ref[head, a:a + 1, :], (GRID_W, LANES))
        return pltpu.roll(row, shift, 1, stride=1, stride_axis=0)

    for head in range(2):
        tiles = [jnp.where(in_window, jnp.where(low_half, toeplitz(head, a, 0), toeplitz(head, a + 1, GRID_W)), NEG)
                 for a in range(NA_BIAS_ROWS - 1)]
        for d in range(NA_WIN_H):
            for m in range(NA_WIN_H // 2):
                bias_ref[d, head * GRID_W:(head + 1) * GRID_W, m * LANES:(m + 1) * LANES] = (
                    tiles[NA_WIN_H - 1 - d + 2 * m])


def _na_kernel(w_ref, q_ref, k_ref, v_ref, o_ref, bias_ref, s0_ref, s1_ref, p0_ref, p1_ref):
    @pl.when(pl.program_id(1) == 0)
    def _():
        _na_build_bias(w_ref, bias_ref)

    lo, hi = _half_masks(GRID_W)
    lo_f = lax.broadcasted_iota(jnp.int32, (GRID_W, LANES), 1) < HEAD_DIM
    span = NA_WIN_H * GRID_W
    ones = jnp.ones((span, LANES), BF16)
    groups = ROWS // NA_ROW_UNROLL

    def group_rows(i):
        for u in range(NA_ROW_UNROLL):
            r = jnp.clip(i, 0, groups - 1) * NA_ROW_UNROLL + u
            yield u, r, jnp.clip(r - NA_WIN_H // 2, 0, ROWS - NA_WIN_H)

    def scores(i, s_ref):
        for u, r, rs in group_rows(i):
            q = q_ref[0, pl.ds(pl.multiple_of(r * GRID_W, GRID_W), GRID_W), :]
            lhs = jnp.concatenate([q * lo, q * hi], axis=0)
            kw = k_ref[0, pl.ds(pl.multiple_of(rs * GRID_W, GRID_W), span), :]
            s_ref[u] = lax.dot_general(lhs, kw, (((1,), (1,)), ((), ())), preferred_element_type=F32)

    def softmax(i, s_ref, p_ref):
        for u, r, rs in group_rows(i):
            s = s_ref[u] + bias_ref[r - rs]
            m = jnp.broadcast_to(jnp.max(s, axis=-1, keepdims=True), (2 * GRID_W, LANES))
            p_ref[u] = jnp.exp2(s - jnp.concatenate([m] * (span // LANES), axis=1)).astype(BF16)

    def values(i, p_ref):
        for u, r, rs in group_rows(i):
            vw = v_ref[0, pl.ds(pl.multiple_of(rs * GRID_W, GRID_W), span), :]
            o = jnp.dot(p_ref[u], jnp.concatenate([vw, ones], axis=1), preferred_element_type=F32)
            o = o[:, :LANES] / o[:, LANES:]
            out = jnp.where(lo_f, o[:GRID_W], o[GRID_W:])
            o_ref[0, pl.ds(pl.multiple_of(r * GRID_W, GRID_W), GRID_W), :] = out.astype(BF16)

    def step(i, carry):
        values(2 * i - 1, p1_ref)
        scores(2 * i + 1, s1_ref)
        softmax(2 * i, s0_ref, p0_ref)
        values(2 * i, p0_ref)
        scores(2 * i + 2, s0_ref)
        softmax(2 * i + 1, s1_ref, p1_ref)
        return carry

    p1_ref[...] = jnp.ones_like(p1_ref)
    scores(0, s0_ref)
    lax.fori_loop(0, groups // 2, step, 0)
    values(groups - 1, p1_ref)


def _na_attention(qkv, rpb_rows):
    pairs = NA_HEADS // 2
    return pl.pallas_call(
        _na_kernel,
        grid=(pairs, BATCH),
        in_specs=[
            pl.BlockSpec((2, NA_BIAS_ROWS, LANES), lambda p, b: (p, 0, 0)),
            pl.BlockSpec((1, SEQ, LANES), lambda p, b: (b, 0, p)),
            pl.BlockSpec((1, SEQ, LANES), lambda p, b: (b, 0, pairs + p)),
            pl.BlockSpec((1, SEQ, LANES), lambda p, b: (b, 0, 2 * pairs + p)),
        ],
        out_specs=pl.BlockSpec((1, SEQ, LANES), lambda p, b: (b, 0, p)),
        out_shape=jax.ShapeDtypeStruct((BATCH, SEQ, NA_HEADS * HEAD_DIM), BF16),
        scratch_shapes=[pltpu.VMEM((NA_WIN_H, 2 * GRID_W, NA_WIN_H * GRID_W), F32),
                        pltpu.VMEM((NA_ROW_UNROLL, 2 * GRID_W, NA_WIN_H * GRID_W), F32),
                        pltpu.VMEM((NA_ROW_UNROLL, 2 * GRID_W, NA_WIN_H * GRID_W), F32),
                        pltpu.VMEM((NA_ROW_UNROLL, 2 * GRID_W, NA_WIN_H * GRID_W), BF16),
                        pltpu.VMEM((NA_ROW_UNROLL, 2 * GRID_W, NA_WIN_H * GRID_W), BF16)],
        compiler_params=_params(("arbitrary", "arbitrary"), 32),
        name="na_attention",
    )(rpb_rows, qkv, qkv, qkv)


def _t5_buckets(rel):
    half = T5_BUCKETS // 2
    max_exact = half // 2
    n = np.abs(rel)
    large = max_exact + (np.log(np.maximum(n, 1) / max_exact)
                         / np.log(T5_MAX_DIST / max_exact) * (half - max_exact)).astype(np.int32)
    large = np.minimum(large, half - 1)
    return (rel > 0).astype(np.int32) * half + np.where(n < max_exact, n, large)


SW_REL_PERIOD = 512
SW_BLOCK_UNROLL = 2


def _sw_rel_table(t5_table):
    k = np.arange(SW_REL_PERIOD)
    rel = np.where(k < SW_REL_PERIOD // 2, k, k - SW_REL_PERIOD)
    ok = np.abs(rel) <= SW_WINDOW
    vals = jnp.where(ok[:, None], t5_table[_t5_buckets(rel)].astype(F32) * LOG2E, NEG)
    return vals.T.reshape(SW_KV_HEADS, SW_GROUP, SW_REL_PERIOD)


def _sw_build_bias(rel_ref, bias_ref):
    for g in range(SW_GROUP):
        row = jnp.broadcast_to(rel_ref[0, g:g + 1, :], (SW_BLOCK, SW_REL_PERIOD))
        t = pltpu.roll(row, 0, 1, stride=1, stride_axis=0)
        rows = slice(g * SW_BLOCK, (g + 1) * SW_BLOCK)
        bias_ref[0, rows, :] = t[:, :SW_SPAN]
        bias_ref[1, rows, :] = jnp.concatenate([t[:, SW_SPAN:], t[:, :2 * SW_BLOCK]], axis=1)
        bias_ref[2, rows, :] = jnp.concatenate([t[:, 2 * SW_BLOCK:], t[:, :SW_BLOCK]], axis=1)


def _sw_kernel(sink_ref, rel_ref, q_ref, k_ref, v_ref, o_ref, bias_ref, s0_ref, s1_ref, p0_ref, p1_ref,
               t0_ref, t1_ref):
    j = pl.program_id(0)
    ones = jnp.ones((SW_SPAN, LANES), BF16)

    @pl.when(pl.program_id(1) == 0)
    def _():
        _sw_build_bias(rel_ref, bias_ref)

    masks = _half_masks(SW_BLOCK)
    lo_f = lax.broadcasted_iota(jnp.int32, (SW_BLOCK, LANES), 1) < HEAD_DIM
    sinks = [jnp.full((SW_BLOCK, LANES), sink_ref[j * SW_GROUP + g] * LOG2E, F32) for g in range(SW_GROUP)]
    groups = SW_NB // SW_BLOCK_UNROLL

    def group_blocks(i):
        for u in range(SW_BLOCK_UNROLL):
            n = jnp.clip(i, 0, groups - 1) * SW_BLOCK_UNROLL + u
            yield u, n, jnp.clip(n - 1, 0, SW_NB - 3)

    def block_rows(n):
        return pl.ds(pl.multiple_of(n * SW_BLOCK, SW_BLOCK), SW_BLOCK)

    def scores(i, s_ref):
        for u, n, first in group_blocks(i):
            kw = k_ref[0, pl.ds(pl.multiple_of(first * SW_BLOCK, SW_BLOCK), SW_SPAN), :]
            for g in range(SW_GROUP):
                q = q_ref[0, block_rows(n), (g // 2) * LANES:(g // 2 + 1) * LANES] * masks[g % 2]
                s_ref[u * SW_GROUP + g] = lax.dot_general(q, kw, (((1,), (1,)), ((), ())),
                                                          preferred_element_type=F32)

    def softmax(i, s_ref, p_ref, t_ref):
        for u, n, first in group_blocks(i):
            for g in range(SW_GROUP):
                c = u * SW_GROUP + g
                s = s_ref[c] + bias_ref[n - first, g * SW_BLOCK:(g + 1) * SW_BLOCK, :]
                m = jnp.maximum(jnp.broadcast_to(jnp.max(s, axis=-1, keepdims=True), (SW_BLOCK, LANES)), sinks[g])
                p_ref[c] = jnp.exp2(s - jnp.concatenate([m] * (SW_SPAN // LANES), axis=1)).astype(BF16)
                t_ref[c] = jnp.exp2(sinks[g] - m)

    def values(i, p_ref, t_ref):
        for u, n, first in group_blocks(i):
            vw = v_ref[0, pl.ds(pl.multiple_of(first * SW_BLOCK, SW_BLOCK), SW_SPAN), :]
            v_ones = jnp.concatenate([vw, ones], axis=1)
            heads = []
            for g in range(SW_GROUP):
                c = u * SW_GROUP + g
                o = jnp.dot(p_ref[c], v_ones, preferred_element_type=F32)
                heads.append(o[:, :LANES] / (o[:, LANES:] + t_ref[c]))
            out = jnp.concatenate([jnp.where(lo_f, heads[0], heads[1]), jnp.where(lo_f, heads[2], heads[3])],
                                  axis=1)
            o_ref[0, block_rows(n), :] = out.astype(BF16)

    def step(i, carry):
        values(2 * i - 1, p1_ref, t1_ref)
        scores(2 * i + 1, s1_ref)
        softmax(2 * i, s0_ref, p0_ref, t0_ref)
        values(2 * i, p0_ref, t0_ref)
        scores(2 * i + 2, s0_ref)
        softmax(2 * i + 1, s1_ref, p1_ref, t1_ref)
        return carry

    p1_ref[...] = jnp.ones_like(p1_ref)
    t1_ref[...] = jnp.ones_like(t1_ref)
    scores(0, s0_ref)
    lax.fori_loop(0, groups // 2, step, 0)
    values(groups - 1, p1_ref, t1_ref)


def _sw_attention(qkv, rel_table, sinks):
    qw = SW_GROUP * HEAD_DIM
    k_off = SW_Q_HEADS * HEAD_DIM // LANES
    v_off = k_off + SW_KV_HEADS
    chains = SW_BLOCK_UNROLL * SW_GROUP
    return pl.pallas_call(
        _sw_kernel,
        grid=(SW_KV_HEADS, BATCH),
        in_specs=[
            pl.BlockSpec(memory_space=pltpu.SMEM),
            pl.BlockSpec((1, SW_GROUP, SW_REL_PERIOD), lambda j, b: (j, 0, 0)),
            pl.BlockSpec((1, SEQ, qw), lambda j, b: (b, 0, j)),
            pl.BlockSpec((1, SEQ, LANES), lambda j, b: (b, 0, k_off + j)),
            pl.BlockSpec((1, SEQ, LANES), lambda j, b: (b, 0, v_off + j)),
        ],
        out_specs=pl.BlockSpec((1, SEQ, qw), lambda j, b: (b, 0, j)),
        out_shape=jax.ShapeDtypeStruct((BATCH, SEQ, SW_Q_HEADS * HEAD_DIM), BF16),
        scratch_shapes=[pltpu.VMEM((3, SW_GROUP * SW_BLOCK, SW_SPAN), F32)]
        + [pltpu.VMEM((chains, SW_BLOCK, SW_SPAN), F32)] * 2
        + [pltpu.VMEM((chains, SW_BLOCK, SW_SPAN), BF16)] * 2
        + [pltpu.VMEM((chains, SW_BLOCK, LANES), F32)] * 2,
        compiler_params=_params(("arbitrary", "arbitrary"), 32),
        name="sw_attention",
    )(sinks, rel_table, qkv, qkv, qkv)


def _post_attn_kernel(o_ref, wo_ref, x_ref, gate_ref, g_ref, sc_ref, sh_ref, wr_ref,
                      xo_ref, h_ref, aff_ref):
    y = jnp.dot(o_ref[...], wo_ref[...], preferred_element_type=F32)
    xn = x_ref[...] + gate_ref[0] * y
    xo_ref[...] = xn
    h = _norm_mod(xn, g_ref[0], sc_ref[0], sh_ref[0])
    for j in range(TOKEN_TILE_ROWS):
        h_ref[pl.ds(j, ROW_TILE, stride=TOKEN_TILE_ROWS), :] = h[:, j * LANES:(j + 1) * LANES]
    hb = h.astype(BF16)
    logits = lax.dot_general(wr_ref[...], hb, (((1,), (1,)), ((), ())), preferred_element_type=F32)
    aff_ref[...] = _softmax0(logits)


def _post_attn(o, w_o, x, norm_g3, mod, w_router_t, layer):
    rows = BATCH * SEQ
    return pl.pallas_call(
        _post_attn_kernel,
        grid=(rows // ROW_TILE,),
        in_specs=[
            pl.BlockSpec((ROW_TILE, D_MODEL), lambda i: (i, 0)),
            pl.BlockSpec((D_MODEL, D_MODEL), lambda i: (0, 0)),
            pl.BlockSpec((ROW_TILE, D_MODEL), lambda i: (i, 0)),
            _mod_spec(layer, 2),
            pl.BlockSpec((1, 1, D_MODEL), lambda i: (layer * 2 + 1, 0, 0)),
            _mod_spec(layer, 4),
            _mod_spec(layer, 3),
            pl.BlockSpec((N_EXPERTS, D_MODEL), lambda i: (0, 0)),
        ],
        out_specs=[
            pl.BlockSpec((ROW_TILE, D_MODEL), lambda i: (i, 0)),
            pl.BlockSpec((ROW_TILE * TOKEN_TILE_ROWS, LANES), lambda i: (i, 0)),
            pl.BlockSpec((N_EXPERTS, ROW_TILE), lambda i: (0, i)),
        ],
        out_shape=[
            jax.ShapeDtypeStruct((rows, D_MODEL), F32),
            jax.ShapeDtypeStruct((rows * TOKEN_TILE_ROWS, LANES), F32),
            jax.ShapeDtypeStruct((N_EXPERTS, rows), F32),
        ],
        compiler_params=_params(("arbitrary",), 48),
        name="post_attn",
    )(o, w_o, x, mod, norm_g3, mod, mod, w_router_t)


ROUTE_SEQS = BATCH
ROUTE_ROWS = ROUTE_SEQS * N_EXPERTS
ROUTE_BITS = SEQ.bit_length() - 1
ROUTE_VALID = 1 << (2 * ROUTE_BITS)


def _prefix_count(x):
    rows = x.shape[0]
    nchunk = SEQ // PREFIX_CHUNK
    r = lax.broadcasted_iota(jnp.int32, (PREFIX_CHUNK, PREFIX_CHUNK), 0)
    c = lax.broadcasted_iota(jnp.int32, (PREFIX_CHUNK, PREFIX_CHUNK), 1)
    upper = jnp.where(r < c, 1.0, 0.0).astype(BF16)
    chunks = [x[:, k * PREFIX_CHUNK:(k + 1) * PREFIX_CHUNK] for k in range(nchunk)]
    local = jnp.dot(jnp.concatenate(chunks, axis=0).astype(BF16), upper, preferred_element_type=F32)
    out = []
    offset = jnp.zeros((rows, 1), F32)
    for k in range(nchunk):
        out.append(local[k * rows:(k + 1) * rows] + offset)
        offset = offset + jnp.sum(chunks[k], axis=1, keepdims=True)
    return jnp.concatenate(out, axis=1)


def _route_kernel(aff_ref, slot_ref, tok_ref, gate_ref, first_ref):
    aff = jnp.concatenate([aff_ref[:, b * SEQ:(b + 1) * SEQ] for b in range(ROUTE_SEQS)], axis=0)
    bits = pltpu.bitcast(aff, jnp.int32)

    def count(mask):
        return jnp.sum(jnp.where(mask, 1.0, 0.0), axis=1, keepdims=True)

    def search(_, bounds):
        lo, hi = bounds
        mid = lo + ((hi - lo) >> 1)
        ge = count(bits >= mid) >= CAP
        return jnp.where(ge, mid, lo), jnp.where(ge, hi, mid)

    lo0 = jnp.zeros((ROUTE_ROWS, 1), jnp.int32)
    hi0 = jnp.full((ROUTE_ROWS, 1), 0x7F800000, jnp.int32)
    tau, _ = lax.fori_loop(0, 31, search, (lo0, hi0))
    gt = bits > tau
    eq = jnp.where(bits == tau, 1.0, 0.0)
    need = CAP - count(gt)
    sel = jnp.where(gt, 1.0, jnp.where(_prefix_count(eq) < need, eq, 0.0))
    pos = _prefix_count(sel).astype(jnp.int32)
    chosen = sel > 0.5
    slot_ref[...] = jnp.where(chosen, pos, -1)
    tile_lane = lax.broadcasted_iota(jnp.int32, (ROUTE_ROWS, LANES), 1)
    first = jnp.zeros((ROUTE_ROWS, LANES), jnp.int32)
    for k in range(SEQ // COMBINE_ROWS):
        first = jnp.where(tile_lane == k, pos[:, k * COMBINE_ROWS:k * COMBINE_ROWS + 1], first)
    first_ref[...] = first
    lane = lax.broadcasted_iota(jnp.int32, (ROUTE_ROWS, SEQ), 1)
    packed = jnp.where(chosen, ROUTE_VALID | (lane << ROUTE_BITS) | (lane - pos), 0)
    gate = aff
    for k in range(ROUTE_BITS):
        step = 1 << k
        from_right = pltpu.roll(packed, SEQ - step, 1)
        gate_right = pltpu.roll(gate, SEQ - step, 1)
        arrives = (from_right & step) != 0
        stays = (packed & step) == 0
        packed = jnp.where(arrives, from_right, jnp.where(stays, packed, 0))
        gate = jnp.where(arrives, gate_right, gate)
    tok_ref[...] = (packed[:, :CAP] >> ROUTE_BITS) & (SEQ - 1)
    gate_ref[...] = gate[:, :CAP]


def _route(aff_t):
    return pl.pallas_call(
        _route_kernel,
        grid=(BATCH // ROUTE_SEQS,),
        in_specs=[pl.BlockSpec((N_EXPERTS, ROUTE_SEQS * SEQ), lambda i: (0, i))],
        out_specs=[pl.BlockSpec((ROUTE_ROWS, SEQ), lambda i: (i, 0)),
                   pl.BlockSpec((ROUTE_ROWS, CAP), lambda i: (i, 0)),
                   pl.BlockSpec((ROUTE_ROWS, CAP), lambda i: (i, 0)),
                   pl.BlockSpec((ROUTE_ROWS, LANES), lambda i: (i, 0))],
        out_shape=[jax.ShapeDtypeStruct((BATCH * N_EXPERTS, SEQ), jnp.int32),
                   jax.ShapeDtypeStruct((BATCH * N_EXPERTS, CAP), jnp.int32),
                   jax.ShapeDtypeStruct((BATCH * N_EXPERTS, CAP), F32),
                   jax.ShapeDtypeStruct((BATCH * N_EXPERTS, LANES), jnp.int32)],
        compiler_params=_params(("arbitrary",), 48),
        name="route",
    )(aff_t)


GATHER_ROWS = BATCH * CAP
FF_STEPS = EXPERT_FF // FF_TILE
ROWS_PER_CHUNK = GATHER_ROWS // (FF_STEPS * BATCH)
PROLOGUE_UNROLL = 8


def _expert_kernel(tok_ref, h_ref, gates_ref, wg_ref, wu_ref, wd_ref, y_ref, x_ref, sem, xb_ref, acc_ref):
    e = pl.program_id(0)
    f = pl.program_id(1)
    slot = lax.rem(e, 2)
    next_slot = 1 - slot
    next_e = jnp.minimum(e + 1, N_EXPERTS - 1)

    tile = TOKEN_TILE_ROWS

    def row_copy(expert, row, dst_slot):
        src = pl.multiple_of(tok_ref[expert * GATHER_ROWS + row] * tile, tile)
        dst = pl.multiple_of(row * tile, tile)
        return pltpu.make_async_copy(h_ref.at[pl.ds(src, tile), :], x_ref.at[dst_slot, pl.ds(dst, tile), :],
                                     sem.at[dst_slot])

    def wait_rows(dst_slot):
        pltpu.make_async_copy(h_ref.at[pl.ds(0, GATHER_ROWS * tile), :], x_ref.at[dst_slot],
                              sem.at[dst_slot]).wait()

    @pl.when(jnp.logical_and(e == 0, f == 0))
    def _():
        def issue(i, carry):
            for u in range(PROLOGUE_UNROLL):
                row_copy(0, i * PROLOGUE_UNROLL + u, 0).start()
            return carry
        lax.fori_loop(0, GATHER_ROWS // PROLOGUE_UNROLL, issue, 0)
        acc_ref[...] = jnp.zeros_like(acc_ref)

    @pl.when(f == 0)
    def _():
        wait_rows(slot)
        for b in range(BATCH):
            cols = [x_ref[slot, pl.ds(b * CAP * tile + j, CAP, stride=tile), :] for j in range(tile)]
            xb_ref[b] = jnp.concatenate(cols, axis=1).astype(BF16)

    wg = wg_ref[0, 0].astype(BF16)
    wu = wu_ref[0, 0].astype(BF16)
    wd = wd_ref[0, 0].astype(BF16)
    for b in range(BATCH):
        first = (f * BATCH + b) * ROWS_PER_CHUNK
        for j in range(ROWS_PER_CHUNK):
            row_copy(next_e, first + j, next_slot).start()
        x = xb_ref[b]
        a = jnp.dot(x, wg, preferred_element_type=F32)
        u = jnp.dot(x, wu, preferred_element_type=F32)
        act = ((a * jax.nn.sigmoid(a)) * u).astype(BF16)
        carried = jnp.where(f == 0, 0.0, acc_ref[b])
        acc_ref[b] = carried + jnp.dot(act, wd, preferred_element_type=F32)

    @pl.when(f == FF_STEPS - 1)
    def _():
        gates = gates_ref[0]
        for b in range(BATCH):
            y_ref[0, b] = (acc_ref[b] * gates[:, b:b + 1]).astype(BF16)

    @pl.when(jnp.logical_and(e == N_EXPERTS - 1, f == FF_STEPS - 1))
    def _():
        wait_rows(next_slot)


def _experts(tok, h, gates, w_gate, w_up, w_down, layer):
    grid_spec = pltpu.PrefetchScalarGridSpec(
        num_scalar_prefetch=1,
        grid=(N_EXPERTS, FF_STEPS),
        in_specs=[
            pl.BlockSpec(memory_space=pl.ANY),
            pl.BlockSpec((1, CAP, BATCH), lambda e, f, tok: (e, 0, 0)),
            pl.BlockSpec((1, 1, D_MODEL, FF_TILE), lambda e, f, tok: (layer, e, 0, f)),
            pl.BlockSpec((1, 1, D_MODEL, FF_TILE), lambda e, f, tok: (layer, e, 0, f)),
            pl.BlockSpec((1, 1, FF_TILE, D_MODEL), lambda e, f, tok: (layer, e, f, 0)),
        ],
        out_specs=pl.BlockSpec((1, BATCH, CAP, D_MODEL), lambda e, f, tok: (e, 0, 0, 0)),
        scratch_shapes=[pltpu.VMEM((2, GATHER_ROWS * TOKEN_TILE_ROWS, LANES), F32), pltpu.SemaphoreType.DMA((2,)),
                        pltpu.VMEM((BATCH, CAP, D_MODEL), BF16), pltpu.VMEM((BATCH, CAP, D_MODEL), F32)],
    )
    return pl.pallas_call(
        _expert_kernel,
        grid_spec=grid_spec,
        out_shape=jax.ShapeDtypeStruct((N_EXPERTS, BATCH, CAP, D_MODEL), BF16),
        compiler_params=_params(("arbitrary", "arbitrary"), 60),
        name="moe_experts",
    )(tok, h, gates, w_gate, w_up, w_down)


COMBINE_WINDOW = 64
SLOT_ALIGN = 16


def _combine_kernel(start_ref, short_ref, slot_ref, y_ref, x_ref, gate_ref, g_ref, o_ref, *, final):
    tile = pl.program_id(0) * (SEQ // COMBINE_ROWS) + pl.program_id(1)
    slots = slot_ref[0]

    def finish(moe):
        x = x_ref[0] + gate_ref[0] * moe
        if final:
            x = (x * lax.rsqrt(jnp.mean(x * x, axis=-1, keepdims=True) + RMS_EPS)) * g_ref[...]
        o_ref[0] = x

    @pl.when(short_ref[tile] == 1)
    def _():
        lane = lax.broadcasted_iota(jnp.int32, (COMBINE_ROWS, LANES), 1)
        in_window = jnp.bitwise_and(lane, COMBINE_WINDOW - 1)
        per_tile = LANES // COMBINE_WINDOW
        hits, rows = [], []
        for e0 in range(0, N_EXPERTS, per_tile):
            rel = None
            for w in range(per_tile):
                start = pl.multiple_of(start_ref[tile * N_EXPERTS + e0 + w], SLOT_ALIGN)
                rows.append(y_ref[e0 + w, 0, pl.ds(start, COMBINE_WINDOW), :])
                own = slots[:, e0 + w:e0 + w + 1] - start
                rel = own if rel is None else jnp.where(lane < w * COMBINE_WINDOW, rel, own)
            hits.append(jnp.where(rel == in_window, 1.0, 0.0).astype(BF16))
        finish(jnp.dot(jnp.concatenate(hits, axis=1), jnp.concatenate(rows, axis=0), preferred_element_type=F32))

    @pl.when(short_ref[tile] != 1)
    def _():
        lane = lax.broadcasted_iota(jnp.int32, (COMBINE_ROWS, CAP), 1)
        onehot = jnp.concatenate(
            [jnp.where(slots[:, e:e + 1] == lane, 1.0, 0.0).astype(BF16) for e in range(N_EXPERTS)], axis=1)
        finish(jnp.dot(onehot, y_ref[...].reshape(N_EXPERTS * CAP, D_MODEL), preferred_element_type=F32))


def _combine_windows(first):
    last = jnp.concatenate([first[..., 1:], jnp.full(first.shape[:2] + (1,), CAP, jnp.int32)], axis=-1)
    start = jnp.minimum(first // SLOT_ALIGN * SLOT_ALIGN, CAP - COMBINE_WINDOW)
    short = jnp.all(last <= start + COMBINE_WINDOW, axis=1)
    return start.transpose(0, 2, 1).reshape(-1), short.astype(jnp.int32).reshape(-1)


def _combine(starts, short, slot_t, y, x, mod, final_g, layer, final):
    grid_spec = pltpu.PrefetchScalarGridSpec(
        num_scalar_prefetch=2,
        grid=(BATCH, SEQ // COMBINE_ROWS),
        in_specs=[
            pl.BlockSpec((1, COMBINE_ROWS, N_EXPERTS), lambda b, t, s, f: (b, t, 0)),
            pl.BlockSpec((N_EXPERTS, 1, CAP, D_MODEL), lambda b, t, s, f: (0, b, 0, 0)),
            pl.BlockSpec((1, COMBINE_ROWS, D_MODEL), lambda b, t, s, f: (b, t, 0)),
            pl.BlockSpec((1, 1, D_MODEL), lambda b, t, s, f: ((layer * 6 + 5) * BATCH + b, 0, 0)),
            pl.BlockSpec((1, D_MODEL), lambda b, t, s, f: (0, 0)),
        ],
        out_specs=pl.BlockSpec((1, COMBINE_ROWS, D_MODEL), lambda b, t, s, f: (b, t, 0)),
    )
    return pl.pallas_call(
        functools.partial(_combine_kernel, final=final),
        grid_spec=grid_spec,
        out_shape=jax.ShapeDtypeStruct((BATCH, SEQ, D_MODEL), F32),
        compiler_params=_params(("arbitrary", "arbitrary"), 48),
        name="moe_combine",
    )(starts, short, slot_t, y, x, mod, final_g.reshape(1, D_MODEL))


def _scaled_q_weight(w, q_columns):
    scale = jnp.concatenate([jnp.full((q_columns,), Q_SCALE, F32), jnp.ones((w.shape[1] - q_columns,), F32)])
    return (w * scale).astype(BF16)


def kernel(x, c, ada_w, ada_b, norm_g, na_w_qkv, na_w_o, na_rpb, sw_w_qkv, sw_w_o, sw_sinks, t5_bias,
           moe_w_router, moe_w_gate, moe_w_up, moe_w_down, final_g):
    mod = _ada(c, ada_w, ada_b)
    norm_g3 = norm_g.reshape(DEPTH * 2, 1, D_MODEL)
    x = x.reshape(BATCH * SEQ, D_MODEL)
    for layer in range(DEPTH):
        j = layer // N_MIXERS
        if layer % N_MIXERS == 0:
            qkv = _qkv(x, norm_g3, mod, _scaled_q_weight(na_w_qkv[j], NA_HEADS * HEAD_DIM), layer)
            o = _na_attention(qkv.reshape(BATCH, SEQ, -1), _na_rpb_rows(na_rpb[j]))
            w_o = na_w_o[j]
        else:
            qkv = _qkv(x, norm_g3, mod, _scaled_q_weight(sw_w_qkv[j], SW_Q_HEADS * HEAD_DIM), layer,
                       head_copies=2 * SW_KV_HEADS * HEAD_DIM // LANES)
            o = _sw_attention(qkv.reshape(BATCH, SEQ, -1), _sw_rel_table(t5_bias), sw_sinks[j])
            w_o = sw_w_o[j]
        w_router = moe_w_router[layer].astype(BF16)
        x, h, aff_t = _post_attn(o.reshape(BATCH * SEQ, -1), w_o.astype(BF16), x, norm_g3, mod,
                                 w_router.T, layer)
        slot, tok, gate, first = _route(aff_t)
        tok = tok.reshape(BATCH, N_EXPERTS, CAP)
        rows = tok.transpose(1, 0, 2) + (jnp.arange(BATCH, dtype=jnp.int32) * SEQ)[None, :, None]
        gates = gate.reshape(BATCH, N_EXPERTS, CAP).transpose(1, 2, 0)
        y = _experts(rows.reshape(-1), h, gates, moe_w_gate, moe_w_up, moe_w_down, layer)
        slot_t = slot.reshape(BATCH, N_EXPERTS, SEQ).transpose(0, 2, 1)
        starts, short = _combine_windows(first.reshape(BATCH, N_EXPERTS, LANES)[..., :SEQ // COMBINE_ROWS])
        x = _combine(starts, short, slot_t, y, x.reshape(BATCH, SEQ, D_MODEL), mod, final_g, layer,
                     final=layer == DEPTH - 1).reshape(BATCH * SEQ, D_MODEL)
    return x.reshape(BATCH, SEQ, D_MODEL)
```

```python
import functools

import numpy as np
import jax
import jax.numpy as jnp
from jax import lax
from jax.experimental import pallas as pl
from jax.experimental.pallas import tpu as pltpu

D_MODEL = 1024
BATCH = 8
SEQ = 2048
DEPTH = 2
GRID_W = 64
ROWS = SEQ // GRID_W
N_MIXERS = 2
HEAD_DIM = 64
NA_HEADS = 16
NA_WIN_H = 8
NA_WIN_W = 16
SW_Q_HEADS = 16
SW_KV_HEADS = 4
SW_GROUP = SW_Q_HEADS // SW_KV_HEADS
SW_WINDOW = 128
SW_BLOCK = 128
SW_NB = SEQ // SW_BLOCK
SW_SPAN = 3 * SW_BLOCK
T5_BUCKETS = 32
T5_MAX_DIST = 128
N_EXPERTS = 16
EXPERT_FF = 2048
EC_CAPACITY = 2
CAP = EC_CAPACITY * SEQ // N_EXPERTS
RMS_EPS = 1e-6
NEG = -1e30
LOG2E = 1.4426950408889634
Q_SCALE = HEAD_DIM ** -0.5 * LOG2E

LANES = 128
TOKEN_TILE_ROWS = D_MODEL // LANES
MIB = 1024 * 1024
F32 = jnp.float32
BF16 = jnp.bfloat16

ROW_TILE = 512
FF_TILE = 512
COMBINE_ROWS = 512
PREFIX_CHUNK = 256


def _params(semantics, vmem_mib):
    return pltpu.CompilerParams(dimension_semantics=semantics, vmem_limit_bytes=vmem_mib * MIB)


def _norm_mod(x, g, sc, sh):
    y = x * lax.rsqrt(jnp.mean(x * x, axis=-1, keepdims=True) + RMS_EPS)
    return (y * g) * (1.0 + sc) + sh


def _softmax0(z):
    z = z - jnp.max(z, axis=0, keepdims=True)
    p = jnp.exp(z)
    return p / jnp.sum(p, axis=0, keepdims=True)


def _half_masks(rows):
    lane = lax.broadcasted_iota(jnp.int32, (rows, LANES), 1)
    lo = jnp.where(lane < HEAD_DIM, 1.0, 0.0).astype(BF16)
    hi = jnp.where(lane < HEAD_DIM, 0.0, 1.0).astype(BF16)
    return lo, hi


def _ada_kernel(c_ref, w_ref, b_ref, o_ref):
    c = c_ref[...]
    act = (c * jax.nn.sigmoid(c)).astype(BF16)
    o_ref[0] = jnp.dot(act, w_ref[0].astype(BF16), preferred_element_type=F32) + b_ref[0]


def _ada(c, ada_w, ada_b):
    out = pl.pallas_call(
        _ada_kernel,
        grid=(DEPTH, 6),
        in_specs=[
            pl.BlockSpec((BATCH, D_MODEL), lambda l, k: (0, 0)),
            pl.BlockSpec((1, D_MODEL, D_MODEL), lambda l, k: (l, 0, k)),
            pl.BlockSpec((1, 1, D_MODEL), lambda l, k: (l * 6 + k, 0, 0)),
        ],
        out_specs=pl.BlockSpec((1, BATCH, D_MODEL), lambda l, k: (l * 6 + k, 0, 0)),
        out_shape=jax.ShapeDtypeStruct((DEPTH * 6, BATCH, D_MODEL), F32),
        compiler_params=_params(("arbitrary", "arbitrary"), 32),
        name="ada_mod",
    )(c, ada_w, ada_b.reshape(DEPTH * 6, 1, D_MODEL))
    return out.reshape(DEPTH * 6 * BATCH, 1, D_MODEL)


def _mod_spec(layer, chunk):
    tiles_per_seq = SEQ // ROW_TILE
    return pl.BlockSpec((1, 1, D_MODEL),
                        lambda i: ((layer * 6 + chunk) * BATCH + i // tiles_per_seq, 0, 0))


def _qkv_kernel(x_ref, g_ref, sc_ref, sh_ref, w_ref, o_ref, *, head_copies):
    h = _norm_mod(x_ref[...], g_ref[0], sc_ref[0], sh_ref[0])
    y = jnp.dot(h.astype(BF16), w_ref[...], preferred_element_type=F32)
    if head_copies:
        low_half = lax.broadcasted_iota(jnp.int32, (ROW_TILE, LANES), 1) < HEAD_DIM
        keep = y.shape[1] - head_copies * LANES
        tiles = [y[:, :keep]]
        for t in range(head_copies):
            pair = y[:, keep + t * LANES:keep + (t + 1) * LANES]
            swapped = pltpu.roll(pair, HEAD_DIM, 1)
            tiles += [jnp.where(low_half, pair, swapped), jnp.where(low_half, swapped, pair)]
        y = jnp.concatenate(tiles, axis=1)
    o_ref[...] = y.astype(BF16)


def _qkv(x, norm_g3, mod, w, layer, head_copies=0):
    n = w.shape[1] + head_copies * LANES
    return pl.pallas_call(
        functools.partial(_qkv_kernel, head_copies=head_copies),
        grid=(BATCH * SEQ // ROW_TILE,),
        in_specs=[
            pl.BlockSpec((ROW_TILE, D_MODEL), lambda i: (i, 0)),
            pl.BlockSpec((1, 1, D_MODEL), lambda i: (layer * 2, 0, 0)),
            _mod_spec(layer, 1),
            _mod_spec(layer, 0),
            pl.BlockSpec(w.shape, lambda i: (0, 0)),
        ],
        out_specs=pl.BlockSpec((ROW_TILE, n), lambda i: (i, 0)),
        out_shape=jax.ShapeDtypeStruct((BATCH * SEQ, n), BF16),
        compiler_params=_params(("arbitrary",), 48),
        name="norm_qkv",
    )(x, norm_g3, mod, mod, w)


NA_BIAS_ROWS = 2 * NA_WIN_H - 1
NA_ROW_UNROLL = 4


def _na_rpb_rows(rpb):
    w = NA_WIN_W - 1
    rpb = rpb.astype(F32) * LOG2E
    pad = jnp.zeros(rpb.shape[:2] + (LANES - 2 * w - 1,), F32)
    return jnp.concatenate([rpb[..., w:], pad, rpb[..., :w]], axis=-1)


def _na_build_bias(w_ref, bias_ref):
    lane = lax.broadcasted_iota(jnp.int32, (GRID_W, LANES), 1)
    col = lax.broadcasted_iota(jnp.int32, (GRID_W, LANES), 0)
    kc = jnp.bitwise_and(lane, GRID_W - 1)
    cstart = jnp.clip(col - NA_WIN_W // 2, 0, GRID_W - NA_WIN_W)
    in_window = jnp.where(kc >= cstart, jnp.where(kc < cstart + NA_WIN_W, 1.0, 0.0), 0.0) > 0.5
    low_half = lane < GRID_W

    def toeplitz(head, a, shift):
        row = jnp.broadcast_to(w_ref[head, a:a + 1, :], (GRID_W, LANES))
        return pltpu.roll(row, shift, 1, stride=1, stride_axis=0)

    for head in range(2):
        tiles = [jnp.where(in_window, jnp.where(low_half, toeplitz(head, a, 0), toeplitz(head, a + 1, GRID_W)), NEG)
                 for a in range(NA_BIAS_ROWS - 1)]
        for d in range(NA_WIN_H):
            for m in range(NA_WIN_H // 2):
                bias_ref[d, head * GRID_W:(head + 1) * GRID_W, m * LANES:(m + 1) * LANES] = (
                    tiles[NA_WIN_H - 1 - d + 2 * m])


def _na_kernel(w_ref, q_ref, k_ref, v_ref, o_ref, bias_ref, s0_ref, s1_ref, p0_ref, p1_ref):
    @pl.when(pl.program_id(1) == 0)
    def _():
        _na_build_bias(w_ref, bias_ref)

    lo, hi = _half_masks(GRID_W)
    lo_f = lax.broadcasted_iota(jnp.int32, (GRID_W, LANES), 1) < HEAD_DIM
    span = NA_WIN_H * GRID_W
    ones = jnp.ones((span, LANES), BF16)
    groups = ROWS // NA_ROW_UNROLL

    def group_rows(i):
        for u in range(NA_ROW_UNROLL):
            r = jnp.clip(i, 0, groups - 1) * NA_ROW_UNROLL + u
            yield u, r, jnp.clip(r - NA_WIN_H // 2, 0, ROWS - NA_WIN_H)

    def scores(i, s_ref):
        for u, r, rs in group_rows(i):
            q = q_ref[0, pl.ds(pl.multiple_of(r * GRID_W, GRID_W), GRID_W), :]
            lhs = jnp.concatenate([q * lo, q * hi], axis=0)
            kw = k_ref[0, pl.ds(pl.multiple_of(rs * GRID_W, GRID_W), span), :]
            s_ref[u] = lax.dot_general(lhs, kw, (((1,), (1,)), ((), ())), preferred_element_type=F32)

    def softmax(i, s_ref, p_ref):
        for u, r, rs in group_rows(i):
            s = s_ref[u] + bias_ref[r - rs]
            m = jnp.broadcast_to(jnp.max(s, axis=-1, keepdims=True), (2 * GRID_W, LANES))
            p_ref[u] = jnp.exp2(s - jnp.concatenate([m] * (span // LANES), axis=1)).astype(BF16)

    def values(i, p_ref):
        for u, r, rs in group_rows(i):
            vw = v_ref[0, pl.ds(pl.multiple_of(rs * GRID_W, GRID_W), span), :]
            o = jnp.dot(p_ref[u], jnp.concatenate([vw, ones], axis=1), preferred_element_type=F32)
            o = o[:, :LANES] / o[:, LANES:]
            out = jnp.where(lo_f, o[:GRID_W], o[GRID_W:])
            o_ref[0, pl.ds(pl.multiple_of(r * GRID_W, GRID_W), GRID_W), :] = out.astype(BF16)

    def step(i, carry):
        values(2 * i - 1, p1_ref)
        scores(2 * i + 1, s1_ref)
        softmax(2 * i, s0_ref, p0_ref)
        values(2 * i, p0_ref)
        scores(2 * i + 2, s0_ref)
        softmax(2 * i + 1, s1_ref, p1_ref)
        return carry

    p1_ref[...] = jnp.ones_like(p1_ref)
    scores(0, s0_ref)
    lax.fori_loop(0, groups // 2, step, 0)
    values(groups - 1, p1_ref)


def _na_attention(qkv, rpb_rows):
    pairs = NA_HEADS // 2
    return pl.pallas_call(
        _na_kernel,
        grid=(pairs, BATCH),
        in_specs=[
            pl.BlockSpec((2, NA_BIAS_ROWS, LANES), lambda p, b: (p, 0, 0)),
            pl.BlockSpec((1, SEQ, LANES), lambda p, b: (b, 0, p)),
            pl.BlockSpec((1, SEQ, LANES), lambda p, b: (b, 0, pairs + p)),
            pl.BlockSpec((1, SEQ, LANES), lambda p, b: (b, 0, 2 * pairs + p)),
        ],
        out_specs=pl.BlockSpec((1, SEQ, LANES), lambda p, b: (b, 0, p)),
        out_shape=jax.ShapeDtypeStruct((BATCH, SEQ, NA_HEADS * HEAD_DIM), BF16),
        scratch_shapes=[pltpu.VMEM((NA_WIN_H, 2 * GRID_W, NA_WIN_H * GRID_W), F32),
                        pltpu.VMEM((NA_ROW_UNROLL, 2 * GRID_W, NA_WIN_H * GRID_W), F32),
                        pltpu.VMEM((NA_ROW_UNROLL, 2 * GRID_W, NA_WIN_H * GRID_W), F32),
                        pltpu.VMEM((NA_ROW_UNROLL, 2 * GRID_W, NA_WIN_H * GRID_W), BF16),
                        pltpu.VMEM((NA_ROW_UNROLL, 2 * GRID_W, NA_WIN_H * GRID_W), BF16)],
        compiler_params=_params(("arbitrary", "arbitrary"), 32),
        name="na_attention",
    )(rpb_rows, qkv, qkv, qkv)


def _t5_buckets(rel):
    half = T5_BUCKETS // 2
    max_exact = half // 2
    n = np.abs(rel)
    large = max_exact + (np.log(np.maximum(n, 1) / max_exact)
                         / np.log(T5_MAX_DIST / max_exact) * (half - max_exact)).astype(np.int32)
    large = np.minimum(large, half - 1)
    return (rel > 0).astype(np.int32) * half + np.where(n < max_exact, n, large)


SW_REL_PERIOD = 512
SW_BLOCK_UNROLL = 2


def _sw_rel_table(t5_table):
    k = np.arange(SW_REL_PERIOD)
    rel = np.where(k < SW_REL_PERIOD // 2, k, k - SW_REL_PERIOD)
    ok = np.abs(rel) <= SW_WINDOW
    vals = jnp.where(ok[:, None], t5_table[_t5_buckets(rel)].astype(F32) * LOG2E, NEG)
    return vals.T.reshape(SW_KV_HEADS, SW_GROUP, SW_REL_PERIOD)


def _sw_build_bias(rel_ref, bias_ref):
    for g in range(SW_GROUP):
        row = jnp.broadcast_to(rel_ref[0, g:g + 1, :], (SW_BLOCK, SW_REL_PERIOD))
        t = pltpu.roll(row, 0, 1, stride=1, stride_axis=0)
        rows = slice(g * SW_BLOCK, (g + 1) * SW_BLOCK)
        bias_ref[0, rows, :] = t[:, :SW_SPAN]
        bias_ref[1, rows, :] = jnp.concatenate([t[:, SW_SPAN:], t[:, :2 * SW_BLOCK]], axis=1)
        bias_ref[2, rows, :] = jnp.concatenate([t[:, 2 * SW_BLOCK:], t[:, :SW_BLOCK]], axis=1)


def _sw_kernel(sink_ref, rel_ref, q_ref, k_ref, v_ref, o_ref, bias_ref, s0_ref, s1_ref, p0_ref, p1_ref,
               t0_ref, t1_ref):
    j = pl.program_id(0)
    ones = jnp.ones((SW_SPAN, LANES), BF16)

    @pl.when(pl.program_id(1) == 0)
    def _():
        _sw_build_bias(rel_ref, bias_ref)

    masks = _half_masks(SW_BLOCK)
    lo_f = lax.broadcasted_iota(jnp.int32, (SW_BLOCK, LANES), 1) < HEAD_DIM
    sinks = [jnp.full((SW_BLOCK, LANES), sink_ref[j * SW_GROUP + g] * LOG2E, F32) for g in range(SW_GROUP)]
    groups = SW_NB // SW_BLOCK_UNROLL

    def group_blocks(i):
        for u in range(SW_BLOCK_UNROLL):
            n = jnp.clip(i, 0, groups - 1) * SW_BLOCK_UNROLL + u
            yield u, n, jnp.clip(n - 1, 0, SW_NB - 3)

    def block_rows(n):
        return pl.ds(pl.multiple_of(n * SW_BLOCK, SW_BLOCK), SW_BLOCK)

    def scores(i, s_ref):
        for u, n, first in group_blocks(i):
            kw = k_ref[0, pl.ds(pl.multiple_of(first * SW_BLOCK, SW_BLOCK), SW_SPAN), :]
            for g in range(SW_GROUP):
                q = q_ref[0, block_rows(n), (g // 2) * LANES:(g // 2 + 1) * LANES] * masks[g % 2]
                s_ref[u * SW_GROUP + g] = lax.dot_general(q, kw, (((1,), (1,)), ((), ())),
                                                          preferred_element_type=F32)

    def softmax(i, s_ref, p_ref, t_ref):
        for u, n, first in group_blocks(i):
            for g in range(SW_GROUP):
                c = u * SW_GROUP + g
                s = s_ref[c] + bias_ref[n - first, g * SW_BLOCK:(g + 1) * SW_BLOCK, :]
                m = jnp.maximum(jnp.broadcast_to(jnp.max(s, axis=-1, keepdims=True), (SW_BLOCK, LANES)), sinks[g])
                p_ref[c] = jnp.exp2(s - jnp.concatenate([m] * (SW_SPAN // LANES), axis=1)).astype(BF16)
                t_ref[c] = jnp.exp2(sinks[g] - m)

    def values(i, p_ref, t_ref):
        for u, n, first in group_blocks(i):
            vw = v_ref[0, pl.ds(pl.multiple_of(first * SW_BLOCK, SW_BLOCK), SW_SPAN), :]
            v_ones = jnp.concatenate([vw, ones], axis=1)
            heads = []
            for g in range(SW_GROUP):
                c = u * SW_GROUP + g
                o = jnp.dot(p_ref[c], v_ones, preferred_element_type=F32)
                heads.append(o[:, :LANES] / (o[:, LANES:] + t_ref[c]))
            out = jnp.concatenate([jnp.where(lo_f, heads[0], heads[1]), jnp.where(lo_f, heads[2], heads[3])],
                                  axis=1)
            o_ref[0, block_rows(n), :] = out.astype(BF16)

    def step(i, carry):
        values(2 * i - 1, p1_ref, t1_ref)
        scores(2 * i + 1, s1_ref)
        softmax(2 * i, s0_ref, p0_ref, t0_ref)
        values(2 * i, p0_ref, t0_ref)
        scores(2 * i + 2, s0_ref)
        softmax(2 * i + 1, s1_ref, p1_ref, t1_ref)
        return carry

    p1_ref[...] = jnp.ones_like(p1_ref)
    t1_ref[...] = jnp.ones_like(t1_ref)
    scores(0, s0_ref)
    lax.fori_loop(0, groups // 2, step, 0)
    values(groups - 1, p1_ref, t1_ref)


def _sw_attention(qkv, rel_table, sinks):
    qw = SW_GROUP * HEAD_DIM
    k_off = SW_Q_HEADS * HEAD_DIM // LANES
    v_off = k_off + SW_KV_HEADS
    chains = SW_BLOCK_UNROLL * SW_GROUP
    return pl.pallas_call(
        _sw_kernel,
        grid=(SW_KV_HEADS, BATCH),
        in_specs=[
            pl.BlockSpec(memory_space=pltpu.SMEM),
            pl.BlockSpec((1, SW_GROUP, SW_REL_PERIOD), lambda j, b: (j, 0, 0)),
            pl.BlockSpec((1, SEQ, qw), lambda j, b: (b, 0, j)),
            pl.BlockSpec((1, SEQ, LANES), lambda j, b: (b, 0, k_off + j)),
            pl.BlockSpec((1, SEQ, LANES), lambda j, b: (b, 0, v_off + j)),
        ],
        out_specs=pl.BlockSpec((1, SEQ, qw), lambda j, b: (b, 0, j)),
        out_shape=jax.ShapeDtypeStruct((BATCH, SEQ, SW_Q_HEADS * HEAD_DIM), BF16),
        scratch_shapes=[pltpu.VMEM((3, SW_GROUP * SW_BLOCK, SW_SPAN), F32)]
        + [pltpu.VMEM((chains, SW_BLOCK, SW_SPAN), F32)] * 2
        + [pltpu.VMEM((chains, SW_BLOCK, SW_SPAN), BF16)] * 2
        + [pltpu.VMEM((chains, SW_BLOCK, LANES), F32)] * 2,
        compiler_params=_params(("arbitrary", "arbitrary"), 32),
        name="sw_attention",
    )(sinks, rel_table, qkv, qkv, qkv)


def _post_attn_kernel(o_ref, wo_ref, x_ref, gate_ref, g_ref, sc_ref, sh_ref, wr_ref,
                      xo_ref, h_ref, aff_ref):
    y = jnp.dot(o_ref[...], wo_ref[...], preferred_element_type=F32)
    xn = x_ref[...] + gate_ref[0] * y
    xo_ref[...] = xn
    h = _norm_mod(xn, g_ref[0], sc_ref[0], sh_ref[0])
    for j in range(TOKEN_TILE_ROWS):
        h_ref[pl.ds(j, ROW_TILE, stride=TOKEN_TILE_ROWS), :] = h[:, j * LANES:(j + 1) * LANES]
    hb = h.astype(BF16)
    logits = lax.dot_general(wr_ref[...], hb, (((1,), (1,)), ((), ())), preferred_element_type=F32)
    aff_ref[...] = _softmax0(logits)


def _post_attn(o, w_o, x, norm_g3, mod, w_router_t, layer):
    rows = BATCH * SEQ
    return pl.pallas_call(
        _post_attn_kernel,
        grid=(rows // ROW_TILE,),
        in_specs=[
            pl.BlockSpec((ROW_TILE, D_MODEL), lambda i: (i, 0)),
            pl.BlockSpec((D_MODEL, D_MODEL), lambda i: (0, 0)),
            pl.BlockSpec((ROW_TILE, D_MODEL), lambda i: (i, 0)),
            _mod_spec(layer, 2),
            pl.BlockSpec((1, 1, D_MODEL), lambda i: (layer * 2 + 1, 0, 0)),
            _mod_spec(layer, 4),
            _mod_spec(layer, 3),
            pl.BlockSpec((N_EXPERTS, D_MODEL), lambda i: (0, 0)),
        ],
        out_specs=[
            pl.BlockSpec((ROW_TILE, D_MODEL), lambda i: (i, 0)),
            pl.BlockSpec((ROW_TILE * TOKEN_TILE_ROWS, LANES), lambda i: (i, 0)),
            pl.BlockSpec((N_EXPERTS, ROW_TILE), lambda i: (0, i)),
        ],
        out_shape=[
            jax.ShapeDtypeStruct((rows, D_MODEL), F32),
            jax.ShapeDtypeStruct((rows * TOKEN_TILE_ROWS, LANES), F32),
            jax.ShapeDtypeStruct((N_EXPERTS, rows), F32),
        ],
        compiler_params=_params(("arbitrary",), 48),
        name="post_attn",
    )(o, w_o, x, mod, norm_g3, mod, mod, w_router_t)


ROUTE_SEQS = BATCH
ROUTE_ROWS = ROUTE_SEQS * N_EXPERTS
ROUTE_BITS = SEQ.bit_length() - 1
ROUTE_VALID = 1 << (2 * ROUTE_BITS)
F32_INF_BITS = 0x7F800000


def _prefix_count(x):
    rows = x.shape[0]
    nchunk = SEQ // PREFIX_CHUNK
    r = lax.broadcasted_iota(jnp.int32, (PREFIX_CHUNK, PREFIX_CHUNK), 0)
    c = lax.broadcasted_iota(jnp.int32, (PREFIX_CHUNK, PREFIX_CHUNK), 1)
    upper = jnp.where(r < c, 1.0, 0.0).astype(BF16)
    chunks = [x[:, k * PREFIX_CHUNK:(k + 1) * PREFIX_CHUNK] for k in range(nchunk)]
    local = jnp.dot(jnp.concatenate(chunks, axis=0).astype(BF16), upper, preferred_element_type=F32)
    out = []
    offset = jnp.zeros((rows, 1), F32)
    for k in range(nchunk):
        out.append(local[k * rows:(k + 1) * rows] + offset)
        offset = offset + jnp.sum(chunks[k], axis=1, keepdims=True)
    return jnp.concatenate(out, axis=1)


def _route_kernel(aff_ref, slot_ref, tok_ref, gate_ref, first_ref):
    aff = jnp.concatenate([aff_ref[:, b * SEQ:(b + 1) * SEQ] for b in range(ROUTE_SEQS)], axis=0)
    bits = pltpu.bitcast(aff, jnp.int32)

    def count(mask):
        return jnp.sum(jnp.where(mask, 1.0, 0.0), axis=1, keepdims=True)

    def search(_, bounds):
        lo, hi = bounds
        mid = lo + ((hi - lo) >> 1)
        ge = count(bits >= mid) >= CAP
        return jnp.where(ge, mid, lo), jnp.where(ge, hi, mid)

    lo0 = jnp.zeros((ROUTE_ROWS, 1), jnp.int32)
    hi0 = jnp.full((ROUTE_ROWS, 1), F32_INF_BITS, jnp.int32)
    tau, _ = lax.fori_loop(0, F32_INF_BITS.bit_length(), search, (lo0, hi0))
    gt = bits > tau
    eq = jnp.where(bits == tau, 1.0, 0.0)
    need = CAP - count(gt)
    sel = jnp.where(gt, 1.0, jnp.where(_prefix_count(eq) < need, eq, 0.0))
    pos = _prefix_count(sel).astype(jnp.int32)
    chosen = sel > 0.5
    slot_ref[...] = jnp.where(chosen, pos, -1)
    tile_lane = lax.broadcasted_iota(jnp.int32, (ROUTE_ROWS, LANES), 1)
    first = jnp.zeros((ROUTE_ROWS, LANES), jnp.int32)
    for k in range(SEQ // COMBINE_ROWS):
        first = jnp.where(tile_lane == k, pos[:, k * COMBINE_ROWS:k * COMBINE_ROWS + 1], first)
    first_ref[...] = first
    lane = lax.broadcasted_iota(jnp.int32, (ROUTE_ROWS, SEQ), 1)
    packed = jnp.where(chosen, ROUTE_VALID | (lane << ROUTE_BITS) | (lane - pos), 0)
    gate = aff
    for k in range(ROUTE_BITS):
        step = 1 << k
        from_right = pltpu.roll(packed, SEQ - step, 1)
        gate_right = pltpu.roll(gate, SEQ - step, 1)
        arrives = (from_right & step) != 0
        stays = (packed & step) == 0
        packed = jnp.where(arrives, from_right, jnp.where(stays, packed, 0))
        gate = jnp.where(arrives, gate_right, gate)
    tok_ref[...] = (packed[:, :CAP] >> ROUTE_BITS) & (SEQ - 1)
    gate_ref[...] = gate[:, :CAP]


def _route(aff_t):
    return pl.pallas_call(
        _route_kernel,
        grid=(BATCH // ROUTE_SEQS,),
        in_specs=[pl.BlockSpec((N_EXPERTS, ROUTE_SEQS * SEQ), lambda i: (0, i))],
        out_specs=[pl.BlockSpec((ROUTE_ROWS, SEQ), lambda i: (i, 0)),
                   pl.BlockSpec((ROUTE_ROWS, CAP), lambda i: (i, 0)),
                   pl.BlockSpec((ROUTE_ROWS, CAP), lambda i: (i, 0)),
                   pl.BlockSpec((ROUTE_ROWS, LANES), lambda i: (i, 0))],
        out_shape=[jax.ShapeDtypeStruct((BATCH * N_EXPERTS, SEQ), jnp.int32),
                   jax.ShapeDtypeStruct((BATCH * N_EXPERTS, CAP), jnp.int32),
                   jax.ShapeDtypeStruct((BATCH * N_EXPERTS, CAP), F32),
                   jax.ShapeDtypeStruct((BATCH * N_EXPERTS, LANES), jnp.int32)],
        compiler_params=_params(("arbitrary",), 48),
        name="route",
    )(aff_t)


GATHER_ROWS = BATCH * CAP
FF_STEPS = EXPERT_FF // FF_TILE
ROWS_PER_CHUNK = GATHER_ROWS // (FF_STEPS * BATCH)
PROLOGUE_UNROLL = 8


def _expert_kernel(tok_ref, h_ref, gates_ref, wg_ref, wu_ref, wd_ref, y_ref, x_ref, sem, xb_ref, acc_ref):
    e = pl.program_id(0)
    f = pl.program_id(1)
    slot = lax.rem(e, 2)
    next_slot = 1 - slot
    next_e = jnp.minimum(e + 1, N_EXPERTS - 1)

    tile = TOKEN_TILE_ROWS

    def row_copy(expert, row, dst_slot):
        src = pl.multiple_of(tok_ref[expert * GATHER_ROWS + row] * tile, tile)
        dst = pl.multiple_of(row * tile, tile)
        return pltpu.make_async_copy(h_ref.at[pl.ds(src, tile), :], x_ref.at[dst_slot, pl.ds(dst, tile), :],
                                     sem.at[dst_slot])

    def wait_rows(dst_slot):
        pltpu.make_async_copy(h_ref.at[pl.ds(0, GATHER_ROWS * tile), :], x_ref.at[dst_slot],
                              sem.at[dst_slot]).wait()

    @pl.when(jnp.logical_and(e == 0, f == 0))
    def _():
        def issue(i, carry):
            for u in range(PROLOGUE_UNROLL):
                row_copy(0, i * PROLOGUE_UNROLL + u, 0).start()
            return carry
        lax.fori_loop(0, GATHER_ROWS // PROLOGUE_UNROLL, issue, 0)
        acc_ref[...] = jnp.zeros_like(acc_ref)

    @pl.when(f == 0)
    def _():
        wait_rows(slot)
        for b in range(BATCH):
            cols = [x_ref[slot, pl.ds(b * CAP * tile + j, CAP, stride=tile), :] for j in range(tile)]
            xb_ref[b] = jnp.concatenate(cols, axis=1).astype(BF16)

    wg = wg_ref[0, 0].astype(BF16)
    wu = wu_ref[0, 0].astype(BF16)
    wd = wd_ref[0, 0].astype(BF16)
    for b in range(BATCH):
        first = (f * BATCH + b) * ROWS_PER_CHUNK
        for j in range(ROWS_PER_CHUNK):
            row_copy(next_e, first + j, next_slot).start()
        x = xb_ref[b]
        a = jnp.dot(x, wg, preferred_element_type=F32)
        u = jnp.dot(x, wu, preferred_element_type=F32)
        act = ((a * jax.nn.sigmoid(a)) * u).astype(BF16)
        carried = jnp.where(f == 0, 0.0, acc_ref[b])
        acc_ref[b] = carried + jnp.dot(act, wd, preferred_element_type=F32)

    @pl.when(f == FF_STEPS - 1)
    def _():
        gates = gates_ref[0]
        for b in range(BATCH):
            y_ref[0, b] = (acc_ref[b] * gates[:, b:b + 1]).astype(BF16)

    @pl.when(jnp.logical_and(e == N_EXPERTS - 1, f == FF_STEPS - 1))
    def _():
        wait_rows(next_slot)


def _experts(tok, h, gates, w_gate, w_up, w_down, layer):
    grid_spec = pltpu.PrefetchScalarGridSpec(
        num_scalar_prefetch=1,
        grid=(N_EXPERTS, FF_STEPS),
        in_specs=[
            pl.BlockSpec(memory_space=pl.ANY),
            pl.BlockSpec((1, CAP, BATCH), lambda e, f, tok: (e, 0, 0)),
            pl.BlockSpec((1, 1, D_MODEL, FF_TILE), lambda e, f, tok: (layer, e, 0, f)),
            pl.BlockSpec((1, 1, D_MODEL, FF_TILE), lambda e, f, tok: (layer, e, 0, f)),
            pl.BlockSpec((1, 1, FF_TILE, D_MODEL), lambda e, f, tok: (layer, e, f, 0)),
        ],
        out_specs=pl.BlockSpec((1, BATCH, CAP, D_MODEL), lambda e, f, tok: (e, 0, 0, 0)),
        scratch_shapes=[pltpu.VMEM((2, GATHER_ROWS * TOKEN_TILE_ROWS, LANES), F32), pltpu.SemaphoreType.DMA((2,)),
                        pltpu.VMEM((BATCH, CAP, D_MODEL), BF16), pltpu.VMEM((BATCH, CAP, D_MODEL), F32)],
    )
    return pl.pallas_call(
        _expert_kernel,
        grid_spec=grid_spec,
        out_shape=jax.ShapeDtypeStruct((N_EXPERTS, BATCH, CAP, D_MODEL), BF16),
        compiler_params=_params(("arbitrary", "arbitrary"), 60),
        name="moe_experts",
    )(tok, h, gates, w_gate, w_up, w_down)


COMBINE_WINDOW = 128
SLOT_ALIGN = 16


def _combine_kernel(start_ref, short_ref, slot_ref, y_ref, x_ref, gate_ref, g_ref, o_ref, *, final):
    tile = pl.program_id(0) * (SEQ // COMBINE_ROWS) + pl.program_id(1)
    slots = slot_ref[0]

    def finish(moe):
        x = x_ref[0] + gate_ref[0] * moe
        if final:
            x = (x * lax.rsqrt(jnp.mean(x * x, axis=-1, keepdims=True) + RMS_EPS)) * g_ref[...]
        o_ref[0] = x

    @pl.when(short_ref[tile] == 1)
    def _():
        lane = lax.broadcasted_iota(jnp.int32, (COMBINE_ROWS, COMBINE_WINDOW), 1)
        hits, rows = [], []
        for e in range(N_EXPERTS):
            start = pl.multiple_of(start_ref[tile * N_EXPERTS + e], SLOT_ALIGN)
            hits.append(jnp.where(slots[:, e:e + 1] - start == lane, 1.0, 0.0).astype(BF16))
            rows.append(y_ref[e, 0, pl.ds(start, COMBINE_WINDOW), :])
        finish(jnp.dot(jnp.concatenate(hits, axis=1), jnp.concatenate(rows, axis=0), preferred_element_type=F32))

    @pl.when(short_ref[tile] != 1)
    def _():
        lane = lax.broadcasted_iota(jnp.int32, (COMBINE_ROWS, CAP), 1)
        onehot = jnp.concatenate(
            [jnp.where(slots[:, e:e + 1] == lane, 1.0, 0.0).astype(BF16) for e in range(N_EXPERTS)], axis=1)
        finish(jnp.dot(onehot, y_ref[...].reshape(N_EXPERTS * CAP, D_MODEL), preferred_element_type=F32))


def _combine_windows(first):
    last = jnp.concatenate([first[..., 1:], jnp.full(first.shape[:2] + (1,), CAP, jnp.int32)], axis=-1)
    start = jnp.minimum(first // SLOT_ALIGN * SLOT_ALIGN, CAP - COMBINE_WINDOW)
    short = jnp.all(last <= start + COMBINE_WINDOW, axis=1)
    return start.transpose(0, 2, 1).reshape(-1), short.astype(jnp.int32).reshape(-1)


def _combine(starts, short, slot_t, y, x, mod, final_g, layer, final):
    grid_spec = pltpu.PrefetchScalarGridSpec(
        num_scalar_prefetch=2,
        grid=(BATCH, SEQ // COMBINE_ROWS),
        in_specs=[
            pl.BlockSpec((1, COMBINE_ROWS, N_EXPERTS), lambda b, t, s, f: (b, t, 0)),
            pl.BlockSpec((N_EXPERTS, 1, CAP, D_MODEL), lambda b, t, s, f: (0, b, 0, 0)),
            pl.BlockSpec((1, COMBINE_ROWS, D_MODEL), lambda b, t, s, f: (b, t, 0)),
            pl.BlockSpec((1, 1, D_MODEL), lambda b, t, s, f: ((layer * 6 + 5) * BATCH + b, 0, 0)),
            pl.BlockSpec((1, D_MODEL), lambda b, t, s, f: (0, 0)),
        ],
        out_specs=pl.BlockSpec((1, COMBINE_ROWS, D_MODEL), lambda b, t, s, f: (b, t, 0)),
    )
    return pl.pallas_call(
        functools.partial(_combine_kernel, final=final),
        grid_spec=grid_spec,
        out_shape=jax.ShapeDtypeStruct((BATCH, SEQ, D_MODEL), F32),
        compiler_params=_params(("arbitrary", "arbitrary"), 48),
        name="moe_combine",
    )(starts, short, slot_t, y, x, mod, final_g.reshape(1, D_MODEL))


def _scaled_q_weight(w, q_columns):
    scale = jnp.concatenate([jnp.full((q_columns,), Q_SCALE, F32), jnp.ones((w.shape[1] - q_columns,), F32)])
    return (w * scale).astype(BF16)


def kernel(x, c, ada_w, ada_b, norm_g, na_w_qkv, na_w_o, na_rpb, sw_w_qkv, sw_w_o, sw_sinks, t5_bias,
           moe_w_router, moe_w_gate, moe_w_up, moe_w_down, final_g):
    mod = _ada(c, ada_w, ada_b)
    norm_g3 = norm_g.reshape(DEPTH * 2, 1, D_MODEL)
    x = x.reshape(BATCH * SEQ, D_MODEL)
    for layer in range(DEPTH):
        j = layer // N_MIXERS
        if layer % N_MIXERS == 0:
            qkv = _qkv(x, norm_g3, mod, _scaled_q_weight(na_w_qkv[j], NA_HEADS * HEAD_DIM), layer)
            o = _na_attention(qkv.reshape(BATCH, SEQ, -1), _na_rpb_rows(na_rpb[j]))
            w_o = na_w_o[j]
        else:
            qkv = _qkv(x, norm_g3, mod, _scaled_q_weight(sw_w_qkv[j], SW_Q_HEADS * HEAD_DIM), layer,
                       head_copies=2 * SW_KV_HEADS * HEAD_DIM // LANES)
            o = _sw_attention(qkv.reshape(BATCH, SEQ, -1), _sw_rel_table(t5_bias), sw_sinks[j])
            w_o = sw_w_o[j]
        w_router = moe_w_router[layer].astype(BF16)
        x, h, aff_t = _post_attn(o.reshape(BATCH * SEQ, -1), w_o.astype(BF16), x, norm_g3, mod,
                                 w_router.T, layer)
        slot, tok, gate, first = _route(aff_t)
        tok = tok.reshape(BATCH, N_EXPERTS, CAP)
        rows = tok.transpose(1, 0, 2) + (jnp.arange(BATCH, dtype=jnp.int32) * SEQ)[None, :, None]
        gates = gate.reshape(BATCH, N_EXPERTS, CAP).transpose(1, 2, 0)
        y = _experts(rows.reshape(-1), h, gates, moe_w_gate, moe_w_up, moe_w_down, layer)
        slot_t = slot.reshape(BATCH, N_EXPERTS, SEQ).transpose(0, 2, 1)
        starts, short = _combine_windows(first.reshape(BATCH, N_EXPERTS, LANES)[..., :SEQ // COMBINE_ROWS])
        x = _combine(starts, short, slot_t, y, x.reshape(BATCH, SEQ, D_MODEL), mod, final_g, layer,
                     final=layer == DEPTH - 1).reshape(BATCH * SEQ, D_MODEL)
    return x.reshape(BATCH, SEQ, D_MODEL)
```

```python
import functools

import numpy as np
import jax
import jax.numpy as jnp
from jax import lax
from jax.experimental import pallas as pl
from jax.experimental.pallas import tpu as pltpu

D_MODEL = 1024
BATCH = 8
SEQ = 2048
DEPTH = 2
GRID_W = 64
ROWS = SEQ // GRID_W
N_MIXERS = 2
HEAD_DIM = 64
NA_HEADS = 16
NA_WIN_H = 8
NA_WIN_W = 16
SW_Q_HEADS = 16
SW_KV_HEADS = 4
SW_GROUP = SW_Q_HEADS // SW_KV_HEADS
SW_WINDOW = 128
SW_BLOCK = 128
SW_NB = SEQ // SW_BLOCK
SW_SPAN = 3 * SW_BLOCK
T5_BUCKETS = 32
T5_MAX_DIST = 128
N_EXPERTS = 16
EXPERT_FF = 2048
EC_CAPACITY = 2
CAP = EC_CAPACITY * SEQ // N_EXPERTS
RMS_EPS = 1e-6
NEG = -1e30
LOG2E = 1.4426950408889634
Q_SCALE = HEAD_DIM ** -0.5 * LOG2E

LANES = 128
TOKEN_TILE_ROWS = D_MODEL // LANES
MIB = 1024 * 1024
F32 = jnp.float32
BF16 = jnp.bfloat16

ROW_TILE = 512
POST_TILE = 1024
FF_TILE = 512
COMBINE_ROWS = 512
PREFIX_CHUNK = 256


def _params(semantics, vmem_mib):
    return pltpu.CompilerParams(dimension_semantics=semantics, vmem_limit_bytes=vmem_mib * MIB)


def _norm_mod(x, g, sc, sh):
    y = x * lax.rsqrt(jnp.mean(x * x, axis=-1, keepdims=True) + RMS_EPS)
    return (y * g) * (1.0 + sc) + sh


def _softmax0(z):
    z = z - jnp.max(z, axis=0, keepdims=True)
    p = jnp.exp(z)
    return p / jnp.sum(p, axis=0, keepdims=True)


def _half_masks(rows):
    lane = lax.broadcasted_iota(jnp.int32, (rows, LANES), 1)
    lo = jnp.where(lane < HEAD_DIM, 1.0, 0.0).astype(BF16)
    hi = jnp.where(lane < HEAD_DIM, 0.0, 1.0).astype(BF16)
    return lo, hi


def _ada_kernel(c_ref, w_ref, b_ref, o_ref):
    c = c_ref[...]
    act = (c * jax.nn.sigmoid(c)).astype(BF16)
    o_ref[0] = jnp.dot(act, w_ref[0].astype(BF16), preferred_element_type=F32) + b_ref[0]


def _ada(c, ada_w, ada_b):
    out = pl.pallas_call(
        _ada_kernel,
        grid=(DEPTH, 6),
        in_specs=[
            pl.BlockSpec((BATCH, D_MODEL), lambda l, k: (0, 0)),
            pl.BlockSpec((1, D_MODEL, D_MODEL), lambda l, k: (l, 0, k)),
            pl.BlockSpec((1, 1, D_MODEL), lambda l, k: (l * 6 + k, 0, 0)),
        ],
        out_specs=pl.BlockSpec((1, BATCH, D_MODEL), lambda l, k: (l * 6 + k, 0, 0)),
        out_shape=jax.ShapeDtypeStruct((DEPTH * 6, BATCH, D_MODEL), F32),
        compiler_params=_params(("arbitrary", "arbitrary"), 32),
        name="ada_mod",
    )(c, ada_w, ada_b.reshape(DEPTH * 6, 1, D_MODEL))
    return out.reshape(DEPTH * 6 * BATCH, 1, D_MODEL)


def _mod_spec(layer, chunk, row_tile=ROW_TILE):
    tiles_per_seq = SEQ // row_tile
    return pl.BlockSpec((1, 1, D_MODEL),
                        lambda i: ((layer * 6 + chunk) * BATCH + i // tiles_per_seq, 0, 0))


def _qkv_kernel(x_ref, g_ref, sc_ref, sh_ref, w_ref, o_ref, *, head_copies):
    h = _norm_mod(x_ref[...], g_ref[0], sc_ref[0], sh_ref[0])
    y = jnp.dot(h.astype(BF16), w_ref[...], preferred_element_type=F32)
    if head_copies:
        low_half = lax.broadcasted_iota(jnp.int32, (ROW_TILE, LANES), 1) < HEAD_DIM
        keep = y.shape[1] - head_copies * LANES
        tiles = [y[:, :keep]]
        for t in range(head_copies):
            pair = y[:, keep + t * LANES:keep + (t + 1) * LANES]
            swapped = pltpu.roll(pair, HEAD_DIM, 1)
            tiles += [jnp.where(low_half, pair, swapped), jnp.where(low_half, swapped, pair)]
        y = jnp.concatenate(tiles, axis=1)
    o_ref[...] = y.astype(BF16)


def _qkv(x, norm_g3, mod, w, layer, head_copies=0):
    n = w.shape[1] + head_copies * LANES
    return pl.pallas_call(
        functools.partial(_qkv_kernel, head_copies=head_copies),
        grid=(BATCH * SEQ // ROW_TILE,),
        in_specs=[
            pl.BlockSpec((ROW_TILE, D_MODEL), lambda i: (i, 0)),
            pl.BlockSpec((1, 1, D_MODEL), lambda i: (layer * 2, 0, 0)),
            _mod_spec(layer, 1),
            _mod_spec(layer, 0),
            pl.BlockSpec(w.shape, lambda i: (0, 0)),
        ],
        out_specs=pl.BlockSpec((ROW_TILE, n), lambda i: (i, 0)),
        out_shape=jax.ShapeDtypeStruct((BATCH * SEQ, n), BF16),
        compiler_params=_params(("arbitrary",), 48),
        name="norm_qkv",
    )(x, norm_g3, mod, mod, w)


NA_BIAS_ROWS = 2 * NA_WIN_H - 1
NA_ROW_UNROLL = 4


def _na_rpb_rows(rpb):
    w = NA_WIN_W - 1
    rpb = rpb.astype(F32) * LOG2E
    pad = jnp.zeros(rpb.shape[:2] + (LANES - 2 * w - 1,), F32)
    return jnp.concatenate([rpb[..., w:], pad, rpb[..., :w]], axis=-1)


def _na_build_bias(w_ref, bias_ref):
    lane = lax.broadcasted_iota(jnp.int32, (GRID_W, LANES), 1)
    col = lax.broadcasted_iota(jnp.int32, (GRID_W, LANES), 0)
    kc = jnp.bitwise_and(lane, GRID_W - 1)
    cstart = jnp.clip(col - NA_WIN_W // 2, 0, GRID_W - NA_WIN_W)
    in_window = jnp.where(kc >= cstart, jnp.where(kc < cstart + NA_WIN_W, 1.0, 0.0), 0.0) > 0.5
    low_half = lane < GRID_W

    def toeplitz(head, a, shift):
        row = jnp.broadcast_to(w_ref[head, a:a + 1, :], (GRID_W, LANES))
        return pltpu.roll(row, shift, 1, stride=1, stride_axis=0)

    for head in range(2):
        tiles = [jnp.where(in_window, jnp.where(low_half, toeplitz(head, a, 0), toeplitz(head, a + 1, GRID_W)), NEG)
                 for a in range(NA_BIAS_ROWS - 1)]
        for d in range(NA_WIN_H):
            for m in range(NA_WIN_H // 2):
                bias_ref[d, head * GRID_W:(head + 1) * GRID_W, m * LANES:(m + 1) * LANES] = (
                    tiles[NA_WIN_H - 1 - d + 2 * m])


def _na_kernel(w_ref, q_ref, k_ref, v_ref, o_ref, bias_ref, s0_ref, s1_ref, p0_ref, p1_ref):
    @pl.when(pl.program_id(1) == 0)
    def _():
        _na_build_bias(w_ref, bias_ref)

    lo, hi = _half_masks(GRID_W)
    lo_f = lax.broadcasted_iota(jnp.int32, (GRID_W, LANES), 1) < HEAD_DIM
    span = NA_WIN_H * GRID_W
    ones = jnp.ones((span, LANES), BF16)
    groups = ROWS // NA_ROW_UNROLL

    def group_rows(i):
        for u in range(NA_ROW_UNROLL):
            r = jnp.clip(i, 0, groups - 1) * NA_ROW_UNROLL + u
            yield u, r, jnp.clip(r - NA_WIN_H // 2, 0, ROWS - NA_WIN_H)

    def scores(i, s_ref):
        for u, r, rs in group_rows(i):
            q = q_ref[0, pl.ds(pl.multiple_of(r * GRID_W, GRID_W), GRID_W), :]
            lhs = jnp.concatenate([q * lo, q * hi], axis=0)
            kw = k_ref[0, pl.ds(pl.multiple_of(rs * GRID_W, GRID_W), span), :]
            s_ref[u] = lax.dot_general(lhs, kw, (((1,), (1,)), ((), ())), preferred_element_type=F32)

    def softmax(i, s_ref, p_ref):
        for u, r, rs in group_rows(i):
            s = s_ref[u] + bias_ref[r - rs]
            m = jnp.broadcast_to(jnp.max(s, axis=-1, keepdims=True), (2 * GRID_W, LANES))
            p_ref[u] = jnp.exp2(s - jnp.concatenate([m] * (span // LANES), axis=1)).astype(BF16)

    def values(i, p_ref):
        for u, r, rs in group_rows(i):
            vw = v_ref[0, pl.ds(pl.multiple_of(rs * GRID_W, GRID_W), span), :]
            o = jnp.dot(p_ref[u], jnp.concatenate([vw, ones], axis=1), preferred_element_type=F32)
            o = o[:, :LANES] / o[:, LANES:]
            out = jnp.where(lo_f, o[:GRID_W], o[GRID_W:])
            o_ref[0, pl.ds(pl.multiple_of(r * GRID_W, GRID_W), GRID_W), :] = out.astype(BF16)

    def step(i, carry):
        values(2 * i - 1, p1_ref)
        scores(2 * i + 1, s1_ref)
        softmax(2 * i, s0_ref, p0_ref)
        values(2 * i, p0_ref)
        scores(2 * i + 2, s0_ref)
        softmax(2 * i + 1, s1_ref, p1_ref)
        return carry

    p1_ref[...] = jnp.ones_like(p1_ref)
    scores(0, s0_ref)
    lax.fori_loop(0, groups // 2, step, 0)
    values(groups - 1, p1_ref)


def _na_attention(qkv, rpb_rows):
    pairs = NA_HEADS // 2
    return pl.pallas_call(
        _na_kernel,
        grid=(pairs, BATCH),
        in_specs=[
            pl.BlockSpec((2, NA_BIAS_ROWS, LANES), lambda p, b: (p, 0, 0)),
            pl.BlockSpec((1, SEQ, LANES), lambda p, b: (b, 0, p)),
            pl.BlockSpec((1, SEQ, LANES), lambda p, b: (b, 0, pairs + p)),
            pl.BlockSpec((1, SEQ, LANES), lambda p, b: (b, 0, 2 * pairs + p)),
        ],
        out_specs=pl.BlockSpec((1, SEQ, LANES), lambda p, b: (b, 0, p)),
        out_shape=jax.ShapeDtypeStruct((BATCH, SEQ, NA_HEADS * HEAD_DIM), BF16),
        scratch_shapes=[pltpu.VMEM((NA_WIN_H, 2 * GRID_W, NA_WIN_H * GRID_W), F32),
                        pltpu.VMEM((NA_ROW_UNROLL, 2 * GRID_W, NA_WIN_H * GRID_W), F32),
                        pltpu.VMEM((NA_ROW_UNROLL, 2 * GRID_W, NA_WIN_H * GRID_W), F32),
                        pltpu.VMEM((NA_ROW_UNROLL, 2 * GRID_W, NA_WIN_H * GRID_W), BF16),
                        pltpu.VMEM((NA_ROW_UNROLL, 2 * GRID_W, NA_WIN_H * GRID_W), BF16)],
        compiler_params=_params(("arbitrary", "arbitrary"), 32),
        name="na_attention",
    )(rpb_rows, qkv, qkv, qkv)


def _t5_buckets(rel):
    half = T5_BUCKETS // 2
    max_exact = half // 2
    n = np.abs(rel)
    large = max_exact + (np.log(np.maximum(n, 1) / max_exact)
                         / np.log(T5_MAX_DIST / max_exact) * (half - max_exact)).astype(np.int32)
    large = np.minimum(large, half - 1)
    return (rel > 0).astype(np.int32) * half + np.where(n < max_exact, n, large)


SW_REL_PERIOD = 512
SW_BLOCK_UNROLL = 2


def _sw_rel_table(t5_table):
    k = np.arange(SW_REL_PERIOD)
    rel = np.where(k < SW_REL_PERIOD // 2, k, k - SW_REL_PERIOD)
    ok = np.abs(rel) <= SW_WINDOW
    vals = jnp.where(ok[:, None], t5_table[_t5_buckets(rel)].astype(F32) * LOG2E, NEG)
    return vals.T.reshape(SW_KV_HEADS, SW_GROUP, SW_REL_PERIOD)


def _sw_build_bias(rel_ref, bias_ref):
    for g in range(SW_GROUP):
        row = jnp.broadcast_to(rel_ref[0, g:g + 1, :], (SW_BLOCK, SW_REL_PERIOD))
        t = pltpu.roll(row, 0, 1, stride=1, stride_axis=0)
        rows = slice(g * SW_BLOCK, (g + 1) * SW_BLOCK)
        bias_ref[0, rows, :] = t[:, :SW_SPAN]
        bias_ref[1, rows, :] = jnp.concatenate([t[:, SW_SPAN:], t[:, :2 * SW_BLOCK]], axis=1)
        bias_ref[2, rows, :] = jnp.concatenate([t[:, 2 * SW_BLOCK:], t[:, :SW_BLOCK]], axis=1)


def _sw_kernel(sink_ref, rel_ref, q_ref, k_ref, v_ref, o_ref, bias_ref, s0_ref, s1_ref, p0_ref, p1_ref,
               t0_ref, t1_ref):
    j = pl.program_id(0)
    ones = jnp.ones((SW_SPAN, LANES), BF16)

    @pl.when(pl.program_id(1) == 0)
    def _():
        _sw_build_bias(rel_ref, bias_ref)

    masks = _half_masks(SW_BLOCK)
    lo_f = lax.broadcasted_iota(jnp.int32, (SW_BLOCK, LANES), 1) < HEAD_DIM
    sinks = [jnp.full((SW_BLOCK, LANES), sink_ref[j * SW_GROUP + g] * LOG2E, F32) for g in range(SW_GROUP)]
    groups = SW_NB // SW_BLOCK_UNROLL

    def group_blocks(i):
        for u in range(SW_BLOCK_UNROLL):
            n = jnp.clip(i, 0, groups - 1) * SW_BLOCK_UNROLL + u
            yield u, n, jnp.clip(n - 1, 0, SW_NB - 3)

    def block_rows(n):
        return pl.ds(pl.multiple_of(n * SW_BLOCK, SW_BLOCK), SW_BLOCK)

    def scores(i, s_ref):
        for u, n, first in group_blocks(i):
            kw = k_ref[0, pl.ds(pl.multiple_of(first * SW_BLOCK, SW_BLOCK), SW_SPAN), :]
            for g in range(SW_GROUP):
                q = q_ref[0, block_rows(n), (g // 2) * LANES:(g // 2 + 1) * LANES] * masks[g % 2]
                s_ref[u * SW_GROUP + g] = lax.dot_general(q, kw, (((1,), (1,)), ((), ())),
                                                          preferred_element_type=F32)

    def softmax(i, s_ref, p_ref, t_ref):
        for u, n, first in group_blocks(i):
            for g in range(SW_GROUP):
                c = u * SW_GROUP + g
                s = s_ref[c] + bias_ref[n - first, g * SW_BLOCK:(g + 1) * SW_BLOCK, :]
                m = jnp.maximum(jnp.broadcast_to(jnp.max(s, axis=-1, keepdims=True), (SW_BLOCK, LANES)), sinks[g])
                p_ref[c] = jnp.exp2(s - jnp.concatenate([m] * (SW_SPAN // LANES), axis=1)).astype(BF16)
                t_ref[c] = jnp.exp2(sinks[g] - m)

    def values(i, p_ref, t_ref):
        for u, n, first in group_blocks(i):
            vw = v_ref[0, pl.ds(pl.multiple_of(first * SW_BLOCK, SW_BLOCK), SW_SPAN), :]
            v_ones = jnp.concatenate([vw, ones], axis=1)
            heads = []
            for g in range(SW_GROUP):
                c = u * SW_GROUP + g
                o = jnp.dot(p_ref[c], v_ones, preferred_element_type=F32)
                heads.append(o[:, :LANES] / (o[:, LANES:] + t_ref[c]))
            out = jnp.concatenate([jnp.where(lo_f, heads[0], heads[1]), jnp.where(lo_f, heads[2], heads[3])],
                                  axis=1)
            o_ref[0, block_rows(n), :] = out.astype(BF16)

    def step(i, carry):
        values(2 * i - 1, p1_ref, t1_ref)
        scores(2 * i + 1, s1_ref)
        softmax(2 * i, s0_ref, p0_ref, t0_ref)
        values(2 * i, p0_ref, t0_ref)
        scores(2 * i + 2, s0_ref)
        softmax(2 * i + 1, s1_ref, p1_ref, t1_ref)
        return carry

    p1_ref[...] = jnp.ones_like(p1_ref)
    t1_ref[...] = jnp.ones_like(t1_ref)
    scores(0, s0_ref)
    lax.fori_loop(0, groups // 2, step, 0)
    values(groups - 1, p1_ref, t1_ref)


def _sw_attention(qkv, rel_table, sinks):
    qw = SW_GROUP * HEAD_DIM
    k_off = SW_Q_HEADS * HEAD_DIM // LANES
    v_off = k_off + SW_KV_HEADS
    chains = SW_BLOCK_UNROLL * SW_GROUP
    return pl.pallas_call(
        _sw_kernel,
        grid=(SW_KV_HEADS, BATCH),
        in_specs=[
            pl.BlockSpec(memory_space=pltpu.SMEM),
            pl.BlockSpec((1, SW_GROUP, SW_REL_PERIOD), lambda j, b: (j, 0, 0)),
            pl.BlockSpec((1, SEQ, qw), lambda j, b: (b, 0, j)),
            pl.BlockSpec((1, SEQ, LANES), lambda j, b: (b, 0, k_off + j)),
            pl.BlockSpec((1, SEQ, LANES), lambda j, b: (b, 0, v_off + j)),
        ],
        out_specs=pl.BlockSpec((1, SEQ, qw), lambda j, b: (b, 0, j)),
        out_shape=jax.ShapeDtypeStruct((BATCH, SEQ, SW_Q_HEADS * HEAD_DIM), BF16),
        scratch_shapes=[pltpu.VMEM((3, SW_GROUP * SW_BLOCK, SW_SPAN), F32)]
        + [pltpu.VMEM((chains, SW_BLOCK, SW_SPAN), F32)] * 2
        + [pltpu.VMEM((chains, SW_BLOCK, SW_SPAN), BF16)] * 2
        + [pltpu.VMEM((chains, SW_BLOCK, LANES), F32)] * 2,
        compiler_params=_params(("arbitrary", "arbitrary"), 32),
        name="sw_attention",
    )(sinks, rel_table, qkv, qkv, qkv)


def _post_attn_kernel(o_ref, wo_ref, x_ref, gate_ref, g_ref, sc_ref, sh_ref, wr_ref,
                      xo_ref, h_ref, aff_ref):
    y = jnp.dot(o_ref[...], wo_ref[...], preferred_element_type=F32)
    xn = x_ref[...] + gate_ref[0] * y
    xo_ref[...] = xn
    h = _norm_mod(xn, g_ref[0], sc_ref[0], sh_ref[0])
    for j in range(TOKEN_TILE_ROWS):
        h_ref[pl.ds(j, POST_TILE, stride=TOKEN_TILE_ROWS), :] = h[:, j * LANES:(j + 1) * LANES]
    hb = h.astype(BF16)
    logits = lax.dot_general(wr_ref[...], hb, (((1,), (1,)), ((), ())), preferred_element_type=F32)
    aff_ref[...] = _softmax0(logits)


def _post_attn(o, w_o, x, norm_g3, mod, w_router_t, layer):
    rows = BATCH * SEQ
    return pl.pallas_call(
        _post_attn_kernel,
        grid=(rows // POST_TILE,),
        in_specs=[
            pl.BlockSpec((POST_TILE, D_MODEL), lambda i: (i, 0)),
            pl.BlockSpec((D_MODEL, D_MODEL), lambda i: (0, 0)),
            pl.BlockSpec((POST_TILE, D_MODEL), lambda i: (i, 0)),
            _mod_spec(layer, 2, POST_TILE),
            pl.BlockSpec((1, 1, D_MODEL), lambda i: (layer * 2 + 1, 0, 0)),
            _mod_spec(layer, 4, POST_TILE),
            _mod_spec(layer, 3, POST_TILE),
            pl.BlockSpec((N_EXPERTS, D_MODEL), lambda i: (0, 0)),
        ],
        out_specs=[
            pl.BlockSpec((POST_TILE, D_MODEL), lambda i: (i, 0)),
            pl.BlockSpec((POST_TILE * TOKEN_TILE_ROWS, LANES), lambda i: (i, 0)),
            pl.BlockSpec((N_EXPERTS, POST_TILE), lambda i: (0, i)),
        ],
        out_shape=[
            jax.ShapeDtypeStruct((rows, D_MODEL), F32),
            jax.ShapeDtypeStruct((rows * TOKEN_TILE_ROWS, LANES), F32),
            jax.ShapeDtypeStruct((N_EXPERTS, rows), F32),
        ],
        compiler_params=_params(("arbitrary",), 56),
        name="post_attn",
    )(o, w_o, x, mod, norm_g3, mod, mod, w_router_t)


ROUTE_SEQS = BATCH
ROUTE_ROWS = ROUTE_SEQS * N_EXPERTS
ROUTE_BITS = SEQ.bit_length() - 1
ROUTE_VALID = 1 << (2 * ROUTE_BITS)
F32_INF_BITS = 0x7F800000


def _prefix_count(x):
    rows = x.shape[0]
    nchunk = SEQ // PREFIX_CHUNK
    r = lax.broadcasted_iota(jnp.int32, (PREFIX_CHUNK, PREFIX_CHUNK), 0)
    c = lax.broadcasted_iota(jnp.int32, (PREFIX_CHUNK, PREFIX_CHUNK), 1)
    upper = jnp.where(r < c, 1.0, 0.0).astype(BF16)
    chunks = [x[:, k * PREFIX_CHUNK:(k + 1) * PREFIX_CHUNK] for k in range(nchunk)]
    local = jnp.dot(jnp.concatenate(chunks, axis=0).astype(BF16), upper, preferred_element_type=F32)
    out = []
    offset = jnp.zeros((rows, 1), F32)
    for k in range(nchunk):
        out.append(local[k * rows:(k + 1) * rows] + offset)
        offset = offset + jnp.sum(chunks[k], axis=1, keepdims=True)
    return jnp.concatenate(out, axis=1)


def _route_kernel(aff_ref, slot_ref, tok_ref, gate_ref, first_ref):
    aff = jnp.concatenate([aff_ref[:, b * SEQ:(b + 1) * SEQ] for b in range(ROUTE_SEQS)], axis=0)
    bits = pltpu.bitcast(aff, jnp.int32)

    def count(mask):
        return jnp.sum(jnp.where(mask, 1.0, 0.0), axis=1, keepdims=True)

    def search(_, bounds):
        lo, hi = bounds
        mid = lo + ((hi - lo) >> 1)
        ge = count(bits >= mid) >= CAP
        return jnp.where(ge, mid, lo), jnp.where(ge, hi, mid)

    lo0 = jnp.zeros((ROUTE_ROWS, 1), jnp.int32)
    hi0 = jnp.full((ROUTE_ROWS, 1), F32_INF_BITS, jnp.int32)
    tau, _ = lax.fori_loop(0, F32_INF_BITS.bit_length(), search, (lo0, hi0))
    gt = bits > tau
    eq = jnp.where(bits == tau, 1.0, 0.0)
    need = CAP - count(gt)
    sel = jnp.where(gt, 1.0, jnp.where(_prefix_count(eq) < need, eq, 0.0))
    pos = _prefix_count(sel).astype(jnp.int32)
    chosen = sel > 0.5
    slot_ref[...] = jnp.where(chosen, pos, -1)
    tile_lane = lax.broadcasted_iota(jnp.int32, (ROUTE_ROWS, LANES), 1)
    first = jnp.zeros((ROUTE_ROWS, LANES), jnp.int32)
    for k in range(SEQ // COMBINE_ROWS):
        first = jnp.where(tile_lane == k, pos[:, k * COMBINE_ROWS:k * COMBINE_ROWS + 1], first)
    first_ref[...] = first
    lane = lax.broadcasted_iota(jnp.int32, (ROUTE_ROWS, SEQ), 1)
    packed = jnp.where(chosen, ROUTE_VALID | (lane << ROUTE_BITS) | (lane - pos), 0)
    gate = aff
    for k in range(ROUTE_BITS):
        step = 1 << k
        from_right = pltpu.roll(packed, SEQ - step, 1)
        gate_right = pltpu.roll(gate, SEQ - step, 1)
        arrives = (from_right & step) != 0
        stays = (packed & step) == 0
        packed = jnp.where(arrives, from_right, jnp.where(stays, packed, 0))
        gate = jnp.where(arrives, gate_right, gate)
    tok_ref[...] = (packed[:, :CAP] >> ROUTE_BITS) & (SEQ - 1)
    gate_ref[...] = gate[:, :CAP]


def _route(aff_t):
    return pl.pallas_call(
        _route_kernel,
        grid=(BATCH // ROUTE_SEQS,),
        in_specs=[pl.BlockSpec((N_EXPERTS, ROUTE_SEQS * SEQ), lambda i: (0, i))],
        out_specs=[pl.BlockSpec((ROUTE_ROWS, SEQ), lambda i: (i, 0)),
                   pl.BlockSpec((ROUTE_ROWS, CAP), lambda i: (i, 0)),
                   pl.BlockSpec((ROUTE_ROWS, CAP), lambda i: (i, 0)),
                   pl.BlockSpec((ROUTE_ROWS, LANES), lambda i: (i, 0))],
        out_shape=[jax.ShapeDtypeStruct((BATCH * N_EXPERTS, SEQ), jnp.int32),
                   jax.ShapeDtypeStruct((BATCH * N_EXPERTS, CAP), jnp.int32),
                   jax.ShapeDtypeStruct((BATCH * N_EXPERTS, CAP), F32),
                   jax.ShapeDtypeStruct((BATCH * N_EXPERTS, LANES), jnp.int32)],
        compiler_params=_params(("arbitrary",), 48),
        name="route",
    )(aff_t)


GATHER_ROWS = BATCH * CAP
FF_STEPS = EXPERT_FF // FF_TILE
ROWS_PER_CHUNK = GATHER_ROWS // (FF_STEPS * BATCH)
PROLOGUE_UNROLL = 8


def _expert_kernel(tok_ref, h_ref, gates_ref, wg_ref, wu_ref, wd_ref, y_ref, x_ref, sem, xb_ref, acc_ref):
    e = pl.program_id(0)
    f = pl.program_id(1)
    slot = lax.rem(e, 2)
    next_slot = 1 - slot
    next_e = jnp.minimum(e + 1, N_EXPERTS - 1)

    tile = TOKEN_TILE_ROWS

    def row_copy(expert, row, dst_slot):
        src = pl.multiple_of(tok_ref[expert * GATHER_ROWS + row] * tile, tile)
        dst = pl.multiple_of(row * tile, tile)
        return pltpu.make_async_copy(h_ref.at[pl.ds(src, tile), :], x_ref.at[dst_slot, pl.ds(dst, tile), :],
                                     sem.at[dst_slot])

    def wait_rows(dst_slot):
        pltpu.make_async_copy(h_ref.at[pl.ds(0, GATHER_ROWS * tile), :], x_ref.at[dst_slot],
                              sem.at[dst_slot]).wait()

    @pl.when(jnp.logical_and(e == 0, f == 0))
    def _():
        def issue(i, carry):
            for u in range(PROLOGUE_UNROLL):
                row_copy(0, i * PROLOGUE_UNROLL + u, 0).start()
            return carry
        lax.fori_loop(0, GATHER_ROWS // PROLOGUE_UNROLL, issue, 0)
        acc_ref[...] = jnp.zeros_like(acc_ref)

    @pl.when(f == 0)
    def _():
        wait_rows(slot)
        for b in range(BATCH):
            cols = [x_ref[slot, pl.ds(b * CAP * tile + j, CAP, stride=tile), :] for j in range(tile)]
            xb_ref[b] = jnp.concatenate(cols, axis=1).astype(BF16)

    wg = wg_ref[0, 0].astype(BF16)
    wu = wu_ref[0, 0].astype(BF16)
    wd = wd_ref[0, 0].astype(BF16)
    for b in range(BATCH):
        first = (f * BATCH + b) * ROWS_PER_CHUNK
        for j in range(ROWS_PER_CHUNK):
            row_copy(next_e, first + j, next_slot).start()
        x = xb_ref[b]
        a = jnp.dot(x, wg, preferred_element_type=F32)
        u = jnp.dot(x, wu, preferred_element_type=F32)
        act = ((a * jax.nn.sigmoid(a)) * u).astype(BF16)
        carried = jnp.where(f == 0, 0.0, acc_ref[b])
        acc_ref[b] = carried + jnp.dot(act, wd, preferred_element_type=F32)

    @pl.when(f == FF_STEPS - 1)
    def _():
        gates = gates_ref[0]
        for b in range(BATCH):
            y_ref[0, b] = (acc_ref[b] * gates[:, b:b + 1]).astype(BF16)

    @pl.when(jnp.logical_and(e == N_EXPERTS - 1, f == FF_STEPS - 1))
    def _():
        wait_rows(next_slot)


def _experts(tok, h, gates, w_gate, w_up, w_down, layer):
    grid_spec = pltpu.PrefetchScalarGridSpec(
        num_scalar_prefetch=1,
        grid=(N_EXPERTS, FF_STEPS),
        in_specs=[
            pl.BlockSpec(memory_space=pl.ANY),
            pl.BlockSpec((1, CAP, BATCH), lambda e, f, tok: (e, 0, 0)),
            pl.BlockSpec((1, 1, D_MODEL, FF_TILE), lambda e, f, tok: (layer, e, 0, f)),
            pl.BlockSpec((1, 1, D_MODEL, FF_TILE), lambda e, f, tok: (layer, e, 0, f)),
            pl.BlockSpec((1, 1, FF_TILE, D_MODEL), lambda e, f, tok: (layer, e, f, 0)),
        ],
        out_specs=pl.BlockSpec((1, BATCH, CAP, D_MODEL), lambda e, f, tok: (e, 0, 0, 0)),
        scratch_shapes=[pltpu.VMEM((2, GATHER_ROWS * TOKEN_TILE_ROWS, LANES), F32), pltpu.SemaphoreType.DMA((2,)),
                        pltpu.VMEM((BATCH, CAP, D_MODEL), BF16), pltpu.VMEM((BATCH, CAP, D_MODEL), F32)],
    )
    return pl.pallas_call(
        _expert_kernel,
        grid_spec=grid_spec,
        out_shape=jax.ShapeDtypeStruct((N_EXPERTS, BATCH, CAP, D_MODEL), BF16),
        compiler_params=_params(("arbitrary", "arbitrary"), 60),
        name="moe_experts",
    )(tok, h, gates, w_gate, w_up, w_down)


COMBINE_WINDOW = 128
SLOT_ALIGN = 16


def _combine_kernel(start_ref, short_ref, slot_ref, y_ref, x_ref, gate_ref, g_ref, o_ref, *, final):
    tile = pl.program_id(0) * (SEQ // COMBINE_ROWS) + pl.program_id(1)
    slots = slot_ref[0]

    def finish(moe):
        x = x_ref[0] + gate_ref[0] * moe
        if final:
            x = (x * lax.rsqrt(jnp.mean(x * x, axis=-1, keepdims=True) + RMS_EPS)) * g_ref[...]
        o_ref[0] = x

    @pl.when(short_ref[tile] == 1)
    def _():
        lane = lax.broadcasted_iota(jnp.int32, (COMBINE_ROWS, COMBINE_WINDOW), 1)
        hits, rows = [], []
        for e in range(N_EXPERTS):
            start = pl.multiple_of(start_ref[tile * N_EXPERTS + e], SLOT_ALIGN)
            hits.append(jnp.where(slots[:, e:e + 1] - start == lane, 1.0, 0.0).astype(BF16))
            rows.append(y_ref[e, 0, pl.ds(start, COMBINE_WINDOW), :])
        finish(jnp.dot(jnp.concatenate(hits, axis=1), jnp.concatenate(rows, axis=0), preferred_element_type=F32))

    @pl.when(short_ref[tile] != 1)
    def _():
        lane = lax.broadcasted_iota(jnp.int32, (COMBINE_ROWS, CAP), 1)
        onehot = jnp.concatenate(
            [jnp.where(slots[:, e:e + 1] == lane, 1.0, 0.0).astype(BF16) for e in range(N_EXPERTS)], axis=1)
        finish(jnp.dot(onehot, y_ref[...].reshape(N_EXPERTS * CAP, D_MODEL), preferred_element_type=F32))


def _combine_windows(first):
    last = jnp.concatenate([first[..., 1:], jnp.full(first.shape[:2] + (1,), CAP, jnp.int32)], axis=-1)
    start = jnp.minimum(first // SLOT_ALIGN * SLOT_ALIGN, CAP - COMBINE_WINDOW)
    short = jnp.all(last <= start + COMBINE_WINDOW, axis=1)
    return start.transpose(0, 2, 1).reshape(-1), short.astype(jnp.int32).reshape(-1)


def _combine(starts, short, slot_t, y, x, mod, final_g, layer, final):
    grid_spec = pltpu.PrefetchScalarGridSpec(
        num_scalar_prefetch=2,
        grid=(BATCH, SEQ // COMBINE_ROWS),
        in_specs=[
            pl.BlockSpec((1, COMBINE_ROWS, N_EXPERTS), lambda b, t, s, f: (b, t, 0)),
            pl.BlockSpec((N_EXPERTS, 1, CAP, D_MODEL), lambda b, t, s, f: (0, b, 0, 0)),
            pl.BlockSpec((1, COMBINE_ROWS, D_MODEL), lambda b, t, s, f: (b, t, 0)),
            pl.BlockSpec((1, 1, D_MODEL), lambda b, t, s, f: ((layer * 6 + 5) * BATCH + b, 0, 0)),
            pl.BlockSpec((1, D_MODEL), lambda b, t, s, f: (0, 0)),
        ],
        out_specs=pl.BlockSpec((1, COMBINE_ROWS, D_MODEL), lambda b, t, s, f: (b, t, 0)),
    )
    return pl.pallas_call(
        functools.partial(_combine_kernel, final=final),
        grid_spec=grid_spec,
        out_shape=jax.ShapeDtypeStruct((BATCH, SEQ, D_MODEL), F32),
        compiler_params=_params(("arbitrary", "arbitrary"), 48),
        name="moe_combine",
    )(starts, short, slot_t, y, x, mod, final_g.reshape(1, D_MODEL))


def _scaled_q_weight(w, q_columns):
    scale = jnp.concatenate([jnp.full((q_columns,), Q_SCALE, F32), jnp.ones((w.shape[1] - q_columns,), F32)])
    return (w * scale).astype(BF16)


def kernel(x, c, ada_w, ada_b, norm_g, na_w_qkv, na_w_o, na_rpb, sw_w_qkv, sw_w_o, sw_sinks, t5_bias,
           moe_w_router, moe_w_gate, moe_w_up, moe_w_down, final_g):
    mod = _ada(c, ada_w, ada_b)
    norm_g3 = norm_g.reshape(DEPTH * 2, 1, D_MODEL)
    x = x.reshape(BATCH * SEQ, D_MODEL)
    for layer in range(DEPTH):
        j = layer // N_MIXERS
        if layer % N_MIXERS == 0:
            qkv = _qkv(x, norm_g3, mod, _scaled_q_weight(na_w_qkv[j], NA_HEADS * HEAD_DIM), layer)
            o = _na_attention(qkv.reshape(BATCH, SEQ, -1), _na_rpb_rows(na_rpb[j]))
            w_o = na_w_o[j]
        else:
            qkv = _qkv(x, norm_g3, mod, _scaled_q_weight(sw_w_qkv[j], SW_Q_HEADS * HEAD_DIM), layer,
                       head_copies=2 * SW_KV_HEADS * HEAD_DIM // LANES)
            o = _sw_attention(qkv.reshape(BATCH, SEQ, -1), _sw_rel_table(t5_bias), sw_sinks[j])
            w_o = sw_w_o[j]
        w_router = moe_w_router[layer].astype(BF16)
        x, h, aff_t = _post_attn(o.reshape(BATCH * SEQ, -1), w_o.astype(BF16), x, norm_g3, mod,
                                 w_router.T, layer)
        slot, tok, gate, first = _route(aff_t)
        tok = tok.reshape(BATCH, N_EXPERTS, CAP)
        rows = tok.transpose(1, 0, 2) + (jnp.arange(BATCH, dtype=jnp.int32) * SEQ)[None, :, None]
        gates = gate.reshape(BATCH, N_EXPERTS, CAP).transpose(1, 2, 0)
        y = _experts(rows.reshape(-1), h, gates, moe_w_gate, moe_w_up, moe_w_down, layer)
        slot_t = slot.reshape(BATCH, N_EXPERTS, SEQ).transpose(0, 2, 1)
        starts, short = _combine_windows(first.reshape(BATCH, N_EXPERTS, LANES)[..., :SEQ // COMBINE_ROWS])
        x = _combine(starts, short, slot_t, y, x.reshape(BATCH, SEQ, D_MODEL), mod, final_g, layer,
                     final=layer == DEPTH - 1).reshape(BATCH * SEQ, D_MODEL)
    return x.reshape(BATCH, SEQ, D_MODEL)
```

```python
import functools

import numpy as np
import jax
import jax.numpy as jnp
from jax import lax
from jax.experimental import pallas as pl
from jax.experimental.pallas import tpu as pltpu

D_MODEL = 1024
BATCH = 8
SEQ = 2048
DEPTH = 2
GRID_W = 64
ROWS = SEQ // GRID_W
N_MIXERS = 2
HEAD_DIM = 64
NA_HEADS = 16
NA_WIN_H = 8
NA_WIN_W = 16
SW_Q_HEADS = 16
SW_KV_HEADS = 4
SW_GROUP = SW_Q_HEADS // SW_KV_HEADS
SW_WINDOW = 128
SW_BLOCK = 128
SW_NB = SEQ // SW_BLOCK
SW_SPAN = 3 * SW_BLOCK
T5_BUCKETS = 32
T5_MAX_DIST = 128
N_EXPERTS = 16
EXPERT_FF = 2048
EC_CAPACITY = 2
CAP = EC_CAPACITY * SEQ // N_EXPERTS
RMS_EPS = 1e-6
NEG = -1e30
LOG2E = 1.4426950408889634
Q_SCALE = HEAD_DIM ** -0.5 * LOG2E

LANES = 128
TOKEN_TILE_ROWS = D_MODEL // LANES
MIB = 1024 * 1024
F32 = jnp.float32
BF16 = jnp.bfloat16

ROW_TILE = 512
POST_TILE = 1024
ATTN_SEQS = 4
FF_TILE = 512
COMBINE_ROWS = 512
PREFIX_CHUNK = 256


def _params(semantics, vmem_mib):
    return pltpu.CompilerParams(dimension_semantics=semantics, vmem_limit_bytes=vmem_mib * MIB)


def _norm_mod(x, g, sc, sh):
    y = x * lax.rsqrt(jnp.mean(x * x, axis=-1, keepdims=True) + RMS_EPS)
    return (y * g) * (1.0 + sc) + sh


def _softmax0(z):
    z = z - jnp.max(z, axis=0, keepdims=True)
    p = jnp.exp(z)
    return p / jnp.sum(p, axis=0, keepdims=True)


def _half_masks(rows):
    lane = lax.broadcasted_iota(jnp.int32, (rows, LANES), 1)
    lo = jnp.where(lane < HEAD_DIM, 1.0, 0.0).astype(BF16)
    hi = jnp.where(lane < HEAD_DIM, 0.0, 1.0).astype(BF16)
    return lo, hi


def _ada_kernel(c_ref, w_ref, b_ref, o_ref):
    c = c_ref[...]
    act = (c * jax.nn.sigmoid(c)).astype(BF16)
    o_ref[0] = jnp.dot(act, w_ref[0].astype(BF16), preferred_element_type=F32) + b_ref[0]


def _ada(c, ada_w, ada_b):
    out = pl.pallas_call(
        _ada_kernel,
        grid=(DEPTH, 6),
        in_specs=[
            pl.BlockSpec((BATCH, D_MODEL), lambda l, k: (0, 0)),
            pl.BlockSpec((1, D_MODEL, D_MODEL), lambda l, k: (l, 0, k)),
            pl.BlockSpec((1, 1, D_MODEL), lambda l, k: (l * 6 + k, 0, 0)),
        ],
        out_specs=pl.BlockSpec((1, BATCH, D_MODEL), lambda l, k: (l * 6 + k, 0, 0)),
        out_shape=jax.ShapeDtypeStruct((DEPTH * 6, BATCH, D_MODEL), F32),
        compiler_params=_params(("arbitrary", "arbitrary"), 32),
        name="ada_mod",
    )(c, ada_w, ada_b.reshape(DEPTH * 6, 1, D_MODEL))
    return out.reshape(DEPTH * 6 * BATCH, 1, D_MODEL)


def _mod_spec(layer, chunk, row_tile=ROW_TILE):
    tiles_per_seq = SEQ // row_tile
    return pl.BlockSpec((1, 1, D_MODEL),
                        lambda i: ((layer * 6 + chunk) * BATCH + i // tiles_per_seq, 0, 0))


def _qkv_kernel(x_ref, g_ref, sc_ref, sh_ref, w_ref, o_ref, *, head_copies):
    h = _norm_mod(x_ref[...], g_ref[0], sc_ref[0], sh_ref[0])
    y = jnp.dot(h.astype(BF16), w_ref[...], preferred_element_type=F32)
    if head_copies:
        low_half = lax.broadcasted_iota(jnp.int32, (ROW_TILE, LANES), 1) < HEAD_DIM
        keep = y.shape[1] - head_copies * LANES
        tiles = [y[:, :keep]]
        for t in range(head_copies):
            pair = y[:, keep + t * LANES:keep + (t + 1) * LANES]
            swapped = pltpu.roll(pair, HEAD_DIM, 1)
            tiles += [jnp.where(low_half, pair, swapped), jnp.where(low_half, swapped, pair)]
        y = jnp.concatenate(tiles, axis=1)
    o_ref[...] = y.astype(BF16)


def _qkv(x, norm_g3, mod, w, layer, head_copies=0):
    n = w.shape[1] + head_copies * LANES
    return pl.pallas_call(
        functools.partial(_qkv_kernel, head_copies=head_copies),
        grid=(BATCH * SEQ // ROW_TILE,),
        in_specs=[
            pl.BlockSpec((ROW_TILE, D_MODEL), lambda i: (i, 0)),
            pl.BlockSpec((1, 1, D_MODEL), lambda i: (layer * 2, 0, 0)),
            _mod_spec(layer, 1),
            _mod_spec(layer, 0),
            pl.BlockSpec(w.shape, lambda i: (0, 0)),
        ],
        out_specs=pl.BlockSpec((ROW_TILE, n), lambda i: (i, 0)),
        out_shape=jax.ShapeDtypeStruct((BATCH * SEQ, n), BF16),
        compiler_params=_params(("arbitrary",), 48),
        name="norm_qkv",
    )(x, norm_g3, mod, mod, w)


NA_BIAS_ROWS = 2 * NA_WIN_H - 1
NA_ROW_UNROLL = 4


def _na_rpb_rows(rpb):
    w = NA_WIN_W - 1
    rpb = rpb.astype(F32) * LOG2E
    pad = jnp.zeros(rpb.shape[:2] + (LANES - 2 * w - 1,), F32)
    return jnp.concatenate([rpb[..., w:], pad, rpb[..., :w]], axis=-1)


def _na_build_bias(w_ref, bias_ref):
    lane = lax.broadcasted_iota(jnp.int32, (GRID_W, LANES), 1)
    col = lax.broadcasted_iota(jnp.int32, (GRID_W, LANES), 0)
    kc = jnp.bitwise_and(lane, GRID_W - 1)
    cstart = jnp.clip(col - NA_WIN_W // 2, 0, GRID_W - NA_WIN_W)
    in_window = jnp.where(kc >= cstart, jnp.where(kc < cstart + NA_WIN_W, 1.0, 0.0), 0.0) > 0.5
    low_half = lane < GRID_W

    def toeplitz(head, a, shift):
        row = jnp.broadcast_to(w_ref[head, a:a + 1, :], (GRID_W, LANES))
        return pltpu.roll(row, shift, 1, stride=1, stride_axis=0)

    for head in range(2):
        tiles = [jnp.where(in_window, jnp.where(low_half, toeplitz(head, a, 0), toeplitz(head, a + 1, GRID_W)), NEG)
                 for a in range(NA_BIAS_ROWS - 1)]
        for d in range(NA_WIN_H):
            for m in range(NA_WIN_H // 2):
                bias_ref[d, head * GRID_W:(head + 1) * GRID_W, m * LANES:(m + 1) * LANES] = (
                    tiles[NA_WIN_H - 1 - d + 2 * m])


def _na_kernel(w_ref, q_ref, k_ref, v_ref, o_ref, bias_ref, s0_ref, s1_ref, p0_ref, p1_ref):
    @pl.when(pl.program_id(1) == 0)
    def _():
        _na_build_bias(w_ref, bias_ref)

    lo, hi = _half_masks(GRID_W)
    lo_f = lax.broadcasted_iota(jnp.int32, (GRID_W, LANES), 1) < HEAD_DIM
    span = NA_WIN_H * GRID_W
    ones = jnp.ones((span, LANES), BF16)
    seq_groups = ROWS // NA_ROW_UNROLL
    groups = ATTN_SEQS * seq_groups

    def group_rows(i):
        g = jnp.clip(i, 0, groups - 1)
        for u in range(NA_ROW_UNROLL):
            r = lax.rem(g, seq_groups) * NA_ROW_UNROLL + u
            yield u, lax.div(g, seq_groups), r, jnp.clip(r - NA_WIN_H // 2, 0, ROWS - NA_WIN_H)

    def scores(i, s_ref):
        for u, b, r, rs in group_rows(i):
            q = q_ref[b, pl.ds(pl.multiple_of(r * GRID_W, GRID_W), GRID_W), :]
            lhs = jnp.concatenate([q * lo, q * hi], axis=0)
            kw = k_ref[b, pl.ds(pl.multiple_of(rs * GRID_W, GRID_W), span), :]
            s_ref[u] = lax.dot_general(lhs, kw, (((1,), (1,)), ((), ())), preferred_element_type=F32)

    def softmax(i, s_ref, p_ref):
        for u, b, r, rs in group_rows(i):
            s = s_ref[u] + bias_ref[r - rs]
            m = jnp.broadcast_to(jnp.max(s, axis=-1, keepdims=True), (2 * GRID_W, LANES))
            p_ref[u] = jnp.exp2(s - jnp.concatenate([m] * (span // LANES), axis=1)).astype(BF16)

    def values(i, p_ref):
        for u, b, r, rs in group_rows(i):
            vw = v_ref[b, pl.ds(pl.multiple_of(rs * GRID_W, GRID_W), span), :]
            o = jnp.dot(p_ref[u], jnp.concatenate([vw, ones], axis=1), preferred_element_type=F32)
            o = o[:, :LANES] / o[:, LANES:]
            out = jnp.where(lo_f, o[:GRID_W], o[GRID_W:])
            o_ref[b, pl.ds(pl.multiple_of(r * GRID_W, GRID_W), GRID_W), :] = out.astype(BF16)

    def step(i, carry):
        values(2 * i - 1, p1_ref)
        scores(2 * i + 1, s1_ref)
        softmax(2 * i, s0_ref, p0_ref)
        values(2 * i, p0_ref)
        scores(2 * i + 2, s0_ref)
        softmax(2 * i + 1, s1_ref, p1_ref)
        return carry

    p1_ref[...] = jnp.ones_like(p1_ref)
    scores(0, s0_ref)
    lax.fori_loop(0, groups // 2, step, 0)
    values(groups - 1, p1_ref)


def _na_attention(qkv, rpb_rows):
    pairs = NA_HEADS // 2
    return pl.pallas_call(
        _na_kernel,
        grid=(pairs, BATCH // ATTN_SEQS),
        in_specs=[
            pl.BlockSpec((2, NA_BIAS_ROWS, LANES), lambda p, b: (p, 0, 0)),
            pl.BlockSpec((ATTN_SEQS, SEQ, LANES), lambda p, b: (b, 0, p)),
            pl.BlockSpec((ATTN_SEQS, SEQ, LANES), lambda p, b: (b, 0, pairs + p)),
            pl.BlockSpec((ATTN_SEQS, SEQ, LANES), lambda p, b: (b, 0, 2 * pairs + p)),
        ],
        out_specs=pl.BlockSpec((ATTN_SEQS, SEQ, LANES), lambda p, b: (b, 0, p)),
        out_shape=jax.ShapeDtypeStruct((BATCH, SEQ, NA_HEADS * HEAD_DIM), BF16),
        scratch_shapes=[pltpu.VMEM((NA_WIN_H, 2 * GRID_W, NA_WIN_H * GRID_W), F32),
                        pltpu.VMEM((NA_ROW_UNROLL, 2 * GRID_W, NA_WIN_H * GRID_W), F32),
                        pltpu.VMEM((NA_ROW_UNROLL, 2 * GRID_W, NA_WIN_H * GRID_W), F32),
                        pltpu.VMEM((NA_ROW_UNROLL, 2 * GRID_W, NA_WIN_H * GRID_W), BF16),
                        pltpu.VMEM((NA_ROW_UNROLL, 2 * GRID_W, NA_WIN_H * GRID_W), BF16)],
        compiler_params=_params(("arbitrary", "arbitrary"), 32),
        name="na_attention",
    )(rpb_rows, qkv, qkv, qkv)


def _t5_buckets(rel):
    half = T5_BUCKETS // 2
    max_exact = half // 2
    n = np.abs(rel)
    large = max_exact + (np.log(np.maximum(n, 1) / max_exact)
                         / np.log(T5_MAX_DIST / max_exact) * (half - max_exact)).astype(np.int32)
    large = np.minimum(large, half - 1)
    return (rel > 0).astype(np.int32) * half + np.where(n < max_exact, n, large)


SW_REL_PERIOD = 512
SW_BLOCK_UNROLL = 2


def _sw_rel_table(t5_table):
    k = np.arange(SW_REL_PERIOD)
    rel = np.where(k < SW_REL_PERIOD // 2, k, k - SW_REL_PERIOD)
    ok = np.abs(rel) <= SW_WINDOW
    vals = jnp.where(ok[:, None], t5_table[_t5_buckets(rel)].astype(F32) * LOG2E, NEG)
    return vals.T.reshape(SW_KV_HEADS, SW_GROUP, SW_REL_PERIOD)


def _sw_build_bias(rel_ref, bias_ref):
    for g in range(SW_GROUP):
        row = jnp.broadcast_to(rel_ref[0, g:g + 1, :], (SW_BLOCK, SW_REL_PERIOD))
        t = pltpu.roll(row, 0, 1, stride=1, stride_axis=0)
        rows = slice(g * SW_BLOCK, (g + 1) * SW_BLOCK)
        bias_ref[0, rows, :] = t[:, :SW_SPAN]
        bias_ref[1, rows, :] = jnp.concatenate([t[:, SW_SPAN:], t[:, :2 * SW_BLOCK]], axis=1)
        bias_ref[2, rows, :] = jnp.concatenate([t[:, 2 * SW_BLOCK:], t[:, :SW_BLOCK]], axis=1)


def _sw_kernel(sink_ref, rel_ref, q_ref, k_ref, v_ref, o_ref, bias_ref, s0_ref, s1_ref, p0_ref, p1_ref,
               t0_ref, t1_ref):
    j = pl.program_id(0)
    ones = jnp.ones((SW_SPAN, LANES), BF16)

    @pl.when(pl.program_id(1) == 0)
    def _():
        _sw_build_bias(rel_ref, bias_ref)

    masks = _half_masks(SW_BLOCK)
    lo_f = lax.broadcasted_iota(jnp.int32, (SW_BLOCK, LANES), 1) < HEAD_DIM
    sinks = [jnp.full((SW_BLOCK, LANES), sink_ref[j * SW_GROUP + g] * LOG2E, F32) for g in range(SW_GROUP)]
    seq_groups = SW_NB // SW_BLOCK_UNROLL
    groups = ATTN_SEQS * seq_groups

    def group_blocks(i):
        g = jnp.clip(i, 0, groups - 1)
        for u in range(SW_BLOCK_UNROLL):
            n = lax.rem(g, seq_groups) * SW_BLOCK_UNROLL + u
            yield u, lax.div(g, seq_groups), n, jnp.clip(n - 1, 0, SW_NB - 3)

    def block_rows(n):
        return pl.ds(pl.multiple_of(n * SW_BLOCK, SW_BLOCK), SW_BLOCK)

    def scores(i, s_ref):
        for u, b, n, first in group_blocks(i):
            kw = k_ref[b, pl.ds(pl.multiple_of(first * SW_BLOCK, SW_BLOCK), SW_SPAN), :]
            for g in range(SW_GROUP):
                q = q_ref[b, block_rows(n), (g // 2) * LANES:(g // 2 + 1) * LANES] * masks[g % 2]
                s_ref[u * SW_GROUP + g] = lax.dot_general(q, kw, (((1,), (1,)), ((), ())),
                                                          preferred_element_type=F32)

    def softmax(i, s_ref, p_ref, t_ref):
        for u, b, n, first in group_blocks(i):
            for g in range(SW_GROUP):
                c = u * SW_GROUP + g
                s = s_ref[c] + bias_ref[n - first, g * SW_BLOCK:(g + 1) * SW_BLOCK, :]
                m = jnp.maximum(jnp.broadcast_to(jnp.max(s, axis=-1, keepdims=True), (SW_BLOCK, LANES)), sinks[g])
                p_ref[c] = jnp.exp2(s - jnp.concatenate([m] * (SW_SPAN // LANES), axis=1)).astype(BF16)
                t_ref[c] = jnp.exp2(sinks[g] - m)

    def values(i, p_ref, t_ref):
        for u, b, n, first in group_blocks(i):
            vw = v_ref[b, pl.ds(pl.multiple_of(first * SW_BLOCK, SW_BLOCK), SW_SPAN), :]
            v_ones = jnp.concatenate([vw, ones], axis=1)
            heads = []
            for g in range(SW_GROUP):
                c = u * SW_GROUP + g
                o = jnp.dot(p_ref[c], v_ones, preferred_element_type=F32)
                heads.append(o[:, :LANES] / (o[:, LANES:] + t_ref[c]))
            out = jnp.concatenate([jnp.where(lo_f, heads[0], heads[1]), jnp.where(lo_f, heads[2], heads[3])],
                                  axis=1)
            o_ref[b, block_rows(n), :] = out.astype(BF16)

    def step(i, carry):
        values(2 * i - 1, p1_ref, t1_ref)
        scores(2 * i + 1, s1_ref)
        softmax(2 * i, s0_ref, p0_ref, t0_ref)
        values(2 * i, p0_ref, t0_ref)
        scores(2 * i + 2, s0_ref)
        softmax(2 * i + 1, s1_ref, p1_ref, t1_ref)
        return carry

    p1_ref[...] = jnp.ones_like(p1_ref)
    t1_ref[...] = jnp.ones_like(t1_ref)
    scores(0, s0_ref)
    lax.fori_loop(0, groups // 2, step, 0)
    values(groups - 1, p1_ref, t1_ref)


def _sw_attention(qkv, rel_table, sinks):
    qw = SW_GROUP * HEAD_DIM
    k_off = SW_Q_HEADS * HEAD_DIM // LANES
    v_off = k_off + SW_KV_HEADS
    chains = SW_BLOCK_UNROLL * SW_GROUP
    return pl.pallas_call(
        _sw_kernel,
        grid=(SW_KV_HEADS, BATCH // ATTN_SEQS),
        in_specs=[
            pl.BlockSpec(memory_space=pltpu.SMEM),
            pl.BlockSpec((1, SW_GROUP, SW_REL_PERIOD), lambda j, b: (j, 0, 0)),
            pl.BlockSpec((ATTN_SEQS, SEQ, qw), lambda j, b: (b, 0, j)),
            pl.BlockSpec((ATTN_SEQS, SEQ, LANES), lambda j, b: (b, 0, k_off + j)),
            pl.BlockSpec((ATTN_SEQS, SEQ, LANES), lambda j, b: (b, 0, v_off + j)),
        ],
        out_specs=pl.BlockSpec((ATTN_SEQS, SEQ, qw), lambda j, b: (b, 0, j)),
        out_shape=jax.ShapeDtypeStruct((BATCH, SEQ, SW_Q_HEADS * HEAD_DIM), BF16),
        scratch_shapes=[pltpu.VMEM((3, SW_GROUP * SW_BLOCK, SW_SPAN), F32)]
        + [pltpu.VMEM((chains, SW_BLOCK, SW_SPAN), F32)] * 2
        + [pltpu.VMEM((chains, SW_BLOCK, SW_SPAN), BF16)] * 2
        + [pltpu.VMEM((chains, SW_BLOCK, LANES), F32)] * 2,
        compiler_params=_params(("arbitrary", "arbitrary"), 48),
        name="sw_attention",
    )(sinks, rel_table, qkv, qkv, qkv)


def _post_attn_kernel(o_ref, wo_ref, x_ref, gate_ref, g_ref, sc_ref, sh_ref, wr_ref,
                      xo_ref, h_ref, aff_ref):
    y = jnp.dot(o_ref[...], wo_ref[...], preferred_element_type=F32)
    xn = x_ref[...] + gate_ref[0] * y
    xo_ref[...] = xn
    h = _norm_mod(xn, g_ref[0], sc_ref[0], sh_ref[0])
    for j in range(TOKEN_TILE_ROWS):
        h_ref[pl.ds(j, POST_TILE, stride=TOKEN_TILE_ROWS), :] = h[:, j * LANES:(j + 1) * LANES]
    hb = h.astype(BF16)
    logits = lax.dot_general(wr_ref[...], hb, (((1,), (1,)), ((), ())), preferred_element_type=F32)
    aff_ref[...] = _softmax0(logits)


def _post_attn(o, w_o, x, norm_g3, mod, w_router_t, layer):
    rows = BATCH * SEQ
    return pl.pallas_call(
        _post_attn_kernel,
        grid=(rows // POST_TILE,),
        in_specs=[
            pl.BlockSpec((POST_TILE, D_MODEL), lambda i: (i, 0)),
            pl.BlockSpec((D_MODEL, D_MODEL), lambda i: (0, 0)),
            pl.BlockSpec((POST_TILE, D_MODEL), lambda i: (i, 0)),
            _mod_spec(layer, 2, POST_TILE),
            pl.BlockSpec((1, 1, D_MODEL), lambda i: (layer * 2 + 1, 0, 0)),
            _mod_spec(layer, 4, POST_TILE),
            _mod_spec(layer, 3, POST_TILE),
            pl.BlockSpec((N_EXPERTS, D_MODEL), lambda i: (0, 0)),
        ],
        out_specs=[
            pl.BlockSpec((POST_TILE, D_MODEL), lambda i: (i, 0)),
            pl.BlockSpec((POST_TILE * TOKEN_TILE_ROWS, LANES), lambda i: (i, 0)),
            pl.BlockSpec((N_EXPERTS, POST_TILE), lambda i: (0, i)),
        ],
        out_shape=[
            jax.ShapeDtypeStruct((rows, D_MODEL), F32),
            jax.ShapeDtypeStruct((rows * TOKEN_TILE_ROWS, LANES), F32),
            jax.ShapeDtypeStruct((N_EXPERTS, rows), F32),
        ],
        compiler_params=_params(("arbitrary",), 56),
        name="post_attn",
    )(o, w_o, x, mod, norm_g3, mod, mod, w_router_t)


ROUTE_SEQS = BATCH
ROUTE_ROWS = ROUTE_SEQS * N_EXPERTS
ROUTE_BITS = SEQ.bit_length() - 1
ROUTE_VALID = 1 << (2 * ROUTE_BITS)
F32_INF_BITS = 0x7F800000


def _prefix_count(x):
    rows = x.shape[0]
    nchunk = SEQ // PREFIX_CHUNK
    r = lax.broadcasted_iota(jnp.int32, (PREFIX_CHUNK, PREFIX_CHUNK), 0)
    c = lax.broadcasted_iota(jnp.int32, (PREFIX_CHUNK, PREFIX_CHUNK), 1)
    upper = jnp.where(r < c, 1.0, 0.0).astype(BF16)
    chunks = [x[:, k * PREFIX_CHUNK:(k + 1) * PREFIX_CHUNK] for k in range(nchunk)]
    local = jnp.dot(jnp.concatenate(chunks, axis=0).astype(BF16), upper, preferred_element_type=F32)
    out = []
    offset = jnp.zeros((rows, 1), F32)
    for k in range(nchunk):
        out.append(local[k * rows:(k + 1) * rows] + offset)
        offset = offset + jnp.sum(chunks[k], axis=1, keepdims=True)
    return jnp.concatenate(out, axis=1)


def _route_kernel(aff_ref, slot_ref, tok_ref, gate_ref, first_ref):
    aff = jnp.concatenate([aff_ref[:, b * SEQ:(b + 1) * SEQ] for b in range(ROUTE_SEQS)], axis=0)
    bits = pltpu.bitcast(aff, jnp.int32)

    def count(mask):
        return jnp.sum(jnp.where(mask, 1.0, 0.0), axis=1, keepdims=True)

    def search(_, bounds):
        lo, hi = bounds
        mid = lo + ((hi - lo) >> 1)
        ge = count(bits >= mid) >= CAP
        return jnp.where(ge, mid, lo), jnp.where(ge, hi, mid)

    lo0 = jnp.zeros((ROUTE_ROWS, 1), jnp.int32)
    hi0 = jnp.full((ROUTE_ROWS, 1), F32_INF_BITS, jnp.int32)
    tau, _ = lax.fori_loop(0, F32_INF_BITS.bit_length(), search, (lo0, hi0))
    gt = bits > tau
    eq = jnp.where(bits == tau, 1.0, 0.0)
    need = CAP - count(gt)
    sel = jnp.where(gt, 1.0, jnp.where(_prefix_count(eq) < need, eq, 0.0))
    pos = _prefix_count(sel).astype(jnp.int32)
    chosen = sel > 0.5
    slot_ref[...] = jnp.where(chosen, pos, -1)
    tile_lane = lax.broadcasted_iota(jnp.int32, (ROUTE_ROWS, LANES), 1)
    first = jnp.zeros((ROUTE_ROWS, LANES), jnp.int32)
    for k in range(SEQ // COMBINE_ROWS):
        first = jnp.where(tile_lane == k, pos[:, k * COMBINE_ROWS:k * COMBINE_ROWS + 1], first)
    first_ref[...] = first
    lane = lax.broadcasted_iota(jnp.int32, (ROUTE_ROWS, SEQ), 1)
    packed = jnp.where(chosen, ROUTE_VALID | (lane << ROUTE_BITS) | (lane - pos), 0)
    gate = aff
    for k in range(ROUTE_BITS):
        step = 1 << k
        from_right = pltpu.roll(packed, SEQ - step, 1)
        gate_right = pltpu.roll(gate, SEQ - step, 1)
        arrives = (from_right & step) != 0
        stays = (packed & step) == 0
        packed = jnp.where(arrives, from_right, jnp.where(stays, packed, 0))
        gate = jnp.where(arrives, gate_right, gate)
    tok_ref[...] = (packed[:, :CAP] >> ROUTE_BITS) & (SEQ - 1)
    gate_ref[...] = gate[:, :CAP]


def _route(aff_t):
    return pl.pallas_call(
        _route_kernel,
        grid=(BATCH // ROUTE_SEQS,),
        in_specs=[pl.BlockSpec((N_EXPERTS, ROUTE_SEQS * SEQ), lambda i: (0, i))],
        out_specs=[pl.BlockSpec((ROUTE_ROWS, SEQ), lambda i: (i, 0)),
                   pl.BlockSpec((ROUTE_ROWS, CAP), lambda i: (i, 0)),
                   pl.BlockSpec((ROUTE_ROWS, CAP), lambda i: (i, 0)),
                   pl.BlockSpec((ROUTE_ROWS, LANES), lambda i: (i, 0))],
        out_shape=[jax.ShapeDtypeStruct((BATCH * N_EXPERTS, SEQ), jnp.int32),
                   jax.ShapeDtypeStruct((BATCH * N_EXPERTS, CAP), jnp.int32),
                   jax.ShapeDtypeStruct((BATCH * N_EXPERTS, CAP), F32),
                   jax.ShapeDtypeStruct((BATCH * N_EXPERTS, LANES), jnp.int32)],
        compiler_params=_params(("arbitrary",), 48),
        name="route",
    )(aff_t)


GATHER_ROWS = BATCH * CAP
FF_STEPS = EXPERT_FF // FF_TILE
ROWS_PER_CHUNK = GATHER_ROWS // (FF_STEPS * BATCH)
PROLOGUE_UNROLL = 8


def _expert_kernel(tok_ref, h_ref, gates_ref, wg_ref, wu_ref, wd_ref, y_ref, x_ref, sem, xb_ref, acc_ref):
    e = pl.program_id(0)
    f = pl.program_id(1)
    slot = lax.rem(e, 2)
    next_slot = 1 - slot
    next_e = jnp.minimum(e + 1, N_EXPERTS - 1)

    tile = TOKEN_TILE_ROWS

    def row_copy(expert, row, dst_slot):
        src = pl.multiple_of(tok_ref[expert * GATHER_ROWS + row] * tile, tile)
        dst = pl.multiple_of(row * tile, tile)
        return pltpu.make_async_copy(h_ref.at[pl.ds(src, tile), :], x_ref.at[dst_slot, pl.ds(dst, tile), :],
                                     sem.at[dst_slot])

    def wait_rows(dst_slot):
        pltpu.make_async_copy(h_ref.at[pl.ds(0, GATHER_ROWS * tile), :], x_ref.at[dst_slot],
                              sem.at[dst_slot]).wait()

    @pl.when(jnp.logical_and(e == 0, f == 0))
    def _():
        def issue(i, carry):
            for u in range(PROLOGUE_UNROLL):
                row_copy(0, i * PROLOGUE_UNROLL + u, 0).start()
            return carry
        lax.fori_loop(0, GATHER_ROWS // PROLOGUE_UNROLL, issue, 0)
        acc_ref[...] = jnp.zeros_like(acc_ref)

    @pl.when(f == 0)
    def _():
        wait_rows(slot)
        for b in range(BATCH):
            cols = [x_ref[slot, pl.ds(b * CAP * tile + j, CAP, stride=tile), :] for j in range(tile)]
            xb_ref[b] = jnp.concatenate(cols, axis=1).astype(BF16)

    wg = wg_ref[0, 0].astype(BF16)
    wu = wu_ref[0, 0].astype(BF16)
    wd = wd_ref[0, 0].astype(BF16)
    for b in range(BATCH):
        first = (f * BATCH + b) * ROWS_PER_CHUNK
        for j in range(ROWS_PER_CHUNK):
            row_copy(next_e, first + j, next_slot).start()
        x = xb_ref[b]
        a = jnp.dot(x, wg, preferred_element_type=F32)
        u = jnp.dot(x, wu, preferred_element_type=F32)
        act = ((a * jax.nn.sigmoid(a)) * u).astype(BF16)
        carried = jnp.where(f == 0, 0.0, acc_ref[b])
        acc_ref[b] = carried + jnp.dot(act, wd, preferred_element_type=F32)

    @pl.when(f == FF_STEPS - 1)
    def _():
        gates = gates_ref[0]
        for b in range(BATCH):
            y_ref[0, b] = (acc_ref[b] * gates[:, b:b + 1]).astype(BF16)

    @pl.when(jnp.logical_and(e == N_EXPERTS - 1, f == FF_STEPS - 1))
    def _():
        wait_rows(next_slot)


def _experts(tok, h, gates, w_gate, w_up, w_down, layer):
    grid_spec = pltpu.PrefetchScalarGridSpec(
        num_scalar_prefetch=1,
        grid=(N_EXPERTS, FF_STEPS),
        in_specs=[
            pl.BlockSpec(memory_space=pl.ANY),
            pl.BlockSpec((1, CAP, BATCH), lambda e, f, tok: (e, 0, 0)),
            pl.BlockSpec((1, 1, D_MODEL, FF_TILE), lambda e, f, tok: (layer, e, 0, f)),
            pl.BlockSpec((1, 1, D_MODEL, FF_TILE), lambda e, f, tok: (layer, e, 0, f)),
            pl.BlockSpec((1, 1, FF_TILE, D_MODEL), lambda e, f, tok: (layer, e, f, 0)),
        ],
        out_specs=pl.BlockSpec((1, BATCH, CAP, D_MODEL), lambda e, f, tok: (e, 0, 0, 0)),
        scratch_shapes=[pltpu.VMEM((2, GATHER_ROWS * TOKEN_TILE_ROWS, LANES), F32), pltpu.SemaphoreType.DMA((2,)),
                        pltpu.VMEM((BATCH, CAP, D_MODEL), BF16), pltpu.VMEM((BATCH, CAP, D_MODEL), F32)],
    )
    return pl.pallas_call(
        _expert_kernel,
        grid_spec=grid_spec,
        out_shape=jax.ShapeDtypeStruct((N_EXPERTS, BATCH, CAP, D_MODEL), BF16),
        compiler_params=_params(("arbitrary", "arbitrary"), 60),
        name="moe_experts",
    )(tok, h, gates, w_gate, w_up, w_down)


COMBINE_WINDOW = 128
SLOT_ALIGN = 16


def _combine_kernel(start_ref, short_ref, slot_ref, y_ref, x_ref, gate_ref, g_ref, o_ref, *, final):
    tile = pl.program_id(0) * (SEQ // COMBINE_ROWS) + pl.program_id(1)
    slots = slot_ref[0]

    def finish(moe):
        x = x_ref[0] + gate_ref[0] * moe
        if final:
            x = (x * lax.rsqrt(jnp.mean(x * x, axis=-1, keepdims=True) + RMS_EPS)) * g_ref[...]
        o_ref[0] = x

    @pl.when(short_ref[tile] == 1)
    def _():
        lane = lax.broadcasted_iota(jnp.int32, (COMBINE_ROWS, COMBINE_WINDOW), 1)
        hits, rows = [], []
        for e in range(N_EXPERTS):
            start = pl.multiple_of(start_ref[tile * N_EXPERTS + e], SLOT_ALIGN)
            hits.append(jnp.where(slots[:, e:e + 1] - start == lane, 1.0, 0.0).astype(BF16))
            rows.append(y_ref[e, 0, pl.ds(start, COMBINE_WINDOW), :])
        finish(jnp.dot(jnp.concatenate(hits, axis=1), jnp.concatenate(rows, axis=0), preferred_element_type=F32))

    @pl.when(short_ref[tile] != 1)
    def _():
        lane = lax.broadcasted_iota(jnp.int32, (COMBINE_ROWS, CAP), 1)
        onehot = jnp.concatenate(
            [jnp.where(slots[:, e:e + 1] == lane, 1.0, 0.0).astype(BF16) for e in range(N_EXPERTS)], axis=1)
        finish(jnp.dot(onehot, y_ref[...].reshape(N_EXPERTS * CAP, D_MODEL), preferred_element_type=F32))


def _combine_windows(first):
    last = jnp.concatenate([first[..., 1:], jnp.full(first.shape[:2] + (1,), CAP, jnp.int32)], axis=-1)
    start = jnp.minimum(first // SLOT_ALIGN * SLOT_ALIGN, CAP - COMBINE_WINDOW)
    short = jnp.all(last <= start + COMBINE_WINDOW, axis=1)
    return start.transpose(0, 2, 1).reshape(-1), short.astype(jnp.int32).reshape(-1)


def _combine(starts, short, slot_t, y, x, mod, final_g, layer, final):
    grid_spec = pltpu.PrefetchScalarGridSpec(
        num_scalar_prefetch=2,
        grid=(BATCH, SEQ // COMBINE_ROWS),
        in_specs=[
            pl.BlockSpec((1, COMBINE_ROWS, N_EXPERTS), lambda b, t, s, f: (b, t, 0)),
            pl.BlockSpec((N_EXPERTS, 1, CAP, D_MODEL), lambda b, t, s, f: (0, b, 0, 0)),
            pl.BlockSpec((1, COMBINE_ROWS, D_MODEL), lambda b, t, s, f: (b, t, 0)),
            pl.BlockSpec((1, 1, D_MODEL), lambda b, t, s, f: ((layer * 6 + 5) * BATCH + b, 0, 0)),
            pl.BlockSpec((1, D_MODEL), lambda b, t, s, f: (0, 0)),
        ],
        out_specs=pl.BlockSpec((1, COMBINE_ROWS, D_MODEL), lambda b, t, s, f: (b, t, 0)),
    )
    return pl.pallas_call(
        functools.partial(_combine_kernel, final=final),
        grid_spec=grid_spec,
        out_shape=jax.ShapeDtypeStruct((BATCH, SEQ, D_MODEL), F32),
        compiler_params=_params(("arbitrary", "arbitrary"), 48),
        name="moe_combine",
    )(starts, short, slot_t, y, x, mod, final_g.reshape(1, D_MODEL))


def _scaled_q_weight(w, q_columns):
    scale = jnp.concatenate([jnp.full((q_columns,), Q_SCALE, F32), jnp.ones((w.shape[1] - q_columns,), F32)])
    return (w * scale).astype(BF16)


def kernel(x, c, ada_w, ada_b, norm_g, na_w_qkv, na_w_o, na_rpb, sw_w_qkv, sw_w_o, sw_sinks, t5_bias,
           moe_w_router, moe_w_gate, moe_w_up, moe_w_down, final_g):
    mod = _ada(c, ada_w, ada_b)
    norm_g3 = norm_g.reshape(DEPTH * 2, 1, D_MODEL)
    x = x.reshape(BATCH * SEQ, D_MODEL)
    for layer in range(DEPTH):
        j = layer // N_MIXERS
        if layer % N_MIXERS == 0:
            qkv = _qkv(x, norm_g3, mod, _scaled_q_weight(na_w_qkv[j], NA_HEADS * HEAD_DIM), layer)
            o = _na_attention(qkv.reshape(BATCH, SEQ, -1), _na_rpb_rows(na_rpb[j]))
            w_o = na_w_o[j]
        else:
            qkv = _qkv(x, norm_g3, mod, _scaled_q_weight(sw_w_qkv[j], SW_Q_HEADS * HEAD_DIM), layer,
                       head_copies=2 * SW_KV_HEADS * HEAD_DIM // LANES)
            o = _sw_attention(qkv.reshape(BATCH, SEQ, -1), _sw_rel_table(t5_bias), sw_sinks[j])
            w_o = sw_w_o[j]
        w_router = moe_w_router[layer].astype(BF16)
        x, h, aff_t = _post_attn(o.reshape(BATCH * SEQ, -1), w_o.astype(BF16), x, norm_g3, mod,
                                 w_router.T, layer)
        slot, tok, gate, first = _route(aff_t)
        tok = tok.reshape(BATCH, N_EXPERTS, CAP)
        rows = tok.transpose(1, 0, 2) + (jnp.arange(BATCH, dtype=jnp.int32) * SEQ)[None, :, None]
        gates = gate.reshape(BATCH, N_EXPERTS, CAP).transpose(1, 2, 0)
        y = _experts(rows.reshape(-1), h, gates, moe_w_gate, moe_w_up, moe_w_down, layer)
        slot_t = slot.reshape(BATCH, N_EXPERTS, SEQ).transpose(0, 2, 1)
        starts, short = _combine_windows(first.reshape(BATCH, N_EXPERTS, LANES)[..., :SEQ // COMBINE_ROWS])
        x = _combine(starts, short, slot_t, y, x.reshape(BATCH, SEQ, D_MODEL), mod, final_g, layer,
                     final=layer == DEPTH - 1).reshape(BATCH * SEQ, D_MODEL)
    return x.reshape(BATCH, SEQ, D_MODEL)
```

```python
import functools

import numpy as np
import jax
import jax.numpy as jnp
from jax import lax
from jax.experimental import pallas as pl
from jax.experimental.pallas import tpu as pltpu

D_MODEL = 1024
BATCH = 8
SEQ = 2048
DEPTH = 2
GRID_W = 64
ROWS = SEQ // GRID_W
N_MIXERS = 2
HEAD_DIM = 64
NA_HEADS = 16
NA_WIN_H = 8
NA_WIN_W = 16
SW_Q_HEADS = 16
SW_KV_HEADS = 4
SW_GROUP = SW_Q_HEADS // SW_KV_HEADS
SW_WINDOW = 128
SW_BLOCK = 128
SW_NB = SEQ // SW_BLOCK
SW_SPAN = 3 * SW_BLOCK
T5_BUCKETS = 32
T5_MAX_DIST = 128
N_EXPERTS = 16
EXPERT_FF = 2048
EC_CAPACITY = 2
CAP = EC_CAPACITY * SEQ // N_EXPERTS
RMS_EPS = 1e-6
NEG = -1e30
LOG2E = 1.4426950408889634
Q_SCALE = HEAD_DIM ** -0.5 * LOG2E

LANES = 128
TOKEN_TILE_ROWS = D_MODEL // LANES
MIB = 1024 * 1024
F32 = jnp.float32
BF16 = jnp.bfloat16

ROW_TILE = 512
POST_TILE = 1024
ATTN_SEQS = 4
FF_TILE = 512
COMBINE_ROWS = 512
PREFIX_CHUNK = 256


def _params(semantics, vmem_mib):
    return pltpu.CompilerParams(dimension_semantics=semantics, vmem_limit_bytes=vmem_mib * MIB)


def _norm_mod(x, g, sc, sh):
    y = x * lax.rsqrt(jnp.mean(x * x, axis=-1, keepdims=True) + RMS_EPS)
    return (y * g) * (1.0 + sc) + sh


def _softmax0(z):
    z = z - jnp.max(z, axis=0, keepdims=True)
    p = jnp.exp(z)
    return p / jnp.sum(p, axis=0, keepdims=True)


def _half_masks(rows):
    lane = lax.broadcasted_iota(jnp.int32, (rows, LANES), 1)
    lo = jnp.where(lane < HEAD_DIM, 1.0, 0.0).astype(BF16)
    hi = jnp.where(lane < HEAD_DIM, 0.0, 1.0).astype(BF16)
    return lo, hi


def _ada_kernel(c_ref, w_ref, b_ref, o_ref):
    c = c_ref[...]
    act = (c * jax.nn.sigmoid(c)).astype(BF16)
    o_ref[0] = jnp.dot(act, w_ref[0].astype(BF16), preferred_element_type=F32) + b_ref[0]


def _ada(c, ada_w, ada_b):
    out = pl.pallas_call(
        _ada_kernel,
        grid=(DEPTH, 6),
        in_specs=[
            pl.BlockSpec((BATCH, D_MODEL), lambda l, k: (0, 0)),
            pl.BlockSpec((1, D_MODEL, D_MODEL), lambda l, k: (l, 0, k)),
            pl.BlockSpec((1, 1, D_MODEL), lambda l, k: (l * 6 + k, 0, 0)),
        ],
        out_specs=pl.BlockSpec((1, BATCH, D_MODEL), lambda l, k: (l * 6 + k, 0, 0)),
        out_shape=jax.ShapeDtypeStruct((DEPTH * 6, BATCH, D_MODEL), F32),
        compiler_params=_params(("arbitrary", "arbitrary"), 32),
        name="ada_mod",
    )(c, ada_w, ada_b.reshape(DEPTH * 6, 1, D_MODEL))
    return out.reshape(DEPTH * 6 * BATCH, 1, D_MODEL)


def _mod_spec(layer, chunk, row_tile=ROW_TILE):
    tiles_per_seq = SEQ // row_tile
    return pl.BlockSpec((1, 1, D_MODEL),
                        lambda i: ((layer * 6 + chunk) * BATCH + i // tiles_per_seq, 0, 0))


def _qkv_kernel(x_ref, g_ref, sc_ref, sh_ref, w_ref, o_ref, *, head_copies):
    h = _norm_mod(x_ref[...], g_ref[0], sc_ref[0], sh_ref[0])
    y = jnp.dot(h.astype(BF16), w_ref[...], preferred_element_type=F32)
    if head_copies:
        low_half = lax.broadcasted_iota(jnp.int32, (ROW_TILE, LANES), 1) < HEAD_DIM
        keep = y.shape[1] - head_copies * LANES
        tiles = [y[:, :keep]]
        for t in range(head_copies):
            pair = y[:, keep + t * LANES:keep + (t + 1) * LANES]
            swapped = pltpu.roll(pair, HEAD_DIM, 1)
            tiles += [jnp.where(low_half, pair, swapped), jnp.where(low_half, swapped, pair)]
        y = jnp.concatenate(tiles, axis=1)
    o_ref[...] = y.astype(BF16)


def _qkv(x, norm_g3, mod, w, layer, head_copies=0):
    n = w.shape[1] + head_copies * LANES
    return pl.pallas_call(
        functools.partial(_qkv_kernel, head_copies=head_copies),
        grid=(BATCH * SEQ // ROW_TILE,),
        in_specs=[
            pl.BlockSpec((ROW_TILE, D_MODEL), lambda i: (i, 0)),
            pl.BlockSpec((1, 1, D_MODEL), lambda i: (layer * 2, 0, 0)),
            _mod_spec(layer, 1),
            _mod_spec(layer, 0),
            pl.BlockSpec(w.shape, lambda i: (0, 0)),
        ],
        out_specs=pl.BlockSpec((ROW_TILE, n), lambda i: (i, 0)),
        out_shape=jax.ShapeDtypeStruct((BATCH * SEQ, n), BF16),
        compiler_params=_params(("arbitrary",), 48),
        name="norm_qkv",
    )(x, norm_g3, mod, mod, w)


NA_BIAS_ROWS = 2 * NA_WIN_H - 1
NA_ROW_UNROLL = 4


def _na_rpb_rows(rpb):
    w = NA_WIN_W - 1
    rpb = rpb.astype(F32) * LOG2E
    pad = jnp.zeros(rpb.shape[:2] + (LANES - 2 * w - 1,), F32)
    return jnp.concatenate([rpb[..., w:], pad, rpb[..., :w]], axis=-1)


def _na_build_bias(w_ref, bias_ref):
    lane = lax.broadcasted_iota(jnp.int32, (GRID_W, LANES), 1)
    col = lax.broadcasted_iota(jnp.int32, (GRID_W, LANES), 0)
    kc = jnp.bitwise_and(lane, GRID_W - 1)
    cstart = jnp.clip(col - NA_WIN_W // 2, 0, GRID_W - NA_WIN_W)
    in_window = jnp.where(kc >= cstart, jnp.where(kc < cstart + NA_WIN_W, 1.0, 0.0), 0.0) > 0.5
    low_half = lane < GRID_W

    def toeplitz(head, a, shift):
        row = jnp.broadcast_to(w_ref[head, a:a + 1, :], (GRID_W, LANES))
        return pltpu.roll(row, shift, 1, stride=1, stride_axis=0)

    for head in range(2):
        tiles = [jnp.where(in_window, jnp.where(low_half, toeplitz(head, a, 0), toeplitz(head, a + 1, GRID_W)), NEG)
                 for a in range(NA_BIAS_ROWS - 1)]
        for d in range(NA_WIN_H):
            for m in range(NA_WIN_H // 2):
                bias_ref[d, head * GRID_W:(head + 1) * GRID_W, m * LANES:(m + 1) * LANES] = (
                    tiles[NA_WIN_H - 1 - d + 2 * m])


def _na_kernel(w_ref, q_ref, k_ref, v_ref, o_ref, bias_ref, s0_ref, s1_ref, p0_ref, p1_ref):
    @pl.when(pl.program_id(1) == 0)
    def _():
        _na_build_bias(w_ref, bias_ref)

    lo, hi = _half_masks(GRID_W)
    lo_f = lax.broadcasted_iota(jnp.int32, (GRID_W, LANES), 1) < HEAD_DIM
    span = NA_WIN_H * GRID_W
    ones = jnp.ones((span, LANES), BF16)
    seq_groups = ROWS // NA_ROW_UNROLL
    groups = ATTN_SEQS * seq_groups

    def group_rows(i):
        g = jnp.clip(i, 0, groups - 1)
        for u in range(NA_ROW_UNROLL):
            r = lax.rem(g, seq_groups) * NA_ROW_UNROLL + u
            yield u, lax.div(g, seq_groups), r, jnp.clip(r - NA_WIN_H // 2, 0, ROWS - NA_WIN_H)

    def scores(i, s_ref):
        for u, b, r, rs in group_rows(i):
            q = q_ref[b, pl.ds(pl.multiple_of(r * GRID_W, GRID_W), GRID_W), :]
            lhs = jnp.concatenate([q * lo, q * hi], axis=0)
            kw = k_ref[b, pl.ds(pl.multiple_of(rs * GRID_W, GRID_W), span), :]
            s_ref[u] = lax.dot_general(lhs, kw, (((1,), (1,)), ((), ())), preferred_element_type=F32)

    def softmax(i, s_ref, p_ref):
        for u, b, r, rs in group_rows(i):
            s = s_ref[u] + bias_ref[r - rs]
            m = jnp.broadcast_to(jnp.max(s, axis=-1, keepdims=True), (2 * GRID_W, LANES))
            p_ref[u] = jnp.exp2(s - jnp.concatenate([m] * (span // LANES), axis=1)).astype(BF16)

    def values(i, p_ref):
        for u, b, r, rs in group_rows(i):
            vw = v_ref[b, pl.ds(pl.multiple_of(rs * GRID_W, GRID_W), span), :]
            o = jnp.dot(p_ref[u], jnp.concatenate([vw, ones], axis=1), preferred_element_type=F32)
            o = o[:, :LANES] / o[:, LANES:]
            out = jnp.where(lo_f, o[:GRID_W], o[GRID_W:])
            o_ref[b, pl.ds(pl.multiple_of(r * GRID_W, GRID_W), GRID_W), :] = out.astype(BF16)

    def step(i, carry):
        values(2 * i - 1, p1_ref)
        scores(2 * i + 1, s1_ref)
        softmax(2 * i, s0_ref, p0_ref)
        values(2 * i, p0_ref)
        scores(2 * i + 2, s0_ref)
        softmax(2 * i + 1, s1_ref, p1_ref)
        return carry

    p1_ref[...] = jnp.ones_like(p1_ref)
    scores(0, s0_ref)
    lax.fori_loop(0, groups // 2, step, 0)
    values(groups - 1, p1_ref)


def _na_attention(qkv, rpb_rows):
    pairs = NA_HEADS // 2
    return pl.pallas_call(
        _na_kernel,
        grid=(pairs, BATCH // ATTN_SEQS),
        in_specs=[
            pl.BlockSpec((2, NA_BIAS_ROWS, LANES), lambda p, b: (p, 0, 0)),
            pl.BlockSpec((ATTN_SEQS, SEQ, LANES), lambda p, b: (b, 0, p)),
            pl.BlockSpec((ATTN_SEQS, SEQ, LANES), lambda p, b: (b, 0, pairs + p)),
            pl.BlockSpec((ATTN_SEQS, SEQ, LANES), lambda p, b: (b, 0, 2 * pairs + p)),
        ],
        out_specs=pl.BlockSpec((ATTN_SEQS, SEQ, LANES), lambda p, b: (b, 0, p)),
        out_shape=jax.ShapeDtypeStruct((BATCH, SEQ, NA_HEADS * HEAD_DIM), BF16),
        scratch_shapes=[pltpu.VMEM((NA_WIN_H, 2 * GRID_W, NA_WIN_H * GRID_W), F32),
                        pltpu.VMEM((NA_ROW_UNROLL, 2 * GRID_W, NA_WIN_H * GRID_W), F32),
                        pltpu.VMEM((NA_ROW_UNROLL, 2 * GRID_W, NA_WIN_H * GRID_W), F32),
                        pltpu.VMEM((NA_ROW_UNROLL, 2 * GRID_W, NA_WIN_H * GRID_W), BF16),
                        pltpu.VMEM((NA_ROW_UNROLL, 2 * GRID_W, NA_WIN_H * GRID_W), BF16)],
        compiler_params=_params(("arbitrary", "arbitrary"), 32),
        name="na_attention",
    )(rpb_rows, qkv, qkv, qkv)


def _t5_buckets(rel):
    half = T5_BUCKETS // 2
    max_exact = half // 2
    n = np.abs(rel)
    large = max_exact + (np.log(np.maximum(n, 1) / max_exact)
                         / np.log(T5_MAX_DIST / max_exact) * (half - max_exact)).astype(np.int32)
    large = np.minimum(large, half - 1)
    return (rel > 0).astype(np.int32) * half + np.where(n < max_exact, n, large)


SW_REL_PERIOD = 512
SW_BLOCK_UNROLL = 2


def _sw_rel_table(t5_table):
    k = np.arange(SW_REL_PERIOD)
    rel = np.where(k < SW_REL_PERIOD // 2, k, k - SW_REL_PERIOD)
    ok = np.abs(rel) <= SW_WINDOW
    vals = jnp.where(ok[:, None], t5_table[_t5_buckets(rel)].astype(F32) * LOG2E, NEG)
    return vals.T.reshape(SW_KV_HEADS, SW_GROUP, SW_REL_PERIOD)


def _sw_build_bias(rel_ref, bias_ref):
    for g in range(SW_GROUP):
        row = jnp.broadcast_to(rel_ref[0, g:g + 1, :], (SW_BLOCK, SW_REL_PERIOD))
        t = pltpu.roll(row, 0, 1, stride=1, stride_axis=0)
        rows = slice(g * SW_BLOCK, (g + 1) * SW_BLOCK)
        bias_ref[0, rows, :] = t[:, :SW_SPAN]
        bias_ref[1, rows, :] = jnp.concatenate([t[:, SW_SPAN:], t[:, :2 * SW_BLOCK]], axis=1)
        bias_ref[2, rows, :] = jnp.concatenate([t[:, 2 * SW_BLOCK:], t[:, :SW_BLOCK]], axis=1)


def _sw_kernel(sink_ref, rel_ref, q_ref, k_ref, v_ref, o_ref, bias_ref, s0_ref, s1_ref, p0_ref, p1_ref,
               t0_ref, t1_ref):
    j = pl.program_id(0)
    ones = jnp.ones((SW_SPAN, LANES), BF16)

    @pl.when(pl.program_id(1) == 0)
    def _():
        _sw_build_bias(rel_ref, bias_ref)

    masks = _half_masks(SW_BLOCK)
    lo_f = lax.broadcasted_iota(jnp.int32, (SW_BLOCK, LANES), 1) < HEAD_DIM
    sinks = [jnp.full((SW_BLOCK, LANES), sink_ref[j * SW_GROUP + g] * LOG2E, F32) for g in range(SW_GROUP)]
    seq_groups = SW_NB // SW_BLOCK_UNROLL
    groups = ATTN_SEQS * seq_groups

    def group_blocks(i):
        g = jnp.clip(i, 0, groups - 1)
        for u in range(SW_BLOCK_UNROLL):
            n = lax.rem(g, seq_groups) * SW_BLOCK_UNROLL + u
            yield u, lax.div(g, seq_groups), n, jnp.clip(n - 1, 0, SW_NB - 3)

    def block_rows(n):
        return pl.ds(pl.multiple_of(n * SW_BLOCK, SW_BLOCK), SW_BLOCK)

    def scores(i, s_ref):
        for u, b, n, first in group_blocks(i):
            kw = k_ref[b, pl.ds(pl.multiple_of(first * SW_BLOCK, SW_BLOCK), SW_SPAN), :]
            for g in range(SW_GROUP):
                q = q_ref[b, block_rows(n), (g // 2) * LANES:(g // 2 + 1) * LANES] * masks[g % 2]
                s_ref[u * SW_GROUP + g] = lax.dot_general(q, kw, (((1,), (1,)), ((), ())),
                                                          preferred_element_type=F32)

    def softmax(i, s_ref, p_ref, t_ref):
        for u, b, n, first in group_blocks(i):
            for g in range(SW_GROUP):
                c = u * SW_GROUP + g
                s = s_ref[c] + bias_ref[n - first, g * SW_BLOCK:(g + 1) * SW_BLOCK, :]
                m = jnp.maximum(jnp.broadcast_to(jnp.max(s, axis=-1, keepdims=True), (SW_BLOCK, LANES)), sinks[g])
                p_ref[c] = jnp.exp2(s - jnp.concatenate([m] * (SW_SPAN // LANES), axis=1)).astype(BF16)
                t_ref[c] = jnp.exp2(sinks[g] - m)

    def values(i, p_ref, t_ref):
        for u, b, n, first in group_blocks(i):
            vw = v_ref[b, pl.ds(pl.multiple_of(first * SW_BLOCK, SW_BLOCK), SW_SPAN), :]
            v_ones = jnp.concatenate([vw, ones], axis=1)
            heads = []
            for g in range(SW_GROUP):
                c = u * SW_GROUP + g
                o = jnp.dot(p_ref[c], v_ones, preferred_element_type=F32)
                heads.append(o[:, :LANES] / (o[:, LANES:] + t_ref[c]))
            out = jnp.concatenate([jnp.where(lo_f, heads[0], heads[1]), jnp.where(lo_f, heads[2], heads[3])],
                                  axis=1)
            o_ref[b, block_rows(n), :] = out.astype(BF16)

    def step(i, carry):
        values(2 * i - 1, p1_ref, t1_ref)
        scores(2 * i + 1, s1_ref)
        softmax(2 * i, s0_ref, p0_ref, t0_ref)
        values(2 * i, p0_ref, t0_ref)
        scores(2 * i + 2, s0_ref)
        softmax(2 * i + 1, s1_ref, p1_ref, t1_ref)
        return carry

    p1_ref[...] = jnp.ones_like(p1_ref)
    t1_ref[...] = jnp.ones_like(t1_ref)
    scores(0, s0_ref)
    lax.fori_loop(0, groups // 2, step, 0)
    values(groups - 1, p1_ref, t1_ref)


def _sw_attention(qkv, rel_table, sinks):
    qw = SW_GROUP * HEAD_DIM
    k_off = SW_Q_HEADS * HEAD_DIM // LANES
    v_off = k_off + SW_KV_HEADS
    chains = SW_BLOCK_UNROLL * SW_GROUP
    return pl.pallas_call(
        _sw_kernel,
        grid=(SW_KV_HEADS, BATCH // ATTN_SEQS),
        in_specs=[
            pl.BlockSpec(memory_space=pltpu.SMEM),
            pl.BlockSpec((1, SW_GROUP, SW_REL_PERIOD), lambda j, b: (j, 0, 0)),
            pl.BlockSpec((ATTN_SEQS, SEQ, qw), lambda j, b: (b, 0, j)),
            pl.BlockSpec((ATTN_SEQS, SEQ, LANES), lambda j, b: (b, 0, k_off + j)),
            pl.BlockSpec((ATTN_SEQS, SEQ, LANES), lambda j, b: (b, 0, v_off + j)),
        ],
        out_specs=pl.BlockSpec((ATTN_SEQS, SEQ, qw), lambda j, b: (b, 0, j)),
        out_shape=jax.ShapeDtypeStruct((BATCH, SEQ, SW_Q_HEADS * HEAD_DIM), BF16),
        scratch_shapes=[pltpu.VMEM((3, SW_GROUP * SW_BLOCK, SW_SPAN), F32)]
        + [pltpu.VMEM((chains, SW_BLOCK, SW_SPAN), F32)] * 2
        + [pltpu.VMEM((chains, SW_BLOCK, SW_SPAN), BF16)] * 2
        + [pltpu.VMEM((chains, SW_BLOCK, LANES), F32)] * 2,
        compiler_params=_params(("arbitrary", "arbitrary"), 48),
        name="sw_attention",
    )(sinks, rel_table, qkv, qkv, qkv)


def _post_attn_kernel(o_ref, wo_ref, x_ref, gate_ref, g_ref, sc_ref, sh_ref, wr_ref,
                      xo_ref, h_ref, aff_ref):
    y = jnp.dot(o_ref[...], wo_ref[...], preferred_element_type=F32)
    xn = x_ref[...] + gate_ref[0] * y
    xo_ref[...] = xn
    h = _norm_mod(xn, g_ref[0], sc_ref[0], sh_ref[0])
    for j in range(TOKEN_TILE_ROWS):
        h_ref[pl.ds(j, POST_TILE, stride=TOKEN_TILE_ROWS), :] = h[:, j * LANES:(j + 1) * LANES]
    hb = h.astype(BF16)
    logits = lax.dot_general(wr_ref[...], hb, (((1,), (1,)), ((), ())), preferred_element_type=F32)
    aff_ref[...] = _softmax0(logits)


def _post_attn(o, w_o, x, norm_g3, mod, w_router_t, layer):
    rows = BATCH * SEQ
    return pl.pallas_call(
        _post_attn_kernel,
        grid=(rows // POST_TILE,),
        in_specs=[
            pl.BlockSpec((POST_TILE, D_MODEL), lambda i: (i, 0)),
            pl.BlockSpec((D_MODEL, D_MODEL), lambda i: (0, 0)),
            pl.BlockSpec((POST_TILE, D_MODEL), lambda i: (i, 0)),
            _mod_spec(layer, 2, POST_TILE),
            pl.BlockSpec((1, 1, D_MODEL), lambda i: (layer * 2 + 1, 0, 0)),
            _mod_spec(layer, 4, POST_TILE),
            _mod_spec(layer, 3, POST_TILE),
            pl.BlockSpec((N_EXPERTS, D_MODEL), lambda i: (0, 0)),
        ],
        out_specs=[
            pl.BlockSpec((POST_TILE, D_MODEL), lambda i: (i, 0)),
            pl.BlockSpec((POST_TILE * TOKEN_TILE_ROWS, LANES), lambda i: (i, 0)),
            pl.BlockSpec((N_EXPERTS, POST_TILE), lambda i: (0, i)),
        ],
        out_shape=[
            jax.ShapeDtypeStruct((rows, D_MODEL), F32),
            jax.ShapeDtypeStruct((rows * TOKEN_TILE_ROWS, LANES), F32),
            jax.ShapeDtypeStruct((N_EXPERTS, rows), F32),
        ],
        compiler_params=_params(("arbitrary",), 56),
        name="post_attn",
    )(o, w_o, x, mod, norm_g3, mod, mod, w_router_t)


ROUTE_SEQS = BATCH
ROUTE_ROWS = ROUTE_SEQS * N_EXPERTS
ROUTE_BITS = SEQ.bit_length() - 1
ROUTE_VALID = 1 << (2 * ROUTE_BITS)
F32_INF_BITS = 0x7F800000


def _prefix_count(x):
    rows = x.shape[0]
    nchunk = SEQ // PREFIX_CHUNK
    r = lax.broadcasted_iota(jnp.int32, (PREFIX_CHUNK, PREFIX_CHUNK), 0)
    c = lax.broadcasted_iota(jnp.int32, (PREFIX_CHUNK, PREFIX_CHUNK), 1)
    upper = jnp.where(r < c, 1.0, 0.0).astype(BF16)
    chunks = [x[:, k * PREFIX_CHUNK:(k + 1) * PREFIX_CHUNK] for k in range(nchunk)]
    local = jnp.dot(jnp.concatenate(chunks, axis=0).astype(BF16), upper, preferred_element_type=F32)
    out = []
    offset = jnp.zeros((rows, 1), F32)
    for k in range(nchunk):
        out.append(local[k * rows:(k + 1) * rows] + offset)
        offset = offset + jnp.sum(chunks[k], axis=1, keepdims=True)
    return jnp.concatenate(out, axis=1)


def _route_kernel(aff_ref, slot_ref, tok_ref, gate_ref, first_ref):
    aff = jnp.concatenate([aff_ref[:, b * SEQ:(b + 1) * SEQ] for b in range(ROUTE_SEQS)], axis=0)
    bits = pltpu.bitcast(aff, jnp.int32)

    def count(mask):
        return jnp.sum(jnp.where(mask, 1.0, 0.0), axis=1, keepdims=True)

    def search(_, bounds):
        lo, hi = bounds
        mid = lo + ((hi - lo) >> 1)
        ge = count(bits >= mid) >= CAP
        return jnp.where(ge, mid, lo), jnp.where(ge, hi, mid)

    lo0 = jnp.zeros((ROUTE_ROWS, 1), jnp.int32)
    hi0 = jnp.full((ROUTE_ROWS, 1), F32_INF_BITS, jnp.int32)
    tau, _ = lax.fori_loop(0, F32_INF_BITS.bit_length(), search, (lo0, hi0))
    gt = bits > tau
    eq = jnp.where(bits == tau, 1.0, 0.0)
    need = CAP - count(gt)
    sel = jnp.where(gt, 1.0, jnp.where(_prefix_count(eq) < need, eq, 0.0))
    pos = _prefix_count(sel).astype(jnp.int32)
    chosen = sel > 0.5
    slot_ref[...] = jnp.where(chosen, pos, -1)
    tile_lane = lax.broadcasted_iota(jnp.int32, (ROUTE_ROWS, LANES), 1)
    first = jnp.zeros((ROUTE_ROWS, LANES), jnp.int32)
    for k in range(SEQ // COMBINE_ROWS):
        first = jnp.where(tile_lane == k, pos[:, k * COMBINE_ROWS:k * COMBINE_ROWS + 1], first)
    first_ref[...] = first
    lane = lax.broadcasted_iota(jnp.int32, (ROUTE_ROWS, SEQ), 1)
    packed = jnp.where(chosen, ROUTE_VALID | (lane << ROUTE_BITS) | (lane - pos), 0)
    gate = aff
    for k in range(ROUTE_BITS):
        step = 1 << k
        from_right = pltpu.roll(packed, SEQ - step, 1)
        gate_right = pltpu.roll(gate, SEQ - step, 1)
        arrives = (from_right & step) != 0
        stays = (packed & step) == 0
        packed = jnp.where(arrives, from_right, jnp.where(stays, packed, 0))
        gate = jnp.where(arrives, gate_right, gate)
    tok_ref[...] = (packed[:, :CAP] >> ROUTE_BITS) & (SEQ - 1)
    gate_ref[...] = gate[:, :CAP]


def _route(aff_t):
    return pl.pallas_call(
        _route_kernel,
        grid=(BATCH // ROUTE_SEQS,),
        in_specs=[pl.BlockSpec((N_EXPERTS, ROUTE_SEQS * SEQ), lambda i: (0, i))],
        out_specs=[pl.BlockSpec((ROUTE_ROWS, SEQ), lambda i: (i, 0)),
                   pl.BlockSpec((ROUTE_ROWS, CAP), lambda i: (i, 0)),
                   pl.BlockSpec((ROUTE_ROWS, CAP), lambda i: (i, 0)),
                   pl.BlockSpec((ROUTE_ROWS, LANES), lambda i: (i, 0))],
        out_shape=[jax.ShapeDtypeStruct((BATCH * N_EXPERTS, SEQ), jnp.int32),
                   jax.ShapeDtypeStruct((BATCH * N_EXPERTS, CAP), jnp.int32),
                   jax.ShapeDtypeStruct((BATCH * N_EXPERTS, CAP), F32),
                   jax.ShapeDtypeStruct((BATCH * N_EXPERTS, LANES), jnp.int32)],
        compiler_params=_params(("arbitrary",), 48),
        name="route",
    )(aff_t)


GATHER_ROWS = BATCH * CAP
FF_STEPS = EXPERT_FF // FF_TILE
CHUNK_SEQS = 2
ROWS_PER_CHUNK = GATHER_ROWS * CHUNK_SEQS // (FF_STEPS * BATCH)
PROLOGUE_UNROLL = 8


def _expert_kernel(tok_ref, h_ref, gates_ref, wg_ref, wu_ref, wd_ref, y_ref, x_ref, sem, xb_ref, acc_ref):
    e = pl.program_id(0)
    f = pl.program_id(1)
    slot = lax.rem(e, 2)
    next_slot = 1 - slot
    next_e = jnp.minimum(e + 1, N_EXPERTS - 1)

    tile = TOKEN_TILE_ROWS

    def row_copy(expert, row, dst_slot):
        src = pl.multiple_of(tok_ref[expert * GATHER_ROWS + row] * tile, tile)
        dst = pl.multiple_of(row * tile, tile)
        return pltpu.make_async_copy(h_ref.at[pl.ds(src, tile), :], x_ref.at[dst_slot, pl.ds(dst, tile), :],
                                     sem.at[dst_slot])

    def wait_rows(dst_slot):
        pltpu.make_async_copy(h_ref.at[pl.ds(0, GATHER_ROWS * tile), :], x_ref.at[dst_slot],
                              sem.at[dst_slot]).wait()

    @pl.when(jnp.logical_and(e == 0, f == 0))
    def _():
        def issue(i, carry):
            for u in range(PROLOGUE_UNROLL):
                row_copy(0, i * PROLOGUE_UNROLL + u, 0).start()
            return carry
        lax.fori_loop(0, GATHER_ROWS // PROLOGUE_UNROLL, issue, 0)
        acc_ref[...] = jnp.zeros_like(acc_ref)

    @pl.when(f == 0)
    def _():
        wait_rows(slot)
        for b in range(BATCH):
            cols = [x_ref[slot, pl.ds(b * CAP * tile + j, CAP, stride=tile), :] for j in range(tile)]
            xb_ref[b] = jnp.concatenate(cols, axis=1).astype(BF16)

    wg = wg_ref[0, 0].astype(BF16)
    wu = wu_ref[0, 0].astype(BF16)
    wd = wd_ref[0, 0].astype(BF16)
    for c in range(BATCH // CHUNK_SEQS):
        first = (f * (BATCH // CHUNK_SEQS) + c) * ROWS_PER_CHUNK
        for j in range(ROWS_PER_CHUNK):
            row_copy(next_e, first + j, next_slot).start()
        seqs = slice(c * CHUNK_SEQS, (c + 1) * CHUNK_SEQS)
        x = xb_ref[seqs].reshape(CHUNK_SEQS * CAP, D_MODEL)
        a = jnp.dot(x, wg, preferred_element_type=F32)
        u = jnp.dot(x, wu, preferred_element_type=F32)
        act = ((a * jax.nn.sigmoid(a)) * u).astype(BF16)
        carried = jnp.where(f == 0, 0.0, acc_ref[seqs])
        part = jnp.dot(act, wd, preferred_element_type=F32).reshape(CHUNK_SEQS, CAP, D_MODEL)
        acc_ref[seqs] = carried + part

    @pl.when(f == FF_STEPS - 1)
    def _():
        gates = gates_ref[0]
        for b in range(BATCH):
            y_ref[0, b] = (acc_ref[b] * gates[:, b:b + 1]).astype(BF16)

    @pl.when(jnp.logical_and(e == N_EXPERTS - 1, f == FF_STEPS - 1))
    def _():
        wait_rows(next_slot)


def _experts(tok, h, gates, w_gate, w_up, w_down, layer):
    grid_spec = pltpu.PrefetchScalarGridSpec(
        num_scalar_prefetch=1,
        grid=(N_EXPERTS, FF_STEPS),
        in_specs=[
            pl.BlockSpec(memory_space=pl.ANY),
            pl.BlockSpec((1, CAP, BATCH), lambda e, f, tok: (e, 0, 0)),
            pl.BlockSpec((1, 1, D_MODEL, FF_TILE), lambda e, f, tok: (layer, e, 0, f)),
            pl.BlockSpec((1, 1, D_MODEL, FF_TILE), lambda e, f, tok: (layer, e, 0, f)),
            pl.BlockSpec((1, 1, FF_TILE, D_MODEL), lambda e, f, tok: (layer, e, f, 0)),
        ],
        out_specs=pl.BlockSpec((1, BATCH, CAP, D_MODEL), lambda e, f, tok: (e, 0, 0, 0)),
        scratch_shapes=[pltpu.VMEM((2, GATHER_ROWS * TOKEN_TILE_ROWS, LANES), F32), pltpu.SemaphoreType.DMA((2,)),
                        pltpu.VMEM((BATCH, CAP, D_MODEL), BF16), pltpu.VMEM((BATCH, CAP, D_MODEL), F32)],
    )
    return pl.pallas_call(
        _expert_kernel,
        grid_spec=grid_spec,
        out_shape=jax.ShapeDtypeStruct((N_EXPERTS, BATCH, CAP, D_MODEL), BF16),
        compiler_params=_params(("arbitrary", "arbitrary"), 60),
        name="moe_experts",
    )(tok, h, gates, w_gate, w_up, w_down)


COMBINE_WINDOW = 128
SLOT_ALIGN = 16


def _combine_kernel(start_ref, short_ref, slot_ref, y_ref, x_ref, gate_ref, g_ref, o_ref, *, final):
    tile = pl.program_id(0) * (SEQ // COMBINE_ROWS) + pl.program_id(1)
    slots = slot_ref[0]

    def finish(moe):
        x = x_ref[0] + gate_ref[0] * moe
        if final:
            x = (x * lax.rsqrt(jnp.mean(x * x, axis=-1, keepdims=True) + RMS_EPS)) * g_ref[...]
        o_ref[0] = x

    @pl.when(short_ref[tile] == 1)
    def _():
        lane = lax.broadcasted_iota(jnp.int32, (COMBINE_ROWS, COMBINE_WINDOW), 1)
        hits, rows = [], []
        for e in range(N_EXPERTS):
            start = pl.multiple_of(start_ref[tile * N_EXPERTS + e], SLOT_ALIGN)
            hits.append(jnp.where(slots[:, e:e + 1] - start == lane, 1.0, 0.0).astype(BF16))
            rows.append(y_ref[e, 0, pl.ds(start, COMBINE_WINDOW), :])
        finish(jnp.dot(jnp.concatenate(hits, axis=1), jnp.concatenate(rows, axis=0), preferred_element_type=F32))

    @pl.when(short_ref[tile] != 1)
    def _():
        lane = lax.broadcasted_iota(jnp.int32, (COMBINE_ROWS, CAP), 1)
        onehot = jnp.concatenate(
            [jnp.where(slots[:, e:e + 1] == lane, 1.0, 0.0).astype(BF16) for e in range(N_EXPERTS)], axis=1)
        finish(jnp.dot(onehot, y_ref[...].reshape(N_EXPERTS * CAP, D_MODEL), preferred_element_type=F32))


def _combine_windows(first):
    last = jnp.concatenate([first[..., 1:], jnp.full(first.shape[:2] + (1,), CAP, jnp.int32)], axis=-1)
    start = jnp.minimum(first // SLOT_ALIGN * SLOT_ALIGN, CAP - COMBINE_WINDOW)
    short = jnp.all(last <= start + COMBINE_WINDOW, axis=1)
    return start.transpose(0, 2, 1).reshape(-1), short.astype(jnp.int32).reshape(-1)


def _combine(starts, short, slot_t, y, x, mod, final_g, layer, final):
    grid_spec = pltpu.PrefetchScalarGridSpec(
        num_scalar_prefetch=2,
        grid=(BATCH, SEQ // COMBINE_ROWS),
        in_specs=[
            pl.BlockSpec((1, COMBINE_ROWS, N_EXPERTS), lambda b, t, s, f: (b, t, 0)),
            pl.BlockSpec((N_EXPERTS, 1, CAP, D_MODEL), lambda b, t, s, f: (0, b, 0, 0)),
            pl.BlockSpec((1, COMBINE_ROWS, D_MODEL), lambda b, t, s, f: (b, t, 0)),
            pl.BlockSpec((1, 1, D_MODEL), lambda b, t, s, f: ((layer * 6 + 5) * BATCH + b, 0, 0)),
            pl.BlockSpec((1, D_MODEL), lambda b, t, s, f: (0, 0)),
        ],
        out_specs=pl.BlockSpec((1, COMBINE_ROWS, D_MODEL), lambda b, t, s, f: (b, t, 0)),
    )
    return pl.pallas_call(
        functools.partial(_combine_kernel, final=final),
        grid_spec=grid_spec,
        out_shape=jax.ShapeDtypeStruct((BATCH, SEQ, D_MODEL), F32),
        compiler_params=_params(("arbitrary", "arbitrary"), 48),
        name="moe_combine",
    )(starts, short, slot_t, y, x, mod, final_g.reshape(1, D_MODEL))


def _scaled_q_weight(w, q_columns):
    scale = jnp.concatenate([jnp.full((q_columns,), Q_SCALE, F32), jnp.ones((w.shape[1] - q_columns,), F32)])
    return (w * scale).astype(BF16)


def kernel(x, c, ada_w, ada_b, norm_g, na_w_qkv, na_w_o, na_rpb, sw_w_qkv, sw_w_o, sw_sinks, t5_bias,
           moe_w_router, moe_w_gate, moe_w_up, moe_w_down, final_g):
    mod = _ada(c, ada_w, ada_b)
    norm_g3 = norm_g.reshape(DEPTH * 2, 1, D_MODEL)
    x = x.reshape(BATCH * SEQ, D_MODEL)
    for layer in range(DEPTH):
        j = layer // N_MIXERS
        if layer % N_MIXERS == 0:
            qkv = _qkv(x, norm_g3, mod, _scaled_q_weight(na_w_qkv[j], NA_HEADS * HEAD_DIM), layer)
            o = _na_attention(qkv.reshape(BATCH, SEQ, -1), _na_rpb_rows(na_rpb[j]))
            w_o = na_w_o[j]
        else:
            qkv = _qkv(x, norm_g3, mod, _scaled_q_weight(sw_w_qkv[j], SW_Q_HEADS * HEAD_DIM), layer,
                       head_copies=2 * SW_KV_HEADS * HEAD_DIM // LANES)
            o = _sw_attention(qkv.reshape(BATCH, SEQ, -1), _sw_rel_table(t5_bias), sw_sinks[j])
            w_o = sw_w_o[j]
        w_router = moe_w_router[layer].astype(BF16)
        x, h, aff_t = _post_attn(o.reshape(BATCH * SEQ, -1), w_o.astype(BF16), x, norm_g3, mod,
                                 w_router.T, layer)
        slot, tok, gate, first = _route(aff_t)
        tok = tok.reshape(BATCH, N_EXPERTS, CAP)
        rows = tok.transpose(1, 0, 2) + (jnp.arange(BATCH, dtype=jnp.int32) * SEQ)[None, :, None]
        gates = gate.reshape(BATCH, N_EXPERTS, CAP).transpose(1, 2, 0)
        y = _experts(rows.reshape(-1), h, gates, moe_w_gate, moe_w_up, moe_w_down, layer)
        slot_t = slot.reshape(BATCH, N_EXPERTS, SEQ).transpose(0, 2, 1)
        starts, short = _combine_windows(first.reshape(BATCH, N_EXPERTS, LANES)[..., :SEQ // COMBINE_ROWS])
        x = _combine(starts, short, slot_t, y, x.reshape(BATCH, SEQ, D_MODEL), mod, final_g, layer,
                     final=layer == DEPTH - 1).reshape(BATCH * SEQ, D_MODEL)
    return x.reshape(BATCH, SEQ, D_MODEL)
```

```python
import functools

import numpy as np
import jax
import jax.numpy as jnp
from jax import lax
from jax.experimental import pallas as pl
from jax.experimental.pallas import tpu as pltpu

D_MODEL = 1024
BATCH = 8
SEQ = 2048
DEPTH = 2
GRID_W = 64
ROWS = SEQ // GRID_W
N_MIXERS = 2
HEAD_DIM = 64
NA_HEADS = 16
NA_WIN_H = 8
NA_WIN_W = 16
SW_Q_HEADS = 16
SW_KV_HEADS = 4
SW_GROUP = SW_Q_HEADS // SW_KV_HEADS
SW_WINDOW = 128
SW_BLOCK = 128
SW_NB = SEQ // SW_BLOCK
SW_SPAN = 3 * SW_BLOCK
T5_BUCKETS = 32
T5_MAX_DIST = 128
N_EXPERTS = 16
EXPERT_FF = 2048
EC_CAPACITY = 2
CAP = EC_CAPACITY * SEQ // N_EXPERTS
RMS_EPS = 1e-6
NEG = -1e30
LOG2E = 1.4426950408889634
Q_SCALE = HEAD_DIM ** -0.5 * LOG2E

LANES = 128
TOKEN_TILE_ROWS = D_MODEL // LANES
MIB = 1024 * 1024
F32 = jnp.float32
BF16 = jnp.bfloat16

ROW_TILE = 512
POST_TILE = 1024
ATTN_SEQS = 4
FF_TILE = 512
COMBINE_ROWS = 512
PREFIX_CHUNK = 256


def _params(semantics, vmem_mib):
    return pltpu.CompilerParams(dimension_semantics=semantics, vmem_limit_bytes=vmem_mib * MIB)


def _norm_mod(x, g, sc, sh):
    y = x * lax.rsqrt(jnp.mean(x * x, axis=-1, keepdims=True) + RMS_EPS)
    return (y * g) * (1.0 + sc) + sh


def _softmax0(z):
    z = z - jnp.max(z, axis=0, keepdims=True)
    p = jnp.exp(z)
    return p / jnp.sum(p, axis=0, keepdims=True)


def _half_masks(rows):
    lane = lax.broadcasted_iota(jnp.int32, (rows, LANES), 1)
    lo = jnp.where(lane < HEAD_DIM, 1.0, 0.0).astype(BF16)
    hi = jnp.where(lane < HEAD_DIM, 0.0, 1.0).astype(BF16)
    return lo, hi


def _ada_kernel(c_ref, w_ref, b_ref, o_ref):
    c = c_ref[...]
    act = (c * jax.nn.sigmoid(c)).astype(BF16)
    o_ref[0] = jnp.dot(act, w_ref[0].astype(BF16), preferred_element_type=F32) + b_ref[0]


def _ada(c, ada_w, ada_b):
    out = pl.pallas_call(
        _ada_kernel,
        grid=(DEPTH, 6),
        in_specs=[
            pl.BlockSpec((BATCH, D_MODEL), lambda l, k: (0, 0)),
            pl.BlockSpec((1, D_MODEL, D_MODEL), lambda l, k: (l, 0, k)),
            pl.BlockSpec((1, 1, D_MODEL), lambda l, k: (l * 6 + k, 0, 0)),
        ],
        out_specs=pl.BlockSpec((1, BATCH, D_MODEL), lambda l, k: (l * 6 + k, 0, 0)),
        out_shape=jax.ShapeDtypeStruct((DEPTH * 6, BATCH, D_MODEL), F32),
        compiler_params=_params(("arbitrary", "arbitrary"), 32),
        name="ada_mod",
    )(c, ada_w, ada_b.reshape(DEPTH * 6, 1, D_MODEL))
    return out.reshape(DEPTH * 6 * BATCH, 1, D_MODEL)


def _mod_spec(layer, chunk, row_tile=ROW_TILE):
    tiles_per_seq = SEQ // row_tile
    return pl.BlockSpec((1, 1, D_MODEL),
                        lambda i: ((layer * 6 + chunk) * BATCH + i // tiles_per_seq, 0, 0))


def _qkv_kernel(x_ref, g_ref, sc_ref, sh_ref, w_ref, o_ref, *, head_copies):
    h = _norm_mod(x_ref[...], g_ref[0], sc_ref[0], sh_ref[0])
    y = jnp.dot(h.astype(BF16), w_ref[...], preferred_element_type=F32)
    if head_copies:
        low_half = lax.broadcasted_iota(jnp.int32, (ROW_TILE, LANES), 1) < HEAD_DIM
        keep = y.shape[1] - head_copies * LANES
        tiles = [y[:, :keep]]
        for t in range(head_copies):
            pair = y[:, keep + t * LANES:keep + (t + 1) * LANES]
            swapped = pltpu.roll(pair, HEAD_DIM, 1)
            tiles += [jnp.where(low_half, pair, swapped), jnp.where(low_half, swapped, pair)]
        y = jnp.concatenate(tiles, axis=1)
    o_ref[...] = y.astype(BF16)


def _qkv(x, norm_g3, mod, w, layer, head_copies=0):
    n = w.shape[1] + head_copies * LANES
    return pl.pallas_call(
        functools.partial(_qkv_kernel, head_copies=head_copies),
        grid=(BATCH * SEQ // ROW_TILE,),
        in_specs=[
            pl.BlockSpec((ROW_TILE, D_MODEL), lambda i: (i, 0)),
            pl.BlockSpec((1, 1, D_MODEL), lambda i: (layer * 2, 0, 0)),
            _mod_spec(layer, 1),
            _mod_spec(layer, 0),
            pl.BlockSpec(w.shape, lambda i: (0, 0)),
        ],
        out_specs=pl.BlockSpec((ROW_TILE, n), lambda i: (i, 0)),
        out_shape=jax.ShapeDtypeStruct((BATCH * SEQ, n), BF16),
        compiler_params=_params(("arbitrary",), 48),
        name="norm_qkv",
    )(x, norm_g3, mod, mod, w)


NA_BIAS_ROWS = 2 * NA_WIN_H - 1
NA_ROW_UNROLL = 4


def _na_rpb_rows(rpb):
    w = NA_WIN_W - 1
    rpb = rpb.astype(F32) * LOG2E
    pad = jnp.zeros(rpb.shape[:2] + (LANES - 2 * w - 1,), F32)
    return jnp.concatenate([rpb[..., w:], pad, rpb[..., :w]], axis=-1)


def _na_build_bias(w_ref, bias_ref):
    lane = lax.broadcasted_iota(jnp.int32, (GRID_W, LANES), 1)
    col = lax.broadcasted_iota(jnp.int32, (GRID_W, LANES), 0)
    kc = jnp.bitwise_and(lane, GRID_W - 1)
    cstart = jnp.clip(col - NA_WIN_W // 2, 0, GRID_W - NA_WIN_W)
    in_window = jnp.where(kc >= cstart, jnp.where(kc < cstart + NA_WIN_W, 1.0, 0.0), 0.0) > 0.5
    low_half = lane < GRID_W

    def toeplitz(head, a, shift):
        row = jnp.broadcast_to(w_ref[head, a:a + 1, :], (GRID_W, LANES))
        return pltpu.roll(row, shift, 1, stride=1, stride_axis=0)

    for head in range(2):
        tiles = [jnp.where(in_window, jnp.where(low_half, toeplitz(head, a, 0), toeplitz(head, a + 1, GRID_W)), NEG)
                 for a in range(NA_BIAS_ROWS - 1)]
        for d in range(NA_WIN_H):
            for m in range(NA_WIN_H // 2):
                bias_ref[d, head * GRID_W:(head + 1) * GRID_W, m * LANES:(m + 1) * LANES] = (
                    tiles[NA_WIN_H - 1 - d + 2 * m])


def _na_kernel(w_ref, q_ref, k_ref, v_ref, o_ref, bias_ref, s0_ref, s1_ref, p0_ref, p1_ref):
    @pl.when(pl.program_id(1) == 0)
    def _():
        _na_build_bias(w_ref, bias_ref)

    lo, hi = _half_masks(GRID_W)
    lo_f = lax.broadcasted_iota(jnp.int32, (GRID_W, LANES), 1) < HEAD_DIM
    span = NA_WIN_H * GRID_W
    ones = jnp.ones((span, LANES), BF16)
    seq_groups = ROWS // NA_ROW_UNROLL
    groups = ATTN_SEQS * seq_groups

    def group_rows(i):
        g = jnp.clip(i, 0, groups - 1)
        for u in range(NA_ROW_UNROLL):
            r = lax.rem(g, seq_groups) * NA_ROW_UNROLL + u
            yield u, lax.div(g, seq_groups), r, jnp.clip(r - NA_WIN_H // 2, 0, ROWS - NA_WIN_H)

    def scores(i, s_ref):
        for u, b, r, rs in group_rows(i):
            q = q_ref[b, pl.ds(pl.multiple_of(r * GRID_W, GRID_W), GRID_W), :]
            lhs = jnp.concatenate([q * lo, q * hi], axis=0)
            kw = k_ref[b, pl.ds(pl.multiple_of(rs * GRID_W, GRID_W), span), :]
            s_ref[u] = lax.dot_general(lhs, kw, (((1,), (1,)), ((), ())), preferred_element_type=F32)

    def softmax(i, s_ref, p_ref):
        for u, b, r, rs in group_rows(i):
            s = s_ref[u] + bias_ref[r - rs]
            m = jnp.broadcast_to(jnp.max(s, axis=-1, keepdims=True), (2 * GRID_W, LANES))
            p_ref[u] = jnp.exp2(s - jnp.concatenate([m] * (span // LANES), axis=1)).astype(BF16)

    def values(i, p_ref):
        for u, b, r, rs in group_rows(i):
            vw = v_ref[b, pl.ds(pl.multiple_of(rs * GRID_W, GRID_W), span), :]
            o = jnp.dot(p_ref[u], jnp.concatenate([vw, ones], axis=1), preferred_element_type=F32)
            o = o[:, :LANES] / o[:, LANES:]
            out = jnp.where(lo_f, o[:GRID_W], o[GRID_W:])
            o_ref[b, pl.ds(pl.multiple_of(r * GRID_W, GRID_W), GRID_W), :] = out.astype(BF16)

    def step(i, carry):
        values(2 * i - 1, p1_ref)
        scores(2 * i + 1, s1_ref)
        softmax(2 * i, s0_ref, p0_ref)
        values(2 * i, p0_ref)
        scores(2 * i + 2, s0_ref)
        softmax(2 * i + 1, s1_ref, p1_ref)
        return carry

    p1_ref[...] = jnp.ones_like(p1_ref)
    scores(0, s0_ref)
    lax.fori_loop(0, groups // 2, step, 0)
    values(groups - 1, p1_ref)


def _na_attention(qkv, rpb_rows):
    pairs = NA_HEADS // 2
    return pl.pallas_call(
        _na_kernel,
        grid=(pairs, BATCH // ATTN_SEQS),
        in_specs=[
            pl.BlockSpec((2, NA_BIAS_ROWS, LANES), lambda p, b: (p, 0, 0)),
            pl.BlockSpec((ATTN_SEQS, SEQ, LANES), lambda p, b: (b, 0, p)),
            pl.BlockSpec((ATTN_SEQS, SEQ, LANES), lambda p, b: (b, 0, pairs + p)),
            pl.BlockSpec((ATTN_SEQS, SEQ, LANES), lambda p, b: (b, 0, 2 * pairs + p)),
        ],
        out_specs=pl.BlockSpec((ATTN_SEQS, SEQ, LANES), lambda p, b: (b, 0, p)),
        out_shape=jax.ShapeDtypeStruct((BATCH, SEQ, NA_HEADS * HEAD_DIM), BF16),
        scratch_shapes=[pltpu.VMEM((NA_WIN_H, 2 * GRID_W, NA_WIN_H * GRID_W), F32),
                        pltpu.VMEM((NA_ROW_UNROLL, 2 * GRID_W, NA_WIN_H * GRID_W), F32),
                        pltpu.VMEM((NA_ROW_UNROLL, 2 * GRID_W, NA_WIN_H * GRID_W), F32),
                        pltpu.VMEM((NA_ROW_UNROLL, 2 * GRID_W, NA_WIN_H * GRID_W), BF16),
                        pltpu.VMEM((NA_ROW_UNROLL, 2 * GRID_W, NA_WIN_H * GRID_W), BF16)],
        compiler_params=_params(("arbitrary", "arbitrary"), 32),
        name="na_attention",
    )(rpb_rows, qkv, qkv, qkv)


def _t5_buckets(rel):
    half = T5_BUCKETS // 2
    max_exact = half // 2
    n = np.abs(rel)
    large = max_exact + (np.log(np.maximum(n, 1) / max_exact)
                         / np.log(T5_MAX_DIST / max_exact) * (half - max_exact)).astype(np.int32)
    large = np.minimum(large, half - 1)
    return (rel > 0).astype(np.int32) * half + np.where(n < max_exact, n, large)


SW_REL_PERIOD = 512
SW_BLOCK_UNROLL = 2


def _sw_rel_table(t5_table):
    k = np.arange(SW_REL_PERIOD)
    rel = np.where(k < SW_REL_PERIOD // 2, k, k - SW_REL_PERIOD)
    ok = np.abs(rel) <= SW_WINDOW
    vals = jnp.where(ok[:, None], t5_table[_t5_buckets(rel)].astype(F32) * LOG2E, NEG)
    return vals.T.reshape(SW_KV_HEADS, SW_GROUP, SW_REL_PERIOD)


def _sw_build_bias(rel_ref, bias_ref):
    for g in range(SW_GROUP):
        row = jnp.broadcast_to(rel_ref[0, g:g + 1, :], (SW_BLOCK, SW_REL_PERIOD))
        t = pltpu.roll(row, 0, 1, stride=1, stride_axis=0)
        rows = slice(g * SW_BLOCK, (g + 1) * SW_BLOCK)
        bias_ref[0, rows, :] = t[:, :SW_SPAN]
        bias_ref[1, rows, :] = jnp.concatenate([t[:, SW_SPAN:], t[:, :2 * SW_BLOCK]], axis=1)
        bias_ref[2, rows, :] = jnp.concatenate([t[:, 2 * SW_BLOCK:], t[:, :SW_BLOCK]], axis=1)


def _sw_kernel(sink_ref, rel_ref, q_ref, k_ref, v_ref, o_ref, bias_ref, s0_ref, s1_ref, p0_ref, p1_ref,
               t0_ref, t1_ref):
    j = pl.program_id(0)
    ones = jnp.ones((SW_SPAN, LANES), BF16)

    @pl.when(pl.program_id(1) == 0)
    def _():
        _sw_build_bias(rel_ref, bias_ref)

    masks = _half_masks(SW_BLOCK)
    lo_f = lax.broadcasted_iota(jnp.int32, (SW_BLOCK, LANES), 1) < HEAD_DIM
    sinks = [jnp.full((SW_BLOCK, LANES), sink_ref[j * SW_GROUP + g] * LOG2E, F32) for g in range(SW_GROUP)]
    seq_groups = SW_NB // SW_BLOCK_UNROLL
    groups = ATTN_SEQS * seq_groups

    def group_blocks(i):
        g = jnp.clip(i, 0, groups - 1)
        for u in range(SW_BLOCK_UNROLL):
            n = lax.rem(g, seq_groups) * SW_BLOCK_UNROLL + u
            yield u, lax.div(g, seq_groups), n, jnp.clip(n - 1, 0, SW_NB - 3)

    def block_rows(n):
        return pl.ds(pl.multiple_of(n * SW_BLOCK, SW_BLOCK), SW_BLOCK)

    def scores(i, s_ref):
        for u, b, n, first in group_blocks(i):
            kw = k_ref[b, pl.ds(pl.multiple_of(first * SW_BLOCK, SW_BLOCK), SW_SPAN), :]
            for g in range(SW_GROUP):
                q = q_ref[b, block_rows(n), (g // 2) * LANES:(g // 2 + 1) * LANES] * masks[g % 2]
                s_ref[u * SW_GROUP + g] = lax.dot_general(q, kw, (((1,), (1,)), ((), ())),
                                                          preferred_element_type=F32)

    def softmax(i, s_ref, p_ref, t_ref):
        for u, b, n, first in group_blocks(i):
            for g in range(SW_GROUP):
                c = u * SW_GROUP + g
                s = s_ref[c] + bias_ref[n - first, g * SW_BLOCK:(g + 1) * SW_BLOCK, :]
                m = jnp.maximum(jnp.broadcast_to(jnp.max(s, axis=-1, keepdims=True), (SW_BLOCK, LANES)), sinks[g])
                p_ref[c] = jnp.exp2(s - jnp.concatenate([m] * (SW_SPAN // LANES), axis=1)).astype(BF16)
                t_ref[c] = jnp.exp2(sinks[g] - m)

    def values(i, p_ref, t_ref):
        for u, b, n, first in group_blocks(i):
            vw = v_ref[b, pl.ds(pl.multiple_of(first * SW_BLOCK, SW_BLOCK), SW_SPAN), :]
            v_ones = jnp.concatenate([vw, ones], axis=1)
            heads = []
            for g in range(SW_GROUP):
                c = u * SW_GROUP + g
                o = jnp.dot(p_ref[c], v_ones, preferred_element_type=F32)
                heads.append(o[:, :LANES] / (o[:, LANES:] + t_ref[c]))
            out = jnp.concatenate([jnp.where(lo_f, heads[0], heads[1]), jnp.where(lo_f, heads[2], heads[3])],
                                  axis=1)
            o_ref[b, block_rows(n), :] = out.astype(BF16)

    def step(i, carry):
        values(2 * i - 1, p1_ref, t1_ref)
        scores(2 * i + 1, s1_ref)
        softmax(2 * i, s0_ref, p0_ref, t0_ref)
        values(2 * i, p0_ref, t0_ref)
        scores(2 * i + 2, s0_ref)
        softmax(2 * i + 1, s1_ref, p1_ref, t1_ref)
        return carry

    p1_ref[...] = jnp.ones_like(p1_ref)
    t1_ref[...] = jnp.ones_like(t1_ref)
    scores(0, s0_ref)
    lax.fori_loop(0, groups // 2, step, 0)
    values(groups - 1, p1_ref, t1_ref)


def _sw_attention(qkv, rel_table, sinks):
    qw = SW_GROUP * HEAD_DIM
    k_off = SW_Q_HEADS * HEAD_DIM // LANES
    v_off = k_off + SW_KV_HEADS
    chains = SW_BLOCK_UNROLL * SW_GROUP
    return pl.pallas_call(
        _sw_kernel,
        grid=(SW_KV_HEADS, BATCH // ATTN_SEQS),
        in_specs=[
            pl.BlockSpec(memory_space=pltpu.SMEM),
            pl.BlockSpec((1, SW_GROUP, SW_REL_PERIOD), lambda j, b: (j, 0, 0)),
            pl.BlockSpec((ATTN_SEQS, SEQ, qw), lambda j, b: (b, 0, j)),
            pl.BlockSpec((ATTN_SEQS, SEQ, LANES), lambda j, b: (b, 0, k_off + j)),
            pl.BlockSpec((ATTN_SEQS, SEQ, LANES), lambda j, b: (b, 0, v_off + j)),
        ],
        out_specs=pl.BlockSpec((ATTN_SEQS, SEQ, qw), lambda j, b: (b, 0, j)),
        out_shape=jax.ShapeDtypeStruct((BATCH, SEQ, SW_Q_HEADS * HEAD_DIM), BF16),
        scratch_shapes=[pltpu.VMEM((3, SW_GROUP * SW_BLOCK, SW_SPAN), F32)]
        + [pltpu.VMEM((chains, SW_BLOCK, SW_SPAN), F32)] * 2
        + [pltpu.VMEM((chains, SW_BLOCK, SW_SPAN), BF16)] * 2
        + [pltpu.VMEM((chains, SW_BLOCK, LANES), F32)] * 2,
        compiler_params=_params(("arbitrary", "arbitrary"), 48),
        name="sw_attention",
    )(sinks, rel_table, qkv, qkv, qkv)


def _post_attn_kernel(o_ref, wo_ref, x_ref, gate_ref, g_ref, sc_ref, sh_ref, wr_ref,
                      xo_ref, h_ref, aff_ref):
    y = jnp.dot(o_ref[...], wo_ref[...], preferred_element_type=F32)
    xn = x_ref[...] + gate_ref[0] * y
    xo_ref[...] = xn
    h = _norm_mod(xn, g_ref[0], sc_ref[0], sh_ref[0])
    for j in range(TOKEN_TILE_ROWS):
        h_ref[pl.ds(j, POST_TILE, stride=TOKEN_TILE_ROWS), :] = h[:, j * LANES:(j + 1) * LANES]
    hb = h.astype(BF16)
    logits = lax.dot_general(wr_ref[...], hb, (((1,), (1,)), ((), ())), preferred_element_type=F32)
    aff_ref[...] = _softmax0(logits)


def _post_attn(o, w_o, x, norm_g3, mod, w_router_t, layer):
    rows = BATCH * SEQ
    return pl.pallas_call(
        _post_attn_kernel,
        grid=(rows // POST_TILE,),
        in_specs=[
            pl.BlockSpec((POST_TILE, D_MODEL), lambda i: (i, 0)),
            pl.BlockSpec((D_MODEL, D_MODEL), lambda i: (0, 0)),
            pl.BlockSpec((POST_TILE, D_MODEL), lambda i: (i, 0)),
            _mod_spec(layer, 2, POST_TILE),
            pl.BlockSpec((1, 1, D_MODEL), lambda i: (layer * 2 + 1, 0, 0)),
            _mod_spec(layer, 4, POST_TILE),
            _mod_spec(layer, 3, POST_TILE),
            pl.BlockSpec((N_EXPERTS, D_MODEL), lambda i: (0, 0)),
        ],
        out_specs=[
            pl.BlockSpec((POST_TILE, D_MODEL), lambda i: (i, 0)),
            pl.BlockSpec((POST_TILE * TOKEN_TILE_ROWS, LANES), lambda i: (i, 0)),
            pl.BlockSpec((N_EXPERTS, POST_TILE), lambda i: (0, i)),
        ],
        out_shape=[
            jax.ShapeDtypeStruct((rows, D_MODEL), F32),
            jax.ShapeDtypeStruct((rows * TOKEN_TILE_ROWS, LANES), F32),
            jax.ShapeDtypeStruct((N_EXPERTS, rows), F32),
        ],
        compiler_params=_params(("arbitrary",), 56),
        name="post_attn",
    )(o, w_o, x, mod, norm_g3, mod, mod, w_router_t)


ROUTE_SEQS = BATCH
ROUTE_ROWS = ROUTE_SEQS * N_EXPERTS
ROUTE_BITS = SEQ.bit_length() - 1
ROUTE_VALID = 1 << (2 * ROUTE_BITS)
F32_INF_BITS = 0x7F800000


def _prefix_count(x):
    rows = x.shape[0]
    nchunk = SEQ // PREFIX_CHUNK
    r = lax.broadcasted_iota(jnp.int32, (PREFIX_CHUNK, PREFIX_CHUNK), 0)
    c = lax.broadcasted_iota(jnp.int32, (PREFIX_CHUNK, PREFIX_CHUNK), 1)
    upper = jnp.where(r < c, 1.0, 0.0).astype(BF16)
    chunks = [x[:, k * PREFIX_CHUNK:(k + 1) * PREFIX_CHUNK] for k in range(nchunk)]
    local = jnp.dot(jnp.concatenate(chunks, axis=0).astype(BF16), upper, preferred_element_type=F32)
    out = []
    offset = jnp.zeros((rows, 1), F32)
    for k in range(nchunk):
        out.append(local[k * rows:(k + 1) * rows] + offset)
        offset = offset + jnp.sum(chunks[k], axis=1, keepdims=True)
    return jnp.concatenate(out, axis=1)


def _route_kernel(aff_ref, slot_ref, tok_ref, gate_ref, first_ref):
    aff = jnp.concatenate([aff_ref[:, b * SEQ:(b + 1) * SEQ] for b in range(ROUTE_SEQS)], axis=0)
    bits = pltpu.bitcast(aff, jnp.int32)

    def count(mask):
        return jnp.sum(jnp.where(mask, 1.0, 0.0), axis=1, keepdims=True)

    def search(_, bounds):
        lo, hi = bounds
        mid = lo + ((hi - lo) >> 1)
        ge = count(bits >= mid) >= CAP
        return jnp.where(ge, mid, lo), jnp.where(ge, hi, mid)

    lo0 = jnp.zeros((ROUTE_ROWS, 1), jnp.int32)
    hi0 = jnp.full((ROUTE_ROWS, 1), F32_INF_BITS, jnp.int32)
    tau, _ = lax.fori_loop(0, F32_INF_BITS.bit_length(), search, (lo0, hi0))
    gt = bits > tau
    eq = jnp.where(bits == tau, 1.0, 0.0)
    need = CAP - count(gt)
    sel = jnp.where(gt, 1.0, jnp.where(_prefix_count(eq) < need, eq, 0.0))
    pos = _prefix_count(sel).astype(jnp.int32)
    chosen = sel > 0.5
    slot_ref[...] = jnp.where(chosen, pos, -1)
    tile_lane = lax.broadcasted_iota(jnp.int32, (ROUTE_ROWS, LANES), 1)
    first = jnp.zeros((ROUTE_ROWS, LANES), jnp.int32)
    for k in range(SEQ // COMBINE_ROWS):
        first = jnp.where(tile_lane == k, pos[:, k * COMBINE_ROWS:k * COMBINE_ROWS + 1], first)
    first_ref[...] = first
    lane = lax.broadcasted_iota(jnp.int32, (ROUTE_ROWS, SEQ), 1)
    packed = jnp.where(chosen, ROUTE_VALID | (lane << ROUTE_BITS) | (lane - pos), 0)
    gate = aff
    for k in range(ROUTE_BITS):
        step = 1 << k
        from_right = pltpu.roll(packed, SEQ - step, 1)
        gate_right = pltpu.roll(gate, SEQ - step, 1)
        arrives = (from_right & step) != 0
        stays = (packed & step) == 0
        packed = jnp.where(arrives, from_right, jnp.where(stays, packed, 0))
        gate = jnp.where(arrives, gate_right, gate)
    tok_ref[...] = (packed[:, :CAP] >> ROUTE_BITS) & (SEQ - 1)
    gate_ref[...] = gate[:, :CAP]


def _route(aff_t):
    return pl.pallas_call(
        _route_kernel,
        grid=(BATCH // ROUTE_SEQS,),
        in_specs=[pl.BlockSpec((N_EXPERTS, ROUTE_SEQS * SEQ), lambda i: (0, i))],
        out_specs=[pl.BlockSpec((ROUTE_ROWS, SEQ), lambda i: (i, 0)),
                   pl.BlockSpec((ROUTE_ROWS, CAP), lambda i: (i, 0)),
                   pl.BlockSpec((ROUTE_ROWS, CAP), lambda i: (i, 0)),
                   pl.BlockSpec((ROUTE_ROWS, LANES), lambda i: (i, 0))],
        out_shape=[jax.ShapeDtypeStruct((BATCH * N_EXPERTS, SEQ), jnp.int32),
                   jax.ShapeDtypeStruct((BATCH * N_EXPERTS, CAP), jnp.int32),
                   jax.ShapeDtypeStruct((BATCH * N_EXPERTS, CAP), F32),
                   jax.ShapeDtypeStruct((BATCH * N_EXPERTS, LANES), jnp.int32)],
        compiler_params=_params(("arbitrary",), 48),
        name="route",
    )(aff_t)


GATHER_ROWS = BATCH * CAP
FF_STEPS = EXPERT_FF // FF_TILE
CHUNK_SEQS = 4
ROWS_PER_CHUNK = GATHER_ROWS * CHUNK_SEQS // (FF_STEPS * BATCH)
PROLOGUE_UNROLL = 8


def _expert_kernel(tok_ref, h_ref, gates_ref, wg_ref, wu_ref, wd_ref, y_ref, x_ref, sem, xb_ref, acc_ref):
    e = pl.program_id(0)
    f = pl.program_id(1)
    slot = lax.rem(e, 2)
    next_slot = 1 - slot
    next_e = jnp.minimum(e + 1, N_EXPERTS - 1)

    tile = TOKEN_TILE_ROWS

    def row_copy(expert, row, dst_slot):
        src = pl.multiple_of(tok_ref[expert * GATHER_ROWS + row] * tile, tile)
        dst = pl.multiple_of(row * tile, tile)
        return pltpu.make_async_copy(h_ref.at[pl.ds(src, tile), :], x_ref.at[dst_slot, pl.ds(dst, tile), :],
                                     sem.at[dst_slot])

    def wait_rows(dst_slot):
        pltpu.make_async_copy(h_ref.at[pl.ds(0, GATHER_ROWS * tile), :], x_ref.at[dst_slot],
                              sem.at[dst_slot]).wait()

    @pl.when(jnp.logical_and(e == 0, f == 0))
    def _():
        def issue(i, carry):
            for u in range(PROLOGUE_UNROLL):
                row_copy(0, i * PROLOGUE_UNROLL + u, 0).start()
            return carry
        lax.fori_loop(0, GATHER_ROWS // PROLOGUE_UNROLL, issue, 0)
        acc_ref[...] = jnp.zeros_like(acc_ref)

    @pl.when(f == 0)
    def _():
        wait_rows(slot)
        for b in range(BATCH):
            cols = [x_ref[slot, pl.ds(b * CAP * tile + j, CAP, stride=tile), :] for j in range(tile)]
            xb_ref[b] = jnp.concatenate(cols, axis=1).astype(BF16)

    wg = wg_ref[0, 0].astype(BF16)
    wu = wu_ref[0, 0].astype(BF16)
    wd = wd_ref[0, 0].astype(BF16)
    for c in range(BATCH // CHUNK_SEQS):
        first = (f * (BATCH // CHUNK_SEQS) + c) * ROWS_PER_CHUNK
        for j in range(ROWS_PER_CHUNK):
            row_copy(next_e, first + j, next_slot).start()
        seqs = slice(c * CHUNK_SEQS, (c + 1) * CHUNK_SEQS)
        x = xb_ref[seqs].reshape(CHUNK_SEQS * CAP, D_MODEL)
        a = jnp.dot(x, wg, preferred_element_type=F32)
        u = jnp.dot(x, wu, preferred_element_type=F32)
        act = ((a * jax.nn.sigmoid(a)) * u).astype(BF16)
        carried = jnp.where(f == 0, 0.0, acc_ref[seqs])
        part = jnp.dot(act, wd, preferred_element_type=F32).reshape(CHUNK_SEQS, CAP, D_MODEL)
        acc_ref[seqs] = carried + part

    @pl.when(f == FF_STEPS - 1)
    def _():
        gates = gates_ref[0]
        for b in range(BATCH):
            y_ref[0, b] = (acc_ref[b] * gates[:, b:b + 1]).astype(BF16)

    @pl.when(jnp.logical_and(e == N_EXPERTS - 1, f == FF_STEPS - 1))
    def _():
        wait_rows(next_slot)


def _experts(tok, h, gates, w_gate, w_up, w_down, layer):
    grid_spec = pltpu.PrefetchScalarGridSpec(
        num_scalar_prefetch=1,
        grid=(N_EXPERTS, FF_STEPS),
        in_specs=[
            pl.BlockSpec(memory_space=pl.ANY),
            pl.BlockSpec((1, CAP, BATCH), lambda e, f, tok: (e, 0, 0)),
            pl.BlockSpec((1, 1, D_MODEL, FF_TILE), lambda e, f, tok: (layer, e, 0, f)),
            pl.BlockSpec((1, 1, D_MODEL, FF_TILE), lambda e, f, tok: (layer, e, 0, f)),
            pl.BlockSpec((1, 1, FF_TILE, D_MODEL), lambda e, f, tok: (layer, e, f, 0)),
        ],
        out_specs=pl.BlockSpec((1, BATCH, CAP, D_MODEL), lambda e, f, tok: (e, 0, 0, 0)),
        scratch_shapes=[pltpu.VMEM((2, GATHER_ROWS * TOKEN_TILE_ROWS, LANES), F32), pltpu.SemaphoreType.DMA((2,)),
                        pltpu.VMEM((BATCH, CAP, D_MODEL), BF16), pltpu.VMEM((BATCH, CAP, D_MODEL), F32)],
    )
    return pl.pallas_call(
        _expert_kernel,
        grid_spec=grid_spec,
        out_shape=jax.ShapeDtypeStruct((N_EXPERTS, BATCH, CAP, D_MODEL), BF16),
        compiler_params=_params(("arbitrary", "arbitrary"), 60),
        name="moe_experts",
    )(tok, h, gates, w_gate, w_up, w_down)


COMBINE_WINDOW = 128
SLOT_ALIGN = 16


def _combine_kernel(start_ref, short_ref, slot_ref, y_ref, x_ref, gate_ref, g_ref, o_ref, *, final):
    tile = pl.program_id(0) * (SEQ // COMBINE_ROWS) + pl.program_id(1)
    slots = slot_ref[0]

    def finish(moe):
        x = x_ref[0] + gate_ref[0] * moe
        if final:
            x = (x * lax.rsqrt(jnp.mean(x * x, axis=-1, keepdims=True) + RMS_EPS)) * g_ref[...]
        o_ref[0] = x

    @pl.when(short_ref[tile] == 1)
    def _():
        lane = lax.broadcasted_iota(jnp.int32, (COMBINE_ROWS, COMBINE_WINDOW), 1)
        hits, rows = [], []
        for e in range(N_EXPERTS):
            start = pl.multiple_of(start_ref[tile * N_EXPERTS + e], SLOT_ALIGN)
            hits.append(jnp.where(slots[:, e:e + 1] - start == lane, 1.0, 0.0).astype(BF16))
            rows.append(y_ref[e, 0, pl.ds(start, COMBINE_WINDOW), :])
        finish(jnp.dot(jnp.concatenate(hits, axis=1), jnp.concatenate(rows, axis=0), preferred_element_type=F32))

    @pl.when(short_ref[tile] != 1)
    def _():
        lane = lax.broadcasted_iota(jnp.int32, (COMBINE_ROWS, CAP), 1)
        onehot = jnp.concatenate(
            [jnp.where(slots[:, e:e + 1] == lane, 1.0, 0.0).astype(BF16) for e in range(N_EXPERTS)], axis=1)
        finish(jnp.dot(onehot, y_ref[...].reshape(N_EXPERTS * CAP, D_MODEL), preferred_element_type=F32))


def _combine_windows(first):
    last = jnp.concatenate([first[..., 1:], jnp.full(first.shape[:2] + (1,), CAP, jnp.int32)], axis=-1)
    start = jnp.minimum(first // SLOT_ALIGN * SLOT_ALIGN, CAP - COMBINE_WINDOW)
    short = jnp.all(last <= start + COMBINE_WINDOW, axis=1)
    return start.transpose(0, 2, 1).reshape(-1), short.astype(jnp.int32).reshape(-1)


def _combine(starts, short, slot_t, y, x, mod, final_g, layer, final):
    grid_spec = pltpu.PrefetchScalarGridSpec(
        num_scalar_prefetch=2,
        grid=(BATCH, SEQ // COMBINE_ROWS),
        in_specs=[
            pl.BlockSpec((1, COMBINE_ROWS, N_EXPERTS), lambda b, t, s, f: (b, t, 0)),
            pl.BlockSpec((N_EXPERTS, 1, CAP, D_MODEL), lambda b, t, s, f: (0, b, 0, 0)),
            pl.BlockSpec((1, COMBINE_ROWS, D_MODEL), lambda b, t, s, f: (b, t, 0)),
            pl.BlockSpec((1, 1, D_MODEL), lambda b, t, s, f: ((layer * 6 + 5) * BATCH + b, 0, 0)),
            pl.BlockSpec((1, D_MODEL), lambda b, t, s, f: (0, 0)),
        ],
        out_specs=pl.BlockSpec((1, COMBINE_ROWS, D_MODEL), lambda b, t, s, f: (b, t, 0)),
    )
    return pl.pallas_call(
        functools.partial(_combine_kernel, final=final),
        grid_spec=grid_spec,
        out_shape=jax.ShapeDtypeStruct((BATCH, SEQ, D_MODEL), F32),
        compiler_params=_params(("arbitrary", "arbitrary"), 48),
        name="moe_combine",
    )(starts, short, slot_t, y, x, mod, final_g.reshape(1, D_MODEL))


def _scaled_q_weight(w, q_columns):
    scale = jnp.concatenate([jnp.full((q_columns,), Q_SCALE, F32), jnp.ones((w.shape[1] - q_columns,), F32)])
    return (w * scale).astype(BF16)


def kernel(x, c, ada_w, ada_b, norm_g, na_w_qkv, na_w_o, na_rpb, sw_w_qkv, sw_w_o, sw_sinks, t5_bias,
           moe_w_router, moe_w_gate, moe_w_up, moe_w_down, final_g):
    mod = _ada(c, ada_w, ada_b)
    norm_g3 = norm_g.reshape(DEPTH * 2, 1, D_MODEL)
    x = x.reshape(BATCH * SEQ, D_MODEL)
    for layer in range(DEPTH):
        j = layer // N_MIXERS
        if layer % N_MIXERS == 0:
            qkv = _qkv(x, norm_g3, mod, _scaled_q_weight(na_w_qkv[j], NA_HEADS * HEAD_DIM), layer)
            o = _na_attention(qkv.reshape(BATCH, SEQ, -1), _na_rpb_rows(na_rpb[j]))
            w_o = na_w_o[j]
        else:
            qkv = _qkv(x, norm_g3, mod, _scaled_q_weight(sw_w_qkv[j], SW_Q_HEADS * HEAD_DIM), layer,
                       head_copies=2 * SW_KV_HEADS * HEAD_DIM // LANES)
            o = _sw_attention(qkv.reshape(BATCH, SEQ, -1), _sw_rel_table(t5_bias), sw_sinks[j])
            w_o = sw_w_o[j]
        w_router = moe_w_router[layer].astype(BF16)
        x, h, aff_t = _post_attn(o.reshape(BATCH * SEQ, -1), w_o.astype(BF16), x, norm_g3, mod,
                                 w_router.T, layer)
        slot, tok, gate, first = _route(aff_t)
        tok = tok.reshape(BATCH, N_EXPERTS, CAP)
        rows = tok.transpose(1, 0, 2) + (jnp.arange(BATCH, dtype=jnp.int32) * SEQ)[None, :, None]
        gates = gate.reshape(BATCH, N_EXPERTS, CAP).transpose(1, 2, 0)
        y = _experts(rows.reshape(-1), h, gates, moe_w_gate, moe_w_up, moe_w_down, layer)
        slot_t = slot.reshape(BATCH, N_EXPERTS, SEQ).transpose(0, 2, 1)
        starts, short = _combine_windows(first.reshape(BATCH, N_EXPERTS, LANES)[..., :SEQ // COMBINE_ROWS])
        x = _combine(starts, short, slot_t, y, x.reshape(BATCH, SEQ, D_MODEL), mod, final_g, layer,
                     final=layer == DEPTH - 1).reshape(BATCH * SEQ, D_MODEL)
    return x.reshape(BATCH, SEQ, D_MODEL)
```

```python
import functools

import numpy as np
import jax
import jax.numpy as jnp
from jax import lax
from jax.experimental import pallas as pl
from jax.experimental.pallas import tpu as pltpu

D_MODEL = 1024
BATCH = 8
SEQ = 2048
DEPTH = 2
GRID_W = 64
ROWS = SEQ // GRID_W
N_MIXERS = 2
HEAD_DIM = 64
NA_HEADS = 16
NA_WIN_H = 8
NA_WIN_W = 16
SW_Q_HEADS = 16
SW_KV_HEADS = 4
SW_GROUP = SW_Q_HEADS // SW_KV_HEADS
SW_WINDOW = 128
SW_BLOCK = 128
SW_NB = SEQ // SW_BLOCK
SW_SPAN = 3 * SW_BLOCK
T5_BUCKETS = 32
T5_MAX_DIST = 128
N_EXPERTS = 16
EXPERT_FF = 2048
EC_CAPACITY = 2
CAP = EC_CAPACITY * SEQ // N_EXPERTS
RMS_EPS = 1e-6
NEG = -1e30
LOG2E = 1.4426950408889634
Q_SCALE = HEAD_DIM ** -0.5 * LOG2E

LANES = 128
TOKEN_TILE_ROWS = D_MODEL // LANES
MIB = 1024 * 1024
F32 = jnp.float32
BF16 = jnp.bfloat16

ROW_TILE = 1024
POST_TILE = 1024
ATTN_SEQS = 4
FF_TILE = 512
COMBINE_ROWS = 512
PREFIX_CHUNK = 256


def _params(semantics, vmem_mib):
    return pltpu.CompilerParams(dimension_semantics=semantics, vmem_limit_bytes=vmem_mib * MIB)


def _norm_mod(x, g, sc, sh):
    y = x * lax.rsqrt(jnp.mean(x * x, axis=-1, keepdims=True) + RMS_EPS)
    return (y * g) * (1.0 + sc) + sh


def _softmax0(z):
    z = z - jnp.max(z, axis=0, keepdims=True)
    p = jnp.exp(z)
    return p / jnp.sum(p, axis=0, keepdims=True)


def _half_masks(rows):
    lane = lax.broadcasted_iota(jnp.int32, (rows, LANES), 1)
    lo = jnp.where(lane < HEAD_DIM, 1.0, 0.0).astype(BF16)
    hi = jnp.where(lane < HEAD_DIM, 0.0, 1.0).astype(BF16)
    return lo, hi


def _ada_kernel(c_ref, w_ref, b_ref, o_ref):
    c = c_ref[...]
    act = (c * jax.nn.sigmoid(c)).astype(BF16)
    o_ref[0] = jnp.dot(act, w_ref[0].astype(BF16), preferred_element_type=F32) + b_ref[0]


def _ada(c, ada_w, ada_b):
    out = pl.pallas_call(
        _ada_kernel,
        grid=(DEPTH, 6),
        in_specs=[
            pl.BlockSpec((BATCH, D_MODEL), lambda l, k: (0, 0)),
            pl.BlockSpec((1, D_MODEL, D_MODEL), lambda l, k: (l, 0, k)),
            pl.BlockSpec((1, 1, D_MODEL), lambda l, k: (l * 6 + k, 0, 0)),
        ],
        out_specs=pl.BlockSpec((1, BATCH, D_MODEL), lambda l, k: (l * 6 + k, 0, 0)),
        out_shape=jax.ShapeDtypeStruct((DEPTH * 6, BATCH, D_MODEL), F32),
        compiler_params=_params(("arbitrary", "arbitrary"), 32),
        name="ada_mod",
    )(c, ada_w, ada_b.reshape(DEPTH * 6, 1, D_MODEL))
    return out.reshape(DEPTH * 6 * BATCH, 1, D_MODEL)


def _mod_spec(layer, chunk, row_tile=ROW_TILE):
    tiles_per_seq = SEQ // row_tile
    return pl.BlockSpec((1, 1, D_MODEL),
                        lambda i: ((layer * 6 + chunk) * BATCH + i // tiles_per_seq, 0, 0))


def _qkv_kernel(x_ref, g_ref, sc_ref, sh_ref, w_ref, o_ref, *, head_copies):
    h = _norm_mod(x_ref[...], g_ref[0], sc_ref[0], sh_ref[0])
    y = jnp.dot(h.astype(BF16), w_ref[...], preferred_element_type=F32)
    if head_copies:
        low_half = lax.broadcasted_iota(jnp.int32, (ROW_TILE, LANES), 1) < HEAD_DIM
        keep = y.shape[1] - head_copies * LANES
        tiles = [y[:, :keep]]
        for t in range(head_copies):
            pair = y[:, keep + t * LANES:keep + (t + 1) * LANES]
            swapped = pltpu.roll(pair, HEAD_DIM, 1)
            tiles += [jnp.where(low_half, pair, swapped), jnp.where(low_half, swapped, pair)]
        y = jnp.concatenate(tiles, axis=1)
    o_ref[...] = y.astype(BF16)


def _qkv(x, norm_g3, mod, w, layer, head_copies=0):
    n = w.shape[1] + head_copies * LANES
    return pl.pallas_call(
        functools.partial(_qkv_kernel, head_copies=head_copies),
        grid=(BATCH * SEQ // ROW_TILE,),
        in_specs=[
            pl.BlockSpec((ROW_TILE, D_MODEL), lambda i: (i, 0)),
            pl.BlockSpec((1, 1, D_MODEL), lambda i: (layer * 2, 0, 0)),
            _mod_spec(layer, 1),
            _mod_spec(layer, 0),
            pl.BlockSpec(w.shape, lambda i: (0, 0)),
        ],
        out_specs=pl.BlockSpec((ROW_TILE, n), lambda i: (i, 0)),
        out_shape=jax.ShapeDtypeStruct((BATCH * SEQ, n), BF16),
        compiler_params=_params(("arbitrary",), 60),
        name="norm_qkv",
    )(x, norm_g3, mod, mod, w)


NA_BIAS_ROWS = 2 * NA_WIN_H - 1
NA_ROW_UNROLL = 4


def _na_rpb_rows(rpb):
    w = NA_WIN_W - 1
    rpb = rpb.astype(F32) * LOG2E
    pad = jnp.zeros(rpb.shape[:2] + (LANES - 2 * w - 1,), F32)
    return jnp.concatenate([rpb[..., w:], pad, rpb[..., :w]], axis=-1)


def _na_build_bias(w_ref, bias_ref):
    lane = lax.broadcasted_iota(jnp.int32, (GRID_W, LANES), 1)
    col = lax.broadcasted_iota(jnp.int32, (GRID_W, LANES), 0)
    kc = jnp.bitwise_and(lane, GRID_W - 1)
    cstart = jnp.clip(col - NA_WIN_W // 2, 0, GRID_W - NA_WIN_W)
    in_window = jnp.where(kc >= cstart, jnp.where(kc < cstart + NA_WIN_W, 1.0, 0.0), 0.0) > 0.5
    low_half = lane < GRID_W

    def toeplitz(head, a, shift):
        row = jnp.broadcast_to(w_ref[head, a:a + 1, :], (GRID_W, LANES))
        return pltpu.roll(row, shift, 1, stride=1, stride_axis=0)

    for head in range(2):
        tiles = [jnp.where(in_window, jnp.where(low_half, toeplitz(head, a, 0), toeplitz(head, a + 1, GRID_W)), NEG)
                 for a in range(NA_BIAS_ROWS - 1)]
        for d in range(NA_WIN_H):
            for m in range(NA_WIN_H // 2):
                bias_ref[d, head * GRID_W:(head + 1) * GRID_W, m * LANES:(m + 1) * LANES] = (
                    tiles[NA_WIN_H - 1 - d + 2 * m])


def _na_kernel(w_ref, q_ref, k_ref, v_ref, o_ref, bias_ref, s0_ref, s1_ref, p0_ref, p1_ref):
    @pl.when(pl.program_id(1) == 0)
    def _():
        _na_build_bias(w_ref, bias_ref)

    lo, hi = _half_masks(GRID_W)
    lo_f = lax.broadcasted_iota(jnp.int32, (GRID_W, LANES), 1) < HEAD_DIM
    span = NA_WIN_H * GRID_W
    ones = jnp.ones((span, LANES), BF16)
    seq_groups = ROWS // NA_ROW_UNROLL
    groups = ATTN_SEQS * seq_groups

    def group_rows(i):
        g = jnp.clip(i, 0, groups - 1)
        for u in range(NA_ROW_UNROLL):
            r = lax.rem(g, seq_groups) * NA_ROW_UNROLL + u
            yield u, lax.div(g, seq_groups), r, jnp.clip(r - NA_WIN_H // 2, 0, ROWS - NA_WIN_H)

    def scores(i, s_ref):
        for u, b, r, rs in group_rows(i):
            q = q_ref[b, pl.ds(pl.multiple_of(r * GRID_W, GRID_W), GRID_W), :]
            lhs = jnp.concatenate([q * lo, q * hi], axis=0)
            kw = k_ref[b, pl.ds(pl.multiple_of(rs * GRID_W, GRID_W), span), :]
            s_ref[u] = lax.dot_general(lhs, kw, (((1,), (1,)), ((), ())), preferred_element_type=F32)

    def softmax(i, s_ref, p_ref):
        for u, b, r, rs in group_rows(i):
            s = s_ref[u] + bias_ref[r - rs]
            m = jnp.broadcast_to(jnp.max(s, axis=-1, keepdims=True), (2 * GRID_W, LANES))
            p_ref[u] = jnp.exp2(s - jnp.concatenate([m] * (span // LANES), axis=1)).astype(BF16)

    def values(i, p_ref):
        for u, b, r, rs in group_rows(i):
            vw = v_ref[b, pl.ds(pl.multiple_of(rs * GRID_W, GRID_W), span), :]
            o = jnp.dot(p_ref[u], jnp.concatenate([vw, ones], axis=1), preferred_element_type=F32)
            o = o[:, :LANES] / o[:, LANES:]
            out = jnp.where(lo_f, o[:GRID_W], o[GRID_W:])
            o_ref[b, pl.ds(pl.multiple_of(r * GRID_W, GRID_W), GRID_W), :] = out.astype(BF16)

    def step(i, carry):
        values(2 * i - 1, p1_ref)
        scores(2 * i + 1, s1_ref)
        softmax(2 * i, s0_ref, p0_ref)
        values(2 * i, p0_ref)
        scores(2 * i + 2, s0_ref)
        softmax(2 * i + 1, s1_ref, p1_ref)
        return carry

    p1_ref[...] = jnp.ones_like(p1_ref)
    scores(0, s0_ref)
    lax.fori_loop(0, groups // 2, step, 0)
    values(groups - 1, p1_ref)


def _na_attention(qkv, rpb_rows):
    pairs = NA_HEADS // 2
    return pl.pallas_call(
        _na_kernel,
        grid=(pairs, BATCH // ATTN_SEQS),
        in_specs=[
            pl.BlockSpec((2, NA_BIAS_ROWS, LANES), lambda p, b: (p, 0, 0)),
            pl.BlockSpec((ATTN_SEQS, SEQ, LANES), lambda p, b: (b, 0, p)),
            pl.BlockSpec((ATTN_SEQS, SEQ, LANES), lambda p, b: (b, 0, pairs + p)),
            pl.BlockSpec((ATTN_SEQS, SEQ, LANES), lambda p, b: (b, 0, 2 * pairs + p)),
        ],
        out_specs=pl.BlockSpec((ATTN_SEQS, SEQ, LANES), lambda p, b: (b, 0, p)),
        out_shape=jax.ShapeDtypeStruct((BATCH, SEQ, NA_HEADS * HEAD_DIM), BF16),
        scratch_shapes=[pltpu.VMEM((NA_WIN_H, 2 * GRID_W, NA_WIN_H * GRID_W), F32),
                        pltpu.VMEM((NA_ROW_UNROLL, 2 * GRID_W, NA_WIN_H * GRID_W), F32),
                        pltpu.VMEM((NA_ROW_UNROLL, 2 * GRID_W, NA_WIN_H * GRID_W), F32),
                        pltpu.VMEM((NA_ROW_UNROLL, 2 * GRID_W, NA_WIN_H * GRID_W), BF16),
                        pltpu.VMEM((NA_ROW_UNROLL, 2 * GRID_W, NA_WIN_H * GRID_W), BF16)],
        compiler_params=_params(("arbitrary", "arbitrary"), 32),
        name="na_attention",
    )(rpb_rows, qkv, qkv, qkv)


def _t5_buckets(rel):
    half = T5_BUCKETS // 2
    max_exact = half // 2
    n = np.abs(rel)
    large = max_exact + (np.log(np.maximum(n, 1) / max_exact)
                         / np.log(T5_MAX_DIST / max_exact) * (half - max_exact)).astype(np.int32)
    large = np.minimum(large, half - 1)
    return (rel > 0).astype(np.int32) * half + np.where(n < max_exact, n, large)


SW_REL_PERIOD = 512
SW_BLOCK_UNROLL = 2


def _sw_rel_table(t5_table):
    k = np.arange(SW_REL_PERIOD)
    rel = np.where(k < SW_REL_PERIOD // 2, k, k - SW_REL_PERIOD)
    ok = np.abs(rel) <= SW_WINDOW
    vals = jnp.where(ok[:, None], t5_table[_t5_buckets(rel)].astype(F32) * LOG2E, NEG)
    return vals.T.reshape(SW_KV_HEADS, SW_GROUP, SW_REL_PERIOD)


def _sw_build_bias(rel_ref, bias_ref):
    for g in range(SW_GROUP):
        row = jnp.broadcast_to(rel_ref[0, g:g + 1, :], (SW_BLOCK, SW_REL_PERIOD))
        t = pltpu.roll(row, 0, 1, stride=1, stride_axis=0)
        rows = slice(g * SW_BLOCK, (g + 1) * SW_BLOCK)
        bias_ref[0, rows, :] = t[:, :SW_SPAN]
        bias_ref[1, rows, :] = jnp.concatenate([t[:, SW_SPAN:], t[:, :2 * SW_BLOCK]], axis=1)
        bias_ref[2, rows, :] = jnp.concatenate([t[:, 2 * SW_BLOCK:], t[:, :SW_BLOCK]], axis=1)


def _sw_kernel(sink_ref, rel_ref, q_ref, k_ref, v_ref, o_ref, bias_ref, s0_ref, s1_ref, p0_ref, p1_ref,
               t0_ref, t1_ref):
    j = pl.program_id(0)
    ones = jnp.ones((SW_SPAN, LANES), BF16)

    @pl.when(pl.program_id(1) == 0)
    def _():
        _sw_build_bias(rel_ref, bias_ref)

    masks = _half_masks(SW_BLOCK)
    lo_f = lax.broadcasted_iota(jnp.int32, (SW_BLOCK, LANES), 1) < HEAD_DIM
    sinks = [jnp.full((SW_BLOCK, LANES), sink_ref[j * SW_GROUP + g] * LOG2E, F32) for g in range(SW_GROUP)]
    seq_groups = SW_NB // SW_BLOCK_UNROLL
    groups = ATTN_SEQS * seq_groups

    def group_blocks(i):
        g = jnp.clip(i, 0, groups - 1)
        for u in range(SW_BLOCK_UNROLL):
            n = lax.rem(g, seq_groups) * SW_BLOCK_UNROLL + u
            yield u, lax.div(g, seq_groups), n, jnp.clip(n - 1, 0, SW_NB - 3)

    def block_rows(n):
        return pl.ds(pl.multiple_of(n * SW_BLOCK, SW_BLOCK), SW_BLOCK)

    def scores(i, s_ref):
        for u, b, n, first in group_blocks(i):
            kw = k_ref[b, pl.ds(pl.multiple_of(first * SW_BLOCK, SW_BLOCK), SW_SPAN), :]
            for g in range(SW_GROUP):
                q = q_ref[b, block_rows(n), (g // 2) * LANES:(g // 2 + 1) * LANES] * masks[g % 2]
                s_ref[u * SW_GROUP + g] = lax.dot_general(q, kw, (((1,), (1,)), ((), ())),
                                                          preferred_element_type=F32)

    def softmax(i, s_ref, p_ref, t_ref):
        for u, b, n, first in group_blocks(i):
            for g in range(SW_GROUP):
                c = u * SW_GROUP + g
                s = s_ref[c] + bias_ref[n - first, g * SW_BLOCK:(g + 1) * SW_BLOCK, :]
                m = jnp.maximum(jnp.broadcast_to(jnp.max(s, axis=-1, keepdims=True), (SW_BLOCK, LANES)), sinks[g])
                p_ref[c] = jnp.exp2(s - jnp.concatenate([m] * (SW_SPAN // LANES), axis=1)).astype(BF16)
                t_ref[c] = jnp.exp2(sinks[g] - m)

    def values(i, p_ref, t_ref):
        for u, b, n, first in group_blocks(i):
            vw = v_ref[b, pl.ds(pl.multiple_of(first * SW_BLOCK, SW_BLOCK), SW_SPAN), :]
            v_ones = jnp.concatenate([vw, ones], axis=1)
            heads = []
            for g in range(SW_GROUP):
                c = u * SW_GROUP + g
                o = jnp.dot(p_ref[c], v_ones, preferred_element_type=F32)
                heads.append(o[:, :LANES] / (o[:, LANES:] + t_ref[c]))
            out = jnp.concatenate([jnp.where(lo_f, heads[0], heads[1]), jnp.where(lo_f, heads[2], heads[3])],
                                  axis=1)
            o_ref[b, block_rows(n), :] = out.astype(BF16)

    def step(i, carry):
        values(2 * i - 1, p1_ref, t1_ref)
        scores(2 * i + 1, s1_ref)
        softmax(2 * i, s0_ref, p0_ref, t0_ref)
        values(2 * i, p0_ref, t0_ref)
        scores(2 * i + 2, s0_ref)
        softmax(2 * i + 1, s1_ref, p1_ref, t1_ref)
        return carry

    p1_ref[...] = jnp.ones_like(p1_ref)
    t1_ref[...] = jnp.ones_like(t1_ref)
    scores(0, s0_ref)
    lax.fori_loop(0, groups // 2, step, 0)
    values(groups - 1, p1_ref, t1_ref)


def _sw_attention(qkv, rel_table, sinks):
    qw = SW_GROUP * HEAD_DIM
    k_off = SW_Q_HEADS * HEAD_DIM // LANES
    v_off = k_off + SW_KV_HEADS
    chains = SW_BLOCK_UNROLL * SW_GROUP
    return pl.pallas_call(
        _sw_kernel,
        grid=(SW_KV_HEADS, BATCH // ATTN_SEQS),
        in_specs=[
            pl.BlockSpec(memory_space=pltpu.SMEM),
            pl.BlockSpec((1, SW_GROUP, SW_REL_PERIOD), lambda j, b: (j, 0, 0)),
            pl.BlockSpec((ATTN_SEQS, SEQ, qw), lambda j, b: (b, 0, j)),
            pl.BlockSpec((ATTN_SEQS, SEQ, LANES), lambda j, b: (b, 0, k_off + j)),
            pl.BlockSpec((ATTN_SEQS, SEQ, LANES), lambda j, b: (b, 0, v_off + j)),
        ],
        out_specs=pl.BlockSpec((ATTN_SEQS, SEQ, qw), lambda j, b: (b, 0, j)),
        out_shape=jax.ShapeDtypeStruct((BATCH, SEQ, SW_Q_HEADS * HEAD_DIM), BF16),
        scratch_shapes=[pltpu.VMEM((3, SW_GROUP * SW_BLOCK, SW_SPAN), F32)]
        + [pltpu.VMEM((chains, SW_BLOCK, SW_SPAN), F32)] * 2
        + [pltpu.VMEM((chains, SW_BLOCK, SW_SPAN), BF16)] * 2
        + [pltpu.VMEM((chains, SW_BLOCK, LANES), F32)] * 2,
        compiler_params=_params(("arbitrary", "arbitrary"), 48),
        name="sw_attention",
    )(sinks, rel_table, qkv, qkv, qkv)


def _post_attn_kernel(o_ref, wo_ref, x_ref, gate_ref, g_ref, sc_ref, sh_ref, wr_ref,
                      xo_ref, h_ref, aff_ref):
    y = jnp.dot(o_ref[...], wo_ref[...], preferred_element_type=F32)
    xn = x_ref[...] + gate_ref[0] * y
    xo_ref[...] = xn
    h = _norm_mod(xn, g_ref[0], sc_ref[0], sh_ref[0])
    for j in range(TOKEN_TILE_ROWS):
        h_ref[pl.ds(j, POST_TILE, stride=TOKEN_TILE_ROWS), :] = h[:, j * LANES:(j + 1) * LANES]
    hb = h.astype(BF16)
    logits = lax.dot_general(wr_ref[...], hb, (((1,), (1,)), ((), ())), preferred_element_type=F32)
    aff_ref[...] = _softmax0(logits)


def _post_attn(o, w_o, x, norm_g3, mod, w_router_t, layer):
    rows = BATCH * SEQ
    return pl.pallas_call(
        _post_attn_kernel,
        grid=(rows // POST_TILE,),
        in_specs=[
            pl.BlockSpec((POST_TILE, D_MODEL), lambda i: (i, 0)),
            pl.BlockSpec((D_MODEL, D_MODEL), lambda i: (0, 0)),
            pl.BlockSpec((POST_TILE, D_MODEL), lambda i: (i, 0)),
            _mod_spec(layer, 2, POST_TILE),
            pl.BlockSpec((1, 1, D_MODEL), lambda i: (layer * 2 + 1, 0, 0)),
            _mod_spec(layer, 4, POST_TILE),
            _mod_spec(layer, 3, POST_TILE),
            pl.BlockSpec((N_EXPERTS, D_MODEL), lambda i: (0, 0)),
        ],
        out_specs=[
            pl.BlockSpec((POST_TILE, D_MODEL), lambda i: (i, 0)),
            pl.BlockSpec((POST_TILE * TOKEN_TILE_ROWS, LANES), lambda i: (i, 0)),
            pl.BlockSpec((N_EXPERTS, POST_TILE), lambda i: (0, i)),
        ],
        out_shape=[
            jax.ShapeDtypeStruct((rows, D_MODEL), F32),
            jax.ShapeDtypeStruct((rows * TOKEN_TILE_ROWS, LANES), F32),
            jax.ShapeDtypeStruct((N_EXPERTS, rows), F32),
        ],
        compiler_params=_params(("arbitrary",), 56),
        name="post_attn",
    )(o, w_o, x, mod, norm_g3, mod, mod, w_router_t)


ROUTE_SEQS = BATCH
ROUTE_ROWS = ROUTE_SEQS * N_EXPERTS
ROUTE_BITS = SEQ.bit_length() - 1
ROUTE_VALID = 1 << (2 * ROUTE_BITS)
F32_INF_BITS = 0x7F800000


def _prefix_count(x):
    rows = x.shape[0]
    nchunk = SEQ // PREFIX_CHUNK
    r = lax.broadcasted_iota(jnp.int32, (PREFIX_CHUNK, PREFIX_CHUNK), 0)
    c = lax.broadcasted_iota(jnp.int32, (PREFIX_CHUNK, PREFIX_CHUNK), 1)
    upper = jnp.where(r < c, 1.0, 0.0).astype(BF16)
    chunks = [x[:, k * PREFIX_CHUNK:(k + 1) * PREFIX_CHUNK] for k in range(nchunk)]
    local = jnp.dot(jnp.concatenate(chunks, axis=0).astype(BF16), upper, preferred_element_type=F32)
    out = []
    offset = jnp.zeros((rows, 1), F32)
    for k in range(nchunk):
        out.append(local[k * rows:(k + 1) * rows] + offset)
        offset = offset + jnp.sum(chunks[k], axis=1, keepdims=True)
    return jnp.concatenate(out, axis=1)


def _route_kernel(aff_ref, slot_ref, tok_ref, gate_ref, first_ref):
    aff = jnp.concatenate([aff_ref[:, b * SEQ:(b + 1) * SEQ] for b in range(ROUTE_SEQS)], axis=0)
    bits = pltpu.bitcast(aff, jnp.int32)

    def count(mask):
        return jnp.sum(jnp.where(mask, 1.0, 0.0), axis=1, keepdims=True)

    def search(_, bounds):
        lo, hi = bounds
        mid = lo + ((hi - lo) >> 1)
        ge = count(bits >= mid) >= CAP
        return jnp.where(ge, mid, lo), jnp.where(ge, hi, mid)

    lo0 = jnp.zeros((ROUTE_ROWS, 1), jnp.int32)
    hi0 = jnp.full((ROUTE_ROWS, 1), F32_INF_BITS, jnp.int32)
    tau, _ = lax.fori_loop(0, F32_INF_BITS.bit_length(), search, (lo0, hi0))
    gt = bits > tau
    eq = jnp.where(bits == tau, 1.0, 0.0)
    need = CAP - count(gt)
    sel = jnp.where(gt, 1.0, jnp.where(_prefix_count(eq) < need, eq, 0.0))
    pos = _prefix_count(sel).astype(jnp.int32)
    chosen = sel > 0.5
    slot_ref[...] = jnp.where(chosen, pos, -1)
    tile_lane = lax.broadcasted_iota(jnp.int32, (ROUTE_ROWS, LANES), 1)
    first = jnp.zeros((ROUTE_ROWS, LANES), jnp.int32)
    for k in range(SEQ // COMBINE_ROWS):
        first = jnp.where(tile_lane == k, pos[:, k * COMBINE_ROWS:k * COMBINE_ROWS + 1], first)
    first_ref[...] = first
    lane = lax.broadcasted_iota(jnp.int32, (ROUTE_ROWS, SEQ), 1)
    packed = jnp.where(chosen, ROUTE_VALID | (lane << ROUTE_BITS) | (lane - pos), 0)
    gate = aff
    for k in range(ROUTE_BITS):
        step = 1 << k
        from_right = pltpu.roll(packed, SEQ - step, 1)
        gate_right = pltpu.roll(gate, SEQ - step, 1)
        arrives = (from_right & step) != 0
        stays = (packed & step) == 0
        packed = jnp.where(arrives, from_right, jnp.where(stays, packed, 0))
        gate = jnp.where(arrives, gate_right, gate)
    tok_ref[...] = (packed[:, :CAP] >> ROUTE_BITS) & (SEQ - 1)
    gate_ref[...] = gate[:, :CAP]


def _route(aff_t):
    return pl.pallas_call(
        _route_kernel,
        grid=(BATCH // ROUTE_SEQS,),
        in_specs=[pl.BlockSpec((N_EXPERTS, ROUTE_SEQS * SEQ), lambda i: (0, i))],
        out_specs=[pl.BlockSpec((ROUTE_ROWS, SEQ), lambda i: (i, 0)),
                   pl.BlockSpec((ROUTE_ROWS, CAP), lambda i: (i, 0)),
                   pl.BlockSpec((ROUTE_ROWS, CAP), lambda i: (i, 0)),
                   pl.BlockSpec((ROUTE_ROWS, LANES), lambda i: (i, 0))],
        out_shape=[jax.ShapeDtypeStruct((BATCH * N_EXPERTS, SEQ), jnp.int32),
                   jax.ShapeDtypeStruct((BATCH * N_EXPERTS, CAP), jnp.int32),
                   jax.ShapeDtypeStruct((BATCH * N_EXPERTS, CAP), F32),
                   jax.ShapeDtypeStruct((BATCH * N_EXPERTS, LANES), jnp.int32)],
        compiler_params=_params(("arbitrary",), 48),
        name="route",
    )(aff_t)


GATHER_ROWS = BATCH * CAP
FF_STEPS = EXPERT_FF // FF_TILE
CHUNK_SEQS = 4
ROWS_PER_CHUNK = GATHER_ROWS * CHUNK_SEQS // (FF_STEPS * BATCH)
PROLOGUE_UNROLL = 8


def _expert_kernel(tok_ref, h_ref, gates_ref, wg_ref, wu_ref, wd_ref, y_ref, x_ref, sem, xb_ref, acc_ref):
    e = pl.program_id(0)
    f = pl.program_id(1)
    slot = lax.rem(e, 2)
    next_slot = 1 - slot
    next_e = jnp.minimum(e + 1, N_EXPERTS - 1)

    tile = TOKEN_TILE_ROWS

    def row_copy(expert, row, dst_slot):
        src = pl.multiple_of(tok_ref[expert * GATHER_ROWS + row] * tile, tile)
        dst = pl.multiple_of(row * tile, tile)
        return pltpu.make_async_copy(h_ref.at[pl.ds(src, tile), :], x_ref.at[dst_slot, pl.ds(dst, tile), :],
                                     sem.at[dst_slot])

    def wait_rows(dst_slot):
        pltpu.make_async_copy(h_ref.at[pl.ds(0, GATHER_ROWS * tile), :], x_ref.at[dst_slot],
                              sem.at[dst_slot]).wait()

    @pl.when(jnp.logical_and(e == 0, f == 0))
    def _():
        def issue(i, carry):
            for u in range(PROLOGUE_UNROLL):
                row_copy(0, i * PROLOGUE_UNROLL + u, 0).start()
            return carry
        lax.fori_loop(0, GATHER_ROWS // PROLOGUE_UNROLL, issue, 0)
        acc_ref[...] = jnp.zeros_like(acc_ref)

    @pl.when(f == 0)
    def _():
        wait_rows(slot)
        for b in range(BATCH):
            cols = [x_ref[slot, pl.ds(b * CAP * tile + j, CAP, stride=tile), :] for j in range(tile)]
            xb_ref[b] = jnp.concatenate(cols, axis=1).astype(BF16)

    wg = wg_ref[0, 0].astype(BF16)
    wu = wu_ref[0, 0].astype(BF16)
    wd = wd_ref[0, 0].astype(BF16)
    for c in range(BATCH // CHUNK_SEQS):
        first = (f * (BATCH // CHUNK_SEQS) + c) * ROWS_PER_CHUNK
        for j in range(ROWS_PER_CHUNK):
            row_copy(next_e, first + j, next_slot).start()
        seqs = slice(c * CHUNK_SEQS, (c + 1) * CHUNK_SEQS)
        x = xb_ref[seqs].reshape(CHUNK_SEQS * CAP, D_MODEL)
        a = jnp.dot(x, wg, preferred_element_type=F32)
        u = jnp.dot(x, wu, preferred_element_type=F32)
        act = ((a * jax.nn.sigmoid(a)) * u).astype(BF16)
        carried = jnp.where(f == 0, 0.0, acc_ref[seqs])
        part = jnp.dot(act, wd, preferred_element_type=F32).reshape(CHUNK_SEQS, CAP, D_MODEL)
        acc_ref[seqs] = carried + part

    @pl.when(f == FF_STEPS - 1)
    def _():
        gates = gates_ref[0]
        for b in range(BATCH):
            y_ref[0, b] = (acc_ref[b] * gates[:, b:b + 1]).astype(BF16)

    @pl.when(jnp.logical_and(e == N_EXPERTS - 1, f == FF_STEPS - 1))
    def _():
        wait_rows(next_slot)


def _experts(tok, h, gates, w_gate, w_up, w_down, layer):
    grid_spec = pltpu.PrefetchScalarGridSpec(
        num_scalar_prefetch=1,
        grid=(N_EXPERTS, FF_STEPS),
        in_specs=[
            pl.BlockSpec(memory_space=pl.ANY),
            pl.BlockSpec((1, CAP, BATCH), lambda e, f, tok: (e, 0, 0)),
            pl.BlockSpec((1, 1, D_MODEL, FF_TILE), lambda e, f, tok: (layer, e, 0, f)),
            pl.BlockSpec((1, 1, D_MODEL, FF_TILE), lambda e, f, tok: (layer, e, 0, f)),
            pl.BlockSpec((1, 1, FF_TILE, D_MODEL), lambda e, f, tok: (layer, e, f, 0)),
        ],
        out_specs=pl.BlockSpec((1, BATCH, CAP, D_MODEL), lambda e, f, tok: (e, 0, 0, 0)),
        scratch_shapes=[pltpu.VMEM((2, GATHER_ROWS * TOKEN_TILE_ROWS, LANES), F32), pltpu.SemaphoreType.DMA((2,)),
                        pltpu.VMEM((BATCH, CAP, D_MODEL), BF16), pltpu.VMEM((BATCH, CAP, D_MODEL), F32)],
    )
    return pl.pallas_call(
        _expert_kernel,
        grid_spec=grid_spec,
        out_shape=jax.ShapeDtypeStruct((N_EXPERTS, BATCH, CAP, D_MODEL), BF16),
        compiler_params=_params(("arbitrary", "arbitrary"), 60),
        name="moe_experts",
    )(tok, h, gates, w_gate, w_up, w_down)


COMBINE_WINDOW = 128
SLOT_ALIGN = 16


def _combine_kernel(start_ref, short_ref, slot_ref, y_ref, x_ref, gate_ref, g_ref, o_ref, *, final):
    tile = pl.program_id(0) * (SEQ // COMBINE_ROWS) + pl.program_id(1)
    slots = slot_ref[0]

    def finish(moe):
        x = x_ref[0] + gate_ref[0] * moe
        if final:
            x = (x * lax.rsqrt(jnp.mean(x * x, axis=-1, keepdims=True) + RMS_EPS)) * g_ref[...]
        o_ref[0] = x

    @pl.when(short_ref[tile] == 1)
    def _():
        lane = lax.broadcasted_iota(jnp.int32, (COMBINE_ROWS, COMBINE_WINDOW), 1)
        hits, rows = [], []
        for e in range(N_EXPERTS):
            start = pl.multiple_of(start_ref[tile * N_EXPERTS + e], SLOT_ALIGN)
            hits.append(jnp.where(slots[:, e:e + 1] - start == lane, 1.0, 0.0).astype(BF16))
            rows.append(y_ref[e, 0, pl.ds(start, COMBINE_WINDOW), :])
        finish(jnp.dot(jnp.concatenate(hits, axis=1), jnp.concatenate(rows, axis=0), preferred_element_type=F32))

    @pl.when(short_ref[tile] != 1)
    def _():
        lane = lax.broadcasted_iota(jnp.int32, (COMBINE_ROWS, CAP), 1)
        onehot = jnp.concatenate(
            [jnp.where(slots[:, e:e + 1] == lane, 1.0, 0.0).astype(BF16) for e in range(N_EXPERTS)], axis=1)
        finish(jnp.dot(onehot, y_ref[...].reshape(N_EXPERTS * CAP, D_MODEL), preferred_element_type=F32))


def _combine_windows(first):
    last = jnp.concatenate([first[..., 1:], jnp.full(first.shape[:2] + (1,), CAP, jnp.int32)], axis=-1)
    start = jnp.minimum(first // SLOT_ALIGN * SLOT_ALIGN, CAP - COMBINE_WINDOW)
    short = jnp.all(last <= start + COMBINE_WINDOW, axis=1)
    return start.transpose(0, 2, 1).reshape(-1), short.astype(jnp.int32).reshape(-1)


def _combine(starts, short, slot_t, y, x, mod, final_g, layer, final):
    grid_spec = pltpu.PrefetchScalarGridSpec(
        num_scalar_prefetch=2,
        grid=(BATCH, SEQ // COMBINE_ROWS),
        in_specs=[
            pl.BlockSpec((1, COMBINE_ROWS, N_EXPERTS), lambda b, t, s, f: (b, t, 0)),
            pl.BlockSpec((N_EXPERTS, 1, CAP, D_MODEL), lambda b, t, s, f: (0, b, 0, 0)),
            pl.BlockSpec((1, COMBINE_ROWS, D_MODEL), lambda b, t, s, f: (b, t, 0)),
            pl.BlockSpec((1, 1, D_MODEL), lambda b, t, s, f: ((layer * 6 + 5) * BATCH + b, 0, 0)),
            pl.BlockSpec((1, D_MODEL), lambda b, t, s, f: (0, 0)),
        ],
        out_specs=pl.BlockSpec((1, COMBINE_ROWS, D_MODEL), lambda b, t, s, f: (b, t, 0)),
    )
    return pl.pallas_call(
        functools.partial(_combine_kernel, final=final),
        grid_spec=grid_spec,
        out_shape=jax.ShapeDtypeStruct((BATCH, SEQ, D_MODEL), F32),
        compiler_params=_params(("arbitrary", "arbitrary"), 48),
        name="moe_combine",
    )(starts, short, slot_t, y, x, mod, final_g.reshape(1, D_MODEL))


def _scaled_q_weight(w, q_columns):
    scale = jnp.concatenate([jnp.full((q_columns,), Q_SCALE, F32), jnp.ones((w.shape[1] - q_columns,), F32)])
    return (w * scale).astype(BF16)


def kernel(x, c, ada_w, ada_b, norm_g, na_w_qkv, na_w_o, na_rpb, sw_w_qkv, sw_w_o, sw_sinks, t5_bias,
           moe_w_router, moe_w_gate, moe_w_up, moe_w_down, final_g):
    mod = _ada(c, ada_w, ada_b)
    norm_g3 = norm_g.reshape(DEPTH * 2, 1, D_MODEL)
    x = x.reshape(BATCH * SEQ, D_MODEL)
    for layer in range(DEPTH):
        j = layer // N_MIXERS
        if layer % N_MIXERS == 0:
            qkv = _qkv(x, norm_g3, mod, _scaled_q_weight(na_w_qkv[j], NA_HEADS * HEAD_DIM), layer)
            o = _na_attention(qkv.reshape(BATCH, SEQ, -1), _na_rpb_rows(na_rpb[j]))
            w_o = na_w_o[j]
        else:
            qkv = _qkv(x, norm_g3, mod, _scaled_q_weight(sw_w_qkv[j], SW_Q_HEADS * HEAD_DIM), layer,
                       head_copies=2 * SW_KV_HEADS * HEAD_DIM // LANES)
            o = _sw_attention(qkv.reshape(BATCH, SEQ, -1), _sw_rel_table(t5_bias), sw_sinks[j])
            w_o = sw_w_o[j]
        w_router = moe_w_router[layer].astype(BF16)
        x, h, aff_t = _post_attn(o.reshape(BATCH * SEQ, -1), w_o.astype(BF16), x, norm_g3, mod,
                                 w_router.T, layer)
        slot, tok, gate, first = _route(aff_t)
        tok = tok.reshape(BATCH, N_EXPERTS, CAP)
        rows = tok.transpose(1, 0, 2) + (jnp.arange(BATCH, dtype=jnp.int32) * SEQ)[None, :, None]
        gates = gate.reshape(BATCH, N_EXPERTS, CAP).transpose(1, 2, 0)
        y = _experts(rows.reshape(-1), h, gates, moe_w_gate, moe_w_up, moe_w_down, layer)
        slot_t = slot.reshape(BATCH, N_EXPERTS, SEQ).transpose(0, 2, 1)
        starts, short = _combine_windows(first.reshape(BATCH, N_EXPERTS, LANES)[..., :SEQ // COMBINE_ROWS])
        x = _combine(starts, short, slot_t, y, x.reshape(BATCH, SEQ, D_MODEL), mod, final_g, layer,
                     final=layer == DEPTH - 1).reshape(BATCH * SEQ, D_MODEL)
    return x.reshape(BATCH, SEQ, D_MODEL)
```

```python
import functools

import numpy as np
import jax
import jax.numpy as jnp
from jax import lax
from jax.experimental import pallas as pl
from jax.experimental.pallas import tpu as pltpu

D_MODEL = 1024
BATCH = 8
SEQ = 2048
DEPTH = 2
GRID_W = 64
ROWS = SEQ // GRID_W
N_MIXERS = 2
HEAD_DIM = 64
NA_HEADS = 16
NA_WIN_H = 8
NA_WIN_W = 16
SW_Q_HEADS = 16
SW_KV_HEADS = 4
SW_GROUP = SW_Q_HEADS // SW_KV_HEADS
SW_WINDOW = 128
SW_BLOCK = 128
SW_NB = SEQ // SW_BLOCK
SW_SPAN = 3 * SW_BLOCK
T5_BUCKETS = 32
T5_MAX_DIST = 128
N_EXPERTS = 16
EXPERT_FF = 2048
EC_CAPACITY = 2
CAP = EC_CAPACITY * SEQ // N_EXPERTS
RMS_EPS = 1e-6
NEG = -1e30
LOG2E = 1.4426950408889634
Q_SCALE = HEAD_DIM ** -0.5 * LOG2E

LANES = 128
TOKEN_TILE_ROWS = D_MODEL // LANES
MIB = 1024 * 1024
F32 = jnp.float32
BF16 = jnp.bfloat16

ROW_TILE = 1024
POST_TILE = 1024
ATTN_SEQS = 4
FF_TILE = 512
COMBINE_ROWS = 512
PREFIX_CHUNK = 256


def _params(semantics, vmem_mib):
    return pltpu.CompilerParams(dimension_semantics=semantics, vmem_limit_bytes=vmem_mib * MIB)


def _norm_mod(x, g, sc, sh):
    y = x * lax.rsqrt(jnp.mean(x * x, axis=-1, keepdims=True) + RMS_EPS)
    return (y * g) * (1.0 + sc) + sh


def _softmax0(z):
    z = z - jnp.max(z, axis=0, keepdims=True)
    p = jnp.exp(z)
    return p / jnp.sum(p, axis=0, keepdims=True)


def _half_masks(rows):
    lane = lax.broadcasted_iota(jnp.int32, (rows, LANES), 1)
    lo = jnp.where(lane < HEAD_DIM, 1.0, 0.0).astype(BF16)
    hi = jnp.where(lane < HEAD_DIM, 0.0, 1.0).astype(BF16)
    return lo, hi


def _ada_kernel(c_ref, w_ref, b_ref, o_ref):
    c = c_ref[...]
    act = (c * jax.nn.sigmoid(c)).astype(BF16)
    o_ref[0] = jnp.dot(act, w_ref[0].astype(BF16), preferred_element_type=F32) + b_ref[0]


def _ada(c, ada_w, ada_b):
    out = pl.pallas_call(
        _ada_kernel,
        grid=(DEPTH, 6),
        in_specs=[
            pl.BlockSpec((BATCH, D_MODEL), lambda l, k: (0, 0)),
            pl.BlockSpec((1, D_MODEL, D_MODEL), lambda l, k: (l, 0, k)),
            pl.BlockSpec((1, 1, D_MODEL), lambda l, k: (l * 6 + k, 0, 0)),
        ],
        out_specs=pl.BlockSpec((1, BATCH, D_MODEL), lambda l, k: (l * 6 + k, 0, 0)),
        out_shape=jax.ShapeDtypeStruct((DEPTH * 6, BATCH, D_MODEL), F32),
        compiler_params=_params(("arbitrary", "arbitrary"), 32),
        name="ada_mod",
    )(c, ada_w, ada_b.reshape(DEPTH * 6, 1, D_MODEL))
    return out.reshape(DEPTH * 6 * BATCH, 1, D_MODEL)


def _mod_spec(layer, chunk, row_tile=ROW_TILE):
    tiles_per_seq = SEQ // row_tile
    return pl.BlockSpec((1, 1, D_MODEL),
                        lambda i: ((layer * 6 + chunk) * BATCH + i // tiles_per_seq, 0, 0))


def _qkv_kernel(x_ref, g_ref, sc_ref, sh_ref, w_ref, o_ref, *, head_copies):
    h = _norm_mod(x_ref[...], g_ref[0], sc_ref[0], sh_ref[0])
    y = jnp.dot(h.astype(BF16), w_ref[...], preferred_element_type=F32)
    if head_copies:
        low_half = lax.broadcasted_iota(jnp.int32, (ROW_TILE, LANES), 1) < HEAD_DIM
        keep = y.shape[1] - head_copies * LANES
        tiles = [y[:, :keep]]
        for t in range(head_copies):
            pair = y[:, keep + t * LANES:keep + (t + 1) * LANES]
            swapped = pltpu.roll(pair, HEAD_DIM, 1)
            tiles += [jnp.where(low_half, pair, swapped), jnp.where(low_half, swapped, pair)]
        y = jnp.concatenate(tiles, axis=1)
    o_ref[...] = y.astype(BF16)


def _qkv(x, norm_g3, mod, w, layer, head_copies=0):
    n = w.shape[1] + head_copies * LANES
    return pl.pallas_call(
        functools.partial(_qkv_kernel, head_copies=head_copies),
        grid=(BATCH * SEQ // ROW_TILE,),
        in_specs=[
            pl.BlockSpec((ROW_TILE, D_MODEL), lambda i: (i, 0)),
            pl.BlockSpec((1, 1, D_MODEL), lambda i: (layer * 2, 0, 0)),
            _mod_spec(layer, 1),
            _mod_spec(layer, 0),
            pl.BlockSpec(w.shape, lambda i: (0, 0)),
        ],
        out_specs=pl.BlockSpec((ROW_TILE, n), lambda i: (i, 0)),
        out_shape=jax.ShapeDtypeStruct((BATCH * SEQ, n), BF16),
        compiler_params=_params(("arbitrary",), 60),
        name="norm_qkv",
    )(x, norm_g3, mod, mod, w)


NA_BIAS_ROWS = 2 * NA_WIN_H - 1
NA_ROW_UNROLL = 4


def _na_rpb_rows(rpb):
    w = NA_WIN_W - 1
    rpb = rpb.astype(F32) * LOG2E
    pad = jnp.zeros(rpb.shape[:2] + (LANES - 2 * w - 1,), F32)
    return jnp.concatenate([rpb[..., w:], pad, rpb[..., :w]], axis=-1)


def _na_build_bias(w_ref, bias_ref):
    lane = lax.broadcasted_iota(jnp.int32, (GRID_W, LANES), 1)
    col = lax.broadcasted_iota(jnp.int32, (GRID_W, LANES), 0)
    kc = jnp.bitwise_and(lane, GRID_W - 1)
    cstart = jnp.clip(col - NA_WIN_W // 2, 0, GRID_W - NA_WIN_W)
    in_window = jnp.where(kc >= cstart, jnp.where(kc < cstart + NA_WIN_W, 1.0, 0.0), 0.0) > 0.5
    low_half = lane < GRID_W

    def toeplitz(head, a, shift):
        row = jnp.broadcast_to(w_ref[head, a:a + 1, :], (GRID_W, LANES))
        return pltpu.roll(row, shift, 1, stride=1, stride_axis=0)

    for head in range(2):
        tiles = [jnp.where(in_window, jnp.where(low_half, toeplitz(head, a, 0), toeplitz(head, a + 1, GRID_W)), NEG)
                 for a in range(NA_BIAS_ROWS - 1)]
        for d in range(NA_WIN_H):
            for m in range(NA_WIN_H // 2):
                bias_ref[d, head * GRID_W:(head + 1) * GRID_W, m * LANES:(m + 1) * LANES] = (
                    tiles[NA_WIN_H - 1 - d + 2 * m])


def _na_kernel(w_ref, q_ref, k_ref, v_ref, o_ref, bias_ref, s0_ref, s1_ref, p0_ref, p1_ref):
    @pl.when(pl.program_id(1) == 0)
    def _():
        _na_build_bias(w_ref, bias_ref)

    lo, hi = _half_masks(GRID_W)
    lo_f = lax.broadcasted_iota(jnp.int32, (GRID_W, LANES), 1) < HEAD_DIM
    span = NA_WIN_H * GRID_W
    ones = jnp.ones((span, LANES), BF16)
    seq_groups = ROWS // NA_ROW_UNROLL
    groups = ATTN_SEQS * seq_groups

    def group_rows(i):
        g = jnp.clip(i, 0, groups - 1)
        for u in range(NA_ROW_UNROLL):
            r = lax.rem(g, seq_groups) * NA_ROW_UNROLL + u
            yield u, lax.div(g, seq_groups), r, jnp.clip(r - NA_WIN_H // 2, 0, ROWS - NA_WIN_H)

    def scores(i, s_ref):
        for u, b, r, rs in group_rows(i):
            q = q_ref[b, pl.ds(pl.multiple_of(r * GRID_W, GRID_W), GRID_W), :]
            lhs = jnp.concatenate([q * lo, q * hi], axis=0)
            kw = k_ref[b, pl.ds(pl.multiple_of(rs * GRID_W, GRID_W), span), :]
            s_ref[u] = lax.dot_general(lhs, kw, (((1,), (1,)), ((), ())), preferred_element_type=F32)

    def softmax(i, s_ref, p_ref):
        for u, b, r, rs in group_rows(i):
            s = s_ref[u] + bias_ref[r - rs]
            m = jnp.broadcast_to(jnp.max(s, axis=-1, keepdims=True), (2 * GRID_W, LANES))
            p_ref[u] = jnp.exp2(s - jnp.concatenate([m] * (span // LANES), axis=1)).astype(BF16)

    def values(i, p_ref):
        for u, b, r, rs in group_rows(i):
            vw = v_ref[b, pl.ds(pl.multiple_of(rs * GRID_W, GRID_W), span), :]
            o = jnp.dot(p_ref[u], jnp.concatenate([vw, ones], axis=1), preferred_element_type=F32)
            o = o[:, :LANES] / o[:, LANES:]
            out = jnp.where(lo_f, o[:GRID_W], o[GRID_W:])
            o_ref[b, pl.ds(pl.multiple_of(r * GRID_W, GRID_W), GRID_W), :] = out.astype(BF16)

    def step(i, carry):
        values(2 * i - 1, p1_ref)
        scores(2 * i + 1, s1_ref)
        softmax(2 * i, s0_ref, p0_ref)
        values(2 * i, p0_ref)
        scores(2 * i + 2, s0_ref)
        softmax(2 * i + 1, s1_ref, p1_ref)
        return carry

    p1_ref[...] = jnp.ones_like(p1_ref)
    scores(0, s0_ref)
    lax.fori_loop(0, groups // 2, step, 0)
    values(groups - 1, p1_ref)


def _na_attention(qkv, rpb_rows):
    pairs = NA_HEADS // 2
    return pl.pallas_call(
        _na_kernel,
        grid=(pairs, BATCH // ATTN_SEQS),
        in_specs=[
            pl.BlockSpec((2, NA_BIAS_ROWS, LANES), lambda p, b: (p, 0, 0)),
            pl.BlockSpec((ATTN_SEQS, SEQ, LANES), lambda p, b: (b, 0, p)),
            pl.BlockSpec((ATTN_SEQS, SEQ, LANES), lambda p, b: (b, 0, pairs + p)),
            pl.BlockSpec((ATTN_SEQS, SEQ, LANES), lambda p, b: (b, 0, 2 * pairs + p)),
        ],
        out_specs=pl.BlockSpec((ATTN_SEQS, SEQ, LANES), lambda p, b: (b, 0, p)),
        out_shape=jax.ShapeDtypeStruct((BATCH, SEQ, NA_HEADS * HEAD_DIM), BF16),
        scratch_shapes=[pltpu.VMEM((NA_WIN_H, 2 * GRID_W, NA_WIN_H * GRID_W), F32),
                        pltpu.VMEM((NA_ROW_UNROLL, 2 * GRID_W, NA_WIN_H * GRID_W), F32),
                        pltpu.VMEM((NA_ROW_UNROLL, 2 * GRID_W, NA_WIN_H * GRID_W), F32),
                        pltpu.VMEM((NA_ROW_UNROLL, 2 * GRID_W, NA_WIN_H * GRID_W), BF16),
                        pltpu.VMEM((NA_ROW_UNROLL, 2 * GRID_W, NA_WIN_H * GRID_W), BF16)],
        compiler_params=_params(("arbitrary", "arbitrary"), 32),
        name="na_attention",
    )(rpb_rows, qkv, qkv, qkv)


def _t5_buckets(rel):
    half = T5_BUCKETS // 2
    max_exact = half // 2
    n = np.abs(rel)
    large = max_exact + (np.log(np.maximum(n, 1) / max_exact)
                         / np.log(T5_MAX_DIST / max_exact) * (half - max_exact)).astype(np.int32)
    large = np.minimum(large, half - 1)
    return (rel > 0).astype(np.int32) * half + np.where(n < max_exact, n, large)


SW_REL_PERIOD = 512
SW_BLOCK_UNROLL = 2


def _sw_rel_table(t5_table):
    k = np.arange(SW_REL_PERIOD)
    rel = np.where(k < SW_REL_PERIOD // 2, k, k - SW_REL_PERIOD)
    ok = np.abs(rel) <= SW_WINDOW
    vals = jnp.where(ok[:, None], t5_table[_t5_buckets(rel)].astype(F32) * LOG2E, NEG)
    return vals.T.reshape(SW_KV_HEADS, SW_GROUP, SW_REL_PERIOD)


def _sw_build_bias(rel_ref, bias_ref):
    for g in range(SW_GROUP):
        row = jnp.broadcast_to(rel_ref[0, g:g + 1, :], (SW_BLOCK, SW_REL_PERIOD))
        t = pltpu.roll(row, 0, 1, stride=1, stride_axis=0)
        rows = slice(g * SW_BLOCK, (g + 1) * SW_BLOCK)
        bias_ref[0, rows, :] = t[:, :SW_SPAN]
        bias_ref[1, rows, :] = jnp.concatenate([t[:, SW_SPAN:], t[:, :2 * SW_BLOCK]], axis=1)
        bias_ref[2, rows, :] = jnp.concatenate([t[:, 2 * SW_BLOCK:], t[:, :SW_BLOCK]], axis=1)


def _sw_kernel(sink_ref, rel_ref, q_ref, k_ref, v_ref, o_ref, bias_ref, s0_ref, s1_ref, p0_ref, p1_ref,
               t0_ref, t1_ref):
    j = pl.program_id(0)
    ones = jnp.ones((SW_SPAN, LANES), BF16)

    @pl.when(pl.program_id(1) == 0)
    def _():
        _sw_build_bias(rel_ref, bias_ref)

    masks = _half_masks(SW_BLOCK)
    lo_f = lax.broadcasted_iota(jnp.int32, (SW_BLOCK, LANES), 1) < HEAD_DIM
    sinks = [jnp.full((SW_BLOCK, LANES), sink_ref[j * SW_GROUP + g] * LOG2E, F32) for g in range(SW_GROUP)]
    seq_groups = SW_NB // SW_BLOCK_UNROLL
    groups = ATTN_SEQS * seq_groups

    def group_blocks(i):
        g = jnp.clip(i, 0, groups - 1)
        for u in range(SW_BLOCK_UNROLL):
            n = lax.rem(g, seq_groups) * SW_BLOCK_UNROLL + u
            yield u, lax.div(g, seq_groups), n, jnp.clip(n - 1, 0, SW_NB - 3)

    def block_rows(n):
        return pl.ds(pl.multiple_of(n * SW_BLOCK, SW_BLOCK), SW_BLOCK)

    def scores(i, s_ref):
        for u, b, n, first in group_blocks(i):
            kw = k_ref[b, pl.ds(pl.multiple_of(first * SW_BLOCK, SW_BLOCK), SW_SPAN), :]
            q = jnp.concatenate([q_ref[b, block_rows(n), (g // 2) * LANES:(g // 2 + 1) * LANES] * masks[g % 2]
                                 for g in range(SW_GROUP)], axis=0)
            s = lax.dot_general(q, kw, (((1,), (1,)), ((), ())), preferred_element_type=F32)
            s_ref[u * SW_GROUP:(u + 1) * SW_GROUP] = s.reshape(SW_GROUP, SW_BLOCK, SW_SPAN)

    def softmax(i, s_ref, p_ref, t_ref):
        for u, b, n, first in group_blocks(i):
            for g in range(SW_GROUP):
                c = u * SW_GROUP + g
                s = s_ref[c] + bias_ref[n - first, g * SW_BLOCK:(g + 1) * SW_BLOCK, :]
                m = jnp.maximum(jnp.broadcast_to(jnp.max(s, axis=-1, keepdims=True), (SW_BLOCK, LANES)), sinks[g])
                p_ref[c] = jnp.exp2(s - jnp.concatenate([m] * (SW_SPAN // LANES), axis=1)).astype(BF16)
                t_ref[c] = jnp.exp2(sinks[g] - m)

    def values(i, p_ref, t_ref):
        for u, b, n, first in group_blocks(i):
            vw = v_ref[b, pl.ds(pl.multiple_of(first * SW_BLOCK, SW_BLOCK), SW_SPAN), :]
            v_ones = jnp.concatenate([vw, ones], axis=1)
            chains = slice(u * SW_GROUP, (u + 1) * SW_GROUP)
            o = jnp.dot(p_ref[chains].reshape(SW_GROUP * SW_BLOCK, SW_SPAN), v_ones, preferred_element_type=F32)
            o = o[:, :LANES] / (o[:, LANES:] + t_ref[chains].reshape(SW_GROUP * SW_BLOCK, LANES))
            heads = [o[g * SW_BLOCK:(g + 1) * SW_BLOCK] for g in range(SW_GROUP)]
            out = jnp.concatenate([jnp.where(lo_f, heads[0], heads[1]), jnp.where(lo_f, heads[2], heads[3])],
                                  axis=1)
            o_ref[b, block_rows(n), :] = out.astype(BF16)

    def step(i, carry):
        values(2 * i - 1, p1_ref, t1_ref)
        scores(2 * i + 1, s1_ref)
        softmax(2 * i, s0_ref, p0_ref, t0_ref)
        values(2 * i, p0_ref, t0_ref)
        scores(2 * i + 2, s0_ref)
        softmax(2 * i + 1, s1_ref, p1_ref, t1_ref)
        return carry

    p1_ref[...] = jnp.ones_like(p1_ref)
    t1_ref[...] = jnp.ones_like(t1_ref)
    scores(0, s0_ref)
    lax.fori_loop(0, groups // 2, step, 0)
    values(groups - 1, p1_ref, t1_ref)


def _sw_attention(qkv, rel_table, sinks):
    qw = SW_GROUP * HEAD_DIM
    k_off = SW_Q_HEADS * HEAD_DIM // LANES
    v_off = k_off + SW_KV_HEADS
    chains = SW_BLOCK_UNROLL * SW_GROUP
    return pl.pallas_call(
        _sw_kernel,
        grid=(SW_KV_HEADS, BATCH // ATTN_SEQS),
        in_specs=[
            pl.BlockSpec(memory_space=pltpu.SMEM),
            pl.BlockSpec((1, SW_GROUP, SW_REL_PERIOD), lambda j, b: (j, 0, 0)),
            pl.BlockSpec((ATTN_SEQS, SEQ, qw), lambda j, b: (b, 0, j)),
            pl.BlockSpec((ATTN_SEQS, SEQ, LANES), lambda j, b: (b, 0, k_off + j)),
            pl.BlockSpec((ATTN_SEQS, SEQ, LANES), lambda j, b: (b, 0, v_off + j)),
        ],
        out_specs=pl.BlockSpec((ATTN_SEQS, SEQ, qw), lambda j, b: (b, 0, j)),
        out_shape=jax.ShapeDtypeStruct((BATCH, SEQ, SW_Q_HEADS * HEAD_DIM), BF16),
        scratch_shapes=[pltpu.VMEM((3, SW_GROUP * SW_BLOCK, SW_SPAN), F32)]
        + [pltpu.VMEM((chains, SW_BLOCK, SW_SPAN), F32)] * 2
        + [pltpu.VMEM((chains, SW_BLOCK, SW_SPAN), BF16)] * 2
        + [pltpu.VMEM((chains, SW_BLOCK, LANES), F32)] * 2,
        compiler_params=_params(("arbitrary", "arbitrary"), 48),
        name="sw_attention",
    )(sinks, rel_table, qkv, qkv, qkv)


def _post_attn_kernel(o_ref, wo_ref, x_ref, gate_ref, g_ref, sc_ref, sh_ref, wr_ref,
                      xo_ref, h_ref, aff_ref):
    y = jnp.dot(o_ref[...], wo_ref[...], preferred_element_type=F32)
    xn = x_ref[...] + gate_ref[0] * y
    xo_ref[...] = xn
    h = _norm_mod(xn, g_ref[0], sc_ref[0], sh_ref[0])
    for j in range(TOKEN_TILE_ROWS):
        h_ref[pl.ds(j, POST_TILE, stride=TOKEN_TILE_ROWS), :] = h[:, j * LANES:(j + 1) * LANES]
    hb = h.astype(BF16)
    logits = lax.dot_general(wr_ref[...], hb, (((1,), (1,)), ((), ())), preferred_element_type=F32)
    aff_ref[...] = _softmax0(logits)


def _post_attn(o, w_o, x, norm_g3, mod, w_router_t, layer):
    rows = BATCH * SEQ
    return pl.pallas_call(
        _post_attn_kernel,
        grid=(rows // POST_TILE,),
        in_specs=[
            pl.BlockSpec((POST_TILE, D_MODEL), lambda i: (i, 0)),
            pl.BlockSpec((D_MODEL, D_MODEL), lambda i: (0, 0)),
            pl.BlockSpec((POST_TILE, D_MODEL), lambda i: (i, 0)),
            _mod_spec(layer, 2, POST_TILE),
            pl.BlockSpec((1, 1, D_MODEL), lambda i: (layer * 2 + 1, 0, 0)),
            _mod_spec(layer, 4, POST_TILE),
            _mod_spec(layer, 3, POST_TILE),
            pl.BlockSpec((N_EXPERTS, D_MODEL), lambda i: (0, 0)),
        ],
        out_specs=[
            pl.BlockSpec((POST_TILE, D_MODEL), lambda i: (i, 0)),
            pl.BlockSpec((POST_TILE * TOKEN_TILE_ROWS, LANES), lambda i: (i, 0)),
            pl.BlockSpec((N_EXPERTS, POST_TILE), lambda i: (0, i)),
        ],
        out_shape=[
            jax.ShapeDtypeStruct((rows, D_MODEL), F32),
            jax.ShapeDtypeStruct((rows * TOKEN_TILE_ROWS, LANES), F32),
            jax.ShapeDtypeStruct((N_EXPERTS, rows), F32),
        ],
        compiler_params=_params(("arbitrary",), 56),
        name="post_attn",
    )(o, w_o, x, mod, norm_g3, mod, mod, w_router_t)


ROUTE_SEQS = BATCH
ROUTE_ROWS = ROUTE_SEQS * N_EXPERTS
ROUTE_BITS = SEQ.bit_length() - 1
ROUTE_VALID = 1 << (2 * ROUTE_BITS)
F32_INF_BITS = 0x7F800000


def _prefix_count(x):
    rows = x.shape[0]
    nchunk = SEQ // PREFIX_CHUNK
    r = lax.broadcasted_iota(jnp.int32, (PREFIX_CHUNK, PREFIX_CHUNK), 0)
    c = lax.broadcasted_iota(jnp.int32, (PREFIX_CHUNK, PREFIX_CHUNK), 1)
    upper = jnp.where(r < c, 1.0, 0.0).astype(BF16)
    chunks = [x[:, k * PREFIX_CHUNK:(k + 1) * PREFIX_CHUNK] for k in range(nchunk)]
    local = jnp.dot(jnp.concatenate(chunks, axis=0).astype(BF16), upper, preferred_element_type=F32)
    out = []
    offset = jnp.zeros((rows, 1), F32)
    for k in range(nchunk):
        out.append(local[k * rows:(k + 1) * rows] + offset)
        offset = offset + jnp.sum(chunks[k], axis=1, keepdims=True)
    return jnp.concatenate(out, axis=1)


def _route_kernel(aff_ref, slot_ref, tok_ref, gate_ref, first_ref):
    aff = jnp.concatenate([aff_ref[:, b * SEQ:(b + 1) * SEQ] for b in range(ROUTE_SEQS)], axis=0)
    bits = pltpu.bitcast(aff, jnp.int32)

    def count(mask):
        return jnp.sum(jnp.where(mask, 1.0, 0.0), axis=1, keepdims=True)

    def search(_, bounds):
        lo, hi = bounds
        mid = lo + ((hi - lo) >> 1)
        ge = count(bits >= mid) >= CAP
        return jnp.where(ge, mid, lo), jnp.where(ge, hi, mid)

    lo0 = jnp.zeros((ROUTE_ROWS, 1), jnp.int32)
    hi0 = jnp.full((ROUTE_ROWS, 1), F32_INF_BITS, jnp.int32)
    tau, _ = lax.fori_loop(0, F32_INF_BITS.bit_length(), search, (lo0, hi0))
    gt = bits > tau
    eq = jnp.where(bits == tau, 1.0, 0.0)
    need = CAP - count(gt)
    sel = jnp.where(gt, 1.0, jnp.where(_prefix_count(eq) < need, eq, 0.0))
    pos = _prefix_count(sel).astype(jnp.int32)
    chosen = sel > 0.5
    slot_ref[...] = jnp.where(chosen, pos, -1)
    tile_lane = lax.broadcasted_iota(jnp.int32, (ROUTE_ROWS, LANES), 1)
    first = jnp.zeros((ROUTE_ROWS, LANES), jnp.int32)
    for k in range(SEQ // COMBINE_ROWS):
        first = jnp.where(tile_lane == k, pos[:, k * COMBINE_ROWS:k * COMBINE_ROWS + 1], first)
    first_ref[...] = first
    lane = lax.broadcasted_iota(jnp.int32, (ROUTE_ROWS, SEQ), 1)
    packed = jnp.where(chosen, ROUTE_VALID | (lane << ROUTE_BITS) | (lane - pos), 0)
    gate = aff
    for k in range(ROUTE_BITS):
        step = 1 << k
        from_right = pltpu.roll(packed, SEQ - step, 1)
        gate_right = pltpu.roll(gate, SEQ - step, 1)
        arrives = (from_right & step) != 0
        stays = (packed & step) == 0
        packed = jnp.where(arrives, from_right, jnp.where(stays, packed, 0))
        gate = jnp.where(arrives, gate_right, gate)
    tok_ref[...] = (packed[:, :CAP] >> ROUTE_BITS) & (SEQ - 1)
    gate_ref[...] = gate[:, :CAP]


def _route(aff_t):
    return pl.pallas_call(
        _route_kernel,
        grid=(BATCH // ROUTE_SEQS,),
        in_specs=[pl.BlockSpec((N_EXPERTS, ROUTE_SEQS * SEQ), lambda i: (0, i))],
        out_specs=[pl.BlockSpec((ROUTE_ROWS, SEQ), lambda i: (i, 0)),
                   pl.BlockSpec((ROUTE_ROWS, CAP), lambda i: (i, 0)),
                   pl.BlockSpec((ROUTE_ROWS, CAP), lambda i: (i, 0)),
                   pl.BlockSpec((ROUTE_ROWS, LANES), lambda i: (i, 0))],
        out_shape=[jax.ShapeDtypeStruct((BATCH * N_EXPERTS, SEQ), jnp.int32),
                   jax.ShapeDtypeStruct((BATCH * N_EXPERTS, CAP), jnp.int32),
                   jax.ShapeDtypeStruct((BATCH * N_EXPERTS, CAP), F32),
                   jax.ShapeDtypeStruct((BATCH * N_EXPERTS, LANES), jnp.int32)],
        compiler_params=_params(("arbitrary",), 48),
        name="route",
    )(aff_t)


GATHER_ROWS = BATCH * CAP
FF_STEPS = EXPERT_FF // FF_TILE
CHUNK_SEQS = 4
ROWS_PER_CHUNK = GATHER_ROWS * CHUNK_SEQS // (FF_STEPS * BATCH)
PROLOGUE_UNROLL = 8


def _expert_kernel(tok_ref, h_ref, gates_ref, wg_ref, wu_ref, wd_ref, y_ref, x_ref, sem, xb_ref, acc_ref):
    e = pl.program_id(0)
    f = pl.program_id(1)
    slot = lax.rem(e, 2)
    next_slot = 1 - slot
    next_e = jnp.minimum(e + 1, N_EXPERTS - 1)

    tile = TOKEN_TILE_ROWS

    def row_copy(expert, row, dst_slot):
        src = pl.multiple_of(tok_ref[expert * GATHER_ROWS + row] * tile, tile)
        dst = pl.multiple_of(row * tile, tile)
        return pltpu.make_async_copy(h_ref.at[pl.ds(src, tile), :], x_ref.at[dst_slot, pl.ds(dst, tile), :],
                                     sem.at[dst_slot])

    def wait_rows(dst_slot):
        pltpu.make_async_copy(h_ref.at[pl.ds(0, GATHER_ROWS * tile), :], x_ref.at[dst_slot],
                              sem.at[dst_slot]).wait()

    @pl.when(jnp.logical_and(e == 0, f == 0))
    def _():
        def issue(i, carry):
            for u in range(PROLOGUE_UNROLL):
                row_copy(0, i * PROLOGUE_UNROLL + u, 0).start()
            return carry
        lax.fori_loop(0, GATHER_ROWS // PROLOGUE_UNROLL, issue, 0)
        acc_ref[...] = jnp.zeros_like(acc_ref)

    @pl.when(f == 0)
    def _():
        wait_rows(slot)
        for b in range(BATCH):
            cols = [x_ref[slot, pl.ds(b * CAP * tile + j, CAP, stride=tile), :] for j in range(tile)]
            xb_ref[b] = jnp.concatenate(cols, axis=1).astype(BF16)

    wg = wg_ref[0, 0].astype(BF16)
    wu = wu_ref[0, 0].astype(BF16)
    wd = wd_ref[0, 0].astype(BF16)
    for c in range(BATCH // CHUNK_SEQS):
        first = (f * (BATCH // CHUNK_SEQS) + c) * ROWS_PER_CHUNK
        for j in range(ROWS_PER_CHUNK):
            row_copy(next_e, first + j, next_slot).start()
        seqs = slice(c * CHUNK_SEQS, (c + 1) * CHUNK_SEQS)
        x = xb_ref[seqs].reshape(CHUNK_SEQS * CAP, D_MODEL)
        a = jnp.dot(x, wg, preferred_element_type=F32)
        u = jnp.dot(x, wu, preferred_element_type=F32)
        act = ((a * jax.nn.sigmoid(a)) * u).astype(BF16)
        carried = jnp.where(f == 0, 0.0, acc_ref[seqs])
        part = jnp.dot(act, wd, preferred_element_type=F32).reshape(CHUNK_SEQS, CAP, D_MODEL)
        acc_ref[seqs] = carried + part

    @pl.when(f == FF_STEPS - 1)
    def _():
        gates = gates_ref[0]
        for b in range(BATCH):
            y_ref[0, b] = (acc_ref[b] * gates[:, b:b + 1]).astype(BF16)

    @pl.when(jnp.logical_and(e == N_EXPERTS - 1, f == FF_STEPS - 1))
    def _():
        wait_rows(next_slot)


def _experts(tok, h, gates, w_gate, w_up, w_down, layer):
    grid_spec = pltpu.PrefetchScalarGridSpec(
        num_scalar_prefetch=1,
        grid=(N_EXPERTS, FF_STEPS),
        in_specs=[
            pl.BlockSpec(memory_space=pl.ANY),
            pl.BlockSpec((1, CAP, BATCH), lambda e, f, tok: (e, 0, 0)),
            pl.BlockSpec((1, 1, D_MODEL, FF_TILE), lambda e, f, tok: (layer, e, 0, f)),
            pl.BlockSpec((1, 1, D_MODEL, FF_TILE), lambda e, f, tok: (layer, e, 0, f)),
            pl.BlockSpec((1, 1, FF_TILE, D_MODEL), lambda e, f, tok: (layer, e, f, 0)),
        ],
        out_specs=pl.BlockSpec((1, BATCH, CAP, D_MODEL), lambda e, f, tok: (e, 0, 0, 0)),
        scratch_shapes=[pltpu.VMEM((2, GATHER_ROWS * TOKEN_TILE_ROWS, LANES), F32), pltpu.SemaphoreType.DMA((2,)),
                        pltpu.VMEM((BATCH, CAP, D_MODEL), BF16), pltpu.VMEM((BATCH, CAP, D_MODEL), F32)],
    )
    return pl.pallas_call(
        _expert_kernel,
        grid_spec=grid_spec,
        out_shape=jax.ShapeDtypeStruct((N_EXPERTS, BATCH, CAP, D_MODEL), BF16),
        compiler_params=_params(("arbitrary", "arbitrary"), 60),
        name="moe_experts",
    )(tok, h, gates, w_gate, w_up, w_down)


COMBINE_WINDOW = 128
SLOT_ALIGN = 16


def _combine_kernel(start_ref, short_ref, slot_ref, y_ref, x_ref, gate_ref, g_ref, o_ref, *, final):
    tile = pl.program_id(0) * (SEQ // COMBINE_ROWS) + pl.program_id(1)
    slots = slot_ref[0]

    def finish(moe):
        x = x_ref[0] + gate_ref[0] * moe
        if final:
            x = (x * lax.rsqrt(jnp.mean(x * x, axis=-1, keepdims=True) + RMS_EPS)) * g_ref[...]
        o_ref[0] = x

    @pl.when(short_ref[tile] == 1)
    def _():
        lane = lax.broadcasted_iota(jnp.int32, (COMBINE_ROWS, COMBINE_WINDOW), 1)
        hits, rows = [], []
        for e in range(N_EXPERTS):
            start = pl.multiple_of(start_ref[tile * N_EXPERTS + e], SLOT_ALIGN)
            hits.append(jnp.where(slots[:, e:e + 1] - start == lane, 1.0, 0.0).astype(BF16))
            rows.append(y_ref[e, 0, pl.ds(start, COMBINE_WINDOW), :])
        finish(jnp.dot(jnp.concatenate(hits, axis=1), jnp.concatenate(rows, axis=0), preferred_element_type=F32))

    @pl.when(short_ref[tile] != 1)
    def _():
        lane = lax.broadcasted_iota(jnp.int32, (COMBINE_ROWS, CAP), 1)
        onehot = jnp.concatenate(
            [jnp.where(slots[:, e:e + 1] == lane, 1.0, 0.0).astype(BF16) for e in range(N_EXPERTS)], axis=1)
        finish(jnp.dot(onehot, y_ref[...].reshape(N_EXPERTS * CAP, D_MODEL), preferred_element_type=F32))


def _combine_windows(first):
    last = jnp.concatenate([first[..., 1:], jnp.full(first.shape[:2] + (1,), CAP, jnp.int32)], axis=-1)
    start = jnp.minimum(first // SLOT_ALIGN * SLOT_ALIGN, CAP - COMBINE_WINDOW)
    short = jnp.all(last <= start + COMBINE_WINDOW, axis=1)
    return start.transpose(0, 2, 1).reshape(-1), short.astype(jnp.int32).reshape(-1)


def _combine(starts, short, slot_t, y, x, mod, final_g, layer, final):
    grid_spec = pltpu.PrefetchScalarGridSpec(
        num_scalar_prefetch=2,
        grid=(BATCH, SEQ // COMBINE_ROWS),
        in_specs=[
            pl.BlockSpec((1, COMBINE_ROWS, N_EXPERTS), lambda b, t, s, f: (b, t, 0)),
            pl.BlockSpec((N_EXPERTS, 1, CAP, D_MODEL), lambda b, t, s, f: (0, b, 0, 0)),
            pl.BlockSpec((1, COMBINE_ROWS, D_MODEL), lambda b, t, s, f: (b, t, 0)),
            pl.BlockSpec((1, 1, D_MODEL), lambda b, t, s, f: ((layer * 6 + 5) * BATCH + b, 0, 0)),
            pl.BlockSpec((1, D_MODEL), lambda b, t, s, f: (0, 0)),
        ],
        out_specs=pl.BlockSpec((1, COMBINE_ROWS, D_MODEL), lambda b, t, s, f: (b, t, 0)),
    )
    return pl.pallas_call(
        functools.partial(_combine_kernel, final=final),
        grid_spec=grid_spec,
        out_shape=jax.ShapeDtypeStruct((BATCH, SEQ, D_MODEL), F32),
        compiler_params=_params(("arbitrary", "arbitrary"), 48),
        name="moe_combine",
    )(starts, short, slot_t, y, x, mod, final_g.reshape(1, D_MODEL))


def _scaled_q_weight(w, q_columns):
    scale = jnp.concatenate([jnp.full((q_columns,), Q_SCALE, F32), jnp.ones((w.shape[1] - q_columns,), F32)])
    return (w * scale).astype(BF16)


def kernel(x, c, ada_w, ada_b, norm_g, na_w_qkv, na_w_o, na_rpb, sw_w_qkv, sw_w_o, sw_sinks, t5_bias,
           moe_w_router, moe_w_gate, moe_w_up, moe_w_down, final_g):
    mod = _ada(c, ada_w, ada_b)
    norm_g3 = norm_g.reshape(DEPTH * 2, 1, D_MODEL)
    x = x.reshape(BATCH * SEQ, D_MODEL)
    for layer in range(DEPTH):
        j = layer // N_MIXERS
        if layer % N_MIXERS == 0:
            qkv = _qkv(x, norm_g3, mod, _scaled_q_weight(na_w_qkv[j], NA_HEADS * HEAD_DIM), layer)
            o = _na_attention(qkv.reshape(BATCH, SEQ, -1), _na_rpb_rows(na_rpb[j]))
            w_o = na_w_o[j]
        else:
            qkv = _qkv(x, norm_g3, mod, _scaled_q_weight(sw_w_qkv[j], SW_Q_HEADS * HEAD_DIM), layer,
                       head_copies=2 * SW_KV_HEADS * HEAD_DIM // LANES)
            o = _sw_attention(qkv.reshape(BATCH, SEQ, -1), _sw_rel_table(t5_bias), sw_sinks[j])
            w_o = sw_w_o[j]
        w_router = moe_w_router[layer].astype(BF16)
        x, h, aff_t = _post_attn(o.reshape(BATCH * SEQ, -1), w_o.astype(BF16), x, norm_g3, mod,
                                 w_router.T, layer)
        slot, tok, gate, first = _route(aff_t)
        tok = tok.reshape(BATCH, N_EXPERTS, CAP)
        rows = tok.transpose(1, 0, 2) + (jnp.arange(BATCH, dtype=jnp.int32) * SEQ)[None, :, None]
        gates = gate.reshape(BATCH, N_EXPERTS, CAP).transpose(1, 2, 0)
        y = _experts(rows.reshape(-1), h, gates, moe_w_gate, moe_w_up, moe_w_down, layer)
        slot_t = slot.reshape(BATCH, N_EXPERTS, SEQ).transpose(0, 2, 1)
        starts, short = _combine_windows(first.reshape(BATCH, N_EXPERTS, LANES)[..., :SEQ // COMBINE_ROWS])
        x = _combine(starts, short, slot_t, y, x.reshape(BATCH, SEQ, D_MODEL), mod, final_g, layer,
                     final=layer == DEPTH - 1).reshape(BATCH * SEQ, D_MODEL)
    return x.reshape(BATCH, SEQ, D_MODEL)
```

```python
import functools

import numpy as np
import jax
import jax.numpy as jnp
from jax import lax
from jax.experimental import pallas as pl
from jax.experimental.pallas import tpu as pltpu

D_MODEL = 1024
BATCH = 8
SEQ = 2048
DEPTH = 2
GRID_W = 64
ROWS = SEQ // GRID_W
N_MIXERS = 2
HEAD_DIM = 64
NA_HEADS = 16
NA_WIN_H = 8
NA_WIN_W = 16
SW_Q_HEADS = 16
SW_KV_HEADS = 4
SW_GROUP = SW_Q_HEADS // SW_KV_HEADS
SW_WINDOW = 128
SW_BLOCK = 128
SW_NB = SEQ // SW_BLOCK
SW_SPAN = 3 * SW_BLOCK
T5_BUCKETS = 32
T5_MAX_DIST = 128
N_EXPERTS = 16
EXPERT_FF = 2048
EC_CAPACITY = 2
CAP = EC_CAPACITY * SEQ // N_EXPERTS
RMS_EPS = 1e-6
NEG = -1e30
LOG2E = 1.4426950408889634
Q_SCALE = HEAD_DIM ** -0.5 * LOG2E

LANES = 128
TOKEN_TILE_ROWS = D_MODEL // LANES
MIB = 1024 * 1024
F32 = jnp.float32
BF16 = jnp.bfloat16

ROW_TILE = 1024
POST_TILE = 1024
ATTN_SEQS = 4
FF_TILE = 512
COMBINE_ROWS = 512
PREFIX_CHUNK = 256


def _params(semantics, vmem_mib):
    return pltpu.CompilerParams(dimension_semantics=semantics, vmem_limit_bytes=vmem_mib * MIB)


def _norm_mod(x, g, sc, sh):
    y = x * lax.rsqrt(jnp.mean(x * x, axis=-1, keepdims=True) + RMS_EPS)
    return (y * g) * (1.0 + sc) + sh


def _softmax0(z):
    z = z - jnp.max(z, axis=0, keepdims=True)
    p = jnp.exp(z)
    return p / jnp.sum(p, axis=0, keepdims=True)


def _half_masks(rows):
    lane = lax.broadcasted_iota(jnp.int32, (rows, LANES), 1)
    lo = jnp.where(lane < HEAD_DIM, 1.0, 0.0).astype(BF16)
    hi = jnp.where(lane < HEAD_DIM, 0.0, 1.0).astype(BF16)
    return lo, hi


def _ada_kernel(c_ref, w_ref, b_ref, o_ref):
    c = c_ref[...]
    act = (c * jax.nn.sigmoid(c)).astype(BF16)
    o_ref[0] = jnp.dot(act, w_ref[0].astype(BF16), preferred_element_type=F32) + b_ref[0]


def _ada(c, ada_w, ada_b):
    out = pl.pallas_call(
        _ada_kernel,
        grid=(DEPTH, 6),
        in_specs=[
            pl.BlockSpec((BATCH, D_MODEL), lambda l, k: (0, 0)),
            pl.BlockSpec((1, D_MODEL, D_MODEL), lambda l, k: (l, 0, k)),
            pl.BlockSpec((1, 1, D_MODEL), lambda l, k: (l * 6 + k, 0, 0)),
        ],
        out_specs=pl.BlockSpec((1, BATCH, D_MODEL), lambda l, k: (l * 6 + k, 0, 0)),
        out_shape=jax.ShapeDtypeStruct((DEPTH * 6, BATCH, D_MODEL), F32),
        compiler_params=_params(("arbitrary", "arbitrary"), 32),
        name="ada_mod",
    )(c, ada_w, ada_b.reshape(DEPTH * 6, 1, D_MODEL))
    return out.reshape(DEPTH * 6 * BATCH, 1, D_MODEL)


def _mod_spec(layer, chunk, row_tile=ROW_TILE):
    tiles_per_seq = SEQ // row_tile
    return pl.BlockSpec((1, 1, D_MODEL),
                        lambda i: ((layer * 6 + chunk) * BATCH + i // tiles_per_seq, 0, 0))


def _qkv_kernel(x_ref, g_ref, sc_ref, sh_ref, w_ref, o_ref, *, head_copies):
    h = _norm_mod(x_ref[...], g_ref[0], sc_ref[0], sh_ref[0])
    y = jnp.dot(h.astype(BF16), w_ref[...], preferred_element_type=F32)
    if head_copies:
        low_half = lax.broadcasted_iota(jnp.int32, (ROW_TILE, LANES), 1) < HEAD_DIM
        keep = y.shape[1] - head_copies * LANES
        tiles = [y[:, :keep]]
        for t in range(head_copies):
            pair = y[:, keep + t * LANES:keep + (t + 1) * LANES]
            swapped = pltpu.roll(pair, HEAD_DIM, 1)
            tiles += [jnp.where(low_half, pair, swapped), jnp.where(low_half, swapped, pair)]
        y = jnp.concatenate(tiles, axis=1)
    o_ref[...] = y.astype(BF16)


def _qkv(x, norm_g3, mod, w, layer, head_copies=0):
    n = w.shape[1] + head_copies * LANES
    return pl.pallas_call(
        functools.partial(_qkv_kernel, head_copies=head_copies),
        grid=(BATCH * SEQ // ROW_TILE,),
        in_specs=[
            pl.BlockSpec((ROW_TILE, D_MODEL), lambda i: (i, 0)),
            pl.BlockSpec((1, 1, D_MODEL), lambda i: (layer * 2, 0, 0)),
            _mod_spec(layer, 1),
            _mod_spec(layer, 0),
            pl.BlockSpec(w.shape, lambda i: (0, 0)),
        ],
        out_specs=pl.BlockSpec((ROW_TILE, n), lambda i: (i, 0)),
        out_shape=jax.ShapeDtypeStruct((BATCH * SEQ, n), BF16),
        compiler_params=_params(("arbitrary",), 60),
        name="norm_qkv",
    )(x, norm_g3, mod, mod, w)


NA_BIAS_ROWS = 2 * NA_WIN_H - 1
NA_ROW_UNROLL = 4


def _na_rpb_rows(rpb):
    w = NA_WIN_W - 1
    rpb = rpb.astype(F32) * LOG2E
    pad = jnp.zeros(rpb.shape[:2] + (LANES - 2 * w - 1,), F32)
    return jnp.concatenate([rpb[..., w:], pad, rpb[..., :w]], axis=-1)


def _na_build_bias(w_ref, bias_ref):
    lane = lax.broadcasted_iota(jnp.int32, (GRID_W, LANES), 1)
    col = lax.broadcasted_iota(jnp.int32, (GRID_W, LANES), 0)
    kc = jnp.bitwise_and(lane, GRID_W - 1)
    cstart = jnp.clip(col - NA_WIN_W // 2, 0, GRID_W - NA_WIN_W)
    in_window = jnp.where(kc >= cstart, jnp.where(kc < cstart + NA_WIN_W, 1.0, 0.0), 0.0) > 0.5
    low_half = lane < GRID_W

    def toeplitz(head, a, shift):
        row = jnp.broadcast_to(w_ref[head, a:a + 1, :], (GRID_W, LANES))
        return pltpu.roll(row, shift, 1, stride=1, stride_axis=0)

    for head in range(2):
        tiles = [jnp.where(in_window, jnp.where(low_half, toeplitz(head, a, 0), toeplitz(head, a + 1, GRID_W)), NEG)
                 for a in range(NA_BIAS_ROWS - 1)]
        for d in range(NA_WIN_H):
            for m in range(NA_WIN_H // 2):
                bias_ref[d, head * GRID_W:(head + 1) * GRID_W, m * LANES:(m + 1) * LANES] = (
                    tiles[NA_WIN_H - 1 - d + 2 * m])


def _na_kernel(w_ref, q_ref, k_ref, v_ref, o_ref, bias_ref, s0_ref, s1_ref, p0_ref, p1_ref):
    @pl.when(pl.program_id(1) == 0)
    def _():
        _na_build_bias(w_ref, bias_ref)

    lo, hi = _half_masks(GRID_W)
    lo_f = lax.broadcasted_iota(jnp.int32, (GRID_W, LANES), 1) < HEAD_DIM
    span = NA_WIN_H * GRID_W
    ones = jnp.ones((span, LANES), BF16)
    seq_groups = ROWS // NA_ROW_UNROLL
    groups = ATTN_SEQS * seq_groups

    def group_rows(i):
        g = jnp.clip(i, 0, groups - 1)
        for u in range(NA_ROW_UNROLL):
            r = lax.rem(g, seq_groups) * NA_ROW_UNROLL + u
            yield u, lax.div(g, seq_groups), r, jnp.clip(r - NA_WIN_H // 2, 0, ROWS - NA_WIN_H)

    def scores(i, s_ref):
        for u, b, r, rs in group_rows(i):
            q = q_ref[b, pl.ds(pl.multiple_of(r * GRID_W, GRID_W), GRID_W), :]
            lhs = jnp.concatenate([q * lo, q * hi], axis=0)
            kw = k_ref[b, pl.ds(pl.multiple_of(rs * GRID_W, GRID_W), span), :]
            s_ref[u] = lax.dot_general(lhs, kw, (((1,), (1,)), ((), ())), preferred_element_type=F32)

    def softmax(i, s_ref, p_ref):
        for u, b, r, rs in group_rows(i):
            s = s_ref[u] + bias_ref[r - rs]
            m = jnp.broadcast_to(jnp.max(s, axis=-1, keepdims=True), (2 * GRID_W, LANES))
            p_ref[u] = jnp.exp2(s - jnp.concatenate([m] * (span // LANES), axis=1)).astype(BF16)

    def values(i, p_ref):
        for u, b, r, rs in group_rows(i):
            vw = v_ref[b, pl.ds(pl.multiple_of(rs * GRID_W, GRID_W), span), :]
            o = jnp.dot(p_ref[u], jnp.concatenate([vw, ones], axis=1), preferred_element_type=F32)
            o = o[:, :LANES] / o[:, LANES:]
            out = jnp.where(lo_f, o[:GRID_W], o[GRID_W:])
            o_ref[b, pl.ds(pl.multiple_of(r * GRID_W, GRID_W), GRID_W), :] = out.astype(BF16)

    def step(i, carry):
        values(2 * i - 1, p1_ref)
        scores(2 * i + 1, s1_ref)
        softmax(2 * i, s0_ref, p0_ref)
        values(2 * i, p0_ref)
        scores(2 * i + 2, s0_ref)
        softmax(2 * i + 1, s1_ref, p1_ref)
        return carry

    p1_ref[...] = jnp.ones_like(p1_ref)
    scores(0, s0_ref)
    lax.fori_loop(0, groups // 2, step, 0)
    values(groups - 1, p1_ref)


def _na_attention(qkv, rpb_rows):
    pairs = NA_HEADS // 2
    return pl.pallas_call(
        _na_kernel,
        grid=(pairs, BATCH // ATTN_SEQS),
        in_specs=[
            pl.BlockSpec((2, NA_BIAS_ROWS, LANES), lambda p, b: (p, 0, 0)),
            pl.BlockSpec((ATTN_SEQS, SEQ, LANES), lambda p, b: (b, 0, p)),
            pl.BlockSpec((ATTN_SEQS, SEQ, LANES), lambda p, b: (b, 0, pairs + p)),
            pl.BlockSpec((ATTN_SEQS, SEQ, LANES), lambda p, b: (b, 0, 2 * pairs + p)),
        ],
        out_specs=pl.BlockSpec((ATTN_SEQS, SEQ, LANES), lambda p, b: (b, 0, p)),
        out_shape=jax.ShapeDtypeStruct((BATCH, SEQ, NA_HEADS * HEAD_DIM), BF16),
        scratch_shapes=[pltpu.VMEM((NA_WIN_H, 2 * GRID_W, NA_WIN_H * GRID_W), F32),
                        pltpu.VMEM((NA_ROW_UNROLL, 2 * GRID_W, NA_WIN_H * GRID_W), F32),
                        pltpu.VMEM((NA_ROW_UNROLL, 2 * GRID_W, NA_WIN_H * GRID_W), F32),
                        pltpu.VMEM((NA_ROW_UNROLL, 2 * GRID_W, NA_WIN_H * GRID_W), BF16),
                        pltpu.VMEM((NA_ROW_UNROLL, 2 * GRID_W, NA_WIN_H * GRID_W), BF16)],
        compiler_params=_params(("arbitrary", "arbitrary"), 32),
        name="na_attention",
    )(rpb_rows, qkv, qkv, qkv)


def _t5_buckets(rel):
    half = T5_BUCKETS // 2
    max_exact = half // 2
    n = np.abs(rel)
    large = max_exact + (np.log(np.maximum(n, 1) / max_exact)
                         / np.log(T5_MAX_DIST / max_exact) * (half - max_exact)).astype(np.int32)
    large = np.minimum(large, half - 1)
    return (rel > 0).astype(np.int32) * half + np.where(n < max_exact, n, large)


SW_REL_PERIOD = 512
SW_BLOCK_UNROLL = 2


def _sw_rel_table(t5_table):
    k = np.arange(SW_REL_PERIOD)
    rel = np.where(k < SW_REL_PERIOD // 2, k, k - SW_REL_PERIOD)
    ok = np.abs(rel) <= SW_WINDOW
    vals = jnp.where(ok[:, None], t5_table[_t5_buckets(rel)].astype(F32) * LOG2E, NEG)
    return vals.T.reshape(SW_KV_HEADS, SW_GROUP, SW_REL_PERIOD)


def _sw_build_bias(rel_ref, bias_ref):
    for g in range(SW_GROUP):
        row = jnp.broadcast_to(rel_ref[0, g:g + 1, :], (SW_BLOCK, SW_REL_PERIOD))
        t = pltpu.roll(row, 0, 1, stride=1, stride_axis=0)
        rows = slice(g * SW_BLOCK, (g + 1) * SW_BLOCK)
        bias_ref[0, rows, :] = t[:, :SW_SPAN]
        bias_ref[1, rows, :] = jnp.concatenate([t[:, SW_SPAN:], t[:, :2 * SW_BLOCK]], axis=1)
        bias_ref[2, rows, :] = jnp.concatenate([t[:, 2 * SW_BLOCK:], t[:, :SW_BLOCK]], axis=1)


def _sw_kernel(sink_ref, rel_ref, q_ref, k_ref, v_ref, o_ref, bias_ref, s0_ref, s1_ref, p0_ref, p1_ref,
               t0_ref, t1_ref):
    j = pl.program_id(0)
    ones = jnp.ones((SW_SPAN, LANES), BF16)

    @pl.when(pl.program_id(1) == 0)
    def _():
        _sw_build_bias(rel_ref, bias_ref)

    masks = _half_masks(SW_BLOCK)
    lo_f = lax.broadcasted_iota(jnp.int32, (SW_BLOCK, LANES), 1) < HEAD_DIM
    sinks = [jnp.full((SW_BLOCK, LANES), sink_ref[j * SW_GROUP + g] * LOG2E, F32) for g in range(SW_GROUP)]
    seq_groups = SW_NB // SW_BLOCK_UNROLL
    groups = ATTN_SEQS * seq_groups

    def group_blocks(i):
        g = jnp.clip(i, 0, groups - 1)
        for u in range(SW_BLOCK_UNROLL):
            n = lax.rem(g, seq_groups) * SW_BLOCK_UNROLL + u
            yield u, lax.div(g, seq_groups), n, jnp.clip(n - 1, 0, SW_NB - 3)

    def block_rows(n):
        return pl.ds(pl.multiple_of(n * SW_BLOCK, SW_BLOCK), SW_BLOCK)

    def scores(i, s_ref):
        for u, b, n, first in group_blocks(i):
            kw = k_ref[b, pl.ds(pl.multiple_of(first * SW_BLOCK, SW_BLOCK), SW_SPAN), :]
            q = jnp.concatenate([q_ref[b, block_rows(n), (g // 2) * LANES:(g // 2 + 1) * LANES] * masks[g % 2]
                                 for g in range(SW_GROUP)], axis=0)
            s = lax.dot_general(q, kw, (((1,), (1,)), ((), ())), preferred_element_type=F32)
            s_ref[u * SW_GROUP:(u + 1) * SW_GROUP] = s.reshape(SW_GROUP, SW_BLOCK, SW_SPAN)

    def softmax(i, s_ref, p_ref, t_ref):
        for u, b, n, first in group_blocks(i):
            for g in range(SW_GROUP):
                c = u * SW_GROUP + g
                s = s_ref[c] + bias_ref[n - first, g * SW_BLOCK:(g + 1) * SW_BLOCK, :]
                m = jnp.maximum(jnp.broadcast_to(jnp.max(s, axis=-1, keepdims=True), (SW_BLOCK, LANES)), sinks[g])
                p_ref[c] = jnp.exp2(s - jnp.concatenate([m] * (SW_SPAN // LANES), axis=1)).astype(BF16)
                t_ref[c] = jnp.exp2(sinks[g] - m)

    def values(i, p_ref, t_ref):
        for u, b, n, first in group_blocks(i):
            vw = v_ref[b, pl.ds(pl.multiple_of(first * SW_BLOCK, SW_BLOCK), SW_SPAN), :]
            v_ones = jnp.concatenate([vw, ones], axis=1)
            chains = slice(u * SW_GROUP, (u + 1) * SW_GROUP)
            o = jnp.dot(p_ref[chains].reshape(SW_GROUP * SW_BLOCK, SW_SPAN), v_ones, preferred_element_type=F32)
            o = o[:, :LANES] / (o[:, LANES:] + t_ref[chains].reshape(SW_GROUP * SW_BLOCK, LANES))
            heads = [o[g * SW_BLOCK:(g + 1) * SW_BLOCK] for g in range(SW_GROUP)]
            out = jnp.concatenate([jnp.where(lo_f, heads[0], heads[1]), jnp.where(lo_f, heads[2], heads[3])],
                                  axis=1)
            o_ref[b, block_rows(n), :] = out.astype(BF16)

    def step(i, carry):
        values(2 * i - 1, p1_ref, t1_ref)
        scores(2 * i + 1, s1_ref)
        softmax(2 * i, s0_ref, p0_ref, t0_ref)
        values(2 * i, p0_ref, t0_ref)
        scores(2 * i + 2, s0_ref)
        softmax(2 * i + 1, s1_ref, p1_ref, t1_ref)
        return carry

    p1_ref[...] = jnp.ones_like(p1_ref)
    t1_ref[...] = jnp.ones_like(t1_ref)
    scores(0, s0_ref)
    lax.fori_loop(0, groups // 2, step, 0)
    values(groups - 1, p1_ref, t1_ref)


def _sw_attention(qkv, rel_table, sinks):
    qw = SW_GROUP * HEAD_DIM
    k_off = SW_Q_HEADS * HEAD_DIM // LANES
    v_off = k_off + SW_KV_HEADS
    chains = SW_BLOCK_UNROLL * SW_GROUP
    return pl.pallas_call(
        _sw_kernel,
        grid=(SW_KV_HEADS, BATCH // ATTN_SEQS),
        in_specs=[
            pl.BlockSpec(memory_space=pltpu.SMEM),
            pl.BlockSpec((1, SW_GROUP, SW_REL_PERIOD), lambda j, b: (j, 0, 0)),
            pl.BlockSpec((ATTN_SEQS, SEQ, qw), lambda j, b: (b, 0, j)),
            pl.BlockSpec((ATTN_SEQS, SEQ, LANES), lambda j, b: (b, 0, k_off + j)),
            pl.BlockSpec((ATTN_SEQS, SEQ, LANES), lambda j, b: (b, 0, v_off + j)),
        ],
        out_specs=pl.BlockSpec((ATTN_SEQS, SEQ, qw), lambda j, b: (b, 0, j)),
        out_shape=jax.ShapeDtypeStruct((BATCH, SEQ, SW_Q_HEADS * HEAD_DIM), BF16),
        scratch_shapes=[pltpu.VMEM((3, SW_GROUP * SW_BLOCK, SW_SPAN), F32)]
        + [pltpu.VMEM((chains, SW_BLOCK, SW_SPAN), F32)] * 2
        + [pltpu.VMEM((chains, SW_BLOCK, SW_SPAN), BF16)] * 2
        + [pltpu.VMEM((chains, SW_BLOCK, LANES), F32)] * 2,
        compiler_params=_params(("arbitrary", "arbitrary"), 48),
        name="sw_attention",
    )(sinks, rel_table, qkv, qkv, qkv)


def _post_attn_kernel(o_ref, wo_ref, x_ref, gate_ref, g_ref, sc_ref, sh_ref, wr_ref,
                      xo_ref, h_ref, aff_ref):
    y = jnp.dot(o_ref[...], wo_ref[...], preferred_element_type=F32)
    xn = x_ref[...] + gate_ref[0] * y
    xo_ref[...] = xn
    h = _norm_mod(xn, g_ref[0], sc_ref[0], sh_ref[0])
    for j in range(TOKEN_TILE_ROWS):
        h_ref[pl.ds(j, POST_TILE, stride=TOKEN_TILE_ROWS), :] = h[:, j * LANES:(j + 1) * LANES]
    hb = h.astype(BF16)
    logits = lax.dot_general(wr_ref[...], hb, (((1,), (1,)), ((), ())), preferred_element_type=F32)
    aff_ref[...] = _softmax0(logits)


def _post_attn(o, w_o, x, norm_g3, mod, w_router_t, layer):
    rows = BATCH * SEQ
    return pl.pallas_call(
        _post_attn_kernel,
        grid=(rows // POST_TILE,),
        in_specs=[
            pl.BlockSpec((POST_TILE, D_MODEL), lambda i: (i, 0)),
            pl.BlockSpec((D_MODEL, D_MODEL), lambda i: (0, 0)),
            pl.BlockSpec((POST_TILE, D_MODEL), lambda i: (i, 0)),
            _mod_spec(layer, 2, POST_TILE),
            pl.BlockSpec((1, 1, D_MODEL), lambda i: (layer * 2 + 1, 0, 0)),
            _mod_spec(layer, 4, POST_TILE),
            _mod_spec(layer, 3, POST_TILE),
            pl.BlockSpec((N_EXPERTS, D_MODEL), lambda i: (0, 0)),
        ],
        out_specs=[
            pl.BlockSpec((POST_TILE, D_MODEL), lambda i: (i, 0)),
            pl.BlockSpec((POST_TILE * TOKEN_TILE_ROWS, LANES), lambda i: (i, 0)),
            pl.BlockSpec((N_EXPERTS, POST_TILE), lambda i: (0, i)),
        ],
        out_shape=[
            jax.ShapeDtypeStruct((rows, D_MODEL), F32),
            jax.ShapeDtypeStruct((rows * TOKEN_TILE_ROWS, LANES), F32),
            jax.ShapeDtypeStruct((N_EXPERTS, rows), F32),
        ],
        compiler_params=_params(("arbitrary",), 56),
        name="post_attn",
    )(o, w_o, x, mod, norm_g3, mod, mod, w_router_t)


ROUTE_SEQS = BATCH
ROUTE_ROWS = ROUTE_SEQS * N_EXPERTS
ROUTE_BITS = SEQ.bit_length() - 1
ROUTE_VALID = 1 << (2 * ROUTE_BITS)
F32_INF_BITS = 0x7F800000


def _prefix_count(x):
    rows = x.shape[0]
    nchunk = SEQ // PREFIX_CHUNK
    r = lax.broadcasted_iota(jnp.int32, (PREFIX_CHUNK, PREFIX_CHUNK), 0)
    c = lax.broadcasted_iota(jnp.int32, (PREFIX_CHUNK, PREFIX_CHUNK), 1)
    upper = jnp.where(r < c, 1.0, 0.0).astype(BF16)
    chunks = [x[:, k * PREFIX_CHUNK:(k + 1) * PREFIX_CHUNK] for k in range(nchunk)]
    local = jnp.dot(jnp.concatenate(chunks, axis=0).astype(BF16), upper, preferred_element_type=F32)
    out = []
    offset = jnp.zeros((rows, 1), F32)
    for k in range(nchunk):
        out.append(local[k * rows:(k + 1) * rows] + offset)
        offset = offset + jnp.sum(chunks[k], axis=1, keepdims=True)
    return jnp.concatenate(out, axis=1)


def _route_kernel(aff_ref, slot_ref, tok_ref, gate_ref, first_ref):
    aff = jnp.concatenate([aff_ref[:, b * SEQ:(b + 1) * SEQ] for b in range(ROUTE_SEQS)], axis=0)
    bits = pltpu.bitcast(aff, jnp.int32)

    def count(mask):
        return jnp.sum(jnp.where(mask, 1.0, 0.0), axis=1, keepdims=True)

    def search(_, bounds):
        lo, hi = bounds
        mid = lo + ((hi - lo) >> 1)
        ge = count(bits >= mid) >= CAP
        return jnp.where(ge, mid, lo), jnp.where(ge, hi, mid)

    lo0 = jnp.zeros((ROUTE_ROWS, 1), jnp.int32)
    hi0 = jnp.full((ROUTE_ROWS, 1), F32_INF_BITS, jnp.int32)
    tau, _ = lax.fori_loop(0, F32_INF_BITS.bit_length(), search, (lo0, hi0))
    gt = bits > tau
    eq = jnp.where(bits == tau, 1.0, 0.0)
    need = CAP - count(gt)
    sel = jnp.where(gt, 1.0, jnp.where(_prefix_count(eq) < need, eq, 0.0))
    pos = _prefix_count(sel).astype(jnp.int32)
    chosen = sel > 0.5
    slot_ref[...] = jnp.where(chosen, pos, -1)
    tile_lane = lax.broadcasted_iota(jnp.int32, (ROUTE_ROWS, LANES), 1)
    first = jnp.zeros((ROUTE_ROWS, LANES), jnp.int32)
    for k in range(SEQ // COMBINE_ROWS):
        first = jnp.where(tile_lane == k, pos[:, k * COMBINE_ROWS:k * COMBINE_ROWS + 1], first)
    first_ref[...] = first
    lane = lax.broadcasted_iota(jnp.int32, (ROUTE_ROWS, SEQ), 1)
    packed = jnp.where(chosen, ROUTE_VALID | (lane << ROUTE_BITS) | (lane - pos), 0)
    gate = aff
    for k in range(ROUTE_BITS):
        step = 1 << k
        from_right = pltpu.roll(packed, SEQ - step, 1)
        gate_right = pltpu.roll(gate, SEQ - step, 1)
        arrives = (from_right & step) != 0
        stays = (packed & step) == 0
        packed = jnp.where(arrives, from_right, jnp.where(stays, packed, 0))
        gate = jnp.where(arrives, gate_right, gate)
    tok_ref[...] = (packed[:, :CAP] >> ROUTE_BITS) & (SEQ - 1)
    gate_ref[...] = gate[:, :CAP]


def _route(aff_t):
    return pl.pallas_call(
        _route_kernel,
        grid=(BATCH // ROUTE_SEQS,),
        in_specs=[pl.BlockSpec((N_EXPERTS, ROUTE_SEQS * SEQ), lambda i: (0, i))],
        out_specs=[pl.BlockSpec((ROUTE_ROWS, SEQ), lambda i: (i, 0)),
                   pl.BlockSpec((ROUTE_ROWS, CAP), lambda i: (i, 0)),
                   pl.BlockSpec((ROUTE_ROWS, CAP), lambda i: (i, 0)),
                   pl.BlockSpec((ROUTE_ROWS, LANES), lambda i: (i, 0))],
        out_shape=[jax.ShapeDtypeStruct((BATCH * N_EXPERTS, SEQ), jnp.int32),
                   jax.ShapeDtypeStruct((BATCH * N_EXPERTS, CAP), jnp.int32),
                   jax.ShapeDtypeStruct((BATCH * N_EXPERTS, CAP), F32),
                   jax.ShapeDtypeStruct((BATCH * N_EXPERTS, LANES), jnp.int32)],
        compiler_params=_params(("arbitrary",), 48),
        name="route",
    )(aff_t)


GATHER_ROWS = BATCH * CAP
FF_STEPS = EXPERT_FF // FF_TILE
CHUNK_SEQS = 4
ROWS_PER_CHUNK = GATHER_ROWS * CHUNK_SEQS // (FF_STEPS * BATCH)
PROLOGUE_UNROLL = 8


def _expert_kernel(tok_ref, h_ref, gates_ref, wg_ref, wu_ref, wd_ref, y_ref, x_ref, sem, xb_ref, acc_ref):
    e = pl.program_id(0)
    f = pl.program_id(1)
    slot = lax.rem(e, 2)
    next_slot = 1 - slot
    next_e = jnp.minimum(e + 1, N_EXPERTS - 1)

    tile = TOKEN_TILE_ROWS

    def row_copy(expert, row, dst_slot):
        src = pl.multiple_of(tok_ref[expert * GATHER_ROWS + row] * tile, tile)
        dst = pl.multiple_of(row * tile, tile)
        return pltpu.make_async_copy(h_ref.at[pl.ds(src, tile), :], x_ref.at[dst_slot, pl.ds(dst, tile), :],
                                     sem.at[dst_slot])

    def wait_rows(dst_slot):
        pltpu.make_async_copy(h_ref.at[pl.ds(0, GATHER_ROWS * tile), :], x_ref.at[dst_slot],
                              sem.at[dst_slot]).wait()

    @pl.when(jnp.logical_and(e == 0, f == 0))
    def _():
        def issue(i, carry):
            for u in range(PROLOGUE_UNROLL):
                row_copy(0, i * PROLOGUE_UNROLL + u, 0).start()
            return carry
        lax.fori_loop(0, GATHER_ROWS // PROLOGUE_UNROLL, issue, 0)
        acc_ref[...] = jnp.zeros_like(acc_ref)

    @pl.when(f == 0)
    def _():
        wait_rows(slot)
        for b in range(BATCH):
            cols = [x_ref[slot, pl.ds(b * CAP * tile + j, CAP, stride=tile), :] for j in range(tile)]
            xb_ref[b] = jnp.concatenate(cols, axis=1).astype(BF16)

    wg = wg_ref[0, 0].astype(BF16)
    wu = wu_ref[0, 0].astype(BF16)
    wd = wd_ref[0, 0].astype(BF16)
    for c in range(BATCH // CHUNK_SEQS):
        first = (f * (BATCH // CHUNK_SEQS) + c) * ROWS_PER_CHUNK
        for j in range(ROWS_PER_CHUNK):
            row_copy(next_e, first + j, next_slot).start()
        seqs = slice(c * CHUNK_SEQS, (c + 1) * CHUNK_SEQS)
        x = xb_ref[seqs].reshape(CHUNK_SEQS * CAP, D_MODEL)
        a = jnp.dot(x, wg, preferred_element_type=F32)
        u = jnp.dot(x, wu, preferred_element_type=F32)
        act = ((a * jax.nn.sigmoid(a)) * u).astype(BF16)
        carried = jnp.where(f == 0, 0.0, acc_ref[seqs])
        part = jnp.dot(act, wd, preferred_element_type=F32).reshape(CHUNK_SEQS, CAP, D_MODEL)
        acc_ref[seqs] = carried + part

    @pl.when(f == FF_STEPS - 1)
    def _():
        gates = gates_ref[0]
        for b in range(BATCH):
            y_ref[0, b] = (acc_ref[b] * gates[:, b:b + 1]).astype(BF16)

    @pl.when(jnp.logical_and(e == N_EXPERTS - 1, f == FF_STEPS - 1))
    def _():
        wait_rows(next_slot)


def _experts(tok, h, gates, w_gate, w_up, w_down, layer):
    grid_spec = pltpu.PrefetchScalarGridSpec(
        num_scalar_prefetch=1,
        grid=(N_EXPERTS, FF_STEPS),
        in_specs=[
            pl.BlockSpec(memory_space=pl.ANY),
            pl.BlockSpec((1, CAP, BATCH), lambda e, f, tok: (e, 0, 0)),
            pl.BlockSpec((1, 1, D_MODEL, FF_TILE), lambda e, f, tok: (layer, e, 0, f)),
            pl.BlockSpec((1, 1, D_MODEL, FF_TILE), lambda e, f, tok: (layer, e, 0, f)),
            pl.BlockSpec((1, 1, FF_TILE, D_MODEL), lambda e, f, tok: (layer, e, f, 0)),
        ],
        out_specs=pl.BlockSpec((1, BATCH, CAP, D_MODEL), lambda e, f, tok: (e, 0, 0, 0)),
        scratch_shapes=[pltpu.VMEM((2, GATHER_ROWS * TOKEN_TILE_ROWS, LANES), F32), pltpu.SemaphoreType.DMA((2,)),
                        pltpu.VMEM((BATCH, CAP, D_MODEL), BF16), pltpu.VMEM((BATCH, CAP, D_MODEL), F32)],
    )
    return pl.pallas_call(
        _expert_kernel,
        grid_spec=grid_spec,
        out_shape=jax.ShapeDtypeStruct((N_EXPERTS, BATCH, CAP, D_MODEL), BF16),
        compiler_params=_params(("arbitrary", "arbitrary"), 60),
        name="moe_experts",
    )(tok, h, gates, w_gate, w_up, w_down)


COMBINE_WINDOW = 128
SLOT_ALIGN = 16


def _combine_kernel(start_ref, short_ref, slot_ref, y_ref, x_ref, gate_ref, g_ref, o_ref, *, final):
    tile = pl.program_id(0) * (SEQ // COMBINE_ROWS) + pl.program_id(1)
    slots = slot_ref[0]

    def finish(moe):
        x = x_ref[0] + gate_ref[0] * moe
        if final:
            x = (x * lax.rsqrt(jnp.mean(x * x, axis=-1, keepdims=True) + RMS_EPS)) * g_ref[...]
        o_ref[0] = x

    @pl.when(short_ref[tile] == 1)
    def _():
        width = N_EXPERTS * COMBINE_WINDOW
        expert = lax.broadcasted_iota(jnp.int32, (N_EXPERTS, width), 0)
        lane = lax.broadcasted_iota(jnp.int32, (N_EXPERTS, width), 1)
        spread = jnp.where((lane >> (COMBINE_WINDOW.bit_length() - 1)) == expert, 1.0, 0.0).astype(BF16)
        slot_lanes = jnp.dot(slots.astype(F32).astype(BF16), spread, preferred_element_type=F32)
        starts, rows = [], []
        for e in range(N_EXPERTS):
            start = pl.multiple_of(start_ref[tile * N_EXPERTS + e], SLOT_ALIGN)
            starts.append(jnp.full((1, COMBINE_WINDOW), start, jnp.int32))
            rows.append(y_ref[e, 0, pl.ds(start, COMBINE_WINDOW), :])
        wanted = (jnp.concatenate(starts, axis=1) + jnp.bitwise_and(lane[:1], COMBINE_WINDOW - 1)).astype(F32)
        hits = jnp.where(slot_lanes == wanted, 1.0, 0.0).astype(BF16)
        finish(jnp.dot(hits, jnp.concatenate(rows, axis=0), preferred_element_type=F32))

    @pl.when(short_ref[tile] != 1)
    def _():
        lane = lax.broadcasted_iota(jnp.int32, (COMBINE_ROWS, CAP), 1)
        onehot = jnp.concatenate(
            [jnp.where(slots[:, e:e + 1] == lane, 1.0, 0.0).astype(BF16) for e in range(N_EXPERTS)], axis=1)
        finish(jnp.dot(onehot, y_ref[...].reshape(N_EXPERTS * CAP, D_MODEL), preferred_element_type=F32))


def _combine_windows(first):
    last = jnp.concatenate([first[..., 1:], jnp.full(first.shape[:2] + (1,), CAP, jnp.int32)], axis=-1)
    start = jnp.minimum(first // SLOT_ALIGN * SLOT_ALIGN, CAP - COMBINE_WINDOW)
    short = jnp.all(last <= start + COMBINE_WINDOW, axis=1)
    return start.transpose(0, 2, 1).reshape(-1), short.astype(jnp.int32).reshape(-1)


def _combine(starts, short, slot_t, y, x, mod, final_g, layer, final):
    grid_spec = pltpu.PrefetchScalarGridSpec(
        num_scalar_prefetch=2,
        grid=(BATCH, SEQ // COMBINE_ROWS),
        in_specs=[
            pl.BlockSpec((1, COMBINE_ROWS, N_EXPERTS), lambda b, t, s, f: (b, t, 0)),
            pl.BlockSpec((N_EXPERTS, 1, CAP, D_MODEL), lambda b, t, s, f: (0, b, 0, 0)),
            pl.BlockSpec((1, COMBINE_ROWS, D_MODEL), lambda b, t, s, f: (b, t, 0)),
            pl.BlockSpec((1, 1, D_MODEL), lambda b, t, s, f: ((layer * 6 + 5) * BATCH + b, 0, 0)),
            pl.BlockSpec((1, D_MODEL), lambda b, t, s, f: (0, 0)),
        ],
        out_specs=pl.BlockSpec((1, COMBINE_ROWS, D_MODEL), lambda b, t, s, f: (b, t, 0)),
    )
    return pl.pallas_call(
        functools.partial(_combine_kernel, final=final),
        grid_spec=grid_spec,
        out_shape=jax.ShapeDtypeStruct((BATCH, SEQ, D_MODEL), F32),
        compiler_params=_params(("arbitrary", "arbitrary"), 48),
        name="moe_combine",
    )(starts, short, slot_t, y, x, mod, final_g.reshape(1, D_MODEL))


def _scaled_q_weight(w, q_columns):
    scale = jnp.concatenate([jnp.full((q_columns,), Q_SCALE, F32), jnp.ones((w.shape[1] - q_columns,), F32)])
    return (w * scale).astype(BF16)


def kernel(x, c, ada_w, ada_b, norm_g, na_w_qkv, na_w_o, na_rpb, sw_w_qkv, sw_w_o, sw_sinks, t5_bias,
           moe_w_router, moe_w_gate, moe_w_up, moe_w_down, final_g):
    mod = _ada(c, ada_w, ada_b)
    norm_g3 = norm_g.reshape(DEPTH * 2, 1, D_MODEL)
    x = x.reshape(BATCH * SEQ, D_MODEL)
    for layer in range(DEPTH):
        j = layer // N_MIXERS
        if layer % N_MIXERS == 0:
            qkv = _qkv(x, norm_g3, mod, _scaled_q_weight(na_w_qkv[j], NA_HEADS * HEAD_DIM), layer)
            o = _na_attention(qkv.reshape(BATCH, SEQ, -1), _na_rpb_rows(na_rpb[j]))
            w_o = na_w_o[j]
        else:
            qkv = _qkv(x, norm_g3, mod, _scaled_q_weight(sw_w_qkv[j], SW_Q_HEADS * HEAD_DIM), layer,
                       head_copies=2 * SW_KV_HEADS * HEAD_DIM // LANES)
            o = _sw_attention(qkv.reshape(BATCH, SEQ, -1), _sw_rel_table(t5_bias), sw_sinks[j])
            w_o = sw_w_o[j]
        w_router = moe_w_router[layer].astype(BF16)
        x, h, aff_t = _post_attn(o.reshape(BATCH * SEQ, -1), w_o.astype(BF16), x, norm_g3, mod,
                                 w_router.T, layer)
        slot, tok, gate, first = _route(aff_t)
        tok = tok.reshape(BATCH, N_EXPERTS, CAP)
        rows = tok.transpose(1, 0, 2) + (jnp.arange(BATCH, dtype=jnp.int32) * SEQ)[None, :, None]
        gates = gate.reshape(BATCH, N_EXPERTS, CAP).transpose(1, 2, 0)
        y = _experts(rows.reshape(-1), h, gates, moe_w_gate, moe_w_up, moe_w_down, layer)
        slot_t = slot.reshape(BATCH, N_EXPERTS, SEQ).transpose(0, 2, 1)
        starts, short = _combine_windows(first.reshape(BATCH, N_EXPERTS, LANES)[..., :SEQ // COMBINE_ROWS])
        x = _combine(starts, short, slot_t, y, x.reshape(BATCH, SEQ, D_MODEL), mod, final_g, layer,
                     final=layer == DEPTH - 1).reshape(BATCH * SEQ, D_MODEL)
    return x.reshape(BATCH, SEQ, D_MODEL)
```

```python
import functools

import numpy as np
import jax
import jax.numpy as jnp
from jax import lax
from jax.experimental import pallas as pl
from jax.experimental.pallas import tpu as pltpu

D_MODEL = 1024
BATCH = 8
SEQ = 2048
DEPTH = 2
GRID_W = 64
ROWS = SEQ // GRID_W
N_MIXERS = 2
HEAD_DIM = 64
NA_HEADS = 16
NA_WIN_H = 8
NA_WIN_W = 16
SW_Q_HEADS = 16
SW_KV_HEADS = 4
SW_GROUP = SW_Q_HEADS // SW_KV_HEADS
SW_WINDOW = 128
SW_BLOCK = 128
SW_NB = SEQ // SW_BLOCK
SW_SPAN = 3 * SW_BLOCK
T5_BUCKETS = 32
T5_MAX_DIST = 128
N_EXPERTS = 16
EXPERT_FF = 2048
EC_CAPACITY = 2
CAP = EC_CAPACITY * SEQ // N_EXPERTS
RMS_EPS = 1e-6
NEG = -1e30
LOG2E = 1.4426950408889634
Q_SCALE = HEAD_DIM ** -0.5 * LOG2E

LANES = 128
TOKEN_TILE_ROWS = D_MODEL // LANES
MIB = 1024 * 1024
F32 = jnp.float32
BF16 = jnp.bfloat16

ROW_TILE = 1024
POST_TILE = 1024
ATTN_SEQS = 4
FF_TILE = 512
COMBINE_ROWS = 512
PREFIX_CHUNK = 256


def _params(semantics, vmem_mib):
    return pltpu.CompilerParams(dimension_semantics=semantics, vmem_limit_bytes=vmem_mib * MIB)


def _norm_mod(x, g, sc, sh):
    y = x * lax.rsqrt(jnp.mean(x * x, axis=-1, keepdims=True) + RMS_EPS)
    return (y * g) * (1.0 + sc) + sh


def _softmax0(z):
    z = z - jnp.max(z, axis=0, keepdims=True)
    p = jnp.exp(z)
    return p / jnp.sum(p, axis=0, keepdims=True)


def _half_masks(rows):
    lane = lax.broadcasted_iota(jnp.int32, (rows, LANES), 1)
    lo = jnp.where(lane < HEAD_DIM, 1.0, 0.0).astype(BF16)
    hi = jnp.where(lane < HEAD_DIM, 0.0, 1.0).astype(BF16)
    return lo, hi


def _ada_kernel(c_ref, w_ref, b_ref, o_ref):
    c = c_ref[...]
    act = (c * jax.nn.sigmoid(c)).astype(BF16)
    o_ref[0] = jnp.dot(act, w_ref[0].astype(BF16), preferred_element_type=F32) + b_ref[0]


def _ada(c, ada_w, ada_b):
    out = pl.pallas_call(
        _ada_kernel,
        grid=(DEPTH, 6),
        in_specs=[
            pl.BlockSpec((BATCH, D_MODEL), lambda l, k: (0, 0)),
            pl.BlockSpec((1, D_MODEL, D_MODEL), lambda l, k: (l, 0, k)),
            pl.BlockSpec((1, 1, D_MODEL), lambda l, k: (l * 6 + k, 0, 0)),
        ],
        out_specs=pl.BlockSpec((1, BATCH, D_MODEL), lambda l, k: (l * 6 + k, 0, 0)),
        out_shape=jax.ShapeDtypeStruct((DEPTH * 6, BATCH, D_MODEL), F32),
        compiler_params=_params(("arbitrary", "arbitrary"), 32),
        name="ada_mod",
    )(c, ada_w, ada_b.reshape(DEPTH * 6, 1, D_MODEL))
    return out.reshape(DEPTH * 6 * BATCH, 1, D_MODEL)


def _mod_spec(layer, chunk, row_tile=ROW_TILE):
    tiles_per_seq = SEQ // row_tile
    return pl.BlockSpec((1, 1, D_MODEL),
                        lambda i: ((layer * 6 + chunk) * BATCH + i // tiles_per_seq, 0, 0))


def _qkv_kernel(x_ref, g_ref, sc_ref, sh_ref, w_ref, o_ref, *, head_copies):
    h = _norm_mod(x_ref[...], g_ref[0], sc_ref[0], sh_ref[0])
    y = jnp.dot(h.astype(BF16), w_ref[...], preferred_element_type=F32)
    if head_copies:
        low_half = lax.broadcasted_iota(jnp.int32, (ROW_TILE, LANES), 1) < HEAD_DIM
        keep = y.shape[1] - head_copies * LANES
        tiles = [y[:, :keep]]
        for t in range(head_copies):
            pair = y[:, keep + t * LANES:keep + (t + 1) * LANES]
            swapped = pltpu.roll(pair, HEAD_DIM, 1)
            tiles += [jnp.where(low_half, pair, swapped), jnp.where(low_half, swapped, pair)]
        y = jnp.concatenate(tiles, axis=1)
    o_ref[...] = y.astype(BF16)


def _qkv(x, norm_g3, mod, w, layer, head_copies=0):
    n = w.shape[1] + head_copies * LANES
    return pl.pallas_call(
        functools.partial(_qkv_kernel, head_copies=head_copies),
        grid=(BATCH * SEQ // ROW_TILE,),
        in_specs=[
            pl.BlockSpec((ROW_TILE, D_MODEL), lambda i: (i, 0)),
            pl.BlockSpec((1, 1, D_MODEL), lambda i: (layer * 2, 0, 0)),
            _mod_spec(layer, 1),
            _mod_spec(layer, 0),
            pl.BlockSpec(w.shape, lambda i: (0, 0)),
        ],
        out_specs=pl.BlockSpec((ROW_TILE, n), lambda i: (i, 0)),
        out_shape=jax.ShapeDtypeStruct((BATCH * SEQ, n), BF16),
        compiler_params=_params(("arbitrary",), 60),
        name="norm_qkv",
    )(x, norm_g3, mod, mod, w)


NA_BIAS_ROWS = 2 * NA_WIN_H - 1
NA_ROW_UNROLL = 4


def _na_rpb_rows(rpb):
    w = NA_WIN_W - 1
    rpb = rpb.astype(F32) * LOG2E
    pad = jnp.zeros(rpb.shape[:2] + (LANES - 2 * w - 1,), F32)
    return jnp.concatenate([rpb[..., w:], pad, rpb[..., :w]], axis=-1)


def _na_build_bias(w_ref, bias_ref):
    lane = lax.broadcasted_iota(jnp.int32, (GRID_W, LANES), 1)
    col = lax.broadcasted_iota(jnp.int32, (GRID_W, LANES), 0)
    kc = jnp.bitwise_and(lane, GRID_W - 1)
    cstart = jnp.clip(col - NA_WIN_W // 2, 0, GRID_W - NA_WIN_W)
    in_window = jnp.where(kc >= cstart, jnp.where(kc < cstart + NA_WIN_W, 1.0, 0.0), 0.0) > 0.5
    low_half = lane < GRID_W

    def toeplitz(head, a, shift):
        row = jnp.broadcast_to(w_ref[head, a:a + 1, :], (GRID_W, LANES))
        return pltpu.roll(row, shift, 1, stride=1, stride_axis=0)

    for head in range(2):
        tiles = [jnp.where(in_window, jnp.where(low_half, toeplitz(head, a, 0), toeplitz(head, a + 1, GRID_W)), NEG)
                 for a in range(NA_BIAS_ROWS - 1)]
        for d in range(NA_WIN_H):
            for m in range(NA_WIN_H // 2):
                bias_ref[d, head * GRID_W:(head + 1) * GRID_W, m * LANES:(m + 1) * LANES] = (
                    tiles[NA_WIN_H - 1 - d + 2 * m])


def _na_kernel(w_ref, q_ref, k_ref, v_ref, o_ref, bias_ref, s0_ref, s1_ref, p0_ref, p1_ref):
    @pl.when(pl.program_id(1) == 0)
    def _():
        _na_build_bias(w_ref, bias_ref)

    lo, hi = _half_masks(GRID_W)
    lo_f = lax.broadcasted_iota(jnp.int32, (GRID_W, LANES), 1) < HEAD_DIM
    span = NA_WIN_H * GRID_W
    ones = jnp.ones((span, LANES), BF16)
    seq_groups = ROWS // NA_ROW_UNROLL
    groups = ATTN_SEQS * seq_groups

    def group_rows(i):
        g = jnp.clip(i, 0, groups - 1)
        for u in range(NA_ROW_UNROLL):
            r = lax.rem(g, seq_groups) * NA_ROW_UNROLL + u
            yield u, lax.div(g, seq_groups), r, jnp.clip(r - NA_WIN_H // 2, 0, ROWS - NA_WIN_H)

    def scores(i, s_ref):
        for u, b, r, rs in group_rows(i):
            q = q_ref[b, pl.ds(pl.multiple_of(r * GRID_W, GRID_W), GRID_W), :]
            lhs = jnp.concatenate([q * lo, q * hi], axis=0)
            kw = k_ref[b, pl.ds(pl.multiple_of(rs * GRID_W, GRID_W), span), :]
            s_ref[u] = lax.dot_general(lhs, kw, (((1,), (1,)), ((), ())), preferred_element_type=F32)

    def softmax(i, s_ref, p_ref):
        for u, b, r, rs in group_rows(i):
            s = s_ref[u] + bias_ref[r - rs]
            m = jnp.broadcast_to(jnp.max(s, axis=-1, keepdims=True), (2 * GRID_W, LANES))
            p_ref[u] = jnp.exp2(s - jnp.concatenate([m] * (span // LANES), axis=1)).astype(BF16)

    def values(i, p_ref):
        for u, b, r, rs in group_rows(i):
            vw = v_ref[b, pl.ds(pl.multiple_of(rs * GRID_W, GRID_W), span), :]
            o = jnp.dot(p_ref[u], jnp.concatenate([vw, ones], axis=1), preferred_element_type=F32)
            o = o[:, :LANES] / o[:, LANES:]
            out = jnp.where(lo_f, o[:GRID_W], o[GRID_W:])
            o_ref[b, pl.ds(pl.multiple_of(r * GRID_W, GRID_W), GRID_W), :] = out.astype(BF16)

    def step(i, carry):
        values(2 * i - 1, p1_ref)
        scores(2 * i + 1, s1_ref)
        softmax(2 * i, s0_ref, p0_ref)
        values(2 * i, p0_ref)
        scores(2 * i + 2, s0_ref)
        softmax(2 * i + 1, s1_ref, p1_ref)
        return carry

    p1_ref[...] = jnp.ones_like(p1_ref)
    scores(0, s0_ref)
    lax.fori_loop(0, groups // 2, step, 0)
    values(groups - 1, p1_ref)


def _na_attention(qkv, rpb_rows):
    pairs = NA_HEADS // 2
    return pl.pallas_call(
        _na_kernel,
        grid=(pairs, BATCH // ATTN_SEQS),
        in_specs=[
            pl.BlockSpec((2, NA_BIAS_ROWS, LANES), lambda p, b: (p, 0, 0)),
            pl.BlockSpec((ATTN_SEQS, SEQ, LANES), lambda p, b: (b, 0, p)),
            pl.BlockSpec((ATTN_SEQS, SEQ, LANES), lambda p, b: (b, 0, pairs + p)),
            pl.BlockSpec((ATTN_SEQS, SEQ, LANES), lambda p, b: (b, 0, 2 * pairs + p)),
        ],
        out_specs=pl.BlockSpec((ATTN_SEQS, SEQ, LANES), lambda p, b: (b, 0, p)),
        out_shape=jax.ShapeDtypeStruct((BATCH, SEQ, NA_HEADS * HEAD_DIM), BF16),
        scratch_shapes=[pltpu.VMEM((NA_WIN_H, 2 * GRID_W, NA_WIN_H * GRID_W), F32),
                        pltpu.VMEM((NA_ROW_UNROLL, 2 * GRID_W, NA_WIN_H * GRID_W), F32),
                        pltpu.VMEM((NA_ROW_UNROLL, 2 * GRID_W, NA_WIN_H * GRID_W), F32),
                        pltpu.VMEM((NA_ROW_UNROLL, 2 * GRID_W, NA_WIN_H * GRID_W), BF16),
                        pltpu.VMEM((NA_ROW_UNROLL, 2 * GRID_W, NA_WIN_H * GRID_W), BF16)],
        compiler_params=_params(("arbitrary", "arbitrary"), 32),
        name="na_attention",
    )(rpb_rows, qkv, qkv, qkv)


def _t5_buckets(rel):
    half = T5_BUCKETS // 2
    max_exact = half // 2
    n = np.abs(rel)
    large = max_exact + (np.log(np.maximum(n, 1) / max_exact)
                         / np.log(T5_MAX_DIST / max_exact) * (half - max_exact)).astype(np.int32)
    large = np.minimum(large, half - 1)
    return (rel > 0).astype(np.int32) * half + np.where(n < max_exact, n, large)


SW_REL_PERIOD = 512
SW_BLOCK_UNROLL = 2


def _sw_rel_table(t5_table):
    k = np.arange(SW_REL_PERIOD)
    rel = np.where(k < SW_REL_PERIOD // 2, k, k - SW_REL_PERIOD)
    ok = np.abs(rel) <= SW_WINDOW
    vals = jnp.where(ok[:, None], t5_table[_t5_buckets(rel)].astype(F32) * LOG2E, NEG)
    return vals.T.reshape(SW_KV_HEADS, SW_GROUP, SW_REL_PERIOD)


def _sw_build_bias(rel_ref, bias_ref):
    for g in range(SW_GROUP):
        row = jnp.broadcast_to(rel_ref[0, g:g + 1, :], (SW_BLOCK, SW_REL_PERIOD))
        t = pltpu.roll(row, 0, 1, stride=1, stride_axis=0)
        rows = slice(g * SW_BLOCK, (g + 1) * SW_BLOCK)
        bias_ref[0, rows, :] = t[:, :SW_SPAN]
        bias_ref[1, rows, :] = jnp.concatenate([t[:, SW_SPAN:], t[:, :2 * SW_BLOCK]], axis=1)
        bias_ref[2, rows, :] = jnp.concatenate([t[:, 2 * SW_BLOCK:], t[:, :SW_BLOCK]], axis=1)


def _sw_kernel(sink_ref, rel_ref, q_ref, k_ref, v_ref, o_ref, bias_ref, s0_ref, s1_ref, p0_ref, p1_ref,
               t0_ref, t1_ref):
    j = pl.program_id(0)
    ones = jnp.ones((SW_SPAN, LANES), BF16)

    @pl.when(pl.program_id(1) == 0)
    def _():
        _sw_build_bias(rel_ref, bias_ref)

    masks = _half_masks(SW_BLOCK)
    lo_f = lax.broadcasted_iota(jnp.int32, (SW_BLOCK, LANES), 1) < HEAD_DIM
    sinks = [jnp.full((SW_BLOCK, LANES), sink_ref[j * SW_GROUP + g] * LOG2E, F32) for g in range(SW_GROUP)]
    seq_groups = SW_NB // SW_BLOCK_UNROLL
    groups = ATTN_SEQS * seq_groups

    def group_blocks(i):
        g = jnp.clip(i, 0, groups - 1)
        for u in range(SW_BLOCK_UNROLL):
            n = lax.rem(g, seq_groups) * SW_BLOCK_UNROLL + u
            yield u, lax.div(g, seq_groups), n, jnp.clip(n - 1, 0, SW_NB - 3)

    def block_rows(n):
        return pl.ds(pl.multiple_of(n * SW_BLOCK, SW_BLOCK), SW_BLOCK)

    def scores(i, s_ref):
        for u, b, n, first in group_blocks(i):
            kw = k_ref[b, pl.ds(pl.multiple_of(first * SW_BLOCK, SW_BLOCK), SW_SPAN), :]
            q = jnp.concatenate([q_ref[b, block_rows(n), (g // 2) * LANES:(g // 2 + 1) * LANES] * masks[g % 2]
                                 for g in range(SW_GROUP)], axis=0)
            s = lax.dot_general(q, kw, (((1,), (1,)), ((), ())), preferred_element_type=F32)
            s_ref[u * SW_GROUP:(u + 1) * SW_GROUP] = s.reshape(SW_GROUP, SW_BLOCK, SW_SPAN)

    def softmax(i, s_ref, p_ref, t_ref):
        for u, b, n, first in group_blocks(i):
            for g in range(SW_GROUP):
                c = u * SW_GROUP + g
                s = s_ref[c] + bias_ref[n - first, g * SW_BLOCK:(g + 1) * SW_BLOCK, :]
                m = jnp.maximum(jnp.broadcast_to(jnp.max(s, axis=-1, keepdims=True), (SW_BLOCK, LANES)), sinks[g])
                p_ref[c] = jnp.exp2(s - jnp.concatenate([m] * (SW_SPAN // LANES), axis=1)).astype(BF16)
                t_ref[c] = jnp.exp2(sinks[g] - m)

    def values(i, p_ref, t_ref):
        for u, b, n, first in group_blocks(i):
            vw = v_ref[b, pl.ds(pl.multiple_of(first * SW_BLOCK, SW_BLOCK), SW_SPAN), :]
            v_ones = jnp.concatenate([vw, ones], axis=1)
            chains = slice(u * SW_GROUP, (u + 1) * SW_GROUP)
            o = jnp.dot(p_ref[chains].reshape(SW_GROUP * SW_BLOCK, SW_SPAN), v_ones, preferred_element_type=F32)
            o = o[:, :LANES] / (o[:, LANES:] + t_ref[chains].reshape(SW_GROUP * SW_BLOCK, LANES))
            heads = [o[g * SW_BLOCK:(g + 1) * SW_BLOCK] for g in range(SW_GROUP)]
            out = jnp.concatenate([jnp.where(lo_f, heads[0], heads[1]), jnp.where(lo_f, heads[2], heads[3])],
                                  axis=1)
            o_ref[b, block_rows(n), :] = out.astype(BF16)

    def step(i, carry):
        values(2 * i - 1, p1_ref, t1_ref)
        scores(2 * i + 1, s1_ref)
        softmax(2 * i, s0_ref, p0_ref, t0_ref)
        values(2 * i, p0_ref, t0_ref)
        scores(2 * i + 2, s0_ref)
        softmax(2 * i + 1, s1_ref, p1_ref, t1_ref)
        return carry

    p1_ref[...] = jnp.ones_like(p1_ref)
    t1_ref[...] = jnp.ones_like(t1_ref)
    scores(0, s0_ref)
    lax.fori_loop(0, groups // 2, step, 0)
    values(groups - 1, p1_ref, t1_ref)


def _sw_attention(qkv, rel_table, sinks):
    qw = SW_GROUP * HEAD_DIM
    k_off = SW_Q_HEADS * HEAD_DIM // LANES
    v_off = k_off + SW_KV_HEADS
    chains = SW_BLOCK_UNROLL * SW_GROUP
    return pl.pallas_call(
        _sw_kernel,
        grid=(SW_KV_HEADS, BATCH // ATTN_SEQS),
        in_specs=[
            pl.BlockSpec(memory_space=pltpu.SMEM),
            pl.BlockSpec((1, SW_GROUP, SW_REL_PERIOD), lambda j, b: (j, 0, 0)),
            pl.BlockSpec((ATTN_SEQS, SEQ, qw), lambda j, b: (b, 0, j)),
            pl.BlockSpec((ATTN_SEQS, SEQ, LANES), lambda j, b: (b, 0, k_off + j)),
            pl.BlockSpec((ATTN_SEQS, SEQ, LANES), lambda j, b: (b, 0, v_off + j)),
        ],
        out_specs=pl.BlockSpec((ATTN_SEQS, SEQ, qw), lambda j, b: (b, 0, j)),
        out_shape=jax.ShapeDtypeStruct((BATCH, SEQ, SW_Q_HEADS * HEAD_DIM), BF16),
        scratch_shapes=[pltpu.VMEM((3, SW_GROUP * SW_BLOCK, SW_SPAN), F32)]
        + [pltpu.VMEM((chains, SW_BLOCK, SW_SPAN), F32)] * 2
        + [pltpu.VMEM((chains, SW_BLOCK, SW_SPAN), BF16)] * 2
        + [pltpu.VMEM((chains, SW_BLOCK, LANES), F32)] * 2,
        compiler_params=_params(("arbitrary", "arbitrary"), 48),
        name="sw_attention",
    )(sinks, rel_table, qkv, qkv, qkv)


def _post_attn_kernel(o_ref, wo_ref, x_ref, gate_ref, g_ref, sc_ref, sh_ref, wr_ref,
                      xo_ref, h_ref, aff_ref):
    y = jnp.dot(o_ref[...], wo_ref[...], preferred_element_type=F32)
    xn = x_ref[...] + gate_ref[0] * y
    xo_ref[...] = xn
    h = _norm_mod(xn, g_ref[0], sc_ref[0], sh_ref[0])
    for j in range(TOKEN_TILE_ROWS):
        h_ref[pl.ds(j, POST_TILE, stride=TOKEN_TILE_ROWS), :] = h[:, j * LANES:(j + 1) * LANES]
    hb = h.astype(BF16)
    logits = lax.dot_general(wr_ref[...], hb, (((1,), (1,)), ((), ())), preferred_element_type=F32)
    aff_ref[...] = _softmax0(logits)


def _post_attn(o, w_o, x, norm_g3, mod, w_router_t, layer):
    rows = BATCH * SEQ
    return pl.pallas_call(
        _post_attn_kernel,
        grid=(rows // POST_TILE,),
        in_specs=[
            pl.BlockSpec((POST_TILE, D_MODEL), lambda i: (i, 0)),
            pl.BlockSpec((D_MODEL, D_MODEL), lambda i: (0, 0)),
            pl.BlockSpec((POST_TILE, D_MODEL), lambda i: (i, 0)),
            _mod_spec(layer, 2, POST_TILE),
            pl.BlockSpec((1, 1, D_MODEL), lambda i: (layer * 2 + 1, 0, 0)),
            _mod_spec(layer, 4, POST_TILE),
            _mod_spec(layer, 3, POST_TILE),
            pl.BlockSpec((N_EXPERTS, D_MODEL), lambda i: (0, 0)),
        ],
        out_specs=[
            pl.BlockSpec((POST_TILE, D_MODEL), lambda i: (i, 0)),
            pl.BlockSpec((POST_TILE * TOKEN_TILE_ROWS, LANES), lambda i: (i, 0)),
            pl.BlockSpec((N_EXPERTS, POST_TILE), lambda i: (0, i)),
        ],
        out_shape=[
            jax.ShapeDtypeStruct((rows, D_MODEL), F32),
            jax.ShapeDtypeStruct((rows * TOKEN_TILE_ROWS, LANES), F32),
            jax.ShapeDtypeStruct((N_EXPERTS, rows), F32),
        ],
        compiler_params=_params(("arbitrary",), 56),
        name="post_attn",
    )(o, w_o, x, mod, norm_g3, mod, mod, w_router_t)


ROUTE_SEQS = BATCH
ROUTE_ROWS = ROUTE_SEQS * N_EXPERTS
ROUTE_BITS = SEQ.bit_length() - 1
ROUTE_VALID = 1 << (2 * ROUTE_BITS)
F32_INF_BITS = 0x7F800000


def _prefix_count(x):
    rows = x.shape[0]
    nchunk = SEQ // PREFIX_CHUNK
    r = lax.broadcasted_iota(jnp.int32, (PREFIX_CHUNK, PREFIX_CHUNK), 0)
    c = lax.broadcasted_iota(jnp.int32, (PREFIX_CHUNK, PREFIX_CHUNK), 1)
    upper = jnp.where(r < c, 1.0, 0.0).astype(BF16)
    chunks = [x[:, k * PREFIX_CHUNK:(k + 1) * PREFIX_CHUNK] for k in range(nchunk)]
    local = jnp.dot(jnp.concatenate(chunks, axis=0).astype(BF16), upper, preferred_element_type=F32)
    out = []
    offset = jnp.zeros((rows, 1), F32)
    for k in range(nchunk):
        out.append(local[k * rows:(k + 1) * rows] + offset)
        offset = offset + jnp.sum(chunks[k], axis=1, keepdims=True)
    return jnp.concatenate(out, axis=1)


def _route_kernel(aff_ref, slot_ref, tok_ref, gate_ref, first_ref):
    aff = jnp.concatenate([aff_ref[:, b * SEQ:(b + 1) * SEQ] for b in range(ROUTE_SEQS)], axis=0)
    bits = pltpu.bitcast(aff, jnp.int32)

    def count(mask):
        return jnp.sum(jnp.where(mask, 1.0, 0.0), axis=1, keepdims=True)

    def search(_, bounds):
        lo, hi = bounds
        mid = lo + ((hi - lo) >> 1)
        ge = count(bits >= mid) >= CAP
        return jnp.where(ge, mid, lo), jnp.where(ge, hi, mid)

    lo0 = jnp.zeros((ROUTE_ROWS, 1), jnp.int32)
    hi0 = jnp.full((ROUTE_ROWS, 1), F32_INF_BITS, jnp.int32)
    tau, _ = lax.fori_loop(0, F32_INF_BITS.bit_length(), search, (lo0, hi0))
    gt = bits > tau
    eq = jnp.where(bits == tau, 1.0, 0.0)
    need = CAP - count(gt)
    sel = jnp.where(gt, 1.0, jnp.where(_prefix_count(eq) < need, eq, 0.0))
    pos = _prefix_count(sel).astype(jnp.int32)
    chosen = sel > 0.5
    slot_ref[...] = jnp.where(chosen, pos, -1)
    tile_lane = lax.broadcasted_iota(jnp.int32, (ROUTE_ROWS, LANES), 1)
    first = jnp.zeros((ROUTE_ROWS, LANES), jnp.int32)
    for k in range(SEQ // COMBINE_ROWS):
        first = jnp.where(tile_lane == k, pos[:, k * COMBINE_ROWS:k * COMBINE_ROWS + 1], first)
    first_ref[...] = first
    lane = lax.broadcasted_iota(jnp.int32, (ROUTE_ROWS, SEQ), 1)
    packed = jnp.where(chosen, ROUTE_VALID | (lane << ROUTE_BITS) | (lane - pos), 0)
    gate = aff
    for k in range(ROUTE_BITS):
        step = 1 << k
        from_right = pltpu.roll(packed, SEQ - step, 1)
        gate_right = pltpu.roll(gate, SEQ - step, 1)
        arrives = (from_right & step) != 0
        stays = (packed & step) == 0
        packed = jnp.where(arrives, from_right, jnp.where(stays, packed, 0))
        gate = jnp.where(arrives, gate_right, gate)
    tok_ref[...] = (packed[:, :CAP] >> ROUTE_BITS) & (SEQ - 1)
    gate_ref[...] = gate[:, :CAP]


def _route(aff_t):
    return pl.pallas_call(
        _route_kernel,
        grid=(BATCH // ROUTE_SEQS,),
        in_specs=[pl.BlockSpec((N_EXPERTS, ROUTE_SEQS * SEQ), lambda i: (0, i))],
        out_specs=[pl.BlockSpec((ROUTE_ROWS, SEQ), lambda i: (i, 0)),
                   pl.BlockSpec((ROUTE_ROWS, CAP), lambda i: (i, 0)),
                   pl.BlockSpec((ROUTE_ROWS, CAP), lambda i: (i, 0)),
                   pl.BlockSpec((ROUTE_ROWS, LANES), lambda i: (i, 0))],
        out_shape=[jax.ShapeDtypeStruct((BATCH * N_EXPERTS, SEQ), jnp.int32),
                   jax.ShapeDtypeStruct((BATCH * N_EXPERTS, CAP), jnp.int32),
                   jax.ShapeDtypeStruct((BATCH * N_EXPERTS, CAP), F32),
                   jax.ShapeDtypeStruct((BATCH * N_EXPERTS, LANES), jnp.int32)],
        compiler_params=_params(("arbitrary",), 48),
        name="route",
    )(aff_t)


GATHER_ROWS = BATCH * CAP
FF_STEPS = EXPERT_FF // FF_TILE
CHUNK_SEQS = 4
ROWS_PER_CHUNK = GATHER_ROWS * CHUNK_SEQS // (FF_STEPS * BATCH)
PROLOGUE_UNROLL = 8


def _expert_kernel(tok_ref, h_ref, gates_ref, wg_ref, wu_ref, wd_ref, y_ref, x_ref, sem, xb_ref, acc_ref):
    e = pl.program_id(0)
    f = pl.program_id(1)
    slot = lax.rem(e, 2)
    next_slot = 1 - slot
    next_e = jnp.minimum(e + 1, N_EXPERTS - 1)

    tile = TOKEN_TILE_ROWS

    def row_copy(expert, row, dst_slot):
        src = pl.multiple_of(tok_ref[expert * GATHER_ROWS + row] * tile, tile)
        dst = pl.multiple_of(row * tile, tile)
        return pltpu.make_async_copy(h_ref.at[pl.ds(src, tile), :], x_ref.at[dst_slot, pl.ds(dst, tile), :],
                                     sem.at[dst_slot])

    def wait_rows(dst_slot):
        pltpu.make_async_copy(h_ref.at[pl.ds(0, GATHER_ROWS * tile), :], x_ref.at[dst_slot],
                              sem.at[dst_slot]).wait()

    @pl.when(jnp.logical_and(e == 0, f == 0))
    def _():
        def issue(i, carry):
            for u in range(PROLOGUE_UNROLL):
                row_copy(0, i * PROLOGUE_UNROLL + u, 0).start(priority=u % 2)
            return carry
        lax.fori_loop(0, GATHER_ROWS // PROLOGUE_UNROLL, issue, 0)
        acc_ref[...] = jnp.zeros_like(acc_ref)

    @pl.when(f == 0)
    def _():
        wait_rows(slot)
        for b in range(BATCH):
            cols = [x_ref[slot, pl.ds(b * CAP * tile + j, CAP, stride=tile), :] for j in range(tile)]
            xb_ref[b] = jnp.concatenate(cols, axis=1).astype(BF16)

    wg = wg_ref[0, 0].astype(BF16)
    wu = wu_ref[0, 0].astype(BF16)
    wd = wd_ref[0, 0].astype(BF16)
    for c in range(BATCH // CHUNK_SEQS):
        first = (f * (BATCH // CHUNK_SEQS) + c) * ROWS_PER_CHUNK
        for j in range(ROWS_PER_CHUNK):
            row_copy(next_e, first + j, next_slot).start(priority=1)
        seqs = slice(c * CHUNK_SEQS, (c + 1) * CHUNK_SEQS)
        x = xb_ref[seqs].reshape(CHUNK_SEQS * CAP, D_MODEL)
        a = jnp.dot(x, wg, preferred_element_type=F32)
        u = jnp.dot(x, wu, preferred_element_type=F32)
        act = ((a * jax.nn.sigmoid(a)) * u).astype(BF16)
        carried = jnp.where(f == 0, 0.0, acc_ref[seqs])
        part = jnp.dot(act, wd, preferred_element_type=F32).reshape(CHUNK_SEQS, CAP, D_MODEL)
        acc_ref[seqs] = carried + part

    @pl.when(f == FF_STEPS - 1)
    def _():
        gates = gates_ref[0]
        for b in range(BATCH):
            y_ref[0, b] = (acc_ref[b] * gates[:, b:b + 1]).astype(BF16)

    @pl.when(jnp.logical_and(e == N_EXPERTS - 1, f == FF_STEPS - 1))
    def _():
        wait_rows(next_slot)


def _experts(tok, h, gates, w_gate, w_up, w_down, layer):
    grid_spec = pltpu.PrefetchScalarGridSpec(
        num_scalar_prefetch=1,
        grid=(N_EXPERTS, FF_STEPS),
        in_specs=[
            pl.BlockSpec(memory_space=pl.ANY),
            pl.BlockSpec((1, CAP, BATCH), lambda e, f, tok: (e, 0, 0)),
            pl.BlockSpec((1, 1, D_MODEL, FF_TILE), lambda e, f, tok: (layer, e, 0, f)),
            pl.BlockSpec((1, 1, D_MODEL, FF_TILE), lambda e, f, tok: (layer, e, 0, f)),
            pl.BlockSpec((1, 1, FF_TILE, D_MODEL), lambda e, f, tok: (layer, e, f, 0)),
        ],
        out_specs=pl.BlockSpec((1, BATCH, CAP, D_MODEL), lambda e, f, tok: (e, 0, 0, 0)),
        scratch_shapes=[pltpu.VMEM((2, GATHER_ROWS * TOKEN_TILE_ROWS, LANES), F32), pltpu.SemaphoreType.DMA((2,)),
                        pltpu.VMEM((BATCH, CAP, D_MODEL), BF16), pltpu.VMEM((BATCH, CAP, D_MODEL), F32)],
    )
    return pl.pallas_call(
        _expert_kernel,
        grid_spec=grid_spec,
        out_shape=jax.ShapeDtypeStruct((N_EXPERTS, BATCH, CAP, D_MODEL), BF16),
        compiler_params=_params(("arbitrary", "arbitrary"), 60),
        name="moe_experts",
    )(tok, h, gates, w_gate, w_up, w_down)


COMBINE_WINDOW = 128
SLOT_ALIGN = 16


def _combine_kernel(start_ref, short_ref, slot_ref, y_ref, x_ref, gate_ref, g_ref, o_ref, *, final):
    tile = pl.program_id(0) * (SEQ // COMBINE_ROWS) + pl.program_id(1)
    slots = slot_ref[0]

    def finish(moe):
        x = x_ref[0] + gate_ref[0] * moe
        if final:
            x = (x * lax.rsqrt(jnp.mean(x * x, axis=-1, keepdims=True) + RMS_EPS)) * g_ref[...]
        o_ref[0] = x

    @pl.when(short_ref[tile] == 1)
    def _():
        lane = lax.broadcasted_iota(jnp.int32, (COMBINE_ROWS, COMBINE_WINDOW), 1)
        hits, rows = [], []
        for e in range(N_EXPERTS):
            start = pl.multiple_of(start_ref[tile * N_EXPERTS + e], SLOT_ALIGN)
            hits.append(jnp.where(slots[:, e:e + 1] - start == lane, 1.0, 0.0).astype(BF16))
            rows.append(y_ref[e, 0, pl.ds(start, COMBINE_WINDOW), :])
        finish(jnp.dot(jnp.concatenate(hits, axis=1), jnp.concatenate(rows, axis=0), preferred_element_type=F32))

    @pl.when(short_ref[tile] != 1)
    def _():
        lane = lax.broadcasted_iota(jnp.int32, (COMBINE_ROWS, CAP), 1)
        onehot = jnp.concatenate(
            [jnp.where(slots[:, e:e + 1] == lane, 1.0, 0.0).astype(BF16) for e in range(N_EXPERTS)], axis=1)
        finish(jnp.dot(onehot, y_ref[...].reshape(N_EXPERTS * CAP, D_MODEL), preferred_element_type=F32))


def _combine_windows(first):
    last = jnp.concatenate([first[..., 1:], jnp.full(first.shape[:2] + (1,), CAP, jnp.int32)], axis=-1)
    start = jnp.minimum(first // SLOT_ALIGN * SLOT_ALIGN, CAP - COMBINE_WINDOW)
    short = jnp.all(last <= start + COMBINE_WINDOW, axis=1)
    return start.transpose(0, 2, 1).reshape(-1), short.astype(jnp.int32).reshape(-1)


def _combine(starts, short, slot_t, y, x, mod, final_g, layer, final):
    grid_spec = pltpu.PrefetchScalarGridSpec(
        num_scalar_prefetch=2,
        grid=(BATCH, SEQ // COMBINE_ROWS),
        in_specs=[
            pl.BlockSpec((1, COMBINE_ROWS, N_EXPERTS), lambda b, t, s, f: (b, t, 0)),
            pl.BlockSpec((N_EXPERTS, 1, CAP, D_MODEL), lambda b, t, s, f: (0, b, 0, 0)),
            pl.BlockSpec((1, COMBINE_ROWS, D_MODEL), lambda b, t, s, f: (b, t, 0)),
            pl.BlockSpec((1, 1, D_MODEL), lambda b, t, s, f: ((layer * 6 + 5) * BATCH + b, 0, 0)),
            pl.BlockSpec((1, D_MODEL), lambda b, t, s, f: (0, 0)),
        ],
        out_specs=pl.BlockSpec((1, COMBINE_ROWS, D_MODEL), lambda b, t, s, f: (b, t, 0)),
    )
    return pl.pallas_call(
        functools.partial(_combine_kernel, final=final),
        grid_spec=grid_spec,
        out_shape=jax.ShapeDtypeStruct((BATCH, SEQ, D_MODEL), F32),
        compiler_params=_params(("arbitrary", "arbitrary"), 48),
        name="moe_combine",
    )(starts, short, slot_t, y, x, mod, final_g.reshape(1, D_MODEL))


def _scaled_q_weight(w, q_columns):
    scale = jnp.concatenate([jnp.full((q_columns,), Q_SCALE, F32), jnp.ones((w.shape[1] - q_columns,), F32)])
    return (w * scale).astype(BF16)


def kernel(x, c, ada_w, ada_b, norm_g, na_w_qkv, na_w_o, na_rpb, sw_w_qkv, sw_w_o, sw_sinks, t5_bias,
           moe_w_router, moe_w_gate, moe_w_up, moe_w_down, final_g):
    mod = _ada(c, ada_w, ada_b)
    norm_g3 = norm_g.reshape(DEPTH * 2, 1, D_MODEL)
    x = x.reshape(BATCH * SEQ, D_MODEL)
    for layer in range(DEPTH):
        j = layer // N_MIXERS
        if layer % N_MIXERS == 0:
            qkv = _qkv(x, norm_g3, mod, _scaled_q_weight(na_w_qkv[j], NA_HEADS * HEAD_DIM), layer)
            o = _na_attention(qkv.reshape(BATCH, SEQ, -1), _na_rpb_rows(na_rpb[j]))
            w_o = na_w_o[j]
        else:
            qkv = _qkv(x, norm_g3, mod, _scaled_q_weight(sw_w_qkv[j], SW_Q_HEADS * HEAD_DIM), layer,
                       head_copies=2 * SW_KV_HEADS * HEAD_DIM // LANES)
            o = _sw_attention(qkv.reshape(BATCH, SEQ, -1), _sw_rel_table(t5_bias), sw_sinks[j])
            w_o = sw_w_o[j]
        w_router = moe_w_router[layer].astype(BF16)
        x, h, aff_t = _post_attn(o.reshape(BATCH * SEQ, -1), w_o.astype(BF16), x, norm_g3, mod,
                                 w_router.T, layer)
        slot, tok, gate, first = _route(aff_t)
        tok = tok.reshape(BATCH, N_EXPERTS, CAP)
        rows = tok.transpose(1, 0, 2) + (jnp.arange(BATCH, dtype=jnp.int32) * SEQ)[None, :, None]
        gates = gate.reshape(BATCH, N_EXPERTS, CAP).transpose(1, 2, 0)
        y = _experts(rows.reshape(-1), h, gates, moe_w_gate, moe_w_up, moe_w_down, layer)
        slot_t = slot.reshape(BATCH, N_EXPERTS, SEQ).transpose(0, 2, 1)
        starts, short = _combine_windows(first.reshape(BATCH, N_EXPERTS, LANES)[..., :SEQ // COMBINE_ROWS])
        x = _combine(starts, short, slot_t, y, x.reshape(BATCH, SEQ, D_MODEL), mod, final_g, layer,
                     final=layer == DEPTH - 1).reshape(BATCH * SEQ, D_MODEL)
    return x.reshape(BATCH, SEQ, D_MODEL)
```

```python
import functools

import numpy as np
import jax
import jax.numpy as jnp
from jax import lax
from jax.experimental import pallas as pl
from jax.experimental.pallas import tpu as pltpu

D_MODEL = 1024
BATCH = 8
SEQ = 2048
DEPTH = 2
GRID_W = 64
ROWS = SEQ // GRID_W
N_MIXERS = 2
HEAD_DIM = 64
NA_HEADS = 16
NA_WIN_H = 8
NA_WIN_W = 16
SW_Q_HEADS = 16
SW_KV_HEADS = 4
SW_GROUP = SW_Q_HEADS // SW_KV_HEADS
SW_WINDOW = 128
SW_BLOCK = 128
SW_NB = SEQ // SW_BLOCK
SW_SPAN = 3 * SW_BLOCK
T5_BUCKETS = 32
T5_MAX_DIST = 128
N_EXPERTS = 16
EXPERT_FF = 2048
EC_CAPACITY = 2
CAP = EC_CAPACITY * SEQ // N_EXPERTS
RMS_EPS = 1e-6
NEG = -1e30
LOG2E = 1.4426950408889634
Q_SCALE = HEAD_DIM ** -0.5 * LOG2E

LANES = 128
TOKEN_TILE_ROWS = D_MODEL // LANES
MIB = 1024 * 1024
F32 = jnp.float32
BF16 = jnp.bfloat16

ROW_TILE = 1024
POST_TILE = 1024
ATTN_SEQS = 4
FF_TILE = 512
COMBINE_ROWS = 512
PREFIX_CHUNK = 256


def _params(semantics, vmem_mib):
    return pltpu.CompilerParams(dimension_semantics=semantics, vmem_limit_bytes=vmem_mib * MIB)


def _norm_mod(x, g, sc, sh):
    y = x * lax.rsqrt(jnp.mean(x * x, axis=-1, keepdims=True) + RMS_EPS)
    return (y * g) * (1.0 + sc) + sh


def _softmax0(z):
    z = z - jnp.max(z, axis=0, keepdims=True)
    p = jnp.exp(z)
    return p / jnp.sum(p, axis=0, keepdims=True)


def _half_masks(rows):
    lane = lax.broadcasted_iota(jnp.int32, (rows, LANES), 1)
    lo = jnp.where(lane < HEAD_DIM, 1.0, 0.0).astype(BF16)
    hi = jnp.where(lane < HEAD_DIM, 0.0, 1.0).astype(BF16)
    return lo, hi


def _ada_kernel(c_ref, w_ref, b_ref, o_ref):
    c = c_ref[...]
    act = (c * jax.nn.sigmoid(c)).astype(BF16)
    o_ref[0] = jnp.dot(act, w_ref[0].astype(BF16), preferred_element_type=F32) + b_ref[0]


def _ada(c, ada_w, ada_b):
    out = pl.pallas_call(
        _ada_kernel,
        grid=(DEPTH, 6),
        in_specs=[
            pl.BlockSpec((BATCH, D_MODEL), lambda l, k: (0, 0)),
            pl.BlockSpec((1, D_MODEL, D_MODEL), lambda l, k: (l, 0, k)),
            pl.BlockSpec((1, 1, D_MODEL), lambda l, k: (l * 6 + k, 0, 0)),
        ],
        out_specs=pl.BlockSpec((1, BATCH, D_MODEL), lambda l, k: (l * 6 + k, 0, 0)),
        out_shape=jax.ShapeDtypeStruct((DEPTH * 6, BATCH, D_MODEL), F32),
        compiler_params=_params(("arbitrary", "arbitrary"), 32),
        name="ada_mod",
    )(c, ada_w, ada_b.reshape(DEPTH * 6, 1, D_MODEL))
    return out.reshape(DEPTH * 6 * BATCH, 1, D_MODEL)


def _mod_spec(layer, chunk, row_tile=ROW_TILE):
    tiles_per_seq = SEQ // row_tile
    return pl.BlockSpec((1, 1, D_MODEL),
                        lambda i: ((layer * 6 + chunk) * BATCH + i // tiles_per_seq, 0, 0))


def _qkv_kernel(x_ref, g_ref, sc_ref, sh_ref, w_ref, o_ref, *, head_copies):
    h = _norm_mod(x_ref[...], g_ref[0], sc_ref[0], sh_ref[0])
    y = jnp.dot(h.astype(BF16), w_ref[...], preferred_element_type=F32)
    if head_copies:
        low_half = lax.broadcasted_iota(jnp.int32, (ROW_TILE, LANES), 1) < HEAD_DIM
        keep = y.shape[1] - head_copies * LANES
        tiles = [y[:, :keep]]
        for t in range(head_copies):
            pair = y[:, keep + t * LANES:keep + (t + 1) * LANES]
            swapped = pltpu.roll(pair, HEAD_DIM, 1)
            tiles += [jnp.where(low_half, pair, swapped), jnp.where(low_half, swapped, pair)]
        y = jnp.concatenate(tiles, axis=1)
    o_ref[...] = y.astype(BF16)


def _qkv(x, norm_g3, mod, w, layer, head_copies=0):
    n = w.shape[1] + head_copies * LANES
    return pl.pallas_call(
        functools.partial(_qkv_kernel, head_copies=head_copies),
        grid=(BATCH * SEQ // ROW_TILE,),
        in_specs=[
            pl.BlockSpec((ROW_TILE, D_MODEL), lambda i: (i, 0)),
            pl.BlockSpec((1, 1, D_MODEL), lambda i: (layer * 2, 0, 0)),
            _mod_spec(layer, 1),
            _mod_spec(layer, 0),
            pl.BlockSpec(w.shape, lambda i: (0, 0)),
        ],
        out_specs=pl.BlockSpec((ROW_TILE, n), lambda i: (i, 0)),
        out_shape=jax.ShapeDtypeStruct((BATCH * SEQ, n), BF16),
        compiler_params=pltpu.CompilerParams(
            dimension_semantics=("arbitrary",), vmem_limit_bytes=60 * MIB,
            allow_input_fusion=[False, False, False, False, True]),
        name="norm_qkv",
    )(x, norm_g3, mod, mod, w)


NA_BIAS_ROWS = 2 * NA_WIN_H - 1
NA_ROW_UNROLL = 4


def _na_rpb_rows(rpb):
    w = NA_WIN_W - 1
    rpb = rpb.astype(F32) * LOG2E
    pad = jnp.zeros(rpb.shape[:2] + (LANES - 2 * w - 1,), F32)
    return jnp.concatenate([rpb[..., w:], pad, rpb[..., :w]], axis=-1)


def _na_build_bias(w_ref, bias_ref):
    lane = lax.broadcasted_iota(jnp.int32, (GRID_W, LANES), 1)
    col = lax.broadcasted_iota(jnp.int32, (GRID_W, LANES), 0)
    kc = jnp.bitwise_and(lane, GRID_W - 1)
    cstart = jnp.clip(col - NA_WIN_W // 2, 0, GRID_W - NA_WIN_W)
    in_window = jnp.where(kc >= cstart, jnp.where(kc < cstart + NA_WIN_W, 1.0, 0.0), 0.0) > 0.5
    low_half = lane < GRID_W

    def toeplitz(head, a, shift):
        row = jnp.broadcast_to(w_ref[head, a:a + 1, :], (GRID_W, LANES))
        return pltpu.roll(row, shift, 1, stride=1, stride_axis=0)

    for head in range(2):
        tiles = [jnp.where(in_window, jnp.where(low_half, toeplitz(head, a, 0), toeplitz(head, a + 1, GRID_W)), NEG)
                 for a in range(NA_BIAS_ROWS - 1)]
        for d in range(NA_WIN_H):
            for m in range(NA_WIN_H // 2):
                bias_ref[d, head * GRID_W:(head + 1) * GRID_W, m * LANES:(m + 1) * LANES] = (
                    tiles[NA_WIN_H - 1 - d + 2 * m])


def _na_kernel(w_ref, q_ref, k_ref, v_ref, o_ref, bias_ref, s0_ref, s1_ref, p0_ref, p1_ref):
    @pl.when(pl.program_id(1) == 0)
    def _():
        _na_build_bias(w_ref, bias_ref)

    lo, hi = _half_masks(GRID_W)
    lo_f = lax.broadcasted_iota(jnp.int32, (GRID_W, LANES), 1) < HEAD_DIM
    span = NA_WIN_H * GRID_W
    ones = jnp.ones((span, LANES), BF16)
    seq_groups = ROWS // NA_ROW_UNROLL
    groups = ATTN_SEQS * seq_groups

    def group_rows(i):
        g = jnp.clip(i, 0, groups - 1)
        for u in range(NA_ROW_UNROLL):
            r = lax.rem(g, seq_groups) * NA_ROW_UNROLL + u
            yield u, lax.div(g, seq_groups), r, jnp.clip(r - NA_WIN_H // 2, 0, ROWS - NA_WIN_H)

    def scores(i, s_ref):
        for u, b, r, rs in group_rows(i):
            q = q_ref[b, pl.ds(pl.multiple_of(r * GRID_W, GRID_W), GRID_W), :]
            lhs = jnp.concatenate([q * lo, q * hi], axis=0)
            kw = k_ref[b, pl.ds(pl.multiple_of(rs * GRID_W, GRID_W), span), :]
            s_ref[u] = lax.dot_general(lhs, kw, (((1,), (1,)), ((), ())), preferred_element_type=F32)

    def softmax(i, s_ref, p_ref):
        for u, b, r, rs in group_rows(i):
            s = s_ref[u] + bias_ref[r - rs]
            m = jnp.broadcast_to(jnp.max(s, axis=-1, keepdims=True), (2 * GRID_W, LANES))
            p_ref[u] = jnp.exp2(s - jnp.concatenate([m] * (span // LANES), axis=1)).astype(BF16)

    def values(i, p_ref):
        for u, b, r, rs in group_rows(i):
            vw = v_ref[b, pl.ds(pl.multiple_of(rs * GRID_W, GRID_W), span), :]
            o = jnp.dot(p_ref[u], jnp.concatenate([vw, ones], axis=1), preferred_element_type=F32)
            o = o[:, :LANES] / o[:, LANES:]
            out = jnp.where(lo_f, o[:GRID_W], o[GRID_W:])
            o_ref[b, pl.ds(pl.multiple_of(r * GRID_W, GRID_W), GRID_W), :] = out.astype(BF16)

    def step(i, carry):
        values(2 * i - 1, p1_ref)
        scores(2 * i + 1, s1_ref)
        softmax(2 * i, s0_ref, p0_ref)
        values(2 * i, p0_ref)
        scores(2 * i + 2, s0_ref)
        softmax(2 * i + 1, s1_ref, p1_ref)
        return carry

    p1_ref[...] = jnp.ones_like(p1_ref)
    scores(0, s0_ref)
    lax.fori_loop(0, groups // 2, step, 0)
    values(groups - 1, p1_ref)


def _na_attention(qkv, rpb_rows):
    pairs = NA_HEADS // 2
    return pl.pallas_call(
        _na_kernel,
        grid=(pairs, BATCH // ATTN_SEQS),
        in_specs=[
            pl.BlockSpec((2, NA_BIAS_ROWS, LANES), lambda p, b: (p, 0, 0)),
            pl.BlockSpec((ATTN_SEQS, SEQ, LANES), lambda p, b: (b, 0, p)),
            pl.BlockSpec((ATTN_SEQS, SEQ, LANES), lambda p, b: (b, 0, pairs + p)),
            pl.BlockSpec((ATTN_SEQS, SEQ, LANES), lambda p, b: (b, 0, 2 * pairs + p)),
        ],
        out_specs=pl.BlockSpec((ATTN_SEQS, SEQ, LANES), lambda p, b: (b, 0, p)),
        out_shape=jax.ShapeDtypeStruct((BATCH, SEQ, NA_HEADS * HEAD_DIM), BF16),
        scratch_shapes=[pltpu.VMEM((NA_WIN_H, 2 * GRID_W, NA_WIN_H * GRID_W), F32),
                        pltpu.VMEM((NA_ROW_UNROLL, 2 * GRID_W, NA_WIN_H * GRID_W), F32),
                        pltpu.VMEM((NA_ROW_UNROLL, 2 * GRID_W, NA_WIN_H * GRID_W), F32),
                        pltpu.VMEM((NA_ROW_UNROLL, 2 * GRID_W, NA_WIN_H * GRID_W), BF16),
                        pltpu.VMEM((NA_ROW_UNROLL, 2 * GRID_W, NA_WIN_H * GRID_W), BF16)],
        compiler_params=_params(("arbitrary", "arbitrary"), 32),
        name="na_attention",
    )(rpb_rows, qkv, qkv, qkv)


def _t5_buckets(rel):
    half = T5_BUCKETS // 2
    max_exact = half // 2
    n = np.abs(rel)
    large = max_exact + (np.log(np.maximum(n, 1) / max_exact)
                         / np.log(T5_MAX_DIST / max_exact) * (half - max_exact)).astype(np.int32)
    large = np.minimum(large, half - 1)
    return (rel > 0).astype(np.int32) * half + np.where(n < max_exact, n, large)


SW_REL_PERIOD = 512
SW_BLOCK_UNROLL = 2


def _sw_rel_table(t5_table):
    k = np.arange(SW_REL_PERIOD)
    rel = np.where(k < SW_REL_PERIOD // 2, k, k - SW_REL_PERIOD)
    ok = np.abs(rel) <= SW_WINDOW
    vals = jnp.where(ok[:, None], t5_table[_t5_buckets(rel)].astype(F32) * LOG2E, NEG)
    return vals.T.reshape(SW_KV_HEADS, SW_GROUP, SW_REL_PERIOD)


def _sw_build_bias(rel_ref, bias_ref):
    for g in range(SW_GROUP):
        row = jnp.broadcast_to(rel_ref[0, g:g + 1, :], (SW_BLOCK, SW_REL_PERIOD))
        t = pltpu.roll(row, 0, 1, stride=1, stride_axis=0)
        rows = slice(g * SW_BLOCK, (g + 1) * SW_BLOCK)
        bias_ref[0, rows, :] = t[:, :SW_SPAN]
        bias_ref[1, rows, :] = jnp.concatenate([t[:, SW_SPAN:], t[:, :2 * SW_BLOCK]], axis=1)
        bias_ref[2, rows, :] = jnp.concatenate([t[:, 2 * SW_BLOCK:], t[:, :SW_BLOCK]], axis=1)


def _sw_kernel(sink_ref, rel_ref, q_ref, k_ref, v_ref, o_ref, bias_ref, s0_ref, s1_ref, p0_ref, p1_ref,
               t0_ref, t1_ref):
    j = pl.program_id(0)
    ones = jnp.ones((SW_SPAN, LANES), BF16)

    @pl.when(pl.program_id(1) == 0)
    def _():
        _sw_build_bias(rel_ref, bias_ref)

    masks = _half_masks(SW_BLOCK)
    lo_f = lax.broadcasted_iota(jnp.int32, (SW_BLOCK, LANES), 1) < HEAD_DIM
    sinks = [jnp.full((SW_BLOCK, LANES), sink_ref[j * SW_GROUP + g] * LOG2E, F32) for g in range(SW_GROUP)]
    seq_groups = SW_NB // SW_BLOCK_UNROLL
    groups = ATTN_SEQS * seq_groups

    def group_blocks(i):
        g = jnp.clip(i, 0, groups - 1)
        for u in range(SW_BLOCK_UNROLL):
            n = lax.rem(g, seq_groups) * SW_BLOCK_UNROLL + u
            yield u, lax.div(g, seq_groups), n, jnp.clip(n - 1, 0, SW_NB - 3)

    def block_rows(n):
        return pl.ds(pl.multiple_of(n * SW_BLOCK, SW_BLOCK), SW_BLOCK)

    def scores(i, s_ref):
        for u, b, n, first in group_blocks(i):
            kw = k_ref[b, pl.ds(pl.multiple_of(first * SW_BLOCK, SW_BLOCK), SW_SPAN), :]
            q = jnp.concatenate([q_ref[b, block_rows(n), (g // 2) * LANES:(g // 2 + 1) * LANES] * masks[g % 2]
                                 for g in range(SW_GROUP)], axis=0)
            s = lax.dot_general(q, kw, (((1,), (1,)), ((), ())), preferred_element_type=F32)
            s_ref[u * SW_GROUP:(u + 1) * SW_GROUP] = s.reshape(SW_GROUP, SW_BLOCK, SW_SPAN)

    def softmax(i, s_ref, p_ref, t_ref):
        for u, b, n, first in group_blocks(i):
            for g in range(SW_GROUP):
                c = u * SW_GROUP + g
                s = s_ref[c] + bias_ref[n - first, g * SW_BLOCK:(g + 1) * SW_BLOCK, :]
                m = jnp.maximum(jnp.broadcast_to(jnp.max(s, axis=-1, keepdims=True), (SW_BLOCK, LANES)), sinks[g])
                p_ref[c] = jnp.exp2(s - jnp.concatenate([m] * (SW_SPAN // LANES), axis=1)).astype(BF16)
                t_ref[c] = jnp.exp2(sinks[g] - m)

    def values(i, p_ref, t_ref):
        for u, b, n, first in group_blocks(i):
            vw = v_ref[b, pl.ds(pl.multiple_of(first * SW_BLOCK, SW_BLOCK), SW_SPAN), :]
            v_ones = jnp.concatenate([vw, ones], axis=1)
            chains = slice(u * SW_GROUP, (u + 1) * SW_GROUP)
            o = jnp.dot(p_ref[chains].reshape(SW_GROUP * SW_BLOCK, SW_SPAN), v_ones, preferred_element_type=F32)
            o = o[:, :LANES] / (o[:, LANES:] + t_ref[chains].reshape(SW_GROUP * SW_BLOCK, LANES))
            heads = [o[g * SW_BLOCK:(g + 1) * SW_BLOCK] for g in range(SW_GROUP)]
            out = jnp.concatenate([jnp.where(lo_f, heads[0], heads[1]), jnp.where(lo_f, heads[2], heads[3])],
                                  axis=1)
            o_ref[b, block_rows(n), :] = out.astype(BF16)

    def step(i, carry):
        values(2 * i - 1, p1_ref, t1_ref)
        scores(2 * i + 1, s1_ref)
        softmax(2 * i, s0_ref, p0_ref, t0_ref)
        values(2 * i, p0_ref, t0_ref)
        scores(2 * i + 2, s0_ref)
        softmax(2 * i + 1, s1_ref, p1_ref, t1_ref)
        return carry

    p1_ref[...] = jnp.ones_like(p1_ref)
    t1_ref[...] = jnp.ones_like(t1_ref)
    scores(0, s0_ref)
    lax.fori_loop(0, groups // 2, step, 0)
    values(groups - 1, p1_ref, t1_ref)


def _sw_attention(qkv, rel_table, sinks):
    qw = SW_GROUP * HEAD_DIM
    k_off = SW_Q_HEADS * HEAD_DIM // LANES
    v_off = k_off + SW_KV_HEADS
    chains = SW_BLOCK_UNROLL * SW_GROUP
    return pl.pallas_call(
        _sw_kernel,
        grid=(SW_KV_HEADS, BATCH // ATTN_SEQS),
        in_specs=[
            pl.BlockSpec(memory_space=pltpu.SMEM),
            pl.BlockSpec((1, SW_GROUP, SW_REL_PERIOD), lambda j, b: (j, 0, 0)),
            pl.BlockSpec((ATTN_SEQS, SEQ, qw), lambda j, b: (b, 0, j)),
            pl.BlockSpec((ATTN_SEQS, SEQ, LANES), lambda j, b: (b, 0, k_off + j)),
            pl.BlockSpec((ATTN_SEQS, SEQ, LANES), lambda j, b: (b, 0, v_off + j)),
        ],
        out_specs=pl.BlockSpec((ATTN_SEQS, SEQ, qw), lambda j, b: (b, 0, j)),
        out_shape=jax.ShapeDtypeStruct((BATCH, SEQ, SW_Q_HEADS * HEAD_DIM), BF16),
        scratch_shapes=[pltpu.VMEM((3, SW_GROUP * SW_BLOCK, SW_SPAN), F32)]
        + [pltpu.VMEM((chains, SW_BLOCK, SW_SPAN), F32)] * 2
        + [pltpu.VMEM((chains, SW_BLOCK, SW_SPAN), BF16)] * 2
        + [pltpu.VMEM((chains, SW_BLOCK, LANES), F32)] * 2,
        compiler_params=_params(("arbitrary", "arbitrary"), 48),
        name="sw_attention",
    )(sinks, rel_table, qkv, qkv, qkv)


def _post_attn_kernel(o_ref, wo_ref, x_ref, gate_ref, g_ref, sc_ref, sh_ref, wr_ref,
                      xo_ref, h_ref, aff_ref):
    y = jnp.dot(o_ref[...], wo_ref[...], preferred_element_type=F32)
    xn = x_ref[...] + gate_ref[0] * y
    xo_ref[...] = xn
    h = _norm_mod(xn, g_ref[0], sc_ref[0], sh_ref[0])
    for j in range(TOKEN_TILE_ROWS):
        h_ref[pl.ds(j, POST_TILE, stride=TOKEN_TILE_ROWS), :] = h[:, j * LANES:(j + 1) * LANES]
    hb = h.astype(BF16)
    logits = lax.dot_general(wr_ref[...], hb, (((1,), (1,)), ((), ())), preferred_element_type=F32)
    aff_ref[...] = _softmax0(logits)


def _post_attn(o, w_o, x, norm_g3, mod, w_router_t, layer):
    rows = BATCH * SEQ
    return pl.pallas_call(
        _post_attn_kernel,
        grid=(rows // POST_TILE,),
        in_specs=[
            pl.BlockSpec((POST_TILE, D_MODEL), lambda i: (i, 0)),
            pl.BlockSpec((D_MODEL, D_MODEL), lambda i: (0, 0)),
            pl.BlockSpec((POST_TILE, D_MODEL), lambda i: (i, 0)),
            _mod_spec(layer, 2, POST_TILE),
            pl.BlockSpec((1, 1, D_MODEL), lambda i: (layer * 2 + 1, 0, 0)),
            _mod_spec(layer, 4, POST_TILE),
            _mod_spec(layer, 3, POST_TILE),
            pl.BlockSpec((N_EXPERTS, D_MODEL), lambda i: (0, 0)),
        ],
        out_specs=[
            pl.BlockSpec((POST_TILE, D_MODEL), lambda i: (i, 0)),
            pl.BlockSpec((POST_TILE * TOKEN_TILE_ROWS, LANES), lambda i: (i, 0)),
            pl.BlockSpec((N_EXPERTS, POST_TILE), lambda i: (0, i)),
        ],
        out_shape=[
            jax.ShapeDtypeStruct((rows, D_MODEL), F32),
            jax.ShapeDtypeStruct((rows * TOKEN_TILE_ROWS, LANES), F32),
            jax.ShapeDtypeStruct((N_EXPERTS, rows), F32),
        ],
        compiler_params=_params(("arbitrary",), 56),
        name="post_attn",
    )(o, w_o, x, mod, norm_g3, mod, mod, w_router_t)


ROUTE_SEQS = BATCH
ROUTE_ROWS = ROUTE_SEQS * N_EXPERTS
ROUTE_BITS = SEQ.bit_length() - 1
ROUTE_VALID = 1 << (2 * ROUTE_BITS)
F32_INF_BITS = 0x7F800000


def _prefix_count(x):
    rows = x.shape[0]
    nchunk = SEQ // PREFIX_CHUNK
    r = lax.broadcasted_iota(jnp.int32, (PREFIX_CHUNK, PREFIX_CHUNK), 0)
    c = lax.broadcasted_iota(jnp.int32, (PREFIX_CHUNK, PREFIX_CHUNK), 1)
    upper = jnp.where(r < c, 1.0, 0.0).astype(BF16)
    chunks = [x[:, k * PREFIX_CHUNK:(k + 1) * PREFIX_CHUNK] for k in range(nchunk)]
    local = jnp.dot(jnp.concatenate(chunks, axis=0).astype(BF16), upper, preferred_element_type=F32)
    out = []
    offset = jnp.zeros((rows, 1), F32)
    for k in range(nchunk):
        out.append(local[k * rows:(k + 1) * rows] + offset)
        offset = offset + jnp.sum(chunks[k], axis=1, keepdims=True)
    return jnp.concatenate(out, axis=1)


def _route_kernel(aff_ref, slot_ref, tok_ref, gate_ref, first_ref):
    aff = jnp.concatenate([aff_ref[:, b * SEQ:(b + 1) * SEQ] for b in range(ROUTE_SEQS)], axis=0)
    bits = pltpu.bitcast(aff, jnp.int32)

    def count(mask):
        return jnp.sum(jnp.where(mask, 1.0, 0.0), axis=1, keepdims=True)

    def search(_, bounds):
        lo, hi = bounds
        mid = lo + ((hi - lo) >> 1)
        ge = count(bits >= mid) >= CAP
        return jnp.where(ge, mid, lo), jnp.where(ge, hi, mid)

    lo0 = jnp.zeros((ROUTE_ROWS, 1), jnp.int32)
    hi0 = jnp.full((ROUTE_ROWS, 1), F32_INF_BITS, jnp.int32)
    tau, _ = lax.fori_loop(0, F32_INF_BITS.bit_length(), search, (lo0, hi0))
    gt = bits > tau
    eq = jnp.where(bits == tau, 1.0, 0.0)
    need = CAP - count(gt)
    sel = jnp.where(gt, 1.0, jnp.where(_prefix_count(eq) < need, eq, 0.0))
    pos = _prefix_count(sel).astype(jnp.int32)
    chosen = sel > 0.5
    slot_ref[...] = jnp.where(chosen, pos, -1)
    tile_lane = lax.broadcasted_iota(jnp.int32, (ROUTE_ROWS, LANES), 1)
    first = jnp.zeros((ROUTE_ROWS, LANES), jnp.int32)
    for k in range(SEQ // COMBINE_ROWS):
        first = jnp.where(tile_lane == k, pos[:, k * COMBINE_ROWS:k * COMBINE_ROWS + 1], first)
    first_ref[...] = first
    lane = lax.broadcasted_iota(jnp.int32, (ROUTE_ROWS, SEQ), 1)
    packed = jnp.where(chosen, ROUTE_VALID | (lane << ROUTE_BITS) | (lane - pos), 0)
    gate = aff
    for k in range(ROUTE_BITS):
        step = 1 << k
        from_right = pltpu.roll(packed, SEQ - step, 1)
        gate_right = pltpu.roll(gate, SEQ - step, 1)
        arrives = (from_right & step) != 0
        stays = (packed & step) == 0
        packed = jnp.where(arrives, from_right, jnp.where(stays, packed, 0))
        gate = jnp.where(arrives, gate_right, gate)
    tok_ref[...] = (packed[:, :CAP] >> ROUTE_BITS) & (SEQ - 1)
    gate_ref[...] = gate[:, :CAP]


def _route(aff_t):
    return pl.pallas_call(
        _route_kernel,
        grid=(BATCH // ROUTE_SEQS,),
        in_specs=[pl.BlockSpec((N_EXPERTS, ROUTE_SEQS * SEQ), lambda i: (0, i))],
        out_specs=[pl.BlockSpec((ROUTE_ROWS, SEQ), lambda i: (i, 0)),
                   pl.BlockSpec((ROUTE_ROWS, CAP), lambda i: (i, 0)),
                   pl.BlockSpec((ROUTE_ROWS, CAP), lambda i: (i, 0)),
                   pl.BlockSpec((ROUTE_ROWS, LANES), lambda i: (i, 0))],
        out_shape=[jax.ShapeDtypeStruct((BATCH * N_EXPERTS, SEQ), jnp.int32),
                   jax.ShapeDtypeStruct((BATCH * N_EXPERTS, CAP), jnp.int32),
                   jax.ShapeDtypeStruct((BATCH * N_EXPERTS, CAP), F32),
                   jax.ShapeDtypeStruct((BATCH * N_EXPERTS, LANES), jnp.int32)],
        compiler_params=_params(("arbitrary",), 48),
        name="route",
    )(aff_t)


GATHER_ROWS = BATCH * CAP
FF_STEPS = EXPERT_FF // FF_TILE
CHUNK_SEQS = 4
ROWS_PER_CHUNK = GATHER_ROWS * CHUNK_SEQS // (FF_STEPS * BATCH)
PROLOGUE_UNROLL = 8


def _expert_kernel(tok_ref, h_ref, gates_ref, wg_ref, wu_ref, wd_ref, y_ref, x_ref, sem, xb_ref, acc_ref):
    e = pl.program_id(0)
    f = pl.program_id(1)
    slot = lax.rem(e, 2)
    next_slot = 1 - slot
    next_e = jnp.minimum(e + 1, N_EXPERTS - 1)

    tile = TOKEN_TILE_ROWS

    def row_copy(expert, row, dst_slot):
        src = pl.multiple_of(tok_ref[expert * GATHER_ROWS + row] * tile, tile)
        dst = pl.multiple_of(row * tile, tile)
        return pltpu.make_async_copy(h_ref.at[pl.ds(src, tile), :], x_ref.at[dst_slot, pl.ds(dst, tile), :],
                                     sem.at[dst_slot])

    def wait_rows(dst_slot):
        pltpu.make_async_copy(h_ref.at[pl.ds(0, GATHER_ROWS * tile), :], x_ref.at[dst_slot],
                              sem.at[dst_slot]).wait()

    @pl.when(jnp.logical_and(e == 0, f == 0))
    def _():
        def issue(i, carry):
            for u in range(PROLOGUE_UNROLL):
                row_copy(0, i * PROLOGUE_UNROLL + u, 0).start(priority=u % 2)
            return carry
        lax.fori_loop(0, GATHER_ROWS // PROLOGUE_UNROLL, issue, 0)
        acc_ref[...] = jnp.zeros_like(acc_ref)

    @pl.when(f == 0)
    def _():
        wait_rows(slot)
        for b in range(BATCH):
            cols = [x_ref[slot, pl.ds(b * CAP * tile + j, CAP, stride=tile), :] for j in range(tile)]
            xb_ref[b] = jnp.concatenate(cols, axis=1).astype(BF16)

    wg = wg_ref[0, 0].astype(BF16)
    wu = wu_ref[0, 0].astype(BF16)
    wd = wd_ref[0, 0].astype(BF16)
    for c in range(BATCH // CHUNK_SEQS):
        first = (f * (BATCH // CHUNK_SEQS) + c) * ROWS_PER_CHUNK
        for j in range(ROWS_PER_CHUNK):
            row_copy(next_e, first + j, next_slot).start(priority=1)
        seqs = slice(c * CHUNK_SEQS, (c + 1) * CHUNK_SEQS)
        x = xb_ref[seqs].reshape(CHUNK_SEQS * CAP, D_MODEL)
        a = jnp.dot(x, wg, preferred_element_type=F32)
        u = jnp.dot(x, wu, preferred_element_type=F32)
        act = ((a * jax.nn.sigmoid(a)) * u).astype(BF16)
        carried = jnp.where(f == 0, 0.0, acc_ref[seqs])
        part = jnp.dot(act, wd, preferred_element_type=F32).reshape(CHUNK_SEQS, CAP, D_MODEL)
        acc_ref[seqs] = carried + part

    @pl.when(f == FF_STEPS - 1)
    def _():
        gates = gates_ref[0]
        for b in range(BATCH):
            y_ref[0, b] = (acc_ref[b] * gates[:, b:b + 1]).astype(BF16)

    @pl.when(jnp.logical_and(e == N_EXPERTS - 1, f == FF_STEPS - 1))
    def _():
        wait_rows(next_slot)


def _experts(tok, h, gates, w_gate, w_up, w_down, layer):
    grid_spec = pltpu.PrefetchScalarGridSpec(
        num_scalar_prefetch=1,
        grid=(N_EXPERTS, FF_STEPS),
        in_specs=[
            pl.BlockSpec(memory_space=pl.ANY),
            pl.BlockSpec((1, CAP, BATCH), lambda e, f, tok: (e, 0, 0)),
            pl.BlockSpec((1, 1, D_MODEL, FF_TILE), lambda e, f, tok: (layer, e, 0, f)),
            pl.BlockSpec((1, 1, D_MODEL, FF_TILE), lambda e, f, tok: (layer, e, 0, f)),
            pl.BlockSpec((1, 1, FF_TILE, D_MODEL), lambda e, f, tok: (layer, e, f, 0)),
        ],
        out_specs=pl.BlockSpec((1, BATCH, CAP, D_MODEL), lambda e, f, tok: (e, 0, 0, 0)),
        scratch_shapes=[pltpu.VMEM((2, GATHER_ROWS * TOKEN_TILE_ROWS, LANES), F32), pltpu.SemaphoreType.DMA((2,)),
                        pltpu.VMEM((BATCH, CAP, D_MODEL), BF16), pltpu.VMEM((BATCH, CAP, D_MODEL), F32)],
    )
    return pl.pallas_call(
        _expert_kernel,
        grid_spec=grid_spec,
        out_shape=jax.ShapeDtypeStruct((N_EXPERTS, BATCH, CAP, D_MODEL), BF16),
        compiler_params=_params(("arbitrary", "arbitrary"), 60),
        name="moe_experts",
    )(tok, h, gates, w_gate, w_up, w_down)


COMBINE_WINDOW = 128
SLOT_ALIGN = 16


def _combine_kernel(start_ref, short_ref, slot_ref, y_ref, x_ref, gate_ref, g_ref, o_ref, *, final):
    tile = pl.program_id(0) * (SEQ // COMBINE_ROWS) + pl.program_id(1)
    slots = slot_ref[0]

    def finish(moe):
        x = x_ref[0] + gate_ref[0] * moe
        if final:
            x = (x * lax.rsqrt(jnp.mean(x * x, axis=-1, keepdims=True) + RMS_EPS)) * g_ref[...]
        o_ref[0] = x

    @pl.when(short_ref[tile] == 1)
    def _():
        lane = lax.broadcasted_iota(jnp.int32, (COMBINE_ROWS, COMBINE_WINDOW), 1)
        hits, rows = [], []
        for e in range(N_EXPERTS):
            start = pl.multiple_of(start_ref[tile * N_EXPERTS + e], SLOT_ALIGN)
            hits.append(jnp.where(slots[:, e:e + 1] - start == lane, 1.0, 0.0).astype(BF16))
            rows.append(y_ref[e, 0, pl.ds(start, COMBINE_WINDOW), :])
        finish(jnp.dot(jnp.concatenate(hits, axis=1), jnp.concatenate(rows, axis=0), preferred_element_type=F32))

    @pl.when(short_ref[tile] != 1)
    def _():
        lane = lax.broadcasted_iota(jnp.int32, (COMBINE_ROWS, CAP), 1)
        onehot = jnp.concatenate(
            [jnp.where(slots[:, e:e + 1] == lane, 1.0, 0.0).astype(BF16) for e in range(N_EXPERTS)], axis=1)
        finish(jnp.dot(onehot, y_ref[...].reshape(N_EXPERTS * CAP, D_MODEL), preferred_element_type=F32))


def _combine_windows(first):
    last = jnp.concatenate([first[..., 1:], jnp.full(first.shape[:2] + (1,), CAP, jnp.int32)], axis=-1)
    start = jnp.minimum(first // SLOT_ALIGN * SLOT_ALIGN, CAP - COMBINE_WINDOW)
    short = jnp.all(last <= start + COMBINE_WINDOW, axis=1)
    return start.transpose(0, 2, 1).reshape(-1), short.astype(jnp.int32).reshape(-1)


def _combine(starts, short, slot_t, y, x, mod, final_g, layer, final):
    grid_spec = pltpu.PrefetchScalarGridSpec(
        num_scalar_prefetch=2,
        grid=(BATCH, SEQ // COMBINE_ROWS),
        in_specs=[
            pl.BlockSpec((1, COMBINE_ROWS, N_EXPERTS), lambda b, t, s, f: (b, t, 0)),
            pl.BlockSpec((N_EXPERTS, 1, CAP, D_MODEL), lambda b, t, s, f: (0, b, 0, 0)),
            pl.BlockSpec((1, COMBINE_ROWS, D_MODEL), lambda b, t, s, f: (b, t, 0)),
            pl.BlockSpec((1, 1, D_MODEL), lambda b, t, s, f: ((layer * 6 + 5) * BATCH + b, 0, 0)),
            pl.BlockSpec((1, D_MODEL), lambda b, t, s, f: (0, 0)),
        ],
        out_specs=pl.BlockSpec((1, COMBINE_ROWS, D_MODEL), lambda b, t, s, f: (b, t, 0)),
    )
    return pl.pallas_call(
        functools.partial(_combine_kernel, final=final),
        grid_spec=grid_spec,
        out_shape=jax.ShapeDtypeStruct((BATCH, SEQ, D_MODEL), F32),
        compiler_params=_params(("arbitrary", "arbitrary"), 48),
        name="moe_combine",
    )(starts, short, slot_t, y, x, mod, final_g.reshape(1, D_MODEL))


def _scaled_q_weight(w, q_columns):
    scale = jnp.concatenate([jnp.full((q_columns,), Q_SCALE, F32), jnp.ones((w.shape[1] - q_columns,), F32)])
    return (w * scale).astype(BF16)


def kernel(x, c, ada_w, ada_b, norm_g, na_w_qkv, na_w_o, na_rpb, sw_w_qkv, sw_w_o, sw_sinks, t5_bias,
           moe_w_router, moe_w_gate, moe_w_up, moe_w_down, final_g):
    mod = _ada(c, ada_w, ada_b)
    norm_g3 = norm_g.reshape(DEPTH * 2, 1, D_MODEL)
    x = x.reshape(BATCH * SEQ, D_MODEL)
    for layer in range(DEPTH):
        j = layer // N_MIXERS
        if layer % N_MIXERS == 0:
            qkv = _qkv(x, norm_g3, mod, _scaled_q_weight(na_w_qkv[j], NA_HEADS * HEAD_DIM), layer)
            o = _na_attention(qkv.reshape(BATCH, SEQ, -1), _na_rpb_rows(na_rpb[j]))
            w_o = na_w_o[j]
        else:
            qkv = _qkv(x, norm_g3, mod, _scaled_q_weight(sw_w_qkv[j], SW_Q_HEADS * HEAD_DIM), layer,
                       head_copies=2 * SW_KV_HEADS * HEAD_DIM // LANES)
            o = _sw_attention(qkv.reshape(BATCH, SEQ, -1), _sw_rel_table(t5_bias), sw_sinks[j])
            w_o = sw_w_o[j]
        w_router = moe_w_router[layer].astype(BF16)
        x, h, aff_t = _post_attn(o.reshape(BATCH * SEQ, -1), w_o.astype(BF16), x, norm_g3, mod,
                                 w_router.T, layer)
        slot, tok, gate, first = _route(aff_t)
        tok = tok.reshape(BATCH, N_EXPERTS, CAP)
        rows = tok.transpose(1, 0, 2) + (jnp.arange(BATCH, dtype=jnp.int32) * SEQ)[None, :, None]
        gates = gate.reshape(BATCH, N_EXPERTS, CAP).transpose(1, 2, 0)
        y = _experts(rows.reshape(-1), h, gates, moe_w_gate, moe_w_up, moe_w_down, layer)
        slot_t = slot.reshape(BATCH, N_EXPERTS, SEQ).transpose(0, 2, 1)
        starts, short = _combine_windows(first.reshape(BATCH, N_EXPERTS, LANES)[..., :SEQ // COMBINE_ROWS])
        x = _combine(starts, short, slot_t, y, x.reshape(BATCH, SEQ, D_MODEL), mod, final_g, layer,
                     final=layer == DEPTH - 1).reshape(BATCH * SEQ, D_MODEL)
    return x.reshape(BATCH, SEQ, D_MODEL)
```
